```python
import jax, jax.numpy as jnp
from jax import lax
import numpy as np

D_MODEL = 1024
BATCH = 8
SEQ = 16384
DEPTH = 1

PLE_DIM = 256
POOL_GROUPS = 4
POOL_WIDTH = D_MODEL // 2
POOL_GROUP_DIM = POOL_WIDTH // POOL_GROUPS
POOL_WINDOWS = (2, 4, 8, 16)
MAX_WINDOW = 16
LRU_WIDTH = D_MODEL
LRU_HEADS = 8
LRU_HEAD_DIM = LRU_WIDTH // LRU_HEADS
CONV_WIDTH = 4
LRU_C = 8.0
N_BRANCHES = 2
D_FF = ((8 * D_MODEL // 3 + 255) // 256) * 256
RMS_EPS = 1e-6
IN_COLS = POOL_WIDTH + 2 * LRU_WIDTH + N_BRANCHES * D_MODEL

kernel_name = "hybrid_pool_rglru_gated_block"


def rms_norm(x, g):
    xf = x.astype(jnp.float32)
    return xf * lax.rsqrt(jnp.mean(xf * xf, axis=-1, keepdims=True) + RMS_EPS) * g.astype(jnp.float32)


def causal_multiscale_pool(z, w_grp, scale):
    B, S, _ = z.shape
    zf = z.astype(jnp.float32)
    csum = jnp.cumsum(zf, axis=1)
    csum_pad = jnp.pad(csum, ((0, 0), (MAX_WINDOW, 0), (0, 0)))
    pos = jnp.arange(S)
    outs = []
    for g, w in enumerate(POOL_WINDOWS):
        lo, hi = g * POOL_GROUP_DIM, (g + 1) * POOL_GROUP_DIM
        cur = csum[:, :, lo:hi]
        prev = csum_pad[:, MAX_WINDOW - w:MAX_WINDOW - w + S, lo:hi]
        count = jnp.minimum(pos + 1, w).astype(jnp.float32)[None, :, None]
        outs.append((cur - prev) / count - zf[:, :, lo:hi])
    pooled = jnp.stack(outs, axis=2)
    mixed = jnp.einsum('bsgc,gcd->bsgd', pooled, w_grp.astype(jnp.float32))
    return mixed.reshape(B, S, POOL_WIDTH) * scale.astype(jnp.float32)


def _linear_recurrence_combine(earlier, later):
    a1, b1 = earlier
    a2, b2 = later
    return a1 * a2, a2 * b1 + b2


def rglru_branch(z_x, z_g, conv_w, conv_b, w_rg, b_rg, w_ig, b_ig, lam):
    B, S, _ = z_x.shape
    f32 = jnp.float32
    xc = lax.conv_general_dilated(
        z_x.astype(f32), conv_w.astype(f32)[:, None, :], window_strides=(1,),
        padding=[(CONV_WIDTH - 1, 0)], dimension_numbers=('NWC', 'WIO', 'NWC'),
        feature_group_count=LRU_WIDTH) + conv_b.astype(f32)
    xh = xc.reshape(B, S, LRU_HEADS, LRU_HEAD_DIM)
    r = jax.nn.sigmoid(jnp.einsum('bshi,hij->bshj', xh, w_rg.astype(f32)) + b_rg.astype(f32)).reshape(B, S, LRU_WIDTH)
    ig = jax.nn.sigmoid(jnp.einsum('bshi,hij->bshj', xh, w_ig.astype(f32)) + b_ig.astype(f32)).reshape(B, S, LRU_WIDTH)
    log_a = -LRU_C * r * jax.nn.softplus(-lam.astype(f32))
    a = jnp.exp(log_a)
    mult = jnp.sqrt(jnp.maximum(1.0 - jnp.exp(2.0 * log_a), 0.0))
    mult = jnp.where((jnp.arange(S) == 0)[None, :, None], 1.0, mult)
    b = mult * ig * xc
    _, h = lax.associative_scan(_linear_recurrence_combine, (a, b), axis=1)
    return h * jax.nn.gelu(z_g.astype(f32))


def _fwd_setup_inputs(seed: int = 0) -> dict:
    key = jax.random.key(seed)
    ks = jax.random.split(key, 32)
    f32 = jnp.float32
    nrm = lambda k, shape, fan_in: jax.random.normal(k, shape, f32) * (fan_in ** -0.5)
    gain = lambda k, shape: 1.0 + 0.02 * jax.random.normal(k, shape, f32)
    small = lambda k, shape: 0.02 * jax.random.normal(k, shape, f32)
    u = jax.random.uniform(ks[14], (DEPTH, LRU_WIDTH), f32, 0.9, 0.999)
    s = u ** (1.0 / LRU_C)
    lru_lambda = jnp.log(s) - jnp.log1p(-s)
    return {
        "x": jax.random.normal(ks[0], (BATCH, SEQ, D_MODEL), f32),
        "p": jax.random.normal(ks[1], (DEPTH, BATCH, SEQ, PLE_DIM), f32),
        "norm1_g": gain(ks[2], (DEPTH, D_MODEL)),
        "w_in": nrm(ks[3], (DEPTH, D_MODEL, IN_COLS), D_MODEL),
        "b_gate": small(ks[4], (DEPTH, N_BRANCHES, D_MODEL)),
        "pool_w": nrm(ks[5], (DEPTH, POOL_GROUPS, POOL_GROUP_DIM, POOL_GROUP_DIM), POOL_GROUP_DIM),
        "pool_scale": 1.0 + 0.1 * jax.random.normal(ks[6], (DEPTH, POOL_WIDTH), f32),
        "pool_proj": nrm(ks[7], (DEPTH, POOL_WIDTH, D_MODEL), POOL_WIDTH),
        "conv_w": nrm(ks[8], (DEPTH, CONV_WIDTH, LRU_WIDTH), CONV_WIDTH),
        "conv_b": small(ks[9], (DEPTH, LRU_WIDTH)),
        "w_rg": nrm(ks[10], (DEPTH, LRU_HEADS, LRU_HEAD_DIM, LRU_HEAD_DIM), LRU_HEAD_DIM),
        "b_rg": small(ks[11], (DEPTH, LRU_HEADS, LRU_HEAD_DIM)),
        "w_ig": nrm(ks[12], (DEPTH, LRU_HEADS, LRU_HEAD_DIM, LRU_HEAD_DIM), LRU_HEAD_DIM),
        "b_ig": small(ks[13], (DEPTH, LRU_HEADS, LRU_HEAD_DIM)),
        "lru_lambda": lru_lambda,
        "lru_proj": nrm(ks[15], (DEPTH, LRU_WIDTH, D_MODEL), LRU_WIDTH),
        "w_out": nrm(ks[16], (DEPTH, D_MODEL, D_MODEL), D_MODEL),
        "norm2_g": gain(ks[17], (DEPTH, D_MODEL)),
        "w_ffn_in": nrm(ks[18], (DEPTH, D_MODEL, 2 * D_FF), D_MODEL),
        "w_ffn_out": nrm(ks[19], (DEPTH, D_FF, D_MODEL), D_FF),
        "ple_norm_g": gain(ks[20], (DEPTH, D_MODEL)),
        "w_ple_gate": nrm(ks[21], (DEPTH, D_MODEL, D_MODEL), D_MODEL),
        "w_ple_proj": nrm(ks[22], (DEPTH, PLE_DIM, D_MODEL), PLE_DIM),
        "final_g": gain(ks[23], (D_MODEL,)),
    }


def _fwd_reference(x, p, norm1_g, w_in, b_gate, pool_w, pool_scale, pool_proj, conv_w, conv_b,
              w_rg, b_rg, w_ig, b_ig, lru_lambda, lru_proj, w_out, norm2_g, w_ffn_in,
              w_ffn_out, ple_norm_g, w_ple_gate, w_ple_proj, final_g):
    B, S, _ = x.shape
    f32 = jnp.float32
    h = x.astype(f32)
    for i in range(DEPTH):
        u = rms_norm(h, norm1_g[i])
        z = u @ w_in[i].astype(f32)
        z_pool, z_lru, z_gelu, z_gate = jnp.split(
            z, [POOL_WIDTH, POOL_WIDTH + LRU_WIDTH, POOL_WIDTH + 2 * LRU_WIDTH], axis=-1)
        y_pool = causal_multiscale_pool(z_pool, pool_w[i], pool_scale[i]) @ pool_proj[i].astype(f32)
        y_lru = rglru_branch(z_lru, z_gelu, conv_w[i], conv_b[i], w_rg[i], b_rg[i],
                             w_ig[i], b_ig[i], lru_lambda[i]) @ lru_proj[i].astype(f32)
        gates = jax.nn.sigmoid(z_gate.reshape(B, S, N_BRANCHES, D_MODEL) + b_gate[i].astype(f32))
        merged = gates[:, :, 0, :] * y_pool + gates[:, :, 1, :] * y_lru
        h = h + merged @ w_out[i].astype(f32)
        v = rms_norm(h, norm2_g[i])
        g_ff, u_ff = jnp.split(v @ w_ffn_in[i].astype(f32), 2, axis=-1)
        h = h + (jax.nn.silu(g_ff) * u_ff) @ w_ffn_out[i].astype(f32)
        e = p[i].astype(f32) @ w_ple_proj[i].astype(f32)
        ple_gate = jax.nn.sigmoid(rms_norm(h, ple_norm_g[i]) @ w_ple_gate[i].astype(f32))
        h = h + ple_gate * e
    return rms_norm(h, final_g).astype(x.dtype)


import jax as _jax
import jax.numpy as _jnp

TWIN_FORMAT = 'train_step'
FWD_PARAMS = ['x', 'p', 'norm1_g', 'w_in', 'b_gate', 'pool_w', 'pool_scale', 'pool_proj', 'conv_w', 'conv_b', 'w_rg', 'b_rg', 'w_ig', 'b_ig', 'lru_lambda', 'lru_proj', 'w_out', 'norm2_g', 'w_ffn_in', 'w_ffn_out', 'ple_norm_g', 'w_ple_gate', 'w_ple_proj', 'final_g']
TWIN_WEIGHTS = ['norm1_g', 'w_in', 'b_gate', 'pool_w', 'pool_scale', 'pool_proj', 'conv_w', 'conv_b', 'w_rg', 'b_rg', 'w_ig', 'b_ig', 'lru_lambda', 'lru_proj', 'w_out', 'norm2_g', 'w_ffn_in', 'w_ffn_out', 'ple_norm_g', 'w_ple_gate', 'w_ple_proj', 'final_g']
TWIN_DIFF_INPUT = 'x'
TWIN_INPUTS = ['x', 'p', 'norm1_g', 'w_in', 'b_gate', 'pool_w', 'pool_scale', 'pool_proj', 'conv_w', 'conv_b', 'w_rg', 'b_rg', 'w_ig', 'b_ig', 'lru_lambda', 'lru_proj', 'w_out', 'norm2_g', 'w_ffn_in', 'w_ffn_out', 'ple_norm_g', 'w_ple_gate', 'w_ple_proj', 'final_g', 'loss_target', 'm_norm1_g', 'm_w_in', 'm_b_gate', 'm_pool_w', 'm_pool_scale', 'm_pool_proj', 'm_conv_w', 'm_conv_b', 'm_w_rg', 'm_b_rg', 'm_w_ig', 'm_b_ig', 'm_lru_lambda', 'm_lru_proj', 'm_w_out', 'm_norm2_g', 'm_w_ffn_in', 'm_w_ffn_out', 'm_ple_norm_g', 'm_w_ple_gate', 'm_w_ple_proj', 'm_final_g', 'v_norm1_g', 'v_w_in', 'v_b_gate', 'v_pool_w', 'v_pool_scale', 'v_pool_proj', 'v_conv_w', 'v_conv_b', 'v_w_rg', 'v_b_rg', 'v_w_ig', 'v_b_ig', 'v_lru_lambda', 'v_lru_proj', 'v_w_out', 'v_norm2_g', 'v_w_ffn_in', 'v_w_ffn_out', 'v_ple_norm_g', 'v_w_ple_gate', 'v_w_ple_proj', 'v_final_g']
TWIN_OUTPUTS = ['loss', 'grad_x', 'grad_norm1_g', 'grad_w_in', 'grad_b_gate', 'grad_pool_w', 'grad_pool_scale', 'grad_pool_proj', 'grad_conv_w', 'grad_conv_b', 'grad_w_rg', 'grad_b_rg', 'grad_w_ig', 'grad_b_ig', 'grad_lru_lambda', 'grad_lru_proj', 'grad_w_out', 'grad_norm2_g', 'grad_w_ffn_in', 'grad_w_ffn_out', 'grad_ple_norm_g', 'grad_w_ple_gate', 'grad_w_ple_proj', 'grad_final_g', 'delta_norm1_g', 'delta_w_in', 'delta_b_gate', 'delta_pool_w', 'delta_pool_scale', 'delta_pool_proj', 'delta_conv_w', 'delta_conv_b', 'delta_w_rg', 'delta_b_rg', 'delta_w_ig', 'delta_b_ig', 'delta_lru_lambda', 'delta_lru_proj', 'delta_w_out', 'delta_norm2_g', 'delta_w_ffn_in', 'delta_w_ffn_out', 'delta_ple_norm_g', 'delta_w_ple_gate', 'delta_w_ple_proj', 'delta_final_g', 'new_m_norm1_g', 'new_m_w_in', 'new_m_b_gate', 'new_m_pool_w', 'new_m_pool_scale', 'new_m_pool_proj', 'new_m_conv_w', 'new_m_conv_b', 'new_m_w_rg', 'new_m_b_rg', 'new_m_w_ig', 'new_m_b_ig', 'new_m_lru_lambda', 'new_m_lru_proj', 'new_m_w_out', 'new_m_norm2_g', 'new_m_w_ffn_in', 'new_m_w_ffn_out', 'new_m_ple_norm_g', 'new_m_w_ple_gate', 'new_m_w_ple_proj', 'new_m_final_g', 'new_v_norm1_g', 'new_v_w_in', 'new_v_b_gate', 'new_v_pool_w', 'new_v_pool_scale', 'new_v_pool_proj', 'new_v_conv_w', 'new_v_conv_b', 'new_v_w_rg', 'new_v_b_rg', 'new_v_w_ig', 'new_v_b_ig', 'new_v_lru_lambda', 'new_v_lru_proj', 'new_v_w_out', 'new_v_norm2_g', 'new_v_w_ffn_in', 'new_v_w_ffn_out', 'new_v_ple_norm_g', 'new_v_w_ple_gate', 'new_v_w_ple_proj', 'new_v_final_g']
TWIN_LEAF_KINDS = {'loss': 'loss', 'grad_x': 'grad_x', 'grad_norm1_g': 'grad_w', 'grad_w_in': 'grad_w', 'grad_b_gate': 'grad_w', 'grad_pool_w': 'grad_w', 'grad_pool_scale': 'grad_w', 'grad_pool_proj': 'grad_w', 'grad_conv_w': 'grad_w', 'grad_conv_b': 'grad_w', 'grad_w_rg': 'grad_w', 'grad_b_rg': 'grad_w', 'grad_w_ig': 'grad_w', 'grad_b_ig': 'grad_w', 'grad_lru_lambda': 'grad_w', 'grad_lru_proj': 'grad_w', 'grad_w_out': 'grad_w', 'grad_norm2_g': 'grad_w', 'grad_w_ffn_in': 'grad_w', 'grad_w_ffn_out': 'grad_w', 'grad_ple_norm_g': 'grad_w', 'grad_w_ple_gate': 'grad_w', 'grad_w_ple_proj': 'grad_w', 'grad_final_g': 'grad_w', 'delta_norm1_g': 'delta_w', 'delta_w_in': 'delta_w', 'delta_b_gate': 'delta_w', 'delta_pool_w': 'delta_w', 'delta_pool_scale': 'delta_w', 'delta_pool_proj': 'delta_w', 'delta_conv_w': 'delta_w', 'delta_conv_b': 'delta_w', 'delta_w_rg': 'delta_w', 'delta_b_rg': 'delta_w', 'delta_w_ig': 'delta_w', 'delta_b_ig': 'delta_w', 'delta_lru_lambda': 'delta_w', 'delta_lru_proj': 'delta_w', 'delta_w_out': 'delta_w', 'delta_norm2_g': 'delta_w', 'delta_w_ffn_in': 'delta_w', 'delta_w_ffn_out': 'delta_w', 'delta_ple_norm_g': 'delta_w', 'delta_w_ple_gate': 'delta_w', 'delta_w_ple_proj': 'delta_w', 'delta_final_g': 'delta_w', 'new_m_norm1_g': 'new_m', 'new_m_w_in': 'new_m', 'new_m_b_gate': 'new_m', 'new_m_pool_w': 'new_m', 'new_m_pool_scale': 'new_m', 'new_m_pool_proj': 'new_m', 'new_m_conv_w': 'new_m', 'new_m_conv_b': 'new_m', 'new_m_w_rg': 'new_m', 'new_m_b_rg': 'new_m', 'new_m_w_ig': 'new_m', 'new_m_b_ig': 'new_m', 'new_m_lru_lambda': 'new_m', 'new_m_lru_proj': 'new_m', 'new_m_w_out': 'new_m', 'new_m_norm2_g': 'new_m', 'new_m_w_ffn_in': 'new_m', 'new_m_w_ffn_out': 'new_m', 'new_m_ple_norm_g': 'new_m', 'new_m_w_ple_gate': 'new_m', 'new_m_w_ple_proj': 'new_m', 'new_m_final_g': 'new_m', 'new_v_norm1_g': 'new_v', 'new_v_w_in': 'new_v', 'new_v_b_gate': 'new_v', 'new_v_pool_w': 'new_v', 'new_v_pool_scale': 'new_v', 'new_v_pool_proj': 'new_v', 'new_v_conv_w': 'new_v', 'new_v_conv_b': 'new_v', 'new_v_w_rg': 'new_v', 'new_v_b_rg': 'new_v', 'new_v_w_ig': 'new_v', 'new_v_b_ig': 'new_v', 'new_v_lru_lambda': 'new_v', 'new_v_lru_proj': 'new_v', 'new_v_w_out': 'new_v', 'new_v_norm2_g': 'new_v', 'new_v_w_ffn_in': 'new_v', 'new_v_w_ffn_out': 'new_v', 'new_v_ple_norm_g': 'new_v', 'new_v_w_ple_gate': 'new_v', 'new_v_w_ple_proj': 'new_v', 'new_v_final_g': 'new_v'}


def _forward(args):
    return _fwd_reference(*[args[k] for k in FWD_PARAMS])


def _output_shape():
    def fwd():
        inp = _fwd_setup_inputs(0)
        return _fwd_reference(*[inp[k] for k in FWD_PARAMS])
    out = _jax.eval_shape(fwd)
    return out.shape, out.dtype

N_MICROBATCH = 1
ADAM_LR = 0.001
ADAM_B1 = 0.9
ADAM_B2 = 0.999
ADAM_EPS = 1e-08
ADAM_WD = 0.01
ADAM_STEP = 10
PER_EXAMPLE_BATCH_AXIS = {'x': 0, 'p': 1, 'loss_target': 0}
SHARED_INPUTS = []
_WEIGHT_DTYPES = {'norm1_g': _jnp.float32, 'w_in': _jnp.float32, 'b_gate': _jnp.float32, 'pool_w': _jnp.float32, 'pool_scale': _jnp.float32, 'pool_proj': _jnp.float32, 'conv_w': _jnp.float32, 'conv_b': _jnp.float32, 'w_rg': _jnp.float32, 'b_rg': _jnp.float32, 'w_ig': _jnp.float32, 'b_ig': _jnp.float32, 'lru_lambda': _jnp.float32, 'lru_proj': _jnp.float32, 'w_out': _jnp.float32, 'norm2_g': _jnp.float32, 'w_ffn_in': _jnp.float32, 'w_ffn_out': _jnp.float32, 'ple_norm_g': _jnp.float32, 'w_ple_gate': _jnp.float32, 'w_ple_proj': _jnp.float32, 'final_g': _jnp.float32}
MOMENT_SCALE = {'norm1_g': 1.933302e-01, 'w_in': 9.558736e-02, 'b_gate': 4.624966e-02, 'pool_w': 2.200445e-01, 'pool_scale': 2.236977e-01, 'pool_proj': 1.565513e-01, 'conv_w': 8.885540e-02, 'conv_b': 1.178972e+00, 'w_rg': 2.879545e-02, 'b_rg': 2.412126e-02, 'w_ig': 5.120330e-02, 'b_ig': 2.620382e-02, 'lru_lambda': 4.678283e-02, 'lru_proj': 8.221489e-02, 'w_out': 1.760408e-01, 'norm2_g': 2.280639e-01, 'w_ffn_in': 9.496929e-02, 'w_ffn_out': 1.550463e-01, 'ple_norm_g': 6.294267e-02, 'w_ple_gate': 5.414539e-02, 'w_ple_proj': 1.387237e-01, 'final_g': 1.277976e+02}


def _to_microbatches(a, axis):
    t = _jnp.moveaxis(a, axis, 0)
    t = t.reshape((N_MICROBATCH, t.shape[0] // N_MICROBATCH) + t.shape[1:])
    return _jnp.moveaxis(t, 1, axis + 1)


def setup_inputs(seed: int = 0) -> dict:
    inp = _fwd_setup_inputs(seed)
    key = _jax.random.fold_in(_jax.random.key(seed), 7919)
    shape, _ = _output_shape()
    out = dict(inp)
    out["loss_target"] = _jax.random.normal(_jax.random.fold_in(key, 0), shape, _jnp.float32)
    for i, name in enumerate(TWIN_WEIGHTS):
        w = inp[name].astype(_jnp.float32)
        if MOMENT_SCALE is None:
            s = _jnp.sqrt(_jnp.mean(_jnp.square(w)) + 1e-30)
        else:
            s = MOMENT_SCALE[name]
        km, kv = _jax.random.split(_jax.random.fold_in(key, i + 1))
        out[name] = w
        out["m_" + name] = s * _jax.random.normal(km, w.shape, _jnp.float32)
        out["v_" + name] = (s * s) * _jax.random.uniform(kv, w.shape, _jnp.float32, 0.5, 1.5)
    if N_MICROBATCH > 1:
        for name, axis in PER_EXAMPLE_BATCH_AXIS.items():
            out[name] = _to_microbatches(out[name], axis)
    return {'x': out['x'], 'p': out['p'], 'norm1_g': out['norm1_g'], 'w_in': out['w_in'], 'b_gate': out['b_gate'], 'pool_w': out['pool_w'], 'pool_scale': out['pool_scale'], 'pool_proj': out['pool_proj'], 'conv_w': out['conv_w'], 'conv_b': out['conv_b'], 'w_rg': out['w_rg'], 'b_rg': out['b_rg'], 'w_ig': out['w_ig'], 'b_ig': out['b_ig'], 'lru_lambda': out['lru_lambda'], 'lru_proj': out['lru_proj'], 'w_out': out['w_out'], 'norm2_g': out['norm2_g'], 'w_ffn_in': out['w_ffn_in'], 'w_ffn_out': out['w_ffn_out'], 'ple_norm_g': out['ple_norm_g'], 'w_ple_gate': out['w_ple_gate'], 'w_ple_proj': out['w_ple_proj'], 'final_g': out['final_g'], 'loss_target': out['loss_target'], 'm_norm1_g': out['m_norm1_g'], 'm_w_in': out['m_w_in'], 'm_b_gate': out['m_b_gate'], 'm_pool_w': out['m_pool_w'], 'm_pool_scale': out['m_pool_scale'], 'm_pool_proj': out['m_pool_proj'], 'm_conv_w': out['m_conv_w'], 'm_conv_b': out['m_conv_b'], 'm_w_rg': out['m_w_rg'], 'm_b_rg': out['m_b_rg'], 'm_w_ig': out['m_w_ig'], 'm_b_ig': out['m_b_ig'], 'm_lru_lambda': out['m_lru_lambda'], 'm_lru_proj': out['m_lru_proj'], 'm_w_out': out['m_w_out'], 'm_norm2_g': out['m_norm2_g'], 'm_w_ffn_in': out['m_w_ffn_in'], 'm_w_ffn_out': out['m_w_ffn_out'], 'm_ple_norm_g': out['m_ple_norm_g'], 'm_w_ple_gate': out['m_w_ple_gate'], 'm_w_ple_proj': out['m_w_ple_proj'], 'm_final_g': out['m_final_g'], 'v_norm1_g': out['v_norm1_g'], 'v_w_in': out['v_w_in'], 'v_b_gate': out['v_b_gate'], 'v_pool_w': out['v_pool_w'], 'v_pool_scale': out['v_pool_scale'], 'v_pool_proj': out['v_pool_proj'], 'v_conv_w': out['v_conv_w'], 'v_conv_b': out['v_conv_b'], 'v_w_rg': out['v_w_rg'], 'v_b_rg': out['v_b_rg'], 'v_w_ig': out['v_w_ig'], 'v_b_ig': out['v_b_ig'], 'v_lru_lambda': out['v_lru_lambda'], 'v_lru_proj': out['v_lru_proj'], 'v_w_out': out['v_w_out'], 'v_norm2_g': out['v_norm2_g'], 'v_w_ffn_in': out['v_w_ffn_in'], 'v_w_ffn_out': out['v_w_ffn_out'], 'v_ple_norm_g': out['v_ple_norm_g'], 'v_w_ple_gate': out['v_w_ple_gate'], 'v_w_ple_proj': out['v_w_ple_proj'], 'v_final_g': out['v_final_g']}


def _loss(weights, diff, rest, loss_target):
    with _jax.named_scope("forward"):
        args = {**rest, TWIN_DIFF_INPUT: diff, **{k: w.astype(_WEIGHT_DTYPES[k]) for k, w in weights.items()}}
        y = _forward(args)
    with _jax.named_scope("loss_head"):
        err = _jnp.square(y.astype(_jnp.float32) - loss_target)
        return 0.5 * _jnp.sum(_jnp.mean(err, axis=-1)) if err.ndim else 0.5 * err


def _adamw(w, g, m, v):
    m = ADAM_B1 * m + (1.0 - ADAM_B1) * g
    v = ADAM_B2 * v + (1.0 - ADAM_B2) * _jnp.square(g)
    m_hat = m / (1.0 - ADAM_B1 ** ADAM_STEP)
    v_hat = v / (1.0 - ADAM_B2 ** ADAM_STEP)
    delta = -ADAM_LR * (m_hat / (_jnp.sqrt(v_hat) + ADAM_EPS) + ADAM_WD * w)
    return delta, m, v


def reference(x, p, norm1_g, w_in, b_gate, pool_w, pool_scale, pool_proj, conv_w, conv_b, w_rg, b_rg, w_ig, b_ig, lru_lambda, lru_proj, w_out, norm2_g, w_ffn_in, w_ffn_out, ple_norm_g, w_ple_gate, w_ple_proj, final_g, loss_target, m_norm1_g, m_w_in, m_b_gate, m_pool_w, m_pool_scale, m_pool_proj, m_conv_w, m_conv_b, m_w_rg, m_b_rg, m_w_ig, m_b_ig, m_lru_lambda, m_lru_proj, m_w_out, m_norm2_g, m_w_ffn_in, m_w_ffn_out, m_ple_norm_g, m_w_ple_gate, m_w_ple_proj, m_final_g, v_norm1_g, v_w_in, v_b_gate, v_pool_w, v_pool_scale, v_pool_proj, v_conv_w, v_conv_b, v_w_rg, v_b_rg, v_w_ig, v_b_ig, v_lru_lambda, v_lru_proj, v_w_out, v_norm2_g, v_w_ffn_in, v_w_ffn_out, v_ple_norm_g, v_w_ple_gate, v_w_ple_proj, v_final_g):
    given = dict(x=x, p=p, norm1_g=norm1_g, w_in=w_in, b_gate=b_gate, pool_w=pool_w, pool_scale=pool_scale, pool_proj=pool_proj, conv_w=conv_w, conv_b=conv_b, w_rg=w_rg, b_rg=b_rg, w_ig=w_ig, b_ig=b_ig, lru_lambda=lru_lambda, lru_proj=lru_proj, w_out=w_out, norm2_g=norm2_g, w_ffn_in=w_ffn_in, w_ffn_out=w_ffn_out, ple_norm_g=ple_norm_g, w_ple_gate=w_ple_gate, w_ple_proj=w_ple_proj, final_g=final_g, loss_target=loss_target, m_norm1_g=m_norm1_g, m_w_in=m_w_in, m_b_gate=m_b_gate, m_pool_w=m_pool_w, m_pool_scale=m_pool_scale, m_pool_proj=m_pool_proj, m_conv_w=m_conv_w, m_conv_b=m_conv_b, m_w_rg=m_w_rg, m_b_rg=m_b_rg, m_w_ig=m_w_ig, m_b_ig=m_b_ig, m_lru_lambda=m_lru_lambda, m_lru_proj=m_lru_proj, m_w_out=m_w_out, m_norm2_g=m_norm2_g, m_w_ffn_in=m_w_ffn_in, m_w_ffn_out=m_w_ffn_out, m_ple_norm_g=m_ple_norm_g, m_w_ple_gate=m_w_ple_gate, m_w_ple_proj=m_w_ple_proj, m_final_g=m_final_g, v_norm1_g=v_norm1_g, v_w_in=v_w_in, v_b_gate=v_b_gate, v_pool_w=v_pool_w, v_pool_scale=v_pool_scale, v_pool_proj=v_pool_proj, v_conv_w=v_conv_w, v_conv_b=v_conv_b, v_w_rg=v_w_rg, v_b_rg=v_b_rg, v_w_ig=v_w_ig, v_b_ig=v_b_ig, v_lru_lambda=v_lru_lambda, v_lru_proj=v_lru_proj, v_w_out=v_w_out, v_norm2_g=v_norm2_g, v_w_ffn_in=v_w_ffn_in, v_w_ffn_out=v_w_ffn_out, v_ple_norm_g=v_ple_norm_g, v_w_ple_gate=v_w_ple_gate, v_w_ple_proj=v_w_ple_proj, v_final_g=v_final_g)
    weights = {n: given[n] for n in TWIN_WEIGHTS}
    shared = {n: given[n] for n in SHARED_INPUTS}
    per_example = {n: given[n] for n in ['x', 'p']}
    grad_fn = _jax.value_and_grad(_loss, argnums=(0, 1))

    def one_microbatch(ex, loss_target):
        ex = dict(ex)
        diff = ex.pop(TWIN_DIFF_INPUT)
        return grad_fn(weights, diff, {**shared, **ex}, loss_target)

    if N_MICROBATCH == 1:
        loss, (grad_w, grad_x) = one_microbatch(per_example, given["loss_target"])
    else:
        def body(carry, xs):
            loss_sum, grad_sum = carry
            l_k, (gw_k, gx_k) = one_microbatch(xs[0], xs[1])
            with _jax.named_scope("update"):
                return (loss_sum + l_k, _jax.tree.map(_jnp.add, grad_sum, gw_k)), gx_k

        init = (_jnp.zeros((), _jnp.float32), _jax.tree.map(_jnp.zeros_like, weights))
        (loss, grad_w), grad_x = _jax.lax.scan(body, init, (per_example, given["loss_target"]))
    with _jax.named_scope("update"):
        delta_w, new_m, new_v = {}, {}, {}
        for n in TWIN_WEIGHTS:
            delta_w[n], new_m[n], new_v[n] = _adamw(weights[n], grad_w[n], given["m_" + n], given["v_" + n])
    return (loss, grad_x, *[grad_w[n] for n in TWIN_WEIGHTS], *[delta_w[n] for n in TWIN_WEIGHTS],
            *[new_m[n] for n in TWIN_WEIGHTS], *[new_v[n] for n in TWIN_WEIGHTS])
```

```python
import functools

import jax
import jax.numpy as jnp
from jax import lax
from jax.experimental import pallas as pl
from jax.experimental.pallas import tpu as pltpu

F32 = jnp.float32
BF16 = jnp.bfloat16

D_MODEL = 1024
POOL_WIDTH = 512
POOL_GROUP_DIM = 128
POOL_WINDOWS = (2, 4, 8, 16)
POOL_HALO = 16
LRU_HEADS = 8
LRU_HEAD_DIM = 128
CONV_WIDTH = 4
LRU_C = 8.0
D_FF = 2816
PLE_DIM = 256
RMS_EPS = 1e-6
N_SHARDS = 4
N_DEV = 8

ADAM_LR = 0.001
ADAM_B1 = 0.9
ADAM_B2 = 0.999
ADAM_EPS = 1e-08
ADAM_WD = 0.01
ADAM_STEP = 10

ROW_TILE = 256
SUBLANES = 8
VMEM_LIMIT = 56 * 1024 * 1024
MESH = pl.DeviceIdType.MESH
ANY = pl.BlockSpec(memory_space=pl.ANY)


def _params(semantics=None):
    return pltpu.CompilerParams(dimension_semantics=semantics, vmem_limit_bytes=VMEM_LIMIT)


def _resident(shape):
    n = len(shape)
    return pl.BlockSpec(shape, lambda *_: (0,) * n, pipeline_mode=pl.Buffered(1))


def _acc(shape):
    n = len(shape)
    return pl.BlockSpec(shape, lambda *_: (0,) * n)


def _rows(tile, cols):
    return pl.BlockSpec((tile, cols), lambda i: (i, 0))


def _rows_rev(tile, cols, n_tiles):
    return pl.BlockSpec((tile, cols), lambda i: (n_tiles - 1 - i, 0))


def _halo_before(rows, cols, tile):
    per = tile // rows
    return pl.BlockSpec((rows, cols), lambda i: (jnp.maximum(i * per - 1, 0), 0))


def _halo_before_rev(rows, cols, tile, n_tiles):
    per = tile // rows
    return pl.BlockSpec((rows, cols), lambda i: (jnp.maximum((n_tiles - 1 - i) * per - 1, 0), 0))


def _nn(a, b):
    return jnp.dot(a, b, preferred_element_type=F32)


def _nt(a, b):
    return lax.dot_general(a, b, (((1,), (1,)), ((), ())), preferred_element_type=F32)


def _tn(a, b):
    return lax.dot_general(a, b, (((0,), (0,)), ((), ())), preferred_element_type=F32)


def _rms(x):
    r = lax.rsqrt(jnp.mean(x * x, axis=-1, keepdims=True) + RMS_EPS)
    return x * r, r


def _rms_bwd(dn, n, r):
    return r * (dn - n * jnp.mean(dn * n, axis=-1, keepdims=True))


def _sigmoid(x):
    return 1.0 / (1.0 + jnp.exp(-x))


_GELU_C = 0.7978845608028654
_GELU_A = 0.044715


def _gelu(x):
    t = jnp.tanh(_GELU_C * (x + _GELU_A * x * x * x))
    return 0.5 * x * (1.0 + t)


def _gelu_and_grad(x):
    x2 = x * x
    t = jnp.tanh(_GELU_C * (x + _GELU_A * x2 * x))
    cdf = 0.5 * (1.0 + t)
    grad = cdf + 0.5 * x * (1.0 - t * t) * _GELU_C * (1.0 + 3.0 * _GELU_A * x2)
    return x * cdf, grad


def _softplus_neg(lam):
    e = jnp.exp(-jnp.abs(lam))
    sp = jnp.maximum(-lam, 0.0) + jnp.log1p(e)
    return sp, -_sigmoid(-lam)


def _colsum(v):
    return jnp.sum(v, axis=0, keepdims=True)


def _row_ids(shape):
    return lax.broadcasted_iota(jnp.int32, shape, 0)


def _shift_down(cat, k):
    return pltpu.roll(cat, k, 0) if k else cat


def _shift_up(cat, k):
    return pltpu.roll(cat, cat.shape[0] - k, 0) if k else cat


def _f1_in_proj(x, norm1_g, w_in):
    T = x.shape[0]
    tm = ROW_TILE
    splits = (0, POOL_WIDTH, POOL_WIDTH + D_MODEL, POOL_WIDTH + 2 * D_MODEL, POOL_WIDTH + 4 * D_MODEL)

    def body(x_ref, g_ref, w_ref, zp_ref, zl_ref, zg_ref, zt_ref, u_ref):
        n, _ = _rms(x_ref[...])
        u = (n * g_ref[...]).astype(BF16)
        u_ref[...] = u
        for k, o_ref in enumerate((zp_ref, zl_ref, zg_ref, zt_ref)):
            o_ref[...] = _nn(u, w_ref[:, splits[k]:splits[k + 1]])

    widths = [splits[k + 1] - splits[k] for k in range(4)]
    return pl.pallas_call(
        body, name="f1_in_proj", grid=(T // tm,),
        in_specs=[_rows(tm, D_MODEL), _resident((1, D_MODEL)), _resident(w_in.shape)],
        out_specs=[_rows(tm, w) for w in widths] + [_rows(tm, D_MODEL)],
        out_shape=[jax.ShapeDtypeStruct((T, w), F32) for w in widths] + [jax.ShapeDtypeStruct((T, D_MODEL), BF16)],
        compiler_params=_params(("arbitrary",)),
    )(x, norm1_g, w_in)


def _pool_forward(zp_cat, pw_ref, scale, first_row):
    tt = zp_cat.shape[0] - POOL_HALO
    t_glob = first_row + _row_ids((tt, POOL_GROUP_DIM))
    pooled, mixed = [], []
    for g, w in enumerate(POOL_WINDOWS):
        cat = zp_cat[:, g * POOL_GROUP_DIM:(g + 1) * POOL_GROUP_DIM]
        s, k = cat, 1
        while k < w:
            s = s + _shift_down(s, k)
            k *= 2
        cnt = jnp.minimum(t_glob + 1, w).astype(F32)
        pg = s[POOL_HALO:] / cnt - cat[POOL_HALO:]
        pooled.append(pg)
        mixed.append(_nn(pg.astype(BF16), pw_ref[g]))
    return jnp.concatenate(pooled, axis=1), jnp.concatenate(mixed, axis=1)


def _lru_gates(zl_cat, conv_w, conv_b, wrg_ref, brg, wig_ref, big, sp, first_row):
    tt = zl_cat.shape[0] - SUBLANES
    xc = conv_w[CONV_WIDTH - 1:CONV_WIDTH] * zl_cat
    for k in range(1, CONV_WIDTH):
        xc = xc + conv_w[CONV_WIDTH - 1 - k:CONV_WIDTH - k] * _shift_down(zl_cat, k)
    xc = xc[SUBLANES:] + conv_b
    xh = xc.astype(BF16)
    pr, pi = [], []
    for h in range(LRU_HEADS):
        xs = xh[:, h * LRU_HEAD_DIM:(h + 1) * LRU_HEAD_DIM]
        pr.append(_nn(xs, wrg_ref[h]))
        pi.append(_nn(xs, wig_ref[h]))
    r = _sigmoid(jnp.concatenate(pr, axis=1) + brg)
    ig = _sigmoid(jnp.concatenate(pi, axis=1) + big)
    a = jnp.exp(-LRU_C * r * sp)
    mult = jnp.sqrt(jnp.maximum(1.0 - a * a, 0.0))
    t_glob = first_row + _row_ids((tt, D_MODEL))
    mult = jnp.where(t_glob == 0, 1.0, mult)
    return xc, r, ig, a, mult


def _f2_mixer(x, zp, zl, zg, zt, b_gate, pool_w, pool_scale, pool_proj, conv_w, conv_b, w_rg, b_rg, w_ig, b_ig,
              lru_lambda, lru_proj, w_out):
    T = x.shape[0]
    tt = ROW_TILE
    n_groups = tt // SUBLANES

    def body(x_ref, zp_ref, zph_ref, zl_ref, zlh_ref, zg_ref, zt_ref, bg_ref, pw_ref, ps_ref, pp_ref, cw_ref, cb_ref,
             wrg_ref, brg_ref, wig_ref, big_ref, lam_ref, lp_ref, wo_ref,
             h1_ref, hs_ref, yp_ref, yl_ref, a_s, b_s, carry_s):
        i = pl.program_id(0)
        first_row = i * tt
        keep = (i > 0).astype(F32)

        zp_cat = jnp.concatenate([zph_ref[...] * keep, zp_ref[...]], axis=0)
        _, mixed = _pool_forward(zp_cat, pw_ref, ps_ref[...], first_row)
        y_pool = _nn((mixed * ps_ref[...]).astype(BF16), pp_ref[...])

        sp, _ = _softplus_neg(lam_ref[...])
        zl_cat = jnp.concatenate([zlh_ref[...] * keep, zl_ref[...]], axis=0)
        xc, _, ig, a, mult = _lru_gates(zl_cat, cw_ref[...], cb_ref[...], wrg_ref, brg_ref[...], wig_ref,
                                        big_ref[...], sp, first_row)
        a_s[...] = a
        b_s[...] = mult * ig * xc

        @pl.when(i == 0)
        def _():
            carry_s[...] = jnp.zeros_like(carry_s)

        rows8 = _row_ids((SUBLANES, D_MODEL))

        def group(g, carry):
            at = pl.ds(pl.multiple_of(g * SUBLANES, SUBLANES), SUBLANES)
            A, B = a_s[at, :], b_s[at, :]
            for s in (1, 2, 4):
                m = rows8 >= s
                B = jnp.where(m, A * pltpu.roll(B, s, 0) + B, B)
                A = jnp.where(m, A * pltpu.roll(A, s, 0), A)
            h = A * carry + B
            hs_ref[at, :] = h
            return jnp.broadcast_to(h[SUBLANES - 1:SUBLANES, :], (SUBLANES, D_MODEL))

        carry_s[...] = lax.fori_loop(0, n_groups, group, carry_s[...])
        y_lru = _nn((hs_ref[...] * _gelu(zg_ref[...])).astype(BF16), lp_ref[...])

        gates = _sigmoid(zt_ref[...] + bg_ref[...])
        merged = gates[:, :D_MODEL] * y_pool + gates[:, D_MODEL:] * y_lru
        h1_ref[...] = x_ref[...] + _nn(merged.astype(BF16), wo_ref[...])
        yp_ref[...] = y_pool.astype(BF16)
        yl_ref[...] = y_lru.astype(BF16)

    res = [pool_w, pool_scale, pool_proj, conv_w, conv_b, w_rg, b_rg, w_ig, b_ig, lru_lambda, lru_proj, w_out]
    return pl.pallas_call(
        body, name="f2_mixer", grid=(T // tt,),
        in_specs=[_rows(tt, D_MODEL), _rows(tt, POOL_WIDTH), _halo_before(POOL_HALO, POOL_WIDTH, tt),
                  _rows(tt, D_MODEL), _halo_before(SUBLANES, D_MODEL, tt), _rows(tt, D_MODEL), _rows(tt, 2 * D_MODEL),
                  _resident(b_gate.shape)] + [_resident(w.shape) for w in res],
        out_specs=[_rows(tt, D_MODEL)] * 4,
        out_shape=[jax.ShapeDtypeStruct((T, D_MODEL), F32), jax.ShapeDtypeStruct((T, D_MODEL), F32),
                   jax.ShapeDtypeStruct((T, D_MODEL), BF16), jax.ShapeDtypeStruct((T, D_MODEL), BF16)],
        scratch_shapes=[pltpu.VMEM((tt, D_MODEL), F32), pltpu.VMEM((tt, D_MODEL), F32),
                        pltpu.VMEM((SUBLANES, D_MODEL), F32)],
        compiler_params=_params(("arbitrary",)),
    )(x, zp, zp, zl, zl, zg, zt, b_gate, *res)


def _f3_ffn(h1, norm2_g, w_ffn_in, w_ffn_out):
    T = h1.shape[0]
    tm = ROW_TILE

    def body(h_ref, g_ref, wi_ref, wo_ref, h2_ref, v_ref, ff_ref, act_ref):
        h = h_ref[...]
        n, _ = _rms(h)
        v = (n * g_ref[...]).astype(BF16)
        v_ref[...] = v
        g_ff = _nn(v, wi_ref[:, :D_FF])
        u_ff = _nn(v, wi_ref[:, D_FF:])
        ff_ref[:, :D_FF] = g_ff.astype(BF16)
        ff_ref[:, D_FF:] = u_ff.astype(BF16)
        act = (g_ff * _sigmoid(g_ff) * u_ff).astype(BF16)
        act_ref[...] = act
        h2_ref[...] = h + _nn(act, wo_ref[...])

    return pl.pallas_call(
        body, name="f3_ffn", grid=(T // tm,),
        in_specs=[_rows(tm, D_MODEL), _resident((1, D_MODEL)), _resident(w_ffn_in.shape), _resident(w_ffn_out.shape)],
        out_specs=[_rows(tm, D_MODEL), _rows(tm, D_MODEL), _rows(tm, 2 * D_FF), _rows(tm, D_FF)],
        out_shape=[jax.ShapeDtypeStruct((T, D_MODEL), F32), jax.ShapeDtypeStruct((T, D_MODEL), BF16),
                   jax.ShapeDtypeStruct((T, 2 * D_FF), BF16), jax.ShapeDtypeStruct((T, D_FF), BF16)],
        compiler_params=_params(("arbitrary",)),
    )(h1, norm2_g, w_ffn_in, w_ffn_out)


def _b4_ple_loss(h2, p, target, ple_norm_g, w_ple_gate, w_ple_proj, final_g):
    T = h2.shape[0]
    tm = ROW_TILE

    def body(h_ref, p_ref, t_ref, gp_ref, wg_ref, wp_ref, gf_ref, loss_ref, dh2_ref, dwg_ref, dwp_ref, vec_ref):
        @pl.when(pl.program_id(0) == 0)
        def _():
            loss_ref[...] = jnp.zeros_like(loss_ref)
            dwg_ref[...] = jnp.zeros_like(dwg_ref)
            dwp_ref[...] = jnp.zeros_like(dwp_ref)
            vec_ref[...] = jnp.zeros_like(vec_ref)

        h2v = h_ref[...]
        n3, r3 = _rms(h2v)
        n3g = (n3 * gp_ref[...]).astype(BF16)
        pg = _sigmoid(_nn(n3g, wg_ref[...]))
        pb = p_ref[...].astype(BF16)
        e = _nn(pb, wp_ref[...])
        h3 = h2v + pg * e
        n4, r4 = _rms(h3)
        diff = n4 * gf_ref[...] - t_ref[...]
        loss_ref[...] += jnp.sum(diff * diff).reshape(1, 1)
        dy = diff * (1.0 / D_MODEL)
        vec_ref[0:1, :] += _colsum(dy * n4)
        dh3 = _rms_bwd(dy * gf_ref[...], n4, r4)
        dwp_ref[...] += _tn(pb, (dh3 * pg).astype(BF16))
        dq = (dh3 * e * pg * (1.0 - pg)).astype(BF16)
        dwg_ref[...] += _tn(n3g, dq)
        dn3g = _nt(dq, wg_ref[...])
        vec_ref[1:2, :] += _colsum(dn3g * n3)
        dh2_ref[...] = dh3 + _rms_bwd(dn3g * gp_ref[...], n3, r3)

    return pl.pallas_call(
        body, name="b4_ple_loss", grid=(T // tm,),
        in_specs=[_rows(tm, D_MODEL), _rows(tm, PLE_DIM), _rows(tm, D_MODEL), _resident((1, D_MODEL)),
                  _resident(w_ple_gate.shape), _resident(w_ple_proj.shape), _resident((1, D_MODEL))],
        out_specs=[_acc((1, 1)), _rows(tm, D_MODEL), _acc(w_ple_gate.shape), _acc(w_ple_proj.shape),
                   _acc((SUBLANES, D_MODEL))],
        out_shape=[jax.ShapeDtypeStruct((1, 1), F32), jax.ShapeDtypeStruct((T, D_MODEL), F32),
                   jax.ShapeDtypeStruct(w_ple_gate.shape, F32), jax.ShapeDtypeStruct(w_ple_proj.shape, F32),
                   jax.ShapeDtypeStruct((SUBLANES, D_MODEL), F32)],
        compiler_params=_params(("arbitrary",)),
    )(h2, p, target, ple_norm_g, w_ple_gate, w_ple_proj, final_g)


def _b3_ffn(dh2, h1, ff, norm2_g, w_ffn_in, w_ffn_out):
    T = h1.shape[0]
    tm = ROW_TILE

    def body(d_ref, h_ref, ff_ref, g_ref, wi_ref, wo_ref, dff_ref, dh1_ref, vec_ref):
        @pl.when(pl.program_id(0) == 0)
        def _():
            vec_ref[...] = jnp.zeros_like(vec_ref)

        dh2v = d_ref[...]
        dact = _nt(dh2v.astype(BF16), wo_ref[...])
        g_ff = ff_ref[:, :D_FF].astype(F32)
        u_ff = ff_ref[:, D_FF:].astype(F32)
        s = _sigmoid(g_ff)
        dg = (dact * u_ff * (s * (1.0 + g_ff * (1.0 - s)))).astype(BF16)
        du = (dact * (g_ff * s)).astype(BF16)
        dff_ref[:, :D_FF] = dg
        dff_ref[:, D_FF:] = du
        dv = _nt(dg, wi_ref[:, :D_FF]) + _nt(du, wi_ref[:, D_FF:])
        n2, r2 = _rms(h_ref[...])
        vec_ref[0:1, :] += _colsum(dv * n2)
        dh1_ref[...] = dh2v + _rms_bwd(dv * g_ref[...], n2, r2)

    return pl.pallas_call(
        body, name="b3_ffn", grid=(T // tm,),
        in_specs=[_rows(tm, D_MODEL), _rows(tm, D_MODEL), _rows(tm, 2 * D_FF), _resident((1, D_MODEL)),
                  _resident(w_ffn_in.shape), _resident(w_ffn_out.shape)],
        out_specs=[_rows(tm, 2 * D_FF), _rows(tm, D_MODEL), _acc((SUBLANES, D_MODEL))],
        out_shape=[jax.ShapeDtypeStruct((T, 2 * D_FF), BF16), jax.ShapeDtypeStruct((T, D_MODEL), F32),
                   jax.ShapeDtypeStruct((SUBLANES, D_MODEL), F32)],
        compiler_params=_params(("arbitrary",)),
    )(dh2, h1, ff, norm2_g, w_ffn_in, w_ffn_out)


def _wgrad(a, b, col_tile, name):
    T, K = a.shape
    N = b.shape[1]
    tk = 2 * ROW_TILE

    def body(a_ref, b_ref, o_ref):
        @pl.when(pl.program_id(1) == 0)
        def _():
            o_ref[...] = jnp.zeros_like(o_ref)

        o_ref[...] += _tn(a_ref[...].astype(BF16), b_ref[...].astype(BF16))

    return pl.pallas_call(
        body, name=name, grid=(N // col_tile, T // tk),
        in_specs=[pl.BlockSpec((tk, K), lambda j, k: (k, 0)), pl.BlockSpec((tk, col_tile), lambda j, k: (k, j))],
        out_specs=pl.BlockSpec((K, col_tile), lambda j, k: (0, j)),
        out_shape=jax.ShapeDtypeStruct((K, N), F32),
        compiler_params=_params(("arbitrary", "arbitrary")),
    )(a, b)


def _b2_gates(dh1, zt, yp, yl, b_gate, w_out):
    T = dh1.shape[0]
    tm = ROW_TILE

    def body(d_ref, zt_ref, yp_ref, yl_ref, bg_ref, wo_ref, dzt_ref, dyp_ref, dyl_ref, dwo_ref, vec_ref):
        @pl.when(pl.program_id(0) == 0)
        def _():
            dwo_ref[...] = jnp.zeros_like(dwo_ref)
            vec_ref[...] = jnp.zeros_like(vec_ref)

        db = d_ref[...].astype(BF16)
        dm = _nt(db, wo_ref[...])
        gates = _sigmoid(zt_ref[...] + bg_ref[...])
        g0, g1 = gates[:, :D_MODEL], gates[:, D_MODEL:]
        y_pool, y_lru = yp_ref[...].astype(F32), yl_ref[...].astype(F32)
        dwo_ref[...] += _tn((g0 * y_pool + g1 * y_lru).astype(BF16), db)
        dz0 = dm * y_pool * g0 * (1.0 - g0)
        dz1 = dm * y_lru * g1 * (1.0 - g1)
        vec_ref[0:1, :] += _colsum(dz0)
        vec_ref[1:2, :] += _colsum(dz1)
        dzt_ref[:, :D_MODEL] = dz0.astype(BF16)
        dzt_ref[:, D_MODEL:] = dz1.astype(BF16)
        dyp_ref[...] = (dm * g0).astype(BF16)
        dyl_ref[...] = (dm * g1).astype(BF16)

    return pl.pallas_call(
        body, name="b2_gates", grid=(T // tm,),
        in_specs=[_rows(tm, D_MODEL), _rows(tm, 2 * D_MODEL), _rows(tm, D_MODEL), _rows(tm, D_MODEL),
                  _resident(b_gate.shape), _resident(w_out.shape)],
        out_specs=[_rows(tm, 2 * D_MODEL), _rows(tm, D_MODEL), _rows(tm, D_MODEL), _acc(w_out.shape),
                   _acc((SUBLANES, D_MODEL))],
        out_shape=[jax.ShapeDtypeStruct((T, 2 * D_MODEL), BF16), jax.ShapeDtypeStruct((T, D_MODEL), BF16),
                   jax.ShapeDtypeStruct((T, D_MODEL), BF16), jax.ShapeDtypeStruct(w_out.shape, F32),
                   jax.ShapeDtypeStruct((SUBLANES, D_MODEL), F32)],
        compiler_params=_params(("arbitrary",)),
    )(dh1, zt, yp, yl, b_gate, w_out)


def _b2_pool(dyp, zp, pool_w, pool_scale, pool_proj):
    T = zp.shape[0]
    tt = ROW_TILE
    nt = T // tt

    def body(dy_ref, zp_ref, zph_ref, pw_ref, ps_ref, pp_ref, dzp_ref, dpp_ref, dpw_ref, vec_ref, q_next):
        i = pl.program_id(0)
        ti = nt - 1 - i
        first_row = ti * tt

        @pl.when(i == 0)
        def _():
            dpp_ref[...] = jnp.zeros_like(dpp_ref)
            dpw_ref[...] = jnp.zeros_like(dpw_ref)
            vec_ref[...] = jnp.zeros_like(vec_ref)
            q_next[...] = jnp.zeros_like(q_next)

        keep = (ti > 0).astype(F32)
        zp_cat = jnp.concatenate([zph_ref[...] * keep, zp_ref[...]], axis=0)
        pooled, mixed = _pool_forward(zp_cat, pw_ref, ps_ref[...], first_row)
        dy = dy_ref[...]
        dpp_ref[...] += _tn((mixed * ps_ref[...]).astype(BF16), dy)
        dms = _nt(dy, pp_ref[...])
        vec_ref[0:1, :POOL_WIDTH] += _colsum(dms * mixed)
        dmixed = (dms * ps_ref[...]).astype(BF16)
        t_glob = first_row + _row_ids((tt, POOL_GROUP_DIM))
        dz, q_all = [], []
        for g, w in enumerate(POOL_WINDOWS):
            cols = slice(g * POOL_GROUP_DIM, (g + 1) * POOL_GROUP_DIM)
            dpw_ref[g] += _tn(pooled[:, cols].astype(BF16), dmixed[:, cols])
            dpooled = _nt(dmixed[:, cols], pw_ref[g])
            q = dpooled / jnp.minimum(t_glob + 1, w).astype(F32)
            q_all.append(q)
            s, k = jnp.concatenate([q, q_next[:, cols]], axis=0), 1
            while k < w:
                s = s + _shift_up(s, k)
                k *= 2
            dz.append(s[:tt] - dpooled)
        dzp_ref[...] = jnp.concatenate(dz, axis=1).astype(BF16)
        q_next[...] = jnp.concatenate([q[:POOL_HALO] for q in q_all], axis=1)

    return pl.pallas_call(
        body, name="b2_pool", grid=(nt,),
        in_specs=[_rows_rev(tt, D_MODEL, nt), _rows_rev(tt, POOL_WIDTH, nt),
                  _halo_before_rev(POOL_HALO, POOL_WIDTH, tt, nt),
                  _resident(pool_w.shape), _resident(pool_scale.shape), _resident(pool_proj.shape)],
        out_specs=[_rows_rev(tt, POOL_WIDTH, nt), _acc(pool_proj.shape), _acc(pool_w.shape), _acc((SUBLANES, D_MODEL))],
        out_shape=[jax.ShapeDtypeStruct((T, POOL_WIDTH), BF16), jax.ShapeDtypeStruct(pool_proj.shape, F32),
                   jax.ShapeDtypeStruct(pool_w.shape, F32), jax.ShapeDtypeStruct((SUBLANES, D_MODEL), F32)],
        scratch_shapes=[pltpu.VMEM((POOL_HALO, POOL_WIDTH), F32)],
        compiler_params=_params(("arbitrary",)),
    )(dyp, zp, zp, pool_w, pool_scale, pool_proj)


_V_CONVW, _V_CONVB, _V_BRG, _V_BIG, _V_LAM = 0, 4, 5, 6, 7


def _b2_lru(dyl, zl, zg, hs, conv_w, conv_b, w_rg, b_rg, w_ig, b_ig, lru_lambda, lru_proj):
    T = zl.shape[0]
    tt = ROW_TILE
    nt = T // tt
    n_groups = tt // SUBLANES

    def body(dy_ref, zl_ref, zlh_ref, zg_ref, hs_ref, hsh_ref, cw_ref, cb_ref, wrg_ref, brg_ref, wig_ref, big_ref,
             lam_ref, lp_ref, dzl_ref, dzg_ref, dlp_ref, dwrg_ref, dwig_ref, vec_ref,
             c_s, d_s, g_s, g_next, a_next, dxc_next):
        i = pl.program_id(0)
        ti = nt - 1 - i
        first_row = ti * tt

        @pl.when(i == 0)
        def _():
            dlp_ref[...] = jnp.zeros_like(dlp_ref)
            dwrg_ref[...] = jnp.zeros_like(dwrg_ref)
            dwig_ref[...] = jnp.zeros_like(dwig_ref)
            vec_ref[...] = jnp.zeros_like(vec_ref)
            g_next[...] = jnp.zeros_like(g_next)
            a_next[...] = jnp.zeros_like(a_next)
            dxc_next[...] = jnp.zeros_like(dxc_next)

        keep = (ti > 0).astype(F32)
        sp, dsp_dlam = _softplus_neg(lam_ref[...])
        cw = cw_ref[...]
        zl_cat = jnp.concatenate([zlh_ref[...] * keep, zl_ref[...]], axis=0)
        xc, r, ig, a, mult = _lru_gates(zl_cat, cw, cb_ref[...], wrg_ref, brg_ref[...], wig_ref, big_ref[...], sp,
                                        first_row)
        hs = hs_ref[...]
        gelu, dgelu = _gelu_and_grad(zg_ref[...])
        dy = dy_ref[...]
        dlp_ref[...] += _tn((hs * gelu).astype(BF16), dy)
        dyl = _nt(dy, lp_ref[...])
        dzg_ref[...] = (dyl * hs * dgelu).astype(BF16)

        d_s[...] = dyl * gelu
        c_s[...] = _shift_up(jnp.concatenate([a, a_next[...]], axis=0), 1)[:tt]
        rows8 = _row_ids((SUBLANES, D_MODEL))

        def group(k, carry):
            at = pl.ds(pl.multiple_of((n_groups - 1 - k) * SUBLANES, SUBLANES), SUBLANES)
            C, Dv = c_s[at, :], d_s[at, :]
            for s in (1, 2, 4):
                m = rows8 < SUBLANES - s
                Dv = jnp.where(m, C * pltpu.roll(Dv, SUBLANES - s, 0) + Dv, Dv)
                C = jnp.where(m, C * pltpu.roll(C, SUBLANES - s, 0), C)
            G = C * carry + Dv
            g_s[at, :] = G
            return jnp.broadcast_to(G[0:1, :], (SUBLANES, D_MODEL))

        g_next[...] = lax.fori_loop(0, n_groups, group, g_next[...])
        a_next[...] = jnp.broadcast_to(a[0:1, :], (SUBLANES, D_MODEL))
        G = g_s[...]

        h_prev = _shift_down(jnp.concatenate([hsh_ref[...] * keep, hs], axis=0), 1)[SUBLANES:]
        t_glob = first_row + _row_ids((tt, D_MODEL))
        dmult = jnp.where(t_glob == 0, 0.0, G * ig * xc)
        dla = G * h_prev * a - dmult * (a * a) / mult
        vec_ref[_V_LAM:_V_LAM + 1, :] += _colsum(dla * r) * (-LRU_C) * dsp_dlam
        dpr = dla * (-LRU_C) * sp * r * (1.0 - r)
        dpi = G * mult * xc * ig * (1.0 - ig)
        vec_ref[_V_BRG:_V_BRG + 1, :] += _colsum(dpr)
        vec_ref[_V_BIG:_V_BIG + 1, :] += _colsum(dpi)
        dprb, dpib, xh = dpr.astype(BF16), dpi.astype(BF16), xc.astype(BF16)
        dxc_h = []
        for h in range(LRU_HEADS):
            cols = slice(h * LRU_HEAD_DIM, (h + 1) * LRU_HEAD_DIM)
            dwrg_ref[h] += _tn(xh[:, cols], dprb[:, cols])
            dwig_ref[h] += _tn(xh[:, cols], dpib[:, cols])
            dxc_h.append(_nt(dprb[:, cols], wrg_ref[h]) + _nt(dpib[:, cols], wig_ref[h]))
        dxc = G * mult * ig + jnp.concatenate(dxc_h, axis=1)

        vec_ref[_V_CONVB:_V_CONVB + 1, :] += _colsum(dxc)
        dxc_cat = jnp.concatenate([dxc, dxc_next[...]], axis=0)
        dzl = cw[CONV_WIDTH - 1:CONV_WIDTH] * dxc
        for k in range(CONV_WIDTH):
            lag = CONV_WIDTH - 1 - k
            vec_ref[_V_CONVW + k:_V_CONVW + k + 1, :] += _colsum(dxc * _shift_down(zl_cat, lag)[SUBLANES:])
            if lag:
                dzl = dzl + cw[k:k + 1] * _shift_up(dxc_cat, lag)[:tt]
        dzl_ref[...] = dzl.astype(BF16)
        dxc_next[...] = dxc[:SUBLANES]

    res = [conv_w, conv_b, w_rg, b_rg, w_ig, b_ig, lru_lambda, lru_proj]
    return pl.pallas_call(
        body, name="b2_lru", grid=(nt,),
        in_specs=[_rows_rev(tt, D_MODEL, nt), _rows_rev(tt, D_MODEL, nt), _halo_before_rev(SUBLANES, D_MODEL, tt, nt),
                  _rows_rev(tt, D_MODEL, nt), _rows_rev(tt, D_MODEL, nt), _halo_before_rev(SUBLANES, D_MODEL, tt, nt)]
        + [_resident(w.shape) for w in res],
        out_specs=[_rows_rev(tt, D_MODEL, nt), _rows_rev(tt, D_MODEL, nt), _acc(lru_proj.shape), _acc(w_rg.shape),
                   _acc(w_ig.shape), _acc((SUBLANES, D_MODEL))],
        out_shape=[jax.ShapeDtypeStruct((T, D_MODEL), BF16), jax.ShapeDtypeStruct((T, D_MODEL), BF16),
                   jax.ShapeDtypeStruct(lru_proj.shape, F32), jax.ShapeDtypeStruct(w_rg.shape, F32),
                   jax.ShapeDtypeStruct(w_ig.shape, F32), jax.ShapeDtypeStruct((SUBLANES, D_MODEL), F32)],
        scratch_shapes=[pltpu.VMEM((tt, D_MODEL), F32)] * 3 + [pltpu.VMEM((SUBLANES, D_MODEL), F32)] * 3,
        compiler_params=_params(("arbitrary",)),
    )(dyl, zl, zl, zg, hs, hs, *res)


def _b1_in_proj(dzp, dzl, dzg, dzt, x, dh1, norm1_g, w_in):
    T = x.shape[0]
    tm = ROW_TILE
    splits = (0, POOL_WIDTH, POOL_WIDTH + D_MODEL, POOL_WIDTH + 2 * D_MODEL, POOL_WIDTH + 4 * D_MODEL)

    def body(a_ref, b_ref, c_ref, d_ref, x_ref, dh_ref, g_ref, w_ref, dx_ref, vec_ref):
        @pl.when(pl.program_id(0) == 0)
        def _():
            vec_ref[...] = jnp.zeros_like(vec_ref)

        du = None
        for k, dz_ref in enumerate((a_ref, b_ref, c_ref, d_ref)):
            part = _nt(dz_ref[...], w_ref[:, splits[k]:splits[k + 1]])
            du = part if du is None else du + part
        n1, r1 = _rms(x_ref[...])
        vec_ref[0:1, :] += _colsum(du * n1)
        dx_ref[...] = dh_ref[...] + _rms_bwd(du * g_ref[...], n1, r1)

    return pl.pallas_call(
        body, name="b1_in_proj", grid=(T // tm,),
        in_specs=[_rows(tm, POOL_WIDTH), _rows(tm, D_MODEL), _rows(tm, D_MODEL), _rows(tm, 2 * D_MODEL),
                  _rows(tm, D_MODEL), _rows(tm, D_MODEL), _resident((1, D_MODEL)), _resident(w_in.shape)],
        out_specs=[_rows(tm, D_MODEL), _acc((SUBLANES, D_MODEL))],
        out_shape=[jax.ShapeDtypeStruct((T, D_MODEL), F32), jax.ShapeDtypeStruct((SUBLANES, D_MODEL), F32)],
        compiler_params=_params(("arbitrary",)),
    )(dzp, dzl, dzg, dzt, x, dh1, norm1_g, w_in)


def _row_tile(rows):
    for t in (512, 256, 128, 64, 32, 16, 8):
        if rows % t == 0:
            return t
    return rows


def _sum_arrays(arrs, name):
    R, C = arrs[0].shape
    tr = _row_tile(R)

    def body(*refs):
        acc = refs[0][...]
        for r in refs[1:-1]:
            acc = acc + r[...]
        refs[-1][...] = acc

    return pl.pallas_call(
        body, name=name, grid=(R // tr,),
        in_specs=[_rows(tr, C)] * len(arrs), out_specs=_rows(tr, C),
        out_shape=jax.ShapeDtypeStruct((R, C), F32),
        compiler_params=_params(("arbitrary",)),
    )(*arrs)


def _sum_slots(q, name):
    S, R, C = q.shape
    tr = _row_tile(R)

    def body(q_ref, o_ref):
        acc = q_ref[0]
        for s in range(1, S):
            acc = acc + q_ref[s]
        o_ref[...] = acc

    return pl.pallas_call(
        body, name=name, grid=(R // tr,),
        in_specs=[pl.BlockSpec((S, tr, C), lambda i: (0, i, 0))], out_specs=_rows(tr, C),
        out_shape=jax.ShapeDtypeStruct((R, C), F32),
        compiler_params=_params(("arbitrary",)),
    )(q)


def _adamw(w, g, m, v, name):
    R, C = w.shape
    tr = _row_tile(R)
    c1 = 1.0 - ADAM_B1 ** ADAM_STEP
    c2 = 1.0 - ADAM_B2 ** ADAM_STEP

    def body(w_ref, g_ref, m_ref, v_ref, d_ref, nm_ref, nv_ref):
        gv = g_ref[...]
        nm = ADAM_B1 * m_ref[...] + (1.0 - ADAM_B1) * gv
        nv = ADAM_B2 * v_ref[...] + (1.0 - ADAM_B2) * (gv * gv)
        d_ref[...] = -ADAM_LR * ((nm / c1) / (jnp.sqrt(nv / c2) + ADAM_EPS) + ADAM_WD * w_ref[...])
        nm_ref[...] = nm
        nv_ref[...] = nv

    return pl.pallas_call(
        body, name=name, grid=(R // tr,),
        in_specs=[_rows(tr, C)] * 4, out_specs=[_rows(tr, C)] * 3,
        out_shape=[jax.ShapeDtypeStruct((R, C), F32)] * 3,
        compiler_params=_params(("arbitrary",)),
    )(w, g, m, v)


def _place():
    return lax.axis_index("x"), lax.axis_index("y"), lax.axis_index("c")


def _other_chips(x, y):
    return [(1 - x, y), (x, 1 - y), (1 - x, 1 - y)]


def _shard_block(ref, by_rows, R, C, j, half_rows=None):
    if half_rows is None:
        rows, r0 = R, 0
    else:
        rows = R // 2
        r0 = pl.multiple_of(half_rows * rows, 16)
    if by_rows:
        return ref.at[pl.ds(pl.multiple_of(j * R, 16) + r0, rows), :]
    return ref.at[pl.ds(r0, rows), pl.ds(pl.multiple_of(j * C, 128), C)]


def _all_gather_weights(shards, by_rows, small):
    n = len(shards)
    shapes = [s.shape for s in shards]

    def body(*refs):
        ins, small_in = refs[:n], refs[n]
        outs, small_out = refs[n + 1:2 * n + 1], refs[2 * n + 1]
        send_sems, recv_sems, local_sems = refs[2 * n + 2:]
        x, y, c = _place()
        me_j = 2 * x + y
        chips = _other_chips(x, y)
        sibling = (x, y, 1 - c)

        def block(i, j, half):
            R, C = shapes[i]
            return _shard_block(outs[i], by_rows[i], R, C, j, half)

        local = []
        for i in range(n):
            R, C = shapes[i]
            cp = pltpu.make_async_copy(ins[i], _shard_block(outs[i], by_rows[i], R, C, me_j), local_sems.at[i])
            cp.start()
            local.append(cp)
        cp = pltpu.make_async_copy(small_in, small_out.at[:, pl.ds(pl.multiple_of(me_j * 256, 128), 256)],
                                   local_sems.at[n])
        cp.start()
        local.append(cp)

        def ici(i, k, src_j):
            R, C = shapes[i]
            return pltpu.make_async_remote_copy(
                src_ref=ins[i].at[pl.ds(pl.multiple_of(c * (R // 2), 16), R // 2), :], dst_ref=block(i, src_j, c),
                send_sem=send_sems.at[6 * i + k], recv_sem=recv_sems.at[6 * i + k],
                device_id=(*chips[k], c), device_id_type=MESH)

        def relay(i, k, half):
            kj = 2 * chips[k][0] + chips[k][1]
            return pltpu.make_async_remote_copy(
                src_ref=block(i, kj, half), dst_ref=block(i, kj, half),
                send_sem=send_sems.at[6 * i + 3 + k], recv_sem=recv_sems.at[6 * i + 3 + k],
                device_id=sibling, device_id_type=MESH)

        def small_copy(k, src_j):
            cols = pl.ds(pl.multiple_of(src_j * 256, 128), 256)
            return pltpu.make_async_remote_copy(
                src_ref=small_in, dst_ref=small_out.at[:, cols],
                send_sem=send_sems.at[6 * n + k], recv_sem=recv_sems.at[6 * n + k],
                device_id=(*chips[k], c), device_id_type=MESH)

        sends = []
        for i in range(n):
            for k in range(3):
                cp = ici(i, k, me_j)
                cp.start()
                sends.append(cp)
        for k in range(3):
            cp = small_copy(k, me_j)
            cp.start()
            sends.append(cp)
        for i in range(n):
            for k in range(3):
                kj = 2 * chips[k][0] + chips[k][1]
                ici(i, k, kj).wait_recv()
                cp = relay(i, k, c)
                cp.start()
                sends.append(cp)
        for k in range(3):
            small_copy(k, 2 * chips[k][0] + chips[k][1]).wait_recv()
        for i in range(n):
            for k in range(3):
                relay(i, k, 1 - c).wait_recv()
        for cp in sends:
            cp.wait_send()
        for cp in local:
            cp.wait()

    out_shape = [jax.ShapeDtypeStruct((N_SHARDS * R, C) if by_rows[i] else (R, N_SHARDS * C), BF16)
                 for i, (R, C) in enumerate(shapes)]
    out_shape.append(jax.ShapeDtypeStruct((8, N_SHARDS * 256), F32))
    n_sems = 6 * n + 3
    return pl.pallas_call(
        body, name="all_gather_weights",
        in_specs=[ANY] * (n + 1), out_specs=[ANY] * (n + 1), out_shape=out_shape,
        scratch_shapes=[pltpu.SemaphoreType.DMA((n_sems,)), pltpu.SemaphoreType.DMA((n_sems,)),
                        pltpu.SemaphoreType.DMA((n + 1,))],
    )(*shards, small)


def _core_exchange(grads):
    n = len(grads)

    def body(*refs):
        ins, mine, theirs = refs[:n], refs[n:2 * n], refs[2 * n:3 * n]
        send_sems, recv_sems, local_sems = refs[3 * n:]
        x, y, c = _place()
        copies = []
        for i in range(n):
            cp = pltpu.make_async_copy(ins[i].at[c], mine[i], local_sems.at[i])
            cp.start()
            copies.append(cp)
            cp = pltpu.make_async_remote_copy(
                src_ref=ins[i].at[1 - c], dst_ref=theirs[i], send_sem=send_sems.at[i], recv_sem=recv_sems.at[i],
                device_id=(x, y, 1 - c), device_id_type=MESH)
            cp.start()
            copies.append(cp)
        for cp in copies:
            cp.wait()

    half = [jax.ShapeDtypeStruct(g.shape[1:], F32) for g in grads]
    return pl.pallas_call(
        body, name="grad_core_exchange",
        in_specs=[ANY] * n, out_specs=[ANY] * (2 * n), out_shape=half + half,
        scratch_shapes=[pltpu.SemaphoreType.DMA((n,))] * 3,
    )(*grads)


def _chip_exchange(sums, by_rows):
    n = len(sums)
    dims = [(s.shape[1], s.shape[2]) if by_rows[i] else (s.shape[0], s.shape[1] // N_SHARDS)
            for i, s in enumerate(sums)]

    def body(*refs):
        ins, outs = refs[:n], refs[n:2 * n]
        send_sems, recv_sems, local_sems = refs[2 * n:]
        x, y, c = _place()
        me_j = 2 * x + y
        chips = _other_chips(x, y)

        def shard(i, j):
            if by_rows[i]:
                return ins[i].at[j]
            return ins[i].at[:, pl.ds(pl.multiple_of(j * dims[i][1], 128), dims[i][1])]

        copies = []
        for i in range(n):
            cp = pltpu.make_async_copy(shard(i, me_j), outs[i].at[me_j], local_sems.at[i])
            cp.start()
            copies.append(cp)
            for k in range(3):
                kj = 2 * chips[k][0] + chips[k][1]
                cp = pltpu.make_async_remote_copy(
                    src_ref=shard(i, kj), dst_ref=outs[i].at[me_j],
                    send_sem=send_sems.at[3 * i + k], recv_sem=recv_sems.at[3 * i + k],
                    device_id=(*chips[k], c), device_id_type=MESH)
                cp.start()
                copies.append(cp)
        for cp in copies:
            cp.wait()

    return pl.pallas_call(
        body, name="grad_chip_exchange",
        in_specs=[ANY] * n, out_specs=[ANY] * n,
        out_shape=[jax.ShapeDtypeStruct((N_SHARDS, h, cc), F32) for h, cc in dims],
        scratch_shapes=[pltpu.SemaphoreType.DMA((3 * n,)), pltpu.SemaphoreType.DMA((3 * n,)),
                        pltpu.SemaphoreType.DMA((n,))],
    )(*sums)


def _core_share(halves):
    n = len(halves)

    def body(*refs):
        ins, outs = refs[:n], refs[n:2 * n]
        send_sems, recv_sems, local_sems = refs[2 * n:]
        x, y, c = _place()
        copies = []
        for i in range(n):
            cp = pltpu.make_async_copy(ins[i], outs[i].at[c], local_sems.at[i])
            cp.start()
            copies.append(cp)
            cp = pltpu.make_async_remote_copy(
                src_ref=ins[i], dst_ref=outs[i].at[c], send_sem=send_sems.at[i], recv_sem=recv_sems.at[i],
                device_id=(x, y, 1 - c), device_id_type=MESH)
            cp.start()
            copies.append(cp)
        for cp in copies:
            cp.wait()

    return pl.pallas_call(
        body, name="grad_core_share",
        in_specs=[ANY] * n, out_specs=[ANY] * n,
        out_shape=[jax.ShapeDtypeStruct((2,) + h.shape, F32) for h in halves],
        scratch_shapes=[pltpu.SemaphoreType.DMA((n,))] * 3,
    )(*halves)


def _exchange_small(pack):
    def body(in_ref, out_ref, send_sems, recv_sems, local_sem):
        x, y, c = _place()
        me = 4 * x + 2 * y + c
        peers = [(px, py, pc) for px in (x, 1 - x) for py in (y, 1 - y) for pc in (c, 1 - c)][1:]
        copies = [pltpu.make_async_copy(in_ref, out_ref.at[me], local_sem)]
        for k, peer in enumerate(peers):
            copies.append(pltpu.make_async_remote_copy(
                src_ref=in_ref, dst_ref=out_ref.at[me], send_sem=send_sems.at[k], recv_sem=recv_sems.at[k],
                device_id=peer, device_id_type=MESH))
        for cp in copies:
            cp.start()
        for cp in copies:
            cp.wait()

    return pl.pallas_call(
        body, name="grad_small_exchange",
        in_specs=[ANY], out_specs=ANY, out_shape=jax.ShapeDtypeStruct((N_DEV,) + pack.shape, F32),
        scratch_shapes=[pltpu.SemaphoreType.DMA((N_DEV - 1,)), pltpu.SemaphoreType.DMA((N_DEV - 1,)),
                        pltpu.SemaphoreType.DMA],
    )(pack)


def _pack_rows(parts, rows):
    flat = jnp.concatenate([a.reshape(-1) for a in parts])
    return jnp.pad(flat, (0, rows * 128 - flat.shape[0])).reshape(rows, 128)


def _unpack_rows(pack, shapes):
    flat = pack.reshape(-1)
    out, at = [], 0
    for s in shapes:
        size = 1
        for d in s:
            size *= d
        out.append(flat[at:at + size].reshape(s))
        at += size
    return out


def kernel(x, p, norm1_g, w_in, b_gate, pool_w, pool_scale, pool_proj, conv_w, conv_b, w_rg, b_rg, w_ig, b_ig, lru_lambda, lru_proj, w_out, norm2_g, w_ffn_in, w_ffn_out, ple_norm_g, w_ple_gate, w_ple_proj, final_g, loss_target, m_norm1_g, m_w_in, m_b_gate, m_pool_w, m_pool_scale, m_pool_proj, m_conv_w, m_conv_b, m_w_rg, m_b_rg, m_w_ig, m_b_ig, m_lru_lambda, m_lru_proj, m_w_out, m_norm2_g, m_w_ffn_in, m_w_ffn_out, m_ple_norm_g, m_w_ple_gate, m_w_ple_proj, m_final_g, v_norm1_g, v_w_in, v_b_gate, v_pool_w, v_pool_scale, v_pool_proj, v_conv_w, v_conv_b, v_w_rg, v_b_rg, v_w_ig, v_b_ig, v_lru_lambda, v_lru_proj, v_w_out, v_norm2_g, v_w_ffn_in, v_w_ffn_out, v_ple_norm_g, v_w_ple_gate, v_w_ple_proj, v_final_g):
    weights = dict(norm1_g=norm1_g, w_in=w_in, b_gate=b_gate, pool_w=pool_w, pool_scale=pool_scale,
                   pool_proj=pool_proj, conv_w=conv_w, conv_b=conv_b, w_rg=w_rg, b_rg=b_rg, w_ig=w_ig, b_ig=b_ig,
                   lru_lambda=lru_lambda, lru_proj=lru_proj, w_out=w_out, norm2_g=norm2_g, w_ffn_in=w_ffn_in,
                   w_ffn_out=w_ffn_out, ple_norm_g=ple_norm_g, w_ple_gate=w_ple_gate, w_ple_proj=w_ple_proj,
                   final_g=final_g)
    m_in = dict(norm1_g=m_norm1_g, w_in=m_w_in, b_gate=m_b_gate, pool_w=m_pool_w, pool_scale=m_pool_scale,
                pool_proj=m_pool_proj, conv_w=m_conv_w, conv_b=m_conv_b, w_rg=m_w_rg, b_rg=m_b_rg, w_ig=m_w_ig,
                b_ig=m_b_ig, lru_lambda=m_lru_lambda, lru_proj=m_lru_proj, w_out=m_w_out, norm2_g=m_norm2_g,
                w_ffn_in=m_w_ffn_in, w_ffn_out=m_w_ffn_out, ple_norm_g=m_ple_norm_g, w_ple_gate=m_w_ple_gate,
                w_ple_proj=m_w_ple_proj, final_g=m_final_g)
    v_in = dict(norm1_g=v_norm1_g, w_in=v_w_in, b_gate=v_b_gate, pool_w=v_pool_w, pool_scale=v_pool_scale,
                pool_proj=v_pool_proj, conv_w=v_conv_w, conv_b=v_conv_b, w_rg=v_w_rg, b_rg=v_b_rg, w_ig=v_w_ig,
                b_ig=v_b_ig, lru_lambda=v_lru_lambda, lru_proj=v_lru_proj, w_out=v_w_out, norm2_g=v_norm2_g,
                w_ffn_in=v_w_ffn_in, w_ffn_out=v_w_ffn_out, ple_norm_g=v_ple_norm_g, w_ple_gate=v_w_ple_gate,
                w_ple_proj=v_w_ple_proj, final_g=v_final_g)
    names = list(weights)
    big = ["w_in", "pool_proj", "lru_proj", "w_out", "w_ffn_in", "w_ffn_out", "w_ple_gate", "w_ple_proj"]
    by_rows = [n in ("lru_proj", "w_out", "w_ffn_out", "w_ple_gate") for n in big]
    small = [n for n in names if n not in big]

    shard_j = 2 * lax.axis_index("x") + lax.axis_index("y")
    T = x.shape[1]
    xs, ps, tgt = x[0], p[0, 0], loss_target[0]

    small_local = jnp.concatenate([b_gate[0], conv_w[0], jnp.zeros((2, 256), F32)], axis=0)
    gathered = _all_gather_weights([weights[n][0].astype(BF16) for n in big], by_rows, small_local)
    full = dict(zip(big, gathered[:-1]))
    b_gate_full = gathered[-1][0:2].reshape(1, 2 * D_MODEL)
    conv_w_full = gathered[-1][2:6]
    pool_w_b, w_rg_b, w_ig_b = pool_w[0].astype(BF16), w_rg[0].astype(BF16), w_ig[0].astype(BF16)
    b_rg_row, b_ig_row = b_rg.reshape(1, D_MODEL), b_ig.reshape(1, D_MODEL)
    final_row = final_g.reshape(1, D_MODEL)

    zp, zl, zg, zt, u = _f1_in_proj(xs, norm1_g, full["w_in"])
    h1, hs, yp, yl = _f2_mixer(xs, zp, zl, zg, zt, b_gate_full, pool_w_b, pool_scale, full["pool_proj"], conv_w_full,
                               conv_b, w_rg_b, b_rg_row, w_ig_b, b_ig_row, lru_lambda, full["lru_proj"], full["w_out"])
    h2, v, ff, act = _f3_ffn(h1, norm2_g, full["w_ffn_in"], full["w_ffn_out"])

    loss_sum, dh2, g_ple_gate, g_ple_proj, vec4 = _b4_ple_loss(
        h2, ps, tgt, ple_norm_g, full["w_ple_gate"], full["w_ple_proj"], final_row)
    dff, dh1, vec3 = _b3_ffn(dh2, h1, ff, norm2_g, full["w_ffn_in"], full["w_ffn_out"])
    g_ffn_in = _wgrad(v, dff, 2 * D_FF // N_SHARDS, "wgrad_ffn_in")
    g_ffn_out = _wgrad(act, dh2, D_MODEL, "wgrad_ffn_out")
    dzt, dyp, dyl, g_w_out, vec_g = _b2_gates(dh1, zt, yp, yl, b_gate_full, full["w_out"])
    dzp, g_pool_proj, g_pool_w, vec_p = _b2_pool(dyp, zp, pool_w_b, pool_scale, full["pool_proj"])
    dzl, dzg, g_lru_proj, g_w_rg, g_w_ig, vec_l = _b2_lru(
        dyl, zl, zg, hs, conv_w_full, conv_b, w_rg_b, b_rg_row, w_ig_b, b_ig_row, lru_lambda, full["lru_proj"])
    grad_x, vec1 = _b1_in_proj(dzp, dzl, dzg, dzt, xs, dh1, norm1_g, full["w_in"])
    g_w_in = jnp.concatenate([
        _wgrad(u, dzp, POOL_WIDTH, "wgrad_in_pool"), _wgrad(u, dzl, D_MODEL, "wgrad_in_lru"),
        _wgrad(u, dzg, D_MODEL, "wgrad_in_gelu"), _wgrad(u, dzt, D_MODEL, "wgrad_in_gate")], axis=1)

    loss = lax.psum(loss_sum[0, 0] * (0.5 / D_MODEL), ("x", "y", "c"))

    big_grads = dict(w_in=g_w_in, pool_proj=g_pool_proj, lru_proj=g_lru_proj, w_out=g_w_out, w_ffn_in=g_ffn_in,
                     w_ffn_out=g_ffn_out, w_ple_gate=g_ple_gate, w_ple_proj=g_ple_proj)
    halves = []
    for n, rows in zip(big, by_rows):
        g = big_grads[n]
        R, C = g.shape
        if rows:
            rs = R // N_SHARDS
            halves.append(g.reshape(N_SHARDS, 2, rs // 2, C).transpose(1, 0, 2, 3).reshape(2, R // 2, C))
        else:
            halves.append(g.reshape(2, R // 2, C))
    mine, theirs = _split2(_core_exchange(halves))
    chip_sums = []
    for i, n in enumerate(big):
        s = _sum_arrays([mine[i], theirs[i]], "sum_cores_" + n)
        chip_sums.append(s.reshape(N_SHARDS, s.shape[0] // N_SHARDS, s.shape[1]) if by_rows[i] else s)
    slots = _chip_exchange(chip_sums, by_rows)
    reduced = _core_share([_sum_slots(q, "sum_chips_" + n) for q, n in zip(slots, big)])

    grads, deltas, new_m, new_v = {}, {}, {}, {}
    for n, r in zip(big, reduced):
        g = r.reshape(r.shape[0] * r.shape[1], r.shape[2])
        d, nm, nv = _adamw(weights[n][0], g, m_in[n][0], v_in[n][0], "adamw_" + n)
        grads[n], deltas[n], new_m[n], new_v[n] = g[None], d[None], nm[None], nv[None]

    small_full = dict(
        norm1_g=vec1[0], b_gate=vec_g[0:2], pool_w=g_pool_w, pool_scale=vec_p[0, :POOL_WIDTH],
        conv_w=vec_l[_V_CONVW:_V_CONVW + CONV_WIDTH], conv_b=vec_l[_V_CONVB], w_rg=g_w_rg, b_rg=vec_l[_V_BRG],
        w_ig=g_w_ig, b_ig=vec_l[_V_BIG], lru_lambda=vec_l[_V_LAM], norm2_g=vec3[0], ple_norm_g=vec4[1],
        final_g=vec4[0])
    full_shapes = [small_full[n].shape for n in small]
    n_full = sum(int(small_full[n].size) for n in small)
    rows_full = -(-n_full // (128 * SUBLANES)) * SUBLANES
    everyone = _exchange_small(_pack_rows([small_full[n] for n in small], rows_full))
    summed = dict(zip(small, _unpack_rows(_sum_slots(everyone, "sum_small"), full_shapes)))
    summed["b_gate"] = lax.dynamic_slice_in_dim(summed["b_gate"], shard_j * 256, 256, axis=1)
    summed["conv_w"] = lax.dynamic_slice_in_dim(summed["conv_w"], shard_j * 256, 256, axis=1)
    local_shapes = [weights[n].shape for n in small]
    n_local = sum(int(weights[n].size) for n in small)
    rows_local = -(-n_local // (128 * SUBLANES)) * SUBLANES
    packs = [_pack_rows([src[n] for n in small], rows_local) for src in (weights, summed, m_in, v_in)]
    d_s, nm_s, nv_s = _adamw(*packs, "adamw_small")
    for dst, pack in ((grads, packs[1]), (deltas, d_s), (new_m, nm_s), (new_v, nv_s)):
        dst.update(zip(small, _unpack_rows(pack, local_shapes)))

    return (loss, grad_x[None], *[grads[n] for n in names], *[deltas[n] for n in names],
            *[new_m[n] for n in names], *[new_v[n] for n in names])


def _split2(seq):
    n = len(seq) // 2
    return seq[:n], seq[n:]
```

```python
import functools

import jax
import jax.numpy as jnp
from jax import lax
from jax.experimental import pallas as pl
from jax.experimental.pallas import tpu as pltpu

F32 = jnp.float32
BF16 = jnp.bfloat16

D_MODEL = 1024
POOL_WIDTH = 512
POOL_GROUP_DIM = 128
POOL_WINDOWS = (2, 4, 8, 16)
POOL_HALO = 16
LRU_HEADS = 8
LRU_HEAD_DIM = 128
CONV_WIDTH = 4
LRU_C = 8.0
D_FF = 2816
PLE_DIM = 256
RMS_EPS = 1e-6
N_SHARDS = 4
N_DEV = 8

ADAM_LR = 0.001
ADAM_B1 = 0.9
ADAM_B2 = 0.999
ADAM_EPS = 1e-08
ADAM_WD = 0.01
ADAM_STEP = 10

ROW_TILE = 256
SUBLANES = 8
VMEM_LIMIT = 56 * 1024 * 1024
MESH = pl.DeviceIdType.MESH
ANY = pl.BlockSpec(memory_space=pl.ANY)


def _params(semantics=None):
    return pltpu.CompilerParams(dimension_semantics=semantics, vmem_limit_bytes=VMEM_LIMIT)


def _resident(shape):
    n = len(shape)
    return pl.BlockSpec(shape, lambda *_: (0,) * n, pipeline_mode=pl.Buffered(1))


def _acc(shape):
    n = len(shape)
    return pl.BlockSpec(shape, lambda *_: (0,) * n)


def _rows(tile, cols):
    return pl.BlockSpec((tile, cols), lambda i: (i, 0))


def _rows_rev(tile, cols, n_tiles):
    return pl.BlockSpec((tile, cols), lambda i: (n_tiles - 1 - i, 0))


def _halo_before(rows, cols, tile):
    per = tile // rows
    return pl.BlockSpec((rows, cols), lambda i: (jnp.maximum(i * per - 1, 0), 0))


def _halo_before_rev(rows, cols, tile, n_tiles):
    per = tile // rows
    return pl.BlockSpec((rows, cols), lambda i: (jnp.maximum((n_tiles - 1 - i) * per - 1, 0), 0))


def _nn(a, b):
    return jnp.dot(a, b, preferred_element_type=F32)


def _nt(a, b):
    return lax.dot_general(a, b, (((1,), (1,)), ((), ())), preferred_element_type=F32)


def _tn(a, b):
    return lax.dot_general(a, b, (((0,), (0,)), ((), ())), preferred_element_type=F32)


def _rms(x):
    r = lax.rsqrt(jnp.mean(x * x, axis=-1, keepdims=True) + RMS_EPS)
    return x * r, r


def _rms_bwd(dn, n, r):
    return r * (dn - n * jnp.mean(dn * n, axis=-1, keepdims=True))


def _sigmoid(x):
    return 1.0 / (1.0 + jnp.exp(-x))


_GELU_C = 0.7978845608028654
_GELU_A = 0.044715


def _gelu(x):
    t = jnp.tanh(_GELU_C * (x + _GELU_A * x * x * x))
    return 0.5 * x * (1.0 + t)


def _gelu_and_grad(x):
    x2 = x * x
    t = jnp.tanh(_GELU_C * (x + _GELU_A * x2 * x))
    cdf = 0.5 * (1.0 + t)
    grad = cdf + 0.5 * x * (1.0 - t * t) * _GELU_C * (1.0 + 3.0 * _GELU_A * x2)
    return x * cdf, grad


def _softplus_neg(lam):
    e = jnp.exp(-jnp.abs(lam))
    sp = jnp.maximum(-lam, 0.0) + jnp.log1p(e)
    return sp, -_sigmoid(-lam)


def _colsum(v):
    return jnp.sum(v, axis=0, keepdims=True)


def _row_ids(shape):
    return lax.broadcasted_iota(jnp.int32, shape, 0)


def _shift_down(cat, k):
    return pltpu.roll(cat, k, 0) if k else cat


def _shift_up(cat, k):
    return pltpu.roll(cat, cat.shape[0] - k, 0) if k else cat


def _f1_in_proj(x, norm1_g, w_in):
    T = x.shape[0]
    tm = ROW_TILE
    splits = (0, POOL_WIDTH, POOL_WIDTH + D_MODEL, POOL_WIDTH + 2 * D_MODEL, POOL_WIDTH + 4 * D_MODEL)

    def body(x_ref, g_ref, w_ref, zp_ref, zl_ref, zg_ref, zt_ref, u_ref):
        n, _ = _rms(x_ref[...])
        u = (n * g_ref[...]).astype(BF16)
        u_ref[...] = u
        for k, o_ref in enumerate((zp_ref, zl_ref, zg_ref, zt_ref)):
            o_ref[...] = _nn(u, w_ref[:, splits[k]:splits[k + 1]])

    widths = [splits[k + 1] - splits[k] for k in range(4)]
    return pl.pallas_call(
        body, name="f1_in_proj", grid=(T // tm,),
        in_specs=[_rows(tm, D_MODEL), _resident((1, D_MODEL)), _resident(w_in.shape)],
        out_specs=[_rows(tm, w) for w in widths] + [_rows(tm, D_MODEL)],
        out_shape=[jax.ShapeDtypeStruct((T, w), F32) for w in widths] + [jax.ShapeDtypeStruct((T, D_MODEL), BF16)],
        compiler_params=_params(("arbitrary",)),
    )(x, norm1_g, w_in)


def _pool_forward(zp_cat, pw_ref, scale, first_row):
    tt = zp_cat.shape[0] - POOL_HALO
    t_glob = first_row + _row_ids((tt, POOL_GROUP_DIM))
    pooled, mixed = [], []
    for g, w in enumerate(POOL_WINDOWS):
        cat = zp_cat[:, g * POOL_GROUP_DIM:(g + 1) * POOL_GROUP_DIM]
        s, k = cat, 1
        while k < w:
            s = s + _shift_down(s, k)
            k *= 2
        cnt = jnp.minimum(t_glob + 1, w).astype(F32)
        pg = s[POOL_HALO:] / cnt - cat[POOL_HALO:]
        pooled.append(pg)
        mixed.append(_nn(pg.astype(BF16), pw_ref[g]))
    return jnp.concatenate(pooled, axis=1), jnp.concatenate(mixed, axis=1)


def _lru_gates(zl_cat, conv_w, conv_b, wrg_ref, brg, wig_ref, big, sp, first_row):
    tt = zl_cat.shape[0] - SUBLANES
    xc = conv_w[CONV_WIDTH - 1:CONV_WIDTH] * zl_cat
    for k in range(1, CONV_WIDTH):
        xc = xc + conv_w[CONV_WIDTH - 1 - k:CONV_WIDTH - k] * _shift_down(zl_cat, k)
    xc = xc[SUBLANES:] + conv_b
    xh = xc.astype(BF16)
    pr, pi = [], []
    for h in range(LRU_HEADS):
        xs = xh[:, h * LRU_HEAD_DIM:(h + 1) * LRU_HEAD_DIM]
        pr.append(_nn(xs, wrg_ref[h]))
        pi.append(_nn(xs, wig_ref[h]))
    r = _sigmoid(jnp.concatenate(pr, axis=1) + brg)
    ig = _sigmoid(jnp.concatenate(pi, axis=1) + big)
    a = jnp.exp(-LRU_C * r * sp)
    mult = jnp.sqrt(jnp.maximum(1.0 - a * a, 0.0))
    t_glob = first_row + _row_ids((tt, D_MODEL))
    mult = jnp.where(t_glob == 0, 1.0, mult)
    return xc, r, ig, a, mult


def _f2_mixer(x, zp, zl, zg, zt, b_gate, pool_w, pool_scale, pool_proj, conv_w, conv_b, w_rg, b_rg, w_ig, b_ig,
              lru_lambda, lru_proj, w_out):
    T = x.shape[0]
    tt = ROW_TILE
    n_groups = tt // SUBLANES

    def body(x_ref, zp_ref, zph_ref, zl_ref, zlh_ref, zg_ref, zt_ref, bg_ref, pw_ref, ps_ref, pp_ref, cw_ref, cb_ref,
             wrg_ref, brg_ref, wig_ref, big_ref, lam_ref, lp_ref, wo_ref,
             h1_ref, hs_ref, yp_ref, yl_ref, a_s, b_s, carry_s):
        i = pl.program_id(0)
        first_row = i * tt
        keep = (i > 0).astype(F32)

        zp_cat = jnp.concatenate([zph_ref[...] * keep, zp_ref[...]], axis=0)
        _, mixed = _pool_forward(zp_cat, pw_ref, ps_ref[...], first_row)
        y_pool = _nn((mixed * ps_ref[...]).astype(BF16), pp_ref[...])

        sp, _ = _softplus_neg(lam_ref[...])
        zl_cat = jnp.concatenate([zlh_ref[...] * keep, zl_ref[...]], axis=0)
        xc, _, ig, a, mult = _lru_gates(zl_cat, cw_ref[...], cb_ref[...], wrg_ref, brg_ref[...], wig_ref,
                                        big_ref[...], sp, first_row)
        a_s[...] = a
        b_s[...] = mult * ig * xc

        @pl.when(i == 0)
        def _():
            carry_s[...] = jnp.zeros_like(carry_s)

        rows8 = _row_ids((SUBLANES, D_MODEL))

        def group(g, carry):
            at = pl.ds(pl.multiple_of(g * SUBLANES, SUBLANES), SUBLANES)
            A, B = a_s[at, :], b_s[at, :]
            for s in (1, 2, 4):
                m = rows8 >= s
                B = jnp.where(m, A * pltpu.roll(B, s, 0) + B, B)
                A = jnp.where(m, A * pltpu.roll(A, s, 0), A)
            h = A * carry + B
            hs_ref[at, :] = h
            return jnp.broadcast_to(h[SUBLANES - 1:SUBLANES, :], (SUBLANES, D_MODEL))

        carry_s[...] = lax.fori_loop(0, n_groups, group, carry_s[...])
        y_lru = _nn((hs_ref[...] * _gelu(zg_ref[...])).astype(BF16), lp_ref[...])

        gates = _sigmoid(zt_ref[...] + bg_ref[...])
        merged = gates[:, :D_MODEL] * y_pool + gates[:, D_MODEL:] * y_lru
        h1_ref[...] = x_ref[...] + _nn(merged.astype(BF16), wo_ref[...])
        yp_ref[...] = y_pool.astype(BF16)
        yl_ref[...] = y_lru.astype(BF16)

    res = [pool_w, pool_scale, pool_proj, conv_w, conv_b, w_rg, b_rg, w_ig, b_ig, lru_lambda, lru_proj, w_out]
    return pl.pallas_call(
        body, name="f2_mixer", grid=(T // tt,),
        in_specs=[_rows(tt, D_MODEL), _rows(tt, POOL_WIDTH), _halo_before(POOL_HALO, POOL_WIDTH, tt),
                  _rows(tt, D_MODEL), _halo_before(SUBLANES, D_MODEL, tt), _rows(tt, D_MODEL), _rows(tt, 2 * D_MODEL),
                  _resident(b_gate.shape)] + [_resident(w.shape) for w in res],
        out_specs=[_rows(tt, D_MODEL)] * 4,
        out_shape=[jax.ShapeDtypeStruct((T, D_MODEL), F32), jax.ShapeDtypeStruct((T, D_MODEL), F32),
                   jax.ShapeDtypeStruct((T, D_MODEL), BF16), jax.ShapeDtypeStruct((T, D_MODEL), BF16)],
        scratch_shapes=[pltpu.VMEM((tt, D_MODEL), F32), pltpu.VMEM((tt, D_MODEL), F32),
                        pltpu.VMEM((SUBLANES, D_MODEL), F32)],
        compiler_params=_params(("arbitrary",)),
    )(x, zp, zp, zl, zl, zg, zt, b_gate, *res)


def _f3_ffn(h1, norm2_g, w_ffn_in, w_ffn_out):
    T = h1.shape[0]
    tm = ROW_TILE

    def body(h_ref, g_ref, wi_ref, wo_ref, h2_ref, v_ref, ff_ref, act_ref):
        h = h_ref[...]
        n, _ = _rms(h)
        v = (n * g_ref[...]).astype(BF16)
        v_ref[...] = v
        g_ff = _nn(v, wi_ref[:, :D_FF])
        u_ff = _nn(v, wi_ref[:, D_FF:])
        ff_ref[:, :D_FF] = g_ff.astype(BF16)
        ff_ref[:, D_FF:] = u_ff.astype(BF16)
        act = (g_ff * _sigmoid(g_ff) * u_ff).astype(BF16)
        act_ref[...] = act
        h2_ref[...] = h + _nn(act, wo_ref[...])

    return pl.pallas_call(
        body, name="f3_ffn", grid=(T // tm,),
        in_specs=[_rows(tm, D_MODEL), _resident((1, D_MODEL)), _resident(w_ffn_in.shape), _resident(w_ffn_out.shape)],
        out_specs=[_rows(tm, D_MODEL), _rows(tm, D_MODEL), _rows(tm, 2 * D_FF), _rows(tm, D_FF)],
        out_shape=[jax.ShapeDtypeStruct((T, D_MODEL), F32), jax.ShapeDtypeStruct((T, D_MODEL), BF16),
                   jax.ShapeDtypeStruct((T, 2 * D_FF), BF16), jax.ShapeDtypeStruct((T, D_FF), BF16)],
        compiler_params=_params(("arbitrary",)),
    )(h1, norm2_g, w_ffn_in, w_ffn_out)


def _b4_ple_loss(h2, p, target, ple_norm_g, w_ple_gate, w_ple_proj, final_g):
    T = h2.shape[0]
    tm = ROW_TILE

    def body(h_ref, p_ref, t_ref, gp_ref, wg_ref, wp_ref, gf_ref, loss_ref, dh2_ref, dwg_ref, dwp_ref, vec_ref):
        @pl.when(pl.program_id(0) == 0)
        def _():
            loss_ref[...] = jnp.zeros_like(loss_ref)
            dwg_ref[...] = jnp.zeros_like(dwg_ref)
            dwp_ref[...] = jnp.zeros_like(dwp_ref)
            vec_ref[...] = jnp.zeros_like(vec_ref)

        h2v = h_ref[...]
        n3, r3 = _rms(h2v)
        n3g = (n3 * gp_ref[...]).astype(BF16)
        pg = _sigmoid(_nn(n3g, wg_ref[...]))
        pb = p_ref[...].astype(BF16)
        e = _nn(pb, wp_ref[...])
        h3 = h2v + pg * e
        n4, r4 = _rms(h3)
        diff = n4 * gf_ref[...] - t_ref[...]
        loss_ref[...] += jnp.sum(diff * diff).reshape(1, 1)
        dy = diff * (1.0 / D_MODEL)
        vec_ref[0:1, :] += _colsum(dy * n4)
        dh3 = _rms_bwd(dy * gf_ref[...], n4, r4)
        dwp_ref[...] += _tn(pb, (dh3 * pg).astype(BF16))
        dq = (dh3 * e * pg * (1.0 - pg)).astype(BF16)
        dwg_ref[...] += _tn(n3g, dq)
        dn3g = _nt(dq, wg_ref[...])
        vec_ref[1:2, :] += _colsum(dn3g * n3)
        dh2_ref[...] = dh3 + _rms_bwd(dn3g * gp_ref[...], n3, r3)

    return pl.pallas_call(
        body, name="b4_ple_loss", grid=(T // tm,),
        in_specs=[_rows(tm, D_MODEL), _rows(tm, PLE_DIM), _rows(tm, D_MODEL), _resident((1, D_MODEL)),
                  _resident(w_ple_gate.shape), _resident(w_ple_proj.shape), _resident((1, D_MODEL))],
        out_specs=[_acc((1, 1)), _rows(tm, D_MODEL), _acc(w_ple_gate.shape), _acc(w_ple_proj.shape),
                   _acc((SUBLANES, D_MODEL))],
        out_shape=[jax.ShapeDtypeStruct((1, 1), F32), jax.ShapeDtypeStruct((T, D_MODEL), F32),
                   jax.ShapeDtypeStruct(w_ple_gate.shape, F32), jax.ShapeDtypeStruct(w_ple_proj.shape, F32),
                   jax.ShapeDtypeStruct((SUBLANES, D_MODEL), F32)],
        compiler_params=_params(("arbitrary",)),
    )(h2, p, target, ple_norm_g, w_ple_gate, w_ple_proj, final_g)


def _b3_ffn(dh2, h1, ff, norm2_g, w_ffn_in, w_ffn_out):
    T = h1.shape[0]
    tm = ROW_TILE

    def body(d_ref, h_ref, ff_ref, g_ref, wi_ref, wo_ref, dff_ref, dh1_ref, vec_ref):
        @pl.when(pl.program_id(0) == 0)
        def _():
            vec_ref[...] = jnp.zeros_like(vec_ref)

        dh2v = d_ref[...]
        dact = _nt(dh2v.astype(BF16), wo_ref[...])
        g_ff = ff_ref[:, :D_FF].astype(F32)
        u_ff = ff_ref[:, D_FF:].astype(F32)
        s = _sigmoid(g_ff)
        dg = (dact * u_ff * (s * (1.0 + g_ff * (1.0 - s)))).astype(BF16)
        du = (dact * (g_ff * s)).astype(BF16)
        dff_ref[:, :D_FF] = dg
        dff_ref[:, D_FF:] = du
        dv = _nt(dg, wi_ref[:, :D_FF]) + _nt(du, wi_ref[:, D_FF:])
        n2, r2 = _rms(h_ref[...])
        vec_ref[0:1, :] += _colsum(dv * n2)
        dh1_ref[...] = dh2v + _rms_bwd(dv * g_ref[...], n2, r2)

    return pl.pallas_call(
        body, name="b3_ffn", grid=(T // tm,),
        in_specs=[_rows(tm, D_MODEL), _rows(tm, D_MODEL), _rows(tm, 2 * D_FF), _resident((1, D_MODEL)),
                  _resident(w_ffn_in.shape), _resident(w_ffn_out.shape)],
        out_specs=[_rows(tm, 2 * D_FF), _rows(tm, D_MODEL), _acc((SUBLANES, D_MODEL))],
        out_shape=[jax.ShapeDtypeStruct((T, 2 * D_FF), BF16), jax.ShapeDtypeStruct((T, D_MODEL), F32),
                   jax.ShapeDtypeStruct((SUBLANES, D_MODEL), F32)],
        compiler_params=_params(("arbitrary",)),
    )(dh2, h1, ff, norm2_g, w_ffn_in, w_ffn_out)


def _wgrad(a, b, col_tile, name):
    T, K = a.shape
    N = b.shape[1]
    tk = 2 * ROW_TILE

    def body(a_ref, b_ref, o_ref):
        @pl.when(pl.program_id(1) == 0)
        def _():
            o_ref[...] = jnp.zeros_like(o_ref)

        o_ref[...] += _tn(a_ref[...].astype(BF16), b_ref[...].astype(BF16))

    return pl.pallas_call(
        body, name=name, grid=(N // col_tile, T // tk),
        in_specs=[pl.BlockSpec((tk, K), lambda j, k: (k, 0)), pl.BlockSpec((tk, col_tile), lambda j, k: (k, j))],
        out_specs=pl.BlockSpec((K, col_tile), lambda j, k: (0, j)),
        out_shape=jax.ShapeDtypeStruct((K, N), F32),
        compiler_params=_params(("arbitrary", "arbitrary")),
    )(a, b)


def _b2_gates(dh1, zt, yp, yl, b_gate, w_out):
    T = dh1.shape[0]
    tm = ROW_TILE

    def body(d_ref, zt_ref, yp_ref, yl_ref, bg_ref, wo_ref, dzt_ref, dyp_ref, dyl_ref, dwo_ref, vec_ref):
        @pl.when(pl.program_id(0) == 0)
        def _():
            dwo_ref[...] = jnp.zeros_like(dwo_ref)
            vec_ref[...] = jnp.zeros_like(vec_ref)

        db = d_ref[...].astype(BF16)
        dm = _nt(db, wo_ref[...])
        gates = _sigmoid(zt_ref[...] + bg_ref[...])
        g0, g1 = gates[:, :D_MODEL], gates[:, D_MODEL:]
        y_pool, y_lru = yp_ref[...].astype(F32), yl_ref[...].astype(F32)
        dwo_ref[...] += _tn((g0 * y_pool + g1 * y_lru).astype(BF16), db)
        dz0 = dm * y_pool * g0 * (1.0 - g0)
        dz1 = dm * y_lru * g1 * (1.0 - g1)
        vec_ref[0:1, :] += _colsum(dz0)
        vec_ref[1:2, :] += _colsum(dz1)
        dzt_ref[:, :D_MODEL] = dz0.astype(BF16)
        dzt_ref[:, D_MODEL:] = dz1.astype(BF16)
        dyp_ref[...] = (dm * g0).astype(BF16)
        dyl_ref[...] = (dm * g1).astype(BF16)

    return pl.pallas_call(
        body, name="b2_gates", grid=(T // tm,),
        in_specs=[_rows(tm, D_MODEL), _rows(tm, 2 * D_MODEL), _rows(tm, D_MODEL), _rows(tm, D_MODEL),
                  _resident(b_gate.shape), _resident(w_out.shape)],
        out_specs=[_rows(tm, 2 * D_MODEL), _rows(tm, D_MODEL), _rows(tm, D_MODEL), _acc(w_out.shape),
                   _acc((SUBLANES, D_MODEL))],
        out_shape=[jax.ShapeDtypeStruct((T, 2 * D_MODEL), BF16), jax.ShapeDtypeStruct((T, D_MODEL), BF16),
                   jax.ShapeDtypeStruct((T, D_MODEL), BF16), jax.ShapeDtypeStruct(w_out.shape, F32),
                   jax.ShapeDtypeStruct((SUBLANES, D_MODEL), F32)],
        compiler_params=_params(("arbitrary",)),
    )(dh1, zt, yp, yl, b_gate, w_out)


def _b2_pool(dyp, zp, pool_w, pool_scale, pool_proj):
    T = zp.shape[0]
    tt = ROW_TILE
    nt = T // tt

    def body(dy_ref, zp_ref, zph_ref, pw_ref, ps_ref, pp_ref, dzp_ref, dpp_ref, dpw_ref, vec_ref, q_next):
        i = pl.program_id(0)
        ti = nt - 1 - i
        first_row = ti * tt

        @pl.when(i == 0)
        def _():
            dpp_ref[...] = jnp.zeros_like(dpp_ref)
            dpw_ref[...] = jnp.zeros_like(dpw_ref)
            vec_ref[...] = jnp.zeros_like(vec_ref)
            q_next[...] = jnp.zeros_like(q_next)

        keep = (ti > 0).astype(F32)
        zp_cat = jnp.concatenate([zph_ref[...] * keep, zp_ref[...]], axis=0)
        pooled, mixed = _pool_forward(zp_cat, pw_ref, ps_ref[...], first_row)
        dy = dy_ref[...]
        dpp_ref[...] += _tn((mixed * ps_ref[...]).astype(BF16), dy)
        dms = _nt(dy, pp_ref[...])
        vec_ref[0:1, :POOL_WIDTH] += _colsum(dms * mixed)
        dmixed = (dms * ps_ref[...]).astype(BF16)
        t_glob = first_row + _row_ids((tt, POOL_GROUP_DIM))
        dz, q_all = [], []
        for g, w in enumerate(POOL_WINDOWS):
            cols = slice(g * POOL_GROUP_DIM, (g + 1) * POOL_GROUP_DIM)
            dpw_ref[g] += _tn(pooled[:, cols].astype(BF16), dmixed[:, cols])
            dpooled = _nt(dmixed[:, cols], pw_ref[g])
            q = dpooled / jnp.minimum(t_glob + 1, w).astype(F32)
            q_all.append(q)
            s, k = jnp.concatenate([q, q_next[:, cols]], axis=0), 1
            while k < w:
                s = s + _shift_up(s, k)
                k *= 2
            dz.append(s[:tt] - dpooled)
        dzp_ref[...] = jnp.concatenate(dz, axis=1).astype(BF16)
        q_next[...] = jnp.concatenate([q[:POOL_HALO] for q in q_all], axis=1)

    return pl.pallas_call(
        body, name="b2_pool", grid=(nt,),
        in_specs=[_rows_rev(tt, D_MODEL, nt), _rows_rev(tt, POOL_WIDTH, nt),
                  _halo_before_rev(POOL_HALO, POOL_WIDTH, tt, nt),
                  _resident(pool_w.shape), _resident(pool_scale.shape), _resident(pool_proj.shape)],
        out_specs=[_rows_rev(tt, POOL_WIDTH, nt), _acc(pool_proj.shape), _acc(pool_w.shape), _acc((SUBLANES, D_MODEL))],
        out_shape=[jax.ShapeDtypeStruct((T, POOL_WIDTH), BF16), jax.ShapeDtypeStruct(pool_proj.shape, F32),
                   jax.ShapeDtypeStruct(pool_w.shape, F32), jax.ShapeDtypeStruct((SUBLANES, D_MODEL), F32)],
        scratch_shapes=[pltpu.VMEM((POOL_HALO, POOL_WIDTH), F32)],
        compiler_params=_params(("arbitrary",)),
    )(dyp, zp, zp, pool_w, pool_scale, pool_proj)


_V_CONVW, _V_CONVB, _V_BRG, _V_BIG, _V_LAM = 0, 4, 5, 6, 7


def _b2_lru(dyl, zl, zg, hs, conv_w, conv_b, w_rg, b_rg, w_ig, b_ig, lru_lambda, lru_proj):
    T = zl.shape[0]
    tt = ROW_TILE
    nt = T // tt
    n_groups = tt // SUBLANES

    def body(dy_ref, zl_ref, zlh_ref, zg_ref, hs_ref, hsh_ref, cw_ref, cb_ref, wrg_ref, brg_ref, wig_ref, big_ref,
             lam_ref, lp_ref, dzl_ref, dzg_ref, dlp_ref, dwrg_ref, dwig_ref, vec_ref,
             c_s, d_s, g_s, g_next, a_next, dxc_next):
        i = pl.program_id(0)
        ti = nt - 1 - i
        first_row = ti * tt

        @pl.when(i == 0)
        def _():
            dlp_ref[...] = jnp.zeros_like(dlp_ref)
            dwrg_ref[...] = jnp.zeros_like(dwrg_ref)
            dwig_ref[...] = jnp.zeros_like(dwig_ref)
            vec_ref[...] = jnp.zeros_like(vec_ref)
            g_next[...] = jnp.zeros_like(g_next)
            a_next[...] = jnp.zeros_like(a_next)
            dxc_next[...] = jnp.zeros_like(dxc_next)

        keep = (ti > 0).astype(F32)
        sp, dsp_dlam = _softplus_neg(lam_ref[...])
        cw = cw_ref[...]
        zl_cat = jnp.concatenate([zlh_ref[...] * keep, zl_ref[...]], axis=0)
        xc, r, ig, a, mult = _lru_gates(zl_cat, cw, cb_ref[...], wrg_ref, brg_ref[...], wig_ref, big_ref[...], sp,
                                        first_row)
        hs = hs_ref[...]
        gelu, dgelu = _gelu_and_grad(zg_ref[...])
        dy = dy_ref[...]
        dlp_ref[...] += _tn((hs * gelu).astype(BF16), dy)
        dyl = _nt(dy, lp_ref[...])
        dzg_ref[...] = (dyl * hs * dgelu).astype(BF16)

        d_s[...] = dyl * gelu
        c_s[...] = _shift_up(jnp.concatenate([a, a_next[...]], axis=0), 1)[:tt]
        rows8 = _row_ids((SUBLANES, D_MODEL))

        def group(k, carry):
            at = pl.ds(pl.multiple_of((n_groups - 1 - k) * SUBLANES, SUBLANES), SUBLANES)
            C, Dv = c_s[at, :], d_s[at, :]
            for s in (1, 2, 4):
                m = rows8 < SUBLANES - s
                Dv = jnp.where(m, C * pltpu.roll(Dv, SUBLANES - s, 0) + Dv, Dv)
                C = jnp.where(m, C * pltpu.roll(C, SUBLANES - s, 0), C)
            G = C * carry + Dv
            g_s[at, :] = G
            return jnp.broadcast_to(G[0:1, :], (SUBLANES, D_MODEL))

        g_next[...] = lax.fori_loop(0, n_groups, group, g_next[...])
        a_next[...] = jnp.broadcast_to(a[0:1, :], (SUBLANES, D_MODEL))
        G = g_s[...]

        h_prev = _shift_down(jnp.concatenate([hsh_ref[...] * keep, hs], axis=0), 1)[SUBLANES:]
        t_glob = first_row + _row_ids((tt, D_MODEL))
        dmult = jnp.where(t_glob == 0, 0.0, G * ig * xc)
        dla = G * h_prev * a - dmult * (a * a) / mult
        vec_ref[_V_LAM:_V_LAM + 1, :] += _colsum(dla * r) * (-LRU_C) * dsp_dlam
        dpr = dla * (-LRU_C) * sp * r * (1.0 - r)
        dpi = G * mult * xc * ig * (1.0 - ig)
        vec_ref[_V_BRG:_V_BRG + 1, :] += _colsum(dpr)
        vec_ref[_V_BIG:_V_BIG + 1, :] += _colsum(dpi)
        dprb, dpib, xh = dpr.astype(BF16), dpi.astype(BF16), xc.astype(BF16)
        dxc_h = []
        for h in range(LRU_HEADS):
            cols = slice(h * LRU_HEAD_DIM, (h + 1) * LRU_HEAD_DIM)
            dwrg_ref[h] += _tn(xh[:, cols], dprb[:, cols])
            dwig_ref[h] += _tn(xh[:, cols], dpib[:, cols])
            dxc_h.append(_nt(dprb[:, cols], wrg_ref[h]) + _nt(dpib[:, cols], wig_ref[h]))
        dxc = G * mult * ig + jnp.concatenate(dxc_h, axis=1)

        vec_ref[_V_CONVB:_V_CONVB + 1, :] += _colsum(dxc)
        dxc_cat = jnp.concatenate([dxc, dxc_next[...]], axis=0)
        dzl = cw[CONV_WIDTH - 1:CONV_WIDTH] * dxc
        for k in range(CONV_WIDTH):
            lag = CONV_WIDTH - 1 - k
            vec_ref[_V_CONVW + k:_V_CONVW + k + 1, :] += _colsum(dxc * _shift_down(zl_cat, lag)[SUBLANES:])
            if lag:
                dzl = dzl + cw[k:k + 1] * _shift_up(dxc_cat, lag)[:tt]
        dzl_ref[...] = dzl.astype(BF16)
        dxc_next[...] = dxc[:SUBLANES]

    res = [conv_w, conv_b, w_rg, b_rg, w_ig, b_ig, lru_lambda, lru_proj]
    return pl.pallas_call(
        body, name="b2_lru", grid=(nt,),
        in_specs=[_rows_rev(tt, D_MODEL, nt), _rows_rev(tt, D_MODEL, nt), _halo_before_rev(SUBLANES, D_MODEL, tt, nt),
                  _rows_rev(tt, D_MODEL, nt), _rows_rev(tt, D_MODEL, nt), _halo_before_rev(SUBLANES, D_MODEL, tt, nt)]
        + [_resident(w.shape) for w in res],
        out_specs=[_rows_rev(tt, D_MODEL, nt), _rows_rev(tt, D_MODEL, nt), _acc(lru_proj.shape), _acc(w_rg.shape),
                   _acc(w_ig.shape), _acc((SUBLANES, D_MODEL))],
        out_shape=[jax.ShapeDtypeStruct((T, D_MODEL), BF16), jax.ShapeDtypeStruct((T, D_MODEL), BF16),
                   jax.ShapeDtypeStruct(lru_proj.shape, F32), jax.ShapeDtypeStruct(w_rg.shape, F32),
                   jax.ShapeDtypeStruct(w_ig.shape, F32), jax.ShapeDtypeStruct((SUBLANES, D_MODEL), F32)],
        scratch_shapes=[pltpu.VMEM((tt, D_MODEL), F32)] * 3 + [pltpu.VMEM((SUBLANES, D_MODEL), F32)] * 3,
        compiler_params=_params(("arbitrary",)),
    )(dyl, zl, zl, zg, hs, hs, *res)


def _b1_in_proj(dzp, dzl, dzg, dzt, x, dh1, norm1_g, w_in):
    T = x.shape[0]
    tm = ROW_TILE
    splits = (0, POOL_WIDTH, POOL_WIDTH + D_MODEL, POOL_WIDTH + 2 * D_MODEL, POOL_WIDTH + 4 * D_MODEL)

    def body(a_ref, b_ref, c_ref, d_ref, x_ref, dh_ref, g_ref, w_ref, dx_ref, vec_ref):
        @pl.when(pl.program_id(0) == 0)
        def _():
            vec_ref[...] = jnp.zeros_like(vec_ref)

        du = None
        for k, dz_ref in enumerate((a_ref, b_ref, c_ref, d_ref)):
            part = _nt(dz_ref[...], w_ref[:, splits[k]:splits[k + 1]])
            du = part if du is None else du + part
        n1, r1 = _rms(x_ref[...])
        vec_ref[0:1, :] += _colsum(du * n1)
        dx_ref[...] = dh_ref[...] + _rms_bwd(du * g_ref[...], n1, r1)

    return pl.pallas_call(
        body, name="b1_in_proj", grid=(T // tm,),
        in_specs=[_rows(tm, POOL_WIDTH), _rows(tm, D_MODEL), _rows(tm, D_MODEL), _rows(tm, 2 * D_MODEL),
                  _rows(tm, D_MODEL), _rows(tm, D_MODEL), _resident((1, D_MODEL)), _resident(w_in.shape)],
        out_specs=[_rows(tm, D_MODEL), _acc((SUBLANES, D_MODEL))],
        out_shape=[jax.ShapeDtypeStruct((T, D_MODEL), F32), jax.ShapeDtypeStruct((SUBLANES, D_MODEL), F32)],
        compiler_params=_params(("arbitrary",)),
    )(dzp, dzl, dzg, dzt, x, dh1, norm1_g, w_in)


def _row_tile(rows):
    for t in (512, 256, 128, 64, 32, 16, 8):
        if rows % t == 0:
            return t
    return rows


def _scalar_grid(grid, in_specs, out_specs):
    return pltpu.PrefetchScalarGridSpec(num_scalar_prefetch=1, grid=grid, in_specs=in_specs, out_specs=out_specs)


def _cast_into_block(w, by_rows, shard_j, name):
    R, C = w.shape
    tr = _row_tile(R)
    if by_rows:
        out_shape, out_map = (N_SHARDS * R, C), lambda i, j: (j[0] * (R // tr) + i, 0)
    else:
        out_shape, out_map = (R, N_SHARDS * C), lambda i, j: (i, j[0])

    def body(j_ref, w_ref, o_ref):
        o_ref[...] = w_ref[...].astype(BF16)

    return pl.pallas_call(
        body, name=name,
        grid_spec=_scalar_grid((R // tr,), [pl.BlockSpec((tr, C), lambda i, j: (i, 0))], pl.BlockSpec((tr, C), out_map)),
        out_shape=jax.ShapeDtypeStruct(out_shape, BF16),
        compiler_params=_params(("arbitrary",)),
    )(shard_j.reshape(1), w)


def _sum_cores(g, theirs, core, name):
    S, R, C = g.shape
    H = R // 2
    tr = _row_tile(H)
    nh = H // tr

    def body(c_ref, g_ref, t_ref, o_ref):
        o_ref[...] = g_ref[...] + t_ref[...]

    half = pl.BlockSpec((None, tr, C), lambda s, i, c: (s, i, 0))
    return pl.pallas_call(
        body, name=name,
        grid_spec=_scalar_grid((S, nh), [pl.BlockSpec((None, tr, C), lambda s, i, c: (s, c[0] * nh + i, 0)), half], half),
        out_shape=jax.ShapeDtypeStruct((S, H, C), F32),
        compiler_params=_params(("arbitrary", "arbitrary")),
    )(core.reshape(1), g, theirs)


def _sum_chips(sums, slots, by_rows, place, name):
    _, H, C = slots.shape
    tr = _row_tile(H)
    own_map = (lambda i, p: (p[0], i, 0)) if by_rows else (lambda i, p: (0, i, p[0]))

    def body(p_ref, s_ref, q_ref, o_ref):
        o_ref[...] = ((s_ref[...] + q_ref[0]) + q_ref[1]) + q_ref[2]

    return pl.pallas_call(
        body, name=name,
        grid_spec=_scalar_grid(
            (H // tr,),
            [pl.BlockSpec((None, tr, C), own_map), pl.BlockSpec((3, tr, C), lambda i, p: (0, i, 0))],
            pl.BlockSpec((None, tr, C), lambda i, p: (p[1], i, 0))),
        out_shape=jax.ShapeDtypeStruct((2, H, C), F32),
        compiler_params=_params(("arbitrary",)),
    )(place, sums, slots)


def _sum_slots(q, name):
    S, R, C = q.shape
    tr = _row_tile(R)

    def body(q_ref, o_ref):
        acc = q_ref[0]
        for s in range(1, S):
            acc = acc + q_ref[s]
        o_ref[...] = acc

    return pl.pallas_call(
        body, name=name, grid=(R // tr,),
        in_specs=[pl.BlockSpec((S, tr, C), lambda i: (0, i, 0))], out_specs=_rows(tr, C),
        out_shape=jax.ShapeDtypeStruct((R, C), F32),
        compiler_params=_params(("arbitrary",)),
    )(q)


def _adamw(w, g, m, v, name):
    R, C = w.shape
    tr = _row_tile(R)
    c1 = 1.0 - ADAM_B1 ** ADAM_STEP
    c2 = 1.0 - ADAM_B2 ** ADAM_STEP

    def body(w_ref, g_ref, m_ref, v_ref, d_ref, nm_ref, nv_ref):
        gv = g_ref[...]
        nm = ADAM_B1 * m_ref[...] + (1.0 - ADAM_B1) * gv
        nv = ADAM_B2 * v_ref[...] + (1.0 - ADAM_B2) * (gv * gv)
        d_ref[...] = -ADAM_LR * ((nm / c1) / (jnp.sqrt(nv / c2) + ADAM_EPS) + ADAM_WD * w_ref[...])
        nm_ref[...] = nm
        nv_ref[...] = nv

    return pl.pallas_call(
        body, name=name, grid=(R // tr,),
        in_specs=[_rows(tr, C)] * 4, out_specs=[_rows(tr, C)] * 3,
        out_shape=[jax.ShapeDtypeStruct((R, C), F32)] * 3,
        compiler_params=_params(("arbitrary",)),
    )(w, g, m, v)


def _place():
    return lax.axis_index("x"), lax.axis_index("y"), lax.axis_index("c")


def _other_chips(x, y):
    return [(1 - x, y), (x, 1 - y), (1 - x, 1 - y)]


def _shard_block(ref, by_rows, R, C, j, half_rows=None):
    if half_rows is None:
        rows, r0 = R, 0
    else:
        rows = R // 2
        r0 = pl.multiple_of(half_rows * rows, 16)
    if by_rows:
        return ref.at[pl.ds(pl.multiple_of(j * R, 16) + r0, rows), :]
    return ref.at[pl.ds(r0, rows), pl.ds(pl.multiple_of(j * C, 128), C)]


def _all_gather_weights(gathered, shapes, by_rows, small):
    n = len(gathered)

    def body(*refs):
        small_in = refs[n]
        outs, small_out = refs[n + 1:2 * n + 1], refs[2 * n + 1]
        send_sems, recv_sems, local_sem = refs[2 * n + 2:]
        x, y, c = _place()
        me_j = 2 * x + y
        chips = _other_chips(x, y)
        sibling = (x, y, 1 - c)

        def block(i, j, half):
            R, C = shapes[i]
            return _shard_block(outs[i], by_rows[i], R, C, j, half)

        def ici(i, k, src_j):
            return pltpu.make_async_remote_copy(
                src_ref=block(i, src_j, c), dst_ref=block(i, src_j, c),
                send_sem=send_sems.at[6 * i + k], recv_sem=recv_sems.at[6 * i + k],
                device_id=(*chips[k], c), device_id_type=MESH)

        def relay(i, k, half):
            kj = 2 * chips[k][0] + chips[k][1]
            return pltpu.make_async_remote_copy(
                src_ref=block(i, kj, half), dst_ref=block(i, kj, half),
                send_sem=send_sems.at[6 * i + 3 + k], recv_sem=recv_sems.at[6 * i + 3 + k],
                device_id=sibling, device_id_type=MESH)

        def small_copy(k, src_j):
            cols = pl.ds(pl.multiple_of(src_j * 256, 128), 256)
            return pltpu.make_async_remote_copy(
                src_ref=small_in, dst_ref=small_out.at[:, cols],
                send_sem=send_sems.at[6 * n + k], recv_sem=recv_sems.at[6 * n + k],
                device_id=(*chips[k], c), device_id_type=MESH)

        sends = []
        for i in range(n):
            for k in range(3):
                cp = ici(i, k, me_j)
                cp.start()
                sends.append(cp)
        for k in range(3):
            cp = small_copy(k, me_j)
            cp.start()
            sends.append(cp)
        local = pltpu.make_async_copy(small_in, small_out.at[:, pl.ds(pl.multiple_of(me_j * 256, 128), 256)], local_sem)
        local.start()
        for i in range(n):
            for k in range(3):
                kj = 2 * chips[k][0] + chips[k][1]
                ici(i, k, kj).wait_recv()
                cp = relay(i, k, c)
                cp.start()
                sends.append(cp)
        for k in range(3):
            small_copy(k, 2 * chips[k][0] + chips[k][1]).wait_recv()
        for i in range(n):
            for k in range(3):
                relay(i, k, 1 - c).wait_recv()
        for cp in sends:
            cp.wait_send()
        local.wait()

    out_shape = [jax.ShapeDtypeStruct(g.shape, BF16) for g in gathered]
    out_shape.append(jax.ShapeDtypeStruct((8, N_SHARDS * 256), F32))
    n_sems = 6 * n + 3
    return pl.pallas_call(
        body, name="all_gather_weights",
        in_specs=[ANY] * (n + 1), out_specs=[ANY] * (n + 1), out_shape=out_shape,
        input_output_aliases={i: i for i in range(n)},
        scratch_shapes=[pltpu.SemaphoreType.DMA((n_sems,)), pltpu.SemaphoreType.DMA((n_sems,)),
                        pltpu.SemaphoreType.DMA],
    )(*gathered, small)


def _core_exchange(grads):
    n = len(grads)

    def body(*refs):
        ins, theirs = refs[:n], refs[n:2 * n]
        send_sems, recv_sems = refs[2 * n:]
        x, y, c = _place()
        copies = []
        for i in range(n):
            H = grads[i].shape[1] // 2
            cp = pltpu.make_async_remote_copy(
                src_ref=ins[i].at[:, pl.ds(pl.multiple_of((1 - c) * H, 8), H), :], dst_ref=theirs[i],
                send_sem=send_sems.at[i], recv_sem=recv_sems.at[i],
                device_id=(x, y, 1 - c), device_id_type=MESH)
            cp.start()
            copies.append(cp)
        for cp in copies:
            cp.wait()

    return pl.pallas_call(
        body, name="grad_core_exchange",
        in_specs=[ANY] * n, out_specs=[ANY] * n,
        out_shape=[jax.ShapeDtypeStruct((g.shape[0], g.shape[1] // 2, g.shape[2]), F32) for g in grads],
        scratch_shapes=[pltpu.SemaphoreType.DMA((n,))] * 2,
    )(*grads)


def _chip_exchange(sums, by_rows):
    n = len(sums)
    dims = [(s.shape[1], s.shape[2]) if by_rows[i] else (s.shape[1], s.shape[2] // N_SHARDS)
            for i, s in enumerate(sums)]

    def body(*refs):
        ins, outs = refs[:n], refs[n:2 * n]
        send_sems, recv_sems = refs[2 * n:]
        x, y, c = _place()
        chips = _other_chips(x, y)

        def shard(i, j):
            if by_rows[i]:
                return ins[i].at[j]
            return ins[i].at[0, :, pl.ds(pl.multiple_of(j * dims[i][1], 128), dims[i][1])]

        copies = []
        for i in range(n):
            for k in range(3):
                kj = 2 * chips[k][0] + chips[k][1]
                cp = pltpu.make_async_remote_copy(
                    src_ref=shard(i, kj), dst_ref=outs[i].at[k],
                    send_sem=send_sems.at[3 * i + k], recv_sem=recv_sems.at[3 * i + k],
                    device_id=(*chips[k], c), device_id_type=MESH)
                cp.start()
                copies.append(cp)
        for cp in copies:
            cp.wait()

    return pl.pallas_call(
        body, name="grad_chip_exchange",
        in_specs=[ANY] * n, out_specs=[ANY] * n,
        out_shape=[jax.ShapeDtypeStruct((3, h, cc), F32) for h, cc in dims],
        scratch_shapes=[pltpu.SemaphoreType.DMA((3 * n,))] * 2,
    )(*sums)


def _core_share(reduced):
    n = len(reduced)

    def body(*refs):
        outs = refs[n:2 * n]
        send_sems, recv_sems = refs[2 * n:]
        x, y, c = _place()
        copies = []
        for i in range(n):
            cp = pltpu.make_async_remote_copy(
                src_ref=outs[i].at[c], dst_ref=outs[i].at[c], send_sem=send_sems.at[i], recv_sem=recv_sems.at[i],
                device_id=(x, y, 1 - c), device_id_type=MESH)
            cp.start()
            copies.append(cp)
        for cp in copies:
            cp.wait()

    return pl.pallas_call(
        body, name="grad_core_share",
        in_specs=[ANY] * n, out_specs=[ANY] * n,
        out_shape=[jax.ShapeDtypeStruct(r.shape, F32) for r in reduced],
        input_output_aliases={i: i for i in range(n)},
        scratch_shapes=[pltpu.SemaphoreType.DMA((n,))] * 2,
    )(*reduced)


def _exchange_small(pack):
    def body(in_ref, out_ref, send_sems, recv_sems, local_sem):
        x, y, c = _place()
        me = 4 * x + 2 * y + c
        peers = [(px, py, pc) for px in (x, 1 - x) for py in (y, 1 - y) for pc in (c, 1 - c)][1:]
        copies = [pltpu.make_async_copy(in_ref, out_ref.at[me], local_sem)]
        for k, peer in enumerate(peers):
            copies.append(pltpu.make_async_remote_copy(
                src_ref=in_ref, dst_ref=out_ref.at[me], send_sem=send_sems.at[k], recv_sem=recv_sems.at[k],
                device_id=peer, device_id_type=MESH))
        for cp in copies:
            cp.start()
        for cp in copies:
            cp.wait()

    return pl.pallas_call(
        body, name="grad_small_exchange",
        in_specs=[pl.BlockSpec(memory_space=pltpu.VMEM)], out_specs=ANY,
        out_shape=jax.ShapeDtypeStruct((N_DEV,) + pack.shape, F32),
        scratch_shapes=[pltpu.SemaphoreType.DMA((N_DEV - 1,)), pltpu.SemaphoreType.DMA((N_DEV - 1,)),
                        pltpu.SemaphoreType.DMA],
    )(pack)


def _pack_rows(parts, rows):
    flat = jnp.concatenate([a.reshape(-1) for a in parts])
    return jnp.pad(flat, (0, rows * 128 - flat.shape[0])).reshape(rows, 128)


def _unpack_rows(pack, shapes):
    flat = pack.reshape(-1)
    out, at = [], 0
    for s in shapes:
        size = 1
        for d in s:
            size *= d
        out.append(flat[at:at + size].reshape(s))
        at += size
    return out


def kernel(x, p, norm1_g, w_in, b_gate, pool_w, pool_scale, pool_proj, conv_w, conv_b, w_rg, b_rg, w_ig, b_ig, lru_lambda, lru_proj, w_out, norm2_g, w_ffn_in, w_ffn_out, ple_norm_g, w_ple_gate, w_ple_proj, final_g, loss_target, m_norm1_g, m_w_in, m_b_gate, m_pool_w, m_pool_scale, m_pool_proj, m_conv_w, m_conv_b, m_w_rg, m_b_rg, m_w_ig, m_b_ig, m_lru_lambda, m_lru_proj, m_w_out, m_norm2_g, m_w_ffn_in, m_w_ffn_out, m_ple_norm_g, m_w_ple_gate, m_w_ple_proj, m_final_g, v_norm1_g, v_w_in, v_b_gate, v_pool_w, v_pool_scale, v_pool_proj, v_conv_w, v_conv_b, v_w_rg, v_b_rg, v_w_ig, v_b_ig, v_lru_lambda, v_lru_proj, v_w_out, v_norm2_g, v_w_ffn_in, v_w_ffn_out, v_ple_norm_g, v_w_ple_gate, v_w_ple_proj, v_final_g):
    weights = dict(norm1_g=norm1_g, w_in=w_in, b_gate=b_gate, pool_w=pool_w, pool_scale=pool_scale,
                   pool_proj=pool_proj, conv_w=conv_w, conv_b=conv_b, w_rg=w_rg, b_rg=b_rg, w_ig=w_ig, b_ig=b_ig,
                   lru_lambda=lru_lambda, lru_proj=lru_proj, w_out=w_out, norm2_g=norm2_g, w_ffn_in=w_ffn_in,
                   w_ffn_out=w_ffn_out, ple_norm_g=ple_norm_g, w_ple_gate=w_ple_gate, w_ple_proj=w_ple_proj,
                   final_g=final_g)
    m_in = dict(norm1_g=m_norm1_g, w_in=m_w_in, b_gate=m_b_gate, pool_w=m_pool_w, pool_scale=m_pool_scale,
                pool_proj=m_pool_proj, conv_w=m_conv_w, conv_b=m_conv_b, w_rg=m_w_rg, b_rg=m_b_rg, w_ig=m_w_ig,
                b_ig=m_b_ig, lru_lambda=m_lru_lambda, lru_proj=m_lru_proj, w_out=m_w_out, norm2_g=m_norm2_g,
                w_ffn_in=m_w_ffn_in, w_ffn_out=m_w_ffn_out, ple_norm_g=m_ple_norm_g, w_ple_gate=m_w_ple_gate,
                w_ple_proj=m_w_ple_proj, final_g=m_final_g)
    v_in = dict(norm1_g=v_norm1_g, w_in=v_w_in, b_gate=v_b_gate, pool_w=v_pool_w, pool_scale=v_pool_scale,
                pool_proj=v_pool_proj, conv_w=v_conv_w, conv_b=v_conv_b, w_rg=v_w_rg, b_rg=v_b_rg, w_ig=v_w_ig,
                b_ig=v_b_ig, lru_lambda=v_lru_lambda, lru_proj=v_lru_proj, w_out=v_w_out, norm2_g=v_norm2_g,
                w_ffn_in=v_w_ffn_in, w_ffn_out=v_w_ffn_out, ple_norm_g=v_ple_norm_g, w_ple_gate=v_w_ple_gate,
                w_ple_proj=v_w_ple_proj, final_g=v_final_g)
    names = list(weights)
    big = ["w_in", "pool_proj", "lru_proj", "w_out", "w_ffn_in", "w_ffn_out", "w_ple_gate", "w_ple_proj"]
    by_rows = [n in ("lru_proj", "w_out", "w_ffn_out", "w_ple_gate") for n in big]
    small = [n for n in names if n not in big]

    shard_j = 2 * lax.axis_index("x") + lax.axis_index("y")
    T = x.shape[1]
    xs, ps, tgt = x[0], p[0, 0], loss_target[0]

    small_local = jnp.concatenate([b_gate[0], conv_w[0], jnp.zeros((2, 256), F32)], axis=0)
    core = lax.axis_index("c").astype(jnp.int32)
    place = jnp.stack([shard_j, core]).astype(jnp.int32)
    blocks = [_cast_into_block(weights[n][0], rows, place[0], "cast_" + n) for n, rows in zip(big, by_rows)]
    gathered = _all_gather_weights(blocks, [weights[n].shape[1:] for n in big], by_rows, small_local)
    full = dict(zip(big, gathered[:-1]))
    b_gate_full = gathered[-1][0:2].reshape(1, 2 * D_MODEL)
    conv_w_full = gathered[-1][2:6]
    pool_w_b, w_rg_b, w_ig_b = pool_w[0].astype(BF16), w_rg[0].astype(BF16), w_ig[0].astype(BF16)
    b_rg_row, b_ig_row = b_rg.reshape(1, D_MODEL), b_ig.reshape(1, D_MODEL)
    final_row = final_g.reshape(1, D_MODEL)

    zp, zl, zg, zt, u = _f1_in_proj(xs, norm1_g, full["w_in"])
    h1, hs, yp, yl = _f2_mixer(xs, zp, zl, zg, zt, b_gate_full, pool_w_b, pool_scale, full["pool_proj"], conv_w_full,
                               conv_b, w_rg_b, b_rg_row, w_ig_b, b_ig_row, lru_lambda, full["lru_proj"], full["w_out"])
    h2, v, ff, act = _f3_ffn(h1, norm2_g, full["w_ffn_in"], full["w_ffn_out"])

    loss_sum, dh2, g_ple_gate, g_ple_proj, vec4 = _b4_ple_loss(
        h2, ps, tgt, ple_norm_g, full["w_ple_gate"], full["w_ple_proj"], final_row)
    dff, dh1, vec3 = _b3_ffn(dh2, h1, ff, norm2_g, full["w_ffn_in"], full["w_ffn_out"])
    g_ffn_in = _wgrad(v, dff, 2 * D_FF // N_SHARDS, "wgrad_ffn_in")
    g_ffn_out = _wgrad(act, dh2, D_MODEL, "wgrad_ffn_out")
    dzt, dyp, dyl, g_w_out, vec_g = _b2_gates(dh1, zt, yp, yl, b_gate_full, full["w_out"])
    dzp, g_pool_proj, g_pool_w, vec_p = _b2_pool(dyp, zp, pool_w_b, pool_scale, full["pool_proj"])
    dzl, dzg, g_lru_proj, g_w_rg, g_w_ig, vec_l = _b2_lru(
        dyl, zl, zg, hs, conv_w_full, conv_b, w_rg_b, b_rg_row, w_ig_b, b_ig_row, lru_lambda, full["lru_proj"])
    grad_x, vec1 = _b1_in_proj(dzp, dzl, dzg, dzt, xs, dh1, norm1_g, full["w_in"])
    g_w_in = jnp.concatenate([
        _wgrad(u, dzp, POOL_WIDTH, "wgrad_in_pool"), _wgrad(u, dzl, D_MODEL, "wgrad_in_lru"),
        _wgrad(u, dzg, D_MODEL, "wgrad_in_gelu"), _wgrad(u, dzt, D_MODEL, "wgrad_in_gate")], axis=1)

    loss = lax.psum(loss_sum[0, 0] * (0.5 / D_MODEL), ("x", "y", "c"))

    big_grads = dict(w_in=g_w_in, pool_proj=g_pool_proj, lru_proj=g_lru_proj, w_out=g_w_out, w_ffn_in=g_ffn_in,
                     w_ffn_out=g_ffn_out, w_ple_gate=g_ple_gate, w_ple_proj=g_ple_proj)
    stacked = []
    for n, rows in zip(big, by_rows):
        g = big_grads[n]
        stacked.append(g.reshape(N_SHARDS, g.shape[0] // N_SHARDS, g.shape[1]) if rows else g[None])
    theirs = _core_exchange(stacked)
    chip_sums = [_sum_cores(g, t, core, "sum_cores_" + n) for g, t, n in zip(stacked, theirs, big)]
    slots = _chip_exchange(chip_sums, by_rows)
    reduced = _core_share([_sum_chips(s, q, rows, place, "sum_chips_" + n)
                           for s, q, rows, n in zip(chip_sums, slots, by_rows, big)])

    grads, deltas, new_m, new_v = {}, {}, {}, {}
    for n, r in zip(big, reduced):
        g = r.reshape(r.shape[0] * r.shape[1], r.shape[2])
        d, nm, nv = _adamw(weights[n][0], g, m_in[n][0], v_in[n][0], "adamw_" + n)
        grads[n], deltas[n], new_m[n], new_v[n] = g[None], d[None], nm[None], nv[None]

    small_full = dict(
        norm1_g=vec1[0], b_gate=vec_g[0:2], pool_w=g_pool_w, pool_scale=vec_p[0, :POOL_WIDTH],
        conv_w=vec_l[_V_CONVW:_V_CONVW + CONV_WIDTH], conv_b=vec_l[_V_CONVB], w_rg=g_w_rg, b_rg=vec_l[_V_BRG],
        w_ig=g_w_ig, b_ig=vec_l[_V_BIG], lru_lambda=vec_l[_V_LAM], norm2_g=vec3[0], ple_norm_g=vec4[1],
        final_g=vec4[0])
    full_shapes = [small_full[n].shape for n in small]
    n_full = sum(int(small_full[n].size) for n in small)
    rows_full = -(-n_full // (128 * ROW_TILE)) * ROW_TILE
    everyone = _exchange_small(_pack_rows([small_full[n] for n in small], rows_full))
    summed = dict(zip(small, _unpack_rows(_sum_slots(everyone, "sum_small"), full_shapes)))
    summed["b_gate"] = lax.dynamic_slice_in_dim(summed["b_gate"], shard_j * 256, 256, axis=1)
    summed["conv_w"] = lax.dynamic_slice_in_dim(summed["conv_w"], shard_j * 256, 256, axis=1)
    local_shapes = [weights[n].shape for n in small]
    n_local = sum(int(weights[n].size) for n in small)
    rows_local = -(-n_local // (128 * ROW_TILE)) * ROW_TILE
    packs = [_pack_rows([src[n] for n in small], rows_local) for src in (weights, summed, m_in, v_in)]
    d_s, nm_s, nv_s = _adamw(*packs, "adamw_small")
    for dst, pack in ((grads, packs[1]), (deltas, d_s), (new_m, nm_s), (new_v, nv_s)):
        dst.update(zip(small, _unpack_rows(pack, local_shapes)))

    return (loss, grad_x[None], *[grads[n] for n in names], *[deltas[n] for n in names],
            *[new_m[n] for n in names], *[new_v[n] for n in names])
```

```python
import functools

import jax
import jax.numpy as jnp
from jax import lax
from jax.experimental import pallas as pl
from jax.experimental.pallas import tpu as pltpu

F32 = jnp.float32
BF16 = jnp.bfloat16

D_MODEL = 1024
POOL_WIDTH = 512
POOL_GROUP_DIM = 128
POOL_WINDOWS = (2, 4, 8, 16)
POOL_HALO = 16
LRU_HEADS = 8
LRU_HEAD_DIM = 128
CONV_WIDTH = 4
LRU_C = 8.0
D_FF = 2816
PLE_DIM = 256
RMS_EPS = 1e-6
N_SHARDS = 4
N_DEV = 8

ADAM_LR = 0.001
ADAM_B1 = 0.9
ADAM_B2 = 0.999
ADAM_EPS = 1e-08
ADAM_WD = 0.01
ADAM_STEP = 10

ROW_TILE = 256
WIDE_TILE = 512
WGRAD_TOKENS = 2048
SUBLANES = 8
VMEM_LIMIT = 56 * 1024 * 1024
MESH = pl.DeviceIdType.MESH
ANY = pl.BlockSpec(memory_space=pl.ANY)


def _params(semantics=None):
    return pltpu.CompilerParams(dimension_semantics=semantics, vmem_limit_bytes=VMEM_LIMIT)


def _resident(shape):
    n = len(shape)
    return pl.BlockSpec(shape, lambda *_: (0,) * n, pipeline_mode=pl.Buffered(1))


def _acc(shape):
    n = len(shape)
    return pl.BlockSpec(shape, lambda *_: (0,) * n)


def _rows(tile, cols):
    return pl.BlockSpec((tile, cols), lambda i: (i, 0))


def _rows_rev(tile, cols, n_tiles):
    return pl.BlockSpec((tile, cols), lambda i: (n_tiles - 1 - i, 0))


def _halo_before(rows, cols, tile):
    per = tile // rows
    return pl.BlockSpec((rows, cols), lambda i: (jnp.maximum(i * per - 1, 0), 0))


def _halo_before_rev(rows, cols, tile, n_tiles):
    per = tile // rows
    return pl.BlockSpec((rows, cols), lambda i: (jnp.maximum((n_tiles - 1 - i) * per - 1, 0), 0))


def _nn(a, b):
    return jnp.dot(a, b, preferred_element_type=F32)


def _nt(a, b):
    return lax.dot_general(a, b, (((1,), (1,)), ((), ())), preferred_element_type=F32)


def _tn(a, b):
    return lax.dot_general(a, b, (((0,), (0,)), ((), ())), preferred_element_type=F32)


def _rms(x):
    r = lax.rsqrt(jnp.mean(x * x, axis=-1, keepdims=True) + RMS_EPS)
    return x * r, r


def _rms_bwd(dn, n, r):
    return r * (dn - n * jnp.mean(dn * n, axis=-1, keepdims=True))


def _sigmoid(x):
    return 0.5 * jnp.tanh(0.5 * x) + 0.5


_GELU_C = 0.7978845608028654
_GELU_A = 0.044715


def _gelu(x):
    t = jnp.tanh(_GELU_C * (x + _GELU_A * x * x * x))
    return 0.5 * x * (1.0 + t)


def _gelu_and_grad(x):
    x2 = x * x
    t = jnp.tanh(_GELU_C * (x + _GELU_A * x2 * x))
    cdf = 0.5 * (1.0 + t)
    grad = cdf + 0.5 * x * (1.0 - t * t) * _GELU_C * (1.0 + 3.0 * _GELU_A * x2)
    return x * cdf, grad


def _softplus_neg(lam):
    e = jnp.exp(-jnp.abs(lam))
    sp = jnp.maximum(-lam, 0.0) + jnp.log1p(e)
    return sp, -_sigmoid(-lam)


def _colsum(v):
    return jnp.sum(v, axis=0, keepdims=True)


def _row_ids(shape):
    return lax.broadcasted_iota(jnp.int32, shape, 0)


def _shift_down(cat, k):
    return pltpu.roll(cat, k, 0) if k else cat


def _shift_up(cat, k):
    return pltpu.roll(cat, cat.shape[0] - k, 0) if k else cat


def _f1_in_proj(x, norm1_g, w_in):
    T = x.shape[0]
    tm = ROW_TILE
    splits = (0, POOL_WIDTH, POOL_WIDTH + D_MODEL, POOL_WIDTH + 2 * D_MODEL, POOL_WIDTH + 4 * D_MODEL)

    def body(x_ref, g_ref, w_ref, zp_ref, zl_ref, zg_ref, zt_ref, u_ref):
        n, _ = _rms(x_ref[...])
        u = (n * g_ref[...]).astype(BF16)
        u_ref[...] = u
        for k, o_ref in enumerate((zp_ref, zl_ref, zg_ref, zt_ref)):
            o_ref[...] = _nn(u, w_ref[:, splits[k]:splits[k + 1]])

    widths = [splits[k + 1] - splits[k] for k in range(4)]
    return pl.pallas_call(
        body, name="f1_in_proj", grid=(T // tm,),
        in_specs=[_rows(tm, D_MODEL), _resident((1, D_MODEL)), _resident(w_in.shape)],
        out_specs=[_rows(tm, w) for w in widths] + [_rows(tm, D_MODEL)],
        out_shape=[jax.ShapeDtypeStruct((T, w), F32) for w in widths] + [jax.ShapeDtypeStruct((T, D_MODEL), BF16)],
        compiler_params=_params(("arbitrary",)),
    )(x, norm1_g, w_in)


def _pool_forward(zp_cat, pw_ref, first_row):
    tt = zp_cat.shape[0] - POOL_HALO
    t_glob = first_row + _row_ids((tt, POOL_GROUP_DIM))
    pooled, mixed = [], []
    for g, w in enumerate(POOL_WINDOWS):
        cat = zp_cat[:, g * POOL_GROUP_DIM:(g + 1) * POOL_GROUP_DIM]
        s, k = cat, 1
        while k < w:
            s = s + _shift_down(s, k)
            k *= 2
        cnt = jnp.minimum(t_glob + 1, w).astype(F32)
        pg = s[POOL_HALO:] / cnt - cat[POOL_HALO:]
        pooled.append(pg)
        mixed.append(_nn(pg.astype(BF16), pw_ref[g]))
    return jnp.concatenate(pooled, axis=1), jnp.concatenate(mixed, axis=1)


def _lru_gates(zl_cat, conv_w, conv_b, wrg_ref, brg, wig_ref, big, sp, first_row):
    tt = zl_cat.shape[0] - SUBLANES
    xc = conv_w[CONV_WIDTH - 1:CONV_WIDTH] * zl_cat
    for k in range(1, CONV_WIDTH):
        xc = xc + conv_w[CONV_WIDTH - 1 - k:CONV_WIDTH - k] * _shift_down(zl_cat, k)
    xc = xc[SUBLANES:] + conv_b
    xh = xc.astype(BF16)
    pr, pi = [], []
    for h in range(LRU_HEADS):
        xs = xh[:, h * LRU_HEAD_DIM:(h + 1) * LRU_HEAD_DIM]
        pr.append(_nn(xs, wrg_ref[h]))
        pi.append(_nn(xs, wig_ref[h]))
    r = _sigmoid(jnp.concatenate(pr, axis=1) + brg)
    ig = _sigmoid(jnp.concatenate(pi, axis=1) + big)
    a = jnp.exp(-LRU_C * r * sp)
    mult = jnp.sqrt(jnp.maximum(1.0 - a * a, 0.0))
    t_glob = first_row + _row_ids((tt, D_MODEL))
    mult = jnp.where(t_glob == 0, 1.0, mult)
    return xc, r, ig, a, mult


def _f2_mixer(x, zp, zl, zg, zt, b_gate, pool_w, pool_scale, pool_proj, conv_w, conv_b, w_rg, b_rg, w_ig, b_ig,
              lru_lambda, lru_proj, w_out):
    T = x.shape[0]
    tt = ROW_TILE
    n_groups = tt // SUBLANES

    def body(x_ref, zp_ref, zph_ref, zl_ref, zlh_ref, zg_ref, zt_ref, bg_ref, pw_ref, ps_ref, pp_ref, cw_ref, cb_ref,
             wrg_ref, brg_ref, wig_ref, big_ref, lam_ref, lp_ref, wo_ref,
             h1_ref, hs_ref, yp_ref, yl_ref, a_s, b_s, carry_s):
        i = pl.program_id(0)
        first_row = i * tt
        keep = (i > 0).astype(F32)

        zp_cat = jnp.concatenate([zph_ref[...] * keep, zp_ref[...]], axis=0)
        _, mixed = _pool_forward(zp_cat, pw_ref, first_row)
        y_pool = _nn((mixed * ps_ref[...]).astype(BF16), pp_ref[...])

        sp, _ = _softplus_neg(lam_ref[...])
        zl_cat = jnp.concatenate([zlh_ref[...] * keep, zl_ref[...]], axis=0)
        xc, _, ig, a, mult = _lru_gates(zl_cat, cw_ref[...], cb_ref[...], wrg_ref, brg_ref[...], wig_ref,
                                        big_ref[...], sp, first_row)
        a_s[...] = a
        b_s[...] = mult * ig * xc

        @pl.when(i == 0)
        def _():
            carry_s[...] = jnp.zeros_like(carry_s)

        rows8 = _row_ids((SUBLANES, D_MODEL))

        def group(g, carry):
            at = pl.ds(pl.multiple_of(g * SUBLANES, SUBLANES), SUBLANES)
            A, B = a_s[at, :], b_s[at, :]
            for s in (1, 2, 4):
                m = rows8 >= s
                B = jnp.where(m, A * pltpu.roll(B, s, 0) + B, B)
                A = jnp.where(m, A * pltpu.roll(A, s, 0), A)
            h = A * carry + B
            hs_ref[at, :] = h
            return jnp.broadcast_to(h[SUBLANES - 1:SUBLANES, :], (SUBLANES, D_MODEL))

        carry_s[...] = lax.fori_loop(0, n_groups, group, carry_s[...])
        y_lru = _nn((hs_ref[...] * _gelu(zg_ref[...])).astype(BF16), lp_ref[...])

        gates = _sigmoid(zt_ref[...] + bg_ref[...])
        merged = gates[:, :D_MODEL] * y_pool + gates[:, D_MODEL:] * y_lru
        h1_ref[...] = x_ref[...] + _nn(merged.astype(BF16), wo_ref[...])
        yp_ref[...] = y_pool.astype(BF16)
        yl_ref[...] = y_lru.astype(BF16)

    res = [pool_w, pool_scale, pool_proj, conv_w, conv_b, w_rg, b_rg, w_ig, b_ig, lru_lambda, lru_proj, w_out]
    return pl.pallas_call(
        body, name="f2_mixer", grid=(T // tt,),
        in_specs=[_rows(tt, D_MODEL), _rows(tt, POOL_WIDTH), _halo_before(POOL_HALO, POOL_WIDTH, tt),
                  _rows(tt, D_MODEL), _halo_before(SUBLANES, D_MODEL, tt), _rows(tt, D_MODEL), _rows(tt, 2 * D_MODEL),
                  _resident(b_gate.shape)] + [_resident(w.shape) for w in res],
        out_specs=[_rows(tt, D_MODEL)] * 4,
        out_shape=[jax.ShapeDtypeStruct((T, D_MODEL), F32), jax.ShapeDtypeStruct((T, D_MODEL), F32),
                   jax.ShapeDtypeStruct((T, D_MODEL), BF16), jax.ShapeDtypeStruct((T, D_MODEL), BF16)],
        scratch_shapes=[pltpu.VMEM((tt, D_MODEL), F32), pltpu.VMEM((tt, D_MODEL), F32),
                        pltpu.VMEM((SUBLANES, D_MODEL), F32)],
        compiler_params=_params(("arbitrary",)),
    )(x, zp, zp, zl, zl, zg, zt, b_gate, *res)


def _f3_ffn(h1, norm2_g, w_ffn_in, w_ffn_out):
    T = h1.shape[0]
    tm = ROW_TILE

    def body(h_ref, g_ref, wi_ref, wo_ref, h2_ref, v_ref, ff_ref, act_ref):
        h = h_ref[...]
        n, _ = _rms(h)
        v = (n * g_ref[...]).astype(BF16)
        v_ref[...] = v
        g_ff = _nn(v, wi_ref[:, :D_FF])
        u_ff = _nn(v, wi_ref[:, D_FF:])
        ff_ref[:, :D_FF] = g_ff.astype(BF16)
        ff_ref[:, D_FF:] = u_ff.astype(BF16)
        act = (g_ff * _sigmoid(g_ff) * u_ff).astype(BF16)
        act_ref[...] = act
        h2_ref[...] = h + _nn(act, wo_ref[...])

    return pl.pallas_call(
        body, name="f3_ffn", grid=(T // tm,),
        in_specs=[_rows(tm, D_MODEL), _resident((1, D_MODEL)), _resident(w_ffn_in.shape), _resident(w_ffn_out.shape)],
        out_specs=[_rows(tm, D_MODEL), _rows(tm, D_MODEL), _rows(tm, 2 * D_FF), _rows(tm, D_FF)],
        out_shape=[jax.ShapeDtypeStruct((T, D_MODEL), F32), jax.ShapeDtypeStruct((T, D_MODEL), BF16),
                   jax.ShapeDtypeStruct((T, 2 * D_FF), BF16), jax.ShapeDtypeStruct((T, D_FF), BF16)],
        compiler_params=_params(("arbitrary",)),
    )(h1, norm2_g, w_ffn_in, w_ffn_out)


def _b4_ple_loss(h2, p, target, ple_norm_g, w_ple_gate, w_ple_proj, final_g):
    T = h2.shape[0]
    tm = WIDE_TILE

    def body(h_ref, p_ref, t_ref, gp_ref, wg_ref, wp_ref, gf_ref, loss_ref, dh2_ref, dwg_ref, dwp_ref, vec_ref):
        @pl.when(pl.program_id(0) == 0)
        def _():
            loss_ref[...] = jnp.zeros_like(loss_ref)
            dwg_ref[...] = jnp.zeros_like(dwg_ref)
            dwp_ref[...] = jnp.zeros_like(dwp_ref)
            vec_ref[...] = jnp.zeros_like(vec_ref)

        h2v = h_ref[...]
        n3, r3 = _rms(h2v)
        n3g = (n3 * gp_ref[...]).astype(BF16)
        pg = _sigmoid(_nn(n3g, wg_ref[...]))
        pb = p_ref[...].astype(BF16)
        e = _nn(pb, wp_ref[...])
        h3 = h2v + pg * e
        n4, r4 = _rms(h3)
        diff = n4 * gf_ref[...] - t_ref[...]
        loss_ref[...] += jnp.sum(diff * diff).reshape(1, 1)
        dy = diff * (1.0 / D_MODEL)
        vec_ref[0:1, :] += _colsum(dy * n4)
        dh3 = _rms_bwd(dy * gf_ref[...], n4, r4)
        dwp_ref[...] += _tn(pb, (dh3 * pg).astype(BF16))
        dq = (dh3 * e * pg * (1.0 - pg)).astype(BF16)
        dwg_ref[...] += _tn(n3g, dq)
        dn3g = _nt(dq, wg_ref[...])
        vec_ref[1:2, :] += _colsum(dn3g * n3)
        dh2_ref[...] = dh3 + _rms_bwd(dn3g * gp_ref[...], n3, r3)

    return pl.pallas_call(
        body, name="b4_ple_loss", grid=(T // tm,),
        in_specs=[_rows(tm, D_MODEL), _rows(tm, PLE_DIM), _rows(tm, D_MODEL), _resident((1, D_MODEL)),
                  _resident(w_ple_gate.shape), _resident(w_ple_proj.shape), _resident((1, D_MODEL))],
        out_specs=[_acc((1, 1)), _rows(tm, D_MODEL), _acc(w_ple_gate.shape), _acc(w_ple_proj.shape),
                   _acc((SUBLANES, D_MODEL))],
        out_shape=[jax.ShapeDtypeStruct((1, 1), F32), jax.ShapeDtypeStruct((T, D_MODEL), F32),
                   jax.ShapeDtypeStruct(w_ple_gate.shape, F32), jax.ShapeDtypeStruct(w_ple_proj.shape, F32),
                   jax.ShapeDtypeStruct((SUBLANES, D_MODEL), F32)],
        compiler_params=_params(("arbitrary",)),
    )(h2, p, target, ple_norm_g, w_ple_gate, w_ple_proj, final_g)


def _b3_ffn(dh2, h1, ff, norm2_g, w_ffn_in, w_ffn_out):
    T = h1.shape[0]
    tm = ROW_TILE

    def body(d_ref, h_ref, ff_ref, g_ref, wi_ref, wo_ref, dff_ref, dh1_ref, vec_ref):
        @pl.when(pl.program_id(0) == 0)
        def _():
            vec_ref[...] = jnp.zeros_like(vec_ref)

        dh2v = d_ref[...]
        dact = _nt(dh2v.astype(BF16), wo_ref[...])
        g_ff = ff_ref[:, :D_FF].astype(F32)
        u_ff = ff_ref[:, D_FF:].astype(F32)
        s = _sigmoid(g_ff)
        dg = (dact * u_ff * (s * (1.0 + g_ff * (1.0 - s)))).astype(BF16)
        du = (dact * (g_ff * s)).astype(BF16)
        dff_ref[:, :D_FF] = dg
        dff_ref[:, D_FF:] = du
        dv = _nt(dg, wi_ref[:, :D_FF]) + _nt(du, wi_ref[:, D_FF:])
        n2, r2 = _rms(h_ref[...])
        vec_ref[0:1, :] += _colsum(dv * n2)
        dh1_ref[...] = dh2v + _rms_bwd(dv * g_ref[...], n2, r2)

    return pl.pallas_call(
        body, name="b3_ffn", grid=(T // tm,),
        in_specs=[_rows(tm, D_MODEL), _rows(tm, D_MODEL), _rows(tm, 2 * D_FF), _resident((1, D_MODEL)),
                  _resident(w_ffn_in.shape), _resident(w_ffn_out.shape)],
        out_specs=[_rows(tm, 2 * D_FF), _rows(tm, D_MODEL), _acc((SUBLANES, D_MODEL))],
        out_shape=[jax.ShapeDtypeStruct((T, 2 * D_FF), BF16), jax.ShapeDtypeStruct((T, D_MODEL), F32),
                   jax.ShapeDtypeStruct((SUBLANES, D_MODEL), F32)],
        compiler_params=_params(("arbitrary",)),
    )(dh2, h1, ff, norm2_g, w_ffn_in, w_ffn_out)


def _wgrad(a, b, col_tile, name, tokens=WGRAD_TOKENS):
    T, K = a.shape
    N = b.shape[1]
    tk = min(T, tokens)

    def body(a_ref, b_ref, o_ref):
        @pl.when(pl.program_id(1) == 0)
        def _():
            o_ref[...] = jnp.zeros_like(o_ref)

        o_ref[...] += _tn(a_ref[...].astype(BF16), b_ref[...].astype(BF16))

    return pl.pallas_call(
        body, name=name, grid=(N // col_tile, T // tk),
        in_specs=[pl.BlockSpec((tk, K), lambda j, k: (k, 0)), pl.BlockSpec((tk, col_tile), lambda j, k: (k, j))],
        out_specs=pl.BlockSpec((K, col_tile), lambda j, k: (0, j)),
        out_shape=jax.ShapeDtypeStruct((K, N), F32),
        compiler_params=_params(("arbitrary", "arbitrary")),
    )(a, b)


def _b2_gates(dh1, zt, yp, yl, b_gate, w_out):
    T = dh1.shape[0]
    tm = WIDE_TILE

    def body(d_ref, zt_ref, yp_ref, yl_ref, bg_ref, wo_ref, dzt_ref, dyp_ref, dyl_ref, dwo_ref, vec_ref):
        @pl.when(pl.program_id(0) == 0)
        def _():
            dwo_ref[...] = jnp.zeros_like(dwo_ref)
            vec_ref[...] = jnp.zeros_like(vec_ref)

        db = d_ref[...].astype(BF16)
        dm = _nt(db, wo_ref[...])
        gates = _sigmoid(zt_ref[...] + bg_ref[...])
        g0, g1 = gates[:, :D_MODEL], gates[:, D_MODEL:]
        y_pool, y_lru = yp_ref[...].astype(F32), yl_ref[...].astype(F32)
        dwo_ref[...] += _tn((g0 * y_pool + g1 * y_lru).astype(BF16), db)
        dz0 = dm * y_pool * g0 * (1.0 - g0)
        dz1 = dm * y_lru * g1 * (1.0 - g1)
        vec_ref[0:1, :] += _colsum(dz0)
        vec_ref[1:2, :] += _colsum(dz1)
        dzt_ref[:, :D_MODEL] = dz0.astype(BF16)
        dzt_ref[:, D_MODEL:] = dz1.astype(BF16)
        dyp_ref[...] = (dm * g0).astype(BF16)
        dyl_ref[...] = (dm * g1).astype(BF16)

    return pl.pallas_call(
        body, name="b2_gates", grid=(T // tm,),
        in_specs=[_rows(tm, D_MODEL), _rows(tm, 2 * D_MODEL), _rows(tm, D_MODEL), _rows(tm, D_MODEL),
                  _resident(b_gate.shape), _resident(w_out.shape)],
        out_specs=[_rows(tm, 2 * D_MODEL), _rows(tm, D_MODEL), _rows(tm, D_MODEL), _acc(w_out.shape),
                   _acc((SUBLANES, D_MODEL))],
        out_shape=[jax.ShapeDtypeStruct((T, 2 * D_MODEL), BF16), jax.ShapeDtypeStruct((T, D_MODEL), BF16),
                   jax.ShapeDtypeStruct((T, D_MODEL), BF16), jax.ShapeDtypeStruct(w_out.shape, F32),
                   jax.ShapeDtypeStruct((SUBLANES, D_MODEL), F32)],
        compiler_params=_params(("arbitrary",)),
    )(dh1, zt, yp, yl, b_gate, w_out)


def _b2_pool(dyp, zp, pool_w, pool_scale, pool_proj):
    T = zp.shape[0]
    tt = ROW_TILE
    nt = T // tt

    def body(dy_ref, zp_ref, zph_ref, pw_ref, ps_ref, pp_ref, dzp_ref, dpp_ref, dpw_ref, vec_ref, q_next):
        i = pl.program_id(0)
        ti = nt - 1 - i
        first_row = ti * tt

        @pl.when(i == 0)
        def _():
            dpp_ref[...] = jnp.zeros_like(dpp_ref)
            dpw_ref[...] = jnp.zeros_like(dpw_ref)
            vec_ref[...] = jnp.zeros_like(vec_ref)
            q_next[...] = jnp.zeros_like(q_next)

        keep = (ti > 0).astype(F32)
        zp_cat = jnp.concatenate([zph_ref[...] * keep, zp_ref[...]], axis=0)
        pooled, mixed = _pool_forward(zp_cat, pw_ref, first_row)
        dy = dy_ref[...]
        dpp_ref[...] += _tn((mixed * ps_ref[...]).astype(BF16), dy)
        dms = _nt(dy, pp_ref[...])
        vec_ref[0:1, :POOL_WIDTH] += _colsum(dms * mixed)
        dmixed = (dms * ps_ref[...]).astype(BF16)
        t_glob = first_row + _row_ids((tt, POOL_GROUP_DIM))
        dz, q_all = [], []
        for g, w in enumerate(POOL_WINDOWS):
            cols = slice(g * POOL_GROUP_DIM, (g + 1) * POOL_GROUP_DIM)
            dpw_ref[g] += _tn(pooled[:, cols].astype(BF16), dmixed[:, cols])
            dpooled = _nt(dmixed[:, cols], pw_ref[g])
            q = dpooled / jnp.minimum(t_glob + 1, w).astype(F32)
            q_all.append(q)
            s, k = jnp.concatenate([q, q_next[:, cols]], axis=0), 1
            while k < w:
                s = s + _shift_up(s, k)
                k *= 2
            dz.append(s[:tt] - dpooled)
        dzp_ref[...] = jnp.concatenate(dz, axis=1).astype(BF16)
        q_next[...] = jnp.concatenate([q[:POOL_HALO] for q in q_all], axis=1)

    return pl.pallas_call(
        body, name="b2_pool", grid=(nt,),
        in_specs=[_rows_rev(tt, D_MODEL, nt), _rows_rev(tt, POOL_WIDTH, nt),
                  _halo_before_rev(POOL_HALO, POOL_WIDTH, tt, nt),
                  _resident(pool_w.shape), _resident(pool_scale.shape), _resident(pool_proj.shape)],
        out_specs=[_rows_rev(tt, POOL_WIDTH, nt), _acc(pool_proj.shape), _acc(pool_w.shape), _acc((SUBLANES, D_MODEL))],
        out_shape=[jax.ShapeDtypeStruct((T, POOL_WIDTH), BF16), jax.ShapeDtypeStruct(pool_proj.shape, F32),
                   jax.ShapeDtypeStruct(pool_w.shape, F32), jax.ShapeDtypeStruct((SUBLANES, D_MODEL), F32)],
        scratch_shapes=[pltpu.VMEM((POOL_HALO, POOL_WIDTH), F32)],
        compiler_params=_params(("arbitrary",)),
    )(dyp, zp, zp, pool_w, pool_scale, pool_proj)


_V_CONVW, _V_CONVB, _V_BRG, _V_BIG, _V_LAM = 0, 4, 5, 6, 7


def _b2_lru(dyl, zl, zg, hs, conv_w, conv_b, w_rg, b_rg, w_ig, b_ig, lru_lambda, lru_proj):
    T = zl.shape[0]
    tt = ROW_TILE
    nt = T // tt
    n_groups = tt // SUBLANES

    def body(dy_ref, zl_ref, zlh_ref, zg_ref, hs_ref, hsh_ref, cw_ref, cb_ref, wrg_ref, brg_ref, wig_ref, big_ref,
             lam_ref, lp_ref, dzl_ref, dzg_ref, dlp_ref, dwrg_ref, dwig_ref, vec_ref,
             c_s, d_s, g_s, g_next, a_next, dxc_next):
        i = pl.program_id(0)
        ti = nt - 1 - i
        first_row = ti * tt

        @pl.when(i == 0)
        def _():
            dlp_ref[...] = jnp.zeros_like(dlp_ref)
            dwrg_ref[...] = jnp.zeros_like(dwrg_ref)
            dwig_ref[...] = jnp.zeros_like(dwig_ref)
            vec_ref[...] = jnp.zeros_like(vec_ref)
            g_next[...] = jnp.zeros_like(g_next)
            a_next[...] = jnp.zeros_like(a_next)
            dxc_next[...] = jnp.zeros_like(dxc_next)

        keep = (ti > 0).astype(F32)
        sp, dsp_dlam = _softplus_neg(lam_ref[...])
        cw = cw_ref[...]
        zl_cat = jnp.concatenate([zlh_ref[...] * keep, zl_ref[...]], axis=0)
        xc, r, ig, a, mult = _lru_gates(zl_cat, cw, cb_ref[...], wrg_ref, brg_ref[...], wig_ref, big_ref[...], sp,
                                        first_row)
        hs = hs_ref[...]
        gelu, dgelu = _gelu_and_grad(zg_ref[...])
        dy = dy_ref[...]
        dlp_ref[...] += _tn((hs * gelu).astype(BF16), dy)
        dyl = _nt(dy, lp_ref[...])
        dzg_ref[...] = (dyl * hs * dgelu).astype(BF16)

        d_s[...] = dyl * gelu
        c_s[...] = _shift_up(jnp.concatenate([a, a_next[...]], axis=0), 1)[:tt]
        rows8 = _row_ids((SUBLANES, D_MODEL))

        def group(k, carry):
            at = pl.ds(pl.multiple_of((n_groups - 1 - k) * SUBLANES, SUBLANES), SUBLANES)
            C, Dv = c_s[at, :], d_s[at, :]
            for s in (1, 2, 4):
                m = rows8 < SUBLANES - s
                Dv = jnp.where(m, C * pltpu.roll(Dv, SUBLANES - s, 0) + Dv, Dv)
                C = jnp.where(m, C * pltpu.roll(C, SUBLANES - s, 0), C)
            G = C * carry + Dv
            g_s[at, :] = G
            return jnp.broadcast_to(G[0:1, :], (SUBLANES, D_MODEL))

        g_next[...] = lax.fori_loop(0, n_groups, group, g_next[...])
        a_next[...] = jnp.broadcast_to(a[0:1, :], (SUBLANES, D_MODEL))
        G = g_s[...]

        h_prev = _shift_down(jnp.concatenate([hsh_ref[...] * keep, hs], axis=0), 1)[SUBLANES:]
        t_glob = first_row + _row_ids((tt, D_MODEL))
        dmult = jnp.where(t_glob == 0, 0.0, G * ig * xc)
        dla = G * h_prev * a - dmult * (a * a) / mult
        vec_ref[_V_LAM:_V_LAM + 1, :] += _colsum(dla * r) * (-LRU_C) * dsp_dlam
        dpr = dla * (-LRU_C) * sp * r * (1.0 - r)
        dpi = G * mult * xc * ig * (1.0 - ig)
        vec_ref[_V_BRG:_V_BRG + 1, :] += _colsum(dpr)
        vec_ref[_V_BIG:_V_BIG + 1, :] += _colsum(dpi)
        dprb, dpib, xh = dpr.astype(BF16), dpi.astype(BF16), xc.astype(BF16)
        dxc_h = []
        for h in range(LRU_HEADS):
            cols = slice(h * LRU_HEAD_DIM, (h + 1) * LRU_HEAD_DIM)
            dwrg_ref[h] += _tn(xh[:, cols], dprb[:, cols])
            dwig_ref[h] += _tn(xh[:, cols], dpib[:, cols])
            dxc_h.append(_nt(dprb[:, cols], wrg_ref[h]) + _nt(dpib[:, cols], wig_ref[h]))
        dxc = G * mult * ig + jnp.concatenate(dxc_h, axis=1)

        vec_ref[_V_CONVB:_V_CONVB + 1, :] += _colsum(dxc)
        dxc_cat = jnp.concatenate([dxc, dxc_next[...]], axis=0)
        dzl = cw[CONV_WIDTH - 1:CONV_WIDTH] * dxc
        for k in range(CONV_WIDTH):
            lag = CONV_WIDTH - 1 - k
            vec_ref[_V_CONVW + k:_V_CONVW + k + 1, :] += _colsum(dxc * _shift_down(zl_cat, lag)[SUBLANES:])
            if lag:
                dzl = dzl + cw[k:k + 1] * _shift_up(dxc_cat, lag)[:tt]
        dzl_ref[...] = dzl.astype(BF16)
        dxc_next[...] = dxc[:SUBLANES]

    res = [conv_w, conv_b, w_rg, b_rg, w_ig, b_ig, lru_lambda, lru_proj]
    return pl.pallas_call(
        body, name="b2_lru", grid=(nt,),
        in_specs=[_rows_rev(tt, D_MODEL, nt), _rows_rev(tt, D_MODEL, nt), _halo_before_rev(SUBLANES, D_MODEL, tt, nt),
                  _rows_rev(tt, D_MODEL, nt), _rows_rev(tt, D_MODEL, nt), _halo_before_rev(SUBLANES, D_MODEL, tt, nt)]
        + [_resident(w.shape) for w in res],
        out_specs=[_rows_rev(tt, D_MODEL, nt), _rows_rev(tt, D_MODEL, nt), _acc(lru_proj.shape), _acc(w_rg.shape),
                   _acc(w_ig.shape), _acc((SUBLANES, D_MODEL))],
        out_shape=[jax.ShapeDtypeStruct((T, D_MODEL), BF16), jax.ShapeDtypeStruct((T, D_MODEL), BF16),
                   jax.ShapeDtypeStruct(lru_proj.shape, F32), jax.ShapeDtypeStruct(w_rg.shape, F32),
                   jax.ShapeDtypeStruct(w_ig.shape, F32), jax.ShapeDtypeStruct((SUBLANES, D_MODEL), F32)],
        scratch_shapes=[pltpu.VMEM((tt, D_MODEL), F32)] * 3 + [pltpu.VMEM((SUBLANES, D_MODEL), F32)] * 3,
        compiler_params=_params(("arbitrary",)),
    )(dyl, zl, zl, zg, hs, hs, *res)


def _b1_in_proj(dzp, dzl, dzg, dzt, x, dh1, norm1_g, w_in):
    T = x.shape[0]
    tm = ROW_TILE
    splits = (0, POOL_WIDTH, POOL_WIDTH + D_MODEL, POOL_WIDTH + 2 * D_MODEL, POOL_WIDTH + 4 * D_MODEL)

    def body(a_ref, b_ref, c_ref, d_ref, x_ref, dh_ref, g_ref, w_ref, dx_ref, vec_ref):
        @pl.when(pl.program_id(0) == 0)
        def _():
            vec_ref[...] = jnp.zeros_like(vec_ref)

        du = None
        for k, dz_ref in enumerate((a_ref, b_ref, c_ref, d_ref)):
            part = _nt(dz_ref[...], w_ref[:, splits[k]:splits[k + 1]])
            du = part if du is None else du + part
        n1, r1 = _rms(x_ref[...])
        vec_ref[0:1, :] += _colsum(du * n1)
        dx_ref[...] = dh_ref[...] + _rms_bwd(du * g_ref[...], n1, r1)

    return pl.pallas_call(
        body, name="b1_in_proj", grid=(T // tm,),
        in_specs=[_rows(tm, POOL_WIDTH), _rows(tm, D_MODEL), _rows(tm, D_MODEL), _rows(tm, 2 * D_MODEL),
                  _rows(tm, D_MODEL), _rows(tm, D_MODEL), _resident((1, D_MODEL)), _resident(w_in.shape)],
        out_specs=[_rows(tm, D_MODEL), _acc((SUBLANES, D_MODEL))],
        out_shape=[jax.ShapeDtypeStruct((T, D_MODEL), F32), jax.ShapeDtypeStruct((SUBLANES, D_MODEL), F32)],
        compiler_params=_params(("arbitrary",)),
    )(dzp, dzl, dzg, dzt, x, dh1, norm1_g, w_in)


def _row_tile(rows):
    for t in (512, 256, 128, 64, 32, 16, 8):
        if rows % t == 0:
            return t
    return rows


def _scalar_grid(grid, in_specs, out_specs):
    return pltpu.PrefetchScalarGridSpec(num_scalar_prefetch=1, grid=grid, in_specs=in_specs, out_specs=out_specs)


def _cast_into_block(w, by_rows, shard_j, name):
    R, C = w.shape
    tr = _row_tile(R)
    if by_rows:
        out_shape, out_map = (N_SHARDS * R, C), lambda i, j: (j[0] * (R // tr) + i, 0)
    else:
        out_shape, out_map = (R, N_SHARDS * C), lambda i, j: (i, j[0])

    def body(j_ref, w_ref, o_ref):
        o_ref[...] = w_ref[...].astype(BF16)

    return pl.pallas_call(
        body, name=name,
        grid_spec=_scalar_grid((R // tr,), [pl.BlockSpec((tr, C), lambda i, j: (i, 0))], pl.BlockSpec((tr, C), out_map)),
        out_shape=jax.ShapeDtypeStruct(out_shape, BF16),
        compiler_params=_params(("arbitrary",)),
    )(shard_j.reshape(1), w)


def _sum_cores(g, theirs, core, name):
    S, R, C = g.shape
    H = R // 2
    tr = _row_tile(H)
    nh = H // tr

    def body(c_ref, g_ref, t_ref, o_ref):
        o_ref[...] = (g_ref[...] + t_ref[...]).astype(BF16)

    half = pl.BlockSpec((None, tr, C), lambda s, i, c: (s, i, 0))
    return pl.pallas_call(
        body, name=name,
        grid_spec=_scalar_grid((S, nh), [pl.BlockSpec((None, tr, C), lambda s, i, c: (s, c[0] * nh + i, 0)), half], half),
        out_shape=jax.ShapeDtypeStruct((S, H, C), BF16),
        compiler_params=_params(("arbitrary", "arbitrary")),
    )(core.reshape(1), g, theirs)


def _sum_chips(sums, slots, by_rows, place, name):
    _, H, C = slots.shape
    tr = _row_tile(H)
    own_map = (lambda i, p: (p[0], i, 0)) if by_rows else (lambda i, p: (0, i, p[0]))

    def body(p_ref, s_ref, q_ref, o_ref):
        o_ref[...] = ((s_ref[...].astype(F32) + q_ref[0].astype(F32)) + q_ref[1].astype(F32)) + q_ref[2].astype(F32)

    return pl.pallas_call(
        body, name=name,
        grid_spec=_scalar_grid(
            (H // tr,),
            [pl.BlockSpec((None, tr, C), own_map), pl.BlockSpec((3, tr, C), lambda i, p: (0, i, 0))],
            pl.BlockSpec((None, tr, C), lambda i, p: (p[1], i, 0))),
        out_shape=jax.ShapeDtypeStruct((2, H, C), F32),
        compiler_params=_params(("arbitrary",)),
    )(place, sums, slots)


def _sum_slots(q, name):
    S, R, C = q.shape
    tr = _row_tile(R)

    def body(q_ref, o_ref):
        acc = q_ref[0]
        for s in range(1, S):
            acc = acc + q_ref[s]
        o_ref[...] = acc

    return pl.pallas_call(
        body, name=name, grid=(R // tr,),
        in_specs=[pl.BlockSpec((S, tr, C), lambda i: (0, i, 0))], out_specs=_rows(tr, C),
        out_shape=jax.ShapeDtypeStruct((R, C), F32),
        compiler_params=_params(("arbitrary",)),
    )(q)


def _adamw(w, g, m, v, name):
    R, C = w.shape
    tr = _row_tile(R)
    c1 = 1.0 - ADAM_B1 ** ADAM_STEP
    c2 = 1.0 - ADAM_B2 ** ADAM_STEP

    def body(w_ref, g_ref, m_ref, v_ref, d_ref, nm_ref, nv_ref):
        gv = g_ref[...]
        nm = ADAM_B1 * m_ref[...] + (1.0 - ADAM_B1) * gv
        nv = ADAM_B2 * v_ref[...] + (1.0 - ADAM_B2) * (gv * gv)
        d_ref[...] = -ADAM_LR * ((nm / c1) / (jnp.sqrt(nv / c2) + ADAM_EPS) + ADAM_WD * w_ref[...])
        nm_ref[...] = nm
        nv_ref[...] = nv

    return pl.pallas_call(
        body, name=name, grid=(R // tr,),
        in_specs=[_rows(tr, C)] * 4, out_specs=[_rows(tr, C)] * 3,
        out_shape=[jax.ShapeDtypeStruct((R, C), F32)] * 3,
        compiler_params=_params(("arbitrary",)),
    )(w, g, m, v)


def _place():
    return lax.axis_index("x"), lax.axis_index("y"), lax.axis_index("c")


def _other_chips(x, y):
    return [(1 - x, y), (x, 1 - y), (1 - x, 1 - y)]


def _shard_block(ref, by_rows, R, C, j, half_rows=None):
    if half_rows is None:
        rows, r0 = R, 0
    else:
        rows = R // 2
        r0 = pl.multiple_of(half_rows * rows, 16)
    if by_rows:
        return ref.at[pl.ds(pl.multiple_of(j * R, 16) + r0, rows), :]
    return ref.at[pl.ds(r0, rows), pl.ds(pl.multiple_of(j * C, 128), C)]


def _all_gather_weights(gathered, shapes, by_rows, small):
    n = len(gathered)

    def body(*refs):
        small_in = refs[n]
        outs, small_out = refs[n + 1:2 * n + 1], refs[2 * n + 1]
        send_sems, recv_sems, local_sem = refs[2 * n + 2:]
        x, y, c = _place()
        me_j = 2 * x + y
        chips = _other_chips(x, y)
        sibling = (x, y, 1 - c)

        def block(i, j, half):
            R, C = shapes[i]
            return _shard_block(outs[i], by_rows[i], R, C, j, half)

        def ici(i, k, src_j):
            return pltpu.make_async_remote_copy(
                src_ref=block(i, src_j, c), dst_ref=block(i, src_j, c),
                send_sem=send_sems.at[6 * i + k], recv_sem=recv_sems.at[6 * i + k],
                device_id=(*chips[k], c), device_id_type=MESH)

        def relay(i, k, half):
            kj = 2 * chips[k][0] + chips[k][1]
            return pltpu.make_async_remote_copy(
                src_ref=block(i, kj, half), dst_ref=block(i, kj, half),
                send_sem=send_sems.at[6 * i + 3 + k], recv_sem=recv_sems.at[6 * i + 3 + k],
                device_id=sibling, device_id_type=MESH)

        def small_copy(k, src_j):
            cols = pl.ds(pl.multiple_of(src_j * 256, 128), 256)
            return pltpu.make_async_remote_copy(
                src_ref=small_in, dst_ref=small_out.at[:, cols],
                send_sem=send_sems.at[6 * n + k], recv_sem=recv_sems.at[6 * n + k],
                device_id=(*chips[k], c), device_id_type=MESH)

        sends = []
        for i in range(n):
            for k in range(3):
                cp = ici(i, k, me_j)
                cp.start()
                sends.append(cp)
        for k in range(3):
            cp = small_copy(k, me_j)
            cp.start()
            sends.append(cp)
        local = pltpu.make_async_copy(small_in, small_out.at[:, pl.ds(pl.multiple_of(me_j * 256, 128), 256)], local_sem)
        local.start()
        for i in range(n):
            for k in range(3):
                kj = 2 * chips[k][0] + chips[k][1]
                ici(i, k, kj).wait_recv()
                cp = relay(i, k, c)
                cp.start()
                sends.append(cp)
        for k in range(3):
            small_copy(k, 2 * chips[k][0] + chips[k][1]).wait_recv()
        for i in range(n):
            for k in range(3):
                relay(i, k, 1 - c).wait_recv()
        for cp in sends:
            cp.wait_send()
        local.wait()

    out_shape = [jax.ShapeDtypeStruct(g.shape, BF16) for g in gathered]
    out_shape.append(jax.ShapeDtypeStruct((8, N_SHARDS * 256), F32))
    n_sems = 6 * n + 3
    return pl.pallas_call(
        body, name="all_gather_weights",
        in_specs=[ANY] * (n + 1), out_specs=[ANY] * (n + 1), out_shape=out_shape,
        input_output_aliases={i: i for i in range(n)},
        scratch_shapes=[pltpu.SemaphoreType.DMA((n_sems,)), pltpu.SemaphoreType.DMA((n_sems,)),
                        pltpu.SemaphoreType.DMA],
    )(*gathered, small)


def _core_exchange(grads):
    n = len(grads)

    def body(*refs):
        ins, theirs = refs[:n], refs[n:2 * n]
        send_sems, recv_sems = refs[2 * n:]
        x, y, c = _place()
        copies = []
        for i in range(n):
            H = grads[i].shape[1] // 2
            cp = pltpu.make_async_remote_copy(
                src_ref=ins[i].at[:, pl.ds(pl.multiple_of((1 - c) * H, 8), H), :], dst_ref=theirs[i],
                send_sem=send_sems.at[i], recv_sem=recv_sems.at[i],
                device_id=(x, y, 1 - c), device_id_type=MESH)
            cp.start()
            copies.append(cp)
        for cp in copies:
            cp.wait()

    return pl.pallas_call(
        body, name="grad_core_exchange",
        in_specs=[ANY] * n, out_specs=[ANY] * n,
        out_shape=[jax.ShapeDtypeStruct((g.shape[0], g.shape[1] // 2, g.shape[2]), F32) for g in grads],
        scratch_shapes=[pltpu.SemaphoreType.DMA((n,))] * 2,
    )(*grads)


def _chip_exchange(sums, by_rows):
    n = len(sums)
    dims = [(s.shape[1], s.shape[2]) if by_rows[i] else (s.shape[1], s.shape[2] // N_SHARDS)
            for i, s in enumerate(sums)]

    def body(*refs):
        ins, outs = refs[:n], refs[n:2 * n]
        send_sems, recv_sems = refs[2 * n:]
        x, y, c = _place()
        chips = _other_chips(x, y)

        def shard(i, j):
            if by_rows[i]:
                return ins[i].at[j]
            return ins[i].at[0, :, pl.ds(pl.multiple_of(j * dims[i][1], 128), dims[i][1])]

        copies = []
        for i in range(n):
            for k in range(3):
                kj = 2 * chips[k][0] + chips[k][1]
                cp = pltpu.make_async_remote_copy(
                    src_ref=shard(i, kj), dst_ref=outs[i].at[k],
                    send_sem=send_sems.at[3 * i + k], recv_sem=recv_sems.at[3 * i + k],
                    device_id=(*chips[k], c), device_id_type=MESH)
                cp.start()
                copies.append(cp)
        for cp in copies:
            cp.wait()

    return pl.pallas_call(
        body, name="grad_chip_exchange",
        in_specs=[ANY] * n, out_specs=[ANY] * n,
        out_shape=[jax.ShapeDtypeStruct((3, h, cc), BF16) for h, cc in dims],
        scratch_shapes=[pltpu.SemaphoreType.DMA((3 * n,))] * 2,
    )(*sums)


def _core_share(reduced):
    n = len(reduced)

    def body(*refs):
        outs = refs[n:2 * n]
        send_sems, recv_sems = refs[2 * n:]
        x, y, c = _place()
        copies = []
        for i in range(n):
            cp = pltpu.make_async_remote_copy(
                src_ref=outs[i].at[c], dst_ref=outs[i].at[c], send_sem=send_sems.at[i], recv_sem=recv_sems.at[i],
                device_id=(x, y, 1 - c), device_id_type=MESH)
            cp.start()
            copies.append(cp)
        for cp in copies:
            cp.wait()

    return pl.pallas_call(
        body, name="grad_core_share",
        in_specs=[ANY] * n, out_specs=[ANY] * n,
        out_shape=[jax.ShapeDtypeStruct(r.shape, F32) for r in reduced],
        input_output_aliases={i: i for i in range(n)},
        scratch_shapes=[pltpu.SemaphoreType.DMA((n,))] * 2,
    )(*reduced)


def _exchange_small(pack):
    def body(in_ref, out_ref, send_sems, recv_sems, local_sem):
        x, y, c = _place()
        sibling = (x, y, 1 - c)
        chips = _other_chips(x, y)

        def slot(px, py, pc):
            return out_ref.at[4 * px + 2 * py + pc]

        def copy(k, src, sender, to):
            return pltpu.make_async_remote_copy(
                src_ref=src, dst_ref=slot(*sender), send_sem=send_sems.at[k], recv_sem=recv_sems.at[k],
                device_id=to, device_id_type=MESH)

        me = (x, y, c)
        mine = pltpu.make_async_copy(in_ref, slot(*me), local_sem)
        mine.start()
        sends = [copy(0, in_ref, me, sibling)] + [copy(1 + k, in_ref, me, (*chips[k], c)) for k in range(3)]
        for cp in sends:
            cp.start()
        for k in range(3):
            copy(1 + k, in_ref, (*chips[k], c), me).wait_recv()
            cp = copy(4 + k, slot(*chips[k], c), (*chips[k], c), sibling)
            cp.start()
            sends.append(cp)
        copy(0, in_ref, sibling, me).wait_recv()
        for k in range(3):
            copy(4 + k, in_ref, (*chips[k], 1 - c), me).wait_recv()
        for cp in sends:
            cp.wait_send()
        mine.wait()

    return pl.pallas_call(
        body, name="grad_small_exchange",
        in_specs=[pl.BlockSpec(memory_space=pltpu.VMEM)], out_specs=ANY,
        out_shape=jax.ShapeDtypeStruct((N_DEV,) + pack.shape, F32),
        scratch_shapes=[pltpu.SemaphoreType.DMA((N_DEV - 1,)), pltpu.SemaphoreType.DMA((N_DEV - 1,)),
                        pltpu.SemaphoreType.DMA],
    )(pack)


def _pack_rows(parts, rows):
    flat = jnp.concatenate([a.reshape(-1) for a in parts])
    return jnp.pad(flat, (0, rows * 128 - flat.shape[0])).reshape(rows, 128)


def _unpack_rows(pack, shapes):
    flat = pack.reshape(-1)
    out, at = [], 0
    for s in shapes:
        size = 1
        for d in s:
            size *= d
        out.append(flat[at:at + size].reshape(s))
        at += size
    return out


def kernel(x, p, norm1_g, w_in, b_gate, pool_w, pool_scale, pool_proj, conv_w, conv_b, w_rg, b_rg, w_ig, b_ig, lru_lambda, lru_proj, w_out, norm2_g, w_ffn_in, w_ffn_out, ple_norm_g, w_ple_gate, w_ple_proj, final_g, loss_target, m_norm1_g, m_w_in, m_b_gate, m_pool_w, m_pool_scale, m_pool_proj, m_conv_w, m_conv_b, m_w_rg, m_b_rg, m_w_ig, m_b_ig, m_lru_lambda, m_lru_proj, m_w_out, m_norm2_g, m_w_ffn_in, m_w_ffn_out, m_ple_norm_g, m_w_ple_gate, m_w_ple_proj, m_final_g, v_norm1_g, v_w_in, v_b_gate, v_pool_w, v_pool_scale, v_pool_proj, v_conv_w, v_conv_b, v_w_rg, v_b_rg, v_w_ig, v_b_ig, v_lru_lambda, v_lru_proj, v_w_out, v_norm2_g, v_w_ffn_in, v_w_ffn_out, v_ple_norm_g, v_w_ple_gate, v_w_ple_proj, v_final_g):
    weights = dict(norm1_g=norm1_g, w_in=w_in, b_gate=b_gate, pool_w=pool_w, pool_scale=pool_scale,
                   pool_proj=pool_proj, conv_w=conv_w, conv_b=conv_b, w_rg=w_rg, b_rg=b_rg, w_ig=w_ig, b_ig=b_ig,
                   lru_lambda=lru_lambda, lru_proj=lru_proj, w_out=w_out, norm2_g=norm2_g, w_ffn_in=w_ffn_in,
                   w_ffn_out=w_ffn_out, ple_norm_g=ple_norm_g, w_ple_gate=w_ple_gate, w_ple_proj=w_ple_proj,
                   final_g=final_g)
    m_in = dict(norm1_g=m_norm1_g, w_in=m_w_in, b_gate=m_b_gate, pool_w=m_pool_w, pool_scale=m_pool_scale,
                pool_proj=m_pool_proj, conv_w=m_conv_w, conv_b=m_conv_b, w_rg=m_w_rg, b_rg=m_b_rg, w_ig=m_w_ig,
                b_ig=m_b_ig, lru_lambda=m_lru_lambda, lru_proj=m_lru_proj, w_out=m_w_out, norm2_g=m_norm2_g,
                w_ffn_in=m_w_ffn_in, w_ffn_out=m_w_ffn_out, ple_norm_g=m_ple_norm_g, w_ple_gate=m_w_ple_gate,
                w_ple_proj=m_w_ple_proj, final_g=m_final_g)
    v_in = dict(norm1_g=v_norm1_g, w_in=v_w_in, b_gate=v_b_gate, pool_w=v_pool_w, pool_scale=v_pool_scale,
                pool_proj=v_pool_proj, conv_w=v_conv_w, conv_b=v_conv_b, w_rg=v_w_rg, b_rg=v_b_rg, w_ig=v_w_ig,
                b_ig=v_b_ig, lru_lambda=v_lru_lambda, lru_proj=v_lru_proj, w_out=v_w_out, norm2_g=v_norm2_g,
                w_ffn_in=v_w_ffn_in, w_ffn_out=v_w_ffn_out, ple_norm_g=v_ple_norm_g, w_ple_gate=v_w_ple_gate,
                w_ple_proj=v_w_ple_proj, final_g=v_final_g)
    names = list(weights)
    big = ["w_in", "pool_proj", "lru_proj", "w_out", "w_ffn_in", "w_ffn_out", "w_ple_gate", "w_ple_proj"]
    by_rows = [n in ("lru_proj", "w_out", "w_ffn_out", "w_ple_gate") for n in big]
    small = [n for n in names if n not in big]

    shard_j = 2 * lax.axis_index("x") + lax.axis_index("y")
    T = x.shape[1]
    xs, ps, tgt = x[0], p[0, 0], loss_target[0]

    small_local = jnp.concatenate([b_gate[0], conv_w[0], jnp.zeros((2, 256), F32)], axis=0)
    core = lax.axis_index("c").astype(jnp.int32)
    place = jnp.stack([shard_j, core]).astype(jnp.int32)
    blocks = [_cast_into_block(weights[n][0], rows, place[0], "cast_" + n) for n, rows in zip(big, by_rows)]
    gathered = _all_gather_weights(blocks, [weights[n].shape[1:] for n in big], by_rows, small_local)
    full = dict(zip(big, gathered[:-1]))
    b_gate_full = gathered[-1][0:2].reshape(1, 2 * D_MODEL)
    conv_w_full = gathered[-1][2:6]
    pool_w_b, w_rg_b, w_ig_b = pool_w[0].astype(BF16), w_rg[0].astype(BF16), w_ig[0].astype(BF16)
    b_rg_row, b_ig_row = b_rg.reshape(1, D_MODEL), b_ig.reshape(1, D_MODEL)
    final_row = final_g.reshape(1, D_MODEL)

    zp, zl, zg, zt, u = _f1_in_proj(xs, norm1_g, full["w_in"])
    h1, hs, yp, yl = _f2_mixer(xs, zp, zl, zg, zt, b_gate_full, pool_w_b, pool_scale, full["pool_proj"], conv_w_full,
                               conv_b, w_rg_b, b_rg_row, w_ig_b, b_ig_row, lru_lambda, full["lru_proj"], full["w_out"])
    h2, v, ff, act = _f3_ffn(h1, norm2_g, full["w_ffn_in"], full["w_ffn_out"])

    loss_sum, dh2, g_ple_gate, g_ple_proj, vec4 = _b4_ple_loss(
        h2, ps, tgt, ple_norm_g, full["w_ple_gate"], full["w_ple_proj"], final_row)
    dff, dh1, vec3 = _b3_ffn(dh2, h1, ff, norm2_g, full["w_ffn_in"], full["w_ffn_out"])
    g_ffn_in = _wgrad(v, dff, 2 * D_FF // N_SHARDS, "wgrad_ffn_in")
    g_ffn_out = _wgrad(act, dh2, D_MODEL, "wgrad_ffn_out", tokens=WGRAD_TOKENS // 2)
    dzt, dyp, dyl, g_w_out, vec_g = _b2_gates(dh1, zt, yp, yl, b_gate_full, full["w_out"])
    dzp, g_pool_proj, g_pool_w, vec_p = _b2_pool(dyp, zp, pool_w_b, pool_scale, full["pool_proj"])
    dzl, dzg, g_lru_proj, g_w_rg, g_w_ig, vec_l = _b2_lru(
        dyl, zl, zg, hs, conv_w_full, conv_b, w_rg_b, b_rg_row, w_ig_b, b_ig_row, lru_lambda, full["lru_proj"])
    grad_x, vec1 = _b1_in_proj(dzp, dzl, dzg, dzt, xs, dh1, norm1_g, full["w_in"])
    g_w_in = jnp.concatenate([
        _wgrad(u, dzp, POOL_WIDTH, "wgrad_in_pool"), _wgrad(u, dzl, D_MODEL, "wgrad_in_lru"),
        _wgrad(u, dzg, D_MODEL, "wgrad_in_gelu"), _wgrad(u, dzt, D_MODEL, "wgrad_in_gate")], axis=1)

    loss = lax.psum(loss_sum[0, 0] * (0.5 / D_MODEL), ("x", "y", "c"))

    big_grads = dict(w_in=g_w_in, pool_proj=g_pool_proj, lru_proj=g_lru_proj, w_out=g_w_out, w_ffn_in=g_ffn_in,
                     w_ffn_out=g_ffn_out, w_ple_gate=g_ple_gate, w_ple_proj=g_ple_proj)
    stacked = []
    for n, rows in zip(big, by_rows):
        g = big_grads[n]
        stacked.append(g.reshape(N_SHARDS, g.shape[0] // N_SHARDS, g.shape[1]) if rows else g[None])
    theirs = _core_exchange(stacked)
    chip_sums = [_sum_cores(g, t, core, "sum_cores_" + n) for g, t, n in zip(stacked, theirs, big)]
    slots = _chip_exchange(chip_sums, by_rows)
    reduced = _core_share([_sum_chips(s, q, rows, place, "sum_chips_" + n)
                           for s, q, rows, n in zip(chip_sums, slots, by_rows, big)])

    grads, deltas, new_m, new_v = {}, {}, {}, {}
    for n, r in zip(big, reduced):
        g = r.reshape(r.shape[0] * r.shape[1], r.shape[2])
        d, nm, nv = _adamw(weights[n][0], g, m_in[n][0], v_in[n][0], "adamw_" + n)
        grads[n], deltas[n], new_m[n], new_v[n] = g[None], d[None], nm[None], nv[None]

    small_full = dict(
        norm1_g=vec1[0], b_gate=vec_g[0:2], pool_w=g_pool_w, pool_scale=vec_p[0, :POOL_WIDTH],
        conv_w=vec_l[_V_CONVW:_V_CONVW + CONV_WIDTH], conv_b=vec_l[_V_CONVB], w_rg=g_w_rg, b_rg=vec_l[_V_BRG],
        w_ig=g_w_ig, b_ig=vec_l[_V_BIG], lru_lambda=vec_l[_V_LAM], norm2_g=vec3[0], ple_norm_g=vec4[1],
        final_g=vec4[0])
    full_shapes = [small_full[n].shape for n in small]
    n_full = sum(int(small_full[n].size) for n in small)
    rows_full = -(-n_full // (128 * ROW_TILE)) * ROW_TILE
    everyone = _exchange_small(_pack_rows([small_full[n] for n in small], rows_full))
    summed = dict(zip(small, _unpack_rows(_sum_slots(everyone, "sum_small"), full_shapes)))
    summed["b_gate"] = lax.dynamic_slice_in_dim(summed["b_gate"], shard_j * 256, 256, axis=1)
    summed["conv_w"] = lax.dynamic_slice_in_dim(summed["conv_w"], shard_j * 256, 256, axis=1)
    local_shapes = [weights[n].shape for n in small]
    n_local = sum(int(weights[n].size) for n in small)
    rows_local = -(-n_local // (128 * ROW_TILE)) * ROW_TILE
    packs = [_pack_rows([src[n] for n in small], rows_local) for src in (weights, summed, m_in, v_in)]
    d_s, nm_s, nv_s = _adamw(*packs, "adamw_small")
    for dst, pack in ((grads, packs[1]), (deltas, d_s), (new_m, nm_s), (new_v, nv_s)):
        dst.update(zip(small, _unpack_rows(pack, local_shapes)))

    return (loss, grad_x[None], *[grads[n] for n in names], *[deltas[n] for n in names],
            *[new_m[n] for n in names], *[new_v[n] for n in names])
```

```python
import functools

import jax
import jax.numpy as jnp
from jax import lax
from jax.experimental import pallas as pl
from jax.experimental.pallas import tpu as pltpu

F32 = jnp.float32
BF16 = jnp.bfloat16

D_MODEL = 1024
POOL_WIDTH = 512
POOL_GROUP_DIM = 128
POOL_WINDOWS = (2, 4, 8, 16)
POOL_HALO = 16
LRU_HEADS = 8
LRU_HEAD_DIM = 128
CONV_WIDTH = 4
LRU_C = 8.0
D_FF = 2816
PLE_DIM = 256
RMS_EPS = 1e-6
N_SHARDS = 4
N_DEV = 8

ADAM_LR = 0.001
ADAM_B1 = 0.9
ADAM_B2 = 0.999
ADAM_EPS = 1e-08
ADAM_WD = 0.01
ADAM_STEP = 10

ROW_TILE = 256
WIDE_TILE = 512
WGRAD_TOKENS = 2048
SUBLANES = 8
VMEM_LIMIT = 56 * 1024 * 1024
MESH = pl.DeviceIdType.MESH
ANY = pl.BlockSpec(memory_space=pl.ANY)


def _params(semantics=None):
    return pltpu.CompilerParams(dimension_semantics=semantics, vmem_limit_bytes=VMEM_LIMIT)


def _resident(shape):
    n = len(shape)
    return pl.BlockSpec(shape, lambda *_: (0,) * n, pipeline_mode=pl.Buffered(1))


def _acc(shape):
    n = len(shape)
    return pl.BlockSpec(shape, lambda *_: (0,) * n)


def _rows(tile, cols):
    return pl.BlockSpec((tile, cols), lambda i: (i, 0))


def _rows_rev(tile, cols, n_tiles):
    return pl.BlockSpec((tile, cols), lambda i: (n_tiles - 1 - i, 0))


def _halo_before_rev(rows, cols, tile, n_tiles):
    per = tile // rows
    return pl.BlockSpec((rows, cols), lambda i: (jnp.maximum((n_tiles - 1 - i) * per - 1, 0), 0))


def _nn(a, b):
    return jnp.dot(a, b, preferred_element_type=F32)


def _nt(a, b):
    return lax.dot_general(a, b, (((1,), (1,)), ((), ())), preferred_element_type=F32)


def _tn(a, b):
    return lax.dot_general(a, b, (((0,), (0,)), ((), ())), preferred_element_type=F32)


def _rms(x):
    r = lax.rsqrt(jnp.mean(x * x, axis=-1, keepdims=True) + RMS_EPS)
    return x * r, r


def _rms_bwd(dn, n, r):
    return r * (dn - n * jnp.mean(dn * n, axis=-1, keepdims=True))


def _sigmoid(x):
    return 0.5 * jnp.tanh(0.5 * x) + 0.5


_GELU_C = 0.7978845608028654
_GELU_A = 0.044715


def _gelu(x):
    t = jnp.tanh(_GELU_C * (x + _GELU_A * x * x * x))
    return 0.5 * x * (1.0 + t)


def _gelu_and_grad(x):
    x2 = x * x
    t = jnp.tanh(_GELU_C * (x + _GELU_A * x2 * x))
    cdf = 0.5 * (1.0 + t)
    grad = cdf + 0.5 * x * (1.0 - t * t) * _GELU_C * (1.0 + 3.0 * _GELU_A * x2)
    return x * cdf, grad


def _softplus_neg(lam):
    e = jnp.exp(-jnp.abs(lam))
    sp = jnp.maximum(-lam, 0.0) + jnp.log1p(e)
    return sp, -_sigmoid(-lam)


def _colsum(v):
    return jnp.sum(v, axis=0, keepdims=True)


def _row_ids(shape):
    return lax.broadcasted_iota(jnp.int32, shape, 0)


def _shift_down(cat, k):
    return pltpu.roll(cat, k, 0) if k else cat


def _shift_up(cat, k):
    return pltpu.roll(cat, cat.shape[0] - k, 0) if k else cat


IN_SPLITS = (0, POOL_WIDTH, POOL_WIDTH + D_MODEL, POOL_WIDTH + 2 * D_MODEL, POOL_WIDTH + 4 * D_MODEL)
IN_WIDTHS = tuple(IN_SPLITS[k + 1] - IN_SPLITS[k] for k in range(4))
PROJ_CHUNK = 256


def _no_tick():
    pass


class _Interleaved:
    def __init__(self, pieces):
        self._pieces = iter(pieces)

    def tick(self, n=1):
        for _ in range(n):
            piece = next(self._pieces, None)
            if piece is not None:
                piece()

    def flush(self):
        for piece in self._pieces:
            piece()


def _pool_forward(zp_cat, pw_ref, first_row, tick=_no_tick):
    tt = zp_cat.shape[0] - POOL_HALO
    t_glob = first_row + _row_ids((tt, POOL_GROUP_DIM))
    pooled, mixed = [], []
    for g, w in enumerate(POOL_WINDOWS):
        cat = zp_cat[:, g * POOL_GROUP_DIM:(g + 1) * POOL_GROUP_DIM]
        s, k = cat, 1
        while k < w:
            s = s + _shift_down(s, k)
            k *= 2
        cnt = jnp.minimum(t_glob + 1, w).astype(F32)
        pg = s[POOL_HALO:] / cnt - cat[POOL_HALO:]
        pooled.append(pg)
        mixed.append(_nn(pg.astype(BF16), pw_ref[g]))
        tick()
    return jnp.concatenate(pooled, axis=1), jnp.concatenate(mixed, axis=1)


def _lru_gates(zl_cat, conv_w, conv_b, wrg_ref, brg, wig_ref, big, sp, first_row, tick=_no_tick):
    tt = zl_cat.shape[0] - SUBLANES
    xc = conv_w[CONV_WIDTH - 1:CONV_WIDTH] * zl_cat
    for k in range(1, CONV_WIDTH):
        xc = xc + conv_w[CONV_WIDTH - 1 - k:CONV_WIDTH - k] * _shift_down(zl_cat, k)
        tick()
    xc = xc[SUBLANES:] + conv_b
    xh = xc.astype(BF16)
    pr, pi = [], []
    for h in range(LRU_HEADS):
        xs = xh[:, h * LRU_HEAD_DIM:(h + 1) * LRU_HEAD_DIM]
        pr.append(_nn(xs, wrg_ref[h]))
        pi.append(_nn(xs, wig_ref[h]))
    r = _sigmoid(jnp.concatenate(pr, axis=1) + brg)
    tick()
    ig = _sigmoid(jnp.concatenate(pi, axis=1) + big)
    tick()
    a = jnp.exp(-LRU_C * r * sp)
    tick()
    mult = jnp.sqrt(jnp.maximum(1.0 - a * a, 0.0))
    t_glob = first_row + _row_ids((tt, D_MODEL))
    mult = jnp.where(t_glob == 0, 1.0, mult)
    tick()
    return xc, r, ig, a, mult


def _f12_mixer(x, norm1_g, w_in, b_gate, pool_w, pool_scale, pool_proj, conv_w, conv_b, w_rg, b_rg, w_ig, b_ig,
               lru_lambda, lru_proj, w_out):
    T = x.shape[0]
    tt = ROW_TILE
    nt = T // tt
    n_groups = tt // SUBLANES
    proj_mid = IN_SPLITS[3] + D_MODEL // 2

    def body(xm_ref, x_ref, g1_ref, win_ref, bg_ref, pw_ref, ps_ref, pp_ref, cw_ref, cb_ref,
             wrg_ref, brg_ref, wig_ref, big_ref, lam_ref, lp_ref, wo_ref,
             zp_ref, zl_ref, zg_ref, zt_ref, u_ref, h1_ref, hs_ref, yp_ref, yl_ref,
             zbuf, zp_halo, zl_halo, a_s, b_s, carry_s):
        s = pl.program_id(0)

        @pl.when(s == 0)
        def _():
            zbuf[1] = jnp.zeros((tt, IN_SPLITS[4]), F32)
            zp_halo[...] = jnp.zeros_like(zp_halo)
            zl_halo[...] = jnp.zeros_like(zl_halo)
            carry_s[...] = jnp.zeros_like(carry_s)

        z_new, z_old = zbuf.at[s % 2], zbuf.at[(s + 1) % 2]
        first = s <= 1
        first_row = jnp.maximum(s - 1, 0) * tt

        n1, _ = _rms(xm_ref[...])
        u = (n1 * g1_ref[...]).astype(BF16)
        u_ref[...] = u

        z_refs = (zp_ref, zl_ref, zg_ref, zt_ref)

        def project(lo):
            k = max(i for i in range(4) if IN_SPLITS[i] <= lo)
            part = _nn(u, win_ref[:, lo:lo + PROJ_CHUNK])
            z_new[:, lo:lo + PROJ_CHUNK] = part
            z_refs[k][:, lo - IN_SPLITS[k]:lo - IN_SPLITS[k] + PROJ_CHUNK] = part.astype(z_refs[k].dtype)

        before_scan = _Interleaved(functools.partial(project, lo) for lo in range(0, proj_mid, PROJ_CHUNK))
        after_scan = _Interleaved(functools.partial(project, lo) for lo in range(proj_mid, IN_SPLITS[4], PROJ_CHUNK))

        zp_cat = jnp.concatenate([jnp.where(first, 0.0, zp_halo[...]), z_old[:, IN_SPLITS[0]:IN_SPLITS[1]]], axis=0)
        _, mixed = _pool_forward(zp_cat, pw_ref, first_row, before_scan.tick)
        y_pool = _nn((mixed * ps_ref[...]).astype(BF16), pp_ref[...])

        sp, _ = _softplus_neg(lam_ref[...])
        zl_cat = jnp.concatenate([jnp.where(first, 0.0, zl_halo[...]), z_old[:, IN_SPLITS[1]:IN_SPLITS[2]]], axis=0)
        xc, _, ig, a, mult = _lru_gates(zl_cat, cw_ref[...], cb_ref[...], wrg_ref, brg_ref[...], wig_ref,
                                        big_ref[...], sp, first_row, before_scan.tick)
        a_s[...] = a
        b_s[...] = mult * ig * xc
        before_scan.flush()

        rows8 = _row_ids((SUBLANES, D_MODEL))

        def group(g, carry):
            at = pl.ds(pl.multiple_of(g * SUBLANES, SUBLANES), SUBLANES)
            A, B = a_s[at, :], b_s[at, :]
            for s in (1, 2, 4):
                m = rows8 >= s
                B = jnp.where(m, A * pltpu.roll(B, s, 0) + B, B)
                A = jnp.where(m, A * pltpu.roll(A, s, 0), A)
            h = A * carry + B
            hs_ref[at, :] = h
            return jnp.broadcast_to(h[SUBLANES - 1:SUBLANES, :], (SUBLANES, D_MODEL))

        carry_s[...] = lax.fori_loop(0, n_groups, group, jnp.where(first, 0.0, carry_s[...]))
        gelu = _gelu(z_old[:, IN_SPLITS[2]:IN_SPLITS[3]])
        after_scan.tick(2)
        y_lru = _nn((hs_ref[...] * gelu).astype(BF16), lp_ref[...])

        gates = _sigmoid(z_old[:, IN_SPLITS[3]:IN_SPLITS[4]] + bg_ref[...])
        after_scan.tick(2)
        merged = gates[:, :D_MODEL] * y_pool + gates[:, D_MODEL:] * y_lru
        after_scan.flush()
        h1_ref[...] = x_ref[...] + _nn(merged.astype(BF16), wo_ref[...])
        yp_ref[...] = y_pool.astype(BF16)
        yl_ref[...] = y_lru.astype(BF16)
        zp_halo[...] = z_old[tt - POOL_HALO:, IN_SPLITS[0]:IN_SPLITS[1]]
        zl_halo[...] = z_old[tt - SUBLANES:, IN_SPLITS[1]:IN_SPLITS[2]]

    def ahead(cols):
        return pl.BlockSpec((tt, cols), lambda s: (jnp.minimum(s, nt - 1), 0))

    def behind(cols):
        return pl.BlockSpec((tt, cols), lambda s: (jnp.maximum(s - 1, 0), 0))

    res = [norm1_g, w_in, b_gate, pool_w, pool_scale, pool_proj, conv_w, conv_b, w_rg, b_rg, w_ig, b_ig, lru_lambda,
           lru_proj, w_out]
    return pl.pallas_call(
        body, name="f12_mixer", grid=(nt + 1,),
        in_specs=[ahead(D_MODEL), behind(D_MODEL)] + [_resident(w.shape) for w in res],
        out_specs=[ahead(w) for w in IN_WIDTHS] + [ahead(D_MODEL)] + [behind(D_MODEL)] * 4,
        out_shape=[jax.ShapeDtypeStruct((T, w), dt) for w, dt in zip(IN_WIDTHS, (F32, F32, F32, BF16))]
        + [jax.ShapeDtypeStruct((T, D_MODEL), BF16), jax.ShapeDtypeStruct((T, D_MODEL), F32),
           jax.ShapeDtypeStruct((T, D_MODEL), F32), jax.ShapeDtypeStruct((T, D_MODEL), BF16),
           jax.ShapeDtypeStruct((T, D_MODEL), BF16)],
        scratch_shapes=[pltpu.VMEM((2, tt, IN_SPLITS[4]), F32), pltpu.VMEM((POOL_HALO, POOL_WIDTH), F32),
                        pltpu.VMEM((SUBLANES, D_MODEL), F32), pltpu.VMEM((tt, D_MODEL), F32),
                        pltpu.VMEM((tt, D_MODEL), F32), pltpu.VMEM((SUBLANES, D_MODEL), F32)],
        compiler_params=_params(("arbitrary",)),
    )(x, x, *res)


def _f3_ffn(h1, norm2_g, w_ffn_in, w_ffn_out):
    T = h1.shape[0]
    tm = ROW_TILE

    def body(h_ref, g_ref, wi_ref, wo_ref, h2_ref, v_ref, ff_ref, act_ref):
        h = h_ref[...]
        n, _ = _rms(h)
        v = (n * g_ref[...]).astype(BF16)
        v_ref[...] = v
        g_ff = _nn(v, wi_ref[:, :D_FF])
        u_ff = _nn(v, wi_ref[:, D_FF:])
        ff_ref[:, :D_FF] = g_ff.astype(BF16)
        ff_ref[:, D_FF:] = u_ff.astype(BF16)
        act = (g_ff * _sigmoid(g_ff) * u_ff).astype(BF16)
        act_ref[...] = act
        h2_ref[...] = h + _nn(act, wo_ref[...])

    return pl.pallas_call(
        body, name="f3_ffn", grid=(T // tm,),
        in_specs=[_rows(tm, D_MODEL), _resident((1, D_MODEL)), _resident(w_ffn_in.shape), _resident(w_ffn_out.shape)],
        out_specs=[_rows(tm, D_MODEL), _rows(tm, D_MODEL), _rows(tm, 2 * D_FF), _rows(tm, D_FF)],
        out_shape=[jax.ShapeDtypeStruct((T, D_MODEL), F32), jax.ShapeDtypeStruct((T, D_MODEL), BF16),
                   jax.ShapeDtypeStruct((T, 2 * D_FF), BF16), jax.ShapeDtypeStruct((T, D_FF), BF16)],
        compiler_params=_params(("arbitrary",)),
    )(h1, norm2_g, w_ffn_in, w_ffn_out)


def _b4_ple_loss(h2, p, target, ple_norm_g, w_ple_gate, w_ple_proj, final_g):
    T = h2.shape[0]
    tm = WIDE_TILE

    def body(h_ref, p_ref, t_ref, gp_ref, wg_ref, wp_ref, gf_ref, loss_ref, dh2_ref, dwg_ref, dwp_ref, vec_ref):
        @pl.when(pl.program_id(0) == 0)
        def _():
            loss_ref[...] = jnp.zeros_like(loss_ref)
            dwg_ref[...] = jnp.zeros_like(dwg_ref)
            dwp_ref[...] = jnp.zeros_like(dwp_ref)
            vec_ref[...] = jnp.zeros_like(vec_ref)

        h2v = h_ref[...]
        n3, r3 = _rms(h2v)
        n3g = (n3 * gp_ref[...]).astype(BF16)
        pg = _sigmoid(_nn(n3g, wg_ref[...]))
        pb = p_ref[...].astype(BF16)
        e = _nn(pb, wp_ref[...])
        h3 = h2v + pg * e
        n4, r4 = _rms(h3)
        diff = n4 * gf_ref[...] - t_ref[...]
        loss_ref[...] += jnp.sum(diff * diff).reshape(1, 1)
        dy = diff * (1.0 / D_MODEL)
        vec_ref[0:1, :] += _colsum(dy * n4)
        dh3 = _rms_bwd(dy * gf_ref[...], n4, r4)
        dwp_ref[...] += _tn(pb, (dh3 * pg).astype(BF16))
        dq = (dh3 * e * pg * (1.0 - pg)).astype(BF16)
        dwg_ref[...] += _tn(n3g, dq)
        dn3g = _nt(dq, wg_ref[...])
        vec_ref[1:2, :] += _colsum(dn3g * n3)
        dh2_ref[...] = dh3 + _rms_bwd(dn3g * gp_ref[...], n3, r3)

    return pl.pallas_call(
        body, name="b4_ple_loss", grid=(T // tm,),
        in_specs=[_rows(tm, D_MODEL), _rows(tm, PLE_DIM), _rows(tm, D_MODEL), _resident((1, D_MODEL)),
                  _resident(w_ple_gate.shape), _resident(w_ple_proj.shape), _resident((1, D_MODEL))],
        out_specs=[_acc((1, 1)), _rows(tm, D_MODEL), _acc(w_ple_gate.shape), _acc(w_ple_proj.shape),
                   _acc((SUBLANES, D_MODEL))],
        out_shape=[jax.ShapeDtypeStruct((1, 1), F32), jax.ShapeDtypeStruct((T, D_MODEL), F32),
                   jax.ShapeDtypeStruct(w_ple_gate.shape, F32), jax.ShapeDtypeStruct(w_ple_proj.shape, F32),
                   jax.ShapeDtypeStruct((SUBLANES, D_MODEL), F32)],
        compiler_params=_params(("arbitrary",)),
    )(h2, p, target, ple_norm_g, w_ple_gate, w_ple_proj, final_g)


def _b3_ffn(dh2, h1, ff, norm2_g, w_ffn_in, w_ffn_out):
    T = h1.shape[0]
    tm = ROW_TILE

    def body(d_ref, h_ref, ff_ref, g_ref, wi_ref, wo_ref, dff_ref, dh1_ref, vec_ref):
        @pl.when(pl.program_id(0) == 0)
        def _():
            vec_ref[...] = jnp.zeros_like(vec_ref)

        dh2v = d_ref[...]
        dact = _nt(dh2v.astype(BF16), wo_ref[...])
        g_ff = ff_ref[:, :D_FF].astype(F32)
        u_ff = ff_ref[:, D_FF:].astype(F32)
        s = _sigmoid(g_ff)
        dg = (dact * u_ff * (s * (1.0 + g_ff * (1.0 - s)))).astype(BF16)
        du = (dact * (g_ff * s)).astype(BF16)
        dff_ref[:, :D_FF] = dg
        dff_ref[:, D_FF:] = du
        dv = _nt(dg, wi_ref[:, :D_FF]) + _nt(du, wi_ref[:, D_FF:])
        n2, r2 = _rms(h_ref[...])
        vec_ref[0:1, :] += _colsum(dv * n2)
        dh1_ref[...] = dh2v + _rms_bwd(dv * g_ref[...], n2, r2)

    return pl.pallas_call(
        body, name="b3_ffn", grid=(T // tm,),
        in_specs=[_rows(tm, D_MODEL), _rows(tm, D_MODEL), _rows(tm, 2 * D_FF), _resident((1, D_MODEL)),
                  _resident(w_ffn_in.shape), _resident(w_ffn_out.shape)],
        out_specs=[_rows(tm, 2 * D_FF), _rows(tm, D_MODEL), _acc((SUBLANES, D_MODEL))],
        out_shape=[jax.ShapeDtypeStruct((T, 2 * D_FF), BF16), jax.ShapeDtypeStruct((T, D_MODEL), F32),
                   jax.ShapeDtypeStruct((SUBLANES, D_MODEL), F32)],
        compiler_params=_params(("arbitrary",)),
    )(dh2, h1, ff, norm2_g, w_ffn_in, w_ffn_out)


def _wgrad(a, b, col_tile, name, tokens=WGRAD_TOKENS):
    T, K = a.shape
    N = b.shape[1]
    tk = min(T, tokens)

    def body(a_ref, b_ref, o_ref):
        @pl.when(pl.program_id(1) == 0)
        def _():
            o_ref[...] = jnp.zeros_like(o_ref)

        o_ref[...] += _tn(a_ref[...].astype(BF16), b_ref[...].astype(BF16))

    return pl.pallas_call(
        body, name=name, grid=(N // col_tile, T // tk),
        in_specs=[pl.BlockSpec((tk, K), lambda j, k: (k, 0)), pl.BlockSpec((tk, col_tile), lambda j, k: (k, j))],
        out_specs=pl.BlockSpec((K, col_tile), lambda j, k: (0, j)),
        out_shape=jax.ShapeDtypeStruct((K, N), F32),
        compiler_params=_params(("arbitrary", "arbitrary")),
    )(a, b)


def _b2_gates(dh1, zt, yp, yl, b_gate, w_out):
    T = dh1.shape[0]
    tm = WIDE_TILE

    def body(d_ref, zt_ref, yp_ref, yl_ref, bg_ref, wo_ref, dzt_ref, dyp_ref, dyl_ref, dwo_ref, vec_ref):
        @pl.when(pl.program_id(0) == 0)
        def _():
            dwo_ref[...] = jnp.zeros_like(dwo_ref)
            vec_ref[...] = jnp.zeros_like(vec_ref)

        db = d_ref[...].astype(BF16)
        dm = _nt(db, wo_ref[...])
        gates = _sigmoid(zt_ref[...].astype(F32) + bg_ref[...])
        g0, g1 = gates[:, :D_MODEL], gates[:, D_MODEL:]
        y_pool, y_lru = yp_ref[...].astype(F32), yl_ref[...].astype(F32)
        dwo_ref[...] += _tn((g0 * y_pool + g1 * y_lru).astype(BF16), db)
        dz0 = dm * y_pool * g0 * (1.0 - g0)
        dz1 = dm * y_lru * g1 * (1.0 - g1)
        vec_ref[0:1, :] += _colsum(dz0)
        vec_ref[1:2, :] += _colsum(dz1)
        dzt_ref[:, :D_MODEL] = dz0.astype(BF16)
        dzt_ref[:, D_MODEL:] = dz1.astype(BF16)
        dyp_ref[...] = (dm * g0).astype(BF16)
        dyl_ref[...] = (dm * g1).astype(BF16)

    return pl.pallas_call(
        body, name="b2_gates", grid=(T // tm,),
        in_specs=[_rows(tm, D_MODEL), _rows(tm, 2 * D_MODEL), _rows(tm, D_MODEL), _rows(tm, D_MODEL),
                  _resident(b_gate.shape), _resident(w_out.shape)],
        out_specs=[_rows(tm, 2 * D_MODEL), _rows(tm, D_MODEL), _rows(tm, D_MODEL), _acc(w_out.shape),
                   _acc((SUBLANES, D_MODEL))],
        out_shape=[jax.ShapeDtypeStruct((T, 2 * D_MODEL), BF16), jax.ShapeDtypeStruct((T, D_MODEL), BF16),
                   jax.ShapeDtypeStruct((T, D_MODEL), BF16), jax.ShapeDtypeStruct(w_out.shape, F32),
                   jax.ShapeDtypeStruct((SUBLANES, D_MODEL), F32)],
        compiler_params=_params(("arbitrary",)),
    )(dh1, zt, yp, yl, b_gate, w_out)


def _b12_pool_in_proj(dyp, zp, dzl, dzg, dzt, x, dh1, norm1_g, w_in, pool_w, pool_scale, pool_proj):
    T = zp.shape[0]
    tt = ROW_TILE
    nt = T // tt

    def body(dy_ref, zp_ref, zph_ref, dzl_ref, dzg_ref, dzt_ref, x_ref, dh_ref, g1_ref, win_ref, pw_ref, ps_ref, pp_ref,
             dzp_ref, dx_ref, dpp_ref, dpw_ref, vec_ref, q_next):
        i = pl.program_id(0)
        ti = nt - 1 - i
        first_row = ti * tt

        @pl.when(i == 0)
        def _():
            dpp_ref[...] = jnp.zeros_like(dpp_ref)
            dpw_ref[...] = jnp.zeros_like(dpw_ref)
            vec_ref[...] = jnp.zeros_like(vec_ref)
            q_next[...] = jnp.zeros_like(q_next)

        du_parts = []

        def project(lo):
            k = max(i for i in range(4) if IN_SPLITS[i] <= lo)
            dz_ref = (None, dzl_ref, dzg_ref, dzt_ref)[k]
            at = lo - IN_SPLITS[k]
            part = _nt(dz_ref[:, at:at + PROJ_CHUNK], win_ref[:, lo:lo + PROJ_CHUNK])
            du_parts[:] = [part if not du_parts else du_parts[0] + part]

        mxu = _Interleaved(functools.partial(project, lo) for lo in range(IN_SPLITS[1], IN_SPLITS[4], PROJ_CHUNK))

        keep = (ti > 0).astype(F32)
        zp_cat = jnp.concatenate([zph_ref[...] * keep, zp_ref[...]], axis=0)
        pooled, mixed = _pool_forward(zp_cat, pw_ref, first_row, mxu.tick)
        dy = dy_ref[...]
        dpp_ref[...] += _tn((mixed * ps_ref[...]).astype(BF16), dy)
        mxu.tick(2)
        dms = _nt(dy, pp_ref[...])
        mxu.tick(2)
        vec_ref[0:1, :POOL_WIDTH] += _colsum(dms * mixed)
        dmixed = (dms * ps_ref[...]).astype(BF16)
        t_glob = first_row + _row_ids((tt, POOL_GROUP_DIM))
        dz, q_all = [], []
        for g, w in enumerate(POOL_WINDOWS):
            cols = slice(g * POOL_GROUP_DIM, (g + 1) * POOL_GROUP_DIM)
            dpw_ref[g] += _tn(pooled[:, cols].astype(BF16), dmixed[:, cols])
            dpooled = _nt(dmixed[:, cols], pw_ref[g])
            q = dpooled / jnp.minimum(t_glob + 1, w).astype(F32)
            q_all.append(q)
            s, k = jnp.concatenate([q, q_next[:, cols]], axis=0), 1
            while k < w:
                s = s + _shift_up(s, k)
                k *= 2
            dz.append(s[:tt] - dpooled)
            mxu.tick(2)
        dzp = jnp.concatenate(dz, axis=1).astype(BF16)
        dzp_ref[...] = dzp
        q_next[...] = jnp.concatenate([q[:POOL_HALO] for q in q_all], axis=1)
        mxu.flush()

        du = du_parts[0] + _nt(dzp, win_ref[:, IN_SPLITS[0]:IN_SPLITS[1]])
        n1, r1 = _rms(x_ref[...])
        vec_ref[1:2, :] += _colsum(du * n1)
        dx_ref[...] = dh_ref[...] + _rms_bwd(du * g1_ref[...], n1, r1)

    rev = functools.partial(_rows_rev, n_tiles=nt)
    res = [norm1_g, w_in, pool_w, pool_scale, pool_proj]
    return pl.pallas_call(
        body, name="b12_pool_in_proj", grid=(nt,),
        in_specs=[rev(tt, D_MODEL), rev(tt, POOL_WIDTH), _halo_before_rev(POOL_HALO, POOL_WIDTH, tt, nt),
                  rev(tt, D_MODEL), rev(tt, D_MODEL), rev(tt, 2 * D_MODEL), rev(tt, D_MODEL), rev(tt, D_MODEL)]
        + [_resident(w.shape) for w in res],
        out_specs=[rev(tt, POOL_WIDTH), rev(tt, D_MODEL), _acc(pool_proj.shape), _acc(pool_w.shape),
                   _acc((SUBLANES, D_MODEL))],
        out_shape=[jax.ShapeDtypeStruct((T, POOL_WIDTH), BF16), jax.ShapeDtypeStruct((T, D_MODEL), F32),
                   jax.ShapeDtypeStruct(pool_proj.shape, F32), jax.ShapeDtypeStruct(pool_w.shape, F32),
                   jax.ShapeDtypeStruct((SUBLANES, D_MODEL), F32)],
        scratch_shapes=[pltpu.VMEM((POOL_HALO, POOL_WIDTH), F32)],
        compiler_params=_params(("arbitrary",)),
    )(dyp, zp, zp, dzl, dzg, dzt, x, dh1, *res)


_V_CONVW, _V_CONVB, _V_BRG, _V_BIG, _V_LAM = 0, 4, 5, 6, 7


def _b2_lru(dyl, zl, zg, hs, conv_w, conv_b, w_rg, b_rg, w_ig, b_ig, lru_lambda, lru_proj):
    T = zl.shape[0]
    tt = ROW_TILE
    nt = T // tt
    n_groups = tt // SUBLANES

    def body(dy_ref, zl_ref, zlh_ref, zg_ref, hs_ref, hsh_ref, cw_ref, cb_ref, wrg_ref, brg_ref, wig_ref, big_ref,
             lam_ref, lp_ref, dzl_ref, dzg_ref, dlp_ref, dwrg_ref, dwig_ref, vec_ref,
             c_s, d_s, g_s, g_next, a_next, dxc_next):
        i = pl.program_id(0)
        ti = nt - 1 - i
        first_row = ti * tt

        @pl.when(i == 0)
        def _():
            dlp_ref[...] = jnp.zeros_like(dlp_ref)
            dwrg_ref[...] = jnp.zeros_like(dwrg_ref)
            dwig_ref[...] = jnp.zeros_like(dwig_ref)
            vec_ref[...] = jnp.zeros_like(vec_ref)
            g_next[...] = jnp.zeros_like(g_next)
            a_next[...] = jnp.zeros_like(a_next)
            dxc_next[...] = jnp.zeros_like(dxc_next)

        keep = (ti > 0).astype(F32)
        sp, dsp_dlam = _softplus_neg(lam_ref[...])
        cw = cw_ref[...]
        zl_cat = jnp.concatenate([zlh_ref[...] * keep, zl_ref[...]], axis=0)
        xc, r, ig, a, mult = _lru_gates(zl_cat, cw, cb_ref[...], wrg_ref, brg_ref[...], wig_ref, big_ref[...], sp,
                                        first_row)
        hs = hs_ref[...]
        gelu, dgelu = _gelu_and_grad(zg_ref[...])
        dy = dy_ref[...]
        dlp_ref[...] += _tn((hs * gelu).astype(BF16), dy)
        dyl = _nt(dy, lp_ref[...])
        dzg_ref[...] = (dyl * hs * dgelu).astype(BF16)

        d_s[...] = dyl * gelu
        c_s[...] = _shift_up(jnp.concatenate([a, a_next[...]], axis=0), 1)[:tt]
        rows8 = _row_ids((SUBLANES, D_MODEL))

        def group(k, carry):
            at = pl.ds(pl.multiple_of((n_groups - 1 - k) * SUBLANES, SUBLANES), SUBLANES)
            C, Dv = c_s[at, :], d_s[at, :]
            for s in (1, 2, 4):
                m = rows8 < SUBLANES - s
                Dv = jnp.where(m, C * pltpu.roll(Dv, SUBLANES - s, 0) + Dv, Dv)
                C = jnp.where(m, C * pltpu.roll(C, SUBLANES - s, 0), C)
            G = C * carry + Dv
            g_s[at, :] = G
            return jnp.broadcast_to(G[0:1, :], (SUBLANES, D_MODEL))

        g_next[...] = lax.fori_loop(0, n_groups, group, g_next[...])
        a_next[...] = jnp.broadcast_to(a[0:1, :], (SUBLANES, D_MODEL))
        G = g_s[...]

        h_prev = _shift_down(jnp.concatenate([hsh_ref[...] * keep, hs], axis=0), 1)[SUBLANES:]
        t_glob = first_row + _row_ids((tt, D_MODEL))
        dmult = jnp.where(t_glob == 0, 0.0, G * ig * xc)
        dla = G * h_prev * a - dmult * (a * a) / mult
        vec_ref[_V_LAM:_V_LAM + 1, :] += _colsum(dla * r) * (-LRU_C) * dsp_dlam
        dpr = dla * (-LRU_C) * sp * r * (1.0 - r)
        dpi = G * mult * xc * ig * (1.0 - ig)
        vec_ref[_V_BRG:_V_BRG + 1, :] += _colsum(dpr)
        vec_ref[_V_BIG:_V_BIG + 1, :] += _colsum(dpi)
        dprb, dpib, xh = dpr.astype(BF16), dpi.astype(BF16), xc.astype(BF16)
        dxc_h = []
        for h in range(LRU_HEADS):
            cols = slice(h * LRU_HEAD_DIM, (h + 1) * LRU_HEAD_DIM)
            dwrg_ref[h] += _tn(xh[:, cols], dprb[:, cols])
            dwig_ref[h] += _tn(xh[:, cols], dpib[:, cols])
            dxc_h.append(_nt(dprb[:, cols], wrg_ref[h]) + _nt(dpib[:, cols], wig_ref[h]))
        dxc = G * mult * ig + jnp.concatenate(dxc_h, axis=1)

        vec_ref[_V_CONVB:_V_CONVB + 1, :] += _colsum(dxc)
        dxc_cat = jnp.concatenate([dxc, dxc_next[...]], axis=0)
        dzl = cw[CONV_WIDTH - 1:CONV_WIDTH] * dxc
        for k in range(CONV_WIDTH):
            lag = CONV_WIDTH - 1 - k
            vec_ref[_V_CONVW + k:_V_CONVW + k + 1, :] += _colsum(dxc * _shift_down(zl_cat, lag)[SUBLANES:])
            if lag:
                dzl = dzl + cw[k:k + 1] * _shift_up(dxc_cat, lag)[:tt]
        dzl_ref[...] = dzl.astype(BF16)
        dxc_next[...] = dxc[:SUBLANES]

    res = [conv_w, conv_b, w_rg, b_rg, w_ig, b_ig, lru_lambda, lru_proj]
    return pl.pallas_call(
        body, name="b2_lru", grid=(nt,),
        in_specs=[_rows_rev(tt, D_MODEL, nt), _rows_rev(tt, D_MODEL, nt), _halo_before_rev(SUBLANES, D_MODEL, tt, nt),
                  _rows_rev(tt, D_MODEL, nt), _rows_rev(tt, D_MODEL, nt), _halo_before_rev(SUBLANES, D_MODEL, tt, nt)]
        + [_resident(w.shape) for w in res],
        out_specs=[_rows_rev(tt, D_MODEL, nt), _rows_rev(tt, D_MODEL, nt), _acc(lru_proj.shape), _acc(w_rg.shape),
                   _acc(w_ig.shape), _acc((SUBLANES, D_MODEL))],
        out_shape=[jax.ShapeDtypeStruct((T, D_MODEL), BF16), jax.ShapeDtypeStruct((T, D_MODEL), BF16),
                   jax.ShapeDtypeStruct(lru_proj.shape, F32), jax.ShapeDtypeStruct(w_rg.shape, F32),
                   jax.ShapeDtypeStruct(w_ig.shape, F32), jax.ShapeDtypeStruct((SUBLANES, D_MODEL), F32)],
        scratch_shapes=[pltpu.VMEM((tt, D_MODEL), F32)] * 3 + [pltpu.VMEM((SUBLANES, D_MODEL), F32)] * 3,
        compiler_params=_params(("arbitrary",)),
    )(dyl, zl, zl, zg, hs, hs, *res)


def _row_tile(rows):
    for t in (512, 256, 128, 64, 32, 16, 8):
        if rows % t == 0:
            return t
    return rows


def _scalar_grid(grid, in_specs, out_specs):
    return pltpu.PrefetchScalarGridSpec(num_scalar_prefetch=1, grid=grid, in_specs=in_specs, out_specs=out_specs)


def _cast_into_block(w, by_rows, shard_j, name):
    R, C = w.shape
    tr = _row_tile(R)
    if by_rows:
        out_shape, out_map = (N_SHARDS * R, C), lambda i, j: (j[0] * (R // tr) + i, 0)
    else:
        out_shape, out_map = (R, N_SHARDS * C), lambda i, j: (i, j[0])

    def body(j_ref, w_ref, o_ref):
        o_ref[...] = w_ref[...].astype(BF16)

    return pl.pallas_call(
        body, name=name,
        grid_spec=_scalar_grid((R // tr,), [pl.BlockSpec((tr, C), lambda i, j: (i, 0))], pl.BlockSpec((tr, C), out_map)),
        out_shape=jax.ShapeDtypeStruct(out_shape, BF16),
        compiler_params=_params(("arbitrary",)),
    )(shard_j.reshape(1), w)


def _sum_cores(g, theirs, core, name):
    S, R, C = g.shape
    H = R // 2
    tr = _row_tile(H)
    nh = H // tr

    def body(c_ref, g_ref, t_ref, o_ref):
        o_ref[...] = (g_ref[...] + t_ref[...]).astype(BF16)

    half = pl.BlockSpec((None, tr, C), lambda s, i, c: (s, i, 0))
    return pl.pallas_call(
        body, name=name,
        grid_spec=_scalar_grid((S, nh), [pl.BlockSpec((None, tr, C), lambda s, i, c: (s, c[0] * nh + i, 0)), half], half),
        out_shape=jax.ShapeDtypeStruct((S, H, C), BF16),
        compiler_params=_params(("arbitrary", "arbitrary")),
    )(core.reshape(1), g, theirs)


def _sum_chips(sums, slots, by_rows, place, name):
    _, H, C = slots.shape
    tr = _row_tile(H)
    own_map = (lambda i, p: (p[0], i, 0)) if by_rows else (lambda i, p: (0, i, p[0]))

    def body(p_ref, s_ref, q_ref, o_ref):
        o_ref[...] = ((s_ref[...].astype(F32) + q_ref[0].astype(F32)) + q_ref[1].astype(F32)) + q_ref[2].astype(F32)

    return pl.pallas_call(
        body, name=name,
        grid_spec=_scalar_grid(
            (H // tr,),
            [pl.BlockSpec((None, tr, C), own_map), pl.BlockSpec((3, tr, C), lambda i, p: (0, i, 0))],
            pl.BlockSpec((None, tr, C), lambda i, p: (p[1], i, 0))),
        out_shape=jax.ShapeDtypeStruct((2, H, C), F32),
        compiler_params=_params(("arbitrary",)),
    )(place, sums, slots)


def _sum_slots(q, name):
    S, R, C = q.shape
    tr = _row_tile(R)

    def body(q_ref, o_ref):
        acc = q_ref[0]
        for s in range(1, S):
            acc = acc + q_ref[s]
        o_ref[...] = acc

    return pl.pallas_call(
        body, name=name, grid=(R // tr,),
        in_specs=[pl.BlockSpec((S, tr, C), lambda i: (0, i, 0))], out_specs=_rows(tr, C),
        out_shape=jax.ShapeDtypeStruct((R, C), F32),
        compiler_params=_params(("arbitrary",)),
    )(q)


def _adamw(w, g, m, v, name):
    R, C = w.shape
    tr = _row_tile(R)
    c1 = 1.0 - ADAM_B1 ** ADAM_STEP
    c2 = 1.0 - ADAM_B2 ** ADAM_STEP

    def body(w_ref, g_ref, m_ref, v_ref, d_ref, nm_ref, nv_ref):
        gv = g_ref[...]
        nm = ADAM_B1 * m_ref[...] + (1.0 - ADAM_B1) * gv
        nv = ADAM_B2 * v_ref[...] + (1.0 - ADAM_B2) * (gv * gv)
        d_ref[...] = -ADAM_LR * ((nm / c1) / (jnp.sqrt(nv / c2) + ADAM_EPS) + ADAM_WD * w_ref[...])
        nm_ref[...] = nm
        nv_ref[...] = nv

    return pl.pallas_call(
        body, name=name, grid=(R // tr,),
        in_specs=[_rows(tr, C)] * 4, out_specs=[_rows(tr, C)] * 3,
        out_shape=[jax.ShapeDtypeStruct((R, C), F32)] * 3,
        compiler_params=_params(("arbitrary",)),
    )(w, g, m, v)


def _place():
    return lax.axis_index("x"), lax.axis_index("y"), lax.axis_index("c")


def _other_chips(x, y):
    return [(1 - x, y), (x, 1 - y), (1 - x, 1 - y)]


def _shard_block(ref, by_rows, R, C, j, half_rows=None):
    if half_rows is None:
        rows, r0 = R, 0
    else:
        rows = R // 2
        r0 = pl.multiple_of(half_rows * rows, 16)
    if by_rows:
        return ref.at[pl.ds(pl.multiple_of(j * R, 16) + r0, rows), :]
    return ref.at[pl.ds(r0, rows), pl.ds(pl.multiple_of(j * C, 128), C)]


def _all_gather_weights(gathered, shapes, by_rows, small):
    n = len(gathered)

    def body(*refs):
        small_in = refs[n]
        outs, small_out = refs[n + 1:2 * n + 1], refs[2 * n + 1]
        send_sems, recv_sems, local_sem = refs[2 * n + 2:]
        x, y, c = _place()
        me_j = 2 * x + y
        chips = _other_chips(x, y)
        sibling = (x, y, 1 - c)

        def block(i, j, half):
            R, C = shapes[i]
            return _shard_block(outs[i], by_rows[i], R, C, j, half)

        def ici(i, k, src_j):
            return pltpu.make_async_remote_copy(
                src_ref=block(i, src_j, c), dst_ref=block(i, src_j, c),
                send_sem=send_sems.at[6 * i + k], recv_sem=recv_sems.at[6 * i + k],
                device_id=(*chips[k], c), device_id_type=MESH)

        def relay(i, k, half):
            kj = 2 * chips[k][0] + chips[k][1]
            return pltpu.make_async_remote_copy(
                src_ref=block(i, kj, half), dst_ref=block(i, kj, half),
                send_sem=send_sems.at[6 * i + 3 + k], recv_sem=recv_sems.at[6 * i + 3 + k],
                device_id=sibling, device_id_type=MESH)

        def small_copy(k, src_j):
            cols = pl.ds(pl.multiple_of(src_j * 256, 128), 256)
            return pltpu.make_async_remote_copy(
                src_ref=small_in, dst_ref=small_out.at[:, cols],
                send_sem=send_sems.at[6 * n + k], recv_sem=recv_sems.at[6 * n + k],
                device_id=(*chips[k], c), device_id_type=MESH)

        sends = []
        for i in range(n):
            for k in range(3):
                cp = ici(i, k, me_j)
                cp.start()
                sends.append(cp)
        for k in range(3):
            cp = small_copy(k, me_j)
            cp.start()
            sends.append(cp)
        local = pltpu.make_async_copy(small_in, small_out.at[:, pl.ds(pl.multiple_of(me_j * 256, 128), 256)], local_sem)
        local.start()
        for i in range(n):
            for k in range(3):
                kj = 2 * chips[k][0] + chips[k][1]
                ici(i, k, kj).wait_recv()
                cp = relay(i, k, c)
                cp.start()
                sends.append(cp)
        for k in range(3):
            small_copy(k, 2 * chips[k][0] + chips[k][1]).wait_recv()
        for i in range(n):
            for k in range(3):
                relay(i, k, 1 - c).wait_recv()
        for cp in sends:
            cp.wait_send()
        local.wait()

    out_shape = [jax.ShapeDtypeStruct(g.shape, BF16) for g in gathered]
    out_shape.append(jax.ShapeDtypeStruct((8, N_SHARDS * 256), F32))
    n_sems = 6 * n + 3
    return pl.pallas_call(
        body, name="all_gather_weights",
        in_specs=[ANY] * (n + 1), out_specs=[ANY] * (n + 1), out_shape=out_shape,
        input_output_aliases={i: i for i in range(n)},
        scratch_shapes=[pltpu.SemaphoreType.DMA((n_sems,)), pltpu.SemaphoreType.DMA((n_sems,)),
                        pltpu.SemaphoreType.DMA],
    )(*gathered, small)


def _core_exchange(grads, name):
    n = len(grads)

    def body(*refs):
        ins, theirs = refs[:n], refs[n:2 * n]
        send_sems, recv_sems = refs[2 * n:]
        x, y, c = _place()
        copies = []
        for i in range(n):
            H = grads[i].shape[1] // 2
            cp = pltpu.make_async_remote_copy(
                src_ref=ins[i].at[:, pl.ds(pl.multiple_of((1 - c) * H, 8), H), :], dst_ref=theirs[i],
                send_sem=send_sems.at[i], recv_sem=recv_sems.at[i],
                device_id=(x, y, 1 - c), device_id_type=MESH)
            cp.start()
            copies.append(cp)
        for cp in copies:
            cp.wait()

    return pl.pallas_call(
        body, name=name,
        in_specs=[ANY] * n, out_specs=[ANY] * n,
        out_shape=[jax.ShapeDtypeStruct((g.shape[0], g.shape[1] // 2, g.shape[2]), F32) for g in grads],
        scratch_shapes=[pltpu.SemaphoreType.DMA((n,))] * 2,
    )(*grads)


def _chip_exchange(sums, by_rows):
    n = len(sums)
    dims = [(s.shape[1], s.shape[2]) if by_rows[i] else (s.shape[1], s.shape[2] // N_SHARDS)
            for i, s in enumerate(sums)]

    def body(*refs):
        ins, outs = refs[:n], refs[n:2 * n]
        send_sems, recv_sems = refs[2 * n:]
        x, y, c = _place()
        chips = _other_chips(x, y)

        def shard(i, j):
            if by_rows[i]:
                return ins[i].at[j]
            return ins[i].at[0, :, pl.ds(pl.multiple_of(j * dims[i][1], 128), dims[i][1])]

        copies = []
        for i in range(n):
            for k in range(3):
                kj = 2 * chips[k][0] + chips[k][1]
                cp = pltpu.make_async_remote_copy(
                    src_ref=shard(i, kj), dst_ref=outs[i].at[k],
                    send_sem=send_sems.at[3 * i + k], recv_sem=recv_sems.at[3 * i + k],
                    device_id=(*chips[k], c), device_id_type=MESH)
                cp.start()
                copies.append(cp)
        for cp in copies:
            cp.wait()

    return pl.pallas_call(
        body, name="grad_chip_exchange",
        in_specs=[ANY] * n, out_specs=[ANY] * n,
        out_shape=[jax.ShapeDtypeStruct((3, h, cc), BF16) for h, cc in dims],
        scratch_shapes=[pltpu.SemaphoreType.DMA((3 * n,))] * 2,
    )(*sums)


HBM = pl.BlockSpec(memory_space=pltpu.HBM)
SEM = pl.BlockSpec(memory_space=pltpu.SEMAPHORE)
TOKEN = jax.ShapeDtypeStruct((SUBLANES, 128), F32)


def _in_hbm(a):
    return pltpu.with_memory_space_constraint(a, pltpu.HBM)


def _split_params():
    return pltpu.CompilerParams(has_side_effects=pltpu.SideEffectType.DATAFLOW_SIDE_EFFECTING)


def _gather_rest_copies(refs, shapes, by_rows, send_sems, recv_sems):
    x, y, c = _place()
    me_j = 2 * x + y
    chips = _other_chips(x, y)
    pairs = []
    for i, ref in enumerate(refs):
        R, C = shapes[i]
        for k in range(3):
            kj = 2 * chips[k][0] + chips[k][1]

            def copy(j, ref=ref, i=i, k=k, R=R, C=C):
                blk = _shard_block(ref, by_rows[i], R, C, j)
                return pltpu.make_async_remote_copy(
                    src_ref=blk, dst_ref=blk, send_sem=send_sems.at[3 * i + k], recv_sem=recv_sems.at[3 * i + k],
                    device_id=(*chips[k], c), device_id_type=MESH)

            pairs.append((copy(me_j), copy(kj)))
    return pairs


def _gather_rest_start(gathered, shapes, by_rows, after):
    n = len(gathered)

    def body(*refs):
        ins = refs[:n]
        send_sems, recv_sems = refs[n + 1], refs[n + 2]
        token = refs[-1]
        for mine, _ in _gather_rest_copies(ins, shapes, by_rows, send_sems, recv_sems):
            mine.start()
        token[...] = jnp.zeros_like(token)

    out = pl.pallas_call(
        body, name="gather_rest_start",
        out_shape=(pltpu.SemaphoreType.DMA((3 * n,)), pltpu.SemaphoreType.DMA((3 * n,)),
                   *[pltpu.HBM(g.shape, g.dtype) for g in gathered], TOKEN),
        in_specs=[HBM] * n + [ANY], out_specs=(SEM, SEM, *[HBM] * n, pl.BlockSpec(memory_space=pltpu.VMEM)),
        input_output_aliases={i: 2 + i for i in range(n)},
        compiler_params=_split_params(),
    )(*[_in_hbm(g) for g in gathered], after)
    return out[0], out[1], out[2:2 + n], out[-1]


def _gather_rest_wait(send_sems, recv_sems, gathered, shapes, by_rows, after):
    n = len(gathered)

    def body(*refs):
        ins = refs[:n]
        send, recv = refs[n], refs[n + 1]
        for mine, theirs in _gather_rest_copies(ins, shapes, by_rows, send, recv):
            mine.wait_send()
            theirs.wait_recv()

    return pl.pallas_call(
        body, name="gather_rest_wait",
        out_shape=tuple(pltpu.HBM(g.shape, g.dtype) for g in gathered),
        in_specs=[HBM] * n + [SEM, SEM, ANY], out_specs=tuple([HBM] * n),
        input_output_aliases={i: i for i in range(n)},
        compiler_params=_split_params(),
    )(*gathered, send_sems, recv_sems, after)


def _chip_exchange_copies(ins, slots, dims, by_rows, send_sems, recv_sems):
    x, y, c = _place()
    chips = _other_chips(x, y)
    pairs = []
    for i in range(len(ins)):
        for k in range(3):
            kj = 2 * chips[k][0] + chips[k][1]
            if by_rows[i]:
                src = ins[i].at[kj]
            else:
                src = ins[i].at[0, :, pl.ds(pl.multiple_of(kj * dims[i][1], 128), dims[i][1])]
            cp = pltpu.make_async_remote_copy(
                src_ref=src, dst_ref=slots[i].at[k], send_sem=send_sems.at[3 * i + k], recv_sem=recv_sems.at[3 * i + k],
                device_id=(*chips[k], c), device_id_type=MESH)
            pairs.append((cp, cp))
    return pairs


def _exchange_dims(sums, by_rows):
    return [(s.shape[1], s.shape[2]) if by_rows[i] else (s.shape[1], s.shape[2] // N_SHARDS) for i, s in enumerate(sums)]


def _chip_exchange_start(sums, by_rows):
    n = len(sums)
    sums = list(sums)
    dims = _exchange_dims(sums, by_rows)
    slots = [lax.empty((3, h, cc), BF16) for h, cc in dims]

    def body(*refs):
        ins, land = refs[:n], refs[n:2 * n]
        send_sems, recv_sems = refs[2 * n], refs[2 * n + 1]
        token = refs[-1]
        for cp, _ in _chip_exchange_copies(ins, land, dims, by_rows, send_sems, recv_sems):
            cp.start()
        token[...] = jnp.zeros_like(token)

    out = pl.pallas_call(
        body, name="grad_chip_exchange_start",
        out_shape=(pltpu.SemaphoreType.DMA((3 * n,)), pltpu.SemaphoreType.DMA((3 * n,)),
                   *[pltpu.HBM(a.shape, a.dtype) for a in sums + slots], TOKEN),
        in_specs=[HBM] * (2 * n), out_specs=(SEM, SEM, *[HBM] * (2 * n), pl.BlockSpec(memory_space=pltpu.VMEM)),
        input_output_aliases={i: 2 + i for i in range(2 * n)},
        compiler_params=_split_params(),
    )(*[_in_hbm(a) for a in sums + slots])
    return out[0], out[1], out[2:2 + n], out[2 + n:2 + 2 * n], out[-1]


def _chip_exchange_wait(send_sems, recv_sems, sums, slots, by_rows, after):
    n = len(sums)
    sums, slots = list(sums), list(slots)
    dims = _exchange_dims(sums, by_rows)

    def body(*refs):
        ins, land = refs[:n], refs[n:2 * n]
        send, recv = refs[2 * n], refs[2 * n + 1]
        for cp, _ in _chip_exchange_copies(ins, land, dims, by_rows, send, recv):
            cp.wait_send()
            cp.wait_recv()

    out = pl.pallas_call(
        body, name="grad_chip_exchange_wait",
        out_shape=tuple(pltpu.HBM(a.shape, a.dtype) for a in sums + slots),
        in_specs=[HBM] * (2 * n) + [SEM, SEM, ANY], out_specs=tuple([HBM] * (2 * n)),
        input_output_aliases={i: i for i in range(2 * n)},
        compiler_params=_split_params(),
    )(*sums, *slots, send_sems, recv_sems, after)
    return out[:n], out[n:]


def _core_share(reduced):
    n = len(reduced)

    def body(*refs):
        outs = refs[n:2 * n]
        send_sems, recv_sems = refs[2 * n:]
        x, y, c = _place()
        copies = []
        for i in range(n):
            cp = pltpu.make_async_remote_copy(
                src_ref=outs[i].at[c], dst_ref=outs[i].at[c], send_sem=send_sems.at[i], recv_sem=recv_sems.at[i],
                device_id=(x, y, 1 - c), device_id_type=MESH)
            cp.start()
            copies.append(cp)
        for cp in copies:
            cp.wait()

    return pl.pallas_call(
        body, name="grad_core_share",
        in_specs=[ANY] * n, out_specs=[ANY] * n,
        out_shape=[jax.ShapeDtypeStruct(r.shape, F32) for r in reduced],
        input_output_aliases={i: i for i in range(n)},
        scratch_shapes=[pltpu.SemaphoreType.DMA((n,))] * 2,
    )(*reduced)


def _exchange_small(pack):
    def body(in_ref, out_ref, send_sems, recv_sems, local_sem):
        x, y, c = _place()
        sibling = (x, y, 1 - c)
        chips = _other_chips(x, y)

        def slot(px, py, pc):
            return out_ref.at[4 * px + 2 * py + pc]

        def copy(k, src, sender, to):
            return pltpu.make_async_remote_copy(
                src_ref=src, dst_ref=slot(*sender), send_sem=send_sems.at[k], recv_sem=recv_sems.at[k],
                device_id=to, device_id_type=MESH)

        me = (x, y, c)
        mine = pltpu.make_async_copy(in_ref, slot(*me), local_sem)
        mine.start()
        sends = [copy(0, in_ref, me, sibling)] + [copy(1 + k, in_ref, me, (*chips[k], c)) for k in range(3)]
        for cp in sends:
            cp.start()
        for k in range(3):
            copy(1 + k, in_ref, (*chips[k], c), me).wait_recv()
            cp = copy(4 + k, slot(*chips[k], c), (*chips[k], c), sibling)
            cp.start()
            sends.append(cp)
        copy(0, in_ref, sibling, me).wait_recv()
        for k in range(3):
            copy(4 + k, in_ref, (*chips[k], 1 - c), me).wait_recv()
        for cp in sends:
            cp.wait_send()
        mine.wait()

    return pl.pallas_call(
        body, name="grad_small_exchange",
        in_specs=[pl.BlockSpec(memory_space=pltpu.VMEM)], out_specs=ANY,
        out_shape=jax.ShapeDtypeStruct((N_DEV,) + pack.shape, F32),
        scratch_shapes=[pltpu.SemaphoreType.DMA((N_DEV - 1,)), pltpu.SemaphoreType.DMA((N_DEV - 1,)),
                        pltpu.SemaphoreType.DMA],
    )(pack)


def _pack_rows(parts, rows):
    flat = jnp.concatenate([a.reshape(-1) for a in parts])
    return jnp.pad(flat, (0, rows * 128 - flat.shape[0])).reshape(rows, 128)


def _unpack_rows(pack, shapes):
    flat = pack.reshape(-1)
    out, at = [], 0
    for s in shapes:
        size = 1
        for d in s:
            size *= d
        out.append(flat[at:at + size].reshape(s))
        at += size
    return out


def kernel(x, p, norm1_g, w_in, b_gate, pool_w, pool_scale, pool_proj, conv_w, conv_b, w_rg, b_rg, w_ig, b_ig, lru_lambda, lru_proj, w_out, norm2_g, w_ffn_in, w_ffn_out, ple_norm_g, w_ple_gate, w_ple_proj, final_g, loss_target, m_norm1_g, m_w_in, m_b_gate, m_pool_w, m_pool_scale, m_pool_proj, m_conv_w, m_conv_b, m_w_rg, m_b_rg, m_w_ig, m_b_ig, m_lru_lambda, m_lru_proj, m_w_out, m_norm2_g, m_w_ffn_in, m_w_ffn_out, m_ple_norm_g, m_w_ple_gate, m_w_ple_proj, m_final_g, v_norm1_g, v_w_in, v_b_gate, v_pool_w, v_pool_scale, v_pool_proj, v_conv_w, v_conv_b, v_w_rg, v_b_rg, v_w_ig, v_b_ig, v_lru_lambda, v_lru_proj, v_w_out, v_norm2_g, v_w_ffn_in, v_w_ffn_out, v_ple_norm_g, v_w_ple_gate, v_w_ple_proj, v_final_g):
    weights = dict(norm1_g=norm1_g, w_in=w_in, b_gate=b_gate, pool_w=pool_w, pool_scale=pool_scale,
                   pool_proj=pool_proj, conv_w=conv_w, conv_b=conv_b, w_rg=w_rg, b_rg=b_rg, w_ig=w_ig, b_ig=b_ig,
                   lru_lambda=lru_lambda, lru_proj=lru_proj, w_out=w_out, norm2_g=norm2_g, w_ffn_in=w_ffn_in,
                   w_ffn_out=w_ffn_out, ple_norm_g=ple_norm_g, w_ple_gate=w_ple_gate, w_ple_proj=w_ple_proj,
                   final_g=final_g)
    m_in = dict(norm1_g=m_norm1_g, w_in=m_w_in, b_gate=m_b_gate, pool_w=m_pool_w, pool_scale=m_pool_scale,
                pool_proj=m_pool_proj, conv_w=m_conv_w, conv_b=m_conv_b, w_rg=m_w_rg, b_rg=m_b_rg, w_ig=m_w_ig,
                b_ig=m_b_ig, lru_lambda=m_lru_lambda, lru_proj=m_lru_proj, w_out=m_w_out, norm2_g=m_norm2_g,
                w_ffn_in=m_w_ffn_in, w_ffn_out=m_w_ffn_out, ple_norm_g=m_ple_norm_g, w_ple_gate=m_w_ple_gate,
                w_ple_proj=m_w_ple_proj, final_g=m_final_g)
    v_in = dict(norm1_g=v_norm1_g, w_in=v_w_in, b_gate=v_b_gate, pool_w=v_pool_w, pool_scale=v_pool_scale,
                pool_proj=v_pool_proj, conv_w=v_conv_w, conv_b=v_conv_b, w_rg=v_w_rg, b_rg=v_b_rg, w_ig=v_w_ig,
                b_ig=v_b_ig, lru_lambda=v_lru_lambda, lru_proj=v_lru_proj, w_out=v_w_out, norm2_g=v_norm2_g,
                w_ffn_in=v_w_ffn_in, w_ffn_out=v_w_ffn_out, ple_norm_g=v_ple_norm_g, w_ple_gate=v_w_ple_gate,
                w_ple_proj=v_w_ple_proj, final_g=v_final_g)
    names = list(weights)
    big = ["w_in", "pool_proj", "lru_proj", "w_out", "w_ffn_in", "w_ffn_out", "w_ple_gate", "w_ple_proj"]
    by_rows = [n in ("lru_proj", "w_out", "w_ffn_out", "w_ple_gate") for n in big]
    small = [n for n in names if n not in big]

    shard_j = 2 * lax.axis_index("x") + lax.axis_index("y")
    T = x.shape[1]
    xs, ps, tgt = x[0], p[0, 0], loss_target[0]

    small_local = jnp.concatenate([b_gate[0], conv_w[0], jnp.zeros((2, 256), F32)], axis=0)
    core = lax.axis_index("c").astype(jnp.int32)
    place = jnp.stack([shard_j, core]).astype(jnp.int32)
    rows_of = dict(zip(big, by_rows))
    shard_shape = {n: weights[n].shape[1:] for n in big}
    blocks = {n: _cast_into_block(weights[n][0], rows_of[n], place[0], "cast_" + n) for n in big}
    early, late = big[:4], big[4:]
    gathered = _all_gather_weights([blocks[n] for n in early], [shard_shape[n] for n in early],
                                   [rows_of[n] for n in early], small_local)
    full = dict(zip(early, gathered[:-1]))
    late_send, late_recv, late_bufs, late_token = _gather_rest_start(
        [blocks[n] for n in late], [shard_shape[n] for n in late], [rows_of[n] for n in late], gathered[-1])
    b_gate_full = gathered[-1][0:2].reshape(1, 2 * D_MODEL)
    conv_w_full = gathered[-1][2:6]
    pool_w_b, w_rg_b, w_ig_b = pool_w[0].astype(BF16), w_rg[0].astype(BF16), w_ig[0].astype(BF16)
    b_rg_row, b_ig_row = b_rg.reshape(1, D_MODEL), b_ig.reshape(1, D_MODEL)
    final_row = final_g.reshape(1, D_MODEL)

    zp, zl, zg, zt, u, h1, hs, yp, yl = _f12_mixer(
        xs, norm1_g + late_token[0, 0], full["w_in"], b_gate_full, pool_w_b, pool_scale, full["pool_proj"],
        conv_w_full, conv_b, w_rg_b, b_rg_row, w_ig_b, b_ig_row, lru_lambda, full["lru_proj"], full["w_out"])
    full.update(zip(late, _gather_rest_wait(late_send, late_recv, late_bufs, [shard_shape[n] for n in late],
                                            [rows_of[n] for n in late], h1)))
    h2, v, ff, act = _f3_ffn(h1, norm2_g, full["w_ffn_in"], full["w_ffn_out"])

    loss_sum, dh2, g_ple_gate, g_ple_proj, vec4 = _b4_ple_loss(
        h2, ps, tgt, ple_norm_g, full["w_ple_gate"], full["w_ple_proj"], final_row)
    dff, dh1, vec3 = _b3_ffn(dh2, h1, ff, norm2_g, full["w_ffn_in"], full["w_ffn_out"])
    g_ffn_in = _wgrad(v, dff, 2 * D_FF // N_SHARDS, "wgrad_ffn_in")
    g_ffn_out = _wgrad(act, dh2, D_MODEL, "wgrad_ffn_out", tokens=WGRAD_TOKENS // 2)

    def stack(n, g):
        return g.reshape(N_SHARDS, g.shape[0] // N_SHARDS, g.shape[1]) if rows_of[n] else g[None]

    def chip_sums_of(group, grads_of, tag):
        stacked = [stack(n, grads_of[n]) for n in group]
        theirs = _core_exchange(stacked, "grad_core_exchange_" + tag)
        return [_sum_cores(g, t, core, "sum_cores_" + n) for g, t, n in zip(stacked, theirs, group)]

    late_rows = [rows_of[n] for n in late]
    late_sums = chip_sums_of(late, dict(w_ffn_in=g_ffn_in, w_ffn_out=g_ffn_out, w_ple_gate=g_ple_gate,
                                        w_ple_proj=g_ple_proj), "late")
    ex_send, ex_recv, late_sums, late_slots, ex_token = _chip_exchange_start(late_sums, late_rows)
    dzt, dyp, dyl, g_w_out, vec_g = _b2_gates(dh1, zt, yp, yl, b_gate_full + ex_token[0, 0], full["w_out"])
    dzl, dzg, g_lru_proj, g_w_rg, g_w_ig, vec_l = _b2_lru(
        dyl, zl, zg, hs, conv_w_full, conv_b, w_rg_b, b_rg_row, w_ig_b, b_ig_row, lru_lambda, full["lru_proj"])
    dzp, grad_x, g_pool_proj, g_pool_w, vec_p = _b12_pool_in_proj(
        dyp, zp, dzl, dzg, dzt, xs, dh1, norm1_g, full["w_in"], pool_w_b, pool_scale, full["pool_proj"])
    g_w_in = jnp.concatenate([
        _wgrad(u, dzp, POOL_WIDTH, "wgrad_in_pool"), _wgrad(u, dzl, D_MODEL, "wgrad_in_lru"),
        _wgrad(u, dzg, D_MODEL, "wgrad_in_gelu"), _wgrad(u, dzt, D_MODEL, "wgrad_in_gate")], axis=1)

    loss = lax.psum(loss_sum[0, 0] * (0.5 / D_MODEL), ("x", "y", "c"))

    early_rows = [rows_of[n] for n in early]
    early_sums = chip_sums_of(early, dict(w_in=g_w_in, pool_proj=g_pool_proj, lru_proj=g_lru_proj, w_out=g_w_out),
                              "early")
    early_slots = _chip_exchange(early_sums, early_rows)
    late_sums, late_slots = _chip_exchange_wait(ex_send, ex_recv, late_sums, late_slots, late_rows, g_w_in)
    reduced = _core_share([_sum_chips(s, q, rows_of[n], place, "sum_chips_" + n)
                           for s, q, n in zip(list(early_sums) + list(late_sums),
                                              list(early_slots) + list(late_slots), big)])

    grads, deltas, new_m, new_v = {}, {}, {}, {}
    for n, r in zip(big, reduced):
        g = r.reshape(r.shape[0] * r.shape[1], r.shape[2])
        d, nm, nv = _adamw(weights[n][0], g, m_in[n][0], v_in[n][0], "adamw_" + n)
        grads[n], deltas[n], new_m[n], new_v[n] = g[None], d[None], nm[None], nv[None]

    small_full = dict(
        norm1_g=vec_p[1], b_gate=vec_g[0:2], pool_w=g_pool_w, pool_scale=vec_p[0, :POOL_WIDTH],
        conv_w=vec_l[_V_CONVW:_V_CONVW + CONV_WIDTH], conv_b=vec_l[_V_CONVB], w_rg=g_w_rg, b_rg=vec_l[_V_BRG],
        w_ig=g_w_ig, b_ig=vec_l[_V_BIG], lru_lambda=vec_l[_V_LAM], norm2_g=vec3[0], ple_norm_g=vec4[1],
        final_g=vec4[0])
    full_shapes = [small_full[n].shape for n in small]
    n_full = sum(int(small_full[n].size) for n in small)
    rows_full = -(-n_full // (128 * ROW_TILE)) * ROW_TILE
    everyone = _exchange_small(_pack_rows([small_full[n] for n in small], rows_full))
    summed = dict(zip(small, _unpack_rows(_sum_slots(everyone, "sum_small"), full_shapes)))
    summed["b_gate"] = lax.dynamic_slice_in_dim(summed["b_gate"], shard_j * 256, 256, axis=1)
    summed["conv_w"] = lax.dynamic_slice_in_dim(summed["conv_w"], shard_j * 256, 256, axis=1)
    local_shapes = [weights[n].shape for n in small]
    n_local = sum(int(weights[n].size) for n in small)
    rows_local = -(-n_local // (128 * ROW_TILE)) * ROW_TILE
    packs = [_pack_rows([src[n] for n in small], rows_local) for src in (weights, summed, m_in, v_in)]
    d_s, nm_s, nv_s = _adamw(*packs, "adamw_small")
    for dst, pack in ((grads, packs[1]), (deltas, d_s), (new_m, nm_s), (new_v, nv_s)):
        dst.update(zip(small, _unpack_rows(pack, local_shapes)))

    return (loss, grad_x[None], *[grads[n] for n in names], *[deltas[n] for n in names],
            *[new_m[n] for n in names], *[new_v[n] for n in names])
```

```python
import functools

import jax
import jax.numpy as jnp
from jax import lax
from jax.experimental import pallas as pl
from jax.experimental.pallas import tpu as pltpu

F32 = jnp.float32
BF16 = jnp.bfloat16

D_MODEL = 1024
POOL_WIDTH = 512
POOL_GROUP_DIM = 128
POOL_WINDOWS = (2, 4, 8, 16)
POOL_HALO = 16
LRU_HEADS = 8
LRU_HEAD_DIM = 128
CONV_WIDTH = 4
LRU_C = 8.0
D_FF = 2816
PLE_DIM = 256
RMS_EPS = 1e-6
N_SHARDS = 4
N_DEV = 8

ADAM_LR = 0.001
ADAM_B1 = 0.9
ADAM_B2 = 0.999
ADAM_EPS = 1e-08
ADAM_WD = 0.01
ADAM_STEP = 10

ROW_TILE = 256
WIDE_TILE = 512
WGRAD_TOKENS = 2048
SUBLANES = 8
VMEM_LIMIT = 56 * 1024 * 1024
MESH = pl.DeviceIdType.MESH
ANY = pl.BlockSpec(memory_space=pl.ANY)


def _params(semantics=None):
    return pltpu.CompilerParams(dimension_semantics=semantics, vmem_limit_bytes=VMEM_LIMIT)


def _resident(shape):
    n = len(shape)
    return pl.BlockSpec(shape, lambda *_: (0,) * n, pipeline_mode=pl.Buffered(1))


def _acc(shape):
    n = len(shape)
    return pl.BlockSpec(shape, lambda *_: (0,) * n)


def _rows(tile, cols):
    return pl.BlockSpec((tile, cols), lambda i: (i, 0))


def _rows_rev(tile, cols, n_tiles):
    return pl.BlockSpec((tile, cols), lambda i: (n_tiles - 1 - i, 0))


def _halo_before_rev(rows, cols, tile, n_tiles):
    per = tile // rows
    return pl.BlockSpec((rows, cols), lambda i: (jnp.maximum((n_tiles - 1 - i) * per - 1, 0), 0))


def _nn(a, b):
    return jnp.dot(a, b, preferred_element_type=F32)


def _nt(a, b):
    return lax.dot_general(a, b, (((1,), (1,)), ((), ())), preferred_element_type=F32)


def _tn(a, b):
    return lax.dot_general(a, b, (((0,), (0,)), ((), ())), preferred_element_type=F32)


def _rms(x):
    r = lax.rsqrt(jnp.mean(x * x, axis=-1, keepdims=True) + RMS_EPS)
    return x * r, r


def _rms_bwd(dn, n, r):
    return r * (dn - n * jnp.mean(dn * n, axis=-1, keepdims=True))


def _sigmoid(x):
    return 0.5 * jnp.tanh(0.5 * x) + 0.5


_GELU_C = 0.7978845608028654
_GELU_A = 0.044715


def _gelu(x):
    t = jnp.tanh(_GELU_C * (x + _GELU_A * x * x * x))
    return 0.5 * x * (1.0 + t)


def _gelu_and_grad(x):
    x2 = x * x
    t = jnp.tanh(_GELU_C * (x + _GELU_A * x2 * x))
    cdf = 0.5 * (1.0 + t)
    grad = cdf + 0.5 * x * (1.0 - t * t) * _GELU_C * (1.0 + 3.0 * _GELU_A * x2)
    return x * cdf, grad


def _softplus_neg(lam):
    e = jnp.exp(-jnp.abs(lam))
    sp = jnp.maximum(-lam, 0.0) + jnp.log1p(e)
    return sp, -_sigmoid(-lam)


def _colsum(v):
    return jnp.sum(v, axis=0, keepdims=True)


def _row_ids(shape):
    return lax.broadcasted_iota(jnp.int32, shape, 0)


def _shift_down(cat, k):
    return pltpu.roll(cat, k, 0) if k else cat


def _shift_up(cat, k):
    return pltpu.roll(cat, cat.shape[0] - k, 0) if k else cat


IN_SPLITS = (0, POOL_WIDTH, POOL_WIDTH + D_MODEL, POOL_WIDTH + 2 * D_MODEL, POOL_WIDTH + 4 * D_MODEL)
IN_WIDTHS = tuple(IN_SPLITS[k + 1] - IN_SPLITS[k] for k in range(4))
PROJ_CHUNK = 256


def _no_tick():
    pass


class _Interleaved:
    def __init__(self, pieces):
        self._pieces = iter(pieces)

    def tick(self, n=1):
        for _ in range(n):
            piece = next(self._pieces, None)
            if piece is not None:
                piece()

    def flush(self):
        for piece in self._pieces:
            piece()


def _pool_forward(zp_cat, pw_ref, first_row, tick=_no_tick):
    tt = zp_cat.shape[0] - POOL_HALO
    t_glob = first_row + _row_ids((tt, POOL_GROUP_DIM))
    pooled, mixed = [], []
    for g, w in enumerate(POOL_WINDOWS):
        cat = zp_cat[:, g * POOL_GROUP_DIM:(g + 1) * POOL_GROUP_DIM]
        s, k = cat, 1
        while k < w:
            s = s + _shift_down(s, k)
            k *= 2
        cnt = jnp.minimum(t_glob + 1, w).astype(F32)
        pg = s[POOL_HALO:] / cnt - cat[POOL_HALO:]
        pooled.append(pg)
        mixed.append(_nn(pg.astype(BF16), pw_ref[g]))
        tick()
    return jnp.concatenate(pooled, axis=1), jnp.concatenate(mixed, axis=1)


def _lru_gates(zl_cat, conv_w, conv_b, wrg_ref, brg, wig_ref, big, sp, first_row, tick=_no_tick):
    tt = zl_cat.shape[0] - SUBLANES
    xc = conv_w[CONV_WIDTH - 1:CONV_WIDTH] * zl_cat
    for k in range(1, CONV_WIDTH):
        xc = xc + conv_w[CONV_WIDTH - 1 - k:CONV_WIDTH - k] * _shift_down(zl_cat, k)
        tick()
    xc = xc[SUBLANES:] + conv_b
    xh = xc.astype(BF16)
    pr, pi = [], []
    for h in range(LRU_HEADS):
        xs = xh[:, h * LRU_HEAD_DIM:(h + 1) * LRU_HEAD_DIM]
        pr.append(_nn(xs, wrg_ref[h]))
        pi.append(_nn(xs, wig_ref[h]))
    r = _sigmoid(jnp.concatenate(pr, axis=1) + brg)
    tick()
    ig = _sigmoid(jnp.concatenate(pi, axis=1) + big)
    tick()
    a, mult = _decay(r, sp, first_row, tick)
    tick()
    return xc, r, ig, a, mult


def _decay(r, sp, first_row, tick=_no_tick):
    a = jnp.exp(-LRU_C * r * sp)
    tick()
    mult = jnp.sqrt(jnp.maximum(1.0 - a * a, 0.0))
    t_glob = first_row + _row_ids(r.shape)
    return a, jnp.where(t_glob == 0, 1.0, mult)


def _f12_mixer(x, norm1_g, w_in, b_gate, pool_w, pool_scale, pool_proj, conv_w, conv_b, w_rg, b_rg, w_ig, b_ig,
               lru_lambda, lru_proj, w_out):
    T = x.shape[0]
    tt = ROW_TILE
    nt = T // tt
    n_groups = tt // SUBLANES
    proj_mid = IN_SPLITS[3] + D_MODEL // 2

    def body(xm_ref, x_ref, g1_ref, win_ref, bg_ref, pw_ref, ps_ref, pp_ref, cw_ref, cb_ref,
             wrg_ref, brg_ref, wig_ref, big_ref, lam_ref, lp_ref, wo_ref,
             zp_ref, zl_ref, zg_ref, zt_ref, u_ref, h1_ref, hs_ref, yp_ref, yl_ref, xc_ref, r_ref, ig_ref,
             zbuf, zp_halo, zl_halo, a_s, b_s, carry_s):
        s = pl.program_id(0)

        @pl.when(s == 0)
        def _():
            zbuf[1] = jnp.zeros((tt, IN_SPLITS[4]), F32)
            zp_halo[...] = jnp.zeros_like(zp_halo)
            zl_halo[...] = jnp.zeros_like(zl_halo)
            carry_s[...] = jnp.zeros_like(carry_s)

        z_new, z_old = zbuf.at[s % 2], zbuf.at[(s + 1) % 2]
        first = s <= 1
        first_row = jnp.maximum(s - 1, 0) * tt

        n1, _ = _rms(xm_ref[...])
        u = (n1 * g1_ref[...]).astype(BF16)
        u_ref[...] = u

        z_refs = (zp_ref, zl_ref, zg_ref, zt_ref)

        def project(lo):
            k = max(i for i in range(4) if IN_SPLITS[i] <= lo)
            part = _nn(u, win_ref[:, lo:lo + PROJ_CHUNK])
            z_new[:, lo:lo + PROJ_CHUNK] = part
            z_refs[k][:, lo - IN_SPLITS[k]:lo - IN_SPLITS[k] + PROJ_CHUNK] = part.astype(z_refs[k].dtype)

        before_scan = _Interleaved(functools.partial(project, lo) for lo in range(0, proj_mid, PROJ_CHUNK))
        after_scan = _Interleaved(functools.partial(project, lo) for lo in range(proj_mid, IN_SPLITS[4], PROJ_CHUNK))

        zp_cat = jnp.concatenate([jnp.where(first, 0.0, zp_halo[...]), z_old[:, IN_SPLITS[0]:IN_SPLITS[1]]], axis=0)
        _, mixed = _pool_forward(zp_cat, pw_ref, first_row, before_scan.tick)
        y_pool = _nn((mixed * ps_ref[...]).astype(BF16), pp_ref[...])

        sp, _ = _softplus_neg(lam_ref[...])
        zl_cat = jnp.concatenate([jnp.where(first, 0.0, zl_halo[...]), z_old[:, IN_SPLITS[1]:IN_SPLITS[2]]], axis=0)
        xc, r, ig, a, mult = _lru_gates(zl_cat, cw_ref[...], cb_ref[...], wrg_ref, brg_ref[...], wig_ref,
                                        big_ref[...], sp, first_row, before_scan.tick)
        a_s[...] = a
        b_s[...] = mult * ig * xc
        xc_ref[...] = xc.astype(BF16)
        r_ref[...] = r.astype(BF16)
        ig_ref[...] = ig.astype(BF16)
        before_scan.flush()

        rows8 = _row_ids((SUBLANES, D_MODEL))

        def group(g, carry):
            at = pl.ds(pl.multiple_of(g * SUBLANES, SUBLANES), SUBLANES)
            A, B = a_s[at, :], b_s[at, :]
            for s in (1, 2, 4):
                m = rows8 >= s
                B = jnp.where(m, A * pltpu.roll(B, s, 0) + B, B)
                A = jnp.where(m, A * pltpu.roll(A, s, 0), A)
            h = A * carry + B
            hs_ref[at, :] = h
            return jnp.broadcast_to(h[SUBLANES - 1:SUBLANES, :], (SUBLANES, D_MODEL))

        carry_s[...] = lax.fori_loop(0, n_groups, group, jnp.where(first, 0.0, carry_s[...]))
        gelu = _gelu(z_old[:, IN_SPLITS[2]:IN_SPLITS[3]])
        after_scan.tick(2)
        y_lru = _nn((hs_ref[...] * gelu).astype(BF16), lp_ref[...])

        gates = _sigmoid(z_old[:, IN_SPLITS[3]:IN_SPLITS[4]] + bg_ref[...])
        after_scan.tick(2)
        merged = gates[:, :D_MODEL] * y_pool + gates[:, D_MODEL:] * y_lru
        after_scan.flush()
        h1_ref[...] = x_ref[...] + _nn(merged.astype(BF16), wo_ref[...])
        yp_ref[...] = y_pool.astype(BF16)
        yl_ref[...] = y_lru.astype(BF16)
        zp_halo[...] = z_old[tt - POOL_HALO:, IN_SPLITS[0]:IN_SPLITS[1]]
        zl_halo[...] = z_old[tt - SUBLANES:, IN_SPLITS[1]:IN_SPLITS[2]]

    def ahead(cols):
        return pl.BlockSpec((tt, cols), lambda s: (jnp.minimum(s, nt - 1), 0))

    def behind(cols):
        return pl.BlockSpec((tt, cols), lambda s: (jnp.maximum(s - 1, 0), 0))

    res = [norm1_g, w_in, b_gate, pool_w, pool_scale, pool_proj, conv_w, conv_b, w_rg, b_rg, w_ig, b_ig, lru_lambda,
           lru_proj, w_out]
    return pl.pallas_call(
        body, name="f12_mixer", grid=(nt + 1,),
        in_specs=[ahead(D_MODEL), behind(D_MODEL)] + [_resident(w.shape) for w in res],
        out_specs=[ahead(w) for w in IN_WIDTHS] + [ahead(D_MODEL)] + [behind(D_MODEL)] * 7,
        out_shape=[jax.ShapeDtypeStruct((T, w), dt) for w, dt in zip(IN_WIDTHS, (F32, F32, F32, BF16))]
        + [jax.ShapeDtypeStruct((T, D_MODEL), BF16), jax.ShapeDtypeStruct((T, D_MODEL), F32),
           jax.ShapeDtypeStruct((T, D_MODEL), F32)] + [jax.ShapeDtypeStruct((T, D_MODEL), BF16)] * 5,
        scratch_shapes=[pltpu.VMEM((2, tt, IN_SPLITS[4]), F32), pltpu.VMEM((POOL_HALO, POOL_WIDTH), F32),
                        pltpu.VMEM((SUBLANES, D_MODEL), F32), pltpu.VMEM((tt, D_MODEL), F32),
                        pltpu.VMEM((tt, D_MODEL), F32), pltpu.VMEM((SUBLANES, D_MODEL), F32)],
        compiler_params=_params(("arbitrary",)),
    )(x, x, *res)


def _f3_ffn(h1, norm2_g, w_ffn_in, w_ffn_out):
    T = h1.shape[0]
    tm = ROW_TILE

    def body(h_ref, g_ref, wi_ref, wo_ref, h2_ref, v_ref, ff_ref, act_ref):
        h = h_ref[...]
        n, _ = _rms(h)
        v = (n * g_ref[...]).astype(BF16)
        v_ref[...] = v
        g_ff = _nn(v, wi_ref[:, :D_FF])
        u_ff = _nn(v, wi_ref[:, D_FF:])
        ff_ref[:, :D_FF] = g_ff.astype(BF16)
        ff_ref[:, D_FF:] = u_ff.astype(BF16)
        act = (g_ff * _sigmoid(g_ff) * u_ff).astype(BF16)
        act_ref[...] = act
        h2_ref[...] = h + _nn(act, wo_ref[...])

    return pl.pallas_call(
        body, name="f3_ffn", grid=(T // tm,),
        in_specs=[_rows(tm, D_MODEL), _resident((1, D_MODEL)), _resident(w_ffn_in.shape), _resident(w_ffn_out.shape)],
        out_specs=[_rows(tm, D_MODEL), _rows(tm, D_MODEL), _rows(tm, 2 * D_FF), _rows(tm, D_FF)],
        out_shape=[jax.ShapeDtypeStruct((T, D_MODEL), F32), jax.ShapeDtypeStruct((T, D_MODEL), BF16),
                   jax.ShapeDtypeStruct((T, 2 * D_FF), BF16), jax.ShapeDtypeStruct((T, D_FF), BF16)],
        compiler_params=_params(("arbitrary",)),
    )(h1, norm2_g, w_ffn_in, w_ffn_out)


def _b4_ple_loss(h2, p, target, ple_norm_g, w_ple_gate, w_ple_proj, final_g):
    T = h2.shape[0]
    tm = WIDE_TILE

    def body(h_ref, p_ref, t_ref, gp_ref, wg_ref, wp_ref, gf_ref, loss_ref, dh2_ref, dwg_ref, dwp_ref, vec_ref):
        @pl.when(pl.program_id(0) == 0)
        def _():
            loss_ref[...] = jnp.zeros_like(loss_ref)
            dwg_ref[...] = jnp.zeros_like(dwg_ref)
            dwp_ref[...] = jnp.zeros_like(dwp_ref)
            vec_ref[...] = jnp.zeros_like(vec_ref)

        h2v = h_ref[...]
        n3, r3 = _rms(h2v)
        n3g = (n3 * gp_ref[...]).astype(BF16)
        pg = _sigmoid(_nn(n3g, wg_ref[...]))
        pb = p_ref[...].astype(BF16)
        e = _nn(pb, wp_ref[...])
        h3 = h2v + pg * e
        n4, r4 = _rms(h3)
        diff = n4 * gf_ref[...] - t_ref[...]
        loss_ref[...] += jnp.sum(diff * diff).reshape(1, 1)
        dy = diff * (1.0 / D_MODEL)
        vec_ref[0:1, :] += _colsum(dy * n4)
        dh3 = _rms_bwd(dy * gf_ref[...], n4, r4)
        dwp_ref[...] += _tn(pb, (dh3 * pg).astype(BF16))
        dq = (dh3 * e * pg * (1.0 - pg)).astype(BF16)
        dwg_ref[...] += _tn(n3g, dq)
        dn3g = _nt(dq, wg_ref[...])
        vec_ref[1:2, :] += _colsum(dn3g * n3)
        dh2_ref[...] = dh3 + _rms_bwd(dn3g * gp_ref[...], n3, r3)

    return pl.pallas_call(
        body, name="b4_ple_loss", grid=(T // tm,),
        in_specs=[_rows(tm, D_MODEL), _rows(tm, PLE_DIM), _rows(tm, D_MODEL), _resident((1, D_MODEL)),
                  _resident(w_ple_gate.shape), _resident(w_ple_proj.shape), _resident((1, D_MODEL))],
        out_specs=[_acc((1, 1)), _rows(tm, D_MODEL), _acc(w_ple_gate.shape), _acc(w_ple_proj.shape),
                   _acc((SUBLANES, D_MODEL))],
        out_shape=[jax.ShapeDtypeStruct((1, 1), F32), jax.ShapeDtypeStruct((T, D_MODEL), F32),
                   jax.ShapeDtypeStruct(w_ple_gate.shape, F32), jax.ShapeDtypeStruct(w_ple_proj.shape, F32),
                   jax.ShapeDtypeStruct((SUBLANES, D_MODEL), F32)],
        compiler_params=_params(("arbitrary",)),
    )(h2, p, target, ple_norm_g, w_ple_gate, w_ple_proj, final_g)


def _b3_ffn(dh2, h1, ff, norm2_g, w_ffn_in, w_ffn_out):
    T = h1.shape[0]
    tm = ROW_TILE

    def body(d_ref, h_ref, ff_ref, g_ref, wi_ref, wo_ref, dff_ref, dh1_ref, vec_ref):
        @pl.when(pl.program_id(0) == 0)
        def _():
            vec_ref[...] = jnp.zeros_like(vec_ref)

        dh2v = d_ref[...]
        dact = _nt(dh2v.astype(BF16), wo_ref[...])
        g_ff = ff_ref[:, :D_FF].astype(F32)
        u_ff = ff_ref[:, D_FF:].astype(F32)
        s = _sigmoid(g_ff)
        dg = (dact * u_ff * (s * (1.0 + g_ff * (1.0 - s)))).astype(BF16)
        du = (dact * (g_ff * s)).astype(BF16)
        dff_ref[:, :D_FF] = dg
        dff_ref[:, D_FF:] = du
        dv = _nt(dg, wi_ref[:, :D_FF]) + _nt(du, wi_ref[:, D_FF:])
        n2, r2 = _rms(h_ref[...])
        vec_ref[0:1, :] += _colsum(dv * n2)
        dh1_ref[...] = dh2v + _rms_bwd(dv * g_ref[...], n2, r2)

    return pl.pallas_call(
        body, name="b3_ffn", grid=(T // tm,),
        in_specs=[_rows(tm, D_MODEL), _rows(tm, D_MODEL), _rows(tm, 2 * D_FF), _resident((1, D_MODEL)),
                  _resident(w_ffn_in.shape), _resident(w_ffn_out.shape)],
        out_specs=[_rows(tm, 2 * D_FF), _rows(tm, D_MODEL), _acc((SUBLANES, D_MODEL))],
        out_shape=[jax.ShapeDtypeStruct((T, 2 * D_FF), BF16), jax.ShapeDtypeStruct((T, D_MODEL), F32),
                   jax.ShapeDtypeStruct((SUBLANES, D_MODEL), F32)],
        compiler_params=_params(("arbitrary",)),
    )(dh2, h1, ff, norm2_g, w_ffn_in, w_ffn_out)


def _wgrad(a, b, col_tile, name, tokens=WGRAD_TOKENS, after=None):
    T, K = a.shape
    N = b.shape[1]
    tk = min(T, tokens)

    def body(a_ref, b_ref, *rest):
        o_ref = rest[-1]

        @pl.when(pl.program_id(1) == 0)
        def _():
            o_ref[...] = jnp.zeros_like(o_ref)

        o_ref[...] += _tn(a_ref[...].astype(BF16), b_ref[...].astype(BF16))

    return pl.pallas_call(
        body, name=name, grid=(N // col_tile, T // tk),
        in_specs=[pl.BlockSpec((tk, K), lambda j, k: (k, 0)), pl.BlockSpec((tk, col_tile), lambda j, k: (k, j))]
        + ([] if after is None else [ANY]),
        out_specs=pl.BlockSpec((K, col_tile), lambda j, k: (0, j)),
        out_shape=jax.ShapeDtypeStruct((K, N), F32),
        compiler_params=_params(("arbitrary", "arbitrary")),
    )(a, b, *([] if after is None else [after]))


def _b2_gates(dh1, zt, yp, yl, b_gate, w_out):
    T = dh1.shape[0]
    tm = WIDE_TILE

    def body(d_ref, zt_ref, yp_ref, yl_ref, bg_ref, wo_ref, dzt_ref, dyp_ref, dyl_ref, dwo_ref, vec_ref):
        @pl.when(pl.program_id(0) == 0)
        def _():
            dwo_ref[...] = jnp.zeros_like(dwo_ref)
            vec_ref[...] = jnp.zeros_like(vec_ref)

        db = d_ref[...].astype(BF16)
        dm = _nt(db, wo_ref[...])
        gates = _sigmoid(zt_ref[...].astype(F32) + bg_ref[...])
        g0, g1 = gates[:, :D_MODEL], gates[:, D_MODEL:]
        y_pool, y_lru = yp_ref[...].astype(F32), yl_ref[...].astype(F32)
        dwo_ref[...] += _tn((g0 * y_pool + g1 * y_lru).astype(BF16), db)
        dz0 = dm * y_pool * g0 * (1.0 - g0)
        dz1 = dm * y_lru * g1 * (1.0 - g1)
        vec_ref[0:1, :] += _colsum(dz0)
        vec_ref[1:2, :] += _colsum(dz1)
        dzt_ref[:, :D_MODEL] = dz0.astype(BF16)
        dzt_ref[:, D_MODEL:] = dz1.astype(BF16)
        dyp_ref[...] = (dm * g0).astype(BF16)
        dyl_ref[...] = (dm * g1).astype(BF16)

    return pl.pallas_call(
        body, name="b2_gates", grid=(T // tm,),
        in_specs=[_rows(tm, D_MODEL), _rows(tm, 2 * D_MODEL), _rows(tm, D_MODEL), _rows(tm, D_MODEL),
                  _resident(b_gate.shape), _resident(w_out.shape)],
        out_specs=[_rows(tm, 2 * D_MODEL), _rows(tm, D_MODEL), _rows(tm, D_MODEL), _acc(w_out.shape),
                   _acc((SUBLANES, D_MODEL))],
        out_shape=[jax.ShapeDtypeStruct((T, 2 * D_MODEL), BF16), jax.ShapeDtypeStruct((T, D_MODEL), BF16),
                   jax.ShapeDtypeStruct((T, D_MODEL), BF16), jax.ShapeDtypeStruct(w_out.shape, F32),
                   jax.ShapeDtypeStruct((SUBLANES, D_MODEL), F32)],
        compiler_params=_params(("arbitrary",)),
    )(dh1, zt, yp, yl, b_gate, w_out)


def _b12_pool_in_proj(dyp, zp, dzl, dzg, dzt, x, dh1, norm1_g, w_in, pool_w, pool_scale, pool_proj):
    T = zp.shape[0]
    tt = ROW_TILE
    nt = T // tt

    def body(dy_ref, zp_ref, zph_ref, dzl_ref, dzg_ref, dzt_ref, x_ref, dh_ref, g1_ref, win_ref, pw_ref, ps_ref, pp_ref,
             dzp_ref, dx_ref, dpp_ref, dpw_ref, vec_ref, q_next):
        i = pl.program_id(0)
        ti = nt - 1 - i
        first_row = ti * tt

        @pl.when(i == 0)
        def _():
            dpp_ref[...] = jnp.zeros_like(dpp_ref)
            dpw_ref[...] = jnp.zeros_like(dpw_ref)
            vec_ref[...] = jnp.zeros_like(vec_ref)
            q_next[...] = jnp.zeros_like(q_next)

        du_parts = []

        def project(lo):
            k = max(i for i in range(4) if IN_SPLITS[i] <= lo)
            dz_ref = (None, dzl_ref, dzg_ref, dzt_ref)[k]
            at = lo - IN_SPLITS[k]
            part = _nt(dz_ref[:, at:at + PROJ_CHUNK], win_ref[:, lo:lo + PROJ_CHUNK])
            du_parts[:] = [part if not du_parts else du_parts[0] + part]

        mxu = _Interleaved(functools.partial(project, lo) for lo in range(IN_SPLITS[1], IN_SPLITS[4], PROJ_CHUNK))

        keep = (ti > 0).astype(F32)
        zp_cat = jnp.concatenate([zph_ref[...] * keep, zp_ref[...]], axis=0)
        pooled, mixed = _pool_forward(zp_cat, pw_ref, first_row, mxu.tick)
        dy = dy_ref[...]
        dpp_ref[...] += _tn((mixed * ps_ref[...]).astype(BF16), dy)
        mxu.tick(2)
        dms = _nt(dy, pp_ref[...])
        mxu.tick(2)
        vec_ref[0:1, :POOL_WIDTH] += _colsum(dms * mixed)
        dmixed = (dms * ps_ref[...]).astype(BF16)
        t_glob = first_row + _row_ids((tt, POOL_GROUP_DIM))
        dz, q_all = [], []
        for g, w in enumerate(POOL_WINDOWS):
            cols = slice(g * POOL_GROUP_DIM, (g + 1) * POOL_GROUP_DIM)
            dpw_ref[g] += _tn(pooled[:, cols].astype(BF16), dmixed[:, cols])
            dpooled = _nt(dmixed[:, cols], pw_ref[g])
            q = dpooled / jnp.minimum(t_glob + 1, w).astype(F32)
            q_all.append(q)
            s, k = jnp.concatenate([q, q_next[:, cols]], axis=0), 1
            while k < w:
                s = s + _shift_up(s, k)
                k *= 2
            dz.append(s[:tt] - dpooled)
            mxu.tick(2)
        dzp = jnp.concatenate(dz, axis=1).astype(BF16)
        dzp_ref[...] = dzp
        q_next[...] = jnp.concatenate([q[:POOL_HALO] for q in q_all], axis=1)
        mxu.flush()

        du = du_parts[0] + _nt(dzp, win_ref[:, IN_SPLITS[0]:IN_SPLITS[1]])
        n1, r1 = _rms(x_ref[...])
        vec_ref[1:2, :] += _colsum(du * n1)
        dx_ref[...] = dh_ref[...] + _rms_bwd(du * g1_ref[...], n1, r1)

    rev = functools.partial(_rows_rev, n_tiles=nt)
    res = [norm1_g, w_in, pool_w, pool_scale, pool_proj]
    return pl.pallas_call(
        body, name="b12_pool_in_proj", grid=(nt,),
        in_specs=[rev(tt, D_MODEL), rev(tt, POOL_WIDTH), _halo_before_rev(POOL_HALO, POOL_WIDTH, tt, nt),
                  rev(tt, D_MODEL), rev(tt, D_MODEL), rev(tt, 2 * D_MODEL), rev(tt, D_MODEL), rev(tt, D_MODEL)]
        + [_resident(w.shape) for w in res],
        out_specs=[rev(tt, POOL_WIDTH), rev(tt, D_MODEL), _acc(pool_proj.shape), _acc(pool_w.shape),
                   _acc((SUBLANES, D_MODEL))],
        out_shape=[jax.ShapeDtypeStruct((T, POOL_WIDTH), BF16), jax.ShapeDtypeStruct((T, D_MODEL), F32),
                   jax.ShapeDtypeStruct(pool_proj.shape, F32), jax.ShapeDtypeStruct(pool_w.shape, F32),
                   jax.ShapeDtypeStruct((SUBLANES, D_MODEL), F32)],
        scratch_shapes=[pltpu.VMEM((POOL_HALO, POOL_WIDTH), F32)],
        compiler_params=_params(("arbitrary",)),
    )(dyp, zp, zp, dzl, dzg, dzt, x, dh1, *res)


_V_CONVW, _V_CONVB, _V_BRG, _V_BIG, _V_LAM = 0, 4, 5, 6, 7


def _b2_lru(dyl, zl, zg, hs, xc_saved, r_saved, ig_saved, conv_w, w_rg, w_ig, lru_lambda, lru_proj):
    T = zl.shape[0]
    tt = ROW_TILE
    nt = T // tt
    n_groups = tt // SUBLANES

    def body(dy_ref, zl_ref, zlh_ref, zg_ref, hs_ref, hsh_ref, xc_ref, r_ref, ig_ref, cw_ref, wrg_ref, wig_ref,
             lam_ref, lp_ref, dzl_ref, dzg_ref, dlp_ref, dwrg_ref, dwig_ref, vec_ref,
             c_s, d_s, g_s, g_next, a_next, dxc_next):
        i = pl.program_id(0)
        ti = nt - 1 - i
        first_row = ti * tt

        @pl.when(i == 0)
        def _():
            dlp_ref[...] = jnp.zeros_like(dlp_ref)
            dwrg_ref[...] = jnp.zeros_like(dwrg_ref)
            dwig_ref[...] = jnp.zeros_like(dwig_ref)
            vec_ref[...] = jnp.zeros_like(vec_ref)
            g_next[...] = jnp.zeros_like(g_next)
            a_next[...] = jnp.zeros_like(a_next)
            dxc_next[...] = jnp.zeros_like(dxc_next)

        keep = (ti > 0).astype(F32)
        sp, dsp_dlam = _softplus_neg(lam_ref[...])
        cw = cw_ref[...]
        zl_cat = jnp.concatenate([zlh_ref[...] * keep, zl_ref[...]], axis=0)
        xc, r, ig = xc_ref[...].astype(F32), r_ref[...].astype(F32), ig_ref[...].astype(F32)
        a, mult = _decay(r, sp, first_row)
        hs = hs_ref[...]
        gelu, dgelu = _gelu_and_grad(zg_ref[...])
        dy = dy_ref[...]
        dlp_ref[...] += _tn((hs * gelu).astype(BF16), dy)
        dyl = _nt(dy, lp_ref[...])
        dzg_ref[...] = (dyl * hs * dgelu).astype(BF16)

        d_s[...] = dyl * gelu
        c_s[...] = _shift_up(jnp.concatenate([a, a_next[...]], axis=0), 1)[:tt]
        rows8 = _row_ids((SUBLANES, D_MODEL))

        def group(k, carry):
            at = pl.ds(pl.multiple_of((n_groups - 1 - k) * SUBLANES, SUBLANES), SUBLANES)
            C, Dv = c_s[at, :], d_s[at, :]
            for s in (1, 2, 4):
                m = rows8 < SUBLANES - s
                Dv = jnp.where(m, C * pltpu.roll(Dv, SUBLANES - s, 0) + Dv, Dv)
                C = jnp.where(m, C * pltpu.roll(C, SUBLANES - s, 0), C)
            G = C * carry + Dv
            g_s[at, :] = G
            return jnp.broadcast_to(G[0:1, :], (SUBLANES, D_MODEL))

        g_next[...] = lax.fori_loop(0, n_groups, group, g_next[...])
        a_next[...] = jnp.broadcast_to(a[0:1, :], (SUBLANES, D_MODEL))
        G = g_s[...]

        h_prev = _shift_down(jnp.concatenate([hsh_ref[...] * keep, hs], axis=0), 1)[SUBLANES:]
        t_glob = first_row + _row_ids((tt, D_MODEL))
        dmult = jnp.where(t_glob == 0, 0.0, G * ig * xc)
        dla = G * h_prev * a - dmult * (a * a) / mult
        vec_ref[_V_LAM:_V_LAM + 1, :] += _colsum(dla * r) * (-LRU_C) * dsp_dlam
        dpr = dla * (-LRU_C) * sp * r * (1.0 - r)
        dpi = G * mult * xc * ig * (1.0 - ig)
        vec_ref[_V_BRG:_V_BRG + 1, :] += _colsum(dpr)
        vec_ref[_V_BIG:_V_BIG + 1, :] += _colsum(dpi)
        dprb, dpib, xh = dpr.astype(BF16), dpi.astype(BF16), xc_ref[...]
        dxc_h = []
        for h in range(LRU_HEADS):
            cols = slice(h * LRU_HEAD_DIM, (h + 1) * LRU_HEAD_DIM)
            dwrg_ref[h] += _tn(xh[:, cols], dprb[:, cols])
            dwig_ref[h] += _tn(xh[:, cols], dpib[:, cols])
            dxc_h.append(_nt(dprb[:, cols], wrg_ref[h]) + _nt(dpib[:, cols], wig_ref[h]))
        dxc = G * mult * ig + jnp.concatenate(dxc_h, axis=1)

        vec_ref[_V_CONVB:_V_CONVB + 1, :] += _colsum(dxc)
        dxc_cat = jnp.concatenate([dxc, dxc_next[...]], axis=0)
        dzl = cw[CONV_WIDTH - 1:CONV_WIDTH] * dxc
        for k in range(CONV_WIDTH):
            lag = CONV_WIDTH - 1 - k
            vec_ref[_V_CONVW + k:_V_CONVW + k + 1, :] += _colsum(dxc * _shift_down(zl_cat, lag)[SUBLANES:])
            if lag:
                dzl = dzl + cw[k:k + 1] * _shift_up(dxc_cat, lag)[:tt]
        dzl_ref[...] = dzl.astype(BF16)
        dxc_next[...] = dxc[:SUBLANES]

    res = [conv_w, w_rg, w_ig, lru_lambda, lru_proj]
    return pl.pallas_call(
        body, name="b2_lru", grid=(nt,),
        in_specs=[_rows_rev(tt, D_MODEL, nt), _rows_rev(tt, D_MODEL, nt), _halo_before_rev(SUBLANES, D_MODEL, tt, nt),
                  _rows_rev(tt, D_MODEL, nt), _rows_rev(tt, D_MODEL, nt), _halo_before_rev(SUBLANES, D_MODEL, tt, nt)]
        + [_rows_rev(tt, D_MODEL, nt)] * 3 + [_resident(w.shape) for w in res],
        out_specs=[_rows_rev(tt, D_MODEL, nt), _rows_rev(tt, D_MODEL, nt), _acc(lru_proj.shape), _acc(w_rg.shape),
                   _acc(w_ig.shape), _acc((SUBLANES, D_MODEL))],
        out_shape=[jax.ShapeDtypeStruct((T, D_MODEL), BF16), jax.ShapeDtypeStruct((T, D_MODEL), BF16),
                   jax.ShapeDtypeStruct(lru_proj.shape, F32), jax.ShapeDtypeStruct(w_rg.shape, F32),
                   jax.ShapeDtypeStruct(w_ig.shape, F32), jax.ShapeDtypeStruct((SUBLANES, D_MODEL), F32)],
        scratch_shapes=[pltpu.VMEM((tt, D_MODEL), F32)] * 3 + [pltpu.VMEM((SUBLANES, D_MODEL), F32)] * 3,
        compiler_params=_params(("arbitrary",)),
    )(dyl, zl, zl, zg, hs, hs, xc_saved, r_saved, ig_saved, *res)


def _row_tile(rows):
    for t in (512, 256, 128, 64, 32, 16, 8):
        if rows % t == 0:
            return t
    return rows


def _scalar_grid(grid, in_specs, out_specs):
    return pltpu.PrefetchScalarGridSpec(num_scalar_prefetch=1, grid=grid, in_specs=in_specs, out_specs=out_specs)


def _cast_into_block(w, by_rows, shard_j, name):
    R, C = w.shape
    tr = _row_tile(R)
    if by_rows:
        out_shape, out_map = (N_SHARDS * R, C), lambda i, j: (j[0] * (R // tr) + i, 0)
    else:
        out_shape, out_map = (R, N_SHARDS * C), lambda i, j: (i, j[0])

    def body(j_ref, w_ref, o_ref):
        o_ref[...] = w_ref[...].astype(BF16)

    return pl.pallas_call(
        body, name=name,
        grid_spec=_scalar_grid((R // tr,), [pl.BlockSpec((tr, C), lambda i, j: (i, 0))], pl.BlockSpec((tr, C), out_map)),
        out_shape=jax.ShapeDtypeStruct(out_shape, BF16),
        compiler_params=_params(("arbitrary",)),
    )(shard_j.reshape(1), w)


def _sum_cores(g, theirs, core, name):
    S, R, C = g.shape
    H = R // 2
    tr = _row_tile(H)
    nh = H // tr

    def body(c_ref, g_ref, t_ref, o_ref):
        o_ref[...] = (g_ref[...] + t_ref[...]).astype(BF16)

    half = pl.BlockSpec((None, tr, C), lambda s, i, c: (s, i, 0))
    return pl.pallas_call(
        body, name=name,
        grid_spec=_scalar_grid((S, nh), [pl.BlockSpec((None, tr, C), lambda s, i, c: (s, c[0] * nh + i, 0)), half], half),
        out_shape=jax.ShapeDtypeStruct((S, H, C), BF16),
        compiler_params=_params(("arbitrary", "arbitrary")),
    )(core.reshape(1), g, theirs)


def _sum_chips(sums, slots, by_rows, place, name):
    _, H, C = slots.shape
    tr = _row_tile(H)
    own_map = (lambda i, p: (p[0], i, 0)) if by_rows else (lambda i, p: (0, i, p[0]))

    def body(p_ref, s_ref, q_ref, o_ref):
        o_ref[...] = ((s_ref[...].astype(F32) + q_ref[0].astype(F32)) + q_ref[1].astype(F32)) + q_ref[2].astype(F32)

    return pl.pallas_call(
        body, name=name,
        grid_spec=_scalar_grid(
            (H // tr,),
            [pl.BlockSpec((None, tr, C), own_map), pl.BlockSpec((3, tr, C), lambda i, p: (0, i, 0))],
            pl.BlockSpec((None, tr, C), lambda i, p: (p[1], i, 0))),
        out_shape=jax.ShapeDtypeStruct((2, H, C), F32),
        compiler_params=_params(("arbitrary",)),
    )(place, sums, slots)


def _adamw(w, g, m, v, name):
    R, C = w.shape
    tr = _row_tile(R)
    c1 = 1.0 - ADAM_B1 ** ADAM_STEP
    c2 = 1.0 - ADAM_B2 ** ADAM_STEP

    def body(w_ref, g_ref, m_ref, v_ref, d_ref, nm_ref, nv_ref):
        gv = g_ref[...]
        nm = ADAM_B1 * m_ref[...] + (1.0 - ADAM_B1) * gv
        nv = ADAM_B2 * v_ref[...] + (1.0 - ADAM_B2) * (gv * gv)
        d_ref[...] = -ADAM_LR * ((nm / c1) / (jnp.sqrt(nv / c2) + ADAM_EPS) + ADAM_WD * w_ref[...])
        nm_ref[...] = nm
        nv_ref[...] = nv

    return pl.pallas_call(
        body, name=name, grid=(R // tr,),
        in_specs=[_rows(tr, C)] * 4, out_specs=[_rows(tr, C)] * 3,
        out_shape=[jax.ShapeDtypeStruct((R, C), F32)] * 3,
        compiler_params=_params(("arbitrary",)),
    )(w, g, m, v)


def _place():
    return lax.axis_index("x"), lax.axis_index("y"), lax.axis_index("c")


def _other_chips(x, y):
    return [(1 - x, y), (x, 1 - y), (1 - x, 1 - y)]


def _shard_block(ref, by_rows, R, C, j, half_rows=None):
    if half_rows is None:
        rows, r0 = R, 0
    else:
        rows = R // 2
        r0 = pl.multiple_of(half_rows * rows, 16)
    if by_rows:
        return ref.at[pl.ds(pl.multiple_of(j * R, 16) + r0, rows), :]
    return ref.at[pl.ds(r0, rows), pl.ds(pl.multiple_of(j * C, 128), C)]


def _all_gather_weights(gathered, shapes, by_rows, small):
    n = len(gathered)

    def body(*refs):
        small_in = refs[n]
        outs, small_out = refs[n + 1:2 * n + 1], refs[2 * n + 1]
        send_sems, recv_sems, local_sem = refs[2 * n + 2:]
        x, y, c = _place()
        me_j = 2 * x + y
        chips = _other_chips(x, y)
        sibling = (x, y, 1 - c)

        def block(i, j, half):
            R, C = shapes[i]
            return _shard_block(outs[i], by_rows[i], R, C, j, half)

        def ici(i, k, src_j):
            return pltpu.make_async_remote_copy(
                src_ref=block(i, src_j, c), dst_ref=block(i, src_j, c),
                send_sem=send_sems.at[6 * i + k], recv_sem=recv_sems.at[6 * i + k],
                device_id=(*chips[k], c), device_id_type=MESH)

        def relay(i, k, half):
            kj = 2 * chips[k][0] + chips[k][1]
            return pltpu.make_async_remote_copy(
                src_ref=block(i, kj, half), dst_ref=block(i, kj, half),
                send_sem=send_sems.at[6 * i + 3 + k], recv_sem=recv_sems.at[6 * i + 3 + k],
                device_id=sibling, device_id_type=MESH)

        def small_copy(k, src_j):
            cols = pl.ds(pl.multiple_of(src_j * 256, 128), 256)
            return pltpu.make_async_remote_copy(
                src_ref=small_in, dst_ref=small_out.at[:, cols],
                send_sem=send_sems.at[6 * n + k], recv_sem=recv_sems.at[6 * n + k],
                device_id=(*chips[k], c), device_id_type=MESH)

        sends = []
        for i in range(n):
            for k in range(3):
                cp = ici(i, k, me_j)
                cp.start()
                sends.append(cp)
        for k in range(3):
            cp = small_copy(k, me_j)
            cp.start()
            sends.append(cp)
        local = pltpu.make_async_copy(small_in, small_out.at[:, pl.ds(pl.multiple_of(me_j * 256, 128), 256)], local_sem)
        local.start()
        for i in range(n):
            for k in range(3):
                kj = 2 * chips[k][0] + chips[k][1]
                ici(i, k, kj).wait_recv()
                cp = relay(i, k, c)
                cp.start()
                sends.append(cp)
        for k in range(3):
            small_copy(k, 2 * chips[k][0] + chips[k][1]).wait_recv()
        for i in range(n):
            for k in range(3):
                relay(i, k, 1 - c).wait_recv()
        for cp in sends:
            cp.wait_send()
        local.wait()

    out_shape = [jax.ShapeDtypeStruct(g.shape, BF16) for g in gathered]
    out_shape.append(jax.ShapeDtypeStruct((8, N_SHARDS * 256), F32))
    n_sems = 6 * n + 3
    return pl.pallas_call(
        body, name="all_gather_weights",
        in_specs=[ANY] * (n + 1), out_specs=[ANY] * (n + 1), out_shape=out_shape,
        input_output_aliases={i: i for i in range(n)},
        scratch_shapes=[pltpu.SemaphoreType.DMA((n_sems,)), pltpu.SemaphoreType.DMA((n_sems,)),
                        pltpu.SemaphoreType.DMA],
    )(*gathered, small)


def _core_exchange(grads, name):
    n = len(grads)

    def body(*refs):
        ins, theirs = refs[:n], refs[n:2 * n]
        send_sems, recv_sems = refs[2 * n:]
        x, y, c = _place()
        copies = []
        for i in range(n):
            H = grads[i].shape[1] // 2
            cp = pltpu.make_async_remote_copy(
                src_ref=ins[i].at[:, pl.ds(pl.multiple_of((1 - c) * H, 8), H), :], dst_ref=theirs[i],
                send_sem=send_sems.at[i], recv_sem=recv_sems.at[i],
                device_id=(x, y, 1 - c), device_id_type=MESH)
            cp.start()
            copies.append(cp)
        for cp in copies:
            cp.wait()

    return pl.pallas_call(
        body, name=name,
        in_specs=[ANY] * n, out_specs=[ANY] * n,
        out_shape=[jax.ShapeDtypeStruct((g.shape[0], g.shape[1] // 2, g.shape[2]), F32) for g in grads],
        scratch_shapes=[pltpu.SemaphoreType.DMA((n,))] * 2,
    )(*grads)


def _chip_exchange(sums, by_rows):
    n = len(sums)
    dims = [(s.shape[1], s.shape[2]) if by_rows[i] else (s.shape[1], s.shape[2] // N_SHARDS)
            for i, s in enumerate(sums)]

    def body(*refs):
        ins, outs = refs[:n], refs[n:2 * n]
        send_sems, recv_sems = refs[2 * n:]
        x, y, c = _place()
        chips = _other_chips(x, y)

        def shard(i, j):
            if by_rows[i]:
                return ins[i].at[j]
            return ins[i].at[0, :, pl.ds(pl.multiple_of(j * dims[i][1], 128), dims[i][1])]

        copies = []
        for i in range(n):
            for k in range(3):
                kj = 2 * chips[k][0] + chips[k][1]
                cp = pltpu.make_async_remote_copy(
                    src_ref=shard(i, kj), dst_ref=outs[i].at[k],
                    send_sem=send_sems.at[3 * i + k], recv_sem=recv_sems.at[3 * i + k],
                    device_id=(*chips[k], c), device_id_type=MESH)
                cp.start()
                copies.append(cp)
        for cp in copies:
            cp.wait()

    return pl.pallas_call(
        body, name="grad_chip_exchange",
        in_specs=[ANY] * n, out_specs=[ANY] * n,
        out_shape=[jax.ShapeDtypeStruct((3, h, cc), BF16) for h, cc in dims],
        scratch_shapes=[pltpu.SemaphoreType.DMA((3 * n,))] * 2,
    )(*sums)


HBM = pl.BlockSpec(memory_space=pltpu.HBM)
SEM = pl.BlockSpec(memory_space=pltpu.SEMAPHORE)
TOKEN = jax.ShapeDtypeStruct((SUBLANES, 128), F32)


def _in_hbm(a):
    return pltpu.with_memory_space_constraint(a, pltpu.HBM)


def _split_params():
    return pltpu.CompilerParams(has_side_effects=pltpu.SideEffectType.DATAFLOW_SIDE_EFFECTING)


def _gather_rest_copies(refs, shapes, by_rows, send_sems, recv_sems):
    x, y, c = _place()
    me_j = 2 * x + y
    chips = _other_chips(x, y)
    pairs = []
    for i, ref in enumerate(refs):
        R, C = shapes[i]
        for k in range(3):
            kj = 2 * chips[k][0] + chips[k][1]

            def copy(j, ref=ref, i=i, k=k, R=R, C=C):
                blk = _shard_block(ref, by_rows[i], R, C, j)
                return pltpu.make_async_remote_copy(
                    src_ref=blk, dst_ref=blk, send_sem=send_sems.at[3 * i + k], recv_sem=recv_sems.at[3 * i + k],
                    device_id=(*chips[k], c), device_id_type=MESH)

            pairs.append((copy(me_j), copy(kj)))
    return pairs


def _gather_rest_start(gathered, shapes, by_rows, after):
    n = len(gathered)

    def body(*refs):
        ins = refs[:n]
        send_sems, recv_sems = refs[n + 1], refs[n + 2]
        token = refs[-1]
        for mine, _ in _gather_rest_copies(ins, shapes, by_rows, send_sems, recv_sems):
            mine.start()
        token[...] = jnp.zeros_like(token)

    out = pl.pallas_call(
        body, name="gather_rest_start",
        out_shape=(pltpu.SemaphoreType.DMA((3 * n,)), pltpu.SemaphoreType.DMA((3 * n,)),
                   *[pltpu.HBM(g.shape, g.dtype) for g in gathered], TOKEN),
        in_specs=[HBM] * n + [ANY], out_specs=(SEM, SEM, *[HBM] * n, pl.BlockSpec(memory_space=pltpu.VMEM)),
        input_output_aliases={i: 2 + i for i in range(n)},
        compiler_params=_split_params(),
    )(*[_in_hbm(g) for g in gathered], after)
    return out[0], out[1], out[2:2 + n], out[-1]


def _gather_rest_wait(send_sems, recv_sems, gathered, shapes, by_rows, after):
    n = len(gathered)

    def body(*refs):
        ins = refs[:n]
        send, recv = refs[n], refs[n + 1]
        for mine, theirs in _gather_rest_copies(ins, shapes, by_rows, send, recv):
            mine.wait_send()
            theirs.wait_recv()

    return pl.pallas_call(
        body, name="gather_rest_wait",
        out_shape=tuple(pltpu.HBM(g.shape, g.dtype) for g in gathered),
        in_specs=[HBM] * n + [SEM, SEM, ANY], out_specs=tuple([HBM] * n),
        input_output_aliases={i: i for i in range(n)},
        compiler_params=_split_params(),
    )(*gathered, send_sems, recv_sems, after)


def _chip_exchange_copies(ins, slots, dims, by_rows, send_sems, recv_sems):
    x, y, c = _place()
    chips = _other_chips(x, y)
    pairs = []
    for i in range(len(ins)):
        for k in range(3):
            kj = 2 * chips[k][0] + chips[k][1]
            if by_rows[i]:
                src = ins[i].at[kj]
            else:
                src = ins[i].at[0, :, pl.ds(pl.multiple_of(kj * dims[i][1], 128), dims[i][1])]
            cp = pltpu.make_async_remote_copy(
                src_ref=src, dst_ref=slots[i].at[k], send_sem=send_sems.at[3 * i + k], recv_sem=recv_sems.at[3 * i + k],
                device_id=(*chips[k], c), device_id_type=MESH)
            pairs.append((cp, cp))
    return pairs


def _exchange_dims(sums, by_rows):
    return [(s.shape[1], s.shape[2]) if by_rows[i] else (s.shape[1], s.shape[2] // N_SHARDS) for i, s in enumerate(sums)]


def _chip_exchange_start(sums, by_rows):
    n = len(sums)
    sums = list(sums)
    dims = _exchange_dims(sums, by_rows)
    slots = [lax.empty((3, h, cc), BF16) for h, cc in dims]

    def body(*refs):
        ins, land = refs[:n], refs[n:2 * n]
        send_sems, recv_sems = refs[2 * n], refs[2 * n + 1]
        token = refs[-1]
        for cp, _ in _chip_exchange_copies(ins, land, dims, by_rows, send_sems, recv_sems):
            cp.start()
        token[...] = jnp.zeros_like(token)

    out = pl.pallas_call(
        body, name="grad_chip_exchange_start",
        out_shape=(pltpu.SemaphoreType.DMA((3 * n,)), pltpu.SemaphoreType.DMA((3 * n,)),
                   *[pltpu.HBM(a.shape, a.dtype) for a in sums + slots], TOKEN),
        in_specs=[HBM] * (2 * n), out_specs=(SEM, SEM, *[HBM] * (2 * n), pl.BlockSpec(memory_space=pltpu.VMEM)),
        input_output_aliases={i: 2 + i for i in range(2 * n)},
        compiler_params=_split_params(),
    )(*[_in_hbm(a) for a in sums + slots])
    return out[0], out[1], out[2:2 + n], out[2 + n:2 + 2 * n], out[-1]


def _chip_exchange_wait(send_sems, recv_sems, sums, slots, by_rows, after):
    n = len(sums)
    sums, slots = list(sums), list(slots)
    dims = _exchange_dims(sums, by_rows)

    def body(*refs):
        ins, land = refs[:n], refs[n:2 * n]
        send, recv = refs[2 * n], refs[2 * n + 1]
        for cp, _ in _chip_exchange_copies(ins, land, dims, by_rows, send, recv):
            cp.wait_send()
            cp.wait_recv()

    out = pl.pallas_call(
        body, name="grad_chip_exchange_wait",
        out_shape=tuple(pltpu.HBM(a.shape, a.dtype) for a in sums + slots),
        in_specs=[HBM] * (2 * n) + [SEM, SEM, ANY], out_specs=tuple([HBM] * (2 * n)),
        input_output_aliases={i: i for i in range(2 * n)},
        compiler_params=_split_params(),
    )(*sums, *slots, send_sems, recv_sems, after)
    return out[:n], out[n:]


def _core_share(reduced):
    n = len(reduced)

    def body(*refs):
        outs = refs[n:2 * n]
        send_sems, recv_sems = refs[2 * n:]
        x, y, c = _place()
        copies = []
        for i in range(n):
            cp = pltpu.make_async_remote_copy(
                src_ref=outs[i].at[c], dst_ref=outs[i].at[c], send_sem=send_sems.at[i], recv_sem=recv_sems.at[i],
                device_id=(x, y, 1 - c), device_id_type=MESH)
            cp.start()
            copies.append(cp)
        for cp in copies:
            cp.wait()

    return pl.pallas_call(
        body, name="grad_core_share",
        in_specs=[ANY] * n, out_specs=[ANY] * n,
        out_shape=[jax.ShapeDtypeStruct(r.shape, F32) for r in reduced],
        input_output_aliases={i: i for i in range(n)},
        scratch_shapes=[pltpu.SemaphoreType.DMA((n,))] * 2,
    )(*reduced)


def _small_exchange_copies(pack_ref, slots_ref, send_sems, recv_sems):
    x, y, c = _place()
    peers = [(px, py, pc) for px in (x, 1 - x) for py in (y, 1 - y) for pc in (c, 1 - c)][1:]
    pairs = []
    for k, peer in enumerate(peers):
        def copy(sender, k=k, peer=peer):
            return pltpu.make_async_remote_copy(
                src_ref=pack_ref, dst_ref=slots_ref.at[4 * sender[0] + 2 * sender[1] + sender[2]],
                send_sem=send_sems.at[k], recv_sem=recv_sems.at[k], device_id=peer, device_id_type=MESH)

        pairs.append((copy((x, y, c)), copy(peer)))
    return pairs


def _small_exchange_start(pack):
    slots = lax.empty((N_DEV,) + pack.shape, F32)

    def body(pack_ref, slots_ref, send_sems, recv_sems, pack_thru, slots_thru, token):
        for mine, _ in _small_exchange_copies(pack_ref, slots_ref, send_sems, recv_sems):
            mine.start()
        token[...] = jnp.zeros_like(token)

    return pl.pallas_call(
        body, name="grad_small_exchange_start",
        out_shape=(pltpu.SemaphoreType.DMA((N_DEV - 1,)), pltpu.SemaphoreType.DMA((N_DEV - 1,)),
                   pltpu.HBM(pack.shape, F32), pltpu.HBM(slots.shape, F32), TOKEN),
        in_specs=[HBM, HBM], out_specs=(SEM, SEM, HBM, HBM, pl.BlockSpec(memory_space=pltpu.VMEM)),
        input_output_aliases={0: 2, 1: 3},
        compiler_params=_split_params(),
    )(_in_hbm(pack), _in_hbm(slots))


def _small_exchange_wait(send_sems, recv_sems, pack, slots, after):
    def body(pack_ref, slots_ref, send, recv, after_ref, pack_thru, slots_thru):
        for mine, theirs in _small_exchange_copies(pack_ref, slots_ref, send, recv):
            mine.wait_send()
            theirs.wait_recv()

    return pl.pallas_call(
        body, name="grad_small_exchange_wait",
        out_shape=(pltpu.HBM(pack.shape, F32), pltpu.HBM(slots.shape, F32)),
        in_specs=[HBM, HBM, SEM, SEM, ANY], out_specs=(HBM, HBM),
        input_output_aliases={0: 0, 1: 1},
        compiler_params=_split_params(),
    )(pack, slots, send_sems, recv_sems, after)


def _sum_small(pack, slots, me):
    R, C = pack.shape
    tr = _row_tile(R)

    def body(me_ref, p_ref, q_ref, o_ref):
        acc = jnp.where(me_ref[0] == 0, p_ref[...], q_ref[0])
        for d in range(1, N_DEV):
            acc = acc + jnp.where(me_ref[0] == d, p_ref[...], q_ref[d])
        o_ref[...] = acc

    return pl.pallas_call(
        body, name="sum_small",
        grid_spec=_scalar_grid((R // tr,), [pl.BlockSpec((tr, C), lambda i, m: (i, 0)),
                                            pl.BlockSpec((N_DEV, tr, C), lambda i, m: (0, i, 0))],
                               pl.BlockSpec((tr, C), lambda i, m: (i, 0))),
        out_shape=jax.ShapeDtypeStruct((R, C), F32),
        compiler_params=_params(("arbitrary",)),
    )(me.reshape(1), pack, slots)


def _pack_rows(parts, rows):
    flat = jnp.concatenate([a.reshape(-1) for a in parts])
    return jnp.pad(flat, (0, rows * 128 - flat.shape[0])).reshape(rows, 128)


def _unpack_rows(pack, shapes):
    flat = pack.reshape(-1)
    out, at = [], 0
    for s in shapes:
        size = 1
        for d in s:
            size *= d
        out.append(flat[at:at + size].reshape(s))
        at += size
    return out


def kernel(x, p, norm1_g, w_in, b_gate, pool_w, pool_scale, pool_proj, conv_w, conv_b, w_rg, b_rg, w_ig, b_ig, lru_lambda, lru_proj, w_out, norm2_g, w_ffn_in, w_ffn_out, ple_norm_g, w_ple_gate, w_ple_proj, final_g, loss_target, m_norm1_g, m_w_in, m_b_gate, m_pool_w, m_pool_scale, m_pool_proj, m_conv_w, m_conv_b, m_w_rg, m_b_rg, m_w_ig, m_b_ig, m_lru_lambda, m_lru_proj, m_w_out, m_norm2_g, m_w_ffn_in, m_w_ffn_out, m_ple_norm_g, m_w_ple_gate, m_w_ple_proj, m_final_g, v_norm1_g, v_w_in, v_b_gate, v_pool_w, v_pool_scale, v_pool_proj, v_conv_w, v_conv_b, v_w_rg, v_b_rg, v_w_ig, v_b_ig, v_lru_lambda, v_lru_proj, v_w_out, v_norm2_g, v_w_ffn_in, v_w_ffn_out, v_ple_norm_g, v_w_ple_gate, v_w_ple_proj, v_final_g):
    weights = dict(norm1_g=norm1_g, w_in=w_in, b_gate=b_gate, pool_w=pool_w, pool_scale=pool_scale,
                   pool_proj=pool_proj, conv_w=conv_w, conv_b=conv_b, w_rg=w_rg, b_rg=b_rg, w_ig=w_ig, b_ig=b_ig,
                   lru_lambda=lru_lambda, lru_proj=lru_proj, w_out=w_out, norm2_g=norm2_g, w_ffn_in=w_ffn_in,
                   w_ffn_out=w_ffn_out, ple_norm_g=ple_norm_g, w_ple_gate=w_ple_gate, w_ple_proj=w_ple_proj,
                   final_g=final_g)
    m_in = dict(norm1_g=m_norm1_g, w_in=m_w_in, b_gate=m_b_gate, pool_w=m_pool_w, pool_scale=m_pool_scale,
                pool_proj=m_pool_proj, conv_w=m_conv_w, conv_b=m_conv_b, w_rg=m_w_rg, b_rg=m_b_rg, w_ig=m_w_ig,
                b_ig=m_b_ig, lru_lambda=m_lru_lambda, lru_proj=m_lru_proj, w_out=m_w_out, norm2_g=m_norm2_g,
                w_ffn_in=m_w_ffn_in, w_ffn_out=m_w_ffn_out, ple_norm_g=m_ple_norm_g, w_ple_gate=m_w_ple_gate,
                w_ple_proj=m_w_ple_proj, final_g=m_final_g)
    v_in = dict(norm1_g=v_norm1_g, w_in=v_w_in, b_gate=v_b_gate, pool_w=v_pool_w, pool_scale=v_pool_scale,
                pool_proj=v_pool_proj, conv_w=v_conv_w, conv_b=v_conv_b, w_rg=v_w_rg, b_rg=v_b_rg, w_ig=v_w_ig,
                b_ig=v_b_ig, lru_lambda=v_lru_lambda, lru_proj=v_lru_proj, w_out=v_w_out, norm2_g=v_norm2_g,
                w_ffn_in=v_w_ffn_in, w_ffn_out=v_w_ffn_out, ple_norm_g=v_ple_norm_g, w_ple_gate=v_w_ple_gate,
                w_ple_proj=v_w_ple_proj, final_g=v_final_g)
    names = list(weights)
    big = ["w_in", "pool_proj", "lru_proj", "w_out", "w_ffn_in", "w_ffn_out", "w_ple_gate", "w_ple_proj"]
    by_rows = [n in ("lru_proj", "w_out", "w_ffn_out", "w_ple_gate") for n in big]
    small = [n for n in names if n not in big]

    shard_j = 2 * lax.axis_index("x") + lax.axis_index("y")
    T = x.shape[1]
    xs, ps, tgt = x[0], p[0, 0], loss_target[0]

    small_local = jnp.concatenate([b_gate[0], conv_w[0], jnp.zeros((2, 256), F32)], axis=0)
    core = lax.axis_index("c").astype(jnp.int32)
    place = jnp.stack([shard_j, core]).astype(jnp.int32)
    rows_of = dict(zip(big, by_rows))
    shard_shape = {n: weights[n].shape[1:] for n in big}
    blocks = {n: _cast_into_block(weights[n][0], rows_of[n], place[0], "cast_" + n) for n in big}
    early, late = big[:4], big[4:]
    gathered = _all_gather_weights([blocks[n] for n in early], [shard_shape[n] for n in early],
                                   [rows_of[n] for n in early], small_local)
    full = dict(zip(early, gathered[:-1]))
    late_send, late_recv, late_bufs, late_token = _gather_rest_start(
        [blocks[n] for n in late], [shard_shape[n] for n in late], [rows_of[n] for n in late], gathered[-1])
    b_gate_full = gathered[-1][0:2].reshape(1, 2 * D_MODEL)
    conv_w_full = gathered[-1][2:6]
    pool_w_b, w_rg_b, w_ig_b = pool_w[0].astype(BF16), w_rg[0].astype(BF16), w_ig[0].astype(BF16)
    b_rg_row, b_ig_row = b_rg.reshape(1, D_MODEL), b_ig.reshape(1, D_MODEL)
    final_row = final_g.reshape(1, D_MODEL)

    zp, zl, zg, zt, u, h1, hs, yp, yl, xc_saved, r_saved, ig_saved = _f12_mixer(
        xs, norm1_g + late_token[0, 0], full["w_in"], b_gate_full, pool_w_b, pool_scale, full["pool_proj"],
        conv_w_full, conv_b, w_rg_b, b_rg_row, w_ig_b, b_ig_row, lru_lambda, full["lru_proj"], full["w_out"])
    full.update(zip(late, _gather_rest_wait(late_send, late_recv, late_bufs, [shard_shape[n] for n in late],
                                            [rows_of[n] for n in late], h1)))
    h2, v, ff, act = _f3_ffn(h1, norm2_g, full["w_ffn_in"], full["w_ffn_out"])

    loss_sum, dh2, g_ple_gate, g_ple_proj, vec4 = _b4_ple_loss(
        h2, ps, tgt, ple_norm_g, full["w_ple_gate"], full["w_ple_proj"], final_row)
    dff, dh1, vec3 = _b3_ffn(dh2, h1, ff, norm2_g, full["w_ffn_in"], full["w_ffn_out"])
    g_ffn_in = _wgrad(v, dff, 2 * D_FF // N_SHARDS, "wgrad_ffn_in")
    g_ffn_out = _wgrad(act, dh2, D_MODEL, "wgrad_ffn_out", tokens=WGRAD_TOKENS // 2)

    def stack(n, g):
        return g.reshape(N_SHARDS, g.shape[0] // N_SHARDS, g.shape[1]) if rows_of[n] else g[None]

    def chip_sums_of(group, grads_of, tag):
        stacked = [stack(n, grads_of[n]) for n in group]
        theirs = _core_exchange(stacked, "grad_core_exchange_" + tag)
        return [_sum_cores(g, t, core, "sum_cores_" + n) for g, t, n in zip(stacked, theirs, group)]

    late_rows = [rows_of[n] for n in late]
    late_sums = chip_sums_of(late, dict(w_ffn_in=g_ffn_in, w_ffn_out=g_ffn_out, w_ple_gate=g_ple_gate,
                                        w_ple_proj=g_ple_proj), "late")
    ex_send, ex_recv, late_sums, late_slots, ex_token = _chip_exchange_start(late_sums, late_rows)
    dzt, dyp, dyl, g_w_out, vec_g = _b2_gates(dh1, zt, yp, yl, b_gate_full + ex_token[0, 0], full["w_out"])
    dzl, dzg, g_lru_proj, g_w_rg, g_w_ig, vec_l = _b2_lru(
        dyl, zl, zg, hs, xc_saved, r_saved, ig_saved, conv_w_full, w_rg_b, w_ig_b, lru_lambda, full["lru_proj"])
    dzp, grad_x, g_pool_proj, g_pool_w, vec_p = _b12_pool_in_proj(
        dyp, zp, dzl, dzg, dzt, xs, dh1, norm1_g, full["w_in"], pool_w_b, pool_scale, full["pool_proj"])
    small_full = dict(
        norm1_g=vec_p[1], b_gate=vec_g[0:2], pool_w=g_pool_w, pool_scale=vec_p[0, :POOL_WIDTH],
        conv_w=vec_l[_V_CONVW:_V_CONVW + CONV_WIDTH], conv_b=vec_l[_V_CONVB], w_rg=g_w_rg, b_rg=vec_l[_V_BRG],
        w_ig=g_w_ig, b_ig=vec_l[_V_BIG], lru_lambda=vec_l[_V_LAM], norm2_g=vec3[0], ple_norm_g=vec4[1],
        final_g=vec4[0])
    full_shapes = [small_full[n].shape for n in small]
    n_full = sum(int(small_full[n].size) for n in small)
    rows_full = -(-n_full // (128 * ROW_TILE)) * ROW_TILE
    sm_send, sm_recv, sm_pack, sm_slots, sm_token = _small_exchange_start(
        _pack_rows([small_full[n] for n in small], rows_full))
    g_w_in = jnp.concatenate([
        _wgrad(u, dzp, POOL_WIDTH, "wgrad_in_pool", after=sm_token),
        _wgrad(u, dzl, D_MODEL, "wgrad_in_lru", after=sm_token),
        _wgrad(u, dzg, D_MODEL, "wgrad_in_gelu", after=sm_token),
        _wgrad(u, dzt, D_MODEL, "wgrad_in_gate", after=sm_token)], axis=1)

    loss = lax.psum(loss_sum[0, 0] * (0.5 / D_MODEL), ("x", "y", "c"))

    early_rows = [rows_of[n] for n in early]
    early_sums = chip_sums_of(early, dict(w_in=g_w_in, pool_proj=g_pool_proj, lru_proj=g_lru_proj, w_out=g_w_out),
                              "early")
    early_slots = _chip_exchange(early_sums, early_rows)
    late_sums, late_slots = _chip_exchange_wait(ex_send, ex_recv, late_sums, late_slots, late_rows, g_w_in)
    reduced = _core_share([_sum_chips(s, q, rows_of[n], place, "sum_chips_" + n)
                           for s, q, n in zip(list(early_sums) + list(late_sums),
                                              list(early_slots) + list(late_slots), big)])

    grads, deltas, new_m, new_v = {}, {}, {}, {}
    for n, r in zip(big, reduced):
        g = r.reshape(r.shape[0] * r.shape[1], r.shape[2])
        d, nm, nv = _adamw(weights[n][0], g, m_in[n][0], v_in[n][0], "adamw_" + n)
        grads[n], deltas[n], new_m[n], new_v[n] = g[None], d[None], nm[None], nv[None]

    sm_pack, sm_slots = _small_exchange_wait(sm_send, sm_recv, sm_pack, sm_slots, g_w_in)
    device = (4 * lax.axis_index("x") + 2 * lax.axis_index("y") + lax.axis_index("c")).astype(jnp.int32)
    summed = dict(zip(small, _unpack_rows(_sum_small(sm_pack, sm_slots, device), full_shapes)))
    summed["b_gate"] = lax.dynamic_slice_in_dim(summed["b_gate"], shard_j * 256, 256, axis=1)
    summed["conv_w"] = lax.dynamic_slice_in_dim(summed["conv_w"], shard_j * 256, 256, axis=1)
    local_shapes = [weights[n].shape for n in small]
    n_local = sum(int(weights[n].size) for n in small)
    rows_local = -(-n_local // (128 * ROW_TILE)) * ROW_TILE
    packs = [_pack_rows([src[n] for n in small], rows_local) for src in (weights, summed, m_in, v_in)]
    d_s, nm_s, nv_s = _adamw(*packs, "adamw_small")
    for dst, pack in ((grads, packs[1]), (deltas, d_s), (new_m, nm_s), (new_v, nv_s)):
        dst.update(zip(small, _unpack_rows(pack, local_shapes)))

    return (loss, grad_x[None], *[grads[n] for n in names], *[deltas[n] for n in names],
            *[new_m[n] for n in names], *[new_v[n] for n in names])
```

```python
import functools

import jax
import jax.numpy as jnp
from jax import lax
from jax.experimental import pallas as pl
from jax.experimental.pallas import tpu as pltpu

F32 = jnp.float32
BF16 = jnp.bfloat16

D_MODEL = 1024
POOL_WIDTH = 512
POOL_GROUP_DIM = 128
POOL_WINDOWS = (2, 4, 8, 16)
POOL_HALO = 16
LRU_HEADS = 8
LRU_HEAD_DIM = 128
CONV_WIDTH = 4
LRU_C = 8.0
D_FF = 2816
PLE_DIM = 256
RMS_EPS = 1e-6
N_SHARDS = 4
N_DEV = 8

ADAM_LR = 0.001
ADAM_B1 = 0.9
ADAM_B2 = 0.999
ADAM_EPS = 1e-08
ADAM_WD = 0.01
ADAM_STEP = 10

ROW_TILE = 256
WIDE_TILE = 512
WGRAD_TOKENS = 2048
SUBLANES = 8
VMEM_LIMIT = 56 * 1024 * 1024
MESH = pl.DeviceIdType.MESH
ANY = pl.BlockSpec(memory_space=pl.ANY)


def _params(semantics=None):
    return pltpu.CompilerParams(dimension_semantics=semantics, vmem_limit_bytes=VMEM_LIMIT)


def _resident(shape):
    n = len(shape)
    return pl.BlockSpec(shape, lambda *_: (0,) * n, pipeline_mode=pl.Buffered(1))


def _acc(shape):
    n = len(shape)
    return pl.BlockSpec(shape, lambda *_: (0,) * n)


def _rows(tile, cols):
    return pl.BlockSpec((tile, cols), lambda i: (i, 0))


def _rows_rev(tile, cols, n_tiles):
    return pl.BlockSpec((tile, cols), lambda i: (n_tiles - 1 - i, 0))


def _halo_before_rev(rows, cols, tile, n_tiles):
    per = tile // rows
    return pl.BlockSpec((rows, cols), lambda i: (jnp.maximum((n_tiles - 1 - i) * per - 1, 0), 0))


def _nn(a, b):
    return jnp.dot(a, b, preferred_element_type=F32)


def _nt(a, b):
    return lax.dot_general(a, b, (((1,), (1,)), ((), ())), preferred_element_type=F32)


def _tn(a, b):
    return lax.dot_general(a, b, (((0,), (0,)), ((), ())), preferred_element_type=F32)


def _rms(x):
    r = lax.rsqrt(jnp.mean(x * x, axis=-1, keepdims=True) + RMS_EPS)
    return x * r, r


def _rms_bwd(dn, n, r):
    return r * (dn - n * jnp.mean(dn * n, axis=-1, keepdims=True))


def _sigmoid(x):
    return 0.5 * jnp.tanh(0.5 * x) + 0.5


_GELU_C = 0.7978845608028654
_GELU_A = 0.044715


def _gelu(x):
    t = jnp.tanh(_GELU_C * (x + _GELU_A * x * x * x))
    return 0.5 * x * (1.0 + t)


def _gelu_and_grad(x):
    x2 = x * x
    t = jnp.tanh(_GELU_C * (x + _GELU_A * x2 * x))
    cdf = 0.5 * (1.0 + t)
    grad = cdf + 0.5 * x * (1.0 - t * t) * _GELU_C * (1.0 + 3.0 * _GELU_A * x2)
    return x * cdf, grad


def _softplus_neg(lam):
    e = jnp.exp(-jnp.abs(lam))
    sp = jnp.maximum(-lam, 0.0) + jnp.log1p(e)
    return sp, -_sigmoid(-lam)


def _colsum(v):
    return jnp.sum(v, axis=0, keepdims=True)


def _row_ids(shape):
    return lax.broadcasted_iota(jnp.int32, shape, 0)


def _shift_down(cat, k):
    return pltpu.roll(cat, k, 0) if k else cat


def _shift_up(cat, k):
    return pltpu.roll(cat, cat.shape[0] - k, 0) if k else cat


IN_SPLITS = (0, POOL_WIDTH, POOL_WIDTH + D_MODEL, POOL_WIDTH + 2 * D_MODEL, POOL_WIDTH + 4 * D_MODEL)
IN_WIDTHS = tuple(IN_SPLITS[k + 1] - IN_SPLITS[k] for k in range(4))
PROJ_CHUNK = 256
PAIR_DIM = 2 * LRU_HEAD_DIM


def _pair_blocks(w):
    zero = jnp.zeros_like(w[0::2])
    return jnp.concatenate([jnp.concatenate([w[0::2], zero], axis=2), jnp.concatenate([zero, w[1::2]], axis=2)], axis=1)


def _unpair_blocks(w):
    n, d2, _ = w.shape
    d = d2 // 2
    return jnp.stack([w[:, :d, :d], w[:, d:, d:]], axis=1).reshape(2 * n, d, d)


def _no_tick():
    pass


class _Interleaved:
    def __init__(self, pieces):
        self._pieces = iter(pieces)

    def tick(self, n=1):
        for _ in range(n):
            piece = next(self._pieces, None)
            if piece is not None:
                piece()

    def flush(self):
        for piece in self._pieces:
            piece()


def _pool_forward(zp_cat, pw_ref, first_row, tick=_no_tick):
    tt = zp_cat.shape[0] - POOL_HALO
    t_glob = first_row + _row_ids((tt, POOL_GROUP_DIM))
    pooled, mixed = [], []
    for g, w in enumerate(POOL_WINDOWS):
        cat = zp_cat[:, g * POOL_GROUP_DIM:(g + 1) * POOL_GROUP_DIM]
        s, k = cat, 1
        while k < w:
            s = s + _shift_down(s, k)
            k *= 2
        cnt = jnp.minimum(t_glob + 1, w).astype(F32)
        pooled.append(s[POOL_HALO:] / cnt - cat[POOL_HALO:])
        if g % 2:
            pair = jnp.concatenate(pooled[-2:], axis=1).astype(BF16)
            mixed.append(_nn(pair, pw_ref[g // 2]))
        tick()
    return jnp.concatenate(pooled, axis=1), jnp.concatenate(mixed, axis=1)


def _lru_gates(zl_cat, conv_w, conv_b, wrg_ref, brg, wig_ref, big, sp, first_row, tick=_no_tick):
    tt = zl_cat.shape[0] - SUBLANES
    xc = conv_w[CONV_WIDTH - 1:CONV_WIDTH] * zl_cat
    for k in range(1, CONV_WIDTH):
        xc = xc + conv_w[CONV_WIDTH - 1 - k:CONV_WIDTH - k] * _shift_down(zl_cat, k)
        tick()
    xc = xc[SUBLANES:] + conv_b
    xh = xc.astype(BF16)
    pr, pi = [], []
    for p in range(LRU_HEADS // 2):
        xs = xh[:, p * PAIR_DIM:(p + 1) * PAIR_DIM]
        pr.append(_nn(xs, wrg_ref[p]))
        pi.append(_nn(xs, wig_ref[p]))
    r = _sigmoid(jnp.concatenate(pr, axis=1) + brg)
    tick()
    ig = _sigmoid(jnp.concatenate(pi, axis=1) + big)
    tick()
    a, mult = _decay(r, sp, first_row, tick)
    tick()
    return xc, r, ig, a, mult


def _decay(r, sp, first_row, tick=_no_tick):
    a = jnp.exp(-LRU_C * r * sp)
    tick()
    mult = jnp.sqrt(jnp.maximum(1.0 - a * a, 0.0))
    t_glob = first_row + _row_ids(r.shape)
    return a, jnp.where(t_glob == 0, 1.0, mult)


def _f12_mixer(x, norm1_g, w_in, b_gate, pool_w, pool_scale, pool_proj, conv_w, conv_b, w_rg, b_rg, w_ig, b_ig,
               lru_lambda, lru_proj, w_out):
    T = x.shape[0]
    tt = ROW_TILE
    nt = T // tt
    n_groups = tt // SUBLANES
    proj_mid = IN_SPLITS[3] + D_MODEL // 2

    def body(xm_ref, x_ref, g1_ref, win_ref, bg_ref, pw_ref, ps_ref, pp_ref, cw_ref, cb_ref,
             wrg_ref, brg_ref, wig_ref, big_ref, lam_ref, lp_ref, wo_ref,
             zp_ref, zl_ref, zg_ref, zt_ref, u_ref, h1_ref, hs_ref, yp_ref, yl_ref, xc_ref, r_ref, ig_ref,
             zbuf, zp_halo, zl_halo, a_s, b_s, carry_s):
        s = pl.program_id(0)

        @pl.when(s == 0)
        def _():
            zbuf[1] = jnp.zeros((tt, IN_SPLITS[4]), F32)
            zp_halo[...] = jnp.zeros_like(zp_halo)
            zl_halo[...] = jnp.zeros_like(zl_halo)
            carry_s[...] = jnp.zeros_like(carry_s)

        z_new, z_old = zbuf.at[s % 2], zbuf.at[(s + 1) % 2]
        first = s <= 1
        first_row = jnp.maximum(s - 1, 0) * tt

        n1, _ = _rms(xm_ref[...])
        u = (n1 * g1_ref[...]).astype(BF16)
        u_ref[...] = u

        z_refs = (zp_ref, zl_ref, zg_ref, zt_ref)

        def project(lo):
            k = max(i for i in range(4) if IN_SPLITS[i] <= lo)
            part = _nn(u, win_ref[:, lo:lo + PROJ_CHUNK])
            z_new[:, lo:lo + PROJ_CHUNK] = part
            z_refs[k][:, lo - IN_SPLITS[k]:lo - IN_SPLITS[k] + PROJ_CHUNK] = part.astype(z_refs[k].dtype)

        before_scan = _Interleaved(functools.partial(project, lo) for lo in range(0, proj_mid, PROJ_CHUNK))
        after_scan = _Interleaved(functools.partial(project, lo) for lo in range(proj_mid, IN_SPLITS[4], PROJ_CHUNK))

        zp_cat = jnp.concatenate([jnp.where(first, 0.0, zp_halo[...]), z_old[:, IN_SPLITS[0]:IN_SPLITS[1]]], axis=0)
        _, mixed = _pool_forward(zp_cat, pw_ref, first_row, before_scan.tick)
        y_pool = _nn((mixed * ps_ref[...]).astype(BF16), pp_ref[...])

        sp, _ = _softplus_neg(lam_ref[...])
        zl_cat = jnp.concatenate([jnp.where(first, 0.0, zl_halo[...]), z_old[:, IN_SPLITS[1]:IN_SPLITS[2]]], axis=0)
        xc, r, ig, a, mult = _lru_gates(zl_cat, cw_ref[...], cb_ref[...], wrg_ref, brg_ref[...], wig_ref,
                                        big_ref[...], sp, first_row, before_scan.tick)
        a_s[...] = a
        b_s[...] = mult * ig * xc
        xc_ref[...] = xc.astype(BF16)
        r_ref[...] = r.astype(BF16)
        ig_ref[...] = ig.astype(BF16)
        before_scan.flush()

        rows8 = _row_ids((SUBLANES, D_MODEL))

        def group(g, carry):
            at = pl.ds(pl.multiple_of(g * SUBLANES, SUBLANES), SUBLANES)
            A, B = a_s[at, :], b_s[at, :]
            for s in (1, 2, 4):
                m = rows8 >= s
                B = jnp.where(m, A * pltpu.roll(B, s, 0) + B, B)
                A = jnp.where(m, A * pltpu.roll(A, s, 0), A)
            h = A * carry + B
            hs_ref[at, :] = h
            return jnp.broadcast_to(h[SUBLANES - 1:SUBLANES, :], (SUBLANES, D_MODEL))

        carry_s[...] = lax.fori_loop(0, n_groups, group, jnp.where(first, 0.0, carry_s[...]))
        gelu = _gelu(z_old[:, IN_SPLITS[2]:IN_SPLITS[3]])
        after_scan.tick(2)
        y_lru = _nn((hs_ref[...] * gelu).astype(BF16), lp_ref[...])

        gates = _sigmoid(z_old[:, IN_SPLITS[3]:IN_SPLITS[4]] + bg_ref[...])
        after_scan.tick(2)
        merged = gates[:, :D_MODEL] * y_pool + gates[:, D_MODEL:] * y_lru
        after_scan.flush()
        h1_ref[...] = x_ref[...] + _nn(merged.astype(BF16), wo_ref[...])
        yp_ref[...] = y_pool.astype(BF16)
        yl_ref[...] = y_lru.astype(BF16)
        zp_halo[...] = z_old[tt - POOL_HALO:, IN_SPLITS[0]:IN_SPLITS[1]]
        zl_halo[...] = z_old[tt - SUBLANES:, IN_SPLITS[1]:IN_SPLITS[2]]

    def ahead(cols):
        return pl.BlockSpec((tt, cols), lambda s: (jnp.minimum(s, nt - 1), 0))

    def behind(cols):
        return pl.BlockSpec((tt, cols), lambda s: (jnp.maximum(s - 1, 0), 0))

    res = [norm1_g, w_in, b_gate, pool_w, pool_scale, pool_proj, conv_w, conv_b, w_rg, b_rg, w_ig, b_ig, lru_lambda,
           lru_proj, w_out]
    return pl.pallas_call(
        body, name="f12_mixer", grid=(nt + 1,),
        in_specs=[ahead(D_MODEL), behind(D_MODEL)] + [_resident(w.shape) for w in res],
        out_specs=[ahead(w) for w in IN_WIDTHS] + [ahead(D_MODEL)] + [behind(D_MODEL)] * 7,
        out_shape=[jax.ShapeDtypeStruct((T, w), dt) for w, dt in zip(IN_WIDTHS, (F32, F32, F32, BF16))]
        + [jax.ShapeDtypeStruct((T, D_MODEL), BF16), jax.ShapeDtypeStruct((T, D_MODEL), F32),
           jax.ShapeDtypeStruct((T, D_MODEL), F32)] + [jax.ShapeDtypeStruct((T, D_MODEL), BF16)] * 5,
        scratch_shapes=[pltpu.VMEM((2, tt, IN_SPLITS[4]), F32), pltpu.VMEM((POOL_HALO, POOL_WIDTH), F32),
                        pltpu.VMEM((SUBLANES, D_MODEL), F32), pltpu.VMEM((tt, D_MODEL), F32),
                        pltpu.VMEM((tt, D_MODEL), F32), pltpu.VMEM((SUBLANES, D_MODEL), F32)],
        compiler_params=_params(("arbitrary",)),
    )(x, x, *res)


def _f3_ffn(h1, norm2_g, w_ffn_in, w_ffn_out):
    T = h1.shape[0]
    tm = ROW_TILE

    def body(h_ref, g_ref, wi_ref, wo_ref, h2_ref, v_ref, ff_ref, act_ref):
        h = h_ref[...]
        n, _ = _rms(h)
        v = (n * g_ref[...]).astype(BF16)
        v_ref[...] = v
        g_ff = _nn(v, wi_ref[:, :D_FF])
        u_ff = _nn(v, wi_ref[:, D_FF:])
        ff_ref[:, :D_FF] = g_ff.astype(BF16)
        ff_ref[:, D_FF:] = u_ff.astype(BF16)
        act = (g_ff * _sigmoid(g_ff) * u_ff).astype(BF16)
        act_ref[...] = act
        h2_ref[...] = h + _nn(act, wo_ref[...])

    return pl.pallas_call(
        body, name="f3_ffn", grid=(T // tm,),
        in_specs=[_rows(tm, D_MODEL), _resident((1, D_MODEL)), _resident(w_ffn_in.shape), _resident(w_ffn_out.shape)],
        out_specs=[_rows(tm, D_MODEL), _rows(tm, D_MODEL), _rows(tm, 2 * D_FF), _rows(tm, D_FF)],
        out_shape=[jax.ShapeDtypeStruct((T, D_MODEL), F32), jax.ShapeDtypeStruct((T, D_MODEL), BF16),
                   jax.ShapeDtypeStruct((T, 2 * D_FF), BF16), jax.ShapeDtypeStruct((T, D_FF), BF16)],
        compiler_params=_params(("arbitrary",)),
    )(h1, norm2_g, w_ffn_in, w_ffn_out)


def _b4_ple_loss(h2, p, target, ple_norm_g, w_ple_gate, w_ple_proj, final_g):
    T = h2.shape[0]
    tm = WIDE_TILE

    def body(h_ref, p_ref, t_ref, gp_ref, wg_ref, wp_ref, gf_ref, loss_ref, dh2_ref, dwg_ref, dwp_ref, vec_ref):
        @pl.when(pl.program_id(0) == 0)
        def _():
            loss_ref[...] = jnp.zeros_like(loss_ref)
            dwg_ref[...] = jnp.zeros_like(dwg_ref)
            dwp_ref[...] = jnp.zeros_like(dwp_ref)
            vec_ref[...] = jnp.zeros_like(vec_ref)

        h2v = h_ref[...]
        n3, r3 = _rms(h2v)
        n3g = (n3 * gp_ref[...]).astype(BF16)
        pg = _sigmoid(_nn(n3g, wg_ref[...]))
        pb = p_ref[...].astype(BF16)
        e = _nn(pb, wp_ref[...])
        h3 = h2v + pg * e
        n4, r4 = _rms(h3)
        diff = n4 * gf_ref[...] - t_ref[...]
        loss_ref[...] += jnp.sum(diff * diff).reshape(1, 1)
        dy = diff * (1.0 / D_MODEL)
        vec_ref[0:1, :] += _colsum(dy * n4)
        dh3 = _rms_bwd(dy * gf_ref[...], n4, r4)
        dwp_ref[...] += _tn(pb, (dh3 * pg).astype(BF16))
        dq = (dh3 * e * pg * (1.0 - pg)).astype(BF16)
        dwg_ref[...] += _tn(n3g, dq)
        dn3g = _nt(dq, wg_ref[...])
        vec_ref[1:2, :] += _colsum(dn3g * n3)
        dh2_ref[...] = dh3 + _rms_bwd(dn3g * gp_ref[...], n3, r3)

    return pl.pallas_call(
        body, name="b4_ple_loss", grid=(T // tm,),
        in_specs=[_rows(tm, D_MODEL), _rows(tm, PLE_DIM), _rows(tm, D_MODEL), _resident((1, D_MODEL)),
                  _resident(w_ple_gate.shape), _resident(w_ple_proj.shape), _resident((1, D_MODEL))],
        out_specs=[_acc((1, 1)), _rows(tm, D_MODEL), _acc(w_ple_gate.shape), _acc(w_ple_proj.shape),
                   _acc((SUBLANES, D_MODEL))],
        out_shape=[jax.ShapeDtypeStruct((1, 1), F32), jax.ShapeDtypeStruct((T, D_MODEL), F32),
                   jax.ShapeDtypeStruct(w_ple_gate.shape, F32), jax.ShapeDtypeStruct(w_ple_proj.shape, F32),
                   jax.ShapeDtypeStruct((SUBLANES, D_MODEL), F32)],
        compiler_params=_params(("arbitrary",)),
    )(h2, p, target, ple_norm_g, w_ple_gate, w_ple_proj, final_g)


def _b3_ffn(dh2, h1, ff, norm2_g, w_ffn_in, w_ffn_out):
    T = h1.shape[0]
    tm = ROW_TILE

    def body(d_ref, h_ref, ff_ref, g_ref, wi_ref, wo_ref, dff_ref, dh1_ref, vec_ref):
        @pl.when(pl.program_id(0) == 0)
        def _():
            vec_ref[...] = jnp.zeros_like(vec_ref)

        dh2v = d_ref[...]
        dact = _nt(dh2v.astype(BF16), wo_ref[...])
        g_ff = ff_ref[:, :D_FF].astype(F32)
        u_ff = ff_ref[:, D_FF:].astype(F32)
        s = _sigmoid(g_ff)
        dg = (dact * u_ff * (s * (1.0 + g_ff * (1.0 - s)))).astype(BF16)
        du = (dact * (g_ff * s)).astype(BF16)
        dff_ref[:, :D_FF] = dg
        dff_ref[:, D_FF:] = du
        dv = _nt(dg, wi_ref[:, :D_FF]) + _nt(du, wi_ref[:, D_FF:])
        n2, r2 = _rms(h_ref[...])
        vec_ref[0:1, :] += _colsum(dv * n2)
        dh1_ref[...] = dh2v + _rms_bwd(dv * g_ref[...], n2, r2)

    return pl.pallas_call(
        body, name="b3_ffn", grid=(T // tm,),
        in_specs=[_rows(tm, D_MODEL), _rows(tm, D_MODEL), _rows(tm, 2 * D_FF), _resident((1, D_MODEL)),
                  _resident(w_ffn_in.shape), _resident(w_ffn_out.shape)],
        out_specs=[_rows(tm, 2 * D_FF), _rows(tm, D_MODEL), _acc((SUBLANES, D_MODEL))],
        out_shape=[jax.ShapeDtypeStruct((T, 2 * D_FF), BF16), jax.ShapeDtypeStruct((T, D_MODEL), F32),
                   jax.ShapeDtypeStruct((SUBLANES, D_MODEL), F32)],
        compiler_params=_params(("arbitrary",)),
    )(dh2, h1, ff, norm2_g, w_ffn_in, w_ffn_out)


def _wgrad(a, b, col_tile, name, tokens=WGRAD_TOKENS, after=None):
    T, K = a.shape
    N = b.shape[1]
    tk = min(T, tokens)

    def body(a_ref, b_ref, *rest):
        o_ref = rest[-1]

        @pl.when(pl.program_id(1) == 0)
        def _():
            o_ref[...] = jnp.zeros_like(o_ref)

        o_ref[...] += _tn(a_ref[...].astype(BF16), b_ref[...].astype(BF16))

    return pl.pallas_call(
        body, name=name, grid=(N // col_tile, T // tk),
        in_specs=[pl.BlockSpec((tk, K), lambda j, k: (k, 0)), pl.BlockSpec((tk, col_tile), lambda j, k: (k, j))]
        + ([] if after is None else [ANY]),
        out_specs=pl.BlockSpec((K, col_tile), lambda j, k: (0, j)),
        out_shape=jax.ShapeDtypeStruct((K, N), F32),
        compiler_params=_params(("arbitrary", "arbitrary")),
    )(a, b, *([] if after is None else [after]))


def _b2_gates(dh1, zt, yp, yl, b_gate, w_out):
    T = dh1.shape[0]
    tm = WIDE_TILE

    def body(d_ref, zt_ref, yp_ref, yl_ref, bg_ref, wo_ref, dzt_ref, dyp_ref, dyl_ref, dwo_ref, vec_ref):
        @pl.when(pl.program_id(0) == 0)
        def _():
            dwo_ref[...] = jnp.zeros_like(dwo_ref)
            vec_ref[...] = jnp.zeros_like(vec_ref)

        db = d_ref[...].astype(BF16)
        dm = _nt(db, wo_ref[...])
        gates = _sigmoid(zt_ref[...].astype(F32) + bg_ref[...])
        g0, g1 = gates[:, :D_MODEL], gates[:, D_MODEL:]
        y_pool, y_lru = yp_ref[...].astype(F32), yl_ref[...].astype(F32)
        dwo_ref[...] += _tn((g0 * y_pool + g1 * y_lru).astype(BF16), db)
        dz0 = dm * y_pool * g0 * (1.0 - g0)
        dz1 = dm * y_lru * g1 * (1.0 - g1)
        vec_ref[0:1, :] += _colsum(dz0)
        vec_ref[1:2, :] += _colsum(dz1)
        dzt_ref[:, :D_MODEL] = dz0.astype(BF16)
        dzt_ref[:, D_MODEL:] = dz1.astype(BF16)
        dyp_ref[...] = (dm * g0).astype(BF16)
        dyl_ref[...] = (dm * g1).astype(BF16)

    return pl.pallas_call(
        body, name="b2_gates", grid=(T // tm,),
        in_specs=[_rows(tm, D_MODEL), _rows(tm, 2 * D_MODEL), _rows(tm, D_MODEL), _rows(tm, D_MODEL),
                  _resident(b_gate.shape), _resident(w_out.shape)],
        out_specs=[_rows(tm, 2 * D_MODEL), _rows(tm, D_MODEL), _rows(tm, D_MODEL), _acc(w_out.shape),
                   _acc((SUBLANES, D_MODEL))],
        out_shape=[jax.ShapeDtypeStruct((T, 2 * D_MODEL), BF16), jax.ShapeDtypeStruct((T, D_MODEL), BF16),
                   jax.ShapeDtypeStruct((T, D_MODEL), BF16), jax.ShapeDtypeStruct(w_out.shape, F32),
                   jax.ShapeDtypeStruct((SUBLANES, D_MODEL), F32)],
        compiler_params=_params(("arbitrary",)),
    )(dh1, zt, yp, yl, b_gate, w_out)


def _b12_pool_in_proj(dyp, zp, dzl, dzg, dzt, x, dh1, norm1_g, w_in, pool_w, pool_scale, pool_proj):
    T = zp.shape[0]
    tt = ROW_TILE
    nt = T // tt

    def body(dy_ref, zp_ref, zph_ref, dzl_ref, dzg_ref, dzt_ref, x_ref, dh_ref, g1_ref, win_ref, pw_ref, ps_ref, pp_ref,
             dzp_ref, dx_ref, dpp_ref, dpw_ref, vec_ref, q_next):
        i = pl.program_id(0)
        ti = nt - 1 - i
        first_row = ti * tt

        @pl.when(i == 0)
        def _():
            dpp_ref[...] = jnp.zeros_like(dpp_ref)
            dpw_ref[...] = jnp.zeros_like(dpw_ref)
            vec_ref[...] = jnp.zeros_like(vec_ref)
            q_next[...] = jnp.zeros_like(q_next)

        du_parts = []

        def project(lo):
            k = max(i for i in range(4) if IN_SPLITS[i] <= lo)
            dz_ref = (None, dzl_ref, dzg_ref, dzt_ref)[k]
            at = lo - IN_SPLITS[k]
            part = _nt(dz_ref[:, at:at + PROJ_CHUNK], win_ref[:, lo:lo + PROJ_CHUNK])
            du_parts[:] = [part if not du_parts else du_parts[0] + part]

        mxu = _Interleaved(functools.partial(project, lo) for lo in range(IN_SPLITS[1], IN_SPLITS[4], PROJ_CHUNK))

        keep = (ti > 0).astype(F32)
        zp_cat = jnp.concatenate([zph_ref[...] * keep, zp_ref[...]], axis=0)
        pooled, mixed = _pool_forward(zp_cat, pw_ref, first_row, mxu.tick)
        dy = dy_ref[...]
        dpp_ref[...] += _tn((mixed * ps_ref[...]).astype(BF16), dy)
        mxu.tick(2)
        dms = _nt(dy, pp_ref[...])
        mxu.tick(2)
        vec_ref[0:1, :POOL_WIDTH] += _colsum(dms * mixed)
        dmixed = (dms * ps_ref[...]).astype(BF16)
        t_glob = first_row + _row_ids((tt, POOL_GROUP_DIM))
        dz, q_all, dpooled_pairs = [], [], []
        for p in range(len(POOL_WINDOWS) // 2):
            pair = slice(p * PAIR_DIM, (p + 1) * PAIR_DIM)
            dpw_ref[p] += _tn(pooled[:, pair].astype(BF16), dmixed[:, pair])
            dpooled_pairs.append(_nt(dmixed[:, pair], pw_ref[p]))
        dpooled_all = jnp.concatenate(dpooled_pairs, axis=1)
        for g, w in enumerate(POOL_WINDOWS):
            cols = slice(g * POOL_GROUP_DIM, (g + 1) * POOL_GROUP_DIM)
            dpooled = dpooled_all[:, cols]
            q = dpooled / jnp.minimum(t_glob + 1, w).astype(F32)
            q_all.append(q)
            s, k = jnp.concatenate([q, q_next[:, cols]], axis=0), 1
            while k < w:
                s = s + _shift_up(s, k)
                k *= 2
            dz.append(s[:tt] - dpooled)
            mxu.tick(2)
        dzp = jnp.concatenate(dz, axis=1).astype(BF16)
        dzp_ref[...] = dzp
        q_next[...] = jnp.concatenate([q[:POOL_HALO] for q in q_all], axis=1)
        mxu.flush()

        du = du_parts[0] + _nt(dzp, win_ref[:, IN_SPLITS[0]:IN_SPLITS[1]])
        n1, r1 = _rms(x_ref[...])
        vec_ref[1:2, :] += _colsum(du * n1)
        dx_ref[...] = dh_ref[...] + _rms_bwd(du * g1_ref[...], n1, r1)

    rev = functools.partial(_rows_rev, n_tiles=nt)
    res = [norm1_g, w_in, pool_w, pool_scale, pool_proj]
    return pl.pallas_call(
        body, name="b12_pool_in_proj", grid=(nt,),
        in_specs=[rev(tt, D_MODEL), rev(tt, POOL_WIDTH), _halo_before_rev(POOL_HALO, POOL_WIDTH, tt, nt),
                  rev(tt, D_MODEL), rev(tt, D_MODEL), rev(tt, 2 * D_MODEL), rev(tt, D_MODEL), rev(tt, D_MODEL)]
        + [_resident(w.shape) for w in res],
        out_specs=[rev(tt, POOL_WIDTH), rev(tt, D_MODEL), _acc(pool_proj.shape), _acc(pool_w.shape),
                   _acc((SUBLANES, D_MODEL))],
        out_shape=[jax.ShapeDtypeStruct((T, POOL_WIDTH), BF16), jax.ShapeDtypeStruct((T, D_MODEL), F32),
                   jax.ShapeDtypeStruct(pool_proj.shape, F32), jax.ShapeDtypeStruct(pool_w.shape, F32),
                   jax.ShapeDtypeStruct((SUBLANES, D_MODEL), F32)],
        scratch_shapes=[pltpu.VMEM((POOL_HALO, POOL_WIDTH), F32)],
        compiler_params=_params(("arbitrary",)),
    )(dyp, zp, zp, dzl, dzg, dzt, x, dh1, *res)


_V_CONVW, _V_CONVB, _V_BRG, _V_BIG, _V_LAM = 0, 4, 5, 6, 7


def _b2_lru(dyl, zl, zg, hs, xc_saved, r_saved, ig_saved, conv_w, w_rg, w_ig, lru_lambda, lru_proj):
    T = zl.shape[0]
    tt = ROW_TILE
    nt = T // tt
    n_groups = tt // SUBLANES

    def body(dy_ref, zl_ref, zlh_ref, zg_ref, hs_ref, hsh_ref, xc_ref, r_ref, ig_ref, cw_ref, wrg_ref, wig_ref,
             lam_ref, lp_ref, dzl_ref, dzg_ref, dlp_ref, dwrg_ref, dwig_ref, vec_ref,
             c_s, d_s, g_s, g_next, a_next, dxc_next):
        i = pl.program_id(0)
        ti = nt - 1 - i
        first_row = ti * tt

        @pl.when(i == 0)
        def _():
            dlp_ref[...] = jnp.zeros_like(dlp_ref)
            dwrg_ref[...] = jnp.zeros_like(dwrg_ref)
            dwig_ref[...] = jnp.zeros_like(dwig_ref)
            vec_ref[...] = jnp.zeros_like(vec_ref)
            g_next[...] = jnp.zeros_like(g_next)
            a_next[...] = jnp.zeros_like(a_next)
            dxc_next[...] = jnp.zeros_like(dxc_next)

        keep = (ti > 0).astype(F32)
        sp, dsp_dlam = _softplus_neg(lam_ref[...])
        cw = cw_ref[...]
        zl_cat = jnp.concatenate([zlh_ref[...] * keep, zl_ref[...]], axis=0)
        xc, r, ig = xc_ref[...].astype(F32), r_ref[...].astype(F32), ig_ref[...].astype(F32)
        a, mult = _decay(r, sp, first_row)
        hs = hs_ref[...]
        gelu, dgelu = _gelu_and_grad(zg_ref[...])
        dy = dy_ref[...]
        dlp_ref[...] += _tn((hs * gelu).astype(BF16), dy)
        dyl = _nt(dy, lp_ref[...])
        dzg_ref[...] = (dyl * hs * dgelu).astype(BF16)

        d_s[...] = dyl * gelu
        c_s[...] = _shift_up(jnp.concatenate([a, a_next[...]], axis=0), 1)[:tt]
        rows8 = _row_ids((SUBLANES, D_MODEL))

        def group(k, carry):
            at = pl.ds(pl.multiple_of((n_groups - 1 - k) * SUBLANES, SUBLANES), SUBLANES)
            C, Dv = c_s[at, :], d_s[at, :]
            for s in (1, 2, 4):
                m = rows8 < SUBLANES - s
                Dv = jnp.where(m, C * pltpu.roll(Dv, SUBLANES - s, 0) + Dv, Dv)
                C = jnp.where(m, C * pltpu.roll(C, SUBLANES - s, 0), C)
            G = C * carry + Dv
            g_s[at, :] = G
            return jnp.broadcast_to(G[0:1, :], (SUBLANES, D_MODEL))

        g_next[...] = lax.fori_loop(0, n_groups, group, g_next[...])
        a_next[...] = jnp.broadcast_to(a[0:1, :], (SUBLANES, D_MODEL))
        G = g_s[...]

        h_prev = _shift_down(jnp.concatenate([hsh_ref[...] * keep, hs], axis=0), 1)[SUBLANES:]
        t_glob = first_row + _row_ids((tt, D_MODEL))
        dmult = jnp.where(t_glob == 0, 0.0, G * ig * xc)
        dla = G * h_prev * a - dmult * (a * a) / mult
        vec_ref[_V_LAM:_V_LAM + 1, :] += _colsum(dla * r) * (-LRU_C) * dsp_dlam
        dpr = dla * (-LRU_C) * sp * r * (1.0 - r)
        dpi = G * mult * xc * ig * (1.0 - ig)
        vec_ref[_V_BRG:_V_BRG + 1, :] += _colsum(dpr)
        vec_ref[_V_BIG:_V_BIG + 1, :] += _colsum(dpi)
        dprb, dpib, xh = dpr.astype(BF16), dpi.astype(BF16), xc_ref[...]
        dxc_h = []
        for p in range(LRU_HEADS // 2):
            cols = slice(p * PAIR_DIM, (p + 1) * PAIR_DIM)
            dwrg_ref[p] += _tn(xh[:, cols], dprb[:, cols])
            dwig_ref[p] += _tn(xh[:, cols], dpib[:, cols])
            dxc_h.append(_nt(dprb[:, cols], wrg_ref[p]) + _nt(dpib[:, cols], wig_ref[p]))
        dxc = G * mult * ig + jnp.concatenate(dxc_h, axis=1)

        vec_ref[_V_CONVB:_V_CONVB + 1, :] += _colsum(dxc)
        dxc_cat = jnp.concatenate([dxc, dxc_next[...]], axis=0)
        dzl = cw[CONV_WIDTH - 1:CONV_WIDTH] * dxc
        for k in range(CONV_WIDTH):
            lag = CONV_WIDTH - 1 - k
            vec_ref[_V_CONVW + k:_V_CONVW + k + 1, :] += _colsum(dxc * _shift_down(zl_cat, lag)[SUBLANES:])
            if lag:
                dzl = dzl + cw[k:k + 1] * _shift_up(dxc_cat, lag)[:tt]
        dzl_ref[...] = dzl.astype(BF16)
        dxc_next[...] = dxc[:SUBLANES]

    res = [conv_w, w_rg, w_ig, lru_lambda, lru_proj]
    return pl.pallas_call(
        body, name="b2_lru", grid=(nt,),
        in_specs=[_rows_rev(tt, D_MODEL, nt), _rows_rev(tt, D_MODEL, nt), _halo_before_rev(SUBLANES, D_MODEL, tt, nt),
                  _rows_rev(tt, D_MODEL, nt), _rows_rev(tt, D_MODEL, nt), _halo_before_rev(SUBLANES, D_MODEL, tt, nt)]
        + [_rows_rev(tt, D_MODEL, nt)] * 3 + [_resident(w.shape) for w in res],
        out_specs=[_rows_rev(tt, D_MODEL, nt), _rows_rev(tt, D_MODEL, nt), _acc(lru_proj.shape), _acc(w_rg.shape),
                   _acc(w_ig.shape), _acc((SUBLANES, D_MODEL))],
        out_shape=[jax.ShapeDtypeStruct((T, D_MODEL), BF16), jax.ShapeDtypeStruct((T, D_MODEL), BF16),
                   jax.ShapeDtypeStruct(lru_proj.shape, F32), jax.ShapeDtypeStruct(w_rg.shape, F32),
                   jax.ShapeDtypeStruct(w_ig.shape, F32), jax.ShapeDtypeStruct((SUBLANES, D_MODEL), F32)],
        scratch_shapes=[pltpu.VMEM((tt, D_MODEL), F32)] * 3 + [pltpu.VMEM((SUBLANES, D_MODEL), F32)] * 3,
        compiler_params=_params(("arbitrary",)),
    )(dyl, zl, zl, zg, hs, hs, xc_saved, r_saved, ig_saved, *res)


def _row_tile(rows):
    for t in (512, 256, 128, 64, 32, 16, 8):
        if rows % t == 0:
            return t
    return rows


def _scalar_grid(grid, in_specs, out_specs):
    return pltpu.PrefetchScalarGridSpec(num_scalar_prefetch=1, grid=grid, in_specs=in_specs, out_specs=out_specs)


def _cast_into_block(w, by_rows, shard_j, name):
    R, C = w.shape
    tr = _row_tile(R)
    if by_rows:
        out_shape, out_map = (N_SHARDS * R, C), lambda i, j: (j[0] * (R // tr) + i, 0)
    else:
        out_shape, out_map = (R, N_SHARDS * C), lambda i, j: (i, j[0])

    def body(j_ref, w_ref, o_ref):
        o_ref[...] = w_ref[...].astype(BF16)

    return pl.pallas_call(
        body, name=name,
        grid_spec=_scalar_grid((R // tr,), [pl.BlockSpec((tr, C), lambda i, j: (i, 0))], pl.BlockSpec((tr, C), out_map)),
        out_shape=jax.ShapeDtypeStruct(out_shape, BF16),
        compiler_params=_params(("arbitrary",)),
    )(shard_j.reshape(1), w)


def _sum_cores(g, theirs, core, name):
    S, R, C = g.shape
    H = R // 2
    tr = _row_tile(H)
    nh = H // tr

    def body(c_ref, g_ref, t_ref, o_ref):
        o_ref[...] = (g_ref[...] + t_ref[...]).astype(BF16)

    half = pl.BlockSpec((None, tr, C), lambda s, i, c: (s, i, 0))
    return pl.pallas_call(
        body, name=name,
        grid_spec=_scalar_grid((S, nh), [pl.BlockSpec((None, tr, C), lambda s, i, c: (s, c[0] * nh + i, 0)), half], half),
        out_shape=jax.ShapeDtypeStruct((S, H, C), BF16),
        compiler_params=_params(("arbitrary", "arbitrary")),
    )(core.reshape(1), g, theirs)


def _sum_chips(sums, slots, by_rows, place, name):
    _, H, C = slots.shape
    tr = _row_tile(H)
    own_map = (lambda i, p: (p[0], i, 0)) if by_rows else (lambda i, p: (0, i, p[0]))

    def body(p_ref, s_ref, q_ref, o_ref):
        o_ref[...] = ((s_ref[...].astype(F32) + q_ref[0].astype(F32)) + q_ref[1].astype(F32)) + q_ref[2].astype(F32)

    return pl.pallas_call(
        body, name=name,
        grid_spec=_scalar_grid(
            (H // tr,),
            [pl.BlockSpec((None, tr, C), own_map), pl.BlockSpec((3, tr, C), lambda i, p: (0, i, 0))],
            pl.BlockSpec((None, tr, C), lambda i, p: (p[1], i, 0))),
        out_shape=jax.ShapeDtypeStruct((2, H, C), F32),
        compiler_params=_params(("arbitrary",)),
    )(place, sums, slots)


def _adamw(w, g, m, v, name):
    R, C = w.shape
    tr = _row_tile(R)
    c1 = 1.0 - ADAM_B1 ** ADAM_STEP
    c2 = 1.0 - ADAM_B2 ** ADAM_STEP

    def body(w_ref, g_ref, m_ref, v_ref, d_ref, nm_ref, nv_ref):
        gv = g_ref[...]
        nm = ADAM_B1 * m_ref[...] + (1.0 - ADAM_B1) * gv
        nv = ADAM_B2 * v_ref[...] + (1.0 - ADAM_B2) * (gv * gv)
        d_ref[...] = -ADAM_LR * ((nm / c1) / (jnp.sqrt(nv / c2) + ADAM_EPS) + ADAM_WD * w_ref[...])
        nm_ref[...] = nm
        nv_ref[...] = nv

    return pl.pallas_call(
        body, name=name, grid=(R // tr,),
        in_specs=[_rows(tr, C)] * 4, out_specs=[_rows(tr, C)] * 3,
        out_shape=[jax.ShapeDtypeStruct((R, C), F32)] * 3,
        compiler_params=_params(("arbitrary",)),
    )(w, g, m, v)


def _place():
    return lax.axis_index("x"), lax.axis_index("y"), lax.axis_index("c")


def _other_chips(x, y):
    return [(1 - x, y), (x, 1 - y), (1 - x, 1 - y)]


def _shard_block(ref, by_rows, R, C, j, half_rows=None):
    if half_rows is None:
        rows, r0 = R, 0
    else:
        rows = R // 2
        r0 = pl.multiple_of(half_rows * rows, 16)
    if by_rows:
        return ref.at[pl.ds(pl.multiple_of(j * R, 16) + r0, rows), :]
    return ref.at[pl.ds(r0, rows), pl.ds(pl.multiple_of(j * C, 128), C)]


def _all_gather_weights(gathered, shapes, by_rows, small):
    n = len(gathered)

    def body(*refs):
        small_in = refs[n]
        outs, small_out = refs[n + 1:2 * n + 1], refs[2 * n + 1]
        send_sems, recv_sems, local_sem = refs[2 * n + 2:]
        x, y, c = _place()
        me_j = 2 * x + y
        chips = _other_chips(x, y)
        sibling = (x, y, 1 - c)

        def block(i, j, half):
            R, C = shapes[i]
            return _shard_block(outs[i], by_rows[i], R, C, j, half)

        def ici(i, k, src_j):
            return pltpu.make_async_remote_copy(
                src_ref=block(i, src_j, c), dst_ref=block(i, src_j, c),
                send_sem=send_sems.at[6 * i + k], recv_sem=recv_sems.at[6 * i + k],
                device_id=(*chips[k], c), device_id_type=MESH)

        def relay(i, k, half):
            kj = 2 * chips[k][0] + chips[k][1]
            return pltpu.make_async_remote_copy(
                src_ref=block(i, kj, half), dst_ref=block(i, kj, half),
                send_sem=send_sems.at[6 * i + 3 + k], recv_sem=recv_sems.at[6 * i + 3 + k],
                device_id=sibling, device_id_type=MESH)

        def small_copy(k, src_j):
            cols = pl.ds(pl.multiple_of(src_j * 256, 128), 256)
            return pltpu.make_async_remote_copy(
                src_ref=small_in, dst_ref=small_out.at[:, cols],
                send_sem=send_sems.at[6 * n + k], recv_sem=recv_sems.at[6 * n + k],
                device_id=(*chips[k], c), device_id_type=MESH)

        sends = []
        for i in range(n):
            for k in range(3):
                cp = ici(i, k, me_j)
                cp.start()
                sends.append(cp)
        for k in range(3):
            cp = small_copy(k, me_j)
            cp.start()
            sends.append(cp)
        local = pltpu.make_async_copy(small_in, small_out.at[:, pl.ds(pl.multiple_of(me_j * 256, 128), 256)], local_sem)
        local.start()
        for i in range(n):
            for k in range(3):
                kj = 2 * chips[k][0] + chips[k][1]
                ici(i, k, kj).wait_recv()
                cp = relay(i, k, c)
                cp.start()
                sends.append(cp)
        for k in range(3):
            small_copy(k, 2 * chips[k][0] + chips[k][1]).wait_recv()
        for i in range(n):
            for k in range(3):
                relay(i, k, 1 - c).wait_recv()
        for cp in sends:
            cp.wait_send()
        local.wait()

    out_shape = [jax.ShapeDtypeStruct(g.shape, BF16) for g in gathered]
    out_shape.append(jax.ShapeDtypeStruct((8, N_SHARDS * 256), F32))
    n_sems = 6 * n + 3
    return pl.pallas_call(
        body, name="all_gather_weights",
        in_specs=[ANY] * (n + 1), out_specs=[ANY] * (n + 1), out_shape=out_shape,
        input_output_aliases={i: i for i in range(n)},
        scratch_shapes=[pltpu.SemaphoreType.DMA((n_sems,)), pltpu.SemaphoreType.DMA((n_sems,)),
                        pltpu.SemaphoreType.DMA],
    )(*gathered, small)


def _core_exchange(grads, name):
    n = len(grads)

    def body(*refs):
        copies = _core_exchange_copies(refs[:n], refs[n:2 * n], refs[2 * n], refs[2 * n + 1])
        for cp in copies:
            cp.start()
        for cp in copies:
            cp.wait()

    return pl.pallas_call(
        body, name=name,
        in_specs=[ANY] * n, out_specs=[ANY] * n,
        out_shape=[jax.ShapeDtypeStruct((g.shape[0], g.shape[1] // 2, g.shape[2]), F32) for g in grads],
        scratch_shapes=[pltpu.SemaphoreType.DMA((n,))] * 2,
    )(*grads)


HBM = pl.BlockSpec(memory_space=pltpu.HBM)
SEM = pl.BlockSpec(memory_space=pltpu.SEMAPHORE)
TOKEN = jax.ShapeDtypeStruct((SUBLANES, 128), F32)


def _in_hbm(a):
    return pltpu.with_memory_space_constraint(a, pltpu.HBM)


def _split_params():
    return pltpu.CompilerParams(has_side_effects=pltpu.SideEffectType.DATAFLOW_SIDE_EFFECTING)


def _gather_rest_copies(refs, shapes, by_rows, send_sems, recv_sems):
    x, y, c = _place()
    me_j = 2 * x + y
    chips = _other_chips(x, y)
    pairs = []
    for i, ref in enumerate(refs):
        R, C = shapes[i]
        for k in range(3):
            kj = 2 * chips[k][0] + chips[k][1]

            def copy(j, ref=ref, i=i, k=k, R=R, C=C):
                blk = _shard_block(ref, by_rows[i], R, C, j)
                return pltpu.make_async_remote_copy(
                    src_ref=blk, dst_ref=blk, send_sem=send_sems.at[3 * i + k], recv_sem=recv_sems.at[3 * i + k],
                    device_id=(*chips[k], c), device_id_type=MESH)

            pairs.append((copy(me_j), copy(kj)))
    return pairs


def _gather_rest_start(gathered, shapes, by_rows, after):
    n = len(gathered)

    def body(*refs):
        ins = refs[:n]
        send_sems, recv_sems = refs[n + 1], refs[n + 2]
        token = refs[-1]
        for mine, _ in _gather_rest_copies(ins, shapes, by_rows, send_sems, recv_sems):
            mine.start()
        token[...] = jnp.zeros_like(token)

    out = pl.pallas_call(
        body, name="gather_rest_start",
        out_shape=(pltpu.SemaphoreType.DMA((3 * n,)), pltpu.SemaphoreType.DMA((3 * n,)),
                   *[pltpu.HBM(g.shape, g.dtype) for g in gathered], TOKEN),
        in_specs=[HBM] * n + [ANY], out_specs=(SEM, SEM, *[HBM] * n, pl.BlockSpec(memory_space=pltpu.VMEM)),
        input_output_aliases={i: 2 + i for i in range(n)},
        compiler_params=_split_params(),
    )(*[_in_hbm(g) for g in gathered], after)
    return out[0], out[1], out[2:2 + n], out[-1]


def _gather_rest_wait(send_sems, recv_sems, gathered, shapes, by_rows, after):
    n = len(gathered)

    def body(*refs):
        ins = refs[:n]
        send, recv = refs[n], refs[n + 1]
        for mine, theirs in _gather_rest_copies(ins, shapes, by_rows, send, recv):
            mine.wait_send()
            theirs.wait_recv()

    return pl.pallas_call(
        body, name="gather_rest_wait",
        out_shape=tuple(pltpu.HBM(g.shape, g.dtype) for g in gathered),
        in_specs=[HBM] * n + [SEM, SEM, ANY], out_specs=tuple([HBM] * n),
        input_output_aliases={i: i for i in range(n)},
        compiler_params=_split_params(),
    )(*gathered, send_sems, recv_sems, after)


def _chip_exchange_copies(ins, slots, dims, by_rows, send_sems, recv_sems):
    x, y, c = _place()
    chips = _other_chips(x, y)
    pairs = []
    for i in range(len(ins)):
        for k in range(3):
            kj = 2 * chips[k][0] + chips[k][1]
            if by_rows[i]:
                src = ins[i].at[kj]
            else:
                src = ins[i].at[0, :, pl.ds(pl.multiple_of(kj * dims[i][1], 128), dims[i][1])]
            cp = pltpu.make_async_remote_copy(
                src_ref=src, dst_ref=slots[i].at[k], send_sem=send_sems.at[3 * i + k], recv_sem=recv_sems.at[3 * i + k],
                device_id=(*chips[k], c), device_id_type=MESH)
            pairs.append((cp, cp))
    return pairs


def _exchange_dims(sums, by_rows):
    return [(s.shape[1], s.shape[2]) if by_rows[i] else (s.shape[1], s.shape[2] // N_SHARDS) for i, s in enumerate(sums)]


def _chip_exchange_start(sums, by_rows, tag):
    n = len(sums)
    sums = list(sums)
    dims = _exchange_dims(sums, by_rows)
    slots = [lax.empty((3, h, cc), BF16) for h, cc in dims]

    def body(*refs):
        ins, land = refs[:n], refs[n:2 * n]
        send_sems, recv_sems = refs[2 * n], refs[2 * n + 1]
        token = refs[-1]
        for cp, _ in _chip_exchange_copies(ins, land, dims, by_rows, send_sems, recv_sems):
            cp.start()
        token[...] = jnp.zeros_like(token)

    out = pl.pallas_call(
        body, name="grad_chip_exchange_start_" + tag,
        out_shape=(pltpu.SemaphoreType.DMA((3 * n,)), pltpu.SemaphoreType.DMA((3 * n,)),
                   *[pltpu.HBM(a.shape, a.dtype) for a in sums + slots], TOKEN),
        in_specs=[HBM] * (2 * n), out_specs=(SEM, SEM, *[HBM] * (2 * n), pl.BlockSpec(memory_space=pltpu.VMEM)),
        input_output_aliases={i: 2 + i for i in range(2 * n)},
        compiler_params=_split_params(),
    )(*[_in_hbm(a) for a in sums + slots])
    return out[0], out[1], out[2:2 + n], out[2 + n:2 + 2 * n], out[-1]


def _chip_exchange_wait(send_sems, recv_sems, sums, slots, by_rows, after, tag):
    n = len(sums)
    sums, slots = list(sums), list(slots)
    dims = _exchange_dims(sums, by_rows)

    def body(*refs):
        ins, land = refs[:n], refs[n:2 * n]
        send, recv = refs[2 * n], refs[2 * n + 1]
        for cp, _ in _chip_exchange_copies(ins, land, dims, by_rows, send, recv):
            cp.wait_send()
            cp.wait_recv()

    out = pl.pallas_call(
        body, name="grad_chip_exchange_wait_" + tag,
        out_shape=tuple(pltpu.HBM(a.shape, a.dtype) for a in sums + slots),
        in_specs=[HBM] * (2 * n) + [SEM, SEM, ANY], out_specs=tuple([HBM] * (2 * n)),
        input_output_aliases={i: i for i in range(2 * n)},
        compiler_params=_split_params(),
    )(*sums, *slots, send_sems, recv_sems, after)
    return out[:n], out[n:]


def _core_exchange_copies(ins, theirs, send_sems, recv_sems):
    x, y, c = _place()
    copies = []
    for i in range(len(ins)):
        H = ins[i].shape[1] // 2
        copies.append(pltpu.make_async_remote_copy(
            src_ref=ins[i].at[:, pl.ds(pl.multiple_of((1 - c) * H, 8), H), :], dst_ref=theirs[i],
            send_sem=send_sems.at[i], recv_sem=recv_sems.at[i], device_id=(x, y, 1 - c), device_id_type=MESH))
    return copies


def _core_exchange_start(grads):
    n = len(grads)
    grads = list(grads)
    theirs = [lax.empty((g.shape[0], g.shape[1] // 2, g.shape[2]), F32) for g in grads]

    def body(*refs):
        for cp in _core_exchange_copies(refs[:n], refs[n:2 * n], refs[2 * n], refs[2 * n + 1]):
            cp.start()
        refs[-1][...] = jnp.zeros_like(refs[-1])

    out = pl.pallas_call(
        body, name="grad_core_exchange_start",
        out_shape=(pltpu.SemaphoreType.DMA((n,)), pltpu.SemaphoreType.DMA((n,)),
                   *[pltpu.HBM(a.shape, a.dtype) for a in grads + theirs], TOKEN),
        in_specs=[HBM] * (2 * n), out_specs=(SEM, SEM, *[HBM] * (2 * n), pl.BlockSpec(memory_space=pltpu.VMEM)),
        input_output_aliases={i: 2 + i for i in range(2 * n)},
        compiler_params=_split_params(),
    )(*[_in_hbm(a) for a in grads + theirs])
    return out[0], out[1], out[2:2 + n], out[2 + n:2 + 2 * n], out[-1]


def _core_exchange_wait(send_sems, recv_sems, grads, theirs, after):
    n = len(grads)
    grads, theirs = list(grads), list(theirs)

    def body(*refs):
        for cp in _core_exchange_copies(refs[:n], refs[n:2 * n], refs[2 * n], refs[2 * n + 1]):
            cp.wait_send()
            cp.wait_recv()

    out = pl.pallas_call(
        body, name="grad_core_exchange_wait",
        out_shape=tuple(pltpu.HBM(a.shape, a.dtype) for a in grads + theirs),
        in_specs=[HBM] * (2 * n) + [SEM, SEM, ANY], out_specs=tuple([HBM] * (2 * n)),
        input_output_aliases={i: i for i in range(2 * n)},
        compiler_params=_split_params(),
    )(*grads, *theirs, send_sems, recv_sems, after)
    return out[:n], out[n:]


def _core_share(reduced, tag):
    n = len(reduced)

    def body(*refs):
        outs = refs[n:2 * n]
        send_sems, recv_sems = refs[2 * n:]
        x, y, c = _place()
        copies = []
        for i in range(n):
            cp = pltpu.make_async_remote_copy(
                src_ref=outs[i].at[c], dst_ref=outs[i].at[c], send_sem=send_sems.at[i], recv_sem=recv_sems.at[i],
                device_id=(x, y, 1 - c), device_id_type=MESH)
            cp.start()
            copies.append(cp)
        for cp in copies:
            cp.wait()

    return pl.pallas_call(
        body, name="grad_core_share_" + tag,
        in_specs=[ANY] * n, out_specs=[ANY] * n,
        out_shape=[jax.ShapeDtypeStruct(r.shape, F32) for r in reduced],
        input_output_aliases={i: i for i in range(n)},
        scratch_shapes=[pltpu.SemaphoreType.DMA((n,))] * 2,
    )(*reduced)


def _small_exchange_copies(pack_ref, slots_ref, send_sems, recv_sems):
    x, y, c = _place()
    peers = [(px, py, pc) for px in (x, 1 - x) for py in (y, 1 - y) for pc in (c, 1 - c)][1:]
    pairs = []
    for k, peer in enumerate(peers):
        def copy(sender, k=k, peer=peer):
            return pltpu.make_async_remote_copy(
                src_ref=pack_ref, dst_ref=slots_ref.at[4 * sender[0] + 2 * sender[1] + sender[2]],
                send_sem=send_sems.at[k], recv_sem=recv_sems.at[k], device_id=peer, device_id_type=MESH)

        pairs.append((copy((x, y, c)), copy(peer)))
    return pairs


def _small_exchange_start(pack):
    slots = lax.empty((N_DEV,) + pack.shape, F32)

    def body(pack_ref, slots_ref, send_sems, recv_sems, pack_thru, slots_thru, token):
        for mine, _ in _small_exchange_copies(pack_ref, slots_ref, send_sems, recv_sems):
            mine.start()
        token[...] = jnp.zeros_like(token)

    return pl.pallas_call(
        body, name="grad_small_exchange_start",
        out_shape=(pltpu.SemaphoreType.DMA((N_DEV - 1,)), pltpu.SemaphoreType.DMA((N_DEV - 1,)),
                   pltpu.HBM(pack.shape, F32), pltpu.HBM(slots.shape, F32), TOKEN),
        in_specs=[HBM, HBM], out_specs=(SEM, SEM, HBM, HBM, pl.BlockSpec(memory_space=pltpu.VMEM)),
        input_output_aliases={0: 2, 1: 3},
        compiler_params=_split_params(),
    )(_in_hbm(pack), _in_hbm(slots))


def _small_exchange_wait(send_sems, recv_sems, pack, slots, after):
    def body(pack_ref, slots_ref, send, recv, after_ref, pack_thru, slots_thru):
        for mine, theirs in _small_exchange_copies(pack_ref, slots_ref, send, recv):
            mine.wait_send()
            theirs.wait_recv()

    return pl.pallas_call(
        body, name="grad_small_exchange_wait",
        out_shape=(pltpu.HBM(pack.shape, F32), pltpu.HBM(slots.shape, F32)),
        in_specs=[HBM, HBM, SEM, SEM, ANY], out_specs=(HBM, HBM),
        input_output_aliases={0: 0, 1: 1},
        compiler_params=_split_params(),
    )(pack, slots, send_sems, recv_sems, after)


def _sum_small(pack, slots, me):
    R, C = pack.shape
    tr = _row_tile(R)

    def body(me_ref, p_ref, q_ref, o_ref):
        acc = jnp.where(me_ref[0] == 0, p_ref[...], q_ref[0])
        for d in range(1, N_DEV):
            acc = acc + jnp.where(me_ref[0] == d, p_ref[...], q_ref[d])
        o_ref[...] = acc

    return pl.pallas_call(
        body, name="sum_small",
        grid_spec=_scalar_grid((R // tr,), [pl.BlockSpec((tr, C), lambda i, m: (i, 0)),
                                            pl.BlockSpec((N_DEV, tr, C), lambda i, m: (0, i, 0))],
                               pl.BlockSpec((tr, C), lambda i, m: (i, 0))),
        out_shape=jax.ShapeDtypeStruct((R, C), F32),
        compiler_params=_params(("arbitrary",)),
    )(me.reshape(1), pack, slots)


def _pack_rows(parts, rows):
    flat = jnp.concatenate([a.reshape(-1) for a in parts])
    return jnp.pad(flat, (0, rows * 128 - flat.shape[0])).reshape(rows, 128)


def _unpack_rows(pack, shapes):
    flat = pack.reshape(-1)
    out, at = [], 0
    for s in shapes:
        size = 1
        for d in s:
            size *= d
        out.append(flat[at:at + size].reshape(s))
        at += size
    return out


def kernel(x, p, norm1_g, w_in, b_gate, pool_w, pool_scale, pool_proj, conv_w, conv_b, w_rg, b_rg, w_ig, b_ig, lru_lambda, lru_proj, w_out, norm2_g, w_ffn_in, w_ffn_out, ple_norm_g, w_ple_gate, w_ple_proj, final_g, loss_target, m_norm1_g, m_w_in, m_b_gate, m_pool_w, m_pool_scale, m_pool_proj, m_conv_w, m_conv_b, m_w_rg, m_b_rg, m_w_ig, m_b_ig, m_lru_lambda, m_lru_proj, m_w_out, m_norm2_g, m_w_ffn_in, m_w_ffn_out, m_ple_norm_g, m_w_ple_gate, m_w_ple_proj, m_final_g, v_norm1_g, v_w_in, v_b_gate, v_pool_w, v_pool_scale, v_pool_proj, v_conv_w, v_conv_b, v_w_rg, v_b_rg, v_w_ig, v_b_ig, v_lru_lambda, v_lru_proj, v_w_out, v_norm2_g, v_w_ffn_in, v_w_ffn_out, v_ple_norm_g, v_w_ple_gate, v_w_ple_proj, v_final_g):
    weights = dict(norm1_g=norm1_g, w_in=w_in, b_gate=b_gate, pool_w=pool_w, pool_scale=pool_scale,
                   pool_proj=pool_proj, conv_w=conv_w, conv_b=conv_b, w_rg=w_rg, b_rg=b_rg, w_ig=w_ig, b_ig=b_ig,
                   lru_lambda=lru_lambda, lru_proj=lru_proj, w_out=w_out, norm2_g=norm2_g, w_ffn_in=w_ffn_in,
                   w_ffn_out=w_ffn_out, ple_norm_g=ple_norm_g, w_ple_gate=w_ple_gate, w_ple_proj=w_ple_proj,
                   final_g=final_g)
    m_in = dict(norm1_g=m_norm1_g, w_in=m_w_in, b_gate=m_b_gate, pool_w=m_pool_w, pool_scale=m_pool_scale,
                pool_proj=m_pool_proj, conv_w=m_conv_w, conv_b=m_conv_b, w_rg=m_w_rg, b_rg=m_b_rg, w_ig=m_w_ig,
                b_ig=m_b_ig, lru_lambda=m_lru_lambda, lru_proj=m_lru_proj, w_out=m_w_out, norm2_g=m_norm2_g,
                w_ffn_in=m_w_ffn_in, w_ffn_out=m_w_ffn_out, ple_norm_g=m_ple_norm_g, w_ple_gate=m_w_ple_gate,
                w_ple_proj=m_w_ple_proj, final_g=m_final_g)
    v_in = dict(norm1_g=v_norm1_g, w_in=v_w_in, b_gate=v_b_gate, pool_w=v_pool_w, pool_scale=v_pool_scale,
                pool_proj=v_pool_proj, conv_w=v_conv_w, conv_b=v_conv_b, w_rg=v_w_rg, b_rg=v_b_rg, w_ig=v_w_ig,
                b_ig=v_b_ig, lru_lambda=v_lru_lambda, lru_proj=v_lru_proj, w_out=v_w_out, norm2_g=v_norm2_g,
                w_ffn_in=v_w_ffn_in, w_ffn_out=v_w_ffn_out, ple_norm_g=v_ple_norm_g, w_ple_gate=v_w_ple_gate,
                w_ple_proj=v_w_ple_proj, final_g=v_final_g)
    names = list(weights)
    big = ["w_in", "pool_proj", "lru_proj", "w_out", "w_ffn_in", "w_ffn_out", "w_ple_gate", "w_ple_proj"]
    by_rows = [n in ("lru_proj", "w_out", "w_ffn_out", "w_ple_gate") for n in big]
    small = [n for n in names if n not in big]

    shard_j = 2 * lax.axis_index("x") + lax.axis_index("y")
    T = x.shape[1]
    xs, ps, tgt = x[0], p[0, 0], loss_target[0]

    small_local = jnp.concatenate([b_gate[0], conv_w[0], jnp.zeros((2, 256), F32)], axis=0)
    core = lax.axis_index("c").astype(jnp.int32)
    place = jnp.stack([shard_j, core]).astype(jnp.int32)
    rows_of = dict(zip(big, by_rows))
    shard_shape = {n: weights[n].shape[1:] for n in big}
    blocks = {n: _cast_into_block(weights[n][0], rows_of[n], place[0], "cast_" + n) for n in big}
    early, late = big[:4], big[4:]
    gathered = _all_gather_weights([blocks[n] for n in early], [shard_shape[n] for n in early],
                                   [rows_of[n] for n in early], small_local)
    full = dict(zip(early, gathered[:-1]))
    late_send, late_recv, late_bufs, late_token = _gather_rest_start(
        [blocks[n] for n in late], [shard_shape[n] for n in late], [rows_of[n] for n in late], gathered[-1])
    b_gate_full = gathered[-1][0:2].reshape(1, 2 * D_MODEL)
    conv_w_full = gathered[-1][2:6]
    pool_w_b, w_rg_b, w_ig_b = [_pair_blocks(w[0].astype(BF16)) for w in (pool_w, w_rg, w_ig)]
    b_rg_row, b_ig_row = b_rg.reshape(1, D_MODEL), b_ig.reshape(1, D_MODEL)
    final_row = final_g.reshape(1, D_MODEL)

    zp, zl, zg, zt, u, h1, hs, yp, yl, xc_saved, r_saved, ig_saved = _f12_mixer(
        xs, norm1_g + late_token[0, 0], full["w_in"], b_gate_full, pool_w_b, pool_scale, full["pool_proj"],
        conv_w_full, conv_b, w_rg_b, b_rg_row, w_ig_b, b_ig_row, lru_lambda, full["lru_proj"], full["w_out"])
    full.update(zip(late, _gather_rest_wait(late_send, late_recv, late_bufs, [shard_shape[n] for n in late],
                                            [rows_of[n] for n in late], h1)))
    h2, v, ff, act = _f3_ffn(h1, norm2_g, full["w_ffn_in"], full["w_ffn_out"])

    loss_sum, dh2, g_ple_gate, g_ple_proj, vec4 = _b4_ple_loss(
        h2, ps, tgt, ple_norm_g, full["w_ple_gate"], full["w_ple_proj"], final_row)
    dff, dh1, vec3 = _b3_ffn(dh2, h1, ff, norm2_g, full["w_ffn_in"], full["w_ffn_out"])
    g_ffn_in = _wgrad(v, dff, 2 * D_FF // N_SHARDS, "wgrad_ffn_in")
    g_ffn_out = _wgrad(act, dh2, D_MODEL, "wgrad_ffn_out", tokens=WGRAD_TOKENS // 2)

    def stack(n, g):
        return g.reshape(N_SHARDS, g.shape[0] // N_SHARDS, g.shape[1]) if rows_of[n] else g[None]

    def chip_sums_of(group, grads_of, tag):
        stacked = [stack(n, grads_of[n]) for n in group]
        theirs = _core_exchange(stacked, "grad_core_exchange_" + tag)
        return [_sum_cores(g, t, core, "sum_cores_" + n) for g, t, n in zip(stacked, theirs, group)]

    late_rows = [rows_of[n] for n in late]
    late_grads = dict(w_ffn_in=g_ffn_in, w_ffn_out=g_ffn_out, w_ple_gate=g_ple_gate, w_ple_proj=g_ple_proj)
    cx_send, cx_recv, late_stacked, late_theirs, cx_token = _core_exchange_start(
        [stack(n, late_grads[n]) for n in late])
    dzt, dyp, dyl, g_w_out, vec_g = _b2_gates(dh1, zt, yp, yl, b_gate_full + cx_token[0, 0], full["w_out"])
    late_stacked, late_theirs = _core_exchange_wait(cx_send, cx_recv, late_stacked, late_theirs, dzt)
    late_sums = [_sum_cores(g, t, core, "sum_cores_" + n) for g, t, n in zip(late_stacked, late_theirs, late)]
    ex_send, ex_recv, late_sums, late_slots, ex_token = _chip_exchange_start(late_sums, late_rows, "late")
    dzl, dzg, g_lru_proj, g_w_rg, g_w_ig, vec_l = _b2_lru(
        dyl, zl, zg, hs, xc_saved, r_saved, ig_saved, conv_w_full, w_rg_b, w_ig_b, lru_lambda + ex_token[0, 0],
        full["lru_proj"])
    dzp, grad_x, g_pool_proj, g_pool_w, vec_p = _b12_pool_in_proj(
        dyp, zp, dzl, dzg, dzt, xs, dh1, norm1_g, full["w_in"], pool_w_b, pool_scale, full["pool_proj"])
    small_full = dict(
        norm1_g=vec_p[1], b_gate=vec_g[0:2], pool_w=_unpair_blocks(g_pool_w), pool_scale=vec_p[0, :POOL_WIDTH],
        conv_w=vec_l[_V_CONVW:_V_CONVW + CONV_WIDTH], conv_b=vec_l[_V_CONVB], w_rg=_unpair_blocks(g_w_rg),
        b_rg=vec_l[_V_BRG], w_ig=_unpair_blocks(g_w_ig), b_ig=vec_l[_V_BIG], lru_lambda=vec_l[_V_LAM], norm2_g=vec3[0], ple_norm_g=vec4[1],
        final_g=vec4[0])
    full_shapes = [small_full[n].shape for n in small]
    n_full = sum(int(small_full[n].size) for n in small)
    rows_full = -(-n_full // (128 * ROW_TILE)) * ROW_TILE
    sm_send, sm_recv, sm_pack, sm_slots, sm_token = _small_exchange_start(
        _pack_rows([small_full[n] for n in small], rows_full))
    g_w_in = jnp.concatenate([
        _wgrad(u, dzp, POOL_WIDTH, "wgrad_in_pool", after=sm_token),
        _wgrad(u, dzl, D_MODEL, "wgrad_in_lru", after=sm_token),
        _wgrad(u, dzg, D_MODEL, "wgrad_in_gelu", after=sm_token),
        _wgrad(u, dzt, D_MODEL, "wgrad_in_gate", after=sm_token)], axis=1)

    loss = lax.psum(loss_sum[0, 0] * (0.5 / D_MODEL), ("x", "y", "c"))

    early_rows = [rows_of[n] for n in early]
    early_sums = chip_sums_of(early, dict(w_in=g_w_in, pool_proj=g_pool_proj, lru_proj=g_lru_proj, w_out=g_w_out),
                              "early")
    e_send, e_recv, early_sums, early_slots, e_token = _chip_exchange_start(early_sums, early_rows, "early")
    grads, deltas, new_m, new_v = {}, {}, {}, {}

    def finish(group, sums, slots, tag):
        reduced = _core_share([_sum_chips(s, q, rows_of[n], place, "sum_chips_" + n)
                               for s, q, n in zip(sums, slots, group)], tag)
        for n, r in zip(group, reduced):
            g = r.reshape(r.shape[0] * r.shape[1], r.shape[2])
            d, nm, nv = _adamw(weights[n][0], g, m_in[n][0], v_in[n][0], "adamw_" + n)
            grads[n], deltas[n], new_m[n], new_v[n] = g[None], d[None], nm[None], nv[None]

    late_sums, late_slots = _chip_exchange_wait(ex_send, ex_recv, late_sums, late_slots, late_rows, e_token, "late")
    finish(late, late_sums, late_slots, "late")
    early_sums, early_slots = _chip_exchange_wait(e_send, e_recv, early_sums, early_slots, early_rows,
                                                  deltas[late[-1]], "early")
    finish(early, early_sums, early_slots, "early")

    sm_pack, sm_slots = _small_exchange_wait(sm_send, sm_recv, sm_pack, sm_slots, g_w_in)
    device = (4 * lax.axis_index("x") + 2 * lax.axis_index("y") + lax.axis_index("c")).astype(jnp.int32)
    summed = dict(zip(small, _unpack_rows(_sum_small(sm_pack, sm_slots, device), full_shapes)))
    summed["b_gate"] = lax.dynamic_slice_in_dim(summed["b_gate"], shard_j * 256, 256, axis=1)
    summed["conv_w"] = lax.dynamic_slice_in_dim(summed["conv_w"], shard_j * 256, 256, axis=1)
    local_shapes = [weights[n].shape for n in small]
    n_local = sum(int(weights[n].size) for n in small)
    rows_local = -(-n_local // (128 * ROW_TILE)) * ROW_TILE
    packs = [_pack_rows([src[n] for n in small], rows_local) for src in (weights, summed, m_in, v_in)]
    d_s, nm_s, nv_s = _adamw(*packs, "adamw_small")
    for dst, pack in ((grads, packs[1]), (deltas, d_s), (new_m, nm_s), (new_v, nv_s)):
        dst.update(zip(small, _unpack_rows(pack, local_shapes)))

    return (loss, grad_x[None], *[grads[n] for n in names], *[deltas[n] for n in names],
            *[new_m[n] for n in names], *[new_v[n] for n in names])
```

```python
import functools

import jax
import jax.numpy as jnp
from jax import lax
from jax.experimental import pallas as pl
from jax.experimental.pallas import tpu as pltpu

F32 = jnp.float32
BF16 = jnp.bfloat16

D_MODEL = 1024
POOL_WIDTH = 512
POOL_GROUP_DIM = 128
POOL_WINDOWS = (2, 4, 8, 16)
POOL_HALO = 16
LRU_HEADS = 8
LRU_HEAD_DIM = 128
CONV_WIDTH = 4
LRU_C = 8.0
D_FF = 2816
PLE_DIM = 256
RMS_EPS = 1e-6
N_SHARDS = 4
N_DEV = 8

ADAM_LR = 0.001
ADAM_B1 = 0.9
ADAM_B2 = 0.999
ADAM_EPS = 1e-08
ADAM_WD = 0.01
ADAM_STEP = 10

ROW_TILE = 256
WIDE_TILE = 512
WGRAD_TOKENS = 2048
SUBLANES = 8
VMEM_LIMIT = 56 * 1024 * 1024
MESH = pl.DeviceIdType.MESH
ANY = pl.BlockSpec(memory_space=pl.ANY)


def _params(semantics=None):
    return pltpu.CompilerParams(dimension_semantics=semantics, vmem_limit_bytes=VMEM_LIMIT)


def _resident(shape):
    n = len(shape)
    return pl.BlockSpec(shape, lambda *_: (0,) * n, pipeline_mode=pl.Buffered(1))


def _acc(shape):
    n = len(shape)
    return pl.BlockSpec(shape, lambda *_: (0,) * n)


def _rows(tile, cols):
    return pl.BlockSpec((tile, cols), lambda i: (i, 0))


def _rows_rev(tile, cols, n_tiles):
    return pl.BlockSpec((tile, cols), lambda i: (n_tiles - 1 - i, 0))


def _halo_before_rev(rows, cols, tile, n_tiles):
    per = tile // rows
    return pl.BlockSpec((rows, cols), lambda i: (jnp.maximum((n_tiles - 1 - i) * per - 1, 0), 0))


def _nn(a, b):
    return jnp.dot(a, b, preferred_element_type=F32)


def _nt(a, b):
    return lax.dot_general(a, b, (((1,), (1,)), ((), ())), preferred_element_type=F32)


def _tn(a, b):
    return lax.dot_general(a, b, (((0,), (0,)), ((), ())), preferred_element_type=F32)


def _rms(x):
    r = lax.rsqrt(jnp.mean(x * x, axis=-1, keepdims=True) + RMS_EPS)
    return x * r, r


def _rms_bwd(dn, n, r):
    return r * (dn - n * jnp.mean(dn * n, axis=-1, keepdims=True))


def _sigmoid(x):
    return 0.5 * jnp.tanh(0.5 * x) + 0.5


_GELU_C = 0.7978845608028654
_GELU_A = 0.044715


def _gelu(x):
    t = jnp.tanh(_GELU_C * (x + _GELU_A * x * x * x))
    return 0.5 * x * (1.0 + t)


def _gelu_and_grad(x):
    x2 = x * x
    t = jnp.tanh(_GELU_C * (x + _GELU_A * x2 * x))
    cdf = 0.5 * (1.0 + t)
    grad = cdf + 0.5 * x * (1.0 - t * t) * _GELU_C * (1.0 + 3.0 * _GELU_A * x2)
    return x * cdf, grad


def _softplus_neg(lam):
    e = jnp.exp(-jnp.abs(lam))
    sp = jnp.maximum(-lam, 0.0) + jnp.log1p(e)
    return sp, -_sigmoid(-lam)


def _colsum(v):
    return jnp.sum(v, axis=0, keepdims=True)


def _row_ids(shape):
    return lax.broadcasted_iota(jnp.int32, shape, 0)


def _shift_down(cat, k):
    return pltpu.roll(cat, k, 0) if k else cat


def _shift_up(cat, k):
    return pltpu.roll(cat, cat.shape[0] - k, 0) if k else cat


IN_SPLITS = (0, POOL_WIDTH, POOL_WIDTH + D_MODEL, POOL_WIDTH + 2 * D_MODEL, POOL_WIDTH + 4 * D_MODEL)
IN_WIDTHS = tuple(IN_SPLITS[k + 1] - IN_SPLITS[k] for k in range(4))
PROJ_CHUNK = 256
PAIR_DIM = 2 * LRU_HEAD_DIM


def _pair_blocks(w):
    zero = jnp.zeros_like(w[0::2])
    return jnp.concatenate([jnp.concatenate([w[0::2], zero], axis=2), jnp.concatenate([zero, w[1::2]], axis=2)], axis=1)


def _unpair_blocks(w):
    n, d2, _ = w.shape
    d = d2 // 2
    return jnp.stack([w[:, :d, :d], w[:, d:, d:]], axis=1).reshape(2 * n, d, d)


def _no_tick():
    pass


class _Interleaved:
    def __init__(self, pieces):
        self._pieces = iter(pieces)

    def tick(self, n=1):
        for _ in range(n):
            piece = next(self._pieces, None)
            if piece is not None:
                piece()

    def flush(self):
        for piece in self._pieces:
            piece()


def _pool_forward(zp_cat, pw_ref, first_row, tick=_no_tick):
    tt = zp_cat.shape[0] - POOL_HALO
    t_glob = first_row + _row_ids((tt, POOL_GROUP_DIM))
    pooled, mixed = [], []
    for g, w in enumerate(POOL_WINDOWS):
        cat = zp_cat[:, g * POOL_GROUP_DIM:(g + 1) * POOL_GROUP_DIM]
        s, k = cat, 1
        while k < w:
            s = s + _shift_down(s, k)
            k *= 2
        cnt = jnp.minimum(t_glob + 1, w).astype(F32)
        pooled.append(s[POOL_HALO:] / cnt - cat[POOL_HALO:])
        if g % 2:
            pair = jnp.concatenate(pooled[-2:], axis=1).astype(BF16)
            mixed.append(_nn(pair, pw_ref[g // 2]))
        tick()
    return jnp.concatenate(pooled, axis=1), jnp.concatenate(mixed, axis=1)


def _lru_gates(zl_cat, conv_w, conv_b, wrg_ref, brg, wig_ref, big, sp, first_row, tick=_no_tick):
    tt = zl_cat.shape[0] - SUBLANES
    xc = conv_w[CONV_WIDTH - 1:CONV_WIDTH] * zl_cat
    for k in range(1, CONV_WIDTH):
        xc = xc + conv_w[CONV_WIDTH - 1 - k:CONV_WIDTH - k] * _shift_down(zl_cat, k)
        tick()
    xc = xc[SUBLANES:] + conv_b
    xh = xc.astype(BF16)
    pr, pi = [], []
    for p in range(LRU_HEADS // 2):
        xs = xh[:, p * PAIR_DIM:(p + 1) * PAIR_DIM]
        pr.append(_nn(xs, wrg_ref[p]))
        pi.append(_nn(xs, wig_ref[p]))
    r = _sigmoid(jnp.concatenate(pr, axis=1) + brg)
    tick()
    ig = _sigmoid(jnp.concatenate(pi, axis=1) + big)
    tick()
    a, mult = _decay(r, sp, first_row, tick)
    tick()
    return xc, r, ig, a, mult


def _decay(r, sp, first_row, tick=_no_tick):
    a = jnp.exp(-LRU_C * r * sp)
    tick()
    mult = jnp.sqrt(jnp.maximum(1.0 - a * a, 0.0))
    t_glob = first_row + _row_ids(r.shape)
    return a, jnp.where(t_glob == 0, 1.0, mult)


def _f12_mixer(x, norm1_g, w_in, b_gate, pool_w, pool_scale, pool_proj, conv_w, conv_b, w_rg, b_rg, w_ig, b_ig,
               lru_lambda, lru_proj, w_out):
    T = x.shape[0]
    tt = ROW_TILE
    nt = T // tt
    n_groups = tt // SUBLANES
    proj_mid = IN_SPLITS[3] + D_MODEL // 2

    def body(xm_ref, x_ref, g1_ref, win_ref, bg_ref, pw_ref, ps_ref, pp_ref, cw_ref, cb_ref,
             wrg_ref, brg_ref, wig_ref, big_ref, lam_ref, lp_ref, wo_ref,
             zp_ref, zl_ref, zg_ref, zt_ref, u_ref, h1_ref, hs_ref, yp_ref, yl_ref, xc_ref, r_ref, ig_ref,
             zbuf, zp_halo, zl_halo, a_s, b_s, carry_s):
        s = pl.program_id(0)

        @pl.when(s == 0)
        def _():
            zbuf[1] = jnp.zeros((tt, IN_SPLITS[4]), F32)
            zp_halo[...] = jnp.zeros_like(zp_halo)
            zl_halo[...] = jnp.zeros_like(zl_halo)
            carry_s[...] = jnp.zeros_like(carry_s)

        z_new, z_old = zbuf.at[s % 2], zbuf.at[(s + 1) % 2]
        first = s <= 1
        first_row = jnp.maximum(s - 1, 0) * tt

        n1, _ = _rms(xm_ref[...])
        u = (n1 * g1_ref[...]).astype(BF16)
        u_ref[...] = u

        z_refs = (zp_ref, zl_ref, zg_ref, zt_ref)

        def project(lo):
            k = max(i for i in range(4) if IN_SPLITS[i] <= lo)
            part = _nn(u, win_ref[:, lo:lo + PROJ_CHUNK])
            z_new[:, lo:lo + PROJ_CHUNK] = part
            z_refs[k][:, lo - IN_SPLITS[k]:lo - IN_SPLITS[k] + PROJ_CHUNK] = part.astype(z_refs[k].dtype)

        before_scan = _Interleaved(functools.partial(project, lo) for lo in range(0, proj_mid, PROJ_CHUNK))
        after_scan = _Interleaved(functools.partial(project, lo) for lo in range(proj_mid, IN_SPLITS[4], PROJ_CHUNK))

        zp_cat = jnp.concatenate([jnp.where(first, 0.0, zp_halo[...]), z_old[:, IN_SPLITS[0]:IN_SPLITS[1]]], axis=0)
        _, mixed = _pool_forward(zp_cat, pw_ref, first_row, before_scan.tick)
        y_pool = _nn((mixed * ps_ref[...]).astype(BF16), pp_ref[...])

        sp, _ = _softplus_neg(lam_ref[...])
        zl_cat = jnp.concatenate([jnp.where(first, 0.0, zl_halo[...]), z_old[:, IN_SPLITS[1]:IN_SPLITS[2]]], axis=0)
        xc, r, ig, a, mult = _lru_gates(zl_cat, cw_ref[...], cb_ref[...], wrg_ref, brg_ref[...], wig_ref,
                                        big_ref[...], sp, first_row, before_scan.tick)
        a_s[...] = a
        b_s[...] = mult * ig * xc
        xc_ref[...] = xc.astype(BF16)
        r_ref[...] = r.astype(BF16)
        ig_ref[...] = ig.astype(BF16)
        before_scan.flush()

        rows8 = _row_ids((SUBLANES, D_MODEL))

        def group(g, carry):
            at = pl.ds(pl.multiple_of(g * SUBLANES, SUBLANES), SUBLANES)
            A, B = a_s[at, :], b_s[at, :]
            for s in (1, 2, 4):
                m = rows8 >= s
                B = jnp.where(m, A * pltpu.roll(B, s, 0) + B, B)
                A = jnp.where(m, A * pltpu.roll(A, s, 0), A)
            h = A * carry + B
            hs_ref[at, :] = h
            return jnp.broadcast_to(h[SUBLANES - 1:SUBLANES, :], (SUBLANES, D_MODEL))

        carry_s[...] = lax.fori_loop(0, n_groups, group, jnp.where(first, 0.0, carry_s[...]))
        gelu = _gelu(z_old[:, IN_SPLITS[2]:IN_SPLITS[3]])
        after_scan.tick(2)
        y_lru = _nn((hs_ref[...] * gelu).astype(BF16), lp_ref[...])

        gates = _sigmoid(z_old[:, IN_SPLITS[3]:IN_SPLITS[4]] + bg_ref[...])
        after_scan.tick(2)
        merged = gates[:, :D_MODEL] * y_pool + gates[:, D_MODEL:] * y_lru
        after_scan.flush()
        h1_ref[...] = x_ref[...] + _nn(merged.astype(BF16), wo_ref[...])
        yp_ref[...] = y_pool.astype(BF16)
        yl_ref[...] = y_lru.astype(BF16)
        zp_halo[...] = z_old[tt - POOL_HALO:, IN_SPLITS[0]:IN_SPLITS[1]]
        zl_halo[...] = z_old[tt - SUBLANES:, IN_SPLITS[1]:IN_SPLITS[2]]

    def ahead(cols):
        return pl.BlockSpec((tt, cols), lambda s: (jnp.minimum(s, nt - 1), 0))

    def behind(cols):
        return pl.BlockSpec((tt, cols), lambda s: (jnp.maximum(s - 1, 0), 0))

    res = [norm1_g, w_in, b_gate, pool_w, pool_scale, pool_proj, conv_w, conv_b, w_rg, b_rg, w_ig, b_ig, lru_lambda,
           lru_proj, w_out]
    return pl.pallas_call(
        body, name="f12_mixer", grid=(nt + 1,),
        in_specs=[ahead(D_MODEL), behind(D_MODEL)] + [_resident(w.shape) for w in res],
        out_specs=[ahead(w) for w in IN_WIDTHS] + [ahead(D_MODEL)] + [behind(D_MODEL)] * 7,
        out_shape=[jax.ShapeDtypeStruct((T, w), dt) for w, dt in zip(IN_WIDTHS, (F32, F32, F32, BF16))]
        + [jax.ShapeDtypeStruct((T, D_MODEL), BF16), jax.ShapeDtypeStruct((T, D_MODEL), F32),
           jax.ShapeDtypeStruct((T, D_MODEL), F32)] + [jax.ShapeDtypeStruct((T, D_MODEL), BF16)] * 5,
        scratch_shapes=[pltpu.VMEM((2, tt, IN_SPLITS[4]), F32), pltpu.VMEM((POOL_HALO, POOL_WIDTH), F32),
                        pltpu.VMEM((SUBLANES, D_MODEL), F32), pltpu.VMEM((tt, D_MODEL), F32),
                        pltpu.VMEM((tt, D_MODEL), F32), pltpu.VMEM((SUBLANES, D_MODEL), F32)],
        compiler_params=_params(("arbitrary",)),
    )(x, x, *res)


def _f3_ffn(h1, norm2_g, w_ffn_in, w_ffn_out):
    T = h1.shape[0]
    tm = ROW_TILE

    def body(h_ref, g_ref, wi_ref, wo_ref, h2_ref, v_ref, ff_ref, act_ref):
        h = h_ref[...]
        n, _ = _rms(h)
        v = (n * g_ref[...]).astype(BF16)
        v_ref[...] = v
        g_ff = _nn(v, wi_ref[:, :D_FF])
        u_ff = _nn(v, wi_ref[:, D_FF:])
        ff_ref[:, :D_FF] = g_ff.astype(BF16)
        ff_ref[:, D_FF:] = u_ff.astype(BF16)
        act = (g_ff * _sigmoid(g_ff) * u_ff).astype(BF16)
        act_ref[...] = act
        h2_ref[...] = h + _nn(act, wo_ref[...])

    return pl.pallas_call(
        body, name="f3_ffn", grid=(T // tm,),
        in_specs=[_rows(tm, D_MODEL), _resident((1, D_MODEL)), _resident(w_ffn_in.shape), _resident(w_ffn_out.shape)],
        out_specs=[_rows(tm, D_MODEL), _rows(tm, D_MODEL), _rows(tm, 2 * D_FF), _rows(tm, D_FF)],
        out_shape=[jax.ShapeDtypeStruct((T, D_MODEL), F32), jax.ShapeDtypeStruct((T, D_MODEL), BF16),
                   jax.ShapeDtypeStruct((T, 2 * D_FF), BF16), jax.ShapeDtypeStruct((T, D_FF), BF16)],
        compiler_params=_params(("arbitrary",)),
    )(h1, norm2_g, w_ffn_in, w_ffn_out)


def _b4_ple_loss(h2, p, target, ple_norm_g, w_ple_gate, w_ple_proj, final_g):
    T = h2.shape[0]
    tm = WIDE_TILE

    def body(h_ref, p_ref, t_ref, gp_ref, wg_ref, wp_ref, gf_ref, loss_ref, dh2_ref, dwg_ref, dwp_ref, vec_ref):
        @pl.when(pl.program_id(0) == 0)
        def _():
            loss_ref[...] = jnp.zeros_like(loss_ref)
            dwg_ref[...] = jnp.zeros_like(dwg_ref)
            dwp_ref[...] = jnp.zeros_like(dwp_ref)
            vec_ref[...] = jnp.zeros_like(vec_ref)

        h2v = h_ref[...]
        n3, r3 = _rms(h2v)
        n3g = (n3 * gp_ref[...]).astype(BF16)
        pg = _sigmoid(_nn(n3g, wg_ref[...]))
        pb = p_ref[...].astype(BF16)
        e = _nn(pb, wp_ref[...])
        h3 = h2v + pg * e
        n4, r4 = _rms(h3)
        diff = n4 * gf_ref[...] - t_ref[...]
        loss_ref[...] += jnp.sum(diff * diff).reshape(1, 1)
        dy = diff * (1.0 / D_MODEL)
        vec_ref[0:1, :] += _colsum(dy * n4)
        dh3 = _rms_bwd(dy * gf_ref[...], n4, r4)
        dwp_ref[...] += _tn(pb, (dh3 * pg).astype(BF16))
        dq = (dh3 * e * pg * (1.0 - pg)).astype(BF16)
        dwg_ref[...] += _tn(n3g, dq)
        dn3g = _nt(dq, wg_ref[...])
        vec_ref[1:2, :] += _colsum(dn3g * n3)
        dh2_ref[...] = dh3 + _rms_bwd(dn3g * gp_ref[...], n3, r3)

    return pl.pallas_call(
        body, name="b4_ple_loss", grid=(T // tm,),
        in_specs=[_rows(tm, D_MODEL), _rows(tm, PLE_DIM), _rows(tm, D_MODEL), _resident((1, D_MODEL)),
                  _resident(w_ple_gate.shape), _resident(w_ple_proj.shape), _resident((1, D_MODEL))],
        out_specs=[_acc((1, 1)), _rows(tm, D_MODEL), _acc(w_ple_gate.shape), _acc(w_ple_proj.shape),
                   _acc((SUBLANES, D_MODEL))],
        out_shape=[jax.ShapeDtypeStruct((1, 1), F32), jax.ShapeDtypeStruct((T, D_MODEL), F32),
                   jax.ShapeDtypeStruct(w_ple_gate.shape, F32), jax.ShapeDtypeStruct(w_ple_proj.shape, F32),
                   jax.ShapeDtypeStruct((SUBLANES, D_MODEL), F32)],
        compiler_params=_params(("arbitrary",)),
    )(h2, p, target, ple_norm_g, w_ple_gate, w_ple_proj, final_g)


def _b3_ffn(dh2, h1, ff, norm2_g, w_ffn_in, w_ffn_out):
    T = h1.shape[0]
    tm = ROW_TILE

    def body(d_ref, h_ref, ff_ref, g_ref, wi_ref, wo_ref, dff_ref, dh1_ref, vec_ref):
        @pl.when(pl.program_id(0) == 0)
        def _():
            vec_ref[...] = jnp.zeros_like(vec_ref)

        dh2v = d_ref[...]
        dact = _nt(dh2v.astype(BF16), wo_ref[...])
        g_ff = ff_ref[:, :D_FF].astype(F32)
        u_ff = ff_ref[:, D_FF:].astype(F32)
        s = _sigmoid(g_ff)
        dg = (dact * u_ff * (s * (1.0 + g_ff * (1.0 - s)))).astype(BF16)
        du = (dact * (g_ff * s)).astype(BF16)
        dff_ref[:, :D_FF] = dg
        dff_ref[:, D_FF:] = du
        dv = _nt(dg, wi_ref[:, :D_FF]) + _nt(du, wi_ref[:, D_FF:])
        n2, r2 = _rms(h_ref[...])
        vec_ref[0:1, :] += _colsum(dv * n2)
        dh1_ref[...] = dh2v + _rms_bwd(dv * g_ref[...], n2, r2)

    return pl.pallas_call(
        body, name="b3_ffn", grid=(T // tm,),
        in_specs=[_rows(tm, D_MODEL), _rows(tm, D_MODEL), _rows(tm, 2 * D_FF), _resident((1, D_MODEL)),
                  _resident(w_ffn_in.shape), _resident(w_ffn_out.shape)],
        out_specs=[_rows(tm, 2 * D_FF), _rows(tm, D_MODEL), _acc((SUBLANES, D_MODEL))],
        out_shape=[jax.ShapeDtypeStruct((T, 2 * D_FF), BF16), jax.ShapeDtypeStruct((T, D_MODEL), F32),
                   jax.ShapeDtypeStruct((SUBLANES, D_MODEL), F32)],
        compiler_params=_params(("arbitrary",)),
    )(dh2, h1, ff, norm2_g, w_ffn_in, w_ffn_out)


def _wgrad(a, b, col_tile, name, tokens=WGRAD_TOKENS, after=None):
    T, K = a.shape
    N = b.shape[1]
    tk = min(T, tokens)

    def body(a_ref, b_ref, *rest):
        o_ref = rest[-1]

        @pl.when(pl.program_id(1) == 0)
        def _():
            o_ref[...] = jnp.zeros_like(o_ref)

        o_ref[...] += _tn(a_ref[...].astype(BF16), b_ref[...].astype(BF16))

    return pl.pallas_call(
        body, name=name, grid=(N // col_tile, T // tk),
        in_specs=[pl.BlockSpec((tk, K), lambda j, k: (k, 0)), pl.BlockSpec((tk, col_tile), lambda j, k: (k, j))]
        + ([] if after is None else [ANY]),
        out_specs=pl.BlockSpec((K, col_tile), lambda j, k: (0, j)),
        out_shape=jax.ShapeDtypeStruct((K, N), F32),
        compiler_params=_params(("arbitrary", "arbitrary")),
    )(a, b, *([] if after is None else [after]))


def _b2_gates(dh1, zt, yp, yl, b_gate, w_out):
    T = dh1.shape[0]
    tm = WIDE_TILE

    def body(d_ref, zt_ref, yp_ref, yl_ref, bg_ref, wo_ref, dzt_ref, dyp_ref, dyl_ref, dwo_ref, vec_ref):
        @pl.when(pl.program_id(0) == 0)
        def _():
            dwo_ref[...] = jnp.zeros_like(dwo_ref)
            vec_ref[...] = jnp.zeros_like(vec_ref)

        db = d_ref[...].astype(BF16)
        dm = _nt(db, wo_ref[...])
        gates = _sigmoid(zt_ref[...].astype(F32) + bg_ref[...])
        g0, g1 = gates[:, :D_MODEL], gates[:, D_MODEL:]
        y_pool, y_lru = yp_ref[...].astype(F32), yl_ref[...].astype(F32)
        dwo_ref[...] += _tn((g0 * y_pool + g1 * y_lru).astype(BF16), db)
        dz0 = dm * y_pool * g0 * (1.0 - g0)
        dz1 = dm * y_lru * g1 * (1.0 - g1)
        vec_ref[0:1, :] += _colsum(dz0)
        vec_ref[1:2, :] += _colsum(dz1)
        dzt_ref[:, :D_MODEL] = dz0.astype(BF16)
        dzt_ref[:, D_MODEL:] = dz1.astype(BF16)
        dyp_ref[...] = (dm * g0).astype(BF16)
        dyl_ref[...] = (dm * g1).astype(BF16)

    return pl.pallas_call(
        body, name="b2_gates", grid=(T // tm,),
        in_specs=[_rows(tm, D_MODEL), _rows(tm, 2 * D_MODEL), _rows(tm, D_MODEL), _rows(tm, D_MODEL),
                  _resident(b_gate.shape), _resident(w_out.shape)],
        out_specs=[_rows(tm, 2 * D_MODEL), _rows(tm, D_MODEL), _rows(tm, D_MODEL), _acc(w_out.shape),
                   _acc((SUBLANES, D_MODEL))],
        out_shape=[jax.ShapeDtypeStruct((T, 2 * D_MODEL), BF16), jax.ShapeDtypeStruct((T, D_MODEL), BF16),
                   jax.ShapeDtypeStruct((T, D_MODEL), BF16), jax.ShapeDtypeStruct(w_out.shape, F32),
                   jax.ShapeDtypeStruct((SUBLANES, D_MODEL), F32)],
        compiler_params=_params(("arbitrary",)),
    )(dh1, zt, yp, yl, b_gate, w_out)


def _b12_pool_in_proj(dyp, zp, dzl, dzg, dzt, x, dh1, norm1_g, w_in, pool_w, pool_scale, pool_proj):
    T = zp.shape[0]
    tt = ROW_TILE
    nt = T // tt

    def body(dy_ref, zp_ref, zph_ref, dzl_ref, dzg_ref, dzt_ref, x_ref, dh_ref, g1_ref, win_ref, pw_ref, ps_ref, pp_ref,
             dzp_ref, dx_ref, dpp_ref, dpw_ref, vec_ref, q_next):
        i = pl.program_id(0)
        ti = nt - 1 - i
        first_row = ti * tt

        @pl.when(i == 0)
        def _():
            dpp_ref[...] = jnp.zeros_like(dpp_ref)
            dpw_ref[...] = jnp.zeros_like(dpw_ref)
            vec_ref[...] = jnp.zeros_like(vec_ref)
            q_next[...] = jnp.zeros_like(q_next)

        du_parts = []

        def project(lo):
            k = max(i for i in range(4) if IN_SPLITS[i] <= lo)
            dz_ref = (None, dzl_ref, dzg_ref, dzt_ref)[k]
            at = lo - IN_SPLITS[k]
            part = _nt(dz_ref[:, at:at + PROJ_CHUNK], win_ref[:, lo:lo + PROJ_CHUNK])
            du_parts[:] = [part if not du_parts else du_parts[0] + part]

        mxu = _Interleaved(functools.partial(project, lo) for lo in range(IN_SPLITS[1], IN_SPLITS[4], PROJ_CHUNK))

        keep = (ti > 0).astype(F32)
        zp_cat = jnp.concatenate([zph_ref[...] * keep, zp_ref[...]], axis=0)
        pooled, mixed = _pool_forward(zp_cat, pw_ref, first_row, mxu.tick)
        dy = dy_ref[...]
        dpp_ref[...] += _tn((mixed * ps_ref[...]).astype(BF16), dy)
        mxu.tick(2)
        dms = _nt(dy, pp_ref[...])
        mxu.tick(2)
        vec_ref[0:1, :POOL_WIDTH] += _colsum(dms * mixed)
        dmixed = (dms * ps_ref[...]).astype(BF16)
        t_glob = first_row + _row_ids((tt, POOL_GROUP_DIM))
        dz, q_all, dpooled_pairs = [], [], []
        for p in range(len(POOL_WINDOWS) // 2):
            pair = slice(p * PAIR_DIM, (p + 1) * PAIR_DIM)
            dpw_ref[p] += _tn(pooled[:, pair].astype(BF16), dmixed[:, pair])
            dpooled_pairs.append(_nt(dmixed[:, pair], pw_ref[p]))
        dpooled_all = jnp.concatenate(dpooled_pairs, axis=1)
        for g, w in enumerate(POOL_WINDOWS):
            cols = slice(g * POOL_GROUP_DIM, (g + 1) * POOL_GROUP_DIM)
            dpooled = dpooled_all[:, cols]
            q = dpooled / jnp.minimum(t_glob + 1, w).astype(F32)
            q_all.append(q)
            s, k = jnp.concatenate([q, q_next[:, cols]], axis=0), 1
            while k < w:
                s = s + _shift_up(s, k)
                k *= 2
            dz.append(s[:tt] - dpooled)
            mxu.tick(2)
        dzp = jnp.concatenate(dz, axis=1).astype(BF16)
        dzp_ref[...] = dzp
        q_next[...] = jnp.concatenate([q[:POOL_HALO] for q in q_all], axis=1)
        mxu.flush()

        du = du_parts[0] + _nt(dzp, win_ref[:, IN_SPLITS[0]:IN_SPLITS[1]])
        n1, r1 = _rms(x_ref[...])
        vec_ref[1:2, :] += _colsum(du * n1)
        dx_ref[...] = dh_ref[...] + _rms_bwd(du * g1_ref[...], n1, r1)

    rev = functools.partial(_rows_rev, n_tiles=nt)
    res = [norm1_g, w_in, pool_w, pool_scale, pool_proj]
    return pl.pallas_call(
        body, name="b12_pool_in_proj", grid=(nt,),
        in_specs=[rev(tt, D_MODEL), rev(tt, POOL_WIDTH), _halo_before_rev(POOL_HALO, POOL_WIDTH, tt, nt),
                  rev(tt, D_MODEL), rev(tt, D_MODEL), rev(tt, 2 * D_MODEL), rev(tt, D_MODEL), rev(tt, D_MODEL)]
        + [_resident(w.shape) for w in res],
        out_specs=[rev(tt, POOL_WIDTH), rev(tt, D_MODEL), _acc(pool_proj.shape), _acc(pool_w.shape),
                   _acc((SUBLANES, D_MODEL))],
        out_shape=[jax.ShapeDtypeStruct((T, POOL_WIDTH), BF16), jax.ShapeDtypeStruct((T, D_MODEL), F32),
                   jax.ShapeDtypeStruct(pool_proj.shape, F32), jax.ShapeDtypeStruct(pool_w.shape, F32),
                   jax.ShapeDtypeStruct((SUBLANES, D_MODEL), F32)],
        scratch_shapes=[pltpu.VMEM((POOL_HALO, POOL_WIDTH), F32)],
        compiler_params=_params(("arbitrary",)),
    )(dyp, zp, zp, dzl, dzg, dzt, x, dh1, *res)


_V_CONVW, _V_CONVB, _V_BRG, _V_BIG, _V_LAM = 0, 4, 5, 6, 7


def _b2_lru(dyl, zl, zg, hs, xc_saved, r_saved, ig_saved, conv_w, w_rg, w_ig, lru_lambda, lru_proj):
    T = zl.shape[0]
    tt = ROW_TILE
    nt = T // tt
    n_groups = tt // SUBLANES

    def body(dy_ref, zl_ref, zlh_ref, zg_ref, hs_ref, hsh_ref, xc_ref, r_ref, ig_ref, cw_ref, wrg_ref, wig_ref,
             lam_ref, lp_ref, dzl_ref, dzg_ref, dlp_ref, dwrg_ref, dwig_ref, vec_ref,
             c_s, d_s, g_s, g_next, a_next, dxc_next):
        i = pl.program_id(0)
        ti = nt - 1 - i
        first_row = ti * tt

        @pl.when(i == 0)
        def _():
            dlp_ref[...] = jnp.zeros_like(dlp_ref)
            dwrg_ref[...] = jnp.zeros_like(dwrg_ref)
            dwig_ref[...] = jnp.zeros_like(dwig_ref)
            vec_ref[...] = jnp.zeros_like(vec_ref)
            g_next[...] = jnp.zeros_like(g_next)
            a_next[...] = jnp.zeros_like(a_next)
            dxc_next[...] = jnp.zeros_like(dxc_next)

        keep = (ti > 0).astype(F32)
        sp, dsp_dlam = _softplus_neg(lam_ref[...])
        cw = cw_ref[...]
        zl_cat = jnp.concatenate([zlh_ref[...] * keep, zl_ref[...]], axis=0)
        xc, r, ig = xc_ref[...].astype(F32), r_ref[...].astype(F32), ig_ref[...].astype(F32)
        a, mult = _decay(r, sp, first_row)
        hs = hs_ref[...]
        gelu, dgelu = _gelu_and_grad(zg_ref[...])
        dy = dy_ref[...]
        dlp_ref[...] += _tn((hs * gelu).astype(BF16), dy)
        dyl = _nt(dy, lp_ref[...])
        dzg_ref[...] = (dyl * hs * dgelu).astype(BF16)

        d_s[...] = dyl * gelu
        c_s[...] = _shift_up(jnp.concatenate([a, a_next[...]], axis=0), 1)[:tt]
        rows8 = _row_ids((SUBLANES, D_MODEL))

        def group(k, carry):
            at = pl.ds(pl.multiple_of((n_groups - 1 - k) * SUBLANES, SUBLANES), SUBLANES)
            C, Dv = c_s[at, :], d_s[at, :]
            for s in (1, 2, 4):
                m = rows8 < SUBLANES - s
                Dv = jnp.where(m, C * pltpu.roll(Dv, SUBLANES - s, 0) + Dv, Dv)
                C = jnp.where(m, C * pltpu.roll(C, SUBLANES - s, 0), C)
            G = C * carry + Dv
            g_s[at, :] = G
            return jnp.broadcast_to(G[0:1, :], (SUBLANES, D_MODEL))

        g_next[...] = lax.fori_loop(0, n_groups, group, g_next[...])
        a_next[...] = jnp.broadcast_to(a[0:1, :], (SUBLANES, D_MODEL))
        G = g_s[...]

        h_prev = _shift_down(jnp.concatenate([hsh_ref[...] * keep, hs], axis=0), 1)[SUBLANES:]
        t_glob = first_row + _row_ids((tt, D_MODEL))
        dmult = jnp.where(t_glob == 0, 0.0, G * ig * xc)
        dla = G * h_prev * a - dmult * (a * a) / mult
        vec_ref[_V_LAM:_V_LAM + 1, :] += _colsum(dla * r) * (-LRU_C) * dsp_dlam
        dpr = dla * (-LRU_C) * sp * r * (1.0 - r)
        dpi = G * mult * xc * ig * (1.0 - ig)
        vec_ref[_V_BRG:_V_BRG + 1, :] += _colsum(dpr)
        vec_ref[_V_BIG:_V_BIG + 1, :] += _colsum(dpi)
        dprb, dpib, xh = dpr.astype(BF16), dpi.astype(BF16), xc_ref[...]
        dxc_h = []
        for p in range(LRU_HEADS // 2):
            cols = slice(p * PAIR_DIM, (p + 1) * PAIR_DIM)
            dwrg_ref[p] += _tn(xh[:, cols], dprb[:, cols])
            dwig_ref[p] += _tn(xh[:, cols], dpib[:, cols])
            dxc_h.append(_nt(dprb[:, cols], wrg_ref[p]) + _nt(dpib[:, cols], wig_ref[p]))
        dxc = G * mult * ig + jnp.concatenate(dxc_h, axis=1)

        vec_ref[_V_CONVB:_V_CONVB + 1, :] += _colsum(dxc)
        dxc_cat = jnp.concatenate([dxc, dxc_next[...]], axis=0)
        dzl = cw[CONV_WIDTH - 1:CONV_WIDTH] * dxc
        for k in range(CONV_WIDTH):
            lag = CONV_WIDTH - 1 - k
            vec_ref[_V_CONVW + k:_V_CONVW + k + 1, :] += _colsum(dxc * _shift_down(zl_cat, lag)[SUBLANES:])
            if lag:
                dzl = dzl + cw[k:k + 1] * _shift_up(dxc_cat, lag)[:tt]
        dzl_ref[...] = dzl.astype(BF16)
        dxc_next[...] = dxc[:SUBLANES]

    res = [conv_w, w_rg, w_ig, lru_lambda, lru_proj]
    return pl.pallas_call(
        body, name="b2_lru", grid=(nt,),
        in_specs=[_rows_rev(tt, D_MODEL, nt), _rows_rev(tt, D_MODEL, nt), _halo_before_rev(SUBLANES, D_MODEL, tt, nt),
                  _rows_rev(tt, D_MODEL, nt), _rows_rev(tt, D_MODEL, nt), _halo_before_rev(SUBLANES, D_MODEL, tt, nt)]
        + [_rows_rev(tt, D_MODEL, nt)] * 3 + [_resident(w.shape) for w in res],
        out_specs=[_rows_rev(tt, D_MODEL, nt), _rows_rev(tt, D_MODEL, nt), _acc(lru_proj.shape), _acc(w_rg.shape),
                   _acc(w_ig.shape), _acc((SUBLANES, D_MODEL))],
        out_shape=[jax.ShapeDtypeStruct((T, D_MODEL), BF16), jax.ShapeDtypeStruct((T, D_MODEL), BF16),
                   jax.ShapeDtypeStruct(lru_proj.shape, F32), jax.ShapeDtypeStruct(w_rg.shape, F32),
                   jax.ShapeDtypeStruct(w_ig.shape, F32), jax.ShapeDtypeStruct((SUBLANES, D_MODEL), F32)],
        scratch_shapes=[pltpu.VMEM((tt, D_MODEL), F32)] * 3 + [pltpu.VMEM((SUBLANES, D_MODEL), F32)] * 3,
        compiler_params=_params(("arbitrary",)),
    )(dyl, zl, zl, zg, hs, hs, xc_saved, r_saved, ig_saved, *res)


def _row_tile(rows):
    for t in (512, 256, 128, 64, 32, 16, 8):
        if rows % t == 0:
            return t
    return rows


def _scalar_grid(grid, in_specs, out_specs):
    return pltpu.PrefetchScalarGridSpec(num_scalar_prefetch=1, grid=grid, in_specs=in_specs, out_specs=out_specs)


def _cast_into_block(w, by_rows, shard_j, name):
    R, C = w.shape
    tr = _row_tile(R)
    if by_rows:
        out_shape, out_map = (N_SHARDS * R, C), lambda i, j: (j[0] * (R // tr) + i, 0)
    else:
        out_shape, out_map = (R, N_SHARDS * C), lambda i, j: (i, j[0])

    def body(j_ref, w_ref, o_ref):
        o_ref[...] = w_ref[...].astype(BF16)

    return pl.pallas_call(
        body, name=name,
        grid_spec=_scalar_grid((R // tr,), [pl.BlockSpec((tr, C), lambda i, j: (i, 0))], pl.BlockSpec((tr, C), out_map)),
        out_shape=jax.ShapeDtypeStruct(out_shape, BF16),
        compiler_params=_params(("arbitrary",)),
    )(shard_j.reshape(1), w)


def _sum_cores(g, theirs, core, name):
    S, R, C = g.shape
    H = R // 2
    tr = _row_tile(H)
    nh = H // tr

    def body(c_ref, g_ref, t_ref, o_ref):
        o_ref[...] = (g_ref[...] + t_ref[...]).astype(BF16)

    half = pl.BlockSpec((None, tr, C), lambda s, i, c: (s, i, 0))
    return pl.pallas_call(
        body, name=name,
        grid_spec=_scalar_grid((S, nh), [pl.BlockSpec((None, tr, C), lambda s, i, c: (s, c[0] * nh + i, 0)), half], half),
        out_shape=jax.ShapeDtypeStruct((S, H, C), BF16),
        compiler_params=_params(("arbitrary", "arbitrary")),
    )(core.reshape(1), g, theirs)


def _sum_chips(sums, slots, by_rows, place, name):
    _, H, C = slots.shape
    tr = _row_tile(H)
    own_map = (lambda i, p: (p[0], i, 0)) if by_rows else (lambda i, p: (0, i, p[0]))

    def body(p_ref, s_ref, q_ref, o_ref):
        o_ref[...] = ((s_ref[...].astype(F32) + q_ref[0].astype(F32)) + q_ref[1].astype(F32)) + q_ref[2].astype(F32)

    return pl.pallas_call(
        body, name=name,
        grid_spec=_scalar_grid(
            (H // tr,),
            [pl.BlockSpec((None, tr, C), own_map), pl.BlockSpec((3, tr, C), lambda i, p: (0, i, 0))],
            pl.BlockSpec((None, tr, C), lambda i, p: (p[1], i, 0))),
        out_shape=jax.ShapeDtypeStruct((2, H, C), F32),
        compiler_params=_params(("arbitrary",)),
    )(place, sums, slots)


def _adamw(w, g, m, v, name):
    R, C = w.shape
    tr = _row_tile(R)
    c1 = 1.0 - ADAM_B1 ** ADAM_STEP
    c2 = 1.0 - ADAM_B2 ** ADAM_STEP

    def body(w_ref, g_ref, m_ref, v_ref, d_ref, nm_ref, nv_ref):
        gv = g_ref[...]
        nm = ADAM_B1 * m_ref[...] + (1.0 - ADAM_B1) * gv
        nv = ADAM_B2 * v_ref[...] + (1.0 - ADAM_B2) * (gv * gv)
        d_ref[...] = -ADAM_LR * ((nm / c1) / (jnp.sqrt(nv / c2) + ADAM_EPS) + ADAM_WD * w_ref[...])
        nm_ref[...] = nm
        nv_ref[...] = nv

    return pl.pallas_call(
        body, name=name, grid=(R // tr,),
        in_specs=[_rows(tr, C)] * 4, out_specs=[_rows(tr, C)] * 3,
        out_shape=[jax.ShapeDtypeStruct((R, C), F32)] * 3,
        compiler_params=_params(("arbitrary",)),
    )(w, g, m, v)


def _place():
    return lax.axis_index("x"), lax.axis_index("y"), lax.axis_index("c")


def _other_chips(x, y):
    return [(1 - x, y), (x, 1 - y), (1 - x, 1 - y)]


def _shard_block(ref, by_rows, R, C, j, half_rows=None):
    if half_rows is None:
        rows, r0 = R, 0
    else:
        rows = R // 2
        r0 = pl.multiple_of(half_rows * rows, 16)
    if by_rows:
        return ref.at[pl.ds(pl.multiple_of(j * R, 16) + r0, rows), :]
    return ref.at[pl.ds(r0, rows), pl.ds(pl.multiple_of(j * C, 128), C)]


def _all_gather_weights(gathered, shapes, by_rows, small):
    n = len(gathered)

    def body(*refs):
        small_in = refs[n]
        outs, small_out = refs[n + 1:2 * n + 1], refs[2 * n + 1]
        send_sems, recv_sems, local_sem = refs[2 * n + 2:]
        x, y, c = _place()
        me_j = 2 * x + y
        chips = _other_chips(x, y)
        sibling = (x, y, 1 - c)

        def block(i, j, half):
            R, C = shapes[i]
            return _shard_block(outs[i], by_rows[i], R, C, j, half)

        def ici(i, k, src_j):
            return pltpu.make_async_remote_copy(
                src_ref=block(i, src_j, c), dst_ref=block(i, src_j, c),
                send_sem=send_sems.at[6 * i + k], recv_sem=recv_sems.at[6 * i + k],
                device_id=(*chips[k], c), device_id_type=MESH)

        def relay(i, k, half):
            kj = 2 * chips[k][0] + chips[k][1]
            return pltpu.make_async_remote_copy(
                src_ref=block(i, kj, half), dst_ref=block(i, kj, half),
                send_sem=send_sems.at[6 * i + 3 + k], recv_sem=recv_sems.at[6 * i + 3 + k],
                device_id=sibling, device_id_type=MESH)

        def small_copy(k, src_j):
            cols = pl.ds(pl.multiple_of(src_j * 256, 128), 256)
            return pltpu.make_async_remote_copy(
                src_ref=small_in, dst_ref=small_out.at[:, cols],
                send_sem=send_sems.at[6 * n + k], recv_sem=recv_sems.at[6 * n + k],
                device_id=(*chips[k], c), device_id_type=MESH)

        sends = []
        for i in range(n):
            for k in range(3):
                cp = ici(i, k, me_j)
                cp.start()
                sends.append(cp)
        for k in range(3):
            cp = small_copy(k, me_j)
            cp.start()
            sends.append(cp)
        local = pltpu.make_async_copy(small_in, small_out.at[:, pl.ds(pl.multiple_of(me_j * 256, 128), 256)], local_sem)
        local.start()
        for i in range(n):
            for k in range(3):
                kj = 2 * chips[k][0] + chips[k][1]
                ici(i, k, kj).wait_recv()
                cp = relay(i, k, c)
                cp.start()
                sends.append(cp)
        for k in range(3):
            small_copy(k, 2 * chips[k][0] + chips[k][1]).wait_recv()
        for i in range(n):
            for k in range(3):
                relay(i, k, 1 - c).wait_recv()
        for cp in sends:
            cp.wait_send()
        local.wait()

    out_shape = [jax.ShapeDtypeStruct(g.shape, BF16) for g in gathered]
    out_shape.append(jax.ShapeDtypeStruct((8, N_SHARDS * 256), F32))
    n_sems = 6 * n + 3
    return pl.pallas_call(
        body, name="all_gather_weights",
        in_specs=[ANY] * (n + 1), out_specs=[ANY] * (n + 1), out_shape=out_shape,
        input_output_aliases={i: i for i in range(n)},
        scratch_shapes=[pltpu.SemaphoreType.DMA((n_sems,)), pltpu.SemaphoreType.DMA((n_sems,)),
                        pltpu.SemaphoreType.DMA],
    )(*gathered, small)


def _core_exchange(grads, name):
    n = len(grads)

    def body(*refs):
        copies = _core_exchange_copies(refs[:n], refs[n:2 * n], refs[2 * n], refs[2 * n + 1])
        for cp in copies:
            cp.start()
        for cp in copies:
            cp.wait()

    return pl.pallas_call(
        body, name=name,
        in_specs=[ANY] * n, out_specs=[ANY] * n,
        out_shape=[jax.ShapeDtypeStruct((g.shape[0], g.shape[1] // 2, g.shape[2]), F32) for g in grads],
        scratch_shapes=[pltpu.SemaphoreType.DMA((n,))] * 2,
    )(*grads)


HBM = pl.BlockSpec(memory_space=pltpu.HBM)
SEM = pl.BlockSpec(memory_space=pltpu.SEMAPHORE)
TOKEN = jax.ShapeDtypeStruct((SUBLANES, 128), F32)


def _in_hbm(a):
    return pltpu.with_memory_space_constraint(a, pltpu.HBM)


def _split_params():
    return pltpu.CompilerParams(has_side_effects=pltpu.SideEffectType.DATAFLOW_SIDE_EFFECTING)


def _gather_rest_copies(refs, shapes, by_rows, send_sems, recv_sems):
    x, y, c = _place()
    me_j = 2 * x + y
    chips = _other_chips(x, y)
    pairs = []
    for i, ref in enumerate(refs):
        R, C = shapes[i]
        for k in range(3):
            kj = 2 * chips[k][0] + chips[k][1]

            def copy(j, ref=ref, i=i, k=k, R=R, C=C):
                blk = _shard_block(ref, by_rows[i], R, C, j)
                return pltpu.make_async_remote_copy(
                    src_ref=blk, dst_ref=blk, send_sem=send_sems.at[3 * i + k], recv_sem=recv_sems.at[3 * i + k],
                    device_id=(*chips[k], c), device_id_type=MESH)

            pairs.append((copy(me_j), copy(kj)))
    return pairs


def _gather_rest_start(gathered, shapes, by_rows, after):
    n = len(gathered)

    def body(*refs):
        ins = refs[:n]
        send_sems, recv_sems = refs[n + 1], refs[n + 2]
        token = refs[-1]
        for mine, _ in _gather_rest_copies(ins, shapes, by_rows, send_sems, recv_sems):
            mine.start()
        token[...] = jnp.zeros_like(token)

    out = pl.pallas_call(
        body, name="gather_rest_start",
        out_shape=(pltpu.SemaphoreType.DMA((3 * n,)), pltpu.SemaphoreType.DMA((3 * n,)),
                   *[pltpu.HBM(g.shape, g.dtype) for g in gathered], TOKEN),
        in_specs=[HBM] * n + [ANY], out_specs=(SEM, SEM, *[HBM] * n, pl.BlockSpec(memory_space=pltpu.VMEM)),
        input_output_aliases={i: 2 + i for i in range(n)},
        compiler_params=_split_params(),
    )(*[_in_hbm(g) for g in gathered], after)
    return out[0], out[1], out[2:2 + n], out[-1]


def _gather_rest_wait(send_sems, recv_sems, gathered, shapes, by_rows, after):
    n = len(gathered)

    def body(*refs):
        ins = refs[:n]
        send, recv = refs[n], refs[n + 1]
        for mine, theirs in _gather_rest_copies(ins, shapes, by_rows, send, recv):
            mine.wait_send()
            theirs.wait_recv()

    return pl.pallas_call(
        body, name="gather_rest_wait",
        out_shape=tuple(pltpu.HBM(g.shape, g.dtype) for g in gathered),
        in_specs=[HBM] * n + [SEM, SEM, ANY], out_specs=tuple([HBM] * n),
        input_output_aliases={i: i for i in range(n)},
        compiler_params=_split_params(),
    )(*gathered, send_sems, recv_sems, after)


def _chip_exchange_copies(ins, slots, dims, by_rows, send_sems, recv_sems):
    x, y, c = _place()
    chips = _other_chips(x, y)
    pairs = []
    for i in range(len(ins)):
        for k in range(3):
            kj = 2 * chips[k][0] + chips[k][1]
            if by_rows[i]:
                src = ins[i].at[kj]
            else:
                src = ins[i].at[0, :, pl.ds(pl.multiple_of(kj * dims[i][1], 128), dims[i][1])]
            cp = pltpu.make_async_remote_copy(
                src_ref=src, dst_ref=slots[i].at[k], send_sem=send_sems.at[3 * i + k], recv_sem=recv_sems.at[3 * i + k],
                device_id=(*chips[k], c), device_id_type=MESH)
            pairs.append((cp, cp))
    return pairs


def _exchange_dims(sums, by_rows):
    return [(s.shape[1], s.shape[2]) if by_rows[i] else (s.shape[1], s.shape[2] // N_SHARDS) for i, s in enumerate(sums)]


def _chip_exchange_start(sums, by_rows, tag):
    n = len(sums)
    sums = list(sums)
    dims = _exchange_dims(sums, by_rows)
    slots = [lax.empty((3, h, cc), BF16) for h, cc in dims]

    def body(*refs):
        ins, land = refs[:n], refs[n:2 * n]
        send_sems, recv_sems = refs[2 * n], refs[2 * n + 1]
        token = refs[-1]
        for cp, _ in _chip_exchange_copies(ins, land, dims, by_rows, send_sems, recv_sems):
            cp.start()
        token[...] = jnp.zeros_like(token)

    out = pl.pallas_call(
        body, name="grad_chip_exchange_start_" + tag,
        out_shape=(pltpu.SemaphoreType.DMA((3 * n,)), pltpu.SemaphoreType.DMA((3 * n,)),
                   *[pltpu.HBM(a.shape, a.dtype) for a in sums + slots], TOKEN),
        in_specs=[HBM] * (2 * n), out_specs=(SEM, SEM, *[HBM] * (2 * n), pl.BlockSpec(memory_space=pltpu.VMEM)),
        input_output_aliases={i: 2 + i for i in range(2 * n)},
        compiler_params=_split_params(),
    )(*[_in_hbm(a) for a in sums + slots])
    return out[0], out[1], out[2:2 + n], out[2 + n:2 + 2 * n], out[-1]


def _chip_exchange_wait(send_sems, recv_sems, sums, slots, by_rows, after, tag):
    n = len(sums)
    sums, slots = list(sums), list(slots)
    dims = _exchange_dims(sums, by_rows)

    def body(*refs):
        ins, land = refs[:n], refs[n:2 * n]
        send, recv = refs[2 * n], refs[2 * n + 1]
        for cp, _ in _chip_exchange_copies(ins, land, dims, by_rows, send, recv):
            cp.wait_send()
            cp.wait_recv()

    out = pl.pallas_call(
        body, name="grad_chip_exchange_wait_" + tag,
        out_shape=tuple(pltpu.HBM(a.shape, a.dtype) for a in sums + slots),
        in_specs=[HBM] * (2 * n) + [SEM, SEM, ANY], out_specs=tuple([HBM] * (2 * n)),
        input_output_aliases={i: i for i in range(2 * n)},
        compiler_params=_split_params(),
    )(*sums, *slots, send_sems, recv_sems, after)
    return out[:n], out[n:]


def _core_exchange_copies(ins, theirs, send_sems, recv_sems):
    x, y, c = _place()
    copies = []
    for i in range(len(ins)):
        H = ins[i].shape[1] // 2
        copies.append(pltpu.make_async_remote_copy(
            src_ref=ins[i].at[:, pl.ds(pl.multiple_of((1 - c) * H, 8), H), :], dst_ref=theirs[i],
            send_sem=send_sems.at[i], recv_sem=recv_sems.at[i], device_id=(x, y, 1 - c), device_id_type=MESH))
    return copies


def _core_exchange_start(grads):
    n = len(grads)
    grads = list(grads)
    theirs = [lax.empty((g.shape[0], g.shape[1] // 2, g.shape[2]), F32) for g in grads]

    def body(*refs):
        for cp in _core_exchange_copies(refs[:n], refs[n:2 * n], refs[2 * n], refs[2 * n + 1]):
            cp.start()
        refs[-1][...] = jnp.zeros_like(refs[-1])

    out = pl.pallas_call(
        body, name="grad_core_exchange_start",
        out_shape=(pltpu.SemaphoreType.DMA((n,)), pltpu.SemaphoreType.DMA((n,)),
                   *[pltpu.HBM(a.shape, a.dtype) for a in grads + theirs], TOKEN),
        in_specs=[HBM] * (2 * n), out_specs=(SEM, SEM, *[HBM] * (2 * n), pl.BlockSpec(memory_space=pltpu.VMEM)),
        input_output_aliases={i: 2 + i for i in range(2 * n)},
        compiler_params=_split_params(),
    )(*[_in_hbm(a) for a in grads + theirs])
    return out[0], out[1], out[2:2 + n], out[2 + n:2 + 2 * n], out[-1]


def _core_exchange_wait(send_sems, recv_sems, grads, theirs, after):
    n = len(grads)
    grads, theirs = list(grads), list(theirs)

    def body(*refs):
        for cp in _core_exchange_copies(refs[:n], refs[n:2 * n], refs[2 * n], refs[2 * n + 1]):
            cp.wait_send()
            cp.wait_recv()

    out = pl.pallas_call(
        body, name="grad_core_exchange_wait",
        out_shape=tuple(pltpu.HBM(a.shape, a.dtype) for a in grads + theirs),
        in_specs=[HBM] * (2 * n) + [SEM, SEM, ANY], out_specs=tuple([HBM] * (2 * n)),
        input_output_aliases={i: i for i in range(2 * n)},
        compiler_params=_split_params(),
    )(*grads, *theirs, send_sems, recv_sems, after)
    return out[:n], out[n:]


def _core_share(reduced, tag):
    n = len(reduced)

    def body(*refs):
        outs = refs[n:2 * n]
        send_sems, recv_sems = refs[2 * n:]
        x, y, c = _place()
        copies = []
        for i in range(n):
            cp = pltpu.make_async_remote_copy(
                src_ref=outs[i].at[c], dst_ref=outs[i].at[c], send_sem=send_sems.at[i], recv_sem=recv_sems.at[i],
                device_id=(x, y, 1 - c), device_id_type=MESH)
            cp.start()
            copies.append(cp)
        for cp in copies:
            cp.wait()

    return pl.pallas_call(
        body, name="grad_core_share_" + tag,
        in_specs=[ANY] * n, out_specs=[ANY] * n,
        out_shape=[jax.ShapeDtypeStruct(r.shape, F32) for r in reduced],
        input_output_aliases={i: i for i in range(n)},
        scratch_shapes=[pltpu.SemaphoreType.DMA((n,))] * 2,
    )(*reduced)


def _small_exchange_copies(pack_ref, slots_ref, send_sems, recv_sems):
    x, y, c = _place()
    peers = [(px, py, pc) for px in (x, 1 - x) for py in (y, 1 - y) for pc in (c, 1 - c)][1:]
    pairs = []
    for k, peer in enumerate(peers):
        def copy(sender, k=k, peer=peer):
            return pltpu.make_async_remote_copy(
                src_ref=pack_ref, dst_ref=slots_ref.at[4 * sender[0] + 2 * sender[1] + sender[2]],
                send_sem=send_sems.at[k], recv_sem=recv_sems.at[k], device_id=peer, device_id_type=MESH)

        pairs.append((copy((x, y, c)), copy(peer)))
    return pairs


def _small_exchange_start(pack):
    slots = lax.empty((N_DEV,) + pack.shape, F32)

    def body(pack_ref, slots_ref, send_sems, recv_sems, pack_thru, slots_thru, token):
        for mine, _ in _small_exchange_copies(pack_ref, slots_ref, send_sems, recv_sems):
            mine.start()
        token[...] = jnp.zeros_like(token)

    return pl.pallas_call(
        body, name="grad_small_exchange_start",
        out_shape=(pltpu.SemaphoreType.DMA((N_DEV - 1,)), pltpu.SemaphoreType.DMA((N_DEV - 1,)),
                   pltpu.HBM(pack.shape, F32), pltpu.HBM(slots.shape, F32), TOKEN),
        in_specs=[HBM, HBM], out_specs=(SEM, SEM, HBM, HBM, pl.BlockSpec(memory_space=pltpu.VMEM)),
        input_output_aliases={0: 2, 1: 3},
        compiler_params=_split_params(),
    )(_in_hbm(pack), _in_hbm(slots))


def _small_exchange_wait(send_sems, recv_sems, pack, slots, after):
    def body(pack_ref, slots_ref, send, recv, after_ref, pack_thru, slots_thru):
        for mine, theirs in _small_exchange_copies(pack_ref, slots_ref, send, recv):
            mine.wait_send()
            theirs.wait_recv()

    return pl.pallas_call(
        body, name="grad_small_exchange_wait",
        out_shape=(pltpu.HBM(pack.shape, F32), pltpu.HBM(slots.shape, F32)),
        in_specs=[HBM, HBM, SEM, SEM, ANY], out_specs=(HBM, HBM),
        input_output_aliases={0: 0, 1: 1},
        compiler_params=_split_params(),
    )(pack, slots, send_sems, recv_sems, after)


def _sum_small(pack, slots, me):
    R, C = pack.shape
    tr = _row_tile(R)

    def body(me_ref, p_ref, q_ref, o_ref):
        acc = jnp.where(me_ref[0] == 0, p_ref[...], q_ref[0])
        for d in range(1, N_DEV):
            acc = acc + jnp.where(me_ref[0] == d, p_ref[...], q_ref[d])
        o_ref[...] = acc

    return pl.pallas_call(
        body, name="sum_small",
        grid_spec=_scalar_grid((R // tr,), [pl.BlockSpec((tr, C), lambda i, m: (i, 0)),
                                            pl.BlockSpec((N_DEV, tr, C), lambda i, m: (0, i, 0))],
                               pl.BlockSpec((tr, C), lambda i, m: (i, 0))),
        out_shape=jax.ShapeDtypeStruct((R, C), F32),
        compiler_params=_params(("arbitrary",)),
    )(me.reshape(1), pack, slots)


def _pack_rows(parts, rows):
    flat = jnp.concatenate([a.reshape(-1) for a in parts])
    return jnp.pad(flat, (0, rows * 128 - flat.shape[0])).reshape(rows, 128)


def _unpack_rows(pack, shapes):
    flat = pack.reshape(-1)
    out, at = [], 0
    for s in shapes:
        size = 1
        for d in s:
            size *= d
        out.append(flat[at:at + size].reshape(s))
        at += size
    return out


def kernel(x, p, norm1_g, w_in, b_gate, pool_w, pool_scale, pool_proj, conv_w, conv_b, w_rg, b_rg, w_ig, b_ig, lru_lambda, lru_proj, w_out, norm2_g, w_ffn_in, w_ffn_out, ple_norm_g, w_ple_gate, w_ple_proj, final_g, loss_target, m_norm1_g, m_w_in, m_b_gate, m_pool_w, m_pool_scale, m_pool_proj, m_conv_w, m_conv_b, m_w_rg, m_b_rg, m_w_ig, m_b_ig, m_lru_lambda, m_lru_proj, m_w_out, m_norm2_g, m_w_ffn_in, m_w_ffn_out, m_ple_norm_g, m_w_ple_gate, m_w_ple_proj, m_final_g, v_norm1_g, v_w_in, v_b_gate, v_pool_w, v_pool_scale, v_pool_proj, v_conv_w, v_conv_b, v_w_rg, v_b_rg, v_w_ig, v_b_ig, v_lru_lambda, v_lru_proj, v_w_out, v_norm2_g, v_w_ffn_in, v_w_ffn_out, v_ple_norm_g, v_w_ple_gate, v_w_ple_proj, v_final_g):
    weights = dict(norm1_g=norm1_g, w_in=w_in, b_gate=b_gate, pool_w=pool_w, pool_scale=pool_scale,
                   pool_proj=pool_proj, conv_w=conv_w, conv_b=conv_b, w_rg=w_rg, b_rg=b_rg, w_ig=w_ig, b_ig=b_ig,
                   lru_lambda=lru_lambda, lru_proj=lru_proj, w_out=w_out, norm2_g=norm2_g, w_ffn_in=w_ffn_in,
                   w_ffn_out=w_ffn_out, ple_norm_g=ple_norm_g, w_ple_gate=w_ple_gate, w_ple_proj=w_ple_proj,
                   final_g=final_g)
    m_in = dict(norm1_g=m_norm1_g, w_in=m_w_in, b_gate=m_b_gate, pool_w=m_pool_w, pool_scale=m_pool_scale,
                pool_proj=m_pool_proj, conv_w=m_conv_w, conv_b=m_conv_b, w_rg=m_w_rg, b_rg=m_b_rg, w_ig=m_w_ig,
                b_ig=m_b_ig, lru_lambda=m_lru_lambda, lru_proj=m_lru_proj, w_out=m_w_out, norm2_g=m_norm2_g,
                w_ffn_in=m_w_ffn_in, w_ffn_out=m_w_ffn_out, ple_norm_g=m_ple_norm_g, w_ple_gate=m_w_ple_gate,
                w_ple_proj=m_w_ple_proj, final_g=m_final_g)
    v_in = dict(norm1_g=v_norm1_g, w_in=v_w_in, b_gate=v_b_gate, pool_w=v_pool_w, pool_scale=v_pool_scale,
                pool_proj=v_pool_proj, conv_w=v_conv_w, conv_b=v_conv_b, w_rg=v_w_rg, b_rg=v_b_rg, w_ig=v_w_ig,
                b_ig=v_b_ig, lru_lambda=v_lru_lambda, lru_proj=v_lru_proj, w_out=v_w_out, norm2_g=v_norm2_g,
                w_ffn_in=v_w_ffn_in, w_ffn_out=v_w_ffn_out, ple_norm_g=v_ple_norm_g, w_ple_gate=v_w_ple_gate,
                w_ple_proj=v_w_ple_proj, final_g=v_final_g)
    names = list(weights)
    big = ["w_in", "pool_proj", "lru_proj", "w_out", "w_ffn_in", "w_ffn_out", "w_ple_gate", "w_ple_proj"]
    by_rows = [n in ("lru_proj", "w_out", "w_ffn_out", "w_ple_gate") for n in big]
    small = [n for n in names if n not in big]

    shard_j = 2 * lax.axis_index("x") + lax.axis_index("y")
    T = x.shape[1]
    xs, ps, tgt = x[0], p[0, 0], loss_target[0]

    small_local = jnp.concatenate([b_gate[0], conv_w[0], jnp.zeros((2, 256), F32)], axis=0)
    core = lax.axis_index("c").astype(jnp.int32)
    place = jnp.stack([shard_j, core]).astype(jnp.int32)
    rows_of = dict(zip(big, by_rows))
    shard_shape = {n: weights[n].shape[1:] for n in big}
    blocks = {n: _cast_into_block(weights[n][0], rows_of[n], place[0], "cast_" + n) for n in big}
    early, late = big[:4], big[4:]
    gathered = _all_gather_weights([blocks[n] for n in early], [shard_shape[n] for n in early],
                                   [rows_of[n] for n in early], small_local)
    full = dict(zip(early, gathered[:-1]))
    late_send, late_recv, late_bufs, late_token = _gather_rest_start(
        [blocks[n] for n in late], [shard_shape[n] for n in late], [rows_of[n] for n in late], gathered[-1])
    b_gate_full = gathered[-1][0:2].reshape(1, 2 * D_MODEL)
    conv_w_full = gathered[-1][2:6]
    pool_w_b, w_rg_b, w_ig_b = [_pair_blocks(w[0].astype(BF16)) for w in (pool_w, w_rg, w_ig)]
    b_rg_row, b_ig_row = b_rg.reshape(1, D_MODEL), b_ig.reshape(1, D_MODEL)
    final_row = final_g.reshape(1, D_MODEL)

    zp, zl, zg, zt, u, h1, hs, yp, yl, xc_saved, r_saved, ig_saved = _f12_mixer(
        xs, norm1_g + late_token[0, 0], full["w_in"], b_gate_full, pool_w_b, pool_scale, full["pool_proj"],
        conv_w_full, conv_b, w_rg_b, b_rg_row, w_ig_b, b_ig_row, lru_lambda, full["lru_proj"], full["w_out"])
    full.update(zip(late, _gather_rest_wait(late_send, late_recv, late_bufs, [shard_shape[n] for n in late],
                                            [rows_of[n] for n in late], h1)))
    h2, v, ff, act = _f3_ffn(h1, norm2_g, full["w_ffn_in"], full["w_ffn_out"])

    loss_sum, dh2, g_ple_gate, g_ple_proj, vec4 = _b4_ple_loss(
        h2, ps, tgt, ple_norm_g, full["w_ple_gate"], full["w_ple_proj"], final_row)
    dff, dh1, vec3 = _b3_ffn(dh2, h1, ff, norm2_g, full["w_ffn_in"], full["w_ffn_out"])
    g_ffn_in = _wgrad(v, dff, 2 * D_FF // N_SHARDS, "wgrad_ffn_in")
    g_ffn_out = _wgrad(act, dh2, D_MODEL, "wgrad_ffn_out", tokens=WGRAD_TOKENS // 2)

    def stack(n, g):
        return g.reshape(N_SHARDS, g.shape[0] // N_SHARDS, g.shape[1]) if rows_of[n] else g[None]

    def chip_sums_of(group, grads_of, tag):
        stacked = [stack(n, grads_of[n]) for n in group]
        theirs = _core_exchange(stacked, "grad_core_exchange_" + tag)
        return [_sum_cores(g, t, core, "sum_cores_" + n) for g, t, n in zip(stacked, theirs, group)]

    late_rows = [rows_of[n] for n in late]
    late_grads = dict(w_ffn_in=g_ffn_in, w_ffn_out=g_ffn_out, w_ple_gate=g_ple_gate, w_ple_proj=g_ple_proj)
    cx_send, cx_recv, late_stacked, late_theirs, cx_token = _core_exchange_start(
        [stack(n, late_grads[n]) for n in late])
    dzt, dyp, dyl, g_w_out, vec_g = _b2_gates(dh1, zt, yp, yl, b_gate_full + cx_token[0, 0], full["w_out"])
    late_stacked, late_theirs = _core_exchange_wait(cx_send, cx_recv, late_stacked, late_theirs, dzt)
    late_sums = [_sum_cores(g, t, core, "sum_cores_" + n) for g, t, n in zip(late_stacked, late_theirs, late)]
    ex_send, ex_recv, late_sums, late_slots, ex_token = _chip_exchange_start(late_sums, late_rows, "late")
    dzl, dzg, g_lru_proj, g_w_rg, g_w_ig, vec_l = _b2_lru(
        dyl, zl, zg, hs, xc_saved, r_saved, ig_saved, conv_w_full, w_rg_b, w_ig_b, lru_lambda + ex_token[0, 0],
        full["lru_proj"])
    dzp, grad_x, g_pool_proj, g_pool_w, vec_p = _b12_pool_in_proj(
        dyp, zp, dzl, dzg, dzt, xs, dh1, norm1_g, full["w_in"], pool_w_b, pool_scale, full["pool_proj"])
    small_full = dict(
        norm1_g=vec_p[1], b_gate=vec_g[0:2], pool_w=_unpair_blocks(g_pool_w), pool_scale=vec_p[0, :POOL_WIDTH],
        conv_w=vec_l[_V_CONVW:_V_CONVW + CONV_WIDTH], conv_b=vec_l[_V_CONVB], w_rg=_unpair_blocks(g_w_rg),
        b_rg=vec_l[_V_BRG], w_ig=_unpair_blocks(g_w_ig), b_ig=vec_l[_V_BIG], lru_lambda=vec_l[_V_LAM], norm2_g=vec3[0], ple_norm_g=vec4[1],
        final_g=vec4[0])
    full_shapes = [small_full[n].shape for n in small]
    n_full = sum(int(small_full[n].size) for n in small)
    rows_full = -(-n_full // (128 * ROW_TILE)) * ROW_TILE
    sm_send, sm_recv, sm_pack, sm_slots, sm_token = _small_exchange_start(
        _pack_rows([small_full[n] for n in small], rows_full))
    g_w_in = jnp.concatenate([
        _wgrad(u, dzp, POOL_WIDTH, "wgrad_in_pool", after=sm_token),
        _wgrad(u, dzl, D_MODEL, "wgrad_in_lru", after=sm_token),
        _wgrad(u, dzg, D_MODEL, "wgrad_in_gelu", after=sm_token),
        _wgrad(u, dzt, D_MODEL, "wgrad_in_gate", after=sm_token)], axis=1)

    loss = lax.psum(loss_sum[0, 0] * (0.5 / D_MODEL), ("x", "y", "c"))

    early_rows = [rows_of[n] for n in early]
    early_sums = chip_sums_of(early, dict(w_in=g_w_in, pool_proj=g_pool_proj, lru_proj=g_lru_proj, w_out=g_w_out),
                              "early")
    e_send, e_recv, early_sums, early_slots, e_token = _chip_exchange_start(early_sums, early_rows, "early")
    grads, deltas, new_m, new_v = {}, {}, {}, {}

    def finish(group, sums, slots, tag):
        reduced = _core_share([_sum_chips(s, q, rows_of[n], place, "sum_chips_" + n)
                               for s, q, n in zip(sums, slots, group)], tag)
        for n, r in zip(group, reduced):
            g = r.reshape(r.shape[0] * r.shape[1], r.shape[2])
            d, nm, nv = _adamw(weights[n][0], g, m_in[n][0], v_in[n][0], "adamw_" + n)
            grads[n], deltas[n], new_m[n], new_v[n] = g[None], d[None], nm[None], nv[None]

    late_sums, late_slots = _chip_exchange_wait(ex_send, ex_recv, late_sums, late_slots, late_rows, e_token, "late")
    finish(late, late_sums, late_slots, "late")

    sm_pack, sm_slots = _small_exchange_wait(sm_send, sm_recv, sm_pack, sm_slots, e_token)
    device = (4 * lax.axis_index("x") + 2 * lax.axis_index("y") + lax.axis_index("c")).astype(jnp.int32)
    summed = dict(zip(small, _unpack_rows(_sum_small(sm_pack, sm_slots, device), full_shapes)))
    summed["b_gate"] = lax.dynamic_slice_in_dim(summed["b_gate"], shard_j * 256, 256, axis=1)
    summed["conv_w"] = lax.dynamic_slice_in_dim(summed["conv_w"], shard_j * 256, 256, axis=1)
    local_shapes = [weights[n].shape for n in small]
    n_local = sum(int(weights[n].size) for n in small)
    rows_local = -(-n_local // (128 * ROW_TILE)) * ROW_TILE
    packs = [_pack_rows([src[n] for n in small], rows_local) for src in (weights, summed, m_in, v_in)]
    d_s, nm_s, nv_s = _adamw(*packs, "adamw_small")
    for dst, pack in ((grads, packs[1]), (deltas, d_s), (new_m, nm_s), (new_v, nv_s)):
        dst.update(zip(small, _unpack_rows(pack, local_shapes)))

    done = d_s[:SUBLANES]
    for n in late:
        done = done + deltas[n][0, :SUBLANES, :128]
    early_sums, early_slots = _chip_exchange_wait(e_send, e_recv, early_sums, early_slots, early_rows, done, "early")
    finish(early, early_sums, early_slots, "early")

    return (loss, grad_x[None], *[grads[n] for n in names], *[deltas[n] for n in names],
            *[new_m[n] for n in names], *[new_v[n] for n in names])
```

```python
import functools

import jax
import jax.numpy as jnp
from jax import lax
from jax.experimental import pallas as pl
from jax.experimental.pallas import tpu as pltpu

F32 = jnp.float32
BF16 = jnp.bfloat16

D_MODEL = 1024
POOL_WIDTH = 512
POOL_GROUP_DIM = 128
POOL_WINDOWS = (2, 4, 8, 16)
POOL_HALO = 16
LRU_HEADS = 8
LRU_HEAD_DIM = 128
CONV_WIDTH = 4
LRU_C = 8.0
D_FF = 2816
PLE_DIM = 256
RMS_EPS = 1e-6
N_SHARDS = 4
N_DEV = 8

ADAM_LR = 0.001
ADAM_B1 = 0.9
ADAM_B2 = 0.999
ADAM_EPS = 1e-08
ADAM_WD = 0.01
ADAM_STEP = 10

ROW_TILE = 256
WIDE_TILE = 512
WGRAD_TOKENS = 2048
SUBLANES = 8
VMEM_LIMIT = 56 * 1024 * 1024
MESH = pl.DeviceIdType.MESH
ANY = pl.BlockSpec(memory_space=pl.ANY)


def _params(semantics=None):
    return pltpu.CompilerParams(dimension_semantics=semantics, vmem_limit_bytes=VMEM_LIMIT)


def _resident(shape):
    n = len(shape)
    return pl.BlockSpec(shape, lambda *_: (0,) * n, pipeline_mode=pl.Buffered(1))


def _acc(shape):
    n = len(shape)
    return pl.BlockSpec(shape, lambda *_: (0,) * n)


def _rows(tile, cols):
    return pl.BlockSpec((tile, cols), lambda i: (i, 0))


def _rows_rev(tile, cols, n_tiles):
    return pl.BlockSpec((tile, cols), lambda i: (n_tiles - 1 - i, 0))


def _halo_before_rev(rows, cols, tile, n_tiles):
    per = tile // rows
    return pl.BlockSpec((rows, cols), lambda i: (jnp.maximum((n_tiles - 1 - i) * per - 1, 0), 0))


def _nn(a, b):
    return jnp.dot(a, b, preferred_element_type=F32)


def _nt(a, b):
    return lax.dot_general(a, b, (((1,), (1,)), ((), ())), preferred_element_type=F32)


def _tn(a, b):
    return lax.dot_general(a, b, (((0,), (0,)), ((), ())), preferred_element_type=F32)


def _rms(x):
    r = lax.rsqrt(jnp.mean(x * x, axis=-1, keepdims=True) + RMS_EPS)
    return x * r, r


def _rms_bwd(dn, n, r):
    return r * (dn - n * jnp.mean(dn * n, axis=-1, keepdims=True))


def _sigmoid(x):
    return 0.5 * jnp.tanh(0.5 * x) + 0.5


_GELU_C = 0.7978845608028654
_GELU_A = 0.044715


def _gelu(x):
    t = jnp.tanh(_GELU_C * (x + _GELU_A * x * x * x))
    return 0.5 * x * (1.0 + t)


def _gelu_and_grad(x):
    x2 = x * x
    t = jnp.tanh(_GELU_C * (x + _GELU_A * x2 * x))
    cdf = 0.5 * (1.0 + t)
    grad = cdf + 0.5 * x * (1.0 - t * t) * _GELU_C * (1.0 + 3.0 * _GELU_A * x2)
    return x * cdf, grad


def _softplus_neg(lam):
    e = jnp.exp(-jnp.abs(lam))
    sp = jnp.maximum(-lam, 0.0) + jnp.log1p(e)
    return sp, -_sigmoid(-lam)


def _colsum(v):
    return jnp.sum(v, axis=0, keepdims=True)


def _row_ids(shape):
    return lax.broadcasted_iota(jnp.int32, shape, 0)


def _shift_down(cat, k):
    return pltpu.roll(cat, k, 0) if k else cat


def _shift_up(cat, k):
    return pltpu.roll(cat, cat.shape[0] - k, 0) if k else cat


IN_SPLITS = (0, POOL_WIDTH, POOL_WIDTH + D_MODEL, POOL_WIDTH + 2 * D_MODEL, POOL_WIDTH + 4 * D_MODEL)
IN_WIDTHS = tuple(IN_SPLITS[k + 1] - IN_SPLITS[k] for k in range(4))
PROJ_CHUNK = 256
PAIR_DIM = 2 * LRU_HEAD_DIM


def _pair_blocks(w):
    zero = jnp.zeros_like(w[0::2])
    return jnp.concatenate([jnp.concatenate([w[0::2], zero], axis=2), jnp.concatenate([zero, w[1::2]], axis=2)], axis=1)


def _unpair_blocks(w):
    n, d2, _ = w.shape
    d = d2 // 2
    return jnp.stack([w[:, :d, :d], w[:, d:, d:]], axis=1).reshape(2 * n, d, d)


def _no_tick():
    pass


class _Interleaved:
    def __init__(self, pieces):
        self._pieces = iter(pieces)

    def tick(self, n=1):
        for _ in range(n):
            piece = next(self._pieces, None)
            if piece is not None:
                piece()

    def flush(self):
        for piece in self._pieces:
            piece()


def _pool_forward(zp_cat, pw_ref, first_row, tick=_no_tick):
    tt = zp_cat.shape[0] - POOL_HALO
    t_glob = first_row + _row_ids((tt, POOL_GROUP_DIM))
    pooled, mixed = [], []
    for g, w in enumerate(POOL_WINDOWS):
        cat = zp_cat[:, g * POOL_GROUP_DIM:(g + 1) * POOL_GROUP_DIM]
        s, k = cat, 1
        while k < w:
            s = s + _shift_down(s, k)
            k *= 2
        cnt = jnp.minimum(t_glob + 1, w).astype(F32)
        pooled.append(s[POOL_HALO:] / cnt - cat[POOL_HALO:])
        per = pw_ref.shape[-1] // POOL_GROUP_DIM
        if (g + 1) % per == 0:
            block = jnp.concatenate(pooled[-per:], axis=1).astype(BF16)
            mixed.append(_nn(block, pw_ref[g // per]))
        tick()
    return jnp.concatenate(pooled, axis=1), jnp.concatenate(mixed, axis=1)


def _lru_gates(zl_cat, conv_w, conv_b, wrg_ref, brg, wig_ref, big, sp, first_row, tick=_no_tick):
    tt = zl_cat.shape[0] - SUBLANES
    xc = conv_w[CONV_WIDTH - 1:CONV_WIDTH] * zl_cat
    for k in range(1, CONV_WIDTH):
        xc = xc + conv_w[CONV_WIDTH - 1 - k:CONV_WIDTH - k] * _shift_down(zl_cat, k)
        tick()
    xc = xc[SUBLANES:] + conv_b
    xh = xc.astype(BF16)
    pr, pi = [], []
    width = wrg_ref.shape[-1]
    for p in range(D_MODEL // width):
        xs = xh[:, p * width:(p + 1) * width]
        pr.append(_nn(xs, wrg_ref[p]))
        pi.append(_nn(xs, wig_ref[p]))
    r = _sigmoid(jnp.concatenate(pr, axis=1) + brg)
    tick()
    ig = _sigmoid(jnp.concatenate(pi, axis=1) + big)
    tick()
    a, mult = _decay(r, sp, first_row, tick)
    tick()
    return xc, r, ig, a, mult


def _decay(r, sp, first_row, tick=_no_tick):
    a = jnp.exp(-LRU_C * r * sp)
    tick()
    mult = jnp.sqrt(jnp.maximum(1.0 - a * a, 0.0))
    t_glob = first_row + _row_ids(r.shape)
    return a, jnp.where(t_glob == 0, 1.0, mult)


def _f12_mixer(x, norm1_g, w_in, b_gate, pool_w, pool_scale, pool_proj, conv_w, conv_b, w_rg, b_rg, w_ig, b_ig,
               lru_lambda, lru_proj, w_out):
    T = x.shape[0]
    tt = ROW_TILE
    nt = T // tt
    n_groups = tt // SUBLANES
    proj_mid = IN_SPLITS[3] + D_MODEL // 2

    def body(xm_ref, x_ref, g1_ref, win_ref, bg_ref, pw_ref, ps_ref, pp_ref, cw_ref, cb_ref,
             wrg_ref, brg_ref, wig_ref, big_ref, lam_ref, lp_ref, wo_ref,
             zp_ref, zl_ref, zg_ref, zt_ref, u_ref, h1_ref, hs_ref, yp_ref, yl_ref, xc_ref, r_ref, ig_ref,
             zbuf, zp_halo, zl_halo, a_s, b_s, carry_s):
        s = pl.program_id(0)

        @pl.when(s == 0)
        def _():
            zbuf[1] = jnp.zeros((tt, IN_SPLITS[4]), F32)
            zp_halo[...] = jnp.zeros_like(zp_halo)
            zl_halo[...] = jnp.zeros_like(zl_halo)
            carry_s[...] = jnp.zeros_like(carry_s)

        z_new, z_old = zbuf.at[s % 2], zbuf.at[(s + 1) % 2]
        first = s <= 1
        first_row = jnp.maximum(s - 1, 0) * tt

        n1, _ = _rms(xm_ref[...])
        u = (n1 * g1_ref[...]).astype(BF16)
        u_ref[...] = u

        z_refs = (zp_ref, zl_ref, zg_ref, zt_ref)

        def project(lo):
            k = max(i for i in range(4) if IN_SPLITS[i] <= lo)
            part = _nn(u, win_ref[:, lo:lo + PROJ_CHUNK])
            z_new[:, lo:lo + PROJ_CHUNK] = part
            z_refs[k][:, lo - IN_SPLITS[k]:lo - IN_SPLITS[k] + PROJ_CHUNK] = part.astype(z_refs[k].dtype)

        before_scan = _Interleaved(functools.partial(project, lo) for lo in range(0, proj_mid, PROJ_CHUNK))
        after_scan = _Interleaved(functools.partial(project, lo) for lo in range(proj_mid, IN_SPLITS[4], PROJ_CHUNK))

        zp_cat = jnp.concatenate([jnp.where(first, 0.0, zp_halo[...]), z_old[:, IN_SPLITS[0]:IN_SPLITS[1]]], axis=0)
        _, mixed = _pool_forward(zp_cat, pw_ref, first_row, before_scan.tick)
        y_pool = _nn((mixed * ps_ref[...]).astype(BF16), pp_ref[...])

        sp, _ = _softplus_neg(lam_ref[...])
        zl_cat = jnp.concatenate([jnp.where(first, 0.0, zl_halo[...]), z_old[:, IN_SPLITS[1]:IN_SPLITS[2]]], axis=0)
        xc, r, ig, a, mult = _lru_gates(zl_cat, cw_ref[...], cb_ref[...], wrg_ref, brg_ref[...], wig_ref,
                                        big_ref[...], sp, first_row, before_scan.tick)
        a_s[...] = a
        b_s[...] = mult * ig * xc
        xc_ref[...] = xc.astype(BF16)
        r_ref[...] = r.astype(BF16)
        ig_ref[...] = ig.astype(BF16)
        before_scan.flush()

        rows8 = _row_ids((SUBLANES, D_MODEL))

        def group(g, carry):
            at = pl.ds(pl.multiple_of(g * SUBLANES, SUBLANES), SUBLANES)
            A, B = a_s[at, :], b_s[at, :]
            for s in (1, 2, 4):
                m = rows8 >= s
                B = jnp.where(m, A * pltpu.roll(B, s, 0) + B, B)
                A = jnp.where(m, A * pltpu.roll(A, s, 0), A)
            h = A * carry + B
            hs_ref[at, :] = h
            return jnp.broadcast_to(h[SUBLANES - 1:SUBLANES, :], (SUBLANES, D_MODEL))

        carry_s[...] = lax.fori_loop(0, n_groups, group, jnp.where(first, 0.0, carry_s[...]))
        gelu = _gelu(z_old[:, IN_SPLITS[2]:IN_SPLITS[3]])
        after_scan.tick(2)
        y_lru = _nn((hs_ref[...] * gelu).astype(BF16), lp_ref[...])

        gates = _sigmoid(z_old[:, IN_SPLITS[3]:IN_SPLITS[4]] + bg_ref[...])
        after_scan.tick(2)
        merged = gates[:, :D_MODEL] * y_pool + gates[:, D_MODEL:] * y_lru
        after_scan.flush()
        h1_ref[...] = x_ref[...] + _nn(merged.astype(BF16), wo_ref[...])
        yp_ref[...] = y_pool.astype(BF16)
        yl_ref[...] = y_lru.astype(BF16)
        zp_halo[...] = z_old[tt - POOL_HALO:, IN_SPLITS[0]:IN_SPLITS[1]]
        zl_halo[...] = z_old[tt - SUBLANES:, IN_SPLITS[1]:IN_SPLITS[2]]

    def ahead(cols):
        return pl.BlockSpec((tt, cols), lambda s: (jnp.minimum(s, nt - 1), 0))

    def behind(cols):
        return pl.BlockSpec((tt, cols), lambda s: (jnp.maximum(s - 1, 0), 0))

    res = [norm1_g, w_in, b_gate, pool_w, pool_scale, pool_proj, conv_w, conv_b, w_rg, b_rg, w_ig, b_ig, lru_lambda,
           lru_proj, w_out]
    return pl.pallas_call(
        body, name="f12_mixer", grid=(nt + 1,),
        in_specs=[ahead(D_MODEL), behind(D_MODEL)] + [_resident(w.shape) for w in res],
        out_specs=[ahead(w) for w in IN_WIDTHS] + [ahead(D_MODEL)] + [behind(D_MODEL)] * 7,
        out_shape=[jax.ShapeDtypeStruct((T, w), dt) for w, dt in zip(IN_WIDTHS, (F32, F32, F32, BF16))]
        + [jax.ShapeDtypeStruct((T, D_MODEL), BF16), jax.ShapeDtypeStruct((T, D_MODEL), F32),
           jax.ShapeDtypeStruct((T, D_MODEL), F32)] + [jax.ShapeDtypeStruct((T, D_MODEL), BF16)] * 5,
        scratch_shapes=[pltpu.VMEM((2, tt, IN_SPLITS[4]), F32), pltpu.VMEM((POOL_HALO, POOL_WIDTH), F32),
                        pltpu.VMEM((SUBLANES, D_MODEL), F32), pltpu.VMEM((tt, D_MODEL), F32),
                        pltpu.VMEM((tt, D_MODEL), F32), pltpu.VMEM((SUBLANES, D_MODEL), F32)],
        compiler_params=_params(("arbitrary",)),
    )(x, x, *res)


def _f3_ffn(h1, norm2_g, w_ffn_in, w_ffn_out):
    T = h1.shape[0]
    tm = ROW_TILE

    def body(h_ref, g_ref, wi_ref, wo_ref, h2_ref, v_ref, ff_ref, act_ref):
        h = h_ref[...]
        n, _ = _rms(h)
        v = (n * g_ref[...]).astype(BF16)
        v_ref[...] = v
        g_ff = _nn(v, wi_ref[:, :D_FF])
        u_ff = _nn(v, wi_ref[:, D_FF:])
        ff_ref[:, :D_FF] = g_ff.astype(BF16)
        ff_ref[:, D_FF:] = u_ff.astype(BF16)
        act = (g_ff * _sigmoid(g_ff) * u_ff).astype(BF16)
        act_ref[...] = act
        h2_ref[...] = h + _nn(act, wo_ref[...])

    return pl.pallas_call(
        body, name="f3_ffn", grid=(T // tm,),
        in_specs=[_rows(tm, D_MODEL), _resident((1, D_MODEL)), _resident(w_ffn_in.shape), _resident(w_ffn_out.shape)],
        out_specs=[_rows(tm, D_MODEL), _rows(tm, D_MODEL), _rows(tm, 2 * D_FF), _rows(tm, D_FF)],
        out_shape=[jax.ShapeDtypeStruct((T, D_MODEL), F32), jax.ShapeDtypeStruct((T, D_MODEL), BF16),
                   jax.ShapeDtypeStruct((T, 2 * D_FF), BF16), jax.ShapeDtypeStruct((T, D_FF), BF16)],
        compiler_params=_params(("arbitrary",)),
    )(h1, norm2_g, w_ffn_in, w_ffn_out)


def _b4_ple_loss(h2, p, target, ple_norm_g, w_ple_gate, w_ple_proj, final_g):
    T = h2.shape[0]
    tm = WIDE_TILE

    def body(h_ref, p_ref, t_ref, gp_ref, wg_ref, wp_ref, gf_ref, loss_ref, dh2_ref, dwg_ref, dwp_ref, vec_ref):
        @pl.when(pl.program_id(0) == 0)
        def _():
            loss_ref[...] = jnp.zeros_like(loss_ref)
            dwg_ref[...] = jnp.zeros_like(dwg_ref)
            dwp_ref[...] = jnp.zeros_like(dwp_ref)
            vec_ref[...] = jnp.zeros_like(vec_ref)

        h2v = h_ref[...]
        n3, r3 = _rms(h2v)
        n3g = (n3 * gp_ref[...]).astype(BF16)
        pg = _sigmoid(_nn(n3g, wg_ref[...]))
        pb = p_ref[...].astype(BF16)
        e = _nn(pb, wp_ref[...])
        h3 = h2v + pg * e
        n4, r4 = _rms(h3)
        diff = n4 * gf_ref[...] - t_ref[...]
        loss_ref[...] += jnp.sum(diff * diff).reshape(1, 1)
        dy = diff * (1.0 / D_MODEL)
        vec_ref[0:1, :] += _colsum(dy * n4)
        dh3 = _rms_bwd(dy * gf_ref[...], n4, r4)
        dwp_ref[...] += _tn(pb, (dh3 * pg).astype(BF16))
        dq = (dh3 * e * pg * (1.0 - pg)).astype(BF16)
        dwg_ref[...] += _tn(n3g, dq)
        dn3g = _nt(dq, wg_ref[...])
        vec_ref[1:2, :] += _colsum(dn3g * n3)
        dh2_ref[...] = dh3 + _rms_bwd(dn3g * gp_ref[...], n3, r3)

    return pl.pallas_call(
        body, name="b4_ple_loss", grid=(T // tm,),
        in_specs=[_rows(tm, D_MODEL), _rows(tm, PLE_DIM), _rows(tm, D_MODEL), _resident((1, D_MODEL)),
                  _resident(w_ple_gate.shape), _resident(w_ple_proj.shape), _resident((1, D_MODEL))],
        out_specs=[_acc((1, 1)), _rows(tm, D_MODEL), _acc(w_ple_gate.shape), _acc(w_ple_proj.shape),
                   _acc((SUBLANES, D_MODEL))],
        out_shape=[jax.ShapeDtypeStruct((1, 1), F32), jax.ShapeDtypeStruct((T, D_MODEL), F32),
                   jax.ShapeDtypeStruct(w_ple_gate.shape, F32), jax.ShapeDtypeStruct(w_ple_proj.shape, F32),
                   jax.ShapeDtypeStruct((SUBLANES, D_MODEL), F32)],
        compiler_params=_params(("arbitrary",)),
    )(h2, p, target, ple_norm_g, w_ple_gate, w_ple_proj, final_g)


def _b3_ffn(dh2, h1, ff, norm2_g, w_ffn_in, w_ffn_out):
    T = h1.shape[0]
    tm = ROW_TILE

    def body(d_ref, h_ref, ff_ref, g_ref, wi_ref, wo_ref, dff_ref, dh1_ref, vec_ref):
        @pl.when(pl.program_id(0) == 0)
        def _():
            vec_ref[...] = jnp.zeros_like(vec_ref)

        dh2v = d_ref[...]
        dact = _nt(dh2v.astype(BF16), wo_ref[...])
        g_ff = ff_ref[:, :D_FF].astype(F32)
        u_ff = ff_ref[:, D_FF:].astype(F32)
        s = _sigmoid(g_ff)
        dg = (dact * u_ff * (s * (1.0 + g_ff * (1.0 - s)))).astype(BF16)
        du = (dact * (g_ff * s)).astype(BF16)
        dff_ref[:, :D_FF] = dg
        dff_ref[:, D_FF:] = du
        dv = _nt(dg, wi_ref[:, :D_FF]) + _nt(du, wi_ref[:, D_FF:])
        n2, r2 = _rms(h_ref[...])
        vec_ref[0:1, :] += _colsum(dv * n2)
        dh1_ref[...] = dh2v + _rms_bwd(dv * g_ref[...], n2, r2)

    return pl.pallas_call(
        body, name="b3_ffn", grid=(T // tm,),
        in_specs=[_rows(tm, D_MODEL), _rows(tm, D_MODEL), _rows(tm, 2 * D_FF), _resident((1, D_MODEL)),
                  _resident(w_ffn_in.shape), _resident(w_ffn_out.shape)],
        out_specs=[_rows(tm, 2 * D_FF), _rows(tm, D_MODEL), _acc((SUBLANES, D_MODEL))],
        out_shape=[jax.ShapeDtypeStruct((T, 2 * D_FF), BF16), jax.ShapeDtypeStruct((T, D_MODEL), F32),
                   jax.ShapeDtypeStruct((SUBLANES, D_MODEL), F32)],
        compiler_params=_params(("arbitrary",)),
    )(dh2, h1, ff, norm2_g, w_ffn_in, w_ffn_out)


def _wgrad(a, b, col_tile, name, tokens=WGRAD_TOKENS, after=None):
    T, K = a.shape
    N = b.shape[1]
    tk = min(T, tokens)

    def body(a_ref, b_ref, *rest):
        o_ref = rest[-1]

        @pl.when(pl.program_id(1) == 0)
        def _():
            o_ref[...] = jnp.zeros_like(o_ref)

        o_ref[...] += _tn(a_ref[...].astype(BF16), b_ref[...].astype(BF16))

    return pl.pallas_call(
        body, name=name, grid=(N // col_tile, T // tk),
        in_specs=[pl.BlockSpec((tk, K), lambda j, k: (k, 0)), pl.BlockSpec((tk, col_tile), lambda j, k: (k, j))]
        + ([] if after is None else [ANY]),
        out_specs=pl.BlockSpec((K, col_tile), lambda j, k: (0, j)),
        out_shape=jax.ShapeDtypeStruct((K, N), F32),
        compiler_params=_params(("arbitrary", "arbitrary")),
    )(a, b, *([] if after is None else [after]))


def _b2_gates(dh1, zt, yp, yl, b_gate, w_out):
    T = dh1.shape[0]
    tm = WIDE_TILE

    def body(d_ref, zt_ref, yp_ref, yl_ref, bg_ref, wo_ref, dzt_ref, dyp_ref, dyl_ref, dwo_ref, vec_ref):
        @pl.when(pl.program_id(0) == 0)
        def _():
            dwo_ref[...] = jnp.zeros_like(dwo_ref)
            vec_ref[...] = jnp.zeros_like(vec_ref)

        db = d_ref[...].astype(BF16)
        dm = _nt(db, wo_ref[...])
        gates = _sigmoid(zt_ref[...].astype(F32) + bg_ref[...])
        g0, g1 = gates[:, :D_MODEL], gates[:, D_MODEL:]
        y_pool, y_lru = yp_ref[...].astype(F32), yl_ref[...].astype(F32)
        dwo_ref[...] += _tn((g0 * y_pool + g1 * y_lru).astype(BF16), db)
        dz0 = dm * y_pool * g0 * (1.0 - g0)
        dz1 = dm * y_lru * g1 * (1.0 - g1)
        vec_ref[0:1, :] += _colsum(dz0)
        vec_ref[1:2, :] += _colsum(dz1)
        dzt_ref[:, :D_MODEL] = dz0.astype(BF16)
        dzt_ref[:, D_MODEL:] = dz1.astype(BF16)
        dyp_ref[...] = (dm * g0).astype(BF16)
        dyl_ref[...] = (dm * g1).astype(BF16)

    return pl.pallas_call(
        body, name="b2_gates", grid=(T // tm,),
        in_specs=[_rows(tm, D_MODEL), _rows(tm, 2 * D_MODEL), _rows(tm, D_MODEL), _rows(tm, D_MODEL),
                  _resident(b_gate.shape), _resident(w_out.shape)],
        out_specs=[_rows(tm, 2 * D_MODEL), _rows(tm, D_MODEL), _rows(tm, D_MODEL), _acc(w_out.shape),
                   _acc((SUBLANES, D_MODEL))],
        out_shape=[jax.ShapeDtypeStruct((T, 2 * D_MODEL), BF16), jax.ShapeDtypeStruct((T, D_MODEL), BF16),
                   jax.ShapeDtypeStruct((T, D_MODEL), BF16), jax.ShapeDtypeStruct(w_out.shape, F32),
                   jax.ShapeDtypeStruct((SUBLANES, D_MODEL), F32)],
        compiler_params=_params(("arbitrary",)),
    )(dh1, zt, yp, yl, b_gate, w_out)


def _b12_pool_in_proj(dyp, zp, dzl, dzg, dzt, x, dh1, norm1_g, w_in, pool_w, pool_scale, pool_proj):
    T = zp.shape[0]
    tt = ROW_TILE
    nt = T // tt

    def body(dy_ref, zp_ref, zph_ref, dzl_ref, dzg_ref, dzt_ref, x_ref, dh_ref, g1_ref, win_ref, pw_ref, ps_ref, pp_ref,
             dzp_ref, dx_ref, dpp_ref, dpw_ref, vec_ref, q_next):
        i = pl.program_id(0)
        ti = nt - 1 - i
        first_row = ti * tt

        @pl.when(i == 0)
        def _():
            dpp_ref[...] = jnp.zeros_like(dpp_ref)
            dpw_ref[...] = jnp.zeros_like(dpw_ref)
            vec_ref[...] = jnp.zeros_like(vec_ref)
            q_next[...] = jnp.zeros_like(q_next)

        du_parts = []

        def project(lo):
            k = max(i for i in range(4) if IN_SPLITS[i] <= lo)
            dz_ref = (None, dzl_ref, dzg_ref, dzt_ref)[k]
            at = lo - IN_SPLITS[k]
            part = _nt(dz_ref[:, at:at + PROJ_CHUNK], win_ref[:, lo:lo + PROJ_CHUNK])
            du_parts[:] = [part if not du_parts else du_parts[0] + part]

        mxu = _Interleaved(functools.partial(project, lo) for lo in range(IN_SPLITS[1], IN_SPLITS[4], PROJ_CHUNK))

        keep = (ti > 0).astype(F32)
        zp_cat = jnp.concatenate([zph_ref[...] * keep, zp_ref[...]], axis=0)
        pooled, mixed = _pool_forward(zp_cat, pw_ref, first_row, mxu.tick)
        dy = dy_ref[...]
        dpp_ref[...] += _tn((mixed * ps_ref[...]).astype(BF16), dy)
        mxu.tick(2)
        dms = _nt(dy, pp_ref[...])
        mxu.tick(2)
        vec_ref[0:1, :POOL_WIDTH] += _colsum(dms * mixed)
        dmixed = (dms * ps_ref[...]).astype(BF16)
        t_glob = first_row + _row_ids((tt, POOL_GROUP_DIM))
        dz, q_all, dpooled_pairs = [], [], []
        for p in range(len(POOL_WINDOWS) // 2):
            pair = slice(p * PAIR_DIM, (p + 1) * PAIR_DIM)
            dpw_ref[p] += _tn(pooled[:, pair].astype(BF16), dmixed[:, pair])
            dpooled_pairs.append(_nt(dmixed[:, pair], pw_ref[p]))
        dpooled_all = jnp.concatenate(dpooled_pairs, axis=1)
        for g, w in enumerate(POOL_WINDOWS):
            cols = slice(g * POOL_GROUP_DIM, (g + 1) * POOL_GROUP_DIM)
            dpooled = dpooled_all[:, cols]
            q = dpooled / jnp.minimum(t_glob + 1, w).astype(F32)
            q_all.append(q)
            s, k = jnp.concatenate([q, q_next[:, cols]], axis=0), 1
            while k < w:
                s = s + _shift_up(s, k)
                k *= 2
            dz.append(s[:tt] - dpooled)
            mxu.tick(2)
        dzp = jnp.concatenate(dz, axis=1).astype(BF16)
        dzp_ref[...] = dzp
        q_next[...] = jnp.concatenate([q[:POOL_HALO] for q in q_all], axis=1)
        mxu.flush()

        du = du_parts[0] + _nt(dzp, win_ref[:, IN_SPLITS[0]:IN_SPLITS[1]])
        n1, r1 = _rms(x_ref[...])
        vec_ref[1:2, :] += _colsum(du * n1)
        dx_ref[...] = dh_ref[...] + _rms_bwd(du * g1_ref[...], n1, r1)

    rev = functools.partial(_rows_rev, n_tiles=nt)
    res = [norm1_g, w_in, pool_w, pool_scale, pool_proj]
    return pl.pallas_call(
        body, name="b12_pool_in_proj", grid=(nt,),
        in_specs=[rev(tt, D_MODEL), rev(tt, POOL_WIDTH), _halo_before_rev(POOL_HALO, POOL_WIDTH, tt, nt),
                  rev(tt, D_MODEL), rev(tt, D_MODEL), rev(tt, 2 * D_MODEL), rev(tt, D_MODEL), rev(tt, D_MODEL)]
        + [_resident(w.shape) for w in res],
        out_specs=[rev(tt, POOL_WIDTH), rev(tt, D_MODEL), _acc(pool_proj.shape), _acc(pool_w.shape),
                   _acc((SUBLANES, D_MODEL))],
        out_shape=[jax.ShapeDtypeStruct((T, POOL_WIDTH), BF16), jax.ShapeDtypeStruct((T, D_MODEL), F32),
                   jax.ShapeDtypeStruct(pool_proj.shape, F32), jax.ShapeDtypeStruct(pool_w.shape, F32),
                   jax.ShapeDtypeStruct((SUBLANES, D_MODEL), F32)],
        scratch_shapes=[pltpu.VMEM((POOL_HALO, POOL_WIDTH), F32)],
        compiler_params=_params(("arbitrary",)),
    )(dyp, zp, zp, dzl, dzg, dzt, x, dh1, *res)


_V_CONVW, _V_CONVB, _V_BRG, _V_BIG, _V_LAM = 0, 4, 5, 6, 7


def _b2_lru(dyl, zl, zg, hs, xc_saved, r_saved, ig_saved, conv_w, w_rg, w_ig, lru_lambda, lru_proj):
    T = zl.shape[0]
    tt = ROW_TILE
    nt = T // tt
    n_groups = tt // SUBLANES

    def body(dy_ref, zl_ref, zlh_ref, zg_ref, hs_ref, hsh_ref, xc_ref, r_ref, ig_ref, cw_ref, wrg_ref, wig_ref,
             lam_ref, lp_ref, dzl_ref, dzg_ref, dlp_ref, dwrg_ref, dwig_ref, vec_ref,
             c_s, d_s, g_s, g_next, a_next, dxc_next):
        i = pl.program_id(0)
        ti = nt - 1 - i
        first_row = ti * tt

        @pl.when(i == 0)
        def _():
            dlp_ref[...] = jnp.zeros_like(dlp_ref)
            dwrg_ref[...] = jnp.zeros_like(dwrg_ref)
            dwig_ref[...] = jnp.zeros_like(dwig_ref)
            vec_ref[...] = jnp.zeros_like(vec_ref)
            g_next[...] = jnp.zeros_like(g_next)
            a_next[...] = jnp.zeros_like(a_next)
            dxc_next[...] = jnp.zeros_like(dxc_next)

        keep = (ti > 0).astype(F32)
        sp, dsp_dlam = _softplus_neg(lam_ref[...])
        hs = hs_ref[...]
        gelu, dgelu = _gelu_and_grad(zg_ref[...])
        dy = dy_ref[...]
        dlp_ref[...] += _tn((hs * gelu).astype(BF16), dy)
        dyl = _nt(dy, lp_ref[...])
        dzg_ref[...] = (dyl * hs * dgelu).astype(BF16)

        d_s[...] = dyl * gelu
        a_tile = jnp.exp(-LRU_C * r_ref[...].astype(F32) * sp)
        c_s[...] = _shift_up(jnp.concatenate([a_tile, a_next[...]], axis=0), 1)[:tt]
        a_next[...] = jnp.broadcast_to(a_tile[0:1, :], (SUBLANES, D_MODEL))
        rows8 = _row_ids((SUBLANES, D_MODEL))

        def group(k, carry):
            at = pl.ds(pl.multiple_of((n_groups - 1 - k) * SUBLANES, SUBLANES), SUBLANES)
            C, Dv = c_s[at, :], d_s[at, :]
            for s in (1, 2, 4):
                m = rows8 < SUBLANES - s
                Dv = jnp.where(m, C * pltpu.roll(Dv, SUBLANES - s, 0) + Dv, Dv)
                C = jnp.where(m, C * pltpu.roll(C, SUBLANES - s, 0), C)
            G = C * carry + Dv
            g_s[at, :] = G
            return jnp.broadcast_to(G[0:1, :], (SUBLANES, D_MODEL))

        g_next[...] = lax.fori_loop(0, n_groups, group, g_next[...])
        G = g_s[...]

        cw = cw_ref[...]
        zl_cat = jnp.concatenate([zlh_ref[...] * keep, zl_ref[...]], axis=0)
        xc, r, ig = xc_ref[...].astype(F32), r_ref[...].astype(F32), ig_ref[...].astype(F32)
        a, mult = _decay(r, sp, first_row)
        h_prev = _shift_down(jnp.concatenate([hsh_ref[...] * keep, hs_ref[...]], axis=0), 1)[SUBLANES:]
        t_glob = first_row + _row_ids((tt, D_MODEL))
        dmult = jnp.where(t_glob == 0, 0.0, G * ig * xc)
        dla = G * h_prev * a - dmult * (a * a) / mult
        vec_ref[_V_LAM:_V_LAM + 1, :] += _colsum(dla * r) * (-LRU_C) * dsp_dlam
        dpr = dla * (-LRU_C) * sp * r * (1.0 - r)
        dpi = G * mult * xc * ig * (1.0 - ig)
        vec_ref[_V_BRG:_V_BRG + 1, :] += _colsum(dpr)
        vec_ref[_V_BIG:_V_BIG + 1, :] += _colsum(dpi)
        dprb, dpib, xh = dpr.astype(BF16), dpi.astype(BF16), xc_ref[...]
        dxc_h = []
        for p in range(LRU_HEADS // 2):
            cols = slice(p * PAIR_DIM, (p + 1) * PAIR_DIM)
            dwrg_ref[p] += _tn(xh[:, cols], dprb[:, cols])
            dwig_ref[p] += _tn(xh[:, cols], dpib[:, cols])
            dxc_h.append(_nt(dprb[:, cols], wrg_ref[p]) + _nt(dpib[:, cols], wig_ref[p]))
        dxc = G * mult * ig + jnp.concatenate(dxc_h, axis=1)

        vec_ref[_V_CONVB:_V_CONVB + 1, :] += _colsum(dxc)
        dxc_cat = jnp.concatenate([dxc, dxc_next[...]], axis=0)
        dzl = cw[CONV_WIDTH - 1:CONV_WIDTH] * dxc
        for k in range(CONV_WIDTH):
            lag = CONV_WIDTH - 1 - k
            vec_ref[_V_CONVW + k:_V_CONVW + k + 1, :] += _colsum(dxc * _shift_down(zl_cat, lag)[SUBLANES:])
            if lag:
                dzl = dzl + cw[k:k + 1] * _shift_up(dxc_cat, lag)[:tt]
        dzl_ref[...] = dzl.astype(BF16)
        dxc_next[...] = dxc[:SUBLANES]

    res = [conv_w, w_rg, w_ig, lru_lambda, lru_proj]
    return pl.pallas_call(
        body, name="b2_lru", grid=(nt,),
        in_specs=[_rows_rev(tt, D_MODEL, nt), _rows_rev(tt, D_MODEL, nt), _halo_before_rev(SUBLANES, D_MODEL, tt, nt),
                  _rows_rev(tt, D_MODEL, nt), _rows_rev(tt, D_MODEL, nt), _halo_before_rev(SUBLANES, D_MODEL, tt, nt)]
        + [_rows_rev(tt, D_MODEL, nt)] * 3 + [_resident(w.shape) for w in res],
        out_specs=[_rows_rev(tt, D_MODEL, nt), _rows_rev(tt, D_MODEL, nt), _acc(lru_proj.shape), _acc(w_rg.shape),
                   _acc(w_ig.shape), _acc((SUBLANES, D_MODEL))],
        out_shape=[jax.ShapeDtypeStruct((T, D_MODEL), BF16), jax.ShapeDtypeStruct((T, D_MODEL), BF16),
                   jax.ShapeDtypeStruct(lru_proj.shape, F32), jax.ShapeDtypeStruct(w_rg.shape, F32),
                   jax.ShapeDtypeStruct(w_ig.shape, F32), jax.ShapeDtypeStruct((SUBLANES, D_MODEL), F32)],
        scratch_shapes=[pltpu.VMEM((tt, D_MODEL), F32)] * 3 + [pltpu.VMEM((SUBLANES, D_MODEL), F32)] * 3,
        compiler_params=_params(("arbitrary",)),
    )(dyl, zl, zl, zg, hs, hs, xc_saved, r_saved, ig_saved, *res)


def _row_tile(rows):
    for t in (512, 256, 128, 64, 32, 16, 8):
        if rows % t == 0:
            return t
    return rows


def _scalar_grid(grid, in_specs, out_specs):
    return pltpu.PrefetchScalarGridSpec(num_scalar_prefetch=1, grid=grid, in_specs=in_specs, out_specs=out_specs)


def _cast_into_block(w, by_rows, shard_j, name):
    R, C = w.shape
    tr = _row_tile(R)
    if by_rows:
        out_shape, out_map = (N_SHARDS * R, C), lambda i, j: (j[0] * (R // tr) + i, 0)
    else:
        out_shape, out_map = (R, N_SHARDS * C), lambda i, j: (i, j[0])

    def body(j_ref, w_ref, o_ref):
        o_ref[...] = w_ref[...].astype(BF16)

    return pl.pallas_call(
        body, name=name,
        grid_spec=_scalar_grid((R // tr,), [pl.BlockSpec((tr, C), lambda i, j: (i, 0))], pl.BlockSpec((tr, C), out_map)),
        out_shape=jax.ShapeDtypeStruct(out_shape, BF16),
        compiler_params=_params(("arbitrary",)),
    )(shard_j.reshape(1), w)


def _sum_cores(g, theirs, core, name):
    S, R, C = g.shape
    H = R // 2
    tr = _row_tile(H)
    nh = H // tr

    def body(c_ref, g_ref, t_ref, o_ref):
        o_ref[...] = (g_ref[...] + t_ref[...]).astype(BF16)

    half = pl.BlockSpec((None, tr, C), lambda s, i, c: (s, i, 0))
    return pl.pallas_call(
        body, name=name,
        grid_spec=_scalar_grid((S, nh), [pl.BlockSpec((None, tr, C), lambda s, i, c: (s, c[0] * nh + i, 0)), half], half),
        out_shape=jax.ShapeDtypeStruct((S, H, C), BF16),
        compiler_params=_params(("arbitrary", "arbitrary")),
    )(core.reshape(1), g, theirs)


def _sum_chips(sums, slots, by_rows, place, name):
    _, H, C = slots.shape
    tr = _row_tile(H)
    own_map = (lambda i, p: (p[0], i, 0)) if by_rows else (lambda i, p: (0, i, p[0]))

    def body(p_ref, s_ref, q_ref, o_ref):
        o_ref[...] = ((s_ref[...].astype(F32) + q_ref[0].astype(F32)) + q_ref[1].astype(F32)) + q_ref[2].astype(F32)

    return pl.pallas_call(
        body, name=name,
        grid_spec=_scalar_grid(
            (H // tr,),
            [pl.BlockSpec((None, tr, C), own_map), pl.BlockSpec((3, tr, C), lambda i, p: (0, i, 0))],
            pl.BlockSpec((None, tr, C), lambda i, p: (p[1], i, 0))),
        out_shape=jax.ShapeDtypeStruct((2, H, C), F32),
        compiler_params=_params(("arbitrary",)),
    )(place, sums, slots)


def _adamw(w, g, m, v, name):
    R, C = w.shape
    tr = _row_tile(R)
    c1 = 1.0 - ADAM_B1 ** ADAM_STEP
    c2 = 1.0 - ADAM_B2 ** ADAM_STEP

    def body(w_ref, g_ref, m_ref, v_ref, d_ref, nm_ref, nv_ref):
        gv = g_ref[...]
        nm = ADAM_B1 * m_ref[...] + (1.0 - ADAM_B1) * gv
        nv = ADAM_B2 * v_ref[...] + (1.0 - ADAM_B2) * (gv * gv)
        d_ref[...] = -ADAM_LR * ((nm / c1) / (jnp.sqrt(nv / c2) + ADAM_EPS) + ADAM_WD * w_ref[...])
        nm_ref[...] = nm
        nv_ref[...] = nv

    return pl.pallas_call(
        body, name=name, grid=(R // tr,),
        in_specs=[_rows(tr, C)] * 4, out_specs=[_rows(tr, C)] * 3,
        out_shape=[jax.ShapeDtypeStruct((R, C), F32)] * 3,
        compiler_params=_params(("arbitrary",)),
    )(w, g, m, v)


def _place():
    return lax.axis_index("x"), lax.axis_index("y"), lax.axis_index("c")


def _other_chips(x, y):
    return [(1 - x, y), (x, 1 - y), (1 - x, 1 - y)]


def _shard_block(ref, by_rows, R, C, j, half_rows=None):
    if half_rows is None:
        rows, r0 = R, 0
    else:
        rows = R // 2
        r0 = pl.multiple_of(half_rows * rows, 16)
    if by_rows:
        return ref.at[pl.ds(pl.multiple_of(j * R, 16) + r0, rows), :]
    return ref.at[pl.ds(r0, rows), pl.ds(pl.multiple_of(j * C, 128), C)]


def _all_gather_weights(gathered, shapes, by_rows, small):
    n = len(gathered)

    def body(*refs):
        small_in = refs[n]
        outs, small_out = refs[n + 1:2 * n + 1], refs[2 * n + 1]
        send_sems, recv_sems, local_sem = refs[2 * n + 2:]
        x, y, c = _place()
        me_j = 2 * x + y
        chips = _other_chips(x, y)
        sibling = (x, y, 1 - c)

        def block(i, j, half):
            R, C = shapes[i]
            return _shard_block(outs[i], by_rows[i], R, C, j, half)

        def ici(i, k, src_j):
            return pltpu.make_async_remote_copy(
                src_ref=block(i, src_j, c), dst_ref=block(i, src_j, c),
                send_sem=send_sems.at[6 * i + k], recv_sem=recv_sems.at[6 * i + k],
                device_id=(*chips[k], c), device_id_type=MESH)

        def relay(i, k, half):
            kj = 2 * chips[k][0] + chips[k][1]
            return pltpu.make_async_remote_copy(
                src_ref=block(i, kj, half), dst_ref=block(i, kj, half),
                send_sem=send_sems.at[6 * i + 3 + k], recv_sem=recv_sems.at[6 * i + 3 + k],
                device_id=sibling, device_id_type=MESH)

        def small_copy(k, src_j):
            cols = pl.ds(pl.multiple_of(src_j * 256, 128), 256)
            return pltpu.make_async_remote_copy(
                src_ref=small_in, dst_ref=small_out.at[:, cols],
                send_sem=send_sems.at[6 * n + k], recv_sem=recv_sems.at[6 * n + k],
                device_id=(*chips[k], c), device_id_type=MESH)

        sends = []
        for i in range(n):
            for k in range(3):
                cp = ici(i, k, me_j)
                cp.start()
                sends.append(cp)
        for k in range(3):
            cp = small_copy(k, me_j)
            cp.start()
            sends.append(cp)
        local = pltpu.make_async_copy(small_in, small_out.at[:, pl.ds(pl.multiple_of(me_j * 256, 128), 256)], local_sem)
        local.start()
        for i in range(n):
            for k in range(3):
                kj = 2 * chips[k][0] + chips[k][1]
                ici(i, k, kj).wait_recv()
                cp = relay(i, k, c)
                cp.start()
                sends.append(cp)
        for k in range(3):
            small_copy(k, 2 * chips[k][0] + chips[k][1]).wait_recv()
        for i in range(n):
            for k in range(3):
                relay(i, k, 1 - c).wait_recv()
        for cp in sends:
            cp.wait_send()
        local.wait()

    out_shape = [jax.ShapeDtypeStruct(g.shape, BF16) for g in gathered]
    out_shape.append(jax.ShapeDtypeStruct((8, N_SHARDS * 256), F32))
    n_sems = 6 * n + 3
    return pl.pallas_call(
        body, name="all_gather_weights",
        in_specs=[ANY] * (n + 1), out_specs=[ANY] * (n + 1), out_shape=out_shape,
        input_output_aliases={i: i for i in range(n)},
        scratch_shapes=[pltpu.SemaphoreType.DMA((n_sems,)), pltpu.SemaphoreType.DMA((n_sems,)),
                        pltpu.SemaphoreType.DMA],
    )(*gathered, small)


def _core_exchange(grads, name):
    n = len(grads)

    def body(*refs):
        copies = _core_exchange_copies(refs[:n], refs[n:2 * n], refs[2 * n], refs[2 * n + 1])
        for cp in copies:
            cp.start()
        for cp in copies:
            cp.wait()

    return pl.pallas_call(
        body, name=name,
        in_specs=[ANY] * n, out_specs=[ANY] * n,
        out_shape=[jax.ShapeDtypeStruct((g.shape[0], g.shape[1] // 2, g.shape[2]), F32) for g in grads],
        scratch_shapes=[pltpu.SemaphoreType.DMA((n,))] * 2,
    )(*grads)


HBM = pl.BlockSpec(memory_space=pltpu.HBM)
SEM = pl.BlockSpec(memory_space=pltpu.SEMAPHORE)
TOKEN = jax.ShapeDtypeStruct((SUBLANES, 128), F32)


def _in_hbm(a):
    return pltpu.with_memory_space_constraint(a, pltpu.HBM)


def _split_params():
    return pltpu.CompilerParams(has_side_effects=pltpu.SideEffectType.DATAFLOW_SIDE_EFFECTING)


def _gather_rest_copies(refs, shapes, by_rows, send_sems, recv_sems):
    x, y, c = _place()
    me_j = 2 * x + y
    chips = _other_chips(x, y)
    pairs = []
    for i, ref in enumerate(refs):
        R, C = shapes[i]
        for k in range(3):
            kj = 2 * chips[k][0] + chips[k][1]

            def copy(j, ref=ref, i=i, k=k, R=R, C=C):
                blk = _shard_block(ref, by_rows[i], R, C, j)
                return pltpu.make_async_remote_copy(
                    src_ref=blk, dst_ref=blk, send_sem=send_sems.at[3 * i + k], recv_sem=recv_sems.at[3 * i + k],
                    device_id=(*chips[k], c), device_id_type=MESH)

            pairs.append((copy(me_j), copy(kj)))
    return pairs


def _gather_rest_start(gathered, shapes, by_rows, after):
    n = len(gathered)

    def body(*refs):
        ins = refs[:n]
        send_sems, recv_sems = refs[n + 1], refs[n + 2]
        token = refs[-1]
        for mine, _ in _gather_rest_copies(ins, shapes, by_rows, send_sems, recv_sems):
            mine.start()
        token[...] = jnp.zeros_like(token)

    out = pl.pallas_call(
        body, name="gather_rest_start",
        out_shape=(pltpu.SemaphoreType.DMA((3 * n,)), pltpu.SemaphoreType.DMA((3 * n,)),
                   *[pltpu.HBM(g.shape, g.dtype) for g in gathered], TOKEN),
        in_specs=[HBM] * n + [ANY], out_specs=(SEM, SEM, *[HBM] * n, pl.BlockSpec(memory_space=pltpu.VMEM)),
        input_output_aliases={i: 2 + i for i in range(n)},
        compiler_params=_split_params(),
    )(*[_in_hbm(g) for g in gathered], after)
    return out[0], out[1], out[2:2 + n], out[-1]


def _gather_rest_wait(send_sems, recv_sems, gathered, shapes, by_rows, after):
    n = len(gathered)

    def body(*refs):
        ins = refs[:n]
        send, recv = refs[n], refs[n + 1]
        for mine, theirs in _gather_rest_copies(ins, shapes, by_rows, send, recv):
            mine.wait_send()
            theirs.wait_recv()

    return pl.pallas_call(
        body, name="gather_rest_wait",
        out_shape=tuple(pltpu.HBM(g.shape, g.dtype) for g in gathered),
        in_specs=[HBM] * n + [SEM, SEM, ANY], out_specs=tuple([HBM] * n),
        input_output_aliases={i: i for i in range(n)},
        compiler_params=_split_params(),
    )(*gathered, send_sems, recv_sems, after)


def _chip_exchange_copies(ins, slots, dims, by_rows, send_sems, recv_sems):
    x, y, c = _place()
    chips = _other_chips(x, y)
    pairs = []
    for i in range(len(ins)):
        for k in range(3):
            kj = 2 * chips[k][0] + chips[k][1]
            if by_rows[i]:
                src = ins[i].at[kj]
            else:
                src = ins[i].at[0, :, pl.ds(pl.multiple_of(kj * dims[i][1], 128), dims[i][1])]
            cp = pltpu.make_async_remote_copy(
                src_ref=src, dst_ref=slots[i].at[k], send_sem=send_sems.at[3 * i + k], recv_sem=recv_sems.at[3 * i + k],
                device_id=(*chips[k], c), device_id_type=MESH)
            pairs.append((cp, cp))
    return pairs


def _exchange_dims(sums, by_rows):
    return [(s.shape[1], s.shape[2]) if by_rows[i] else (s.shape[1], s.shape[2] // N_SHARDS) for i, s in enumerate(sums)]


def _chip_exchange_start(sums, by_rows, tag):
    n = len(sums)
    sums = list(sums)
    dims = _exchange_dims(sums, by_rows)
    slots = [lax.empty((3, h, cc), BF16) for h, cc in dims]

    def body(*refs):
        ins, land = refs[:n], refs[n:2 * n]
        send_sems, recv_sems = refs[2 * n], refs[2 * n + 1]
        token = refs[-1]
        for cp, _ in _chip_exchange_copies(ins, land, dims, by_rows, send_sems, recv_sems):
            cp.start()
        token[...] = jnp.zeros_like(token)

    out = pl.pallas_call(
        body, name="grad_chip_exchange_start_" + tag,
        out_shape=(pltpu.SemaphoreType.DMA((3 * n,)), pltpu.SemaphoreType.DMA((3 * n,)),
                   *[pltpu.HBM(a.shape, a.dtype) for a in sums + slots], TOKEN),
        in_specs=[HBM] * (2 * n), out_specs=(SEM, SEM, *[HBM] * (2 * n), pl.BlockSpec(memory_space=pltpu.VMEM)),
        input_output_aliases={i: 2 + i for i in range(2 * n)},
        compiler_params=_split_params(),
    )(*[_in_hbm(a) for a in sums + slots])
    return out[0], out[1], out[2:2 + n], out[2 + n:2 + 2 * n], out[-1]


def _chip_exchange_wait(send_sems, recv_sems, sums, slots, by_rows, after, tag):
    n = len(sums)
    sums, slots = list(sums), list(slots)
    dims = _exchange_dims(sums, by_rows)

    def body(*refs):
        ins, land = refs[:n], refs[n:2 * n]
        send, recv = refs[2 * n], refs[2 * n + 1]
        for cp, _ in _chip_exchange_copies(ins, land, dims, by_rows, send, recv):
            cp.wait_send()
            cp.wait_recv()

    out = pl.pallas_call(
        body, name="grad_chip_exchange_wait_" + tag,
        out_shape=tuple(pltpu.HBM(a.shape, a.dtype) for a in sums + slots),
        in_specs=[HBM] * (2 * n) + [SEM, SEM, ANY], out_specs=tuple([HBM] * (2 * n)),
        input_output_aliases={i: i for i in range(2 * n)},
        compiler_params=_split_params(),
    )(*sums, *slots, send_sems, recv_sems, after)
    return out[:n], out[n:]


def _core_exchange_copies(ins, theirs, send_sems, recv_sems):
    x, y, c = _place()
    copies = []
    for i in range(len(ins)):
        H = ins[i].shape[1] // 2
        copies.append(pltpu.make_async_remote_copy(
            src_ref=ins[i].at[:, pl.ds(pl.multiple_of((1 - c) * H, 8), H), :], dst_ref=theirs[i],
            send_sem=send_sems.at[i], recv_sem=recv_sems.at[i], device_id=(x, y, 1 - c), device_id_type=MESH))
    return copies


def _core_exchange_start(grads):
    n = len(grads)
    grads = list(grads)
    theirs = [lax.empty((g.shape[0], g.shape[1] // 2, g.shape[2]), F32) for g in grads]

    def body(*refs):
        for cp in _core_exchange_copies(refs[:n], refs[n:2 * n], refs[2 * n], refs[2 * n + 1]):
            cp.start()
        refs[-1][...] = jnp.zeros_like(refs[-1])

    out = pl.pallas_call(
        body, name="grad_core_exchange_start",
        out_shape=(pltpu.SemaphoreType.DMA((n,)), pltpu.SemaphoreType.DMA((n,)),
                   *[pltpu.HBM(a.shape, a.dtype) for a in grads + theirs], TOKEN),
        in_specs=[HBM] * (2 * n), out_specs=(SEM, SEM, *[HBM] * (2 * n), pl.BlockSpec(memory_space=pltpu.VMEM)),
        input_output_aliases={i: 2 + i for i in range(2 * n)},
        compiler_params=_split_params(),
    )(*[_in_hbm(a) for a in grads + theirs])
    return out[0], out[1], out[2:2 + n], out[2 + n:2 + 2 * n], out[-1]


def _core_exchange_wait(send_sems, recv_sems, grads, theirs, after):
    n = len(grads)
    grads, theirs = list(grads), list(theirs)

    def body(*refs):
        for cp in _core_exchange_copies(refs[:n], refs[n:2 * n], refs[2 * n], refs[2 * n + 1]):
            cp.wait_send()
            cp.wait_recv()

    out = pl.pallas_call(
        body, name="grad_core_exchange_wait",
        out_shape=tuple(pltpu.HBM(a.shape, a.dtype) for a in grads + theirs),
        in_specs=[HBM] * (2 * n) + [SEM, SEM, ANY], out_specs=tuple([HBM] * (2 * n)),
        input_output_aliases={i: i for i in range(2 * n)},
        compiler_params=_split_params(),
    )(*grads, *theirs, send_sems, recv_sems, after)
    return out[:n], out[n:]


def _core_share(reduced, tag):
    n = len(reduced)

    def body(*refs):
        outs = refs[n:2 * n]
        send_sems, recv_sems = refs[2 * n:]
        x, y, c = _place()
        copies = []
        for i in range(n):
            cp = pltpu.make_async_remote_copy(
                src_ref=outs[i].at[c], dst_ref=outs[i].at[c], send_sem=send_sems.at[i], recv_sem=recv_sems.at[i],
                device_id=(x, y, 1 - c), device_id_type=MESH)
            cp.start()
            copies.append(cp)
        for cp in copies:
            cp.wait()

    return pl.pallas_call(
        body, name="grad_core_share_" + tag,
        in_specs=[ANY] * n, out_specs=[ANY] * n,
        out_shape=[jax.ShapeDtypeStruct(r.shape, F32) for r in reduced],
        input_output_aliases={i: i for i in range(n)},
        scratch_shapes=[pltpu.SemaphoreType.DMA((n,))] * 2,
    )(*reduced)


def _small_exchange_copies(pack_ref, slots_ref, send_sems, recv_sems):
    x, y, c = _place()
    peers = [(px, py, pc) for px in (x, 1 - x) for py in (y, 1 - y) for pc in (c, 1 - c)][1:]
    pairs = []
    for k, peer in enumerate(peers):
        def copy(sender, k=k, peer=peer):
            return pltpu.make_async_remote_copy(
                src_ref=pack_ref, dst_ref=slots_ref.at[4 * sender[0] + 2 * sender[1] + sender[2]],
                send_sem=send_sems.at[k], recv_sem=recv_sems.at[k], device_id=peer, device_id_type=MESH)

        pairs.append((copy((x, y, c)), copy(peer)))
    return pairs


def _small_exchange_start(pack):
    slots = lax.empty((N_DEV,) + pack.shape, F32)

    def body(pack_ref, slots_ref, send_sems, recv_sems, pack_thru, slots_thru, token):
        for mine, _ in _small_exchange_copies(pack_ref, slots_ref, send_sems, recv_sems):
            mine.start()
        token[...] = jnp.zeros_like(token)

    return pl.pallas_call(
        body, name="grad_small_exchange_start",
        out_shape=(pltpu.SemaphoreType.DMA((N_DEV - 1,)), pltpu.SemaphoreType.DMA((N_DEV - 1,)),
                   pltpu.HBM(pack.shape, F32), pltpu.HBM(slots.shape, F32), TOKEN),
        in_specs=[HBM, HBM], out_specs=(SEM, SEM, HBM, HBM, pl.BlockSpec(memory_space=pltpu.VMEM)),
        input_output_aliases={0: 2, 1: 3},
        compiler_params=_split_params(),
    )(_in_hbm(pack), _in_hbm(slots))


def _small_exchange_wait(send_sems, recv_sems, pack, slots, after):
    def body(pack_ref, slots_ref, send, recv, after_ref, pack_thru, slots_thru):
        for mine, theirs in _small_exchange_copies(pack_ref, slots_ref, send, recv):
            mine.wait_send()
            theirs.wait_recv()

    return pl.pallas_call(
        body, name="grad_small_exchange_wait",
        out_shape=(pltpu.HBM(pack.shape, F32), pltpu.HBM(slots.shape, F32)),
        in_specs=[HBM, HBM, SEM, SEM, ANY], out_specs=(HBM, HBM),
        input_output_aliases={0: 0, 1: 1},
        compiler_params=_split_params(),
    )(pack, slots, send_sems, recv_sems, after)


def _sum_small(pack, slots, me):
    R, C = pack.shape
    tr = _row_tile(R)

    def body(me_ref, p_ref, q_ref, o_ref):
        acc = jnp.where(me_ref[0] == 0, p_ref[...], q_ref[0])
        for d in range(1, N_DEV):
            acc = acc + jnp.where(me_ref[0] == d, p_ref[...], q_ref[d])
        o_ref[...] = acc

    return pl.pallas_call(
        body, name="sum_small",
        grid_spec=_scalar_grid((R // tr,), [pl.BlockSpec((tr, C), lambda i, m: (i, 0)),
                                            pl.BlockSpec((N_DEV, tr, C), lambda i, m: (0, i, 0))],
                               pl.BlockSpec((tr, C), lambda i, m: (i, 0))),
        out_shape=jax.ShapeDtypeStruct((R, C), F32),
        compiler_params=_params(("arbitrary",)),
    )(me.reshape(1), pack, slots)


def _pack_rows(parts, rows):
    flat = jnp.concatenate([a.reshape(-1) for a in parts])
    return jnp.pad(flat, (0, rows * 128 - flat.shape[0])).reshape(rows, 128)


def _unpack_rows(pack, shapes):
    flat = pack.reshape(-1)
    out, at = [], 0
    for s in shapes:
        size = 1
        for d in s:
            size *= d
        out.append(flat[at:at + size].reshape(s))
        at += size
    return out


def kernel(x, p, norm1_g, w_in, b_gate, pool_w, pool_scale, pool_proj, conv_w, conv_b, w_rg, b_rg, w_ig, b_ig, lru_lambda, lru_proj, w_out, norm2_g, w_ffn_in, w_ffn_out, ple_norm_g, w_ple_gate, w_ple_proj, final_g, loss_target, m_norm1_g, m_w_in, m_b_gate, m_pool_w, m_pool_scale, m_pool_proj, m_conv_w, m_conv_b, m_w_rg, m_b_rg, m_w_ig, m_b_ig, m_lru_lambda, m_lru_proj, m_w_out, m_norm2_g, m_w_ffn_in, m_w_ffn_out, m_ple_norm_g, m_w_ple_gate, m_w_ple_proj, m_final_g, v_norm1_g, v_w_in, v_b_gate, v_pool_w, v_pool_scale, v_pool_proj, v_conv_w, v_conv_b, v_w_rg, v_b_rg, v_w_ig, v_b_ig, v_lru_lambda, v_lru_proj, v_w_out, v_norm2_g, v_w_ffn_in, v_w_ffn_out, v_ple_norm_g, v_w_ple_gate, v_w_ple_proj, v_final_g):
    weights = dict(norm1_g=norm1_g, w_in=w_in, b_gate=b_gate, pool_w=pool_w, pool_scale=pool_scale,
                   pool_proj=pool_proj, conv_w=conv_w, conv_b=conv_b, w_rg=w_rg, b_rg=b_rg, w_ig=w_ig, b_ig=b_ig,
                   lru_lambda=lru_lambda, lru_proj=lru_proj, w_out=w_out, norm2_g=norm2_g, w_ffn_in=w_ffn_in,
                   w_ffn_out=w_ffn_out, ple_norm_g=ple_norm_g, w_ple_gate=w_ple_gate, w_ple_proj=w_ple_proj,
                   final_g=final_g)
    m_in = dict(norm1_g=m_norm1_g, w_in=m_w_in, b_gate=m_b_gate, pool_w=m_pool_w, pool_scale=m_pool_scale,
                pool_proj=m_pool_proj, conv_w=m_conv_w, conv_b=m_conv_b, w_rg=m_w_rg, b_rg=m_b_rg, w_ig=m_w_ig,
                b_ig=m_b_ig, lru_lambda=m_lru_lambda, lru_proj=m_lru_proj, w_out=m_w_out, norm2_g=m_norm2_g,
                w_ffn_in=m_w_ffn_in, w_ffn_out=m_w_ffn_out, ple_norm_g=m_ple_norm_g, w_ple_gate=m_w_ple_gate,
                w_ple_proj=m_w_ple_proj, final_g=m_final_g)
    v_in = dict(norm1_g=v_norm1_g, w_in=v_w_in, b_gate=v_b_gate, pool_w=v_pool_w, pool_scale=v_pool_scale,
                pool_proj=v_pool_proj, conv_w=v_conv_w, conv_b=v_conv_b, w_rg=v_w_rg, b_rg=v_b_rg, w_ig=v_w_ig,
                b_ig=v_b_ig, lru_lambda=v_lru_lambda, lru_proj=v_lru_proj, w_out=v_w_out, norm2_g=v_norm2_g,
                w_ffn_in=v_w_ffn_in, w_ffn_out=v_w_ffn_out, ple_norm_g=v_ple_norm_g, w_ple_gate=v_w_ple_gate,
                w_ple_proj=v_w_ple_proj, final_g=v_final_g)
    names = list(weights)
    big = ["w_in", "pool_proj", "lru_proj", "w_out", "w_ffn_in", "w_ffn_out", "w_ple_gate", "w_ple_proj"]
    by_rows = [n in ("lru_proj", "w_out", "w_ffn_out", "w_ple_gate") for n in big]
    small = [n for n in names if n not in big]

    shard_j = 2 * lax.axis_index("x") + lax.axis_index("y")
    T = x.shape[1]
    xs, ps, tgt = x[0], p[0, 0], loss_target[0]

    small_local = jnp.concatenate([b_gate[0], conv_w[0], jnp.zeros((2, 256), F32)], axis=0)
    core = lax.axis_index("c").astype(jnp.int32)
    place = jnp.stack([shard_j, core]).astype(jnp.int32)
    rows_of = dict(zip(big, by_rows))
    shard_shape = {n: weights[n].shape[1:] for n in big}
    blocks = {n: _cast_into_block(weights[n][0], rows_of[n], place[0], "cast_" + n) for n in big}
    early, late = big[:4], big[4:]
    gathered = _all_gather_weights([blocks[n] for n in early], [shard_shape[n] for n in early],
                                   [rows_of[n] for n in early], small_local)
    full = dict(zip(early, gathered[:-1]))
    late_send, late_recv, late_bufs, late_token = _gather_rest_start(
        [blocks[n] for n in late], [shard_shape[n] for n in late], [rows_of[n] for n in late], gathered[-1])
    b_gate_full = gathered[-1][0:2].reshape(1, 2 * D_MODEL)
    conv_w_full = gathered[-1][2:6]
    pool_w_1, w_rg_1, w_ig_1 = [w[0].astype(BF16) for w in (pool_w, w_rg, w_ig)]
    pool_w_b, w_rg_b, w_ig_b = [_pair_blocks(w) for w in (pool_w_1, w_rg_1, w_ig_1)]
    b_rg_row, b_ig_row = b_rg.reshape(1, D_MODEL), b_ig.reshape(1, D_MODEL)
    final_row = final_g.reshape(1, D_MODEL)

    zp, zl, zg, zt, u, h1, hs, yp, yl, xc_saved, r_saved, ig_saved = _f12_mixer(
        xs, norm1_g + late_token[0, 0], full["w_in"], b_gate_full, pool_w_1, pool_scale, full["pool_proj"],
        conv_w_full, conv_b, w_rg_1, b_rg_row, w_ig_1, b_ig_row, lru_lambda, full["lru_proj"], full["w_out"])
    full.update(zip(late, _gather_rest_wait(late_send, late_recv, late_bufs, [shard_shape[n] for n in late],
                                            [rows_of[n] for n in late], h1)))
    h2, v, ff, act = _f3_ffn(h1, norm2_g, full["w_ffn_in"], full["w_ffn_out"])

    loss_sum, dh2, g_ple_gate, g_ple_proj, vec4 = _b4_ple_loss(
        h2, ps, tgt, ple_norm_g, full["w_ple_gate"], full["w_ple_proj"], final_row)
    dff, dh1, vec3 = _b3_ffn(dh2, h1, ff, norm2_g, full["w_ffn_in"], full["w_ffn_out"])
    g_ffn_in = _wgrad(v, dff, 2 * D_FF // N_SHARDS, "wgrad_ffn_in")
    g_ffn_out = _wgrad(act, dh2, D_MODEL, "wgrad_ffn_out", tokens=WGRAD_TOKENS // 2)

    def stack(n, g):
        return g.reshape(N_SHARDS, g.shape[0] // N_SHARDS, g.shape[1]) if rows_of[n] else g[None]

    def chip_sums_of(group, grads_of, tag):
        stacked = [stack(n, grads_of[n]) for n in group]
        theirs = _core_exchange(stacked, "grad_core_exchange_" + tag)
        return [_sum_cores(g, t, core, "sum_cores_" + n) for g, t, n in zip(stacked, theirs, group)]

    late_rows = [rows_of[n] for n in late]
    late_grads = dict(w_ffn_in=g_ffn_in, w_ffn_out=g_ffn_out, w_ple_gate=g_ple_gate, w_ple_proj=g_ple_proj)
    cx_send, cx_recv, late_stacked, late_theirs, cx_token = _core_exchange_start(
        [stack(n, late_grads[n]) for n in late])
    dzt, dyp, dyl, g_w_out, vec_g = _b2_gates(dh1, zt, yp, yl, b_gate_full + cx_token[0, 0], full["w_out"])
    late_stacked, late_theirs = _core_exchange_wait(cx_send, cx_recv, late_stacked, late_theirs, dzt)
    late_sums = [_sum_cores(g, t, core, "sum_cores_" + n) for g, t, n in zip(late_stacked, late_theirs, late)]
    ex_send, ex_recv, late_sums, late_slots, ex_token = _chip_exchange_start(late_sums, late_rows, "late")
    dzl, dzg, g_lru_proj, g_w_rg, g_w_ig, vec_l = _b2_lru(
        dyl, zl, zg, hs, xc_saved, r_saved, ig_saved, conv_w_full, w_rg_b, w_ig_b, lru_lambda + ex_token[0, 0],
        full["lru_proj"])
    dzp, grad_x, g_pool_proj, g_pool_w, vec_p = _b12_pool_in_proj(
        dyp, zp, dzl, dzg, dzt, xs, dh1, norm1_g, full["w_in"], pool_w_b, pool_scale, full["pool_proj"])
    small_full = dict(
        norm1_g=vec_p[1], b_gate=vec_g[0:2], pool_w=_unpair_blocks(g_pool_w), pool_scale=vec_p[0, :POOL_WIDTH],
        conv_w=vec_l[_V_CONVW:_V_CONVW + CONV_WIDTH], conv_b=vec_l[_V_CONVB], w_rg=_unpair_blocks(g_w_rg),
        b_rg=vec_l[_V_BRG], w_ig=_unpair_blocks(g_w_ig), b_ig=vec_l[_V_BIG], lru_lambda=vec_l[_V_LAM], norm2_g=vec3[0], ple_norm_g=vec4[1],
        final_g=vec4[0])
    full_shapes = [small_full[n].shape for n in small]
    n_full = sum(int(small_full[n].size) for n in small)
    rows_full = -(-n_full // (128 * ROW_TILE)) * ROW_TILE
    sm_send, sm_recv, sm_pack, sm_slots, sm_token = _small_exchange_start(
        _pack_rows([small_full[n] for n in small], rows_full))
    g_w_in = jnp.concatenate([
        _wgrad(u, dzp, POOL_WIDTH, "wgrad_in_pool", after=sm_token),
        _wgrad(u, dzl, D_MODEL, "wgrad_in_lru", after=sm_token),
        _wgrad(u, dzg, D_MODEL, "wgrad_in_gelu", after=sm_token),
        _wgrad(u, dzt, D_MODEL, "wgrad_in_gate", after=sm_token)], axis=1)

    loss = lax.psum(loss_sum[0, 0] * (0.5 / D_MODEL), ("x", "y", "c"))

    early_rows = [rows_of[n] for n in early]
    early_sums = chip_sums_of(early, dict(w_in=g_w_in, pool_proj=g_pool_proj, lru_proj=g_lru_proj, w_out=g_w_out),
                              "early")
    e_send, e_recv, early_sums, early_slots, e_token = _chip_exchange_start(early_sums, early_rows, "early")
    grads, deltas, new_m, new_v = {}, {}, {}, {}

    def finish(group, sums, slots, tag):
        reduced = _core_share([_sum_chips(s, q, rows_of[n], place, "sum_chips_" + n)
                               for s, q, n in zip(sums, slots, group)], tag)
        for n, r in zip(group, reduced):
            g = r.reshape(r.shape[0] * r.shape[1], r.shape[2])
            d, nm, nv = _adamw(weights[n][0], g, m_in[n][0], v_in[n][0], "adamw_" + n)
            grads[n], deltas[n], new_m[n], new_v[n] = g[None], d[None], nm[None], nv[None]

    late_sums, late_slots = _chip_exchange_wait(ex_send, ex_recv, late_sums, late_slots, late_rows, e_token, "late")
    finish(late, late_sums, late_slots, "late")

    sm_pack, sm_slots = _small_exchange_wait(sm_send, sm_recv, sm_pack, sm_slots, e_token)
    device = (4 * lax.axis_index("x") + 2 * lax.axis_index("y") + lax.axis_index("c")).astype(jnp.int32)
    summed = dict(zip(small, _unpack_rows(_sum_small(sm_pack, sm_slots, device), full_shapes)))
    summed["b_gate"] = lax.dynamic_slice_in_dim(summed["b_gate"], shard_j * 256, 256, axis=1)
    summed["conv_w"] = lax.dynamic_slice_in_dim(summed["conv_w"], shard_j * 256, 256, axis=1)
    local_shapes = [weights[n].shape for n in small]
    n_local = sum(int(weights[n].size) for n in small)
    rows_local = -(-n_local // (128 * ROW_TILE)) * ROW_TILE
    packs = [_pack_rows([src[n] for n in small], rows_local) for src in (weights, summed, m_in, v_in)]
    d_s, nm_s, nv_s = _adamw(*packs, "adamw_small")
    for dst, pack in ((grads, packs[1]), (deltas, d_s), (new_m, nm_s), (new_v, nv_s)):
        dst.update(zip(small, _unpack_rows(pack, local_shapes)))

    done = d_s[:SUBLANES]
    for n in late:
        done = done + deltas[n][0, :SUBLANES, :128]
    early_sums, early_slots = _chip_exchange_wait(e_send, e_recv, early_sums, early_slots, early_rows, done, "early")
    finish(early, early_sums, early_slots, "early")

    return (loss, grad_x[None], *[grads[n] for n in names], *[deltas[n] for n in names],
            *[new_m[n] for n in names], *[new_v[n] for n in names])
```

```python
import functools

import jax
import jax.numpy as jnp
from jax import lax
from jax.experimental import pallas as pl
from jax.experimental.pallas import tpu as pltpu

F32 = jnp.float32
BF16 = jnp.bfloat16

D_MODEL = 1024
POOL_WIDTH = 512
POOL_GROUP_DIM = 128
POOL_WINDOWS = (2, 4, 8, 16)
POOL_HALO = 16
LRU_HEADS = 8
LRU_HEAD_DIM = 128
CONV_WIDTH = 4
LRU_C = 8.0
D_FF = 2816
PLE_DIM = 256
RMS_EPS = 1e-6
N_SHARDS = 4
N_DEV = 8

ADAM_LR = 0.001
ADAM_B1 = 0.9
ADAM_B2 = 0.999
ADAM_EPS = 1e-08
ADAM_WD = 0.01
ADAM_STEP = 10

ROW_TILE = 256
WIDE_TILE = 512
WGRAD_TOKENS = 2048
SUBLANES = 8
VMEM_LIMIT = 56 * 1024 * 1024
MESH = pl.DeviceIdType.MESH
ANY = pl.BlockSpec(memory_space=pl.ANY)


def _params(semantics=None):
    return pltpu.CompilerParams(dimension_semantics=semantics, vmem_limit_bytes=VMEM_LIMIT)


def _resident(shape):
    n = len(shape)
    return pl.BlockSpec(shape, lambda *_: (0,) * n, pipeline_mode=pl.Buffered(1))


def _acc(shape):
    n = len(shape)
    return pl.BlockSpec(shape, lambda *_: (0,) * n)


def _rows(tile, cols):
    return pl.BlockSpec((tile, cols), lambda i: (i, 0))


def _rows_rev(tile, cols, n_tiles):
    return pl.BlockSpec((tile, cols), lambda i: (n_tiles - 1 - i, 0))


def _halo_before_rev(rows, cols, tile, n_tiles):
    per = tile // rows
    return pl.BlockSpec((rows, cols), lambda i: (jnp.maximum((n_tiles - 1 - i) * per - 1, 0), 0))


def _nn(a, b):
    return jnp.dot(a, b, preferred_element_type=F32)


def _nt(a, b):
    return lax.dot_general(a, b, (((1,), (1,)), ((), ())), preferred_element_type=F32)


def _tn(a, b):
    return lax.dot_general(a, b, (((0,), (0,)), ((), ())), preferred_element_type=F32)


def _rms(x):
    r = lax.rsqrt(jnp.mean(x * x, axis=-1, keepdims=True) + RMS_EPS)
    return x * r, r


def _rms_bwd(dn, n, r):
    return r * (dn - n * jnp.mean(dn * n, axis=-1, keepdims=True))


def _sigmoid(x):
    return 0.5 * jnp.tanh(0.5 * x) + 0.5


_GELU_C = 0.7978845608028654
_GELU_A = 0.044715


def _gelu(x):
    t = jnp.tanh(_GELU_C * (x + _GELU_A * x * x * x))
    return 0.5 * x * (1.0 + t)


def _gelu_and_grad(x):
    x2 = x * x
    t = jnp.tanh(_GELU_C * (x + _GELU_A * x2 * x))
    cdf = 0.5 * (1.0 + t)
    grad = cdf + 0.5 * x * (1.0 - t * t) * _GELU_C * (1.0 + 3.0 * _GELU_A * x2)
    return x * cdf, grad


def _softplus_neg(lam):
    e = jnp.exp(-jnp.abs(lam))
    sp = jnp.maximum(-lam, 0.0) + jnp.log1p(e)
    return sp, -_sigmoid(-lam)


def _colsum(v):
    return jnp.sum(v, axis=0, keepdims=True)


def _row_ids(shape):
    return lax.broadcasted_iota(jnp.int32, shape, 0)


def _shift_down(cat, k):
    return pltpu.roll(cat, k, 0) if k else cat


def _shift_up(cat, k):
    return pltpu.roll(cat, cat.shape[0] - k, 0) if k else cat


IN_SPLITS = (0, POOL_WIDTH, POOL_WIDTH + D_MODEL, POOL_WIDTH + 2 * D_MODEL, POOL_WIDTH + 4 * D_MODEL)
IN_WIDTHS = tuple(IN_SPLITS[k + 1] - IN_SPLITS[k] for k in range(4))
PROJ_CHUNK = 256
PAIR_DIM = 2 * LRU_HEAD_DIM


def _pair_blocks(w):
    zero = jnp.zeros_like(w[0::2])
    return jnp.concatenate([jnp.concatenate([w[0::2], zero], axis=2), jnp.concatenate([zero, w[1::2]], axis=2)], axis=1)


def _unpair_blocks(w):
    n, d2, _ = w.shape
    d = d2 // 2
    return jnp.stack([w[:, :d, :d], w[:, d:, d:]], axis=1).reshape(2 * n, d, d)


def _no_tick():
    pass


class _Interleaved:
    def __init__(self, pieces):
        self._pieces = iter(pieces)

    def tick(self, n=1):
        for _ in range(n):
            piece = next(self._pieces, None)
            if piece is not None:
                piece()

    def flush(self):
        for piece in self._pieces:
            piece()


def _pool_forward(zp_cat, pw_ref, first_row, tick=_no_tick):
    tt = zp_cat.shape[0] - POOL_HALO
    t_glob = first_row + _row_ids((tt, POOL_GROUP_DIM))
    pooled, mixed = [], []
    for g, w in enumerate(POOL_WINDOWS):
        cat = zp_cat[:, g * POOL_GROUP_DIM:(g + 1) * POOL_GROUP_DIM]
        s, k = cat, 1
        while k < w:
            s = s + _shift_down(s, k)
            k *= 2
        cnt = jnp.minimum(t_glob + 1, w).astype(F32)
        pooled.append(s[POOL_HALO:] / cnt - cat[POOL_HALO:])
        per = pw_ref.shape[-1] // POOL_GROUP_DIM
        if (g + 1) % per == 0:
            block = jnp.concatenate(pooled[-per:], axis=1).astype(BF16)
            mixed.append(_nn(block, pw_ref[g // per]))
        tick()
    return jnp.concatenate(pooled, axis=1), jnp.concatenate(mixed, axis=1)


def _lru_gates(zl_cat, conv_w, conv_b, wrg_ref, brg, wig_ref, big, sp, first_row, tick=_no_tick):
    tt = zl_cat.shape[0] - SUBLANES
    xc = conv_w[CONV_WIDTH - 1:CONV_WIDTH] * zl_cat
    for k in range(1, CONV_WIDTH):
        xc = xc + conv_w[CONV_WIDTH - 1 - k:CONV_WIDTH - k] * _shift_down(zl_cat, k)
        tick()
    xc = xc[SUBLANES:] + conv_b
    xh = xc.astype(BF16)
    pr, pi = [], []
    width = wrg_ref.shape[-1]
    for p in range(D_MODEL // width):
        xs = xh[:, p * width:(p + 1) * width]
        pr.append(_nn(xs, wrg_ref[p]))
        pi.append(_nn(xs, wig_ref[p]))
    r = _sigmoid(jnp.concatenate(pr, axis=1) + brg)
    tick()
    ig = _sigmoid(jnp.concatenate(pi, axis=1) + big)
    tick()
    a, mult = _decay(r, sp, first_row, tick)
    tick()
    return xc, r, ig, a, mult


def _decay(r, sp, first_row, tick=_no_tick):
    a = jnp.exp(-LRU_C * r * sp)
    tick()
    mult = jnp.sqrt(jnp.maximum(1.0 - a * a, 0.0))
    t_glob = first_row + _row_ids(r.shape)
    return a, jnp.where(t_glob == 0, 1.0, mult)


def _f12_mixer(x, norm1_g, w_in, b_gate, pool_w, pool_scale, pool_proj, conv_w, conv_b, w_rg, b_rg, w_ig, b_ig,
               lru_lambda, lru_proj, w_out):
    T = x.shape[0]
    tt = ROW_TILE
    nt = T // tt
    n_groups = tt // SUBLANES
    proj_mid = IN_SPLITS[3] + D_MODEL // 2

    def body(xm_ref, x_ref, g1_ref, win_ref, bg_ref, pw_ref, ps_ref, pp_ref, cw_ref, cb_ref,
             wrg_ref, brg_ref, wig_ref, big_ref, lam_ref, lp_ref, wo_ref,
             zp_ref, zl_ref, zg_ref, zt_ref, u_ref, h1_ref, hs_ref, yp_ref, yl_ref, xc_ref, r_ref, ig_ref,
             zbuf, zp_halo, zl_halo, a_s, b_s, carry_s):
        s = pl.program_id(0)

        @pl.when(s == 0)
        def _():
            zbuf[1] = jnp.zeros((tt, IN_SPLITS[4]), F32)
            zp_halo[...] = jnp.zeros_like(zp_halo)
            zl_halo[...] = jnp.zeros_like(zl_halo)
            carry_s[...] = jnp.zeros_like(carry_s)

        z_new, z_old = zbuf.at[s % 2], zbuf.at[(s + 1) % 2]
        first = s <= 1
        first_row = jnp.maximum(s - 1, 0) * tt

        n1, _ = _rms(xm_ref[...])
        u = (n1 * g1_ref[...]).astype(BF16)
        u_ref[...] = u

        z_refs = (zp_ref, zl_ref, zg_ref, zt_ref)

        def project(lo):
            k = max(i for i in range(4) if IN_SPLITS[i] <= lo)
            part = _nn(u, win_ref[:, lo:lo + PROJ_CHUNK])
            z_new[:, lo:lo + PROJ_CHUNK] = part
            z_refs[k][:, lo - IN_SPLITS[k]:lo - IN_SPLITS[k] + PROJ_CHUNK] = part.astype(z_refs[k].dtype)

        before_scan = _Interleaved(functools.partial(project, lo) for lo in range(0, proj_mid, PROJ_CHUNK))
        after_scan = _Interleaved(functools.partial(project, lo) for lo in range(proj_mid, IN_SPLITS[4], PROJ_CHUNK))

        zp_cat = jnp.concatenate([jnp.where(first, 0.0, zp_halo[...]), z_old[:, IN_SPLITS[0]:IN_SPLITS[1]]], axis=0)
        _, mixed = _pool_forward(zp_cat, pw_ref, first_row, before_scan.tick)
        y_pool = _nn((mixed * ps_ref[...]).astype(BF16), pp_ref[...])

        sp, _ = _softplus_neg(lam_ref[...])
        zl_cat = jnp.concatenate([jnp.where(first, 0.0, zl_halo[...]), z_old[:, IN_SPLITS[1]:IN_SPLITS[2]]], axis=0)
        xc, r, ig, a, mult = _lru_gates(zl_cat, cw_ref[...], cb_ref[...], wrg_ref, brg_ref[...], wig_ref,
                                        big_ref[...], sp, first_row, before_scan.tick)
        a_s[...] = a
        b_s[...] = mult * ig * xc
        xc_ref[...] = xc.astype(BF16)
        r_ref[...] = r.astype(BF16)
        ig_ref[...] = ig.astype(BF16)
        before_scan.flush()

        rows8 = _row_ids((SUBLANES, D_MODEL))

        def group(g, carry):
            at = pl.ds(pl.multiple_of(g * SUBLANES, SUBLANES), SUBLANES)
            A, B = a_s[at, :], b_s[at, :]
            for s in (1, 2, 4):
                m = rows8 >= s
                B = jnp.where(m, A * pltpu.roll(B, s, 0) + B, B)
                A = jnp.where(m, A * pltpu.roll(A, s, 0), A)
            h = A * carry + B
            hs_ref[at, :] = h
            return jnp.broadcast_to(h[SUBLANES - 1:SUBLANES, :], (SUBLANES, D_MODEL))

        carry_s[...] = lax.fori_loop(0, n_groups, group, jnp.where(first, 0.0, carry_s[...]))
        gelu = _gelu(z_old[:, IN_SPLITS[2]:IN_SPLITS[3]])
        after_scan.tick(2)
        y_lru = _nn((hs_ref[...] * gelu).astype(BF16), lp_ref[...])

        gates = _sigmoid(z_old[:, IN_SPLITS[3]:IN_SPLITS[4]] + bg_ref[...])
        after_scan.tick(2)
        merged = gates[:, :D_MODEL] * y_pool + gates[:, D_MODEL:] * y_lru
        after_scan.flush()
        h1_ref[...] = x_ref[...] + _nn(merged.astype(BF16), wo_ref[...])
        yp_ref[...] = y_pool.astype(BF16)
        yl_ref[...] = y_lru.astype(BF16)
        zp_halo[...] = z_old[tt - POOL_HALO:, IN_SPLITS[0]:IN_SPLITS[1]]
        zl_halo[...] = z_old[tt - SUBLANES:, IN_SPLITS[1]:IN_SPLITS[2]]

    def ahead(cols):
        return pl.BlockSpec((tt, cols), lambda s: (jnp.minimum(s, nt - 1), 0))

    def behind(cols):
        return pl.BlockSpec((tt, cols), lambda s: (jnp.maximum(s - 1, 0), 0))

    res = [norm1_g, w_in, b_gate, pool_w, pool_scale, pool_proj, conv_w, conv_b, w_rg, b_rg, w_ig, b_ig, lru_lambda,
           lru_proj, w_out]
    return pl.pallas_call(
        body, name="f12_mixer", grid=(nt + 1,),
        in_specs=[ahead(D_MODEL), behind(D_MODEL)] + [_resident(w.shape) for w in res],
        out_specs=[ahead(w) for w in IN_WIDTHS] + [ahead(D_MODEL)] + [behind(D_MODEL)] * 7,
        out_shape=[jax.ShapeDtypeStruct((T, w), dt) for w, dt in zip(IN_WIDTHS, (F32, F32, F32, BF16))]
        + [jax.ShapeDtypeStruct((T, D_MODEL), BF16), jax.ShapeDtypeStruct((T, D_MODEL), F32),
           jax.ShapeDtypeStruct((T, D_MODEL), F32)] + [jax.ShapeDtypeStruct((T, D_MODEL), BF16)] * 5,
        scratch_shapes=[pltpu.VMEM((2, tt, IN_SPLITS[4]), F32), pltpu.VMEM((POOL_HALO, POOL_WIDTH), F32),
                        pltpu.VMEM((SUBLANES, D_MODEL), F32), pltpu.VMEM((tt, D_MODEL), F32),
                        pltpu.VMEM((tt, D_MODEL), F32), pltpu.VMEM((SUBLANES, D_MODEL), F32)],
        compiler_params=_params(("arbitrary",)),
    )(x, x, *res)


def _f3_ffn(h1, norm2_g, w_ffn_in, w_ffn_out):
    T = h1.shape[0]
    tm = ROW_TILE

    def body(h_ref, g_ref, wi_ref, wo_ref, h2_ref, v_ref, ff_ref, act_ref):
        h = h_ref[...]
        n, _ = _rms(h)
        v = (n * g_ref[...]).astype(BF16)
        v_ref[...] = v
        g_ff = _nn(v, wi_ref[:, :D_FF])
        u_ff = _nn(v, wi_ref[:, D_FF:])
        ff_ref[:, :D_FF] = g_ff.astype(BF16)
        ff_ref[:, D_FF:] = u_ff.astype(BF16)
        act = (g_ff * _sigmoid(g_ff) * u_ff).astype(BF16)
        act_ref[...] = act
        h2_ref[...] = h + _nn(act, wo_ref[...])

    return pl.pallas_call(
        body, name="f3_ffn", grid=(T // tm,),
        in_specs=[_rows(tm, D_MODEL), _resident((1, D_MODEL)), _resident(w_ffn_in.shape), _resident(w_ffn_out.shape)],
        out_specs=[_rows(tm, D_MODEL), _rows(tm, D_MODEL), _rows(tm, 2 * D_FF), _rows(tm, D_FF)],
        out_shape=[jax.ShapeDtypeStruct((T, D_MODEL), F32), jax.ShapeDtypeStruct((T, D_MODEL), BF16),
                   jax.ShapeDtypeStruct((T, 2 * D_FF), BF16), jax.ShapeDtypeStruct((T, D_FF), BF16)],
        compiler_params=_params(("arbitrary",)),
    )(h1, norm2_g, w_ffn_in, w_ffn_out)


def _b4_ple_loss(h2, p, target, ple_norm_g, w_ple_gate, w_ple_proj, final_g):
    T = h2.shape[0]
    tm = WIDE_TILE

    def body(h_ref, p_ref, t_ref, gp_ref, wg_ref, wp_ref, gf_ref, loss_ref, dh2_ref, dwg_ref, dwp_ref, vec_ref):
        @pl.when(pl.program_id(0) == 0)
        def _():
            loss_ref[...] = jnp.zeros_like(loss_ref)
            dwg_ref[...] = jnp.zeros_like(dwg_ref)
            dwp_ref[...] = jnp.zeros_like(dwp_ref)
            vec_ref[...] = jnp.zeros_like(vec_ref)

        h2v = h_ref[...]
        n3, r3 = _rms(h2v)
        n3g = (n3 * gp_ref[...]).astype(BF16)
        pb = p_ref[...].astype(BF16)
        q = _nn(n3g, wg_ref[...])
        e = _nn(pb, wp_ref[...])
        pg = _sigmoid(q)
        h3 = h2v + pg * e
        n4, r4 = _rms(h3)
        diff = n4 * gf_ref[...] - t_ref[...]
        loss_ref[...] += jnp.sum(diff * diff).reshape(1, 1)
        dy = diff * (1.0 / D_MODEL)
        vec_ref[0:1, :] += _colsum(dy * n4)
        dh3 = _rms_bwd(dy * gf_ref[...], n4, r4)
        de = (dh3 * pg).astype(BF16)
        dq = (dh3 * e * pg * (1.0 - pg)).astype(BF16)
        dn3g = _nt(dq, wg_ref[...])
        dwg_ref[...] += _tn(n3g, dq)
        dwp_ref[...] += _tn(pb, de)
        vec_ref[1:2, :] += _colsum(dn3g * n3)
        dh2_ref[...] = dh3 + _rms_bwd(dn3g * gp_ref[...], n3, r3)

    return pl.pallas_call(
        body, name="b4_ple_loss", grid=(T // tm,),
        in_specs=[_rows(tm, D_MODEL), _rows(tm, PLE_DIM), _rows(tm, D_MODEL), _resident((1, D_MODEL)),
                  _resident(w_ple_gate.shape), _resident(w_ple_proj.shape), _resident((1, D_MODEL))],
        out_specs=[_acc((1, 1)), _rows(tm, D_MODEL), _acc(w_ple_gate.shape), _acc(w_ple_proj.shape),
                   _acc((SUBLANES, D_MODEL))],
        out_shape=[jax.ShapeDtypeStruct((1, 1), F32), jax.ShapeDtypeStruct((T, D_MODEL), F32),
                   jax.ShapeDtypeStruct(w_ple_gate.shape, F32), jax.ShapeDtypeStruct(w_ple_proj.shape, F32),
                   jax.ShapeDtypeStruct((SUBLANES, D_MODEL), F32)],
        compiler_params=_params(("arbitrary",)),
    )(h2, p, target, ple_norm_g, w_ple_gate, w_ple_proj, final_g)


def _b3_ffn(dh2, h1, ff, norm2_g, w_ffn_in, w_ffn_out):
    T = h1.shape[0]
    tm = ROW_TILE

    def body(d_ref, h_ref, ff_ref, g_ref, wi_ref, wo_ref, dff_ref, dh1_ref, vec_ref):
        @pl.when(pl.program_id(0) == 0)
        def _():
            vec_ref[...] = jnp.zeros_like(vec_ref)

        dh2v = d_ref[...]
        dact = _nt(dh2v.astype(BF16), wo_ref[...])
        g_ff = ff_ref[:, :D_FF].astype(F32)
        u_ff = ff_ref[:, D_FF:].astype(F32)
        s = _sigmoid(g_ff)
        dg = (dact * u_ff * (s * (1.0 + g_ff * (1.0 - s)))).astype(BF16)
        du = (dact * (g_ff * s)).astype(BF16)
        dff_ref[:, :D_FF] = dg
        dff_ref[:, D_FF:] = du
        dv = _nt(dg, wi_ref[:, :D_FF]) + _nt(du, wi_ref[:, D_FF:])
        n2, r2 = _rms(h_ref[...])
        vec_ref[0:1, :] += _colsum(dv * n2)
        dh1_ref[...] = dh2v + _rms_bwd(dv * g_ref[...], n2, r2)

    return pl.pallas_call(
        body, name="b3_ffn", grid=(T // tm,),
        in_specs=[_rows(tm, D_MODEL), _rows(tm, D_MODEL), _rows(tm, 2 * D_FF), _resident((1, D_MODEL)),
                  _resident(w_ffn_in.shape), _resident(w_ffn_out.shape)],
        out_specs=[_rows(tm, 2 * D_FF), _rows(tm, D_MODEL), _acc((SUBLANES, D_MODEL))],
        out_shape=[jax.ShapeDtypeStruct((T, 2 * D_FF), BF16), jax.ShapeDtypeStruct((T, D_MODEL), F32),
                   jax.ShapeDtypeStruct((SUBLANES, D_MODEL), F32)],
        compiler_params=_params(("arbitrary",)),
    )(dh2, h1, ff, norm2_g, w_ffn_in, w_ffn_out)


def _wgrad(a, b, col_tile, name, tokens=WGRAD_TOKENS, after=None):
    T, K = a.shape
    N = b.shape[1]
    tk = min(T, tokens)

    def body(a_ref, b_ref, *rest):
        o_ref = rest[-1]

        @pl.when(pl.program_id(1) == 0)
        def _():
            o_ref[...] = jnp.zeros_like(o_ref)

        o_ref[...] += _tn(a_ref[...].astype(BF16), b_ref[...].astype(BF16))

    return pl.pallas_call(
        body, name=name, grid=(N // col_tile, T // tk),
        in_specs=[pl.BlockSpec((tk, K), lambda j, k: (k, 0)), pl.BlockSpec((tk, col_tile), lambda j, k: (k, j))]
        + ([] if after is None else [ANY]),
        out_specs=pl.BlockSpec((K, col_tile), lambda j, k: (0, j)),
        out_shape=jax.ShapeDtypeStruct((K, N), F32),
        compiler_params=_params(("arbitrary", "arbitrary")),
    )(a, b, *([] if after is None else [after]))


def _b2_gates(dh1, zt, yp, yl, b_gate, w_out):
    T = dh1.shape[0]
    tm = WIDE_TILE

    def body(d_ref, zt_ref, yp_ref, yl_ref, bg_ref, wo_ref, dzt_ref, dyp_ref, dyl_ref, dwo_ref, vec_ref):
        @pl.when(pl.program_id(0) == 0)
        def _():
            dwo_ref[...] = jnp.zeros_like(dwo_ref)
            vec_ref[...] = jnp.zeros_like(vec_ref)

        db = d_ref[...].astype(BF16)
        dm = _nt(db, wo_ref[...])
        gates = _sigmoid(zt_ref[...].astype(F32) + bg_ref[...])
        g0, g1 = gates[:, :D_MODEL], gates[:, D_MODEL:]
        y_pool, y_lru = yp_ref[...].astype(F32), yl_ref[...].astype(F32)
        dwo_ref[...] += _tn((g0 * y_pool + g1 * y_lru).astype(BF16), db)
        dz0 = dm * y_pool * g0 * (1.0 - g0)
        dz1 = dm * y_lru * g1 * (1.0 - g1)
        vec_ref[0:1, :] += _colsum(dz0)
        vec_ref[1:2, :] += _colsum(dz1)
        dzt_ref[:, :D_MODEL] = dz0.astype(BF16)
        dzt_ref[:, D_MODEL:] = dz1.astype(BF16)
        dyp_ref[...] = (dm * g0).astype(BF16)
        dyl_ref[...] = (dm * g1).astype(BF16)

    return pl.pallas_call(
        body, name="b2_gates", grid=(T // tm,),
        in_specs=[_rows(tm, D_MODEL), _rows(tm, 2 * D_MODEL), _rows(tm, D_MODEL), _rows(tm, D_MODEL),
                  _resident(b_gate.shape), _resident(w_out.shape)],
        out_specs=[_rows(tm, 2 * D_MODEL), _rows(tm, D_MODEL), _rows(tm, D_MODEL), _acc(w_out.shape),
                   _acc((SUBLANES, D_MODEL))],
        out_shape=[jax.ShapeDtypeStruct((T, 2 * D_MODEL), BF16), jax.ShapeDtypeStruct((T, D_MODEL), BF16),
                   jax.ShapeDtypeStruct((T, D_MODEL), BF16), jax.ShapeDtypeStruct(w_out.shape, F32),
                   jax.ShapeDtypeStruct((SUBLANES, D_MODEL), F32)],
        compiler_params=_params(("arbitrary",)),
    )(dh1, zt, yp, yl, b_gate, w_out)


def _b12_pool_in_proj(dyp, zp, dzl, dzg, dzt, x, dh1, norm1_g, w_in, pool_w, pool_scale, pool_proj):
    T = zp.shape[0]
    tt = ROW_TILE
    nt = T // tt

    def body(dy_ref, zp_ref, zph_ref, dzl_ref, dzg_ref, dzt_ref, x_ref, dh_ref, g1_ref, win_ref, pw_ref, ps_ref, pp_ref,
             dzp_ref, dx_ref, dpp_ref, dpw_ref, vec_ref, q_next):
        i = pl.program_id(0)
        ti = nt - 1 - i
        first_row = ti * tt

        @pl.when(i == 0)
        def _():
            dpp_ref[...] = jnp.zeros_like(dpp_ref)
            dpw_ref[...] = jnp.zeros_like(dpw_ref)
            vec_ref[...] = jnp.zeros_like(vec_ref)
            q_next[...] = jnp.zeros_like(q_next)

        du_parts = []

        def project(lo):
            k = max(i for i in range(4) if IN_SPLITS[i] <= lo)
            dz_ref = (None, dzl_ref, dzg_ref, dzt_ref)[k]
            at = lo - IN_SPLITS[k]
            part = _nt(dz_ref[:, at:at + PROJ_CHUNK], win_ref[:, lo:lo + PROJ_CHUNK])
            du_parts[:] = [part if not du_parts else du_parts[0] + part]

        mxu = _Interleaved(functools.partial(project, lo) for lo in range(IN_SPLITS[1], IN_SPLITS[4], PROJ_CHUNK))

        keep = (ti > 0).astype(F32)
        zp_cat = jnp.concatenate([zph_ref[...] * keep, zp_ref[...]], axis=0)
        pooled, mixed = _pool_forward(zp_cat, pw_ref, first_row, mxu.tick)
        dy = dy_ref[...]
        dpp_ref[...] += _tn((mixed * ps_ref[...]).astype(BF16), dy)
        mxu.tick(2)
        dms = _nt(dy, pp_ref[...])
        mxu.tick(2)
        vec_ref[0:1, :POOL_WIDTH] += _colsum(dms * mixed)
        dmixed = (dms * ps_ref[...]).astype(BF16)
        t_glob = first_row + _row_ids((tt, POOL_GROUP_DIM))
        dz, q_all, dpooled_pairs = [], [], []
        for p in range(len(POOL_WINDOWS) // 2):
            pair = slice(p * PAIR_DIM, (p + 1) * PAIR_DIM)
            dpw_ref[p] += _tn(pooled[:, pair].astype(BF16), dmixed[:, pair])
            dpooled_pairs.append(_nt(dmixed[:, pair], pw_ref[p]))
        dpooled_all = jnp.concatenate(dpooled_pairs, axis=1)
        for g, w in enumerate(POOL_WINDOWS):
            cols = slice(g * POOL_GROUP_DIM, (g + 1) * POOL_GROUP_DIM)
            dpooled = dpooled_all[:, cols]
            q = dpooled / jnp.minimum(t_glob + 1, w).astype(F32)
            q_all.append(q)
            s, k = jnp.concatenate([q, q_next[:, cols]], axis=0), 1
            while k < w:
                s = s + _shift_up(s, k)
                k *= 2
            dz.append(s[:tt] - dpooled)
            mxu.tick(2)
        dzp = jnp.concatenate(dz, axis=1).astype(BF16)
        dzp_ref[...] = dzp
        q_next[...] = jnp.concatenate([q[:POOL_HALO] for q in q_all], axis=1)
        mxu.flush()

        du = du_parts[0] + _nt(dzp, win_ref[:, IN_SPLITS[0]:IN_SPLITS[1]])
        n1, r1 = _rms(x_ref[...])
        vec_ref[1:2, :] += _colsum(du * n1)
        dx_ref[...] = dh_ref[...] + _rms_bwd(du * g1_ref[...], n1, r1)

    rev = functools.partial(_rows_rev, n_tiles=nt)
    res = [norm1_g, w_in, pool_w, pool_scale, pool_proj]
    return pl.pallas_call(
        body, name="b12_pool_in_proj", grid=(nt,),
        in_specs=[rev(tt, D_MODEL), rev(tt, POOL_WIDTH), _halo_before_rev(POOL_HALO, POOL_WIDTH, tt, nt),
                  rev(tt, D_MODEL), rev(tt, D_MODEL), rev(tt, 2 * D_MODEL), rev(tt, D_MODEL), rev(tt, D_MODEL)]
        + [_resident(w.shape) for w in res],
        out_specs=[rev(tt, POOL_WIDTH), rev(tt, D_MODEL), _acc(pool_proj.shape), _acc(pool_w.shape),
                   _acc((SUBLANES, D_MODEL))],
        out_shape=[jax.ShapeDtypeStruct((T, POOL_WIDTH), BF16), jax.ShapeDtypeStruct((T, D_MODEL), F32),
                   jax.ShapeDtypeStruct(pool_proj.shape, F32), jax.ShapeDtypeStruct(pool_w.shape, F32),
                   jax.ShapeDtypeStruct((SUBLANES, D_MODEL), F32)],
        scratch_shapes=[pltpu.VMEM((POOL_HALO, POOL_WIDTH), F32)],
        compiler_params=_params(("arbitrary",)),
    )(dyp, zp, zp, dzl, dzg, dzt, x, dh1, *res)


_V_CONVW, _V_CONVB, _V_BRG, _V_BIG, _V_LAM = 0, 4, 5, 6, 7


def _b2_lru(dyl, zl, zg, hs, xc_saved, r_saved, ig_saved, conv_w, w_rg, w_ig, lru_lambda, lru_proj):
    T = zl.shape[0]
    tt = ROW_TILE
    nt = T // tt
    n_groups = tt // SUBLANES

    def body(dy_ref, zl_ref, zlh_ref, zg_ref, hs_ref, hsh_ref, xc_ref, r_ref, ig_ref, cw_ref, wrg_ref, wig_ref,
             lam_ref, lp_ref, dzl_ref, dzg_ref, dlp_ref, dwrg_ref, dwig_ref, vec_ref,
             c_s, d_s, g_s, g_next, a_next, dxc_next):
        i = pl.program_id(0)
        ti = nt - 1 - i
        first_row = ti * tt

        @pl.when(i == 0)
        def _():
            dlp_ref[...] = jnp.zeros_like(dlp_ref)
            dwrg_ref[...] = jnp.zeros_like(dwrg_ref)
            dwig_ref[...] = jnp.zeros_like(dwig_ref)
            vec_ref[...] = jnp.zeros_like(vec_ref)
            g_next[...] = jnp.zeros_like(g_next)
            a_next[...] = jnp.zeros_like(a_next)
            dxc_next[...] = jnp.zeros_like(dxc_next)

        keep = (ti > 0).astype(F32)
        sp, dsp_dlam = _softplus_neg(lam_ref[...])
        hs = hs_ref[...]
        gelu, dgelu = _gelu_and_grad(zg_ref[...])
        dy = dy_ref[...]
        dlp_ref[...] += _tn((hs * gelu).astype(BF16), dy)
        dyl = _nt(dy, lp_ref[...])
        dzg_ref[...] = (dyl * hs * dgelu).astype(BF16)

        d_s[...] = dyl * gelu
        a_tile = jnp.exp(-LRU_C * r_ref[...].astype(F32) * sp)
        c_s[...] = _shift_up(jnp.concatenate([a_tile, a_next[...]], axis=0), 1)[:tt]
        a_next[...] = jnp.broadcast_to(a_tile[0:1, :], (SUBLANES, D_MODEL))
        rows8 = _row_ids((SUBLANES, D_MODEL))

        def group(k, carry):
            at = pl.ds(pl.multiple_of((n_groups - 1 - k) * SUBLANES, SUBLANES), SUBLANES)
            C, Dv = c_s[at, :], d_s[at, :]
            for s in (1, 2, 4):
                m = rows8 < SUBLANES - s
                Dv = jnp.where(m, C * pltpu.roll(Dv, SUBLANES - s, 0) + Dv, Dv)
                C = jnp.where(m, C * pltpu.roll(C, SUBLANES - s, 0), C)
            G = C * carry + Dv
            g_s[at, :] = G
            return jnp.broadcast_to(G[0:1, :], (SUBLANES, D_MODEL))

        g_next[...] = lax.fori_loop(0, n_groups, group, g_next[...])
        G = g_s[...]

        cw = cw_ref[...]
        zl_cat = jnp.concatenate([zlh_ref[...] * keep, zl_ref[...]], axis=0)
        xc, r, ig = xc_ref[...].astype(F32), r_ref[...].astype(F32), ig_ref[...].astype(F32)
        a, mult = _decay(r, sp, first_row)
        h_prev = _shift_down(jnp.concatenate([hsh_ref[...] * keep, hs_ref[...]], axis=0), 1)[SUBLANES:]
        t_glob = first_row + _row_ids((tt, D_MODEL))
        dmult = jnp.where(t_glob == 0, 0.0, G * ig * xc)
        dla = G * h_prev * a - dmult * (a * a) / mult
        vec_ref[_V_LAM:_V_LAM + 1, :] += _colsum(dla * r) * (-LRU_C) * dsp_dlam
        dpr = dla * (-LRU_C) * sp * r * (1.0 - r)
        dpi = G * mult * xc * ig * (1.0 - ig)
        vec_ref[_V_BRG:_V_BRG + 1, :] += _colsum(dpr)
        vec_ref[_V_BIG:_V_BIG + 1, :] += _colsum(dpi)
        dprb, dpib, xh = dpr.astype(BF16), dpi.astype(BF16), xc_ref[...]
        dxc_h = []
        for p in range(LRU_HEADS // 2):
            cols = slice(p * PAIR_DIM, (p + 1) * PAIR_DIM)
            dwrg_ref[p] += _tn(xh[:, cols], dprb[:, cols])
            dwig_ref[p] += _tn(xh[:, cols], dpib[:, cols])
            dxc_h.append(_nt(dprb[:, cols], wrg_ref[p]) + _nt(dpib[:, cols], wig_ref[p]))
        dxc = G * mult * ig + jnp.concatenate(dxc_h, axis=1)

        vec_ref[_V_CONVB:_V_CONVB + 1, :] += _colsum(dxc)
        dxc_cat = jnp.concatenate([dxc, dxc_next[...]], axis=0)
        dzl = cw[CONV_WIDTH - 1:CONV_WIDTH] * dxc
        for k in range(CONV_WIDTH):
            lag = CONV_WIDTH - 1 - k
            vec_ref[_V_CONVW + k:_V_CONVW + k + 1, :] += _colsum(dxc * _shift_down(zl_cat, lag)[SUBLANES:])
            if lag:
                dzl = dzl + cw[k:k + 1] * _shift_up(dxc_cat, lag)[:tt]
        dzl_ref[...] = dzl.astype(BF16)
        dxc_next[...] = dxc[:SUBLANES]

    res = [conv_w, w_rg, w_ig, lru_lambda, lru_proj]
    return pl.pallas_call(
        body, name="b2_lru", grid=(nt,),
        in_specs=[_rows_rev(tt, D_MODEL, nt), _rows_rev(tt, D_MODEL, nt), _halo_before_rev(SUBLANES, D_MODEL, tt, nt),
                  _rows_rev(tt, D_MODEL, nt), _rows_rev(tt, D_MODEL, nt), _halo_before_rev(SUBLANES, D_MODEL, tt, nt)]
        + [_rows_rev(tt, D_MODEL, nt)] * 3 + [_resident(w.shape) for w in res],
        out_specs=[_rows_rev(tt, D_MODEL, nt), _rows_rev(tt, D_MODEL, nt), _acc(lru_proj.shape), _acc(w_rg.shape),
                   _acc(w_ig.shape), _acc((SUBLANES, D_MODEL))],
        out_shape=[jax.ShapeDtypeStruct((T, D_MODEL), BF16), jax.ShapeDtypeStruct((T, D_MODEL), BF16),
                   jax.ShapeDtypeStruct(lru_proj.shape, F32), jax.ShapeDtypeStruct(w_rg.shape, F32),
                   jax.ShapeDtypeStruct(w_ig.shape, F32), jax.ShapeDtypeStruct((SUBLANES, D_MODEL), F32)],
        scratch_shapes=[pltpu.VMEM((tt, D_MODEL), F32)] * 3 + [pltpu.VMEM((SUBLANES, D_MODEL), F32)] * 3,
        compiler_params=_params(("arbitrary",)),
    )(dyl, zl, zl, zg, hs, hs, xc_saved, r_saved, ig_saved, *res)


def _row_tile(rows):
    for t in (512, 256, 128, 64, 32, 16, 8):
        if rows % t == 0:
            return t
    return rows


def _scalar_grid(grid, in_specs, out_specs):
    return pltpu.PrefetchScalarGridSpec(num_scalar_prefetch=1, grid=grid, in_specs=in_specs, out_specs=out_specs)


def _cast_into_block(w, by_rows, shard_j, name):
    R, C = w.shape
    tr = _row_tile(R)
    if by_rows:
        out_shape, out_map = (N_SHARDS * R, C), lambda i, j: (j[0] * (R // tr) + i, 0)
    else:
        out_shape, out_map = (R, N_SHARDS * C), lambda i, j: (i, j[0])

    def body(j_ref, w_ref, o_ref):
        o_ref[...] = w_ref[...].astype(BF16)

    return pl.pallas_call(
        body, name=name,
        grid_spec=_scalar_grid((R // tr,), [pl.BlockSpec((tr, C), lambda i, j: (i, 0))], pl.BlockSpec((tr, C), out_map)),
        out_shape=jax.ShapeDtypeStruct(out_shape, BF16),
        compiler_params=_params(("arbitrary",)),
    )(shard_j.reshape(1), w)


def _sum_cores(g, theirs, core, name):
    S, R, C = g.shape
    H = R // 2
    tr = _row_tile(H)
    nh = H // tr

    def body(c_ref, g_ref, t_ref, o_ref):
        o_ref[...] = (g_ref[...] + t_ref[...]).astype(BF16)

    half = pl.BlockSpec((None, tr, C), lambda s, i, c: (s, i, 0))
    return pl.pallas_call(
        body, name=name,
        grid_spec=_scalar_grid((S, nh), [pl.BlockSpec((None, tr, C), lambda s, i, c: (s, c[0] * nh + i, 0)), half], half),
        out_shape=jax.ShapeDtypeStruct((S, H, C), BF16),
        compiler_params=_params(("arbitrary", "arbitrary")),
    )(core.reshape(1), g, theirs)


def _sum_chips(sums, slots, by_rows, place, name):
    _, H, C = slots.shape
    tr = _row_tile(H)
    own_map = (lambda i, p: (p[0], i, 0)) if by_rows else (lambda i, p: (0, i, p[0]))

    def body(p_ref, s_ref, q_ref, o_ref):
        o_ref[...] = ((s_ref[...].astype(F32) + q_ref[0].astype(F32)) + q_ref[1].astype(F32)) + q_ref[2].astype(F32)

    return pl.pallas_call(
        body, name=name,
        grid_spec=_scalar_grid(
            (H // tr,),
            [pl.BlockSpec((None, tr, C), own_map), pl.BlockSpec((3, tr, C), lambda i, p: (0, i, 0))],
            pl.BlockSpec((None, tr, C), lambda i, p: (p[1], i, 0))),
        out_shape=jax.ShapeDtypeStruct((2, H, C), F32),
        compiler_params=_params(("arbitrary",)),
    )(place, sums, slots)


def _adamw(w, g, m, v, name):
    R, C = w.shape
    tr = _row_tile(R)
    c1 = 1.0 - ADAM_B1 ** ADAM_STEP
    c2 = 1.0 - ADAM_B2 ** ADAM_STEP

    def body(w_ref, g_ref, m_ref, v_ref, d_ref, nm_ref, nv_ref):
        gv = g_ref[...]
        nm = ADAM_B1 * m_ref[...] + (1.0 - ADAM_B1) * gv
        nv = ADAM_B2 * v_ref[...] + (1.0 - ADAM_B2) * (gv * gv)
        d_ref[...] = -ADAM_LR * ((nm / c1) / (jnp.sqrt(nv / c2) + ADAM_EPS) + ADAM_WD * w_ref[...])
        nm_ref[...] = nm
        nv_ref[...] = nv

    return pl.pallas_call(
        body, name=name, grid=(R // tr,),
        in_specs=[_rows(tr, C)] * 4, out_specs=[_rows(tr, C)] * 3,
        out_shape=[jax.ShapeDtypeStruct((R, C), F32)] * 3,
        compiler_params=_params(("arbitrary",)),
    )(w, g, m, v)


def _place():
    return lax.axis_index("x"), lax.axis_index("y"), lax.axis_index("c")


def _other_chips(x, y):
    return [(1 - x, y), (x, 1 - y), (1 - x, 1 - y)]


def _shard_block(ref, by_rows, R, C, j, half_rows=None):
    if half_rows is None:
        rows, r0 = R, 0
    else:
        rows = R // 2
        r0 = pl.multiple_of(half_rows * rows, 16)
    if by_rows:
        return ref.at[pl.ds(pl.multiple_of(j * R, 16) + r0, rows), :]
    return ref.at[pl.ds(r0, rows), pl.ds(pl.multiple_of(j * C, 128), C)]


def _all_gather_weights(gathered, shapes, by_rows, small):
    n = len(gathered)

    def body(*refs):
        small_in = refs[n]
        outs, small_out = refs[n + 1:2 * n + 1], refs[2 * n + 1]
        send_sems, recv_sems, local_sem = refs[2 * n + 2:]
        x, y, c = _place()
        me_j = 2 * x + y
        chips = _other_chips(x, y)
        sibling = (x, y, 1 - c)

        def block(i, j, half):
            R, C = shapes[i]
            return _shard_block(outs[i], by_rows[i], R, C, j, half)

        def ici(i, k, src_j):
            return pltpu.make_async_remote_copy(
                src_ref=block(i, src_j, c), dst_ref=block(i, src_j, c),
                send_sem=send_sems.at[6 * i + k], recv_sem=recv_sems.at[6 * i + k],
                device_id=(*chips[k], c), device_id_type=MESH)

        def relay(i, k, half):
            kj = 2 * chips[k][0] + chips[k][1]
            return pltpu.make_async_remote_copy(
                src_ref=block(i, kj, half), dst_ref=block(i, kj, half),
                send_sem=send_sems.at[6 * i + 3 + k], recv_sem=recv_sems.at[6 * i + 3 + k],
                device_id=sibling, device_id_type=MESH)

        def small_copy(k, src_j):
            cols = pl.ds(pl.multiple_of(src_j * 256, 128), 256)
            return pltpu.make_async_remote_copy(
                src_ref=small_in, dst_ref=small_out.at[:, cols],
                send_sem=send_sems.at[6 * n + k], recv_sem=recv_sems.at[6 * n + k],
                device_id=(*chips[k], c), device_id_type=MESH)

        sends = []
        for i in range(n):
            for k in range(3):
                cp = ici(i, k, me_j)
                cp.start()
                sends.append(cp)
        for k in range(3):
            cp = small_copy(k, me_j)
            cp.start()
            sends.append(cp)
        local = pltpu.make_async_copy(small_in, small_out.at[:, pl.ds(pl.multiple_of(me_j * 256, 128), 256)], local_sem)
        local.start()
        for i in range(n):
            for k in range(3):
                kj = 2 * chips[k][0] + chips[k][1]
                ici(i, k, kj).wait_recv()
                cp = relay(i, k, c)
                cp.start()
                sends.append(cp)
        for k in range(3):
            small_copy(k, 2 * chips[k][0] + chips[k][1]).wait_recv()
        for i in range(n):
            for k in range(3):
                relay(i, k, 1 - c).wait_recv()
        for cp in sends:
            cp.wait_send()
        local.wait()

    out_shape = [jax.ShapeDtypeStruct(g.shape, BF16) for g in gathered]
    out_shape.append(jax.ShapeDtypeStruct((8, N_SHARDS * 256), F32))
    n_sems = 6 * n + 3
    return pl.pallas_call(
        body, name="all_gather_weights",
        in_specs=[ANY] * (n + 1), out_specs=[ANY] * (n + 1), out_shape=out_shape,
        input_output_aliases={i: i for i in range(n)},
        scratch_shapes=[pltpu.SemaphoreType.DMA((n_sems,)), pltpu.SemaphoreType.DMA((n_sems,)),
                        pltpu.SemaphoreType.DMA],
    )(*gathered, small)


def _core_exchange(grads, name):
    n = len(grads)

    def body(*refs):
        copies = _core_exchange_copies(refs[:n], refs[n:2 * n], refs[2 * n], refs[2 * n + 1])
        for cp in copies:
            cp.start()
        for cp in copies:
            cp.wait()

    return pl.pallas_call(
        body, name=name,
        in_specs=[ANY] * n, out_specs=[ANY] * n,
        out_shape=[jax.ShapeDtypeStruct((g.shape[0], g.shape[1] // 2, g.shape[2]), F32) for g in grads],
        scratch_shapes=[pltpu.SemaphoreType.DMA((n,))] * 2,
    )(*grads)


HBM = pl.BlockSpec(memory_space=pltpu.HBM)
SEM = pl.BlockSpec(memory_space=pltpu.SEMAPHORE)
TOKEN = jax.ShapeDtypeStruct((SUBLANES, 128), F32)


def _in_hbm(a):
    return pltpu.with_memory_space_constraint(a, pltpu.HBM)


def _split_params():
    return pltpu.CompilerParams(has_side_effects=pltpu.SideEffectType.DATAFLOW_SIDE_EFFECTING)


def _gather_rest_copies(refs, shapes, by_rows, send_sems, recv_sems):
    x, y, c = _place()
    me_j = 2 * x + y
    chips = _other_chips(x, y)
    pairs = []
    for i, ref in enumerate(refs):
        R, C = shapes[i]
        for k in range(3):
            kj = 2 * chips[k][0] + chips[k][1]

            def copy(j, ref=ref, i=i, k=k, R=R, C=C):
                blk = _shard_block(ref, by_rows[i], R, C, j)
                return pltpu.make_async_remote_copy(
                    src_ref=blk, dst_ref=blk, send_sem=send_sems.at[3 * i + k], recv_sem=recv_sems.at[3 * i + k],
                    device_id=(*chips[k], c), device_id_type=MESH)

            pairs.append((copy(me_j), copy(kj)))
    return pairs


def _gather_rest_start(gathered, shapes, by_rows, after):
    n = len(gathered)

    def body(*refs):
        ins = refs[:n]
        send_sems, recv_sems = refs[n + 1], refs[n + 2]
        token = refs[-1]
        for mine, _ in _gather_rest_copies(ins, shapes, by_rows, send_sems, recv_sems):
            mine.start()
        token[...] = jnp.zeros_like(token)

    out = pl.pallas_call(
        body, name="gather_rest_start",
        out_shape=(pltpu.SemaphoreType.DMA((3 * n,)), pltpu.SemaphoreType.DMA((3 * n,)),
                   *[pltpu.HBM(g.shape, g.dtype) for g in gathered], TOKEN),
        in_specs=[HBM] * n + [ANY], out_specs=(SEM, SEM, *[HBM] * n, pl.BlockSpec(memory_space=pltpu.VMEM)),
        input_output_aliases={i: 2 + i for i in range(n)},
        compiler_params=_split_params(),
    )(*[_in_hbm(g) for g in gathered], after)
    return out[0], out[1], out[2:2 + n], out[-1]


def _gather_rest_wait(send_sems, recv_sems, gathered, shapes, by_rows, after):
    n = len(gathered)

    def body(*refs):
        ins = refs[:n]
        send, recv = refs[n], refs[n + 1]
        for mine, theirs in _gather_rest_copies(ins, shapes, by_rows, send, recv):
            mine.wait_send()
            theirs.wait_recv()

    return pl.pallas_call(
        body, name="gather_rest_wait",
        out_shape=tuple(pltpu.HBM(g.shape, g.dtype) for g in gathered),
        in_specs=[HBM] * n + [SEM, SEM, ANY], out_specs=tuple([HBM] * n),
        input_output_aliases={i: i for i in range(n)},
        compiler_params=_split_params(),
    )(*gathered, send_sems, recv_sems, after)


def _chip_exchange_copies(ins, slots, dims, by_rows, send_sems, recv_sems):
    x, y, c = _place()
    chips = _other_chips(x, y)
    pairs = []
    for i in range(len(ins)):
        for k in range(3):
            kj = 2 * chips[k][0] + chips[k][1]
            if by_rows[i]:
                src = ins[i].at[kj]
            else:
                src = ins[i].at[0, :, pl.ds(pl.multiple_of(kj * dims[i][1], 128), dims[i][1])]
            cp = pltpu.make_async_remote_copy(
                src_ref=src, dst_ref=slots[i].at[k], send_sem=send_sems.at[3 * i + k], recv_sem=recv_sems.at[3 * i + k],
                device_id=(*chips[k], c), device_id_type=MESH)
            pairs.append((cp, cp))
    return pairs


def _exchange_dims(sums, by_rows):
    return [(s.shape[1], s.shape[2]) if by_rows[i] else (s.shape[1], s.shape[2] // N_SHARDS) for i, s in enumerate(sums)]


def _chip_exchange_start(sums, by_rows, tag):
    n = len(sums)
    sums = list(sums)
    dims = _exchange_dims(sums, by_rows)
    slots = [lax.empty((3, h, cc), BF16) for h, cc in dims]

    def body(*refs):
        ins, land = refs[:n], refs[n:2 * n]
        send_sems, recv_sems = refs[2 * n], refs[2 * n + 1]
        token = refs[-1]
        for cp, _ in _chip_exchange_copies(ins, land, dims, by_rows, send_sems, recv_sems):
            cp.start()
        token[...] = jnp.zeros_like(token)

    out = pl.pallas_call(
        body, name="grad_chip_exchange_start_" + tag,
        out_shape=(pltpu.SemaphoreType.DMA((3 * n,)), pltpu.SemaphoreType.DMA((3 * n,)),
                   *[pltpu.HBM(a.shape, a.dtype) for a in sums + slots], TOKEN),
        in_specs=[HBM] * (2 * n), out_specs=(SEM, SEM, *[HBM] * (2 * n), pl.BlockSpec(memory_space=pltpu.VMEM)),
        input_output_aliases={i: 2 + i for i in range(2 * n)},
        compiler_params=_split_params(),
    )(*[_in_hbm(a) for a in sums + slots])
    return out[0], out[1], out[2:2 + n], out[2 + n:2 + 2 * n], out[-1]


def _chip_exchange_wait(send_sems, recv_sems, sums, slots, by_rows, after, tag):
    n = len(sums)
    sums, slots = list(sums), list(slots)
    dims = _exchange_dims(sums, by_rows)

    def body(*refs):
        ins, land = refs[:n], refs[n:2 * n]
        send, recv = refs[2 * n], refs[2 * n + 1]
        for cp, _ in _chip_exchange_copies(ins, land, dims, by_rows, send, recv):
            cp.wait_send()
            cp.wait_recv()

    out = pl.pallas_call(
        body, name="grad_chip_exchange_wait_" + tag,
        out_shape=tuple(pltpu.HBM(a.shape, a.dtype) for a in sums + slots),
        in_specs=[HBM] * (2 * n) + [SEM, SEM, ANY], out_specs=tuple([HBM] * (2 * n)),
        input_output_aliases={i: i for i in range(2 * n)},
        compiler_params=_split_params(),
    )(*sums, *slots, send_sems, recv_sems, after)
    return out[:n], out[n:]


def _core_exchange_copies(ins, theirs, send_sems, recv_sems):
    x, y, c = _place()
    copies = []
    for i in range(len(ins)):
        H = ins[i].shape[1] // 2
        copies.append(pltpu.make_async_remote_copy(
            src_ref=ins[i].at[:, pl.ds(pl.multiple_of((1 - c) * H, 8), H), :], dst_ref=theirs[i],
            send_sem=send_sems.at[i], recv_sem=recv_sems.at[i], device_id=(x, y, 1 - c), device_id_type=MESH))
    return copies


def _core_exchange_start(grads):
    n = len(grads)
    grads = list(grads)
    theirs = [lax.empty((g.shape[0], g.shape[1] // 2, g.shape[2]), F32) for g in grads]

    def body(*refs):
        for cp in _core_exchange_copies(refs[:n], refs[n:2 * n], refs[2 * n], refs[2 * n + 1]):
            cp.start()
        refs[-1][...] = jnp.zeros_like(refs[-1])

    out = pl.pallas_call(
        body, name="grad_core_exchange_start",
        out_shape=(pltpu.SemaphoreType.DMA((n,)), pltpu.SemaphoreType.DMA((n,)),
                   *[pltpu.HBM(a.shape, a.dtype) for a in grads + theirs], TOKEN),
        in_specs=[HBM] * (2 * n), out_specs=(SEM, SEM, *[HBM] * (2 * n), pl.BlockSpec(memory_space=pltpu.VMEM)),
        input_output_aliases={i: 2 + i for i in range(2 * n)},
        compiler_params=_split_params(),
    )(*[_in_hbm(a) for a in grads + theirs])
    return out[0], out[1], out[2:2 + n], out[2 + n:2 + 2 * n], out[-1]


def _core_exchange_wait(send_sems, recv_sems, grads, theirs, after):
    n = len(grads)
    grads, theirs = list(grads), list(theirs)

    def body(*refs):
        for cp in _core_exchange_copies(refs[:n], refs[n:2 * n], refs[2 * n], refs[2 * n + 1]):
            cp.wait_send()
            cp.wait_recv()

    out = pl.pallas_call(
        body, name="grad_core_exchange_wait",
        out_shape=tuple(pltpu.HBM(a.shape, a.dtype) for a in grads + theirs),
        in_specs=[HBM] * (2 * n) + [SEM, SEM, ANY], out_specs=tuple([HBM] * (2 * n)),
        input_output_aliases={i: i for i in range(2 * n)},
        compiler_params=_split_params(),
    )(*grads, *theirs, send_sems, recv_sems, after)
    return out[:n], out[n:]


def _core_share(reduced, tag):
    n = len(reduced)

    def body(*refs):
        outs = refs[n:2 * n]
        send_sems, recv_sems = refs[2 * n:]
        x, y, c = _place()
        copies = []
        for i in range(n):
            cp = pltpu.make_async_remote_copy(
                src_ref=outs[i].at[c], dst_ref=outs[i].at[c], send_sem=send_sems.at[i], recv_sem=recv_sems.at[i],
                device_id=(x, y, 1 - c), device_id_type=MESH)
            cp.start()
            copies.append(cp)
        for cp in copies:
            cp.wait()

    return pl.pallas_call(
        body, name="grad_core_share_" + tag,
        in_specs=[ANY] * n, out_specs=[ANY] * n,
        out_shape=[jax.ShapeDtypeStruct(r.shape, F32) for r in reduced],
        input_output_aliases={i: i for i in range(n)},
        scratch_shapes=[pltpu.SemaphoreType.DMA((n,))] * 2,
    )(*reduced)


def _small_exchange_copies(pack_ref, slots_ref, send_sems, recv_sems):
    x, y, c = _place()
    peers = [(px, py, pc) for px in (x, 1 - x) for py in (y, 1 - y) for pc in (c, 1 - c)][1:]
    pairs = []
    for k, peer in enumerate(peers):
        def copy(sender, k=k, peer=peer):
            return pltpu.make_async_remote_copy(
                src_ref=pack_ref, dst_ref=slots_ref.at[4 * sender[0] + 2 * sender[1] + sender[2]],
                send_sem=send_sems.at[k], recv_sem=recv_sems.at[k], device_id=peer, device_id_type=MESH)

        pairs.append((copy((x, y, c)), copy(peer)))
    return pairs


def _small_exchange_start(pack):
    slots = lax.empty((N_DEV,) + pack.shape, F32)

    def body(pack_ref, slots_ref, send_sems, recv_sems, pack_thru, slots_thru, token):
        for mine, _ in _small_exchange_copies(pack_ref, slots_ref, send_sems, recv_sems):
            mine.start()
        token[...] = jnp.zeros_like(token)

    return pl.pallas_call(
        body, name="grad_small_exchange_start",
        out_shape=(pltpu.SemaphoreType.DMA((N_DEV - 1,)), pltpu.SemaphoreType.DMA((N_DEV - 1,)),
                   pltpu.HBM(pack.shape, F32), pltpu.HBM(slots.shape, F32), TOKEN),
        in_specs=[HBM, HBM], out_specs=(SEM, SEM, HBM, HBM, pl.BlockSpec(memory_space=pltpu.VMEM)),
        input_output_aliases={0: 2, 1: 3},
        compiler_params=_split_params(),
    )(_in_hbm(pack), _in_hbm(slots))


def _small_exchange_wait(send_sems, recv_sems, pack, slots, after):
    def body(pack_ref, slots_ref, send, recv, after_ref, pack_thru, slots_thru):
        for mine, theirs in _small_exchange_copies(pack_ref, slots_ref, send, recv):
            mine.wait_send()
            theirs.wait_recv()

    return pl.pallas_call(
        body, name="grad_small_exchange_wait",
        out_shape=(pltpu.HBM(pack.shape, F32), pltpu.HBM(slots.shape, F32)),
        in_specs=[HBM, HBM, SEM, SEM, ANY], out_specs=(HBM, HBM),
        input_output_aliases={0: 0, 1: 1},
        compiler_params=_split_params(),
    )(pack, slots, send_sems, recv_sems, after)


def _sum_small(pack, slots, me):
    R, C = pack.shape
    tr = _row_tile(R)

    def body(me_ref, p_ref, q_ref, o_ref):
        acc = jnp.where(me_ref[0] == 0, p_ref[...], q_ref[0])
        for d in range(1, N_DEV):
            acc = acc + jnp.where(me_ref[0] == d, p_ref[...], q_ref[d])
        o_ref[...] = acc

    return pl.pallas_call(
        body, name="sum_small",
        grid_spec=_scalar_grid((R // tr,), [pl.BlockSpec((tr, C), lambda i, m: (i, 0)),
                                            pl.BlockSpec((N_DEV, tr, C), lambda i, m: (0, i, 0))],
                               pl.BlockSpec((tr, C), lambda i, m: (i, 0))),
        out_shape=jax.ShapeDtypeStruct((R, C), F32),
        compiler_params=_params(("arbitrary",)),
    )(me.reshape(1), pack, slots)


def _pack_rows(parts, rows):
    flat = jnp.concatenate([a.reshape(-1) for a in parts])
    return jnp.pad(flat, (0, rows * 128 - flat.shape[0])).reshape(rows, 128)


def _unpack_rows(pack, shapes):
    flat = pack.reshape(-1)
    out, at = [], 0
    for s in shapes:
        size = 1
        for d in s:
            size *= d
        out.append(flat[at:at + size].reshape(s))
        at += size
    return out


def kernel(x, p, norm1_g, w_in, b_gate, pool_w, pool_scale, pool_proj, conv_w, conv_b, w_rg, b_rg, w_ig, b_ig, lru_lambda, lru_proj, w_out, norm2_g, w_ffn_in, w_ffn_out, ple_norm_g, w_ple_gate, w_ple_proj, final_g, loss_target, m_norm1_g, m_w_in, m_b_gate, m_pool_w, m_pool_scale, m_pool_proj, m_conv_w, m_conv_b, m_w_rg, m_b_rg, m_w_ig, m_b_ig, m_lru_lambda, m_lru_proj, m_w_out, m_norm2_g, m_w_ffn_in, m_w_ffn_out, m_ple_norm_g, m_w_ple_gate, m_w_ple_proj, m_final_g, v_norm1_g, v_w_in, v_b_gate, v_pool_w, v_pool_scale, v_pool_proj, v_conv_w, v_conv_b, v_w_rg, v_b_rg, v_w_ig, v_b_ig, v_lru_lambda, v_lru_proj, v_w_out, v_norm2_g, v_w_ffn_in, v_w_ffn_out, v_ple_norm_g, v_w_ple_gate, v_w_ple_proj, v_final_g):
    weights = dict(norm1_g=norm1_g, w_in=w_in, b_gate=b_gate, pool_w=pool_w, pool_scale=pool_scale,
                   pool_proj=pool_proj, conv_w=conv_w, conv_b=conv_b, w_rg=w_rg, b_rg=b_rg, w_ig=w_ig, b_ig=b_ig,
                   lru_lambda=lru_lambda, lru_proj=lru_proj, w_out=w_out, norm2_g=norm2_g, w_ffn_in=w_ffn_in,
                   w_ffn_out=w_ffn_out, ple_norm_g=ple_norm_g, w_ple_gate=w_ple_gate, w_ple_proj=w_ple_proj,
                   final_g=final_g)
    m_in = dict(norm1_g=m_norm1_g, w_in=m_w_in, b_gate=m_b_gate, pool_w=m_pool_w, pool_scale=m_pool_scale,
                pool_proj=m_pool_proj, conv_w=m_conv_w, conv_b=m_conv_b, w_rg=m_w_rg, b_rg=m_b_rg, w_ig=m_w_ig,
                b_ig=m_b_ig, lru_lambda=m_lru_lambda, lru_proj=m_lru_proj, w_out=m_w_out, norm2_g=m_norm2_g,
                w_ffn_in=m_w_ffn_in, w_ffn_out=m_w_ffn_out, ple_norm_g=m_ple_norm_g, w_ple_gate=m_w_ple_gate,
                w_ple_proj=m_w_ple_proj, final_g=m_final_g)
    v_in = dict(norm1_g=v_norm1_g, w_in=v_w_in, b_gate=v_b_gate, pool_w=v_pool_w, pool_scale=v_pool_scale,
                pool_proj=v_pool_proj, conv_w=v_conv_w, conv_b=v_conv_b, w_rg=v_w_rg, b_rg=v_b_rg, w_ig=v_w_ig,
                b_ig=v_b_ig, lru_lambda=v_lru_lambda, lru_proj=v_lru_proj, w_out=v_w_out, norm2_g=v_norm2_g,
                w_ffn_in=v_w_ffn_in, w_ffn_out=v_w_ffn_out, ple_norm_g=v_ple_norm_g, w_ple_gate=v_w_ple_gate,
                w_ple_proj=v_w_ple_proj, final_g=v_final_g)
    names = list(weights)
    big = ["w_in", "pool_proj", "lru_proj", "w_out", "w_ffn_in", "w_ffn_out", "w_ple_gate", "w_ple_proj"]
    by_rows = [n in ("lru_proj", "w_out", "w_ffn_out", "w_ple_gate") for n in big]
    small = [n for n in names if n not in big]

    shard_j = 2 * lax.axis_index("x") + lax.axis_index("y")
    T = x.shape[1]
    xs, ps, tgt = x[0], p[0, 0], loss_target[0]

    small_local = jnp.concatenate([b_gate[0], conv_w[0], jnp.zeros((2, 256), F32)], axis=0)
    core = lax.axis_index("c").astype(jnp.int32)
    place = jnp.stack([shard_j, core]).astype(jnp.int32)
    rows_of = dict(zip(big, by_rows))
    shard_shape = {n: weights[n].shape[1:] for n in big}
    blocks = {n: _cast_into_block(weights[n][0], rows_of[n], place[0], "cast_" + n) for n in big}
    early, late = big[:4], big[4:]
    gathered = _all_gather_weights([blocks[n] for n in early], [shard_shape[n] for n in early],
                                   [rows_of[n] for n in early], small_local)
    full = dict(zip(early, gathered[:-1]))
    late_send, late_recv, late_bufs, late_token = _gather_rest_start(
        [blocks[n] for n in late], [shard_shape[n] for n in late], [rows_of[n] for n in late], gathered[-1])
    b_gate_full = gathered[-1][0:2].reshape(1, 2 * D_MODEL)
    conv_w_full = gathered[-1][2:6]
    pool_w_1, w_rg_1, w_ig_1 = [w[0].astype(BF16) for w in (pool_w, w_rg, w_ig)]
    pool_w_b, w_rg_b, w_ig_b = [_pair_blocks(w) for w in (pool_w_1, w_rg_1, w_ig_1)]
    b_rg_row, b_ig_row = b_rg.reshape(1, D_MODEL), b_ig.reshape(1, D_MODEL)
    final_row = final_g.reshape(1, D_MODEL)

    zp, zl, zg, zt, u, h1, hs, yp, yl, xc_saved, r_saved, ig_saved = _f12_mixer(
        xs, norm1_g + late_token[0, 0], full["w_in"], b_gate_full, pool_w_1, pool_scale, full["pool_proj"],
        conv_w_full, conv_b, w_rg_1, b_rg_row, w_ig_1, b_ig_row, lru_lambda, full["lru_proj"], full["w_out"])
    full.update(zip(late, _gather_rest_wait(late_send, late_recv, late_bufs, [shard_shape[n] for n in late],
                                            [rows_of[n] for n in late], h1)))
    h2, v, ff, act = _f3_ffn(h1, norm2_g, full["w_ffn_in"], full["w_ffn_out"])

    loss_sum, dh2, g_ple_gate, g_ple_proj, vec4 = _b4_ple_loss(
        h2, ps, tgt, ple_norm_g, full["w_ple_gate"], full["w_ple_proj"], final_row)
    dff, dh1, vec3 = _b3_ffn(dh2, h1, ff, norm2_g, full["w_ffn_in"], full["w_ffn_out"])
    g_ffn_in = _wgrad(v, dff, 2 * D_FF // N_SHARDS, "wgrad_ffn_in")
    g_ffn_out = _wgrad(act, dh2, D_MODEL, "wgrad_ffn_out", tokens=WGRAD_TOKENS // 2)

    def stack(n, g):
        return g.reshape(N_SHARDS, g.shape[0] // N_SHARDS, g.shape[1]) if rows_of[n] else g[None]

    def chip_sums_of(group, grads_of, tag):
        stacked = [stack(n, grads_of[n]) for n in group]
        theirs = _core_exchange(stacked, "grad_core_exchange_" + tag)
        return [_sum_cores(g, t, core, "sum_cores_" + n) for g, t, n in zip(stacked, theirs, group)]

    late_rows = [rows_of[n] for n in late]
    late_grads = dict(w_ffn_in=g_ffn_in, w_ffn_out=g_ffn_out, w_ple_gate=g_ple_gate, w_ple_proj=g_ple_proj)
    cx_send, cx_recv, late_stacked, late_theirs, cx_token = _core_exchange_start(
        [stack(n, late_grads[n]) for n in late])
    dzt, dyp, dyl, g_w_out, vec_g = _b2_gates(dh1, zt, yp, yl, b_gate_full + cx_token[0, 0], full["w_out"])
    late_stacked, late_theirs = _core_exchange_wait(cx_send, cx_recv, late_stacked, late_theirs, dzt)
    late_sums = [_sum_cores(g, t, core, "sum_cores_" + n) for g, t, n in zip(late_stacked, late_theirs, late)]
    ex_send, ex_recv, late_sums, late_slots, ex_token = _chip_exchange_start(late_sums, late_rows, "late")
    dzl, dzg, g_lru_proj, g_w_rg, g_w_ig, vec_l = _b2_lru(
        dyl, zl, zg, hs, xc_saved, r_saved, ig_saved, conv_w_full, w_rg_b, w_ig_b, lru_lambda + ex_token[0, 0],
        full["lru_proj"])
    dzp, grad_x, g_pool_proj, g_pool_w, vec_p = _b12_pool_in_proj(
        dyp, zp, dzl, dzg, dzt, xs, dh1, norm1_g, full["w_in"], pool_w_b, pool_scale, full["pool_proj"])
    small_full = dict(
        norm1_g=vec_p[1], b_gate=vec_g[0:2], pool_w=_unpair_blocks(g_pool_w), pool_scale=vec_p[0, :POOL_WIDTH],
        conv_w=vec_l[_V_CONVW:_V_CONVW + CONV_WIDTH], conv_b=vec_l[_V_CONVB], w_rg=_unpair_blocks(g_w_rg),
        b_rg=vec_l[_V_BRG], w_ig=_unpair_blocks(g_w_ig), b_ig=vec_l[_V_BIG], lru_lambda=vec_l[_V_LAM], norm2_g=vec3[0], ple_norm_g=vec4[1],
        final_g=vec4[0])
    full_shapes = [small_full[n].shape for n in small]
    n_full = sum(int(small_full[n].size) for n in small)
    rows_full = -(-n_full // (128 * ROW_TILE)) * ROW_TILE
    sm_send, sm_recv, sm_pack, sm_slots, sm_token = _small_exchange_start(
        _pack_rows([small_full[n] for n in small], rows_full))
    g_w_in = jnp.concatenate([
        _wgrad(u, dzp, POOL_WIDTH, "wgrad_in_pool", 2 * WGRAD_TOKENS, after=sm_token),
        _wgrad(u, dzl, D_MODEL, "wgrad_in_lru", 2 * WGRAD_TOKENS, after=sm_token),
        _wgrad(u, dzg, D_MODEL, "wgrad_in_gelu", 2 * WGRAD_TOKENS, after=sm_token),
        _wgrad(u, dzt, D_MODEL, "wgrad_in_gate", 2 * WGRAD_TOKENS, after=sm_token)], axis=1)

    loss = lax.psum(loss_sum[0, 0] * (0.5 / D_MODEL), ("x", "y", "c"))

    early_rows = [rows_of[n] for n in early]
    early_sums = chip_sums_of(early, dict(w_in=g_w_in, pool_proj=g_pool_proj, lru_proj=g_lru_proj, w_out=g_w_out),
                              "early")
    e_send, e_recv, early_sums, early_slots, e_token = _chip_exchange_start(early_sums, early_rows, "early")
    grads, deltas, new_m, new_v = {}, {}, {}, {}

    def finish(group, sums, slots, tag):
        reduced = _core_share([_sum_chips(s, q, rows_of[n], place, "sum_chips_" + n)
                               for s, q, n in zip(sums, slots, group)], tag)
        for n, r in zip(group, reduced):
            g = r.reshape(r.shape[0] * r.shape[1], r.shape[2])
            d, nm, nv = _adamw(weights[n][0], g, m_in[n][0], v_in[n][0], "adamw_" + n)
            grads[n], deltas[n], new_m[n], new_v[n] = g[None], d[None], nm[None], nv[None]

    late_sums, late_slots = _chip_exchange_wait(ex_send, ex_recv, late_sums, late_slots, late_rows, e_token, "late")
    finish(late, late_sums, late_slots, "late")

    sm_pack, sm_slots = _small_exchange_wait(sm_send, sm_recv, sm_pack, sm_slots, e_token)
    device = (4 * lax.axis_index("x") + 2 * lax.axis_index("y") + lax.axis_index("c")).astype(jnp.int32)
    summed = dict(zip(small, _unpack_rows(_sum_small(sm_pack, sm_slots, device), full_shapes)))
    summed["b_gate"] = lax.dynamic_slice_in_dim(summed["b_gate"], shard_j * 256, 256, axis=1)
    summed["conv_w"] = lax.dynamic_slice_in_dim(summed["conv_w"], shard_j * 256, 256, axis=1)
    local_shapes = [weights[n].shape for n in small]
    n_local = sum(int(weights[n].size) for n in small)
    rows_local = -(-n_local // (128 * ROW_TILE)) * ROW_TILE
    packs = [_pack_rows([src[n] for n in small], rows_local) for src in (weights, summed, m_in, v_in)]
    d_s, nm_s, nv_s = _adamw(*packs, "adamw_small")
    for dst, pack in ((grads, packs[1]), (deltas, d_s), (new_m, nm_s), (new_v, nv_s)):
        dst.update(zip(small, _unpack_rows(pack, local_shapes)))

    done = d_s[:SUBLANES]
    for n in late:
        done = done + deltas[n][0, :SUBLANES, :128]
    early_sums, early_slots = _chip_exchange_wait(e_send, e_recv, early_sums, early_slots, early_rows, done, "early")
    finish(early, early_sums, early_slots, "early")

    return (loss, grad_x[None], *[grads[n] for n in names], *[deltas[n] for n in names],
            *[new_m[n] for n in names], *[new_v[n] for n in names])
```

```python
import functools

import jax
import jax.numpy as jnp
from jax import lax
from jax.experimental import pallas as pl
from jax.experimental.pallas import tpu as pltpu

F32 = jnp.float32
BF16 = jnp.bfloat16

D_MODEL = 1024
POOL_WIDTH = 512
POOL_GROUP_DIM = 128
POOL_WINDOWS = (2, 4, 8, 16)
POOL_HALO = 16
LRU_HEADS = 8
LRU_HEAD_DIM = 128
CONV_WIDTH = 4
LRU_C = 8.0
D_FF = 2816
PLE_DIM = 256
RMS_EPS = 1e-6
N_SHARDS = 4
N_DEV = 8

ADAM_LR = 0.001
ADAM_B1 = 0.9
ADAM_B2 = 0.999
ADAM_EPS = 1e-08
ADAM_WD = 0.01
ADAM_STEP = 10

ROW_TILE = 256
WIDE_TILE = 512
WGRAD_TOKENS = 2048
SUBLANES = 8
VMEM_LIMIT = 56 * 1024 * 1024
MESH = pl.DeviceIdType.MESH
ANY = pl.BlockSpec(memory_space=pl.ANY)


def _params(semantics=None):
    return pltpu.CompilerParams(dimension_semantics=semantics, vmem_limit_bytes=VMEM_LIMIT)


def _resident(shape):
    n = len(shape)
    return pl.BlockSpec(shape, lambda *_: (0,) * n, pipeline_mode=pl.Buffered(1))


def _acc(shape):
    n = len(shape)
    return pl.BlockSpec(shape, lambda *_: (0,) * n)


def _rows(tile, cols):
    return pl.BlockSpec((tile, cols), lambda i: (i, 0))


def _rows_rev(tile, cols, n_tiles):
    return pl.BlockSpec((tile, cols), lambda i: (n_tiles - 1 - i, 0))


def _halo_before_rev(rows, cols, tile, n_tiles):
    per = tile // rows
    return pl.BlockSpec((rows, cols), lambda i: (jnp.maximum((n_tiles - 1 - i) * per - 1, 0), 0))


def _nn(a, b):
    return jnp.dot(a, b, preferred_element_type=F32)


def _nt(a, b):
    return lax.dot_general(a, b, (((1,), (1,)), ((), ())), preferred_element_type=F32)


def _tn(a, b):
    return lax.dot_general(a, b, (((0,), (0,)), ((), ())), preferred_element_type=F32)


def _rms(x):
    r = lax.rsqrt(jnp.mean(x * x, axis=-1, keepdims=True) + RMS_EPS)
    return x * r, r


def _rms_bwd(dn, n, r):
    return r * (dn - n * jnp.mean(dn * n, axis=-1, keepdims=True))


def _sigmoid(x):
    return 0.5 * jnp.tanh(0.5 * x) + 0.5


_GELU_C = 0.7978845608028654
_GELU_A = 0.044715


def _gelu(x):
    t = jnp.tanh(_GELU_C * (x + _GELU_A * x * x * x))
    return 0.5 * x * (1.0 + t)


def _gelu_and_grad(x):
    x2 = x * x
    t = jnp.tanh(_GELU_C * (x + _GELU_A * x2 * x))
    cdf = 0.5 * (1.0 + t)
    grad = cdf + 0.5 * x * (1.0 - t * t) * _GELU_C * (1.0 + 3.0 * _GELU_A * x2)
    return x * cdf, grad


def _softplus_neg(lam):
    e = jnp.exp(-jnp.abs(lam))
    sp = jnp.maximum(-lam, 0.0) + jnp.log1p(e)
    return sp, -_sigmoid(-lam)


def _colsum(v):
    return jnp.sum(v, axis=0, keepdims=True)


def _row_ids(shape):
    return lax.broadcasted_iota(jnp.int32, shape, 0)


def _shift_down(cat, k):
    return pltpu.roll(cat, k, 0) if k else cat


def _shift_up(cat, k):
    return pltpu.roll(cat, cat.shape[0] - k, 0) if k else cat


IN_SPLITS = (0, POOL_WIDTH, POOL_WIDTH + D_MODEL, POOL_WIDTH + 2 * D_MODEL, POOL_WIDTH + 4 * D_MODEL)
IN_WIDTHS = tuple(IN_SPLITS[k + 1] - IN_SPLITS[k] for k in range(4))
PROJ_CHUNK = 256
PAIR_DIM = 2 * LRU_HEAD_DIM


def _pair_blocks(w):
    zero = jnp.zeros_like(w[0::2])
    return jnp.concatenate([jnp.concatenate([w[0::2], zero], axis=2), jnp.concatenate([zero, w[1::2]], axis=2)], axis=1)


def _unpair_blocks(w):
    n, d2, _ = w.shape
    d = d2 // 2
    return jnp.stack([w[:, :d, :d], w[:, d:, d:]], axis=1).reshape(2 * n, d, d)


def _no_tick():
    pass


class _Interleaved:
    def __init__(self, pieces):
        self._pieces = iter(pieces)

    def tick(self, n=1):
        for _ in range(n):
            piece = next(self._pieces, None)
            if piece is not None:
                piece()

    def flush(self):
        for piece in self._pieces:
            piece()


def _pool_forward(zp_cat, pw_ref, first_row, tick=_no_tick):
    tt = zp_cat.shape[0] - POOL_HALO
    t_glob = first_row + _row_ids((tt, POOL_GROUP_DIM))
    pooled, mixed = [], []
    for g, w in enumerate(POOL_WINDOWS):
        cat = zp_cat[:, g * POOL_GROUP_DIM:(g + 1) * POOL_GROUP_DIM]
        s, k = cat, 1
        while k < w:
            s = s + _shift_down(s, k)
            k *= 2
        cnt = jnp.minimum(t_glob + 1, w).astype(F32)
        pooled.append(s[POOL_HALO:] / cnt - cat[POOL_HALO:])
        per = pw_ref.shape[-1] // POOL_GROUP_DIM
        if (g + 1) % per == 0:
            block = jnp.concatenate(pooled[-per:], axis=1).astype(BF16)
            mixed.append(_nn(block, pw_ref[g // per]))
        tick()
    return jnp.concatenate(pooled, axis=1), jnp.concatenate(mixed, axis=1)


def _lru_gates(zl_cat, conv_w, conv_b, wrg_ref, brg, wig_ref, big, sp, first_row, tick=_no_tick):
    xc = conv_w[CONV_WIDTH - 1:CONV_WIDTH] * zl_cat
    for k in range(1, CONV_WIDTH):
        xc = xc + conv_w[CONV_WIDTH - 1 - k:CONV_WIDTH - k] * _shift_down(zl_cat, k)
        tick()
    xc = xc[SUBLANES:] + conv_b
    xh = xc.astype(BF16)
    pr, pi = [], []
    width = wrg_ref.shape[-1]
    for p in range(D_MODEL // width):
        xs = xh[:, p * width:(p + 1) * width]
        pr.append(_nn(xs, wrg_ref[p]))
        pi.append(_nn(xs, wig_ref[p]))
    r = _sigmoid(jnp.concatenate(pr, axis=1) + brg)
    tick()
    ig = _sigmoid(jnp.concatenate(pi, axis=1) + big)
    tick()
    a, mult = _decay(r, sp, first_row, tick)
    tick()
    return xc, r, ig, a, mult


def _decay(r, sp, first_row, tick=_no_tick):
    a = jnp.exp(-LRU_C * r * sp)
    tick()
    mult = jnp.sqrt(jnp.maximum(1.0 - a * a, 0.0))
    t_glob = first_row + _row_ids(r.shape)
    return a, jnp.where(t_glob == 0, 1.0, mult)


def _f12_mixer(x, norm1_g, w_in, b_gate, pool_w, pool_scale, pool_proj, conv_w, conv_b, w_rg, b_rg, w_ig, b_ig,
               lru_lambda, lru_proj, w_out):
    T = x.shape[0]
    tt = ROW_TILE
    nt = T // tt
    n_groups = tt // SUBLANES
    proj_mid = IN_SPLITS[3] + D_MODEL // 2

    def body(xm_ref, x_ref, g1_ref, win_ref, bg_ref, pw_ref, ps_ref, pp_ref, cw_ref, cb_ref,
             wrg_ref, brg_ref, wig_ref, big_ref, lam_ref, lp_ref, wo_ref,
             zp_ref, zl_ref, zg_ref, zt_ref, u_ref, h1_ref, hs_ref, yp_ref, yl_ref, xc_ref, r_ref, ig_ref,
             zbuf, zp_halo, zl_halo, a_s, b_s, carry_s):
        s = pl.program_id(0)

        @pl.when(s == 0)
        def _():
            zbuf[1] = jnp.zeros((tt, IN_SPLITS[4]), F32)
            zp_halo[...] = jnp.zeros_like(zp_halo)
            zl_halo[...] = jnp.zeros_like(zl_halo)
            carry_s[...] = jnp.zeros_like(carry_s)

        z_new, z_old = zbuf.at[s % 2], zbuf.at[(s + 1) % 2]
        first = s <= 1
        first_row = jnp.maximum(s - 1, 0) * tt

        n1, _ = _rms(xm_ref[...])
        u = (n1 * g1_ref[...]).astype(BF16)
        u_ref[...] = u

        z_refs = (zp_ref, zl_ref, zg_ref, zt_ref)

        def project(lo):
            k = max(i for i in range(4) if IN_SPLITS[i] <= lo)
            part = _nn(u, win_ref[:, lo:lo + PROJ_CHUNK])
            z_new[:, lo:lo + PROJ_CHUNK] = part
            z_refs[k][:, lo - IN_SPLITS[k]:lo - IN_SPLITS[k] + PROJ_CHUNK] = part.astype(z_refs[k].dtype)

        before_scan = _Interleaved(functools.partial(project, lo) for lo in range(0, proj_mid, PROJ_CHUNK))
        after_scan = _Interleaved(functools.partial(project, lo) for lo in range(proj_mid, IN_SPLITS[4], PROJ_CHUNK))

        zp_cat = jnp.concatenate([jnp.where(first, 0.0, zp_halo[...]), z_old[:, IN_SPLITS[0]:IN_SPLITS[1]]], axis=0)
        _, mixed = _pool_forward(zp_cat, pw_ref, first_row, before_scan.tick)
        y_pool = _nn((mixed * ps_ref[...]).astype(BF16), pp_ref[...])

        sp, _ = _softplus_neg(lam_ref[...])
        zl_cat = jnp.concatenate([jnp.where(first, 0.0, zl_halo[...]), z_old[:, IN_SPLITS[1]:IN_SPLITS[2]]], axis=0)
        xc, r, ig, a, mult = _lru_gates(zl_cat, cw_ref[...], cb_ref[...], wrg_ref, brg_ref[...], wig_ref,
                                        big_ref[...], sp, first_row, before_scan.tick)
        a_s[...] = a
        b_s[...] = mult * ig * xc
        xc_ref[...] = xc.astype(BF16)
        r_ref[...] = r.astype(BF16)
        ig_ref[...] = ig.astype(BF16)
        before_scan.flush()

        rows8 = _row_ids((SUBLANES, D_MODEL))

        def group(g, carry):
            at = pl.ds(pl.multiple_of(g * SUBLANES, SUBLANES), SUBLANES)
            A, B = a_s[at, :], b_s[at, :]
            for s in (1, 2, 4):
                m = rows8 >= s
                B = jnp.where(m, A * pltpu.roll(B, s, 0) + B, B)
                A = jnp.where(m, A * pltpu.roll(A, s, 0), A)
            h = A * carry + B
            hs_ref[at, :] = h
            return jnp.broadcast_to(h[SUBLANES - 1:SUBLANES, :], (SUBLANES, D_MODEL))

        carry_s[...] = lax.fori_loop(0, n_groups, group, jnp.where(first, 0.0, carry_s[...]))
        gelu = _gelu(z_old[:, IN_SPLITS[2]:IN_SPLITS[3]])
        after_scan.tick(2)
        y_lru = _nn((hs_ref[...] * gelu).astype(BF16), lp_ref[...])

        gates = _sigmoid(z_old[:, IN_SPLITS[3]:IN_SPLITS[4]] + bg_ref[...])
        after_scan.tick(2)
        merged = gates[:, :D_MODEL] * y_pool + gates[:, D_MODEL:] * y_lru
        after_scan.flush()
        h1_ref[...] = x_ref[...] + _nn(merged.astype(BF16), wo_ref[...])
        yp_ref[...] = y_pool.astype(BF16)
        yl_ref[...] = y_lru.astype(BF16)
        zp_halo[...] = z_old[tt - POOL_HALO:, IN_SPLITS[0]:IN_SPLITS[1]]
        zl_halo[...] = z_old[tt - SUBLANES:, IN_SPLITS[1]:IN_SPLITS[2]]

    def ahead(cols):
        return pl.BlockSpec((tt, cols), lambda s: (jnp.minimum(s, nt - 1), 0))

    def behind(cols):
        return pl.BlockSpec((tt, cols), lambda s: (jnp.maximum(s - 1, 0), 0))

    res = [norm1_g, w_in, b_gate, pool_w, pool_scale, pool_proj, conv_w, conv_b, w_rg, b_rg, w_ig, b_ig, lru_lambda,
           lru_proj, w_out]
    return pl.pallas_call(
        body, name="f12_mixer", grid=(nt + 1,),
        in_specs=[ahead(D_MODEL), behind(D_MODEL)] + [_resident(w.shape) for w in res],
        out_specs=[ahead(w) for w in IN_WIDTHS] + [ahead(D_MODEL)] + [behind(D_MODEL)] * 7,
        out_shape=[jax.ShapeDtypeStruct((T, w), dt) for w, dt in zip(IN_WIDTHS, (F32, F32, F32, BF16))]
        + [jax.ShapeDtypeStruct((T, D_MODEL), BF16), jax.ShapeDtypeStruct((T, D_MODEL), F32),
           jax.ShapeDtypeStruct((T, D_MODEL), F32)] + [jax.ShapeDtypeStruct((T, D_MODEL), BF16)] * 5,
        scratch_shapes=[pltpu.VMEM((2, tt, IN_SPLITS[4]), F32), pltpu.VMEM((POOL_HALO, POOL_WIDTH), F32),
                        pltpu.VMEM((SUBLANES, D_MODEL), F32), pltpu.VMEM((tt, D_MODEL), F32),
                        pltpu.VMEM((tt, D_MODEL), F32), pltpu.VMEM((SUBLANES, D_MODEL), F32)],
        compiler_params=_params(("arbitrary",)),
    )(x, x, *res)


def _f3_ffn(h1, norm2_g, w_ffn_in, w_ffn_out):
    T = h1.shape[0]
    tm = ROW_TILE

    def body(h_ref, g_ref, wi_ref, wo_ref, h2_ref, v_ref, ff_ref, act_ref):
        h = h_ref[...]
        n, _ = _rms(h)
        v = (n * g_ref[...]).astype(BF16)
        v_ref[...] = v
        g_ff = _nn(v, wi_ref[:, :D_FF])
        u_ff = _nn(v, wi_ref[:, D_FF:])
        ff_ref[:, :D_FF] = g_ff.astype(BF16)
        ff_ref[:, D_FF:] = u_ff.astype(BF16)
        act = (g_ff * _sigmoid(g_ff) * u_ff).astype(BF16)
        act_ref[...] = act
        h2_ref[...] = h + _nn(act, wo_ref[...])

    return pl.pallas_call(
        body, name="f3_ffn", grid=(T // tm,),
        in_specs=[_rows(tm, D_MODEL), _resident((1, D_MODEL)), _resident(w_ffn_in.shape), _resident(w_ffn_out.shape)],
        out_specs=[_rows(tm, D_MODEL), _rows(tm, D_MODEL), _rows(tm, 2 * D_FF), _rows(tm, D_FF)],
        out_shape=[jax.ShapeDtypeStruct((T, D_MODEL), F32), jax.ShapeDtypeStruct((T, D_MODEL), BF16),
                   jax.ShapeDtypeStruct((T, 2 * D_FF), BF16), jax.ShapeDtypeStruct((T, D_FF), BF16)],
        compiler_params=_params(("arbitrary",)),
    )(h1, norm2_g, w_ffn_in, w_ffn_out)


def _b4_ple_loss(h2, p, target, ple_norm_g, w_ple_gate, w_ple_proj, final_g):
    T = h2.shape[0]
    tm = WIDE_TILE

    def body(h_ref, p_ref, t_ref, gp_ref, wg_ref, wp_ref, gf_ref, loss_ref, dh2_ref, dwg_ref, dwp_ref, vec_ref):
        @pl.when(pl.program_id(0) == 0)
        def _():
            loss_ref[...] = jnp.zeros_like(loss_ref)
            dwg_ref[...] = jnp.zeros_like(dwg_ref)
            dwp_ref[...] = jnp.zeros_like(dwp_ref)
            vec_ref[...] = jnp.zeros_like(vec_ref)

        h2v = h_ref[...]
        n3, r3 = _rms(h2v)
        n3g = (n3 * gp_ref[...]).astype(BF16)
        pb = p_ref[...].astype(BF16)
        q = _nn(n3g, wg_ref[...])
        e = _nn(pb, wp_ref[...])
        pg = _sigmoid(q)
        h3 = h2v + pg * e
        n4, r4 = _rms(h3)
        diff = n4 * gf_ref[...] - t_ref[...]
        loss_ref[...] += jnp.sum(diff * diff).reshape(1, 1)
        dy = diff * (1.0 / D_MODEL)
        vec_ref[0:1, :] += _colsum(dy * n4)
        dh3 = _rms_bwd(dy * gf_ref[...], n4, r4)
        de = (dh3 * pg).astype(BF16)
        dq = (dh3 * e * pg * (1.0 - pg)).astype(BF16)
        dn3g = _nt(dq, wg_ref[...])
        dwg_ref[...] += _tn(n3g, dq)
        dwp_ref[...] += _tn(pb, de)
        vec_ref[1:2, :] += _colsum(dn3g * n3)
        dh2_ref[...] = dh3 + _rms_bwd(dn3g * gp_ref[...], n3, r3)

    return pl.pallas_call(
        body, name="b4_ple_loss", grid=(T // tm,),
        in_specs=[_rows(tm, D_MODEL), _rows(tm, PLE_DIM), _rows(tm, D_MODEL), _resident((1, D_MODEL)),
                  _resident(w_ple_gate.shape), _resident(w_ple_proj.shape), _resident((1, D_MODEL))],
        out_specs=[_acc((1, 1)), _rows(tm, D_MODEL), _acc(w_ple_gate.shape), _acc(w_ple_proj.shape),
                   _acc((SUBLANES, D_MODEL))],
        out_shape=[jax.ShapeDtypeStruct((1, 1), F32), jax.ShapeDtypeStruct((T, D_MODEL), F32),
                   jax.ShapeDtypeStruct(w_ple_gate.shape, F32), jax.ShapeDtypeStruct(w_ple_proj.shape, F32),
                   jax.ShapeDtypeStruct((SUBLANES, D_MODEL), F32)],
        compiler_params=_params(("arbitrary",)),
    )(h2, p, target, ple_norm_g, w_ple_gate, w_ple_proj, final_g)


def _b3_ffn(dh2, h1, ff, norm2_g, w_ffn_in, w_ffn_out):
    T = h1.shape[0]
    tm = ROW_TILE

    def body(d_ref, h_ref, ff_ref, g_ref, wi_ref, wo_ref, dff_ref, dh1_ref, vec_ref):
        @pl.when(pl.program_id(0) == 0)
        def _():
            vec_ref[...] = jnp.zeros_like(vec_ref)

        dh2v = d_ref[...]
        dact = _nt(dh2v.astype(BF16), wo_ref[...])
        g_ff = ff_ref[:, :D_FF].astype(F32)
        u_ff = ff_ref[:, D_FF:].astype(F32)
        s = _sigmoid(g_ff)
        dg = (dact * u_ff * (s * (1.0 + g_ff * (1.0 - s)))).astype(BF16)
        du = (dact * (g_ff * s)).astype(BF16)
        dff_ref[:, :D_FF] = dg
        dff_ref[:, D_FF:] = du
        dv = _nt(dg, wi_ref[:, :D_FF]) + _nt(du, wi_ref[:, D_FF:])
        n2, r2 = _rms(h_ref[...])
        vec_ref[0:1, :] += _colsum(dv * n2)
        dh1_ref[...] = dh2v + _rms_bwd(dv * g_ref[...], n2, r2)

    return pl.pallas_call(
        body, name="b3_ffn", grid=(T // tm,),
        in_specs=[_rows(tm, D_MODEL), _rows(tm, D_MODEL), _rows(tm, 2 * D_FF), _resident((1, D_MODEL)),
                  _resident(w_ffn_in.shape), _resident(w_ffn_out.shape)],
        out_specs=[_rows(tm, 2 * D_FF), _rows(tm, D_MODEL), _acc((SUBLANES, D_MODEL))],
        out_shape=[jax.ShapeDtypeStruct((T, 2 * D_FF), BF16), jax.ShapeDtypeStruct((T, D_MODEL), F32),
                   jax.ShapeDtypeStruct((SUBLANES, D_MODEL), F32)],
        compiler_params=_params(("arbitrary",)),
    )(dh2, h1, ff, norm2_g, w_ffn_in, w_ffn_out)


def _wgrad(a, b, col_tile, name, tokens=WGRAD_TOKENS, after=None):
    T, K = a.shape
    N = b.shape[1]
    tk = min(T, tokens)

    def body(a_ref, b_ref, *rest):
        o_ref = rest[-1]

        @pl.when(pl.program_id(1) == 0)
        def _():
            o_ref[...] = jnp.zeros_like(o_ref)

        o_ref[...] += _tn(a_ref[...].astype(BF16), b_ref[...].astype(BF16))

    return pl.pallas_call(
        body, name=name, grid=(N // col_tile, T // tk),
        in_specs=[pl.BlockSpec((tk, K), lambda j, k: (k, 0)), pl.BlockSpec((tk, col_tile), lambda j, k: (k, j))]
        + ([] if after is None else [ANY]),
        out_specs=pl.BlockSpec((K, col_tile), lambda j, k: (0, j)),
        out_shape=jax.ShapeDtypeStruct((K, N), F32),
        compiler_params=_params(("arbitrary", "arbitrary")),
    )(a, b, *([] if after is None else [after]))


def _b2_gates(dh1, zt, yp, yl, b_gate, w_out):
    T = dh1.shape[0]
    tm = WIDE_TILE

    def body(d_ref, zt_ref, yp_ref, yl_ref, bg_ref, wo_ref, dzt_ref, dyp_ref, dyl_ref, dwo_ref, vec_ref):
        @pl.when(pl.program_id(0) == 0)
        def _():
            dwo_ref[...] = jnp.zeros_like(dwo_ref)
            vec_ref[...] = jnp.zeros_like(vec_ref)

        db = d_ref[...].astype(BF16)
        dm = _nt(db, wo_ref[...])
        gates = _sigmoid(zt_ref[...].astype(F32) + bg_ref[...])
        g0, g1 = gates[:, :D_MODEL], gates[:, D_MODEL:]
        y_pool, y_lru = yp_ref[...].astype(F32), yl_ref[...].astype(F32)
        dwo_ref[...] += _tn((g0 * y_pool + g1 * y_lru).astype(BF16), db)
        dz0 = dm * y_pool * g0 * (1.0 - g0)
        dz1 = dm * y_lru * g1 * (1.0 - g1)
        vec_ref[0:1, :] += _colsum(dz0)
        vec_ref[1:2, :] += _colsum(dz1)
        dzt_ref[:, :D_MODEL] = dz0.astype(BF16)
        dzt_ref[:, D_MODEL:] = dz1.astype(BF16)
        dyp_ref[...] = (dm * g0).astype(BF16)
        dyl_ref[...] = (dm * g1).astype(BF16)

    return pl.pallas_call(
        body, name="b2_gates", grid=(T // tm,),
        in_specs=[_rows(tm, D_MODEL), _rows(tm, 2 * D_MODEL), _rows(tm, D_MODEL), _rows(tm, D_MODEL),
                  _resident(b_gate.shape), _resident(w_out.shape)],
        out_specs=[_rows(tm, 2 * D_MODEL), _rows(tm, D_MODEL), _rows(tm, D_MODEL), _acc(w_out.shape),
                   _acc((SUBLANES, D_MODEL))],
        out_shape=[jax.ShapeDtypeStruct((T, 2 * D_MODEL), BF16), jax.ShapeDtypeStruct((T, D_MODEL), BF16),
                   jax.ShapeDtypeStruct((T, D_MODEL), BF16), jax.ShapeDtypeStruct(w_out.shape, F32),
                   jax.ShapeDtypeStruct((SUBLANES, D_MODEL), F32)],
        compiler_params=_params(("arbitrary",)),
    )(dh1, zt, yp, yl, b_gate, w_out)


def _b12_pool_in_proj(dyp, zp, dzl, dzg, dzt, x, dh1, norm1_g, w_in, pool_w, pool_scale, pool_proj):
    T = zp.shape[0]
    tt = ROW_TILE
    nt = T // tt

    def body(dy_ref, zp_ref, zph_ref, dzl_ref, dzg_ref, dzt_ref, x_ref, dh_ref, g1_ref, win_ref, pw_ref, ps_ref, pp_ref,
             dzp_ref, dx_ref, dpp_ref, dpw_ref, vec_ref, q_next):
        i = pl.program_id(0)
        ti = nt - 1 - i
        first_row = ti * tt

        @pl.when(i == 0)
        def _():
            dpp_ref[...] = jnp.zeros_like(dpp_ref)
            dpw_ref[...] = jnp.zeros_like(dpw_ref)
            vec_ref[...] = jnp.zeros_like(vec_ref)
            q_next[...] = jnp.zeros_like(q_next)

        du_parts = []

        def project(lo):
            k = max(i for i in range(4) if IN_SPLITS[i] <= lo)
            dz_ref = (None, dzl_ref, dzg_ref, dzt_ref)[k]
            at = lo - IN_SPLITS[k]
            part = _nt(dz_ref[:, at:at + PROJ_CHUNK], win_ref[:, lo:lo + PROJ_CHUNK])
            du_parts[:] = [part if not du_parts else du_parts[0] + part]

        mxu = _Interleaved(functools.partial(project, lo) for lo in range(IN_SPLITS[1], IN_SPLITS[4], PROJ_CHUNK))

        keep = (ti > 0).astype(F32)
        zp_cat = jnp.concatenate([zph_ref[...] * keep, zp_ref[...]], axis=0)
        pooled, mixed = _pool_forward(zp_cat, pw_ref, first_row, mxu.tick)
        dy = dy_ref[...]
        dpp_ref[...] += _tn((mixed * ps_ref[...]).astype(BF16), dy)
        mxu.tick(2)
        dms = _nt(dy, pp_ref[...])
        mxu.tick(2)
        vec_ref[0:1, :POOL_WIDTH] += _colsum(dms * mixed)
        dmixed = (dms * ps_ref[...]).astype(BF16)
        t_glob = first_row + _row_ids((tt, POOL_GROUP_DIM))
        dz, q_all, dpooled_pairs = [], [], []
        for p in range(len(POOL_WINDOWS) // 2):
            pair = slice(p * PAIR_DIM, (p + 1) * PAIR_DIM)
            dpw_ref[p] += _tn(pooled[:, pair].astype(BF16), dmixed[:, pair])
            dpooled_pairs.append(_nt(dmixed[:, pair], pw_ref[p]))
        dpooled_all = jnp.concatenate(dpooled_pairs, axis=1)
        for g, w in enumerate(POOL_WINDOWS):
            cols = slice(g * POOL_GROUP_DIM, (g + 1) * POOL_GROUP_DIM)
            dpooled = dpooled_all[:, cols]
            q = dpooled / jnp.minimum(t_glob + 1, w).astype(F32)
            q_all.append(q)
            s, k = jnp.concatenate([q, q_next[:, cols]], axis=0), 1
            while k < w:
                s = s + _shift_up(s, k)
                k *= 2
            dz.append(s[:tt] - dpooled)
            mxu.tick(2)
        dzp = jnp.concatenate(dz, axis=1).astype(BF16)
        dzp_ref[...] = dzp
        q_next[...] = jnp.concatenate([q[:POOL_HALO] for q in q_all], axis=1)
        mxu.flush()

        du = du_parts[0] + _nt(dzp, win_ref[:, IN_SPLITS[0]:IN_SPLITS[1]])
        n1, r1 = _rms(x_ref[...])
        vec_ref[1:2, :] += _colsum(du * n1)
        dx_ref[...] = dh_ref[...] + _rms_bwd(du * g1_ref[...], n1, r1)

    rev = functools.partial(_rows_rev, n_tiles=nt)
    res = [norm1_g, w_in, pool_w, pool_scale, pool_proj]
    return pl.pallas_call(
        body, name="b12_pool_in_proj", grid=(nt,),
        in_specs=[rev(tt, D_MODEL), rev(tt, POOL_WIDTH), _halo_before_rev(POOL_HALO, POOL_WIDTH, tt, nt),
                  rev(tt, D_MODEL), rev(tt, D_MODEL), rev(tt, 2 * D_MODEL), rev(tt, D_MODEL), rev(tt, D_MODEL)]
        + [_resident(w.shape) for w in res],
        out_specs=[rev(tt, POOL_WIDTH), rev(tt, D_MODEL), _acc(pool_proj.shape), _acc(pool_w.shape),
                   _acc((SUBLANES, D_MODEL))],
        out_shape=[jax.ShapeDtypeStruct((T, POOL_WIDTH), BF16), jax.ShapeDtypeStruct((T, D_MODEL), F32),
                   jax.ShapeDtypeStruct(pool_proj.shape, F32), jax.ShapeDtypeStruct(pool_w.shape, F32),
                   jax.ShapeDtypeStruct((SUBLANES, D_MODEL), F32)],
        scratch_shapes=[pltpu.VMEM((POOL_HALO, POOL_WIDTH), F32)],
        compiler_params=_params(("arbitrary",)),
    )(dyp, zp, zp, dzl, dzg, dzt, x, dh1, *res)


_V_CONVW, _V_CONVB, _V_BRG, _V_BIG, _V_LAM = 0, 4, 5, 6, 7


def _b2_lru(dyl, zl, zg, hs, xc_saved, r_saved, ig_saved, conv_w, w_rg, w_ig, lru_lambda, lru_proj):
    T = zl.shape[0]
    tt = ROW_TILE
    nt = T // tt
    n_groups = tt // SUBLANES

    def body(dy_ref, zl_ref, zlh_ref, zg_ref, hs_ref, hsh_ref, xc_ref, r_ref, ig_ref, cw_ref, wrg_ref, wig_ref,
             lam_ref, lp_ref, dzl_ref, dzg_ref, dlp_ref, dwrg_ref, dwig_ref, vec_ref,
             c_s, d_s, g_s, g_next, a_next, dxc_next):
        i = pl.program_id(0)
        ti = nt - 1 - i
        first_row = ti * tt

        @pl.when(i == 0)
        def _():
            dlp_ref[...] = jnp.zeros_like(dlp_ref)
            dwrg_ref[...] = jnp.zeros_like(dwrg_ref)
            dwig_ref[...] = jnp.zeros_like(dwig_ref)
            vec_ref[...] = jnp.zeros_like(vec_ref)
            g_next[...] = jnp.zeros_like(g_next)
            a_next[...] = jnp.zeros_like(a_next)
            dxc_next[...] = jnp.zeros_like(dxc_next)

        keep = (ti > 0).astype(F32)
        sp, dsp_dlam = _softplus_neg(lam_ref[...])
        hs = hs_ref[...]
        gelu, dgelu = _gelu_and_grad(zg_ref[...])
        dy = dy_ref[...]
        dlp_ref[...] += _tn((hs * gelu).astype(BF16), dy)
        dyl = _nt(dy, lp_ref[...])
        dzg_ref[...] = (dyl * hs * dgelu).astype(BF16)

        d_s[...] = dyl * gelu
        a_tile = jnp.exp(-LRU_C * r_ref[...].astype(F32) * sp)
        c_s[...] = _shift_up(jnp.concatenate([a_tile, a_next[...]], axis=0), 1)[:tt]
        a_next[...] = jnp.broadcast_to(a_tile[0:1, :], (SUBLANES, D_MODEL))
        rows8 = _row_ids((SUBLANES, D_MODEL))

        def group(k, carry):
            at = pl.ds(pl.multiple_of((n_groups - 1 - k) * SUBLANES, SUBLANES), SUBLANES)
            C, Dv = c_s[at, :], d_s[at, :]
            for s in (1, 2, 4):
                m = rows8 < SUBLANES - s
                Dv = jnp.where(m, C * pltpu.roll(Dv, SUBLANES - s, 0) + Dv, Dv)
                C = jnp.where(m, C * pltpu.roll(C, SUBLANES - s, 0), C)
            G = C * carry + Dv
            g_s[at, :] = G
            return jnp.broadcast_to(G[0:1, :], (SUBLANES, D_MODEL))

        g_next[...] = lax.fori_loop(0, n_groups, group, g_next[...])
        G = g_s[...]

        cw = cw_ref[...]
        zl_cat = jnp.concatenate([zlh_ref[...] * keep, zl_ref[...]], axis=0)
        xc, r, ig = xc_ref[...].astype(F32), r_ref[...].astype(F32), ig_ref[...].astype(F32)
        a, mult = _decay(r, sp, first_row)
        h_prev = _shift_down(jnp.concatenate([hsh_ref[...] * keep, hs_ref[...]], axis=0), 1)[SUBLANES:]
        t_glob = first_row + _row_ids((tt, D_MODEL))
        dmult = jnp.where(t_glob == 0, 0.0, G * ig * xc)
        dla = G * h_prev * a - dmult * (a * a) / mult
        vec_ref[_V_LAM:_V_LAM + 1, :] += _colsum(dla * r) * (-LRU_C) * dsp_dlam
        dpr = dla * (-LRU_C) * sp * r * (1.0 - r)
        dpi = G * mult * xc * ig * (1.0 - ig)
        vec_ref[_V_BRG:_V_BRG + 1, :] += _colsum(dpr)
        vec_ref[_V_BIG:_V_BIG + 1, :] += _colsum(dpi)
        dprb, dpib, xh = dpr.astype(BF16), dpi.astype(BF16), xc_ref[...]
        dxc_h = []
        for p in range(LRU_HEADS // 2):
            cols = slice(p * PAIR_DIM, (p + 1) * PAIR_DIM)
            dwrg_ref[p] += _tn(xh[:, cols], dprb[:, cols])
            dwig_ref[p] += _tn(xh[:, cols], dpib[:, cols])
            dxc_h.append(_nt(dprb[:, cols], wrg_ref[p]) + _nt(dpib[:, cols], wig_ref[p]))
        dxc = G * mult * ig + jnp.concatenate(dxc_h, axis=1)

        vec_ref[_V_CONVB:_V_CONVB + 1, :] += _colsum(dxc)
        dxc_cat = jnp.concatenate([dxc, dxc_next[...]], axis=0)
        dzl = cw[CONV_WIDTH - 1:CONV_WIDTH] * dxc
        for k in range(CONV_WIDTH):
            lag = CONV_WIDTH - 1 - k
            vec_ref[_V_CONVW + k:_V_CONVW + k + 1, :] += _colsum(dxc * _shift_down(zl_cat, lag)[SUBLANES:])
            if lag:
                dzl = dzl + cw[k:k + 1] * _shift_up(dxc_cat, lag)[:tt]
        dzl_ref[...] = dzl.astype(BF16)
        dxc_next[...] = dxc[:SUBLANES]

    res = [conv_w, w_rg, w_ig, lru_lambda, lru_proj]
    return pl.pallas_call(
        body, name="b2_lru", grid=(nt,),
        in_specs=[_rows_rev(tt, D_MODEL, nt), _rows_rev(tt, D_MODEL, nt), _halo_before_rev(SUBLANES, D_MODEL, tt, nt),
                  _rows_rev(tt, D_MODEL, nt), _rows_rev(tt, D_MODEL, nt), _halo_before_rev(SUBLANES, D_MODEL, tt, nt)]
        + [_rows_rev(tt, D_MODEL, nt)] * 3 + [_resident(w.shape) for w in res],
        out_specs=[_rows_rev(tt, D_MODEL, nt), _rows_rev(tt, D_MODEL, nt), _acc(lru_proj.shape), _acc(w_rg.shape),
                   _acc(w_ig.shape), _acc((SUBLANES, D_MODEL))],
        out_shape=[jax.ShapeDtypeStruct((T, D_MODEL), BF16), jax.ShapeDtypeStruct((T, D_MODEL), BF16),
                   jax.ShapeDtypeStruct(lru_proj.shape, F32), jax.ShapeDtypeStruct(w_rg.shape, F32),
                   jax.ShapeDtypeStruct(w_ig.shape, F32), jax.ShapeDtypeStruct((SUBLANES, D_MODEL), F32)],
        scratch_shapes=[pltpu.VMEM((tt, D_MODEL), F32)] * 3 + [pltpu.VMEM((SUBLANES, D_MODEL), F32)] * 3,
        compiler_params=_params(("arbitrary",)),
    )(dyl, zl, zl, zg, hs, hs, xc_saved, r_saved, ig_saved, *res)


def _row_tile(rows):
    for t in (512, 256, 128, 64, 32, 16, 8):
        if rows % t == 0:
            return t
    return rows


def _scalar_grid(grid, in_specs, out_specs):
    return pltpu.PrefetchScalarGridSpec(num_scalar_prefetch=1, grid=grid, in_specs=in_specs, out_specs=out_specs)


def _cast_into_block(w, by_rows, shard_j, name):
    R, C = w.shape
    tr = _row_tile(R)
    if by_rows:
        out_shape, out_map = (N_SHARDS * R, C), lambda i, j: (j[0] * (R // tr) + i, 0)
    else:
        out_shape, out_map = (R, N_SHARDS * C), lambda i, j: (i, j[0])

    def body(j_ref, w_ref, o_ref):
        o_ref[...] = w_ref[...].astype(BF16)

    return pl.pallas_call(
        body, name=name,
        grid_spec=_scalar_grid((R // tr,), [pl.BlockSpec((tr, C), lambda i, j: (i, 0))], pl.BlockSpec((tr, C), out_map)),
        out_shape=jax.ShapeDtypeStruct(out_shape, BF16),
        compiler_params=_params(("arbitrary",)),
    )(shard_j.reshape(1), w)


def _sum_cores(g, theirs, core, name):
    S, R, C = g.shape
    H = R // 2
    tr = _row_tile(H)
    nh = H // tr

    def body(c_ref, g_ref, t_ref, o_ref):
        o_ref[...] = (g_ref[...] + t_ref[...]).astype(BF16)

    half = pl.BlockSpec((None, tr, C), lambda s, i, c: (s, i, 0))
    return pl.pallas_call(
        body, name=name,
        grid_spec=_scalar_grid((S, nh), [pl.BlockSpec((None, tr, C), lambda s, i, c: (s, c[0] * nh + i, 0)), half], half),
        out_shape=jax.ShapeDtypeStruct((S, H, C), BF16),
        compiler_params=_params(("arbitrary", "arbitrary")),
    )(core.reshape(1), g, theirs)


def _sum_chips(sums, slots, by_rows, place, name):
    _, H, C = slots.shape
    tr = _row_tile(H)
    own_map = (lambda i, p: (p[0], i, 0)) if by_rows else (lambda i, p: (0, i, p[0]))

    def body(p_ref, s_ref, q_ref, o_ref):
        o_ref[...] = ((s_ref[...].astype(F32) + q_ref[0].astype(F32)) + q_ref[1].astype(F32)) + q_ref[2].astype(F32)

    return pl.pallas_call(
        body, name=name,
        grid_spec=_scalar_grid(
            (H // tr,),
            [pl.BlockSpec((None, tr, C), own_map), pl.BlockSpec((3, tr, C), lambda i, p: (0, i, 0))],
            pl.BlockSpec((None, tr, C), lambda i, p: (p[1], i, 0))),
        out_shape=jax.ShapeDtypeStruct((2, H, C), F32),
        compiler_params=_params(("arbitrary",)),
    )(place, sums, slots)


def _adamw(w, g, m, v, name):
    R, C = w.shape
    tr = _row_tile(R)
    c1 = 1.0 - ADAM_B1 ** ADAM_STEP
    c2 = 1.0 - ADAM_B2 ** ADAM_STEP

    def body(w_ref, g_ref, m_ref, v_ref, d_ref, nm_ref, nv_ref):
        gv = g_ref[...]
        nm = ADAM_B1 * m_ref[...] + (1.0 - ADAM_B1) * gv
        nv = ADAM_B2 * v_ref[...] + (1.0 - ADAM_B2) * (gv * gv)
        d_ref[...] = -ADAM_LR * ((nm / c1) / (jnp.sqrt(nv / c2) + ADAM_EPS) + ADAM_WD * w_ref[...])
        nm_ref[...] = nm
        nv_ref[...] = nv

    return pl.pallas_call(
        body, name=name, grid=(R // tr,),
        in_specs=[_rows(tr, C)] * 4, out_specs=[_rows(tr, C)] * 3,
        out_shape=[jax.ShapeDtypeStruct((R, C), F32)] * 3,
        compiler_params=_params(("arbitrary",)),
    )(w, g, m, v)


def _place():
    return lax.axis_index("x"), lax.axis_index("y"), lax.axis_index("c")


def _other_chips(x, y):
    return [(1 - x, y), (x, 1 - y), (1 - x, 1 - y)]


def _shard_block(ref, by_rows, R, C, j, half_rows=None):
    if half_rows is None:
        rows, r0 = R, 0
    else:
        rows = R // 2
        r0 = pl.multiple_of(half_rows * rows, 16)
    if by_rows:
        return ref.at[pl.ds(pl.multiple_of(j * R, 16) + r0, rows), :]
    return ref.at[pl.ds(r0, rows), pl.ds(pl.multiple_of(j * C, 128), C)]


def _all_gather_weights(gathered, shapes, by_rows, small):
    n = len(gathered)

    def body(*refs):
        small_in = refs[n]
        outs, small_out = refs[n + 1:2 * n + 1], refs[2 * n + 1]
        send_sems, recv_sems, local_sem = refs[2 * n + 2:]
        x, y, c = _place()
        me_j = 2 * x + y
        chips = _other_chips(x, y)
        sibling = (x, y, 1 - c)

        def block(i, j, half):
            R, C = shapes[i]
            return _shard_block(outs[i], by_rows[i], R, C, j, half)

        def ici(i, k, src_j):
            return pltpu.make_async_remote_copy(
                src_ref=block(i, src_j, c), dst_ref=block(i, src_j, c),
                send_sem=send_sems.at[6 * i + k], recv_sem=recv_sems.at[6 * i + k],
                device_id=(*chips[k], c), device_id_type=MESH)

        def relay(i, k, half):
            kj = 2 * chips[k][0] + chips[k][1]
            return pltpu.make_async_remote_copy(
                src_ref=block(i, kj, half), dst_ref=block(i, kj, half),
                send_sem=send_sems.at[6 * i + 3 + k], recv_sem=recv_sems.at[6 * i + 3 + k],
                device_id=sibling, device_id_type=MESH)

        def small_copy(k, src_j):
            cols = pl.ds(pl.multiple_of(src_j * 256, 128), 256)
            return pltpu.make_async_remote_copy(
                src_ref=small_in, dst_ref=small_out.at[:, cols],
                send_sem=send_sems.at[6 * n + k], recv_sem=recv_sems.at[6 * n + k],
                device_id=(*chips[k], c), device_id_type=MESH)

        sends = []
        for i in range(n):
            for k in range(3):
                cp = ici(i, k, me_j)
                cp.start()
                sends.append(cp)
        for k in range(3):
            cp = small_copy(k, me_j)
            cp.start()
            sends.append(cp)
        local = pltpu.make_async_copy(small_in, small_out.at[:, pl.ds(pl.multiple_of(me_j * 256, 128), 256)], local_sem)
        local.start()
        for i in range(n):
            for k in range(3):
                kj = 2 * chips[k][0] + chips[k][1]
                ici(i, k, kj).wait_recv()
                cp = relay(i, k, c)
                cp.start()
                sends.append(cp)
        for k in range(3):
            small_copy(k, 2 * chips[k][0] + chips[k][1]).wait_recv()
        for i in range(n):
            for k in range(3):
                relay(i, k, 1 - c).wait_recv()
        for cp in sends:
            cp.wait_send()
        local.wait()

    out_shape = [jax.ShapeDtypeStruct(g.shape, BF16) for g in gathered]
    out_shape.append(jax.ShapeDtypeStruct((8, N_SHARDS * 256), F32))
    n_sems = 6 * n + 3
    return pl.pallas_call(
        body, name="all_gather_weights",
        in_specs=[ANY] * (n + 1), out_specs=[ANY] * (n + 1), out_shape=out_shape,
        input_output_aliases={i: i for i in range(n)},
        scratch_shapes=[pltpu.SemaphoreType.DMA((n_sems,)), pltpu.SemaphoreType.DMA((n_sems,)),
                        pltpu.SemaphoreType.DMA],
    )(*gathered, small)


def _core_exchange(grads, name):
    n = len(grads)

    def body(*refs):
        copies = _core_exchange_copies(refs[:n], refs[n:2 * n], refs[2 * n], refs[2 * n + 1])
        for cp in copies:
            cp.start()
        for cp in copies:
            cp.wait()

    return pl.pallas_call(
        body, name=name,
        in_specs=[ANY] * n, out_specs=[ANY] * n,
        out_shape=[jax.ShapeDtypeStruct((g.shape[0], g.shape[1] // 2, g.shape[2]), F32) for g in grads],
        scratch_shapes=[pltpu.SemaphoreType.DMA((n,))] * 2,
    )(*grads)


HBM = pl.BlockSpec(memory_space=pltpu.HBM)
SEM = pl.BlockSpec(memory_space=pltpu.SEMAPHORE)
TOKEN = jax.ShapeDtypeStruct((SUBLANES, 128), F32)


def _in_hbm(a):
    return pltpu.with_memory_space_constraint(a, pltpu.HBM)


def _split_params():
    return pltpu.CompilerParams(has_side_effects=pltpu.SideEffectType.DATAFLOW_SIDE_EFFECTING)


def _gather_rest_copies(refs, shapes, by_rows, send_sems, recv_sems):
    x, y, c = _place()
    me_j = 2 * x + y
    chips = _other_chips(x, y)
    pairs = []
    for i, ref in enumerate(refs):
        R, C = shapes[i]
        for k in range(3):
            kj = 2 * chips[k][0] + chips[k][1]

            def copy(j, ref=ref, i=i, k=k, R=R, C=C):
                blk = _shard_block(ref, by_rows[i], R, C, j)
                return pltpu.make_async_remote_copy(
                    src_ref=blk, dst_ref=blk, send_sem=send_sems.at[3 * i + k], recv_sem=recv_sems.at[3 * i + k],
                    device_id=(*chips[k], c), device_id_type=MESH)

            pairs.append((copy(me_j), copy(kj)))
    return pairs


def _gather_rest_start(gathered, shapes, by_rows, after):
    n = len(gathered)

    def body(*refs):
        ins = refs[:n]
        send_sems, recv_sems = refs[n + 1], refs[n + 2]
        token = refs[-1]
        for mine, _ in _gather_rest_copies(ins, shapes, by_rows, send_sems, recv_sems):
            mine.start()
        token[...] = jnp.zeros_like(token)

    out = pl.pallas_call(
        body, name="gather_rest_start",
        out_shape=(pltpu.SemaphoreType.DMA((3 * n,)), pltpu.SemaphoreType.DMA((3 * n,)),
                   *[pltpu.HBM(g.shape, g.dtype) for g in gathered], TOKEN),
        in_specs=[HBM] * n + [ANY], out_specs=(SEM, SEM, *[HBM] * n, pl.BlockSpec(memory_space=pltpu.VMEM)),
        input_output_aliases={i: 2 + i for i in range(n)},
        compiler_params=_split_params(),
    )(*[_in_hbm(g) for g in gathered], after)
    return out[0], out[1], out[2:2 + n], out[-1]


def _gather_rest_wait(send_sems, recv_sems, gathered, shapes, by_rows, after):
    n = len(gathered)

    def body(*refs):
        ins = refs[:n]
        send, recv = refs[n], refs[n + 1]
        for mine, theirs in _gather_rest_copies(ins, shapes, by_rows, send, recv):
            mine.wait_send()
            theirs.wait_recv()

    return pl.pallas_call(
        body, name="gather_rest_wait",
        out_shape=tuple(pltpu.HBM(g.shape, g.dtype) for g in gathered),
        in_specs=[HBM] * n + [SEM, SEM, ANY], out_specs=tuple([HBM] * n),
        input_output_aliases={i: i for i in range(n)},
        compiler_params=_split_params(),
    )(*gathered, send_sems, recv_sems, after)


def _chip_exchange_copies(ins, slots, dims, by_rows, send_sems, recv_sems):
    x, y, c = _place()
    chips = _other_chips(x, y)
    pairs = []
    for i in range(len(ins)):
        for k in range(3):
            kj = 2 * chips[k][0] + chips[k][1]
            if by_rows[i]:
                src = ins[i].at[kj]
            else:
                src = ins[i].at[0, :, pl.ds(pl.multiple_of(kj * dims[i][1], 128), dims[i][1])]
            cp = pltpu.make_async_remote_copy(
                src_ref=src, dst_ref=slots[i].at[k], send_sem=send_sems.at[3 * i + k], recv_sem=recv_sems.at[3 * i + k],
                device_id=(*chips[k], c), device_id_type=MESH)
            pairs.append((cp, cp))
    return pairs


def _exchange_dims(sums, by_rows):
    return [(s.shape[1], s.shape[2]) if by_rows[i] else (s.shape[1], s.shape[2] // N_SHARDS) for i, s in enumerate(sums)]


def _chip_exchange_start(sums, by_rows, tag):
    n = len(sums)
    sums = list(sums)
    dims = _exchange_dims(sums, by_rows)
    slots = [lax.empty((3, h, cc), BF16) for h, cc in dims]

    def body(*refs):
        ins, land = refs[:n], refs[n:2 * n]
        send_sems, recv_sems = refs[2 * n], refs[2 * n + 1]
        token = refs[-1]
        for cp, _ in _chip_exchange_copies(ins, land, dims, by_rows, send_sems, recv_sems):
            cp.start()
        token[...] = jnp.zeros_like(token)

    out = pl.pallas_call(
        body, name="grad_chip_exchange_start_" + tag,
        out_shape=(pltpu.SemaphoreType.DMA((3 * n,)), pltpu.SemaphoreType.DMA((3 * n,)),
                   *[pltpu.HBM(a.shape, a.dtype) for a in sums + slots], TOKEN),
        in_specs=[HBM] * (2 * n), out_specs=(SEM, SEM, *[HBM] * (2 * n), pl.BlockSpec(memory_space=pltpu.VMEM)),
        input_output_aliases={i: 2 + i for i in range(2 * n)},
        compiler_params=_split_params(),
    )(*[_in_hbm(a) for a in sums + slots])
    return out[0], out[1], out[2:2 + n], out[2 + n:2 + 2 * n], out[-1]


def _chip_exchange_wait(send_sems, recv_sems, sums, slots, by_rows, after, tag):
    n = len(sums)
    sums, slots = list(sums), list(slots)
    dims = _exchange_dims(sums, by_rows)

    def body(*refs):
        ins, land = refs[:n], refs[n:2 * n]
        send, recv = refs[2 * n], refs[2 * n + 1]
        for cp, _ in _chip_exchange_copies(ins, land, dims, by_rows, send, recv):
            cp.wait_send()
            cp.wait_recv()

    out = pl.pallas_call(
        body, name="grad_chip_exchange_wait_" + tag,
        out_shape=tuple(pltpu.HBM(a.shape, a.dtype) for a in sums + slots),
        in_specs=[HBM] * (2 * n) + [SEM, SEM, ANY], out_specs=tuple([HBM] * (2 * n)),
        input_output_aliases={i: i for i in range(2 * n)},
        compiler_params=_split_params(),
    )(*sums, *slots, send_sems, recv_sems, after)
    return out[:n], out[n:]


def _core_exchange_copies(ins, theirs, send_sems, recv_sems):
    x, y, c = _place()
    copies = []
    for i in range(len(ins)):
        H = ins[i].shape[1] // 2
        copies.append(pltpu.make_async_remote_copy(
            src_ref=ins[i].at[:, pl.ds(pl.multiple_of((1 - c) * H, 8), H), :], dst_ref=theirs[i],
            send_sem=send_sems.at[i], recv_sem=recv_sems.at[i], device_id=(x, y, 1 - c), device_id_type=MESH))
    return copies


def _core_exchange_start(grads):
    n = len(grads)
    grads = list(grads)
    theirs = [lax.empty((g.shape[0], g.shape[1] // 2, g.shape[2]), F32) for g in grads]

    def body(*refs):
        for cp in _core_exchange_copies(refs[:n], refs[n:2 * n], refs[2 * n], refs[2 * n + 1]):
            cp.start()
        refs[-1][...] = jnp.zeros_like(refs[-1])

    out = pl.pallas_call(
        body, name="grad_core_exchange_start",
        out_shape=(pltpu.SemaphoreType.DMA((n,)), pltpu.SemaphoreType.DMA((n,)),
                   *[pltpu.HBM(a.shape, a.dtype) for a in grads + theirs], TOKEN),
        in_specs=[HBM] * (2 * n), out_specs=(SEM, SEM, *[HBM] * (2 * n), pl.BlockSpec(memory_space=pltpu.VMEM)),
        input_output_aliases={i: 2 + i for i in range(2 * n)},
        compiler_params=_split_params(),
    )(*[_in_hbm(a) for a in grads + theirs])
    return out[0], out[1], out[2:2 + n], out[2 + n:2 + 2 * n], out[-1]


def _core_exchange_wait(send_sems, recv_sems, grads, theirs, after):
    n = len(grads)
    grads, theirs = list(grads), list(theirs)

    def body(*refs):
        for cp in _core_exchange_copies(refs[:n], refs[n:2 * n], refs[2 * n], refs[2 * n + 1]):
            cp.wait_send()
            cp.wait_recv()

    out = pl.pallas_call(
        body, name="grad_core_exchange_wait",
        out_shape=tuple(pltpu.HBM(a.shape, a.dtype) for a in grads + theirs),
        in_specs=[HBM] * (2 * n) + [SEM, SEM, ANY], out_specs=tuple([HBM] * (2 * n)),
        input_output_aliases={i: i for i in range(2 * n)},
        compiler_params=_split_params(),
    )(*grads, *theirs, send_sems, recv_sems, after)
    return out[:n], out[n:]


def _core_share(reduced, tag):
    n = len(reduced)

    def body(*refs):
        outs = refs[n:2 * n]
        send_sems, recv_sems = refs[2 * n:]
        x, y, c = _place()
        copies = []
        for i in range(n):
            cp = pltpu.make_async_remote_copy(
                src_ref=outs[i].at[c], dst_ref=outs[i].at[c], send_sem=send_sems.at[i], recv_sem=recv_sems.at[i],
                device_id=(x, y, 1 - c), device_id_type=MESH)
            cp.start()
            copies.append(cp)
        for cp in copies:
            cp.wait()

    return pl.pallas_call(
        body, name="grad_core_share_" + tag,
        in_specs=[ANY] * n, out_specs=[ANY] * n,
        out_shape=[jax.ShapeDtypeStruct(r.shape, F32) for r in reduced],
        input_output_aliases={i: i for i in range(n)},
        scratch_shapes=[pltpu.SemaphoreType.DMA((n,))] * 2,
    )(*reduced)


def _small_exchange_copies(pack_ref, slots_ref, send_sems, recv_sems):
    x, y, c = _place()
    peers = [(px, py, pc) for px in (x, 1 - x) for py in (y, 1 - y) for pc in (c, 1 - c)][1:]
    pairs = []
    for k, peer in enumerate(peers):
        def copy(sender, k=k, peer=peer):
            return pltpu.make_async_remote_copy(
                src_ref=pack_ref, dst_ref=slots_ref.at[4 * sender[0] + 2 * sender[1] + sender[2]],
                send_sem=send_sems.at[k], recv_sem=recv_sems.at[k], device_id=peer, device_id_type=MESH)

        pairs.append((copy((x, y, c)), copy(peer)))
    return pairs


def _small_exchange_start(pack):
    slots = lax.empty((N_DEV,) + pack.shape, F32)

    def body(pack_ref, slots_ref, send_sems, recv_sems, pack_thru, slots_thru, token):
        for mine, _ in _small_exchange_copies(pack_ref, slots_ref, send_sems, recv_sems):
            mine.start()
        token[...] = jnp.zeros_like(token)

    return pl.pallas_call(
        body, name="grad_small_exchange_start",
        out_shape=(pltpu.SemaphoreType.DMA((N_DEV - 1,)), pltpu.SemaphoreType.DMA((N_DEV - 1,)),
                   pltpu.HBM(pack.shape, F32), pltpu.HBM(slots.shape, F32), TOKEN),
        in_specs=[HBM, HBM], out_specs=(SEM, SEM, HBM, HBM, pl.BlockSpec(memory_space=pltpu.VMEM)),
        input_output_aliases={0: 2, 1: 3},
        compiler_params=_split_params(),
    )(_in_hbm(pack), _in_hbm(slots))


def _small_exchange_wait(send_sems, recv_sems, pack, slots, after):
    def body(pack_ref, slots_ref, send, recv, after_ref, pack_thru, slots_thru):
        for mine, theirs in _small_exchange_copies(pack_ref, slots_ref, send, recv):
            mine.wait_send()
            theirs.wait_recv()

    return pl.pallas_call(
        body, name="grad_small_exchange_wait",
        out_shape=(pltpu.HBM(pack.shape, F32), pltpu.HBM(slots.shape, F32)),
        in_specs=[HBM, HBM, SEM, SEM, ANY], out_specs=(HBM, HBM),
        input_output_aliases={0: 0, 1: 1},
        compiler_params=_split_params(),
    )(pack, slots, send_sems, recv_sems, after)


def _sum_small(pack, slots, me):
    R, C = pack.shape
    tr = _row_tile(R)

    def body(me_ref, p_ref, q_ref, o_ref):
        acc = jnp.where(me_ref[0] == 0, p_ref[...], q_ref[0])
        for d in range(1, N_DEV):
            acc = acc + jnp.where(me_ref[0] == d, p_ref[...], q_ref[d])
        o_ref[...] = acc

    return pl.pallas_call(
        body, name="sum_small",
        grid_spec=_scalar_grid((R // tr,), [pl.BlockSpec((tr, C), lambda i, m: (i, 0)),
                                            pl.BlockSpec((N_DEV, tr, C), lambda i, m: (0, i, 0))],
                               pl.BlockSpec((tr, C), lambda i, m: (i, 0))),
        out_shape=jax.ShapeDtypeStruct((R, C), F32),
        compiler_params=_params(("arbitrary",)),
    )(me.reshape(1), pack, slots)


def _pack_rows(parts, rows):
    flat = jnp.concatenate([a.reshape(-1) for a in parts])
    return jnp.pad(flat, (0, rows * 128 - flat.shape[0])).reshape(rows, 128)


def _unpack_rows(pack, shapes):
    flat = pack.reshape(-1)
    out, at = [], 0
    for s in shapes:
        size = 1
        for d in s:
            size *= d
        out.append(flat[at:at + size].reshape(s))
        at += size
    return out


def kernel(x, p, norm1_g, w_in, b_gate, pool_w, pool_scale, pool_proj, conv_w, conv_b, w_rg, b_rg, w_ig, b_ig, lru_lambda, lru_proj, w_out, norm2_g, w_ffn_in, w_ffn_out, ple_norm_g, w_ple_gate, w_ple_proj, final_g, loss_target, m_norm1_g, m_w_in, m_b_gate, m_pool_w, m_pool_scale, m_pool_proj, m_conv_w, m_conv_b, m_w_rg, m_b_rg, m_w_ig, m_b_ig, m_lru_lambda, m_lru_proj, m_w_out, m_norm2_g, m_w_ffn_in, m_w_ffn_out, m_ple_norm_g, m_w_ple_gate, m_w_ple_proj, m_final_g, v_norm1_g, v_w_in, v_b_gate, v_pool_w, v_pool_scale, v_pool_proj, v_conv_w, v_conv_b, v_w_rg, v_b_rg, v_w_ig, v_b_ig, v_lru_lambda, v_lru_proj, v_w_out, v_norm2_g, v_w_ffn_in, v_w_ffn_out, v_ple_norm_g, v_w_ple_gate, v_w_ple_proj, v_final_g):
    weights = dict(norm1_g=norm1_g, w_in=w_in, b_gate=b_gate, pool_w=pool_w, pool_scale=pool_scale,
                   pool_proj=pool_proj, conv_w=conv_w, conv_b=conv_b, w_rg=w_rg, b_rg=b_rg, w_ig=w_ig, b_ig=b_ig,
                   lru_lambda=lru_lambda, lru_proj=lru_proj, w_out=w_out, norm2_g=norm2_g, w_ffn_in=w_ffn_in,
                   w_ffn_out=w_ffn_out, ple_norm_g=ple_norm_g, w_ple_gate=w_ple_gate, w_ple_proj=w_ple_proj,
                   final_g=final_g)
    m_in = dict(norm1_g=m_norm1_g, w_in=m_w_in, b_gate=m_b_gate, pool_w=m_pool_w, pool_scale=m_pool_scale,
                pool_proj=m_pool_proj, conv_w=m_conv_w, conv_b=m_conv_b, w_rg=m_w_rg, b_rg=m_b_rg, w_ig=m_w_ig,
                b_ig=m_b_ig, lru_lambda=m_lru_lambda, lru_proj=m_lru_proj, w_out=m_w_out, norm2_g=m_norm2_g,
                w_ffn_in=m_w_ffn_in, w_ffn_out=m_w_ffn_out, ple_norm_g=m_ple_norm_g, w_ple_gate=m_w_ple_gate,
                w_ple_proj=m_w_ple_proj, final_g=m_final_g)
    v_in = dict(norm1_g=v_norm1_g, w_in=v_w_in, b_gate=v_b_gate, pool_w=v_pool_w, pool_scale=v_pool_scale,
                pool_proj=v_pool_proj, conv_w=v_conv_w, conv_b=v_conv_b, w_rg=v_w_rg, b_rg=v_b_rg, w_ig=v_w_ig,
                b_ig=v_b_ig, lru_lambda=v_lru_lambda, lru_proj=v_lru_proj, w_out=v_w_out, norm2_g=v_norm2_g,
                w_ffn_in=v_w_ffn_in, w_ffn_out=v_w_ffn_out, ple_norm_g=v_ple_norm_g, w_ple_gate=v_w_ple_gate,
                w_ple_proj=v_w_ple_proj, final_g=v_final_g)
    names = list(weights)
    big = ["w_in", "pool_proj", "lru_proj", "w_out", "w_ffn_in", "w_ffn_out", "w_ple_gate", "w_ple_proj"]
    by_rows = [n in ("lru_proj", "w_out", "w_ffn_out", "w_ple_gate") for n in big]
    small = [n for n in names if n not in big]

    shard_j = 2 * lax.axis_index("x") + lax.axis_index("y")
    T = x.shape[1]
    xs, ps, tgt = x[0], p[0, 0], loss_target[0]

    small_local = jnp.concatenate([b_gate[0], conv_w[0], jnp.zeros((2, 256), F32)], axis=0)
    core = lax.axis_index("c").astype(jnp.int32)
    place = jnp.stack([shard_j, core]).astype(jnp.int32)
    rows_of = dict(zip(big, by_rows))
    shard_shape = {n: weights[n].shape[1:] for n in big}
    blocks = {n: _cast_into_block(weights[n][0], rows_of[n], place[0], "cast_" + n) for n in big}
    early, late = big[:4], big[4:]
    gathered = _all_gather_weights([blocks[n] for n in early], [shard_shape[n] for n in early],
                                   [rows_of[n] for n in early], small_local)
    full = dict(zip(early, gathered[:-1]))
    late_send, late_recv, late_bufs, late_token = _gather_rest_start(
        [blocks[n] for n in late], [shard_shape[n] for n in late], [rows_of[n] for n in late], gathered[-1])
    b_gate_full = gathered[-1][0:2].reshape(1, 2 * D_MODEL)
    conv_w_full = gathered[-1][2:6]
    pool_w_1, w_rg_1, w_ig_1 = [w[0].astype(BF16) for w in (pool_w, w_rg, w_ig)]
    pool_w_b, w_rg_b, w_ig_b = [_pair_blocks(w) for w in (pool_w_1, w_rg_1, w_ig_1)]
    b_rg_row, b_ig_row = b_rg.reshape(1, D_MODEL), b_ig.reshape(1, D_MODEL)
    final_row = final_g.reshape(1, D_MODEL)

    zp, zl, zg, zt, u, h1, hs, yp, yl, xc_saved, r_saved, ig_saved = _f12_mixer(
        xs, norm1_g + late_token[0, 0], full["w_in"], b_gate_full, pool_w_1, pool_scale, full["pool_proj"],
        conv_w_full, conv_b, w_rg_1, b_rg_row, w_ig_1, b_ig_row, lru_lambda, full["lru_proj"], full["w_out"])
    full.update(zip(late, _gather_rest_wait(late_send, late_recv, late_bufs, [shard_shape[n] for n in late],
                                            [rows_of[n] for n in late], h1)))
    h2, v, ff, act = _f3_ffn(h1, norm2_g, full["w_ffn_in"], full["w_ffn_out"])

    loss_sum, dh2, g_ple_gate, g_ple_proj, vec4 = _b4_ple_loss(
        h2, ps, tgt, ple_norm_g, full["w_ple_gate"], full["w_ple_proj"], final_row)
    dff, dh1, vec3 = _b3_ffn(dh2, h1, ff, norm2_g, full["w_ffn_in"], full["w_ffn_out"])
    g_ffn_in = _wgrad(v, dff, 2 * D_FF // N_SHARDS, "wgrad_ffn_in")
    g_ffn_out = _wgrad(act, dh2, D_MODEL, "wgrad_ffn_out", tokens=WGRAD_TOKENS // 2)

    def stack(n, g):
        return g.reshape(N_SHARDS, g.shape[0] // N_SHARDS, g.shape[1]) if rows_of[n] else g[None]

    def chip_sums_of(group, grads_of, tag):
        stacked = [stack(n, grads_of[n]) for n in group]
        theirs = _core_exchange(stacked, "grad_core_exchange_" + tag)
        return [_sum_cores(g, t, core, "sum_cores_" + n) for g, t, n in zip(stacked, theirs, group)]

    late_rows = [rows_of[n] for n in late]
    late_grads = dict(w_ffn_in=g_ffn_in, w_ffn_out=g_ffn_out, w_ple_gate=g_ple_gate, w_ple_proj=g_ple_proj)
    cx_send, cx_recv, late_stacked, late_theirs, cx_token = _core_exchange_start(
        [stack(n, late_grads[n]) for n in late])
    dzt, dyp, dyl, g_w_out, vec_g = _b2_gates(dh1, zt, yp, yl, b_gate_full + cx_token[0, 0], full["w_out"])
    late_stacked, late_theirs = _core_exchange_wait(cx_send, cx_recv, late_stacked, late_theirs, dzt)
    late_sums = [_sum_cores(g, t, core, "sum_cores_" + n) for g, t, n in zip(late_stacked, late_theirs, late)]
    ex_send, ex_recv, late_sums, late_slots, ex_token = _chip_exchange_start(late_sums, late_rows, "late")
    dzl, dzg, g_lru_proj, g_w_rg, g_w_ig, vec_l = _b2_lru(
        dyl, zl, zg, hs, xc_saved, r_saved, ig_saved, conv_w_full, w_rg_b, w_ig_b, lru_lambda + ex_token[0, 0],
        full["lru_proj"])
    dzp, grad_x, g_pool_proj, g_pool_w, vec_p = _b12_pool_in_proj(
        dyp, zp, dzl, dzg, dzt, xs, dh1, norm1_g, full["w_in"], pool_w_b, pool_scale, full["pool_proj"])
    small_full = dict(
        norm1_g=vec_p[1], b_gate=vec_g[0:2], pool_w=_unpair_blocks(g_pool_w), pool_scale=vec_p[0, :POOL_WIDTH],
        conv_w=vec_l[_V_CONVW:_V_CONVW + CONV_WIDTH], conv_b=vec_l[_V_CONVB], w_rg=_unpair_blocks(g_w_rg),
        b_rg=vec_l[_V_BRG], w_ig=_unpair_blocks(g_w_ig), b_ig=vec_l[_V_BIG], lru_lambda=vec_l[_V_LAM], norm2_g=vec3[0], ple_norm_g=vec4[1],
        final_g=vec4[0])
    full_shapes = [small_full[n].shape for n in small]
    n_full = sum(int(small_full[n].size) for n in small)
    rows_full = -(-n_full // (128 * ROW_TILE)) * ROW_TILE
    sm_send, sm_recv, sm_pack, sm_slots, sm_token = _small_exchange_start(
        _pack_rows([small_full[n] for n in small], rows_full))
    g_w_in = jnp.concatenate([
        _wgrad(u, dzp, POOL_WIDTH, "wgrad_in_pool", after=sm_token),
        _wgrad(u, dzl, D_MODEL, "wgrad_in_lru", after=sm_token),
        _wgrad(u, dzg, D_MODEL, "wgrad_in_gelu", after=sm_token),
        _wgrad(u, dzt, D_MODEL, "wgrad_in_gate", after=sm_token)], axis=1)

    loss = lax.psum(loss_sum[0, 0] * (0.5 / D_MODEL), ("x", "y", "c"))

    early_rows = [rows_of[n] for n in early]
    early_sums = chip_sums_of(early, dict(w_in=g_w_in, pool_proj=g_pool_proj, lru_proj=g_lru_proj, w_out=g_w_out),
                              "early")
    e_send, e_recv, early_sums, early_slots, e_token = _chip_exchange_start(early_sums, early_rows, "early")
    grads, deltas, new_m, new_v = {}, {}, {}, {}

    def finish(group, sums, slots, tag):
        reduced = _core_share([_sum_chips(s, q, rows_of[n], place, "sum_chips_" + n)
                               for s, q, n in zip(sums, slots, group)], tag)
        for n, r in zip(group, reduced):
            g = r.reshape(r.shape[0] * r.shape[1], r.shape[2])
            d, nm, nv = _adamw(weights[n][0], g, m_in[n][0], v_in[n][0], "adamw_" + n)
            grads[n], deltas[n], new_m[n], new_v[n] = g[None], d[None], nm[None], nv[None]

    late_sums, late_slots = _chip_exchange_wait(ex_send, ex_recv, late_sums, late_slots, late_rows, e_token, "late")
    finish(late, late_sums, late_slots, "late")

    sm_pack, sm_slots = _small_exchange_wait(sm_send, sm_recv, sm_pack, sm_slots, e_token)
    device = (4 * lax.axis_index("x") + 2 * lax.axis_index("y") + lax.axis_index("c")).astype(jnp.int32)
    summed = dict(zip(small, _unpack_rows(_sum_small(sm_pack, sm_slots, device), full_shapes)))
    summed["b_gate"] = lax.dynamic_slice_in_dim(summed["b_gate"], shard_j * 256, 256, axis=1)
    summed["conv_w"] = lax.dynamic_slice_in_dim(summed["conv_w"], shard_j * 256, 256, axis=1)
    local_shapes = [weights[n].shape for n in small]
    n_local = sum(int(weights[n].size) for n in small)
    rows_local = -(-n_local // (128 * ROW_TILE)) * ROW_TILE
    packs = [_pack_rows([src[n] for n in small], rows_local) for src in (weights, summed, m_in, v_in)]
    d_s, nm_s, nv_s = _adamw(*packs, "adamw_small")
    for dst, pack in ((grads, packs[1]), (deltas, d_s), (new_m, nm_s), (new_v, nv_s)):
        dst.update(zip(small, _unpack_rows(pack, local_shapes)))

    done = d_s[:SUBLANES]
    for n in late:
        done = done + deltas[n][0, :SUBLANES, :128]
    early_sums, early_slots = _chip_exchange_wait(e_send, e_recv, early_sums, early_slots, early_rows, done, "early")
    finish(early, early_sums, early_slots, "early")

    return (loss, grad_x[None], *[grads[n] for n in names], *[deltas[n] for n in names],
            *[new_m[n] for n in names], *[new_v[n] for n in names])
```

```python
import functools

import jax
import jax.numpy as jnp
from jax import lax
from jax.experimental import pallas as pl
from jax.experimental.pallas import tpu as pltpu

F32 = jnp.float32
BF16 = jnp.bfloat16

D_MODEL = 1024
POOL_WIDTH = 512
POOL_GROUP_DIM = 128
POOL_WINDOWS = (2, 4, 8, 16)
POOL_HALO = 16
LRU_HEADS = 8
LRU_HEAD_DIM = 128
CONV_WIDTH = 4
LRU_C = 8.0
D_FF = 2816
PLE_DIM = 256
RMS_EPS = 1e-6
N_SHARDS = 4
N_DEV = 8

ADAM_LR = 0.001
ADAM_B1 = 0.9
ADAM_B2 = 0.999
ADAM_EPS = 1e-08
ADAM_WD = 0.01
ADAM_STEP = 10

ROW_TILE = 256
WIDE_TILE = 512
WGRAD_TOKENS = 2048
SUBLANES = 8
VMEM_LIMIT = 56 * 1024 * 1024
MESH = pl.DeviceIdType.MESH
ANY = pl.BlockSpec(memory_space=pl.ANY)


def _params(semantics=None):
    return pltpu.CompilerParams(dimension_semantics=semantics, vmem_limit_bytes=VMEM_LIMIT)


def _resident(shape):
    n = len(shape)
    return pl.BlockSpec(shape, lambda *_: (0,) * n, pipeline_mode=pl.Buffered(1))


def _acc(shape):
    n = len(shape)
    return pl.BlockSpec(shape, lambda *_: (0,) * n)


def _rows(tile, cols):
    return pl.BlockSpec((tile, cols), lambda i: (i, 0))


def _rows_rev(tile, cols, n_tiles):
    return pl.BlockSpec((tile, cols), lambda i: (n_tiles - 1 - i, 0))


def _halo_before_rev(rows, cols, tile, n_tiles):
    per = tile // rows
    return pl.BlockSpec((rows, cols), lambda i: (jnp.maximum((n_tiles - 1 - i) * per - 1, 0), 0))


def _nn(a, b):
    return jnp.dot(a, b, preferred_element_type=F32)


def _nt(a, b):
    return lax.dot_general(a, b, (((1,), (1,)), ((), ())), preferred_element_type=F32)


def _tn(a, b):
    return lax.dot_general(a, b, (((0,), (0,)), ((), ())), preferred_element_type=F32)


def _rms(x):
    r = lax.rsqrt(jnp.mean(x * x, axis=-1, keepdims=True) + RMS_EPS)
    return x * r, r


def _rms_bwd(dn, n, r):
    return r * (dn - n * jnp.mean(dn * n, axis=-1, keepdims=True))


def _sigmoid(x):
    return 0.5 * jnp.tanh(0.5 * x) + 0.5


_GELU_C = 0.7978845608028654
_GELU_A = 0.044715


def _gelu(x):
    t = jnp.tanh(_GELU_C * (x + _GELU_A * x * x * x))
    return 0.5 * x * (1.0 + t)


def _gelu_and_grad(x):
    x2 = x * x
    t = jnp.tanh(_GELU_C * (x + _GELU_A * x2 * x))
    cdf = 0.5 * (1.0 + t)
    grad = cdf + 0.5 * x * (1.0 - t * t) * _GELU_C * (1.0 + 3.0 * _GELU_A * x2)
    return x * cdf, grad


def _softplus_neg(lam):
    e = jnp.exp(-jnp.abs(lam))
    sp = jnp.maximum(-lam, 0.0) + jnp.log1p(e)
    return sp, -_sigmoid(-lam)


def _colsum(v):
    return jnp.sum(v, axis=0, keepdims=True)


def _row_ids(shape):
    return lax.broadcasted_iota(jnp.int32, shape, 0)


def _shift_down(cat, k):
    return pltpu.roll(cat, k, 0) if k else cat


def _shift_up(cat, k):
    return pltpu.roll(cat, cat.shape[0] - k, 0) if k else cat


IN_SPLITS = (0, POOL_WIDTH, POOL_WIDTH + D_MODEL, POOL_WIDTH + 2 * D_MODEL, POOL_WIDTH + 4 * D_MODEL)
IN_WIDTHS = tuple(IN_SPLITS[k + 1] - IN_SPLITS[k] for k in range(4))
PROJ_CHUNK = 256
PAIR_DIM = 2 * LRU_HEAD_DIM


def _pair_blocks(w):
    zero = jnp.zeros_like(w[0::2])
    return jnp.concatenate([jnp.concatenate([w[0::2], zero], axis=2), jnp.concatenate([zero, w[1::2]], axis=2)], axis=1)


def _unpair_blocks(w):
    n, d2, _ = w.shape
    d = d2 // 2
    return jnp.stack([w[:, :d, :d], w[:, d:, d:]], axis=1).reshape(2 * n, d, d)


def _no_tick():
    pass


class _Interleaved:
    def __init__(self, pieces):
        self._pieces = iter(pieces)

    def tick(self, n=1):
        for _ in range(n):
            piece = next(self._pieces, None)
            if piece is not None:
                piece()

    def flush(self):
        for piece in self._pieces:
            piece()


def _pool_forward(zp_cat, pw_ref, first_row, tick=_no_tick):
    tt = zp_cat.shape[0] - POOL_HALO
    t_glob = first_row + _row_ids((tt, POOL_GROUP_DIM))
    pooled, mixed = [], []
    for g, w in enumerate(POOL_WINDOWS):
        cat = zp_cat[:, g * POOL_GROUP_DIM:(g + 1) * POOL_GROUP_DIM]
        s, k = cat, 1
        while k < w:
            s = s + _shift_down(s, k)
            k *= 2
        cnt = jnp.minimum(t_glob + 1, w).astype(F32)
        pooled.append(s[POOL_HALO:] / cnt - cat[POOL_HALO:])
        per = pw_ref.shape[-1] // POOL_GROUP_DIM
        if (g + 1) % per == 0:
            block = jnp.concatenate(pooled[-per:], axis=1).astype(BF16)
            mixed.append(_nn(block, pw_ref[g // per]))
        tick()
    return jnp.concatenate(pooled, axis=1), jnp.concatenate(mixed, axis=1)


def _lru_gates(zl_cat, conv_w, conv_b, wrg_ref, brg, wig_ref, big, sp, first_row, tick=_no_tick):
    tt = zl_cat.shape[0] - SUBLANES
    xc = conv_w[CONV_WIDTH - 1:CONV_WIDTH] * zl_cat
    for k in range(1, CONV_WIDTH):
        xc = xc + conv_w[CONV_WIDTH - 1 - k:CONV_WIDTH - k] * _shift_down(zl_cat, k)
        tick()
    xc = xc[SUBLANES:] + conv_b
    xh = xc.astype(BF16)
    pr, pi = [], []
    width = wrg_ref.shape[-1]
    for p in range(D_MODEL // width):
        xs = xh[:, p * width:(p + 1) * width]
        pr.append(_nn(xs, wrg_ref[p]))
        pi.append(_nn(xs, wig_ref[p]))
    r = _sigmoid(jnp.concatenate(pr, axis=1) + brg)
    tick()
    ig = _sigmoid(jnp.concatenate(pi, axis=1) + big)
    tick()
    a, mult = _decay(r, sp, first_row, tick)
    tick()
    return xc, r, ig, a, mult


def _decay(r, sp, first_row, tick=_no_tick):
    a = jnp.exp(-LRU_C * r * sp)
    tick()
    mult = jnp.sqrt(jnp.maximum(1.0 - a * a, 0.0))
    t_glob = first_row + _row_ids(r.shape)
    return a, jnp.where(t_glob == 0, 1.0, mult)


def _f12_mixer(x, norm1_g, w_in, b_gate, pool_w, pool_scale, pool_proj, conv_w, conv_b, w_rg, b_rg, w_ig, b_ig,
               lru_lambda, lru_proj, w_out):
    T = x.shape[0]
    tt = ROW_TILE
    nt = T // tt
    n_groups = tt // SUBLANES
    proj_mid = IN_SPLITS[3] + D_MODEL // 2

    def body(xm_ref, x_ref, g1_ref, win_ref, bg_ref, pw_ref, ps_ref, pp_ref, cw_ref, cb_ref,
             wrg_ref, brg_ref, wig_ref, big_ref, lam_ref, lp_ref, wo_ref,
             zp_ref, zl_ref, zg_ref, zt_ref, u_ref, h1_ref, hs_ref, yp_ref, yl_ref, xc_ref, r_ref, ig_ref,
             zbuf, zp_halo, zl_halo, a_s, b_s, carry_s):
        s = pl.program_id(0)

        @pl.when(s == 0)
        def _():
            zbuf[1] = jnp.zeros((tt, IN_SPLITS[4]), F32)
            zp_halo[...] = jnp.zeros_like(zp_halo)
            zl_halo[...] = jnp.zeros_like(zl_halo)
            carry_s[...] = jnp.zeros_like(carry_s)

        z_new, z_old = zbuf.at[s % 2], zbuf.at[(s + 1) % 2]
        first = s <= 1
        first_row = jnp.maximum(s - 1, 0) * tt

        n1, _ = _rms(xm_ref[...])
        u = (n1 * g1_ref[...]).astype(BF16)
        u_ref[...] = u

        z_refs = (zp_ref, zl_ref, zg_ref, zt_ref)

        def project(lo):
            k = max(i for i in range(4) if IN_SPLITS[i] <= lo)
            part = _nn(u, win_ref[:, lo:lo + PROJ_CHUNK])
            z_new[:, lo:lo + PROJ_CHUNK] = part
            z_refs[k][:, lo - IN_SPLITS[k]:lo - IN_SPLITS[k] + PROJ_CHUNK] = part.astype(z_refs[k].dtype)

        before_scan = _Interleaved(functools.partial(project, lo) for lo in range(0, proj_mid, PROJ_CHUNK))
        after_scan = _Interleaved(functools.partial(project, lo) for lo in range(proj_mid, IN_SPLITS[4], PROJ_CHUNK))

        zp_cat = jnp.concatenate([jnp.where(first, 0.0, zp_halo[...]), z_old[:, IN_SPLITS[0]:IN_SPLITS[1]]], axis=0)
        _, mixed = _pool_forward(zp_cat, pw_ref, first_row, before_scan.tick)
        y_pool = _nn((mixed * ps_ref[...]).astype(BF16), pp_ref[...])

        sp, _ = _softplus_neg(lam_ref[...])
        zl_cat = jnp.concatenate([jnp.where(first, 0.0, zl_halo[...]), z_old[:, IN_SPLITS[1]:IN_SPLITS[2]]], axis=0)
        xc, r, ig, a, mult = _lru_gates(zl_cat, cw_ref[...], cb_ref[...], wrg_ref, brg_ref[...], wig_ref,
                                        big_ref[...], sp, first_row, before_scan.tick)
        a_s[...] = a
        b_s[...] = mult * ig * xc
        xc_ref[...] = xc.astype(BF16)
        r_ref[...] = r.astype(BF16)
        ig_ref[...] = ig.astype(BF16)
        before_scan.flush()

        rows8 = _row_ids((SUBLANES, D_MODEL))

        def group(g, carry):
            at = pl.ds(pl.multiple_of(g * SUBLANES, SUBLANES), SUBLANES)
            A, B = a_s[at, :], b_s[at, :]
            for s in (1, 2, 4):
                m = rows8 >= s
                B = jnp.where(m, A * pltpu.roll(B, s, 0) + B, B)
                A = jnp.where(m, A * pltpu.roll(A, s, 0), A)
            h = A * carry + B
            hs_ref[at, :] = h
            return jnp.broadcast_to(h[SUBLANES - 1:SUBLANES, :], (SUBLANES, D_MODEL))

        carry_s[...] = lax.fori_loop(0, n_groups, group, jnp.where(first, 0.0, carry_s[...]))
        gelu = _gelu(z_old[:, IN_SPLITS[2]:IN_SPLITS[3]])
        after_scan.tick(2)
        y_lru = _nn((hs_ref[...] * gelu).astype(BF16), lp_ref[...])

        gates = _sigmoid(z_old[:, IN_SPLITS[3]:IN_SPLITS[4]] + bg_ref[...])
        after_scan.tick(2)
        merged = gates[:, :D_MODEL] * y_pool + gates[:, D_MODEL:] * y_lru
        after_scan.flush()
        h1_ref[...] = x_ref[...] + _nn(merged.astype(BF16), wo_ref[...])
        yp_ref[...] = y_pool.astype(BF16)
        yl_ref[...] = y_lru.astype(BF16)
        zp_halo[...] = z_old[tt - POOL_HALO:, IN_SPLITS[0]:IN_SPLITS[1]]
        zl_halo[...] = z_old[tt - SUBLANES:, IN_SPLITS[1]:IN_SPLITS[2]]

    def ahead(cols):
        return pl.BlockSpec((tt, cols), lambda s: (jnp.minimum(s, nt - 1), 0))

    def behind(cols):
        return pl.BlockSpec((tt, cols), lambda s: (jnp.maximum(s - 1, 0), 0))

    res = [norm1_g, w_in, b_gate, pool_w, pool_scale, pool_proj, conv_w, conv_b, w_rg, b_rg, w_ig, b_ig, lru_lambda,
           lru_proj, w_out]
    return pl.pallas_call(
        body, name="f12_mixer", grid=(nt + 1,),
        in_specs=[ahead(D_MODEL), behind(D_MODEL)] + [_resident(w.shape) for w in res],
        out_specs=[ahead(w) for w in IN_WIDTHS] + [ahead(D_MODEL)] + [behind(D_MODEL)] * 7,
        out_shape=[jax.ShapeDtypeStruct((T, w), dt) for w, dt in zip(IN_WIDTHS, (F32, F32, F32, BF16))]
        + [jax.ShapeDtypeStruct((T, D_MODEL), BF16), jax.ShapeDtypeStruct((T, D_MODEL), F32),
           jax.ShapeDtypeStruct((T, D_MODEL), F32)] + [jax.ShapeDtypeStruct((T, D_MODEL), BF16)] * 5,
        scratch_shapes=[pltpu.VMEM((2, tt, IN_SPLITS[4]), F32), pltpu.VMEM((POOL_HALO, POOL_WIDTH), F32),
                        pltpu.VMEM((SUBLANES, D_MODEL), F32), pltpu.VMEM((tt, D_MODEL), F32),
                        pltpu.VMEM((tt, D_MODEL), F32), pltpu.VMEM((SUBLANES, D_MODEL), F32)],
        compiler_params=_params(("arbitrary",)),
    )(x, x, *res)


def _f3_ffn(h1, norm2_g, w_ffn_in, w_ffn_out):
    T = h1.shape[0]
    tm = ROW_TILE

    def body(h_ref, g_ref, wi_ref, wo_ref, h2_ref, v_ref, ff_ref, act_ref):
        h = h_ref[...]
        n, _ = _rms(h)
        v = (n * g_ref[...]).astype(BF16)
        v_ref[...] = v
        g_ff = _nn(v, wi_ref[:, :D_FF])
        u_ff = _nn(v, wi_ref[:, D_FF:])
        ff_ref[:, :D_FF] = g_ff.astype(BF16)
        ff_ref[:, D_FF:] = u_ff.astype(BF16)
        act = (g_ff * _sigmoid(g_ff) * u_ff).astype(BF16)
        act_ref[...] = act
        h2_ref[...] = h + _nn(act, wo_ref[...])

    return pl.pallas_call(
        body, name="f3_ffn", grid=(T // tm,),
        in_specs=[_rows(tm, D_MODEL), _resident((1, D_MODEL)), _resident(w_ffn_in.shape), _resident(w_ffn_out.shape)],
        out_specs=[_rows(tm, D_MODEL), _rows(tm, D_MODEL), _rows(tm, 2 * D_FF), _rows(tm, D_FF)],
        out_shape=[jax.ShapeDtypeStruct((T, D_MODEL), F32), jax.ShapeDtypeStruct((T, D_MODEL), BF16),
                   jax.ShapeDtypeStruct((T, 2 * D_FF), BF16), jax.ShapeDtypeStruct((T, D_FF), BF16)],
        compiler_params=_params(("arbitrary",)),
    )(h1, norm2_g, w_ffn_in, w_ffn_out)


def _b4_ple_loss(h2, p, target, ple_norm_g, w_ple_gate, w_ple_proj, final_g):
    T = h2.shape[0]
    tm = WIDE_TILE

    def body(h_ref, p_ref, t_ref, gp_ref, wg_ref, wp_ref, gf_ref, loss_ref, dh2_ref, dwg_ref, dwp_ref, vec_ref):
        @pl.when(pl.program_id(0) == 0)
        def _():
            loss_ref[...] = jnp.zeros_like(loss_ref)
            dwg_ref[...] = jnp.zeros_like(dwg_ref)
            dwp_ref[...] = jnp.zeros_like(dwp_ref)
            vec_ref[...] = jnp.zeros_like(vec_ref)

        h2v = h_ref[...]
        n3, r3 = _rms(h2v)
        n3g = (n3 * gp_ref[...]).astype(BF16)
        pb = p_ref[...].astype(BF16)
        q = _nn(n3g, wg_ref[...])
        e = _nn(pb, wp_ref[...])
        pg = _sigmoid(q)
        h3 = h2v + pg * e
        n4, r4 = _rms(h3)
        diff = n4 * gf_ref[...] - t_ref[...]
        loss_ref[...] += jnp.sum(diff * diff).reshape(1, 1)
        dy = diff * (1.0 / D_MODEL)
        vec_ref[0:1, :] += _colsum(dy * n4)
        dh3 = _rms_bwd(dy * gf_ref[...], n4, r4)
        de = (dh3 * pg).astype(BF16)
        dq = (dh3 * e * pg * (1.0 - pg)).astype(BF16)
        dn3g = _nt(dq, wg_ref[...])
        dwg_ref[...] += _tn(n3g, dq)
        dwp_ref[...] += _tn(pb, de)
        vec_ref[1:2, :] += _colsum(dn3g * n3)
        dh2_ref[...] = dh3 + _rms_bwd(dn3g * gp_ref[...], n3, r3)

    return pl.pallas_call(
        body, name="b4_ple_loss", grid=(T // tm,),
        in_specs=[_rows(tm, D_MODEL), _rows(tm, PLE_DIM), _rows(tm, D_MODEL), _resident((1, D_MODEL)),
                  _resident(w_ple_gate.shape), _resident(w_ple_proj.shape), _resident((1, D_MODEL))],
        out_specs=[_acc((1, 1)), _rows(tm, D_MODEL), _acc(w_ple_gate.shape), _acc(w_ple_proj.shape),
                   _acc((SUBLANES, D_MODEL))],
        out_shape=[jax.ShapeDtypeStruct((1, 1), F32), jax.ShapeDtypeStruct((T, D_MODEL), F32),
                   jax.ShapeDtypeStruct(w_ple_gate.shape, F32), jax.ShapeDtypeStruct(w_ple_proj.shape, F32),
                   jax.ShapeDtypeStruct((SUBLANES, D_MODEL), F32)],
        compiler_params=_params(("arbitrary",)),
    )(h2, p, target, ple_norm_g, w_ple_gate, w_ple_proj, final_g)


def _b3_ffn(dh2, h1, ff, norm2_g, w_ffn_in, w_ffn_out):
    T = h1.shape[0]
    tm = ROW_TILE

    def body(d_ref, h_ref, ff_ref, g_ref, wi_ref, wo_ref, dff_ref, dh1_ref, vec_ref):
        @pl.when(pl.program_id(0) == 0)
        def _():
            vec_ref[...] = jnp.zeros_like(vec_ref)

        dh2v = d_ref[...]
        dact = _nt(dh2v.astype(BF16), wo_ref[...])
        g_ff = ff_ref[:, :D_FF].astype(F32)
        u_ff = ff_ref[:, D_FF:].astype(F32)
        s = _sigmoid(g_ff)
        dg = (dact * u_ff * (s * (1.0 + g_ff * (1.0 - s)))).astype(BF16)
        du = (dact * (g_ff * s)).astype(BF16)
        dff_ref[:, :D_FF] = dg
        dff_ref[:, D_FF:] = du
        dv = _nt(dg, wi_ref[:, :D_FF]) + _nt(du, wi_ref[:, D_FF:])
        n2, r2 = _rms(h_ref[...])
        vec_ref[0:1, :] += _colsum(dv * n2)
        dh1_ref[...] = dh2v + _rms_bwd(dv * g_ref[...], n2, r2)

    return pl.pallas_call(
        body, name="b3_ffn", grid=(T // tm,),
        in_specs=[_rows(tm, D_MODEL), _rows(tm, D_MODEL), _rows(tm, 2 * D_FF), _resident((1, D_MODEL)),
                  _resident(w_ffn_in.shape), _resident(w_ffn_out.shape)],
        out_specs=[_rows(tm, 2 * D_FF), _rows(tm, D_MODEL), _acc((SUBLANES, D_MODEL))],
        out_shape=[jax.ShapeDtypeStruct((T, 2 * D_FF), BF16), jax.ShapeDtypeStruct((T, D_MODEL), F32),
                   jax.ShapeDtypeStruct((SUBLANES, D_MODEL), F32)],
        compiler_params=_params(("arbitrary",)),
    )(dh2, h1, ff, norm2_g, w_ffn_in, w_ffn_out)


def _wgrad(a, b, col_tile, name, tokens=WGRAD_TOKENS, after=None):
    T, K = a.shape
    N = b.shape[1]
    tk = min(T, tokens)

    def body(a_ref, b_ref, *rest):
        o_ref = rest[-1]

        @pl.when(pl.program_id(1) == 0)
        def _():
            o_ref[...] = jnp.zeros_like(o_ref)

        o_ref[...] += _tn(a_ref[...].astype(BF16), b_ref[...].astype(BF16))

    return pl.pallas_call(
        body, name=name, grid=(N // col_tile, T // tk),
        in_specs=[pl.BlockSpec((tk, K), lambda j, k: (k, 0)), pl.BlockSpec((tk, col_tile), lambda j, k: (k, j))]
        + ([] if after is None else [ANY]),
        out_specs=pl.BlockSpec((K, col_tile), lambda j, k: (0, j)),
        out_shape=jax.ShapeDtypeStruct((K, N), F32),
        compiler_params=_params(("arbitrary", "arbitrary")),
    )(a, b, *([] if after is None else [after]))


def _b2_gates(dh1, zt, yp, yl, b_gate, w_out):
    T = dh1.shape[0]
    tm = WIDE_TILE

    def body(d_ref, zt_ref, yp_ref, yl_ref, bg_ref, wo_ref, dzt_ref, dyp_ref, dyl_ref, dwo_ref, vec_ref):
        @pl.when(pl.program_id(0) == 0)
        def _():
            dwo_ref[...] = jnp.zeros_like(dwo_ref)
            vec_ref[...] = jnp.zeros_like(vec_ref)

        db = d_ref[...].astype(BF16)
        dm = _nt(db, wo_ref[...])
        gates = _sigmoid(zt_ref[...].astype(F32) + bg_ref[...])
        g0, g1 = gates[:, :D_MODEL], gates[:, D_MODEL:]
        y_pool, y_lru = yp_ref[...].astype(F32), yl_ref[...].astype(F32)
        dwo_ref[...] += _tn((g0 * y_pool + g1 * y_lru).astype(BF16), db)
        dz0 = dm * y_pool * g0 * (1.0 - g0)
        dz1 = dm * y_lru * g1 * (1.0 - g1)
        vec_ref[0:1, :] += _colsum(dz0)
        vec_ref[1:2, :] += _colsum(dz1)
        dzt_ref[:, :D_MODEL] = dz0.astype(BF16)
        dzt_ref[:, D_MODEL:] = dz1.astype(BF16)
        dyp_ref[...] = (dm * g0).astype(BF16)
        dyl_ref[...] = (dm * g1).astype(BF16)

    return pl.pallas_call(
        body, name="b2_gates", grid=(T // tm,),
        in_specs=[_rows(tm, D_MODEL), _rows(tm, 2 * D_MODEL), _rows(tm, D_MODEL), _rows(tm, D_MODEL),
                  _resident(b_gate.shape), _resident(w_out.shape)],
        out_specs=[_rows(tm, 2 * D_MODEL), _rows(tm, D_MODEL), _rows(tm, D_MODEL), _acc(w_out.shape),
                   _acc((SUBLANES, D_MODEL))],
        out_shape=[jax.ShapeDtypeStruct((T, 2 * D_MODEL), BF16), jax.ShapeDtypeStruct((T, D_MODEL), BF16),
                   jax.ShapeDtypeStruct((T, D_MODEL), BF16), jax.ShapeDtypeStruct(w_out.shape, F32),
                   jax.ShapeDtypeStruct((SUBLANES, D_MODEL), F32)],
        compiler_params=_params(("arbitrary",)),
    )(dh1, zt, yp, yl, b_gate, w_out)


def _b12_pool_in_proj(dyp, zp, dzl, dzg, dzt, x, dh1, norm1_g, w_in, pool_w, pool_scale, pool_proj):
    T = zp.shape[0]
    tt = ROW_TILE
    nt = T // tt

    def body(dy_ref, zp_ref, zph_ref, dzl_ref, dzg_ref, dzt_ref, x_ref, dh_ref, g1_ref, win_ref, pw_ref, ps_ref, pp_ref,
             dzp_ref, dx_ref, dpp_ref, dpw_ref, vec_ref, q_next):
        i = pl.program_id(0)
        ti = nt - 1 - i
        first_row = ti * tt

        @pl.when(i == 0)
        def _():
            dpp_ref[...] = jnp.zeros_like(dpp_ref)
            dpw_ref[...] = jnp.zeros_like(dpw_ref)
            vec_ref[...] = jnp.zeros_like(vec_ref)
            q_next[...] = jnp.zeros_like(q_next)

        du_parts = []

        def project(lo):
            k = max(i for i in range(4) if IN_SPLITS[i] <= lo)
            dz_ref = (None, dzl_ref, dzg_ref, dzt_ref)[k]
            at = lo - IN_SPLITS[k]
            part = _nt(dz_ref[:, at:at + PROJ_CHUNK], win_ref[:, lo:lo + PROJ_CHUNK])
            du_parts[:] = [part if not du_parts else du_parts[0] + part]

        mxu = _Interleaved(functools.partial(project, lo) for lo in range(IN_SPLITS[1], IN_SPLITS[4], PROJ_CHUNK))

        keep = (ti > 0).astype(F32)
        zp_cat = jnp.concatenate([zph_ref[...] * keep, zp_ref[...]], axis=0)
        pooled, mixed = _pool_forward(zp_cat, pw_ref, first_row, mxu.tick)
        dy = dy_ref[...]
        dpp_ref[...] += _tn((mixed * ps_ref[...]).astype(BF16), dy)
        mxu.tick(2)
        dms = _nt(dy, pp_ref[...])
        mxu.tick(2)
        vec_ref[0:1, :POOL_WIDTH] += _colsum(dms * mixed)
        dmixed = (dms * ps_ref[...]).astype(BF16)
        t_glob = first_row + _row_ids((tt, POOL_GROUP_DIM))
        dz, q_all, dpooled_pairs = [], [], []
        for p in range(len(POOL_WINDOWS) // 2):
            pair = slice(p * PAIR_DIM, (p + 1) * PAIR_DIM)
            dpw_ref[p] += _tn(pooled[:, pair].astype(BF16), dmixed[:, pair])
            dpooled_pairs.append(_nt(dmixed[:, pair], pw_ref[p]))
        dpooled_all = jnp.concatenate(dpooled_pairs, axis=1)
        for g, w in enumerate(POOL_WINDOWS):
            cols = slice(g * POOL_GROUP_DIM, (g + 1) * POOL_GROUP_DIM)
            dpooled = dpooled_all[:, cols]
            q = dpooled / jnp.minimum(t_glob + 1, w).astype(F32)
            q_all.append(q)
            s, k = jnp.concatenate([q, q_next[:, cols]], axis=0), 1
            while k < w:
                s = s + _shift_up(s, k)
                k *= 2
            dz.append(s[:tt] - dpooled)
            mxu.tick(2)
        dzp = jnp.concatenate(dz, axis=1).astype(BF16)
        dzp_ref[...] = dzp
        q_next[...] = jnp.concatenate([q[:POOL_HALO] for q in q_all], axis=1)
        mxu.flush()

        du = du_parts[0] + _nt(dzp, win_ref[:, IN_SPLITS[0]:IN_SPLITS[1]])
        n1, r1 = _rms(x_ref[...])
        vec_ref[1:2, :] += _colsum(du * n1)
        dx_ref[...] = dh_ref[...] + _rms_bwd(du * g1_ref[...], n1, r1)

    rev = functools.partial(_rows_rev, n_tiles=nt)
    res = [norm1_g, w_in, pool_w, pool_scale, pool_proj]
    return pl.pallas_call(
        body, name="b12_pool_in_proj", grid=(nt,),
        in_specs=[rev(tt, D_MODEL), rev(tt, POOL_WIDTH), _halo_before_rev(POOL_HALO, POOL_WIDTH, tt, nt),
                  rev(tt, D_MODEL), rev(tt, D_MODEL), rev(tt, 2 * D_MODEL), rev(tt, D_MODEL), rev(tt, D_MODEL)]
        + [_resident(w.shape) for w in res],
        out_specs=[rev(tt, POOL_WIDTH), rev(tt, D_MODEL), _acc(pool_proj.shape), _acc(pool_w.shape),
                   _acc((SUBLANES, D_MODEL))],
        out_shape=[jax.ShapeDtypeStruct((T, POOL_WIDTH), BF16), jax.ShapeDtypeStruct((T, D_MODEL), F32),
                   jax.ShapeDtypeStruct(pool_proj.shape, F32), jax.ShapeDtypeStruct(pool_w.shape, F32),
                   jax.ShapeDtypeStruct((SUBLANES, D_MODEL), F32)],
        scratch_shapes=[pltpu.VMEM((POOL_HALO, POOL_WIDTH), F32)],
        compiler_params=_params(("arbitrary",)),
    )(dyp, zp, zp, dzl, dzg, dzt, x, dh1, *res)


_V_CONVW, _V_CONVB, _V_BRG, _V_BIG, _V_LAM = 0, 4, 5, 6, 7


def _b2_lru(dyl, zl, zg, hs, xc_saved, r_saved, ig_saved, conv_w, w_rg, w_ig, lru_lambda, lru_proj):
    T = zl.shape[0]
    tt = ROW_TILE
    nt = T // tt
    n_groups = tt // SUBLANES

    def body(dy_ref, zl_ref, zlh_ref, zg_ref, hs_ref, hsh_ref, xc_ref, r_ref, ig_ref, cw_ref, wrg_ref, wig_ref,
             lam_ref, lp_ref, dzl_ref, dzg_ref, dlp_ref, dwrg_ref, dwig_ref, vec_ref,
             c_s, d_s, g_s, g_next, a_next, dxc_next):
        i = pl.program_id(0)
        ti = nt - 1 - i
        first_row = ti * tt

        @pl.when(i == 0)
        def _():
            dlp_ref[...] = jnp.zeros_like(dlp_ref)
            dwrg_ref[...] = jnp.zeros_like(dwrg_ref)
            dwig_ref[...] = jnp.zeros_like(dwig_ref)
            vec_ref[...] = jnp.zeros_like(vec_ref)
            g_next[...] = jnp.zeros_like(g_next)
            a_next[...] = jnp.zeros_like(a_next)
            dxc_next[...] = jnp.zeros_like(dxc_next)

        keep = (ti > 0).astype(F32)
        sp, dsp_dlam = _softplus_neg(lam_ref[...])
        hs = hs_ref[...]
        gelu, dgelu = _gelu_and_grad(zg_ref[...])
        dy = dy_ref[...]
        dlp_ref[...] += _tn((hs * gelu).astype(BF16), dy)
        dyl = _nt(dy, lp_ref[...])
        dzg_ref[...] = (dyl * hs * dgelu).astype(BF16)

        d_s[...] = dyl * gelu
        a_tile = jnp.exp(-LRU_C * r_ref[...].astype(F32) * sp)
        c_s[...] = _shift_up(jnp.concatenate([a_tile, a_next[...]], axis=0), 1)[:tt]
        a_next[...] = jnp.broadcast_to(a_tile[0:1, :], (SUBLANES, D_MODEL))
        rows8 = _row_ids((SUBLANES, D_MODEL))

        def group(k, carry):
            at = pl.ds(pl.multiple_of((n_groups - 1 - k) * SUBLANES, SUBLANES), SUBLANES)
            C, Dv = c_s[at, :], d_s[at, :]
            for s in (1, 2, 4):
                m = rows8 < SUBLANES - s
                Dv = jnp.where(m, C * pltpu.roll(Dv, SUBLANES - s, 0) + Dv, Dv)
                C = jnp.where(m, C * pltpu.roll(C, SUBLANES - s, 0), C)
            G = C * carry + Dv
            g_s[at, :] = G
            return jnp.broadcast_to(G[0:1, :], (SUBLANES, D_MODEL))

        g_next[...] = lax.fori_loop(0, n_groups, group, g_next[...])
        G = g_s[...]

        cw = cw_ref[...]
        zl_cat = jnp.concatenate([zlh_ref[...] * keep, zl_ref[...]], axis=0)
        xc, r, ig = xc_ref[...].astype(F32), r_ref[...].astype(F32), ig_ref[...].astype(F32)
        a, mult = _decay(r, sp, first_row)
        h_prev = _shift_down(jnp.concatenate([hsh_ref[...] * keep, hs_ref[...]], axis=0), 1)[SUBLANES:]
        t_glob = first_row + _row_ids((tt, D_MODEL))
        dmult = jnp.where(t_glob == 0, 0.0, G * ig * xc)
        dla = G * h_prev * a - dmult * (a * a) / mult
        vec_ref[_V_LAM:_V_LAM + 1, :] += _colsum(dla * r) * (-LRU_C) * dsp_dlam
        dpr = dla * (-LRU_C) * sp * r * (1.0 - r)
        dpi = G * mult * xc * ig * (1.0 - ig)
        vec_ref[_V_BRG:_V_BRG + 1, :] += _colsum(dpr)
        vec_ref[_V_BIG:_V_BIG + 1, :] += _colsum(dpi)
        dprb, dpib, xh = dpr.astype(BF16), dpi.astype(BF16), xc_ref[...]
        dxc_h = []
        for p in range(LRU_HEADS // 2):
            cols = slice(p * PAIR_DIM, (p + 1) * PAIR_DIM)
            dwrg_ref[p] += _tn(xh[:, cols], dprb[:, cols])
            dwig_ref[p] += _tn(xh[:, cols], dpib[:, cols])
            dxc_h.append(_nt(dprb[:, cols], wrg_ref[p]) + _nt(dpib[:, cols], wig_ref[p]))
        dxc = G * mult * ig + jnp.concatenate(dxc_h, axis=1)

        vec_ref[_V_CONVB:_V_CONVB + 1, :] += _colsum(dxc)
        dxc_cat = jnp.concatenate([dxc, dxc_next[...]], axis=0)
        dzl = cw[CONV_WIDTH - 1:CONV_WIDTH] * dxc
        for k in range(CONV_WIDTH):
            lag = CONV_WIDTH - 1 - k
            vec_ref[_V_CONVW + k:_V_CONVW + k + 1, :] += _colsum(dxc * _shift_down(zl_cat, lag)[SUBLANES:])
            if lag:
                dzl = dzl + cw[k:k + 1] * _shift_up(dxc_cat, lag)[:tt]
        dzl_ref[...] = dzl.astype(BF16)
        dxc_next[...] = dxc[:SUBLANES]

    res = [conv_w, w_rg, w_ig, lru_lambda, lru_proj]
    return pl.pallas_call(
        body, name="b2_lru", grid=(nt,),
        in_specs=[_rows_rev(tt, D_MODEL, nt), _rows_rev(tt, D_MODEL, nt), _halo_before_rev(SUBLANES, D_MODEL, tt, nt),
                  _rows_rev(tt, D_MODEL, nt), _rows_rev(tt, D_MODEL, nt), _halo_before_rev(SUBLANES, D_MODEL, tt, nt)]
        + [_rows_rev(tt, D_MODEL, nt)] * 3 + [_resident(w.shape) for w in res],
        out_specs=[_rows_rev(tt, D_MODEL, nt), _rows_rev(tt, D_MODEL, nt), _acc(lru_proj.shape), _acc(w_rg.shape),
                   _acc(w_ig.shape), _acc((SUBLANES, D_MODEL))],
        out_shape=[jax.ShapeDtypeStruct((T, D_MODEL), BF16), jax.ShapeDtypeStruct((T, D_MODEL), BF16),
                   jax.ShapeDtypeStruct(lru_proj.shape, F32), jax.ShapeDtypeStruct(w_rg.shape, F32),
                   jax.ShapeDtypeStruct(w_ig.shape, F32), jax.ShapeDtypeStruct((SUBLANES, D_MODEL), F32)],
        scratch_shapes=[pltpu.VMEM((tt, D_MODEL), F32)] * 3 + [pltpu.VMEM((SUBLANES, D_MODEL), F32)] * 3,
        compiler_params=_params(("arbitrary",)),
    )(dyl, zl, zl, zg, hs, hs, xc_saved, r_saved, ig_saved, *res)


WHOLE_BLOCK_BYTES = 3 * 512 * 1024


def _row_tile(rows, cols):
    for t in (rows, rows // 2):
        if t % 16 == 0 and rows % t == 0 and t * cols * 4 <= WHOLE_BLOCK_BYTES:
            return t
    for t in (512, 256, 128, 64, 32, 16, 8):
        if rows % t == 0:
            return t
    return rows


def _scalar_grid(grid, in_specs, out_specs):
    return pltpu.PrefetchScalarGridSpec(num_scalar_prefetch=1, grid=grid, in_specs=in_specs, out_specs=out_specs)


def _cast_into_block(w, by_rows, shard_j, name):
    R, C = w.shape
    tr = _row_tile(R, C)
    if by_rows:
        out_shape, out_map = (N_SHARDS * R, C), lambda i, j: (j[0] * (R // tr) + i, 0)
    else:
        out_shape, out_map = (R, N_SHARDS * C), lambda i, j: (i, j[0])

    def body(j_ref, w_ref, o_ref):
        o_ref[...] = w_ref[...].astype(BF16)

    return pl.pallas_call(
        body, name=name,
        grid_spec=_scalar_grid((R // tr,), [pl.BlockSpec((tr, C), lambda i, j: (i, 0))], pl.BlockSpec((tr, C), out_map)),
        out_shape=jax.ShapeDtypeStruct(out_shape, BF16),
        compiler_params=_params(("arbitrary",)),
    )(shard_j.reshape(1), w)


def _sum_cores(g, theirs, core, name):
    S, R, C = g.shape
    H = R // 2
    tr = _row_tile(H, C)
    nh = H // tr

    def body(c_ref, g_ref, t_ref, o_ref):
        o_ref[...] = (g_ref[...] + t_ref[...]).astype(BF16)

    half = pl.BlockSpec((None, tr, C), lambda s, i, c: (s, i, 0))
    return pl.pallas_call(
        body, name=name,
        grid_spec=_scalar_grid((S, nh), [pl.BlockSpec((None, tr, C), lambda s, i, c: (s, c[0] * nh + i, 0)), half], half),
        out_shape=jax.ShapeDtypeStruct((S, H, C), BF16),
        compiler_params=_params(("arbitrary", "arbitrary")),
    )(core.reshape(1), g, theirs)


def _sum_chips(sums, slots, by_rows, place, name):
    _, H, C = slots.shape
    tr = _row_tile(H, C)
    own_map = (lambda i, p: (p[0], i, 0)) if by_rows else (lambda i, p: (0, i, p[0]))

    def body(p_ref, s_ref, q_ref, o_ref):
        o_ref[...] = ((s_ref[...].astype(F32) + q_ref[0].astype(F32)) + q_ref[1].astype(F32)) + q_ref[2].astype(F32)

    return pl.pallas_call(
        body, name=name,
        grid_spec=_scalar_grid(
            (H // tr,),
            [pl.BlockSpec((None, tr, C), own_map), pl.BlockSpec((3, tr, C), lambda i, p: (0, i, 0))],
            pl.BlockSpec((None, tr, C), lambda i, p: (p[1], i, 0))),
        out_shape=jax.ShapeDtypeStruct((2, H, C), F32),
        compiler_params=_params(("arbitrary",)),
    )(place, sums, slots)


def _adamw(w, g, m, v, name):
    R, C = w.shape
    tr = _row_tile(R, C)
    c1 = 1.0 - ADAM_B1 ** ADAM_STEP
    c2 = 1.0 - ADAM_B2 ** ADAM_STEP

    def body(w_ref, g_ref, m_ref, v_ref, d_ref, nm_ref, nv_ref):
        gv = g_ref[...]
        nm = ADAM_B1 * m_ref[...] + (1.0 - ADAM_B1) * gv
        nv = ADAM_B2 * v_ref[...] + (1.0 - ADAM_B2) * (gv * gv)
        d_ref[...] = -ADAM_LR * ((nm / c1) / (jnp.sqrt(nv / c2) + ADAM_EPS) + ADAM_WD * w_ref[...])
        nm_ref[...] = nm
        nv_ref[...] = nv

    return pl.pallas_call(
        body, name=name, grid=(R // tr,),
        in_specs=[_rows(tr, C)] * 4, out_specs=[_rows(tr, C)] * 3,
        out_shape=[jax.ShapeDtypeStruct((R, C), F32)] * 3,
        compiler_params=_params(("arbitrary",)),
    )(w, g, m, v)


def _place():
    return lax.axis_index("x"), lax.axis_index("y"), lax.axis_index("c")


def _other_chips(x, y):
    return [(1 - x, y), (x, 1 - y), (1 - x, 1 - y)]


def _shard_block(ref, by_rows, R, C, j, half_rows=None):
    if half_rows is None:
        rows, r0 = R, 0
    else:
        rows = R // 2
        r0 = pl.multiple_of(half_rows * rows, 16)
    if by_rows:
        return ref.at[pl.ds(pl.multiple_of(j * R, 16) + r0, rows), :]
    return ref.at[pl.ds(r0, rows), pl.ds(pl.multiple_of(j * C, 128), C)]


def _all_gather_weights(gathered, shapes, by_rows, small):
    n = len(gathered)

    def body(*refs):
        small_in = refs[n]
        outs, small_out = refs[n + 1:2 * n + 1], refs[2 * n + 1]
        send_sems, recv_sems, local_sem = refs[2 * n + 2:]
        x, y, c = _place()
        me_j = 2 * x + y
        chips = _other_chips(x, y)
        sibling = (x, y, 1 - c)

        def block(i, j, half):
            R, C = shapes[i]
            return _shard_block(outs[i], by_rows[i], R, C, j, half)

        def ici(i, k, src_j):
            return pltpu.make_async_remote_copy(
                src_ref=block(i, src_j, c), dst_ref=block(i, src_j, c),
                send_sem=send_sems.at[6 * i + k], recv_sem=recv_sems.at[6 * i + k],
                device_id=(*chips[k], c), device_id_type=MESH)

        def relay(i, k, half):
            kj = 2 * chips[k][0] + chips[k][1]
            return pltpu.make_async_remote_copy(
                src_ref=block(i, kj, half), dst_ref=block(i, kj, half),
                send_sem=send_sems.at[6 * i + 3 + k], recv_sem=recv_sems.at[6 * i + 3 + k],
                device_id=sibling, device_id_type=MESH)

        def small_copy(k, src_j):
            cols = pl.ds(pl.multiple_of(src_j * 256, 128), 256)
            return pltpu.make_async_remote_copy(
                src_ref=small_in, dst_ref=small_out.at[:, cols],
                send_sem=send_sems.at[6 * n + k], recv_sem=recv_sems.at[6 * n + k],
                device_id=(*chips[k], c), device_id_type=MESH)

        sends = []
        for i in range(n):
            for k in range(3):
                cp = ici(i, k, me_j)
                cp.start()
                sends.append(cp)
        for k in range(3):
            cp = small_copy(k, me_j)
            cp.start()
            sends.append(cp)
        local = pltpu.make_async_copy(small_in, small_out.at[:, pl.ds(pl.multiple_of(me_j * 256, 128), 256)], local_sem)
        local.start()
        for i in range(n):
            for k in range(3):
                kj = 2 * chips[k][0] + chips[k][1]
                ici(i, k, kj).wait_recv()
                cp = relay(i, k, c)
                cp.start()
                sends.append(cp)
        for k in range(3):
            small_copy(k, 2 * chips[k][0] + chips[k][1]).wait_recv()
        for i in range(n):
            for k in range(3):
                relay(i, k, 1 - c).wait_recv()
        for cp in sends:
            cp.wait_send()
        local.wait()

    out_shape = [jax.ShapeDtypeStruct(g.shape, BF16) for g in gathered]
    out_shape.append(jax.ShapeDtypeStruct((8, N_SHARDS * 256), F32))
    n_sems = 6 * n + 3
    return pl.pallas_call(
        body, name="all_gather_weights",
        in_specs=[ANY] * (n + 1), out_specs=[ANY] * (n + 1), out_shape=out_shape,
        input_output_aliases={i: i for i in range(n)},
        scratch_shapes=[pltpu.SemaphoreType.DMA((n_sems,)), pltpu.SemaphoreType.DMA((n_sems,)),
                        pltpu.SemaphoreType.DMA],
    )(*gathered, small)


def _core_exchange(grads, name):
    n = len(grads)

    def body(*refs):
        copies = _core_exchange_copies(refs[:n], refs[n:2 * n], refs[2 * n], refs[2 * n + 1])
        for cp in copies:
            cp.start()
        for cp in copies:
            cp.wait()

    return pl.pallas_call(
        body, name=name,
        in_specs=[ANY] * n, out_specs=[ANY] * n,
        out_shape=[jax.ShapeDtypeStruct((g.shape[0], g.shape[1] // 2, g.shape[2]), F32) for g in grads],
        scratch_shapes=[pltpu.SemaphoreType.DMA((n,))] * 2,
    )(*grads)


HBM = pl.BlockSpec(memory_space=pltpu.HBM)
SEM = pl.BlockSpec(memory_space=pltpu.SEMAPHORE)
TOKEN = jax.ShapeDtypeStruct((SUBLANES, 128), F32)


def _in_hbm(a):
    return pltpu.with_memory_space_constraint(a, pltpu.HBM)


def _split_params():
    return pltpu.CompilerParams(has_side_effects=pltpu.SideEffectType.DATAFLOW_SIDE_EFFECTING)


def _gather_rest_copies(refs, shapes, by_rows, send_sems, recv_sems):
    x, y, c = _place()
    me_j = 2 * x + y
    chips = _other_chips(x, y)
    pairs = []
    for i, ref in enumerate(refs):
        R, C = shapes[i]
        for k in range(3):
            kj = 2 * chips[k][0] + chips[k][1]

            def copy(j, ref=ref, i=i, k=k, R=R, C=C):
                blk = _shard_block(ref, by_rows[i], R, C, j)
                return pltpu.make_async_remote_copy(
                    src_ref=blk, dst_ref=blk, send_sem=send_sems.at[3 * i + k], recv_sem=recv_sems.at[3 * i + k],
                    device_id=(*chips[k], c), device_id_type=MESH)

            pairs.append((copy(me_j), copy(kj)))
    return pairs


def _gather_rest_start(gathered, shapes, by_rows, after):
    n = len(gathered)

    def body(*refs):
        ins = refs[:n]
        send_sems, recv_sems = refs[n + 1], refs[n + 2]
        token = refs[-1]
        for mine, _ in _gather_rest_copies(ins, shapes, by_rows, send_sems, recv_sems):
            mine.start()
        token[...] = jnp.zeros_like(token)

    out = pl.pallas_call(
        body, name="gather_rest_start",
        out_shape=(pltpu.SemaphoreType.DMA((3 * n,)), pltpu.SemaphoreType.DMA((3 * n,)),
                   *[pltpu.HBM(g.shape, g.dtype) for g in gathered], TOKEN),
        in_specs=[HBM] * n + [ANY], out_specs=(SEM, SEM, *[HBM] * n, pl.BlockSpec(memory_space=pltpu.VMEM)),
        input_output_aliases={i: 2 + i for i in range(n)},
        compiler_params=_split_params(),
    )(*[_in_hbm(g) for g in gathered], after)
    return out[0], out[1], out[2:2 + n], out[-1]


def _gather_rest_wait(send_sems, recv_sems, gathered, shapes, by_rows, after):
    n = len(gathered)

    def body(*refs):
        ins = refs[:n]
        send, recv = refs[n], refs[n + 1]
        for mine, theirs in _gather_rest_copies(ins, shapes, by_rows, send, recv):
            mine.wait_send()
            theirs.wait_recv()

    return pl.pallas_call(
        body, name="gather_rest_wait",
        out_shape=tuple(pltpu.HBM(g.shape, g.dtype) for g in gathered),
        in_specs=[HBM] * n + [SEM, SEM, ANY], out_specs=tuple([HBM] * n),
        input_output_aliases={i: i for i in range(n)},
        compiler_params=_split_params(),
    )(*gathered, send_sems, recv_sems, after)


def _chip_exchange_copies(ins, slots, dims, by_rows, send_sems, recv_sems):
    x, y, c = _place()
    chips = _other_chips(x, y)
    pairs = []
    for i in range(len(ins)):
        for k in range(3):
            kj = 2 * chips[k][0] + chips[k][1]
            if by_rows[i]:
                src = ins[i].at[kj]
            else:
                src = ins[i].at[0, :, pl.ds(pl.multiple_of(kj * dims[i][1], 128), dims[i][1])]
            cp = pltpu.make_async_remote_copy(
                src_ref=src, dst_ref=slots[i].at[k], send_sem=send_sems.at[3 * i + k], recv_sem=recv_sems.at[3 * i + k],
                device_id=(*chips[k], c), device_id_type=MESH)
            pairs.append((cp, cp))
    return pairs


def _exchange_dims(sums, by_rows):
    return [(s.shape[1], s.shape[2]) if by_rows[i] else (s.shape[1], s.shape[2] // N_SHARDS) for i, s in enumerate(sums)]


def _chip_exchange_start(sums, by_rows, tag):
    n = len(sums)
    sums = list(sums)
    dims = _exchange_dims(sums, by_rows)
    slots = [lax.empty((3, h, cc), BF16) for h, cc in dims]

    def body(*refs):
        ins, land = refs[:n], refs[n:2 * n]
        send_sems, recv_sems = refs[2 * n], refs[2 * n + 1]
        token = refs[-1]
        for cp, _ in _chip_exchange_copies(ins, land, dims, by_rows, send_sems, recv_sems):
            cp.start()
        token[...] = jnp.zeros_like(token)

    out = pl.pallas_call(
        body, name="grad_chip_exchange_start_" + tag,
        out_shape=(pltpu.SemaphoreType.DMA((3 * n,)), pltpu.SemaphoreType.DMA((3 * n,)),
                   *[pltpu.HBM(a.shape, a.dtype) for a in sums + slots], TOKEN),
        in_specs=[HBM] * (2 * n), out_specs=(SEM, SEM, *[HBM] * (2 * n), pl.BlockSpec(memory_space=pltpu.VMEM)),
        input_output_aliases={i: 2 + i for i in range(2 * n)},
        compiler_params=_split_params(),
    )(*[_in_hbm(a) for a in sums + slots])
    return out[0], out[1], out[2:2 + n], out[2 + n:2 + 2 * n], out[-1]


def _chip_exchange_wait(send_sems, recv_sems, sums, slots, by_rows, after, tag):
    n = len(sums)
    sums, slots = list(sums), list(slots)
    dims = _exchange_dims(sums, by_rows)

    def body(*refs):
        ins, land = refs[:n], refs[n:2 * n]
        send, recv = refs[2 * n], refs[2 * n + 1]
        for cp, _ in _chip_exchange_copies(ins, land, dims, by_rows, send, recv):
            cp.wait_send()
            cp.wait_recv()

    out = pl.pallas_call(
        body, name="grad_chip_exchange_wait_" + tag,
        out_shape=tuple(pltpu.HBM(a.shape, a.dtype) for a in sums + slots),
        in_specs=[HBM] * (2 * n) + [SEM, SEM, ANY], out_specs=tuple([HBM] * (2 * n)),
        input_output_aliases={i: i for i in range(2 * n)},
        compiler_params=_split_params(),
    )(*sums, *slots, send_sems, recv_sems, after)
    return out[:n], out[n:]


def _core_exchange_copies(ins, theirs, send_sems, recv_sems):
    x, y, c = _place()
    copies = []
    for i in range(len(ins)):
        H = ins[i].shape[1] // 2
        copies.append(pltpu.make_async_remote_copy(
            src_ref=ins[i].at[:, pl.ds(pl.multiple_of((1 - c) * H, 8), H), :], dst_ref=theirs[i],
            send_sem=send_sems.at[i], recv_sem=recv_sems.at[i], device_id=(x, y, 1 - c), device_id_type=MESH))
    return copies


def _core_exchange_start(grads):
    n = len(grads)
    grads = list(grads)
    theirs = [lax.empty((g.shape[0], g.shape[1] // 2, g.shape[2]), F32) for g in grads]

    def body(*refs):
        for cp in _core_exchange_copies(refs[:n], refs[n:2 * n], refs[2 * n], refs[2 * n + 1]):
            cp.start()
        refs[-1][...] = jnp.zeros_like(refs[-1])

    out = pl.pallas_call(
        body, name="grad_core_exchange_start",
        out_shape=(pltpu.SemaphoreType.DMA((n,)), pltpu.SemaphoreType.DMA((n,)),
                   *[pltpu.HBM(a.shape, a.dtype) for a in grads + theirs], TOKEN),
        in_specs=[HBM] * (2 * n), out_specs=(SEM, SEM, *[HBM] * (2 * n), pl.BlockSpec(memory_space=pltpu.VMEM)),
        input_output_aliases={i: 2 + i for i in range(2 * n)},
        compiler_params=_split_params(),
    )(*[_in_hbm(a) for a in grads + theirs])
    return out[0], out[1], out[2:2 + n], out[2 + n:2 + 2 * n], out[-1]


def _core_exchange_wait(send_sems, recv_sems, grads, theirs, after):
    n = len(grads)
    grads, theirs = list(grads), list(theirs)

    def body(*refs):
        for cp in _core_exchange_copies(refs[:n], refs[n:2 * n], refs[2 * n], refs[2 * n + 1]):
            cp.wait_send()
            cp.wait_recv()

    out = pl.pallas_call(
        body, name="grad_core_exchange_wait",
        out_shape=tuple(pltpu.HBM(a.shape, a.dtype) for a in grads + theirs),
        in_specs=[HBM] * (2 * n) + [SEM, SEM, ANY], out_specs=tuple([HBM] * (2 * n)),
        input_output_aliases={i: i for i in range(2 * n)},
        compiler_params=_split_params(),
    )(*grads, *theirs, send_sems, recv_sems, after)
    return out[:n], out[n:]


def _core_share(reduced, tag):
    n = len(reduced)

    def body(*refs):
        outs = refs[n:2 * n]
        send_sems, recv_sems = refs[2 * n:]
        x, y, c = _place()
        copies = []
        for i in range(n):
            cp = pltpu.make_async_remote_copy(
                src_ref=outs[i].at[c], dst_ref=outs[i].at[c], send_sem=send_sems.at[i], recv_sem=recv_sems.at[i],
                device_id=(x, y, 1 - c), device_id_type=MESH)
            cp.start()
            copies.append(cp)
        for cp in copies:
            cp.wait()

    return pl.pallas_call(
        body, name="grad_core_share_" + tag,
        in_specs=[ANY] * n, out_specs=[ANY] * n,
        out_shape=[jax.ShapeDtypeStruct(r.shape, F32) for r in reduced],
        input_output_aliases={i: i for i in range(n)},
        scratch_shapes=[pltpu.SemaphoreType.DMA((n,))] * 2,
    )(*reduced)


def _small_exchange_copies(pack_ref, slots_ref, send_sems, recv_sems):
    x, y, c = _place()
    peers = [(px, py, pc) for px in (x, 1 - x) for py in (y, 1 - y) for pc in (c, 1 - c)][1:]
    pairs = []
    for k, peer in enumerate(peers):
        def copy(sender, k=k, peer=peer):
            return pltpu.make_async_remote_copy(
                src_ref=pack_ref, dst_ref=slots_ref.at[4 * sender[0] + 2 * sender[1] + sender[2]],
                send_sem=send_sems.at[k], recv_sem=recv_sems.at[k], device_id=peer, device_id_type=MESH)

        pairs.append((copy((x, y, c)), copy(peer)))
    return pairs


def _small_exchange_start(pack):
    slots = lax.empty((N_DEV,) + pack.shape, F32)

    def body(pack_ref, slots_ref, send_sems, recv_sems, pack_thru, slots_thru, token):
        for mine, _ in _small_exchange_copies(pack_ref, slots_ref, send_sems, recv_sems):
            mine.start()
        token[...] = jnp.zeros_like(token)

    return pl.pallas_call(
        body, name="grad_small_exchange_start",
        out_shape=(pltpu.SemaphoreType.DMA((N_DEV - 1,)), pltpu.SemaphoreType.DMA((N_DEV - 1,)),
                   pltpu.HBM(pack.shape, F32), pltpu.HBM(slots.shape, F32), TOKEN),
        in_specs=[HBM, HBM], out_specs=(SEM, SEM, HBM, HBM, pl.BlockSpec(memory_space=pltpu.VMEM)),
        input_output_aliases={0: 2, 1: 3},
        compiler_params=_split_params(),
    )(_in_hbm(pack), _in_hbm(slots))


def _small_exchange_wait(send_sems, recv_sems, pack, slots, after):
    def body(pack_ref, slots_ref, send, recv, after_ref, pack_thru, slots_thru):
        for mine, theirs in _small_exchange_copies(pack_ref, slots_ref, send, recv):
            mine.wait_send()
            theirs.wait_recv()

    return pl.pallas_call(
        body, name="grad_small_exchange_wait",
        out_shape=(pltpu.HBM(pack.shape, F32), pltpu.HBM(slots.shape, F32)),
        in_specs=[HBM, HBM, SEM, SEM, ANY], out_specs=(HBM, HBM),
        input_output_aliases={0: 0, 1: 1},
        compiler_params=_split_params(),
    )(pack, slots, send_sems, recv_sems, after)


def _sum_small(pack, slots, me):
    R, C = pack.shape
    tr = _row_tile(R, C)

    def body(me_ref, p_ref, q_ref, o_ref):
        acc = jnp.where(me_ref[0] == 0, p_ref[...], q_ref[0])
        for d in range(1, N_DEV):
            acc = acc + jnp.where(me_ref[0] == d, p_ref[...], q_ref[d])
        o_ref[...] = acc

    return pl.pallas_call(
        body, name="sum_small",
        grid_spec=_scalar_grid((R // tr,), [pl.BlockSpec((tr, C), lambda i, m: (i, 0)),
                                            pl.BlockSpec((N_DEV, tr, C), lambda i, m: (0, i, 0))],
                               pl.BlockSpec((tr, C), lambda i, m: (i, 0))),
        out_shape=jax.ShapeDtypeStruct((R, C), F32),
        compiler_params=_params(("arbitrary",)),
    )(me.reshape(1), pack, slots)


def _pack_rows(parts, rows):
    flat = jnp.concatenate([a.reshape(-1) for a in parts])
    return jnp.pad(flat, (0, rows * 128 - flat.shape[0])).reshape(rows, 128)


def _unpack_rows(pack, shapes):
    flat = pack.reshape(-1)
    out, at = [], 0
    for s in shapes:
        size = 1
        for d in s:
            size *= d
        out.append(flat[at:at + size].reshape(s))
        at += size
    return out


def kernel(x, p, norm1_g, w_in, b_gate, pool_w, pool_scale, pool_proj, conv_w, conv_b, w_rg, b_rg, w_ig, b_ig, lru_lambda, lru_proj, w_out, norm2_g, w_ffn_in, w_ffn_out, ple_norm_g, w_ple_gate, w_ple_proj, final_g, loss_target, m_norm1_g, m_w_in, m_b_gate, m_pool_w, m_pool_scale, m_pool_proj, m_conv_w, m_conv_b, m_w_rg, m_b_rg, m_w_ig, m_b_ig, m_lru_lambda, m_lru_proj, m_w_out, m_norm2_g, m_w_ffn_in, m_w_ffn_out, m_ple_norm_g, m_w_ple_gate, m_w_ple_proj, m_final_g, v_norm1_g, v_w_in, v_b_gate, v_pool_w, v_pool_scale, v_pool_proj, v_conv_w, v_conv_b, v_w_rg, v_b_rg, v_w_ig, v_b_ig, v_lru_lambda, v_lru_proj, v_w_out, v_norm2_g, v_w_ffn_in, v_w_ffn_out, v_ple_norm_g, v_w_ple_gate, v_w_ple_proj, v_final_g):
    weights = dict(norm1_g=norm1_g, w_in=w_in, b_gate=b_gate, pool_w=pool_w, pool_scale=pool_scale,
                   pool_proj=pool_proj, conv_w=conv_w, conv_b=conv_b, w_rg=w_rg, b_rg=b_rg, w_ig=w_ig, b_ig=b_ig,
                   lru_lambda=lru_lambda, lru_proj=lru_proj, w_out=w_out, norm2_g=norm2_g, w_ffn_in=w_ffn_in,
                   w_ffn_out=w_ffn_out, ple_norm_g=ple_norm_g, w_ple_gate=w_ple_gate, w_ple_proj=w_ple_proj,
                   final_g=final_g)
    m_in = dict(norm1_g=m_norm1_g, w_in=m_w_in, b_gate=m_b_gate, pool_w=m_pool_w, pool_scale=m_pool_scale,
                pool_proj=m_pool_proj, conv_w=m_conv_w, conv_b=m_conv_b, w_rg=m_w_rg, b_rg=m_b_rg, w_ig=m_w_ig,
                b_ig=m_b_ig, lru_lambda=m_lru_lambda, lru_proj=m_lru_proj, w_out=m_w_out, norm2_g=m_norm2_g,
                w_ffn_in=m_w_ffn_in, w_ffn_out=m_w_ffn_out, ple_norm_g=m_ple_norm_g, w_ple_gate=m_w_ple_gate,
                w_ple_proj=m_w_ple_proj, final_g=m_final_g)
    v_in = dict(norm1_g=v_norm1_g, w_in=v_w_in, b_gate=v_b_gate, pool_w=v_pool_w, pool_scale=v_pool_scale,
                pool_proj=v_pool_proj, conv_w=v_conv_w, conv_b=v_conv_b, w_rg=v_w_rg, b_rg=v_b_rg, w_ig=v_w_ig,
                b_ig=v_b_ig, lru_lambda=v_lru_lambda, lru_proj=v_lru_proj, w_out=v_w_out, norm2_g=v_norm2_g,
                w_ffn_in=v_w_ffn_in, w_ffn_out=v_w_ffn_out, ple_norm_g=v_ple_norm_g, w_ple_gate=v_w_ple_gate,
                w_ple_proj=v_w_ple_proj, final_g=v_final_g)
    names = list(weights)
    big = ["w_in", "pool_proj", "lru_proj", "w_out", "w_ffn_in", "w_ffn_out", "w_ple_gate", "w_ple_proj"]
    by_rows = [n in ("lru_proj", "w_out", "w_ffn_out", "w_ple_gate") for n in big]
    small = [n for n in names if n not in big]

    shard_j = 2 * lax.axis_index("x") + lax.axis_index("y")
    T = x.shape[1]
    xs, ps, tgt = x[0], p[0, 0], loss_target[0]

    small_local = jnp.concatenate([b_gate[0], conv_w[0], jnp.zeros((2, 256), F32)], axis=0)
    core = lax.axis_index("c").astype(jnp.int32)
    place = jnp.stack([shard_j, core]).astype(jnp.int32)
    rows_of = dict(zip(big, by_rows))
    shard_shape = {n: weights[n].shape[1:] for n in big}
    blocks = {n: _cast_into_block(weights[n][0], rows_of[n], place[0], "cast_" + n) for n in big}
    early, late = big[:4], big[4:]
    gathered = _all_gather_weights([blocks[n] for n in early], [shard_shape[n] for n in early],
                                   [rows_of[n] for n in early], small_local)
    full = dict(zip(early, gathered[:-1]))
    late_send, late_recv, late_bufs, late_token = _gather_rest_start(
        [blocks[n] for n in late], [shard_shape[n] for n in late], [rows_of[n] for n in late], gathered[-1])
    b_gate_full = gathered[-1][0:2].reshape(1, 2 * D_MODEL)
    conv_w_full = gathered[-1][2:6]
    pool_w_1, w_rg_1, w_ig_1 = [w[0].astype(BF16) for w in (pool_w, w_rg, w_ig)]
    pool_w_b, w_rg_b, w_ig_b = [_pair_blocks(w) for w in (pool_w_1, w_rg_1, w_ig_1)]
    b_rg_row, b_ig_row = b_rg.reshape(1, D_MODEL), b_ig.reshape(1, D_MODEL)
    final_row = final_g.reshape(1, D_MODEL)

    zp, zl, zg, zt, u, h1, hs, yp, yl, xc_saved, r_saved, ig_saved = _f12_mixer(
        xs, norm1_g + late_token[0, 0], full["w_in"], b_gate_full, pool_w_1, pool_scale, full["pool_proj"],
        conv_w_full, conv_b, w_rg_1, b_rg_row, w_ig_1, b_ig_row, lru_lambda, full["lru_proj"], full["w_out"])
    full.update(zip(late, _gather_rest_wait(late_send, late_recv, late_bufs, [shard_shape[n] for n in late],
                                            [rows_of[n] for n in late], h1)))
    h2, v, ff, act = _f3_ffn(h1, norm2_g, full["w_ffn_in"], full["w_ffn_out"])

    loss_sum, dh2, g_ple_gate, g_ple_proj, vec4 = _b4_ple_loss(
        h2, ps, tgt, ple_norm_g, full["w_ple_gate"], full["w_ple_proj"], final_row)
    dff, dh1, vec3 = _b3_ffn(dh2, h1, ff, norm2_g, full["w_ffn_in"], full["w_ffn_out"])
    g_ffn_in = _wgrad(v, dff, 2 * D_FF // N_SHARDS, "wgrad_ffn_in")
    g_ffn_out = _wgrad(act, dh2, D_MODEL, "wgrad_ffn_out", tokens=WGRAD_TOKENS // 2)

    def stack(n, g):
        return g.reshape(N_SHARDS, g.shape[0] // N_SHARDS, g.shape[1]) if rows_of[n] else g[None]

    def chip_sums_of(group, grads_of, tag):
        stacked = [stack(n, grads_of[n]) for n in group]
        theirs = _core_exchange(stacked, "grad_core_exchange_" + tag)
        return [_sum_cores(g, t, core, "sum_cores_" + n) for g, t, n in zip(stacked, theirs, group)]

    late_rows = [rows_of[n] for n in late]
    late_grads = dict(w_ffn_in=g_ffn_in, w_ffn_out=g_ffn_out, w_ple_gate=g_ple_gate, w_ple_proj=g_ple_proj)
    cx_send, cx_recv, late_stacked, late_theirs, cx_token = _core_exchange_start(
        [stack(n, late_grads[n]) for n in late])
    dzt, dyp, dyl, g_w_out, vec_g = _b2_gates(dh1, zt, yp, yl, b_gate_full + cx_token[0, 0], full["w_out"])
    late_stacked, late_theirs = _core_exchange_wait(cx_send, cx_recv, late_stacked, late_theirs, dzt)
    late_sums = [_sum_cores(g, t, core, "sum_cores_" + n) for g, t, n in zip(late_stacked, late_theirs, late)]
    ex_send, ex_recv, late_sums, late_slots, ex_token = _chip_exchange_start(late_sums, late_rows, "late")
    dzl, dzg, g_lru_proj, g_w_rg, g_w_ig, vec_l = _b2_lru(
        dyl, zl, zg, hs, xc_saved, r_saved, ig_saved, conv_w_full, w_rg_b, w_ig_b, lru_lambda + ex_token[0, 0],
        full["lru_proj"])
    dzp, grad_x, g_pool_proj, g_pool_w, vec_p = _b12_pool_in_proj(
        dyp, zp, dzl, dzg, dzt, xs, dh1, norm1_g, full["w_in"], pool_w_b, pool_scale, full["pool_proj"])
    small_full = dict(
        norm1_g=vec_p[1], b_gate=vec_g[0:2], pool_w=_unpair_blocks(g_pool_w), pool_scale=vec_p[0, :POOL_WIDTH],
        conv_w=vec_l[_V_CONVW:_V_CONVW + CONV_WIDTH], conv_b=vec_l[_V_CONVB], w_rg=_unpair_blocks(g_w_rg),
        b_rg=vec_l[_V_BRG], w_ig=_unpair_blocks(g_w_ig), b_ig=vec_l[_V_BIG], lru_lambda=vec_l[_V_LAM], norm2_g=vec3[0], ple_norm_g=vec4[1],
        final_g=vec4[0])
    full_shapes = [small_full[n].shape for n in small]
    n_full = sum(int(small_full[n].size) for n in small)
    rows_full = -(-n_full // (128 * ROW_TILE)) * ROW_TILE
    sm_send, sm_recv, sm_pack, sm_slots, sm_token = _small_exchange_start(
        _pack_rows([small_full[n] for n in small], rows_full))
    g_w_in = jnp.concatenate([
        _wgrad(u, dzp, POOL_WIDTH, "wgrad_in_pool", 2 * WGRAD_TOKENS, after=sm_token),
        _wgrad(u, dzl, D_MODEL, "wgrad_in_lru", 2 * WGRAD_TOKENS, after=sm_token),
        _wgrad(u, dzg, D_MODEL, "wgrad_in_gelu", 2 * WGRAD_TOKENS, after=sm_token),
        _wgrad(u, dzt, D_MODEL, "wgrad_in_gate", 2 * WGRAD_TOKENS, after=sm_token)], axis=1)

    loss = lax.psum(loss_sum[0, 0] * (0.5 / D_MODEL), ("x", "y", "c"))

    early_rows = [rows_of[n] for n in early]
    early_sums = chip_sums_of(early, dict(w_in=g_w_in, pool_proj=g_pool_proj, lru_proj=g_lru_proj, w_out=g_w_out),
                              "early")
    e_send, e_recv, early_sums, early_slots, e_token = _chip_exchange_start(early_sums, early_rows, "early")
    grads, deltas, new_m, new_v = {}, {}, {}, {}

    def finish(group, sums, slots, tag):
        reduced = _core_share([_sum_chips(s, q, rows_of[n], place, "sum_chips_" + n)
                               for s, q, n in zip(sums, slots, group)], tag)
        for n, r in zip(group, reduced):
            g = r.reshape(r.shape[0] * r.shape[1], r.shape[2])
            d, nm, nv = _adamw(weights[n][0], g, m_in[n][0], v_in[n][0], "adamw_" + n)
            grads[n], deltas[n], new_m[n], new_v[n] = g[None], d[None], nm[None], nv[None]

    late_sums, late_slots = _chip_exchange_wait(ex_send, ex_recv, late_sums, late_slots, late_rows, e_token, "late")
    finish(late, late_sums, late_slots, "late")

    sm_pack, sm_slots = _small_exchange_wait(sm_send, sm_recv, sm_pack, sm_slots, e_token)
    device = (4 * lax.axis_index("x") + 2 * lax.axis_index("y") + lax.axis_index("c")).astype(jnp.int32)
    summed = dict(zip(small, _unpack_rows(_sum_small(sm_pack, sm_slots, device), full_shapes)))
    summed["b_gate"] = lax.dynamic_slice_in_dim(summed["b_gate"], shard_j * 256, 256, axis=1)
    summed["conv_w"] = lax.dynamic_slice_in_dim(summed["conv_w"], shard_j * 256, 256, axis=1)
    local_shapes = [weights[n].shape for n in small]
    n_local = sum(int(weights[n].size) for n in small)
    rows_local = -(-n_local // (128 * ROW_TILE)) * ROW_TILE
    packs = [_pack_rows([src[n] for n in small], rows_local) for src in (weights, summed, m_in, v_in)]
    d_s, nm_s, nv_s = _adamw(*packs, "adamw_small")
    for dst, pack in ((grads, packs[1]), (deltas, d_s), (new_m, nm_s), (new_v, nv_s)):
        dst.update(zip(small, _unpack_rows(pack, local_shapes)))

    done = d_s[:SUBLANES]
    for n in late:
        done = done + deltas[n][0, :SUBLANES, :128]
    early_sums, early_slots = _chip_exchange_wait(e_send, e_recv, early_sums, early_slots, early_rows, done, "early")
    finish(early, early_sums, early_slots, "early")

    return (loss, grad_x[None], *[grads[n] for n in names], *[deltas[n] for n in names],
            *[new_m[n] for n in names], *[new_v[n] for n in names])
```

```python
import functools

import jax
import jax.numpy as jnp
from jax import lax
from jax.experimental import pallas as pl
from jax.experimental.pallas import tpu as pltpu

F32 = jnp.float32
BF16 = jnp.bfloat16

D_MODEL = 1024
POOL_WIDTH = 512
POOL_GROUP_DIM = 128
POOL_WINDOWS = (2, 4, 8, 16)
POOL_HALO = 16
LRU_HEADS = 8
LRU_HEAD_DIM = 128
CONV_WIDTH = 4
LRU_C = 8.0
D_FF = 2816
PLE_DIM = 256
RMS_EPS = 1e-6
N_SHARDS = 4
N_DEV = 8

ADAM_LR = 0.001
ADAM_B1 = 0.9
ADAM_B2 = 0.999
ADAM_EPS = 1e-08
ADAM_WD = 0.01
ADAM_STEP = 10

ROW_TILE = 256
WIDE_TILE = 512
WGRAD_TOKENS = 2048
SUBLANES = 8
VMEM_LIMIT = 56 * 1024 * 1024
MESH = pl.DeviceIdType.MESH
ANY = pl.BlockSpec(memory_space=pl.ANY)


def _params(semantics=None):
    return pltpu.CompilerParams(dimension_semantics=semantics, vmem_limit_bytes=VMEM_LIMIT)


def _resident(shape):
    n = len(shape)
    return pl.BlockSpec(shape, lambda *_: (0,) * n, pipeline_mode=pl.Buffered(1))


def _acc(shape):
    n = len(shape)
    return pl.BlockSpec(shape, lambda *_: (0,) * n)


def _rows(tile, cols):
    return pl.BlockSpec((tile, cols), lambda i: (i, 0))


def _rows_rev(tile, cols, n_tiles):
    return pl.BlockSpec((tile, cols), lambda i: (n_tiles - 1 - i, 0))


def _halo_before_rev(rows, cols, tile, n_tiles):
    per = tile // rows
    return pl.BlockSpec((rows, cols), lambda i: (jnp.maximum((n_tiles - 1 - i) * per - 1, 0), 0))


def _nn(a, b):
    return jnp.dot(a, b, preferred_element_type=F32)


def _nt(a, b):
    return lax.dot_general(a, b, (((1,), (1,)), ((), ())), preferred_element_type=F32)


def _tn(a, b):
    return lax.dot_general(a, b, (((0,), (0,)), ((), ())), preferred_element_type=F32)


def _rms(x):
    r = lax.rsqrt(jnp.mean(x * x, axis=-1, keepdims=True) + RMS_EPS)
    return x * r, r


def _rms_bwd(dn, n, r):
    return r * (dn - n * jnp.mean(dn * n, axis=-1, keepdims=True))


def _sigmoid(x):
    return 0.5 * jnp.tanh(0.5 * x) + 0.5


_GELU_C = 0.7978845608028654
_GELU_A = 0.044715


def _gelu(x):
    t = jnp.tanh(_GELU_C * (x + _GELU_A * x * x * x))
    return 0.5 * x * (1.0 + t)


def _gelu_and_grad(x):
    x2 = x * x
    t = jnp.tanh(_GELU_C * (x + _GELU_A * x2 * x))
    cdf = 0.5 * (1.0 + t)
    grad = cdf + 0.5 * x * (1.0 - t * t) * _GELU_C * (1.0 + 3.0 * _GELU_A * x2)
    return x * cdf, grad


def _softplus_neg(lam):
    e = jnp.exp(-jnp.abs(lam))
    sp = jnp.maximum(-lam, 0.0) + jnp.log1p(e)
    return sp, -_sigmoid(-lam)


def _colsum(v):
    return jnp.sum(v, axis=0, keepdims=True)


def _row_ids(shape):
    return lax.broadcasted_iota(jnp.int32, shape, 0)


def _shift_down(cat, k):
    return pltpu.roll(cat, k, 0) if k else cat


def _shift_up(cat, k):
    return pltpu.roll(cat, cat.shape[0] - k, 0) if k else cat


IN_SPLITS = (0, POOL_WIDTH, POOL_WIDTH + D_MODEL, POOL_WIDTH + 2 * D_MODEL, POOL_WIDTH + 4 * D_MODEL)
IN_WIDTHS = tuple(IN_SPLITS[k + 1] - IN_SPLITS[k] for k in range(4))
PROJ_CHUNK = 256
PAIR_DIM = 2 * LRU_HEAD_DIM


def _pair_blocks(w):
    zero = jnp.zeros_like(w[0::2])
    return jnp.concatenate([jnp.concatenate([w[0::2], zero], axis=2), jnp.concatenate([zero, w[1::2]], axis=2)], axis=1)


def _unpair_blocks(w):
    n, d2, _ = w.shape
    d = d2 // 2
    return jnp.stack([w[:, :d, :d], w[:, d:, d:]], axis=1).reshape(2 * n, d, d)


def _no_tick():
    pass


class _Interleaved:
    def __init__(self, pieces):
        self._pieces = iter(pieces)

    def tick(self, n=1):
        for _ in range(n):
            piece = next(self._pieces, None)
            if piece is not None:
                piece()

    def flush(self):
        for piece in self._pieces:
            piece()


def _pool_forward(zp_cat, pw_ref, first_row, tick=_no_tick):
    tt = zp_cat.shape[0] - POOL_HALO
    t_glob = first_row + _row_ids((tt, POOL_GROUP_DIM))
    pooled, mixed = [], []
    for g, w in enumerate(POOL_WINDOWS):
        cat = zp_cat[:, g * POOL_GROUP_DIM:(g + 1) * POOL_GROUP_DIM]
        s, k = cat, 1
        while k < w:
            s = s + _shift_down(s, k)
            k *= 2
        cnt = jnp.minimum(t_glob + 1, w).astype(F32)
        pooled.append(s[POOL_HALO:] / cnt - cat[POOL_HALO:])
        per = pw_ref.shape[-1] // POOL_GROUP_DIM
        if (g + 1) % per == 0:
            block = jnp.concatenate(pooled[-per:], axis=1).astype(BF16)
            mixed.append(_nn(block, pw_ref[g // per]))
        tick()
    return jnp.concatenate(pooled, axis=1), jnp.concatenate(mixed, axis=1)


def _lru_gates(zl_cat, conv_w, conv_b, wrg_ref, brg, wig_ref, big, sp, first_row, tick=_no_tick):
    xc = conv_w[CONV_WIDTH - 1:CONV_WIDTH] * zl_cat
    for k in range(1, CONV_WIDTH):
        xc = xc + conv_w[CONV_WIDTH - 1 - k:CONV_WIDTH - k] * _shift_down(zl_cat, k)
        tick()
    xc = xc[SUBLANES:] + conv_b
    xh = xc.astype(BF16)
    pr, pi = [], []
    width = wrg_ref.shape[-1]
    for p in range(D_MODEL // width):
        xs = xh[:, p * width:(p + 1) * width]
        pr.append(_nn(xs, wrg_ref[p]))
        pi.append(_nn(xs, wig_ref[p]))
    r = _sigmoid(jnp.concatenate(pr, axis=1) + brg)
    tick()
    ig = _sigmoid(jnp.concatenate(pi, axis=1) + big)
    tick()
    a, mult = _decay(r, sp, first_row, tick)
    tick()
    return xc, r, ig, a, mult


def _decay(r, sp, first_row, tick=_no_tick):
    a = jnp.exp(-LRU_C * r * sp)
    tick()
    mult = jnp.sqrt(jnp.maximum(1.0 - a * a, 0.0))
    t_glob = first_row + _row_ids(r.shape)
    return a, jnp.where(t_glob == 0, 1.0, mult)


def _f12_mixer(x, norm1_g, w_in, b_gate, pool_w, pool_scale, pool_proj, conv_w, conv_b, w_rg, b_rg, w_ig, b_ig,
               lru_lambda, lru_proj, w_out):
    T = x.shape[0]
    tt = ROW_TILE
    nt = T // tt
    n_groups = tt // SUBLANES
    proj_mid = IN_SPLITS[3] + D_MODEL // 2

    def body(xm_ref, x_ref, g1_ref, win_ref, bg_ref, pw_ref, ps_ref, pp_ref, cw_ref, cb_ref,
             wrg_ref, brg_ref, wig_ref, big_ref, lam_ref, lp_ref, wo_ref,
             zp_ref, zl_ref, zg_ref, zt_ref, u_ref, h1_ref, hs_ref, yp_ref, yl_ref, xc_ref, r_ref, ig_ref,
             zbuf, zp_halo, zl_halo, a_s, b_s, carry_s):
        s = pl.program_id(0)

        @pl.when(s == 0)
        def _():
            zbuf[1] = jnp.zeros((tt, IN_SPLITS[4]), F32)
            zp_halo[...] = jnp.zeros_like(zp_halo)
            zl_halo[...] = jnp.zeros_like(zl_halo)
            carry_s[...] = jnp.zeros_like(carry_s)

        z_new, z_old = zbuf.at[s % 2], zbuf.at[(s + 1) % 2]
        first = s <= 1
        first_row = jnp.maximum(s - 1, 0) * tt

        n1, _ = _rms(xm_ref[...])
        u = (n1 * g1_ref[...]).astype(BF16)
        u_ref[...] = u

        z_refs = (zp_ref, zl_ref, zg_ref, zt_ref)

        def project(lo):
            k = max(i for i in range(4) if IN_SPLITS[i] <= lo)
            part = _nn(u, win_ref[:, lo:lo + PROJ_CHUNK])
            z_new[:, lo:lo + PROJ_CHUNK] = part
            z_refs[k][:, lo - IN_SPLITS[k]:lo - IN_SPLITS[k] + PROJ_CHUNK] = part.astype(z_refs[k].dtype)

        before_scan = _Interleaved(functools.partial(project, lo) for lo in range(0, proj_mid, PROJ_CHUNK))
        after_scan = _Interleaved(functools.partial(project, lo) for lo in range(proj_mid, IN_SPLITS[4], PROJ_CHUNK))

        zp_cat = jnp.concatenate([jnp.where(first, 0.0, zp_halo[...]), z_old[:, IN_SPLITS[0]:IN_SPLITS[1]]], axis=0)
        _, mixed = _pool_forward(zp_cat, pw_ref, first_row, before_scan.tick)
        y_pool = _nn((mixed * ps_ref[...]).astype(BF16), pp_ref[...])

        sp, _ = _softplus_neg(lam_ref[...])
        zl_cat = jnp.concatenate([jnp.where(first, 0.0, zl_halo[...]), z_old[:, IN_SPLITS[1]:IN_SPLITS[2]]], axis=0)
        xc, r, ig, a, mult = _lru_gates(zl_cat, cw_ref[...], cb_ref[...], wrg_ref, brg_ref[...], wig_ref,
                                        big_ref[...], sp, first_row, before_scan.tick)
        a_s[...] = a
        b_s[...] = mult * ig * xc
        xc_ref[...] = xc.astype(BF16)
        r_ref[...] = r.astype(BF16)
        ig_ref[...] = ig.astype(BF16)
        before_scan.flush()

        rows8 = _row_ids((SUBLANES, D_MODEL))

        def group(g, carry):
            at = pl.ds(pl.multiple_of(g * SUBLANES, SUBLANES), SUBLANES)
            A, B = a_s[at, :], b_s[at, :]
            for s in (1, 2, 4):
                m = rows8 >= s
                B = jnp.where(m, A * pltpu.roll(B, s, 0) + B, B)
                A = jnp.where(m, A * pltpu.roll(A, s, 0), A)
            h = A * carry + B
            hs_ref[at, :] = h
            return jnp.broadcast_to(h[SUBLANES - 1:SUBLANES, :], (SUBLANES, D_MODEL))

        carry_s[...] = lax.fori_loop(0, n_groups, group, jnp.where(first, 0.0, carry_s[...]))
        gelu = _gelu(z_old[:, IN_SPLITS[2]:IN_SPLITS[3]])
        after_scan.tick(2)
        y_lru = _nn((hs_ref[...] * gelu).astype(BF16), lp_ref[...])

        gates = _sigmoid(z_old[:, IN_SPLITS[3]:IN_SPLITS[4]] + bg_ref[...])
        after_scan.tick(2)
        merged = gates[:, :D_MODEL] * y_pool + gates[:, D_MODEL:] * y_lru
        after_scan.flush()
        h1_ref[...] = x_ref[...] + _nn(merged.astype(BF16), wo_ref[...])
        yp_ref[...] = y_pool.astype(BF16)
        yl_ref[...] = y_lru.astype(BF16)
        zp_halo[...] = z_old[tt - POOL_HALO:, IN_SPLITS[0]:IN_SPLITS[1]]
        zl_halo[...] = z_old[tt - SUBLANES:, IN_SPLITS[1]:IN_SPLITS[2]]

    def ahead(cols):
        return pl.BlockSpec((tt, cols), lambda s: (jnp.minimum(s, nt - 1), 0))

    def behind(cols):
        return pl.BlockSpec((tt, cols), lambda s: (jnp.maximum(s - 1, 0), 0))

    res = [norm1_g, w_in, b_gate, pool_w, pool_scale, pool_proj, conv_w, conv_b, w_rg, b_rg, w_ig, b_ig, lru_lambda,
           lru_proj, w_out]
    return pl.pallas_call(
        body, name="f12_mixer", grid=(nt + 1,),
        in_specs=[ahead(D_MODEL), behind(D_MODEL)] + [_resident(w.shape) for w in res],
        out_specs=[ahead(w) for w in IN_WIDTHS] + [ahead(D_MODEL)] + [behind(D_MODEL)] * 7,
        out_shape=[jax.ShapeDtypeStruct((T, w), dt) for w, dt in zip(IN_WIDTHS, (F32, F32, F32, BF16))]
        + [jax.ShapeDtypeStruct((T, D_MODEL), BF16), jax.ShapeDtypeStruct((T, D_MODEL), F32),
           jax.ShapeDtypeStruct((T, D_MODEL), F32)] + [jax.ShapeDtypeStruct((T, D_MODEL), BF16)] * 5,
        scratch_shapes=[pltpu.VMEM((2, tt, IN_SPLITS[4]), F32), pltpu.VMEM((POOL_HALO, POOL_WIDTH), F32),
                        pltpu.VMEM((SUBLANES, D_MODEL), F32), pltpu.VMEM((tt, D_MODEL), F32),
                        pltpu.VMEM((tt, D_MODEL), F32), pltpu.VMEM((SUBLANES, D_MODEL), F32)],
        compiler_params=_params(("arbitrary",)),
    )(x, x, *res)


def _f3_ffn(h1, norm2_g, w_ffn_in, w_ffn_out):
    T = h1.shape[0]
    tm = ROW_TILE

    def body(h_ref, g_ref, wi_ref, wo_ref, h2_ref, v_ref, ff_ref, act_ref):
        h = h_ref[...]
        n, _ = _rms(h)
        v = (n * g_ref[...]).astype(BF16)
        v_ref[...] = v
        g_ff = _nn(v, wi_ref[:, :D_FF])
        u_ff = _nn(v, wi_ref[:, D_FF:])
        ff_ref[:, :D_FF] = g_ff.astype(BF16)
        ff_ref[:, D_FF:] = u_ff.astype(BF16)
        act = (g_ff * _sigmoid(g_ff) * u_ff).astype(BF16)
        act_ref[...] = act
        h2_ref[...] = h + _nn(act, wo_ref[...])

    return pl.pallas_call(
        body, name="f3_ffn", grid=(T // tm,),
        in_specs=[_rows(tm, D_MODEL), _resident((1, D_MODEL)), _resident(w_ffn_in.shape), _resident(w_ffn_out.shape)],
        out_specs=[_rows(tm, D_MODEL), _rows(tm, D_MODEL), _rows(tm, 2 * D_FF), _rows(tm, D_FF)],
        out_shape=[jax.ShapeDtypeStruct((T, D_MODEL), F32), jax.ShapeDtypeStruct((T, D_MODEL), BF16),
                   jax.ShapeDtypeStruct((T, 2 * D_FF), BF16), jax.ShapeDtypeStruct((T, D_FF), BF16)],
        compiler_params=_params(("arbitrary",)),
    )(h1, norm2_g, w_ffn_in, w_ffn_out)


def _b4_ple_loss(h2, p, target, ple_norm_g, w_ple_gate, w_ple_proj, final_g):
    T = h2.shape[0]
    tm = WIDE_TILE

    def body(h_ref, p_ref, t_ref, gp_ref, wg_ref, wp_ref, gf_ref, loss_ref, dh2_ref, dwg_ref, dwp_ref, vec_ref):
        @pl.when(pl.program_id(0) == 0)
        def _():
            loss_ref[...] = jnp.zeros_like(loss_ref)
            dwg_ref[...] = jnp.zeros_like(dwg_ref)
            dwp_ref[...] = jnp.zeros_like(dwp_ref)
            vec_ref[...] = jnp.zeros_like(vec_ref)

        h2v = h_ref[...]
        n3, r3 = _rms(h2v)
        n3g = (n3 * gp_ref[...]).astype(BF16)
        pb = p_ref[...].astype(BF16)
        q = _nn(n3g, wg_ref[...])
        e = _nn(pb, wp_ref[...])
        pg = _sigmoid(q)
        h3 = h2v + pg * e
        n4, r4 = _rms(h3)
        diff = n4 * gf_ref[...] - t_ref[...]
        loss_ref[...] += jnp.sum(diff * diff).reshape(1, 1)
        dy = diff * (1.0 / D_MODEL)
        vec_ref[0:1, :] += _colsum(dy * n4)
        dh3 = _rms_bwd(dy * gf_ref[...], n4, r4)
        de = (dh3 * pg).astype(BF16)
        dq = (dh3 * e * pg * (1.0 - pg)).astype(BF16)
        dn3g = _nt(dq, wg_ref[...])
        dwg_ref[...] += _tn(n3g, dq)
        dwp_ref[...] += _tn(pb, de)
        vec_ref[1:2, :] += _colsum(dn3g * n3)
        dh2_ref[...] = dh3 + _rms_bwd(dn3g * gp_ref[...], n3, r3)

    return pl.pallas_call(
        body, name="b4_ple_loss", grid=(T // tm,),
        in_specs=[_rows(tm, D_MODEL), _rows(tm, PLE_DIM), _rows(tm, D_MODEL), _resident((1, D_MODEL)),
                  _resident(w_ple_gate.shape), _resident(w_ple_proj.shape), _resident((1, D_MODEL))],
        out_specs=[_acc((1, 1)), _rows(tm, D_MODEL), _acc(w_ple_gate.shape), _acc(w_ple_proj.shape),
                   _acc((SUBLANES, D_MODEL))],
        out_shape=[jax.ShapeDtypeStruct((1, 1), F32), jax.ShapeDtypeStruct((T, D_MODEL), F32),
                   jax.ShapeDtypeStruct(w_ple_gate.shape, F32), jax.ShapeDtypeStruct(w_ple_proj.shape, F32),
                   jax.ShapeDtypeStruct((SUBLANES, D_MODEL), F32)],
        compiler_params=_params(("arbitrary",)),
    )(h2, p, target, ple_norm_g, w_ple_gate, w_ple_proj, final_g)


def _b3_ffn(dh2, h1, ff, norm2_g, w_ffn_in, w_ffn_out):
    T = h1.shape[0]
    tm = ROW_TILE

    def body(d_ref, h_ref, ff_ref, g_ref, wi_ref, wo_ref, dff_ref, dh1_ref, vec_ref):
        @pl.when(pl.program_id(0) == 0)
        def _():
            vec_ref[...] = jnp.zeros_like(vec_ref)

        dh2v = d_ref[...]
        dact = _nt(dh2v.astype(BF16), wo_ref[...])
        g_ff = ff_ref[:, :D_FF].astype(F32)
        u_ff = ff_ref[:, D_FF:].astype(F32)
        s = _sigmoid(g_ff)
        dg = (dact * u_ff * (s * (1.0 + g_ff * (1.0 - s)))).astype(BF16)
        du = (dact * (g_ff * s)).astype(BF16)
        dff_ref[:, :D_FF] = dg
        dff_ref[:, D_FF:] = du
        dv = _nt(dg, wi_ref[:, :D_FF]) + _nt(du, wi_ref[:, D_FF:])
        n2, r2 = _rms(h_ref[...])
        vec_ref[0:1, :] += _colsum(dv * n2)
        dh1_ref[...] = dh2v + _rms_bwd(dv * g_ref[...], n2, r2)

    return pl.pallas_call(
        body, name="b3_ffn", grid=(T // tm,),
        in_specs=[_rows(tm, D_MODEL), _rows(tm, D_MODEL), _rows(tm, 2 * D_FF), _resident((1, D_MODEL)),
                  _resident(w_ffn_in.shape), _resident(w_ffn_out.shape)],
        out_specs=[_rows(tm, 2 * D_FF), _rows(tm, D_MODEL), _acc((SUBLANES, D_MODEL))],
        out_shape=[jax.ShapeDtypeStruct((T, 2 * D_FF), BF16), jax.ShapeDtypeStruct((T, D_MODEL), F32),
                   jax.ShapeDtypeStruct((SUBLANES, D_MODEL), F32)],
        compiler_params=_params(("arbitrary",)),
    )(dh2, h1, ff, norm2_g, w_ffn_in, w_ffn_out)


def _wgrad(a, b, col_tile, name, tokens=WGRAD_TOKENS, after=None):
    T, K = a.shape
    N = b.shape[1]
    tk = min(T, tokens)

    def body(a_ref, b_ref, *rest):
        o_ref = rest[-1]

        @pl.when(pl.program_id(1) == 0)
        def _():
            o_ref[...] = jnp.zeros_like(o_ref)

        o_ref[...] += _tn(a_ref[...].astype(BF16), b_ref[...].astype(BF16))

    return pl.pallas_call(
        body, name=name, grid=(N // col_tile, T // tk),
        in_specs=[pl.BlockSpec((tk, K), lambda j, k: (k, 0)), pl.BlockSpec((tk, col_tile), lambda j, k: (k, j))]
        + ([] if after is None else [ANY]),
        out_specs=pl.BlockSpec((K, col_tile), lambda j, k: (0, j)),
        out_shape=jax.ShapeDtypeStruct((K, N), F32),
        compiler_params=_params(("arbitrary", "arbitrary")),
    )(a, b, *([] if after is None else [after]))


def _b2_gates(dh1, zt, yp, yl, b_gate, w_out):
    T = dh1.shape[0]
    tm = WIDE_TILE

    def body(d_ref, zt_ref, yp_ref, yl_ref, bg_ref, wo_ref, dzt_ref, dyp_ref, dyl_ref, dwo_ref, vec_ref):
        @pl.when(pl.program_id(0) == 0)
        def _():
            dwo_ref[...] = jnp.zeros_like(dwo_ref)
            vec_ref[...] = jnp.zeros_like(vec_ref)

        db = d_ref[...].astype(BF16)
        dm = _nt(db, wo_ref[...])
        gates = _sigmoid(zt_ref[...].astype(F32) + bg_ref[...])
        g0, g1 = gates[:, :D_MODEL], gates[:, D_MODEL:]
        y_pool, y_lru = yp_ref[...].astype(F32), yl_ref[...].astype(F32)
        dwo_ref[...] += _tn((g0 * y_pool + g1 * y_lru).astype(BF16), db)
        dz0 = dm * y_pool * g0 * (1.0 - g0)
        dz1 = dm * y_lru * g1 * (1.0 - g1)
        vec_ref[0:1, :] += _colsum(dz0)
        vec_ref[1:2, :] += _colsum(dz1)
        dzt_ref[:, :D_MODEL] = dz0.astype(BF16)
        dzt_ref[:, D_MODEL:] = dz1.astype(BF16)
        dyp_ref[...] = (dm * g0).astype(BF16)
        dyl_ref[...] = (dm * g1).astype(BF16)

    return pl.pallas_call(
        body, name="b2_gates", grid=(T // tm,),
        in_specs=[_rows(tm, D_MODEL), _rows(tm, 2 * D_MODEL), _rows(tm, D_MODEL), _rows(tm, D_MODEL),
                  _resident(b_gate.shape), _resident(w_out.shape)],
        out_specs=[_rows(tm, 2 * D_MODEL), _rows(tm, D_MODEL), _rows(tm, D_MODEL), _acc(w_out.shape),
                   _acc((SUBLANES, D_MODEL))],
        out_shape=[jax.ShapeDtypeStruct((T, 2 * D_MODEL), BF16), jax.ShapeDtypeStruct((T, D_MODEL), BF16),
                   jax.ShapeDtypeStruct((T, D_MODEL), BF16), jax.ShapeDtypeStruct(w_out.shape, F32),
                   jax.ShapeDtypeStruct((SUBLANES, D_MODEL), F32)],
        compiler_params=_params(("arbitrary",)),
    )(dh1, zt, yp, yl, b_gate, w_out)


def _b12_pool_in_proj(dyp, zp, dzl, dzg, dzt, x, dh1, norm1_g, w_in, pool_w, pool_scale, pool_proj):
    T = zp.shape[0]
    tt = ROW_TILE
    nt = T // tt

    def body(dy_ref, zp_ref, zph_ref, dzl_ref, dzg_ref, dzt_ref, x_ref, dh_ref, g1_ref, win_ref, pw_ref, ps_ref, pp_ref,
             dzp_ref, dx_ref, dpp_ref, dpw_ref, vec_ref, q_next):
        i = pl.program_id(0)
        ti = nt - 1 - i
        first_row = ti * tt

        @pl.when(i == 0)
        def _():
            dpp_ref[...] = jnp.zeros_like(dpp_ref)
            dpw_ref[...] = jnp.zeros_like(dpw_ref)
            vec_ref[...] = jnp.zeros_like(vec_ref)
            q_next[...] = jnp.zeros_like(q_next)

        du_parts = []

        def project(lo):
            k = max(i for i in range(4) if IN_SPLITS[i] <= lo)
            dz_ref = (None, dzl_ref, dzg_ref, dzt_ref)[k]
            at = lo - IN_SPLITS[k]
            part = _nt(dz_ref[:, at:at + PROJ_CHUNK], win_ref[:, lo:lo + PROJ_CHUNK])
            du_parts[:] = [part if not du_parts else du_parts[0] + part]

        mxu = _Interleaved(functools.partial(project, lo) for lo in range(IN_SPLITS[1], IN_SPLITS[4], PROJ_CHUNK))

        keep = (ti > 0).astype(F32)
        zp_cat = jnp.concatenate([zph_ref[...] * keep, zp_ref[...]], axis=0)
        pooled, mixed = _pool_forward(zp_cat, pw_ref, first_row, mxu.tick)
        dy = dy_ref[...]
        dpp_ref[...] += _tn((mixed * ps_ref[...]).astype(BF16), dy)
        mxu.tick(2)
        dms = _nt(dy, pp_ref[...])
        mxu.tick(2)
        vec_ref[0:1, :POOL_WIDTH] += _colsum(dms * mixed)
        dmixed = (dms * ps_ref[...]).astype(BF16)
        t_glob = first_row + _row_ids((tt, POOL_GROUP_DIM))
        dz, q_all, dpooled_pairs = [], [], []
        for p in range(len(POOL_WINDOWS) // 2):
            pair = slice(p * PAIR_DIM, (p + 1) * PAIR_DIM)
            dpw_ref[p] += _tn(pooled[:, pair].astype(BF16), dmixed[:, pair])
            dpooled_pairs.append(_nt(dmixed[:, pair], pw_ref[p]))
        dpooled_all = jnp.concatenate(dpooled_pairs, axis=1)
        for g, w in enumerate(POOL_WINDOWS):
            cols = slice(g * POOL_GROUP_DIM, (g + 1) * POOL_GROUP_DIM)
            dpooled = dpooled_all[:, cols]
            q = dpooled / jnp.minimum(t_glob + 1, w).astype(F32)
            q_all.append(q)
            s, k = jnp.concatenate([q, q_next[:, cols]], axis=0), 1
            while k < w:
                s = s + _shift_up(s, k)
                k *= 2
            dz.append(s[:tt] - dpooled)
            mxu.tick(2)
        dzp = jnp.concatenate(dz, axis=1).astype(BF16)
        dzp_ref[...] = dzp
        q_next[...] = jnp.concatenate([q[:POOL_HALO] for q in q_all], axis=1)
        mxu.flush()

        du = du_parts[0] + _nt(dzp, win_ref[:, IN_SPLITS[0]:IN_SPLITS[1]])
        n1, r1 = _rms(x_ref[...])
        vec_ref[1:2, :] += _colsum(du * n1)
        dx_ref[...] = dh_ref[...] + _rms_bwd(du * g1_ref[...], n1, r1)

    rev = functools.partial(_rows_rev, n_tiles=nt)
    res = [norm1_g, w_in, pool_w, pool_scale, pool_proj]
    return pl.pallas_call(
        body, name="b12_pool_in_proj", grid=(nt,),
        in_specs=[rev(tt, D_MODEL), rev(tt, POOL_WIDTH), _halo_before_rev(POOL_HALO, POOL_WIDTH, tt, nt),
                  rev(tt, D_MODEL), rev(tt, D_MODEL), rev(tt, 2 * D_MODEL), rev(tt, D_MODEL), rev(tt, D_MODEL)]
        + [_resident(w.shape) for w in res],
        out_specs=[rev(tt, POOL_WIDTH), rev(tt, D_MODEL), _acc(pool_proj.shape), _acc(pool_w.shape),
                   _acc((SUBLANES, D_MODEL))],
        out_shape=[jax.ShapeDtypeStruct((T, POOL_WIDTH), BF16), jax.ShapeDtypeStruct((T, D_MODEL), F32),
                   jax.ShapeDtypeStruct(pool_proj.shape, F32), jax.ShapeDtypeStruct(pool_w.shape, F32),
                   jax.ShapeDtypeStruct((SUBLANES, D_MODEL), F32)],
        scratch_shapes=[pltpu.VMEM((POOL_HALO, POOL_WIDTH), F32)],
        compiler_params=_params(("arbitrary",)),
    )(dyp, zp, zp, dzl, dzg, dzt, x, dh1, *res)


_V_CONVW, _V_CONVB, _V_BRG, _V_BIG, _V_LAM = 0, 4, 5, 6, 7


def _b2_lru(dyl, zl, zg, hs, xc_saved, r_saved, ig_saved, conv_w, w_rg, w_ig, lru_lambda, lru_proj):
    T = zl.shape[0]
    tt = ROW_TILE
    nt = T // tt
    n_groups = tt // SUBLANES

    def body(dy_ref, zl_ref, zlh_ref, zg_ref, hs_ref, hsh_ref, xc_ref, r_ref, ig_ref, cw_ref, wrg_ref, wig_ref,
             lam_ref, lp_ref, dzl_ref, dzg_ref, dlp_ref, dwrg_ref, dwig_ref, vec_ref,
             c_s, d_s, g_s, g_next, a_next, dxc_next):
        i = pl.program_id(0)
        ti = nt - 1 - i
        first_row = ti * tt

        @pl.when(i == 0)
        def _():
            dlp_ref[...] = jnp.zeros_like(dlp_ref)
            dwrg_ref[...] = jnp.zeros_like(dwrg_ref)
            dwig_ref[...] = jnp.zeros_like(dwig_ref)
            vec_ref[...] = jnp.zeros_like(vec_ref)
            g_next[...] = jnp.zeros_like(g_next)
            a_next[...] = jnp.zeros_like(a_next)
            dxc_next[...] = jnp.zeros_like(dxc_next)

        keep = (ti > 0).astype(F32)
        sp, dsp_dlam = _softplus_neg(lam_ref[...])
        hs = hs_ref[...]
        gelu, dgelu = _gelu_and_grad(zg_ref[...])
        dy = dy_ref[...]
        dlp_ref[...] += _tn((hs * gelu).astype(BF16), dy)
        dyl = _nt(dy, lp_ref[...])
        dzg_ref[...] = (dyl * hs * dgelu).astype(BF16)

        d_s[...] = dyl * gelu
        a_tile = jnp.exp(-LRU_C * r_ref[...].astype(F32) * sp)
        c_s[...] = _shift_up(jnp.concatenate([a_tile, a_next[...]], axis=0), 1)[:tt]
        a_next[...] = jnp.broadcast_to(a_tile[0:1, :], (SUBLANES, D_MODEL))
        rows8 = _row_ids((SUBLANES, D_MODEL))

        def group(k, carry):
            at = pl.ds(pl.multiple_of((n_groups - 1 - k) * SUBLANES, SUBLANES), SUBLANES)
            C, Dv = c_s[at, :], d_s[at, :]
            for s in (1, 2, 4):
                m = rows8 < SUBLANES - s
                Dv = jnp.where(m, C * pltpu.roll(Dv, SUBLANES - s, 0) + Dv, Dv)
                C = jnp.where(m, C * pltpu.roll(C, SUBLANES - s, 0), C)
            G = C * carry + Dv
            g_s[at, :] = G
            return jnp.broadcast_to(G[0:1, :], (SUBLANES, D_MODEL))

        g_next[...] = lax.fori_loop(0, n_groups, group, g_next[...])
        G = g_s[...]

        cw = cw_ref[...]
        zl_cat = jnp.concatenate([zlh_ref[...] * keep, zl_ref[...]], axis=0)
        xc, r, ig = xc_ref[...].astype(F32), r_ref[...].astype(F32), ig_ref[...].astype(F32)
        a, mult = _decay(r, sp, first_row)
        h_prev = _shift_down(jnp.concatenate([hsh_ref[...] * keep, hs_ref[...]], axis=0), 1)[SUBLANES:]
        t_glob = first_row + _row_ids((tt, D_MODEL))
        dmult = jnp.where(t_glob == 0, 0.0, G * ig * xc)
        dla = G * h_prev * a - dmult * (a * a) / mult
        vec_ref[_V_LAM:_V_LAM + 1, :] += _colsum(dla * r) * (-LRU_C) * dsp_dlam
        dpr = dla * (-LRU_C) * sp * r * (1.0 - r)
        dpi = G * mult * xc * ig * (1.0 - ig)
        vec_ref[_V_BRG:_V_BRG + 1, :] += _colsum(dpr)
        vec_ref[_V_BIG:_V_BIG + 1, :] += _colsum(dpi)
        dprb, dpib, xh = dpr.astype(BF16), dpi.astype(BF16), xc_ref[...]
        dxc_h = []
        for p in range(LRU_HEADS // 2):
            cols = slice(p * PAIR_DIM, (p + 1) * PAIR_DIM)
            dwrg_ref[p] += _tn(xh[:, cols], dprb[:, cols])
            dwig_ref[p] += _tn(xh[:, cols], dpib[:, cols])
            dxc_h.append(_nt(dprb[:, cols], wrg_ref[p]) + _nt(dpib[:, cols], wig_ref[p]))
        dxc = G * mult * ig + jnp.concatenate(dxc_h, axis=1)

        vec_ref[_V_CONVB:_V_CONVB + 1, :] += _colsum(dxc)
        dxc_cat = jnp.concatenate([dxc, dxc_next[...]], axis=0)
        dzl = cw[CONV_WIDTH - 1:CONV_WIDTH] * dxc
        for k in range(CONV_WIDTH):
            lag = CONV_WIDTH - 1 - k
            vec_ref[_V_CONVW + k:_V_CONVW + k + 1, :] += _colsum(dxc * _shift_down(zl_cat, lag)[SUBLANES:])
            if lag:
                dzl = dzl + cw[k:k + 1] * _shift_up(dxc_cat, lag)[:tt]
        dzl_ref[...] = dzl.astype(BF16)
        dxc_next[...] = dxc[:SUBLANES]

    res = [conv_w, w_rg, w_ig, lru_lambda, lru_proj]
    return pl.pallas_call(
        body, name="b2_lru", grid=(nt,),
        in_specs=[_rows_rev(tt, D_MODEL, nt), _rows_rev(tt, D_MODEL, nt), _halo_before_rev(SUBLANES, D_MODEL, tt, nt),
                  _rows_rev(tt, D_MODEL, nt), _rows_rev(tt, D_MODEL, nt), _halo_before_rev(SUBLANES, D_MODEL, tt, nt)]
        + [_rows_rev(tt, D_MODEL, nt)] * 3 + [_resident(w.shape) for w in res],
        out_specs=[_rows_rev(tt, D_MODEL, nt), _rows_rev(tt, D_MODEL, nt), _acc(lru_proj.shape), _acc(w_rg.shape),
                   _acc(w_ig.shape), _acc((SUBLANES, D_MODEL))],
        out_shape=[jax.ShapeDtypeStruct((T, D_MODEL), BF16), jax.ShapeDtypeStruct((T, D_MODEL), BF16),
                   jax.ShapeDtypeStruct(lru_proj.shape, F32), jax.ShapeDtypeStruct(w_rg.shape, F32),
                   jax.ShapeDtypeStruct(w_ig.shape, F32), jax.ShapeDtypeStruct((SUBLANES, D_MODEL), F32)],
        scratch_shapes=[pltpu.VMEM((tt, D_MODEL), F32)] * 3 + [pltpu.VMEM((SUBLANES, D_MODEL), F32)] * 3,
        compiler_params=_params(("arbitrary",)),
    )(dyl, zl, zl, zg, hs, hs, xc_saved, r_saved, ig_saved, *res)


WHOLE_BLOCK_BYTES = 3 * 512 * 1024
MAX_BLOCK_BYTES = 4 * 1024 * 1024


def _row_tile(rows, cols):
    for t in (rows, rows // 2):
        if t % 16 == 0 and rows % t == 0 and t * cols * 4 <= WHOLE_BLOCK_BYTES:
            return t
    for t in (512, 256, 128, 64, 32, 16):
        if rows % t == 0 and t * cols * 4 <= MAX_BLOCK_BYTES:
            return t
    return 8 if rows % 8 == 0 else rows


def _scalar_grid(grid, in_specs, out_specs):
    return pltpu.PrefetchScalarGridSpec(num_scalar_prefetch=1, grid=grid, in_specs=in_specs, out_specs=out_specs)


def _cast_into_block(w, by_rows, shard_j, name):
    R, C = w.shape
    tr = _row_tile(R, C)
    if by_rows:
        out_shape, out_map = (N_SHARDS * R, C), lambda i, j: (j[0] * (R // tr) + i, 0)
    else:
        out_shape, out_map = (R, N_SHARDS * C), lambda i, j: (i, j[0])

    def body(j_ref, w_ref, o_ref):
        o_ref[...] = w_ref[...].astype(BF16)

    return pl.pallas_call(
        body, name=name,
        grid_spec=_scalar_grid((R // tr,), [pl.BlockSpec((tr, C), lambda i, j: (i, 0))], pl.BlockSpec((tr, C), out_map)),
        out_shape=jax.ShapeDtypeStruct(out_shape, BF16),
        compiler_params=_params(("arbitrary",)),
    )(shard_j.reshape(1), w)


def _sum_cores(g, theirs, core, name):
    S, R, C = g.shape
    H = R // 2
    tr = _row_tile(H, C)
    nh = H // tr

    def body(c_ref, g_ref, t_ref, o_ref):
        o_ref[...] = (g_ref[...] + t_ref[...]).astype(BF16)

    half = pl.BlockSpec((None, tr, C), lambda s, i, c: (s, i, 0))
    return pl.pallas_call(
        body, name=name,
        grid_spec=_scalar_grid((S, nh), [pl.BlockSpec((None, tr, C), lambda s, i, c: (s, c[0] * nh + i, 0)), half], half),
        out_shape=jax.ShapeDtypeStruct((S, H, C), BF16),
        compiler_params=_params(("arbitrary", "arbitrary")),
    )(core.reshape(1), g, theirs)


def _sum_chips(sums, slots, by_rows, place, name):
    _, H, C = slots.shape
    tr = _row_tile(H, C)
    own_map = (lambda i, p: (p[0], i, 0)) if by_rows else (lambda i, p: (0, i, p[0]))

    def body(p_ref, s_ref, q_ref, o_ref):
        o_ref[...] = ((s_ref[...].astype(F32) + q_ref[0].astype(F32)) + q_ref[1].astype(F32)) + q_ref[2].astype(F32)

    return pl.pallas_call(
        body, name=name,
        grid_spec=_scalar_grid(
            (H // tr,),
            [pl.BlockSpec((None, tr, C), own_map), pl.BlockSpec((3, tr, C), lambda i, p: (0, i, 0))],
            pl.BlockSpec((None, tr, C), lambda i, p: (p[1], i, 0))),
        out_shape=jax.ShapeDtypeStruct((2, H, C), F32),
        compiler_params=_params(("arbitrary",)),
    )(place, sums, slots)


def _adamw(w, g, m, v, name):
    R, C = w.shape
    tr = _row_tile(R, C)
    c1 = 1.0 - ADAM_B1 ** ADAM_STEP
    c2 = 1.0 - ADAM_B2 ** ADAM_STEP

    def body(w_ref, g_ref, m_ref, v_ref, d_ref, nm_ref, nv_ref):
        gv = g_ref[...]
        nm = ADAM_B1 * m_ref[...] + (1.0 - ADAM_B1) * gv
        nv = ADAM_B2 * v_ref[...] + (1.0 - ADAM_B2) * (gv * gv)
        d_ref[...] = -ADAM_LR * ((nm / c1) / (jnp.sqrt(nv / c2) + ADAM_EPS) + ADAM_WD * w_ref[...])
        nm_ref[...] = nm
        nv_ref[...] = nv

    return pl.pallas_call(
        body, name=name, grid=(R // tr,),
        in_specs=[_rows(tr, C)] * 4, out_specs=[_rows(tr, C)] * 3,
        out_shape=[jax.ShapeDtypeStruct((R, C), F32)] * 3,
        compiler_params=_params(("arbitrary",)),
    )(w, g, m, v)


def _place():
    return lax.axis_index("x"), lax.axis_index("y"), lax.axis_index("c")


def _other_chips(x, y):
    return [(1 - x, y), (x, 1 - y), (1 - x, 1 - y)]


def _shard_block(ref, by_rows, R, C, j, half_rows=None):
    if half_rows is None:
        rows, r0 = R, 0
    else:
        rows = R // 2
        r0 = pl.multiple_of(half_rows * rows, 16)
    if by_rows:
        return ref.at[pl.ds(pl.multiple_of(j * R, 16) + r0, rows), :]
    return ref.at[pl.ds(r0, rows), pl.ds(pl.multiple_of(j * C, 128), C)]


def _all_gather_weights(gathered, shapes, by_rows, small):
    n = len(gathered)

    def body(*refs):
        small_in = refs[n]
        outs, small_out = refs[n + 1:2 * n + 1], refs[2 * n + 1]
        send_sems, recv_sems, local_sem = refs[2 * n + 2:]
        x, y, c = _place()
        me_j = 2 * x + y
        chips = _other_chips(x, y)
        sibling = (x, y, 1 - c)

        def block(i, j, half):
            R, C = shapes[i]
            return _shard_block(outs[i], by_rows[i], R, C, j, half)

        def ici(i, k, src_j):
            return pltpu.make_async_remote_copy(
                src_ref=block(i, src_j, c), dst_ref=block(i, src_j, c),
                send_sem=send_sems.at[6 * i + k], recv_sem=recv_sems.at[6 * i + k],
                device_id=(*chips[k], c), device_id_type=MESH)

        def relay(i, k, half):
            kj = 2 * chips[k][0] + chips[k][1]
            return pltpu.make_async_remote_copy(
                src_ref=block(i, kj, half), dst_ref=block(i, kj, half),
                send_sem=send_sems.at[6 * i + 3 + k], recv_sem=recv_sems.at[6 * i + 3 + k],
                device_id=sibling, device_id_type=MESH)

        def small_copy(k, src_j):
            cols = pl.ds(pl.multiple_of(src_j * 256, 128), 256)
            return pltpu.make_async_remote_copy(
                src_ref=small_in, dst_ref=small_out.at[:, cols],
                send_sem=send_sems.at[6 * n + k], recv_sem=recv_sems.at[6 * n + k],
                device_id=(*chips[k], c), device_id_type=MESH)

        sends = []
        for i in range(n):
            for k in range(3):
                cp = ici(i, k, me_j)
                cp.start()
                sends.append(cp)
        for k in range(3):
            cp = small_copy(k, me_j)
            cp.start()
            sends.append(cp)
        local = pltpu.make_async_copy(small_in, small_out.at[:, pl.ds(pl.multiple_of(me_j * 256, 128), 256)], local_sem)
        local.start()
        for i in range(n):
            for k in range(3):
                kj = 2 * chips[k][0] + chips[k][1]
                ici(i, k, kj).wait_recv()
                cp = relay(i, k, c)
                cp.start()
                sends.append(cp)
        for k in range(3):
            small_copy(k, 2 * chips[k][0] + chips[k][1]).wait_recv()
        for i in range(n):
            for k in range(3):
                relay(i, k, 1 - c).wait_recv()
        for cp in sends:
            cp.wait_send()
        local.wait()

    out_shape = [jax.ShapeDtypeStruct(g.shape, BF16) for g in gathered]
    out_shape.append(jax.ShapeDtypeStruct((8, N_SHARDS * 256), F32))
    n_sems = 6 * n + 3
    return pl.pallas_call(
        body, name="all_gather_weights",
        in_specs=[ANY] * (n + 1), out_specs=[ANY] * (n + 1), out_shape=out_shape,
        input_output_aliases={i: i for i in range(n)},
        scratch_shapes=[pltpu.SemaphoreType.DMA((n_sems,)), pltpu.SemaphoreType.DMA((n_sems,)),
                        pltpu.SemaphoreType.DMA],
    )(*gathered, small)


def _core_exchange(grads, name):
    n = len(grads)

    def body(*refs):
        copies = _core_exchange_copies(refs[:n], refs[n:2 * n], refs[2 * n], refs[2 * n + 1])
        for cp in copies:
            cp.start()
        for cp in copies:
            cp.wait()

    return pl.pallas_call(
        body, name=name,
        in_specs=[ANY] * n, out_specs=[ANY] * n,
        out_shape=[jax.ShapeDtypeStruct((g.shape[0], g.shape[1] // 2, g.shape[2]), F32) for g in grads],
        scratch_shapes=[pltpu.SemaphoreType.DMA((n,))] * 2,
    )(*grads)


HBM = pl.BlockSpec(memory_space=pltpu.HBM)
SEM = pl.BlockSpec(memory_space=pltpu.SEMAPHORE)
TOKEN = jax.ShapeDtypeStruct((SUBLANES, 128), F32)


def _in_hbm(a):
    return pltpu.with_memory_space_constraint(a, pltpu.HBM)


def _split_params():
    return pltpu.CompilerParams(has_side_effects=pltpu.SideEffectType.DATAFLOW_SIDE_EFFECTING)


def _gather_rest_copies(refs, shapes, by_rows, send_sems, recv_sems):
    x, y, c = _place()
    me_j = 2 * x + y
    chips = _other_chips(x, y)
    pairs = []
    for i, ref in enumerate(refs):
        R, C = shapes[i]
        for k in range(3):
            kj = 2 * chips[k][0] + chips[k][1]

            def copy(j, ref=ref, i=i, k=k, R=R, C=C):
                blk = _shard_block(ref, by_rows[i], R, C, j)
                return pltpu.make_async_remote_copy(
                    src_ref=blk, dst_ref=blk, send_sem=send_sems.at[3 * i + k], recv_sem=recv_sems.at[3 * i + k],
                    device_id=(*chips[k], c), device_id_type=MESH)

            pairs.append((copy(me_j), copy(kj)))
    return pairs


def _gather_rest_start(gathered, shapes, by_rows, after):
    n = len(gathered)

    def body(*refs):
        ins = refs[:n]
        send_sems, recv_sems = refs[n + 1], refs[n + 2]
        token = refs[-1]
        for mine, _ in _gather_rest_copies(ins, shapes, by_rows, send_sems, recv_sems):
            mine.start()
        token[...] = jnp.zeros_like(token)

    out = pl.pallas_call(
        body, name="gather_rest_start",
        out_shape=(pltpu.SemaphoreType.DMA((3 * n,)), pltpu.SemaphoreType.DMA((3 * n,)),
                   *[pltpu.HBM(g.shape, g.dtype) for g in gathered], TOKEN),
        in_specs=[HBM] * n + [ANY], out_specs=(SEM, SEM, *[HBM] * n, pl.BlockSpec(memory_space=pltpu.VMEM)),
        input_output_aliases={i: 2 + i for i in range(n)},
        compiler_params=_split_params(),
    )(*[_in_hbm(g) for g in gathered], after)
    return out[0], out[1], out[2:2 + n], out[-1]


def _gather_rest_wait(send_sems, recv_sems, gathered, shapes, by_rows, after):
    n = len(gathered)

    def body(*refs):
        ins = refs[:n]
        send, recv = refs[n], refs[n + 1]
        for mine, theirs in _gather_rest_copies(ins, shapes, by_rows, send, recv):
            mine.wait_send()
            theirs.wait_recv()

    return pl.pallas_call(
        body, name="gather_rest_wait",
        out_shape=tuple(pltpu.HBM(g.shape, g.dtype) for g in gathered),
        in_specs=[HBM] * n + [SEM, SEM, ANY], out_specs=tuple([HBM] * n),
        input_output_aliases={i: i for i in range(n)},
        compiler_params=_split_params(),
    )(*gathered, send_sems, recv_sems, after)


def _chip_exchange_copies(ins, slots, dims, by_rows, send_sems, recv_sems):
    x, y, c = _place()
    chips = _other_chips(x, y)
    pairs = []
    for i in range(len(ins)):
        for k in range(3):
            kj = 2 * chips[k][0] + chips[k][1]
            if by_rows[i]:
                src = ins[i].at[kj]
            else:
                src = ins[i].at[0, :, pl.ds(pl.multiple_of(kj * dims[i][1], 128), dims[i][1])]
            cp = pltpu.make_async_remote_copy(
                src_ref=src, dst_ref=slots[i].at[k], send_sem=send_sems.at[3 * i + k], recv_sem=recv_sems.at[3 * i + k],
                device_id=(*chips[k], c), device_id_type=MESH)
            pairs.append((cp, cp))
    return pairs


def _exchange_dims(sums, by_rows):
    return [(s.shape[1], s.shape[2]) if by_rows[i] else (s.shape[1], s.shape[2] // N_SHARDS) for i, s in enumerate(sums)]


def _chip_exchange_start(sums, by_rows, tag):
    n = len(sums)
    sums = list(sums)
    dims = _exchange_dims(sums, by_rows)
    slots = [lax.empty((3, h, cc), BF16) for h, cc in dims]

    def body(*refs):
        ins, land = refs[:n], refs[n:2 * n]
        send_sems, recv_sems = refs[2 * n], refs[2 * n + 1]
        token = refs[-1]
        for cp, _ in _chip_exchange_copies(ins, land, dims, by_rows, send_sems, recv_sems):
            cp.start()
        token[...] = jnp.zeros_like(token)

    out = pl.pallas_call(
        body, name="grad_chip_exchange_start_" + tag,
        out_shape=(pltpu.SemaphoreType.DMA((3 * n,)), pltpu.SemaphoreType.DMA((3 * n,)),
                   *[pltpu.HBM(a.shape, a.dtype) for a in sums + slots], TOKEN),
        in_specs=[HBM] * (2 * n), out_specs=(SEM, SEM, *[HBM] * (2 * n), pl.BlockSpec(memory_space=pltpu.VMEM)),
        input_output_aliases={i: 2 + i for i in range(2 * n)},
        compiler_params=_split_params(),
    )(*[_in_hbm(a) for a in sums + slots])
    return out[0], out[1], out[2:2 + n], out[2 + n:2 + 2 * n], out[-1]


def _chip_exchange_wait(send_sems, recv_sems, sums, slots, by_rows, after, tag):
    n = len(sums)
    sums, slots = list(sums), list(slots)
    dims = _exchange_dims(sums, by_rows)

    def body(*refs):
        ins, land = refs[:n], refs[n:2 * n]
        send, recv = refs[2 * n], refs[2 * n + 1]
        for cp, _ in _chip_exchange_copies(ins, land, dims, by_rows, send, recv):
            cp.wait_send()
            cp.wait_recv()

    out = pl.pallas_call(
        body, name="grad_chip_exchange_wait_" + tag,
        out_shape=tuple(pltpu.HBM(a.shape, a.dtype) for a in sums + slots),
        in_specs=[HBM] * (2 * n) + [SEM, SEM, ANY], out_specs=tuple([HBM] * (2 * n)),
        input_output_aliases={i: i for i in range(2 * n)},
        compiler_params=_split_params(),
    )(*sums, *slots, send_sems, recv_sems, after)
    return out[:n], out[n:]


def _core_exchange_copies(ins, theirs, send_sems, recv_sems):
    x, y, c = _place()
    copies = []
    for i in range(len(ins)):
        H = ins[i].shape[1] // 2
        copies.append(pltpu.make_async_remote_copy(
            src_ref=ins[i].at[:, pl.ds(pl.multiple_of((1 - c) * H, 8), H), :], dst_ref=theirs[i],
            send_sem=send_sems.at[i], recv_sem=recv_sems.at[i], device_id=(x, y, 1 - c), device_id_type=MESH))
    return copies


def _core_exchange_start(grads):
    n = len(grads)
    grads = list(grads)
    theirs = [lax.empty((g.shape[0], g.shape[1] // 2, g.shape[2]), F32) for g in grads]

    def body(*refs):
        for cp in _core_exchange_copies(refs[:n], refs[n:2 * n], refs[2 * n], refs[2 * n + 1]):
            cp.start()
        refs[-1][...] = jnp.zeros_like(refs[-1])

    out = pl.pallas_call(
        body, name="grad_core_exchange_start",
        out_shape=(pltpu.SemaphoreType.DMA((n,)), pltpu.SemaphoreType.DMA((n,)),
                   *[pltpu.HBM(a.shape, a.dtype) for a in grads + theirs], TOKEN),
        in_specs=[HBM] * (2 * n), out_specs=(SEM, SEM, *[HBM] * (2 * n), pl.BlockSpec(memory_space=pltpu.VMEM)),
        input_output_aliases={i: 2 + i for i in range(2 * n)},
        compiler_params=_split_params(),
    )(*[_in_hbm(a) for a in grads + theirs])
    return out[0], out[1], out[2:2 + n], out[2 + n:2 + 2 * n], out[-1]


def _core_exchange_wait(send_sems, recv_sems, grads, theirs, after):
    n = len(grads)
    grads, theirs = list(grads), list(theirs)

    def body(*refs):
        for cp in _core_exchange_copies(refs[:n], refs[n:2 * n], refs[2 * n], refs[2 * n + 1]):
            cp.wait_send()
            cp.wait_recv()

    out = pl.pallas_call(
        body, name="grad_core_exchange_wait",
        out_shape=tuple(pltpu.HBM(a.shape, a.dtype) for a in grads + theirs),
        in_specs=[HBM] * (2 * n) + [SEM, SEM, ANY], out_specs=tuple([HBM] * (2 * n)),
        input_output_aliases={i: i for i in range(2 * n)},
        compiler_params=_split_params(),
    )(*grads, *theirs, send_sems, recv_sems, after)
    return out[:n], out[n:]


def _core_share(reduced, tag):
    n = len(reduced)

    def body(*refs):
        outs = refs[n:2 * n]
        send_sems, recv_sems = refs[2 * n:]
        x, y, c = _place()
        copies = []
        for i in range(n):
            cp = pltpu.make_async_remote_copy(
                src_ref=outs[i].at[c], dst_ref=outs[i].at[c], send_sem=send_sems.at[i], recv_sem=recv_sems.at[i],
                device_id=(x, y, 1 - c), device_id_type=MESH)
            cp.start()
            copies.append(cp)
        for cp in copies:
            cp.wait()

    return pl.pallas_call(
        body, name="grad_core_share_" + tag,
        in_specs=[ANY] * n, out_specs=[ANY] * n,
        out_shape=[jax.ShapeDtypeStruct(r.shape, F32) for r in reduced],
        input_output_aliases={i: i for i in range(n)},
        scratch_shapes=[pltpu.SemaphoreType.DMA((n,))] * 2,
    )(*reduced)


def _small_exchange_copies(pack_ref, slots_ref, send_sems, recv_sems):
    x, y, c = _place()
    peers = [(px, py, pc) for px in (x, 1 - x) for py in (y, 1 - y) for pc in (c, 1 - c)][1:]
    pairs = []
    for k, peer in enumerate(peers):
        def copy(sender, k=k, peer=peer):
            return pltpu.make_async_remote_copy(
                src_ref=pack_ref, dst_ref=slots_ref.at[4 * sender[0] + 2 * sender[1] + sender[2]],
                send_sem=send_sems.at[k], recv_sem=recv_sems.at[k], device_id=peer, device_id_type=MESH)

        pairs.append((copy((x, y, c)), copy(peer)))
    return pairs


def _small_exchange_start(pack):
    slots = lax.empty((N_DEV,) + pack.shape, F32)

    def body(pack_ref, slots_ref, send_sems, recv_sems, pack_thru, slots_thru, token):
        for mine, _ in _small_exchange_copies(pack_ref, slots_ref, send_sems, recv_sems):
            mine.start()
        token[...] = jnp.zeros_like(token)

    return pl.pallas_call(
        body, name="grad_small_exchange_start",
        out_shape=(pltpu.SemaphoreType.DMA((N_DEV - 1,)), pltpu.SemaphoreType.DMA((N_DEV - 1,)),
                   pltpu.HBM(pack.shape, F32), pltpu.HBM(slots.shape, F32), TOKEN),
        in_specs=[HBM, HBM], out_specs=(SEM, SEM, HBM, HBM, pl.BlockSpec(memory_space=pltpu.VMEM)),
        input_output_aliases={0: 2, 1: 3},
        compiler_params=_split_params(),
    )(_in_hbm(pack), _in_hbm(slots))


def _small_exchange_wait(send_sems, recv_sems, pack, slots, after):
    def body(pack_ref, slots_ref, send, recv, after_ref, pack_thru, slots_thru):
        for mine, theirs in _small_exchange_copies(pack_ref, slots_ref, send, recv):
            mine.wait_send()
            theirs.wait_recv()

    return pl.pallas_call(
        body, name="grad_small_exchange_wait",
        out_shape=(pltpu.HBM(pack.shape, F32), pltpu.HBM(slots.shape, F32)),
        in_specs=[HBM, HBM, SEM, SEM, ANY], out_specs=(HBM, HBM),
        input_output_aliases={0: 0, 1: 1},
        compiler_params=_split_params(),
    )(pack, slots, send_sems, recv_sems, after)


def _sum_small(pack, slots, me):
    R, C = pack.shape
    tr = _row_tile(R, C)

    def body(me_ref, p_ref, q_ref, o_ref):
        acc = jnp.where(me_ref[0] == 0, p_ref[...], q_ref[0])
        for d in range(1, N_DEV):
            acc = acc + jnp.where(me_ref[0] == d, p_ref[...], q_ref[d])
        o_ref[...] = acc

    return pl.pallas_call(
        body, name="sum_small",
        grid_spec=_scalar_grid((R // tr,), [pl.BlockSpec((tr, C), lambda i, m: (i, 0)),
                                            pl.BlockSpec((N_DEV, tr, C), lambda i, m: (0, i, 0))],
                               pl.BlockSpec((tr, C), lambda i, m: (i, 0))),
        out_shape=jax.ShapeDtypeStruct((R, C), F32),
        compiler_params=_params(("arbitrary",)),
    )(me.reshape(1), pack, slots)


def _pack_rows(parts, rows):
    flat = jnp.concatenate([a.reshape(-1) for a in parts])
    return jnp.pad(flat, (0, rows * 128 - flat.shape[0])).reshape(rows, 128)


def _unpack_rows(pack, shapes):
    flat = pack.reshape(-1)
    out, at = [], 0
    for s in shapes:
        size = 1
        for d in s:
            size *= d
        out.append(flat[at:at + size].reshape(s))
        at += size
    return out


def kernel(x, p, norm1_g, w_in, b_gate, pool_w, pool_scale, pool_proj, conv_w, conv_b, w_rg, b_rg, w_ig, b_ig, lru_lambda, lru_proj, w_out, norm2_g, w_ffn_in, w_ffn_out, ple_norm_g, w_ple_gate, w_ple_proj, final_g, loss_target, m_norm1_g, m_w_in, m_b_gate, m_pool_w, m_pool_scale, m_pool_proj, m_conv_w, m_conv_b, m_w_rg, m_b_rg, m_w_ig, m_b_ig, m_lru_lambda, m_lru_proj, m_w_out, m_norm2_g, m_w_ffn_in, m_w_ffn_out, m_ple_norm_g, m_w_ple_gate, m_w_ple_proj, m_final_g, v_norm1_g, v_w_in, v_b_gate, v_pool_w, v_pool_scale, v_pool_proj, v_conv_w, v_conv_b, v_w_rg, v_b_rg, v_w_ig, v_b_ig, v_lru_lambda, v_lru_proj, v_w_out, v_norm2_g, v_w_ffn_in, v_w_ffn_out, v_ple_norm_g, v_w_ple_gate, v_w_ple_proj, v_final_g):
    weights = dict(norm1_g=norm1_g, w_in=w_in, b_gate=b_gate, pool_w=pool_w, pool_scale=pool_scale,
                   pool_proj=pool_proj, conv_w=conv_w, conv_b=conv_b, w_rg=w_rg, b_rg=b_rg, w_ig=w_ig, b_ig=b_ig,
                   lru_lambda=lru_lambda, lru_proj=lru_proj, w_out=w_out, norm2_g=norm2_g, w_ffn_in=w_ffn_in,
                   w_ffn_out=w_ffn_out, ple_norm_g=ple_norm_g, w_ple_gate=w_ple_gate, w_ple_proj=w_ple_proj,
                   final_g=final_g)
    m_in = dict(norm1_g=m_norm1_g, w_in=m_w_in, b_gate=m_b_gate, pool_w=m_pool_w, pool_scale=m_pool_scale,
                pool_proj=m_pool_proj, conv_w=m_conv_w, conv_b=m_conv_b, w_rg=m_w_rg, b_rg=m_b_rg, w_ig=m_w_ig,
                b_ig=m_b_ig, lru_lambda=m_lru_lambda, lru_proj=m_lru_proj, w_out=m_w_out, norm2_g=m_norm2_g,
                w_ffn_in=m_w_ffn_in, w_ffn_out=m_w_ffn_out, ple_norm_g=m_ple_norm_g, w_ple_gate=m_w_ple_gate,
                w_ple_proj=m_w_ple_proj, final_g=m_final_g)
    v_in = dict(norm1_g=v_norm1_g, w_in=v_w_in, b_gate=v_b_gate, pool_w=v_pool_w, pool_scale=v_pool_scale,
                pool_proj=v_pool_proj, conv_w=v_conv_w, conv_b=v_conv_b, w_rg=v_w_rg, b_rg=v_b_rg, w_ig=v_w_ig,
                b_ig=v_b_ig, lru_lambda=v_lru_lambda, lru_proj=v_lru_proj, w_out=v_w_out, norm2_g=v_norm2_g,
                w_ffn_in=v_w_ffn_in, w_ffn_out=v_w_ffn_out, ple_norm_g=v_ple_norm_g, w_ple_gate=v_w_ple_gate,
                w_ple_proj=v_w_ple_proj, final_g=v_final_g)
    names = list(weights)
    big = ["w_in", "pool_proj", "lru_proj", "w_out", "w_ffn_in", "w_ffn_out", "w_ple_gate", "w_ple_proj"]
    by_rows = [n in ("lru_proj", "w_out", "w_ffn_out", "w_ple_gate") for n in big]
    small = [n for n in names if n not in big]

    shard_j = 2 * lax.axis_index("x") + lax.axis_index("y")
    T = x.shape[1]
    xs, ps, tgt = x[0], p[0, 0], loss_target[0]

    small_local = jnp.concatenate([b_gate[0], conv_w[0], jnp.zeros((2, 256), F32)], axis=0)
    core = lax.axis_index("c").astype(jnp.int32)
    place = jnp.stack([shard_j, core]).astype(jnp.int32)
    rows_of = dict(zip(big, by_rows))
    shard_shape = {n: weights[n].shape[1:] for n in big}
    blocks = {n: _cast_into_block(weights[n][0], rows_of[n], place[0], "cast_" + n) for n in big}
    early, late = big[:4], big[4:]
    gathered = _all_gather_weights([blocks[n] for n in early], [shard_shape[n] for n in early],
                                   [rows_of[n] for n in early], small_local)
    full = dict(zip(early, gathered[:-1]))
    late_send, late_recv, late_bufs, late_token = _gather_rest_start(
        [blocks[n] for n in late], [shard_shape[n] for n in late], [rows_of[n] for n in late], gathered[-1])
    b_gate_full = gathered[-1][0:2].reshape(1, 2 * D_MODEL)
    conv_w_full = gathered[-1][2:6]
    pool_w_1, w_rg_1, w_ig_1 = [w[0].astype(BF16) for w in (pool_w, w_rg, w_ig)]
    pool_w_b, w_rg_b, w_ig_b = [_pair_blocks(w) for w in (pool_w_1, w_rg_1, w_ig_1)]
    b_rg_row, b_ig_row = b_rg.reshape(1, D_MODEL), b_ig.reshape(1, D_MODEL)
    final_row = final_g.reshape(1, D_MODEL)

    zp, zl, zg, zt, u, h1, hs, yp, yl, xc_saved, r_saved, ig_saved = _f12_mixer(
        xs, norm1_g + late_token[0, 0], full["w_in"], b_gate_full, pool_w_1, pool_scale, full["pool_proj"],
        conv_w_full, conv_b, w_rg_1, b_rg_row, w_ig_1, b_ig_row, lru_lambda, full["lru_proj"], full["w_out"])
    full.update(zip(late, _gather_rest_wait(late_send, late_recv, late_bufs, [shard_shape[n] for n in late],
                                            [rows_of[n] for n in late], h1)))
    h2, v, ff, act = _f3_ffn(h1, norm2_g, full["w_ffn_in"], full["w_ffn_out"])

    loss_sum, dh2, g_ple_gate, g_ple_proj, vec4 = _b4_ple_loss(
        h2, ps, tgt, ple_norm_g, full["w_ple_gate"], full["w_ple_proj"], final_row)
    dff, dh1, vec3 = _b3_ffn(dh2, h1, ff, norm2_g, full["w_ffn_in"], full["w_ffn_out"])
    g_ffn_in = _wgrad(v, dff, 2 * D_FF // N_SHARDS, "wgrad_ffn_in")
    g_ffn_out = _wgrad(act, dh2, D_MODEL, "wgrad_ffn_out", tokens=WGRAD_TOKENS // 2)

    def stack(n, g):
        return g.reshape(N_SHARDS, g.shape[0] // N_SHARDS, g.shape[1]) if rows_of[n] else g[None]

    def chip_sums_of(group, grads_of, tag):
        stacked = [stack(n, grads_of[n]) for n in group]
        theirs = _core_exchange(stacked, "grad_core_exchange_" + tag)
        return [_sum_cores(g, t, core, "sum_cores_" + n) for g, t, n in zip(stacked, theirs, group)]

    late_rows = [rows_of[n] for n in late]
    late_grads = dict(w_ffn_in=g_ffn_in, w_ffn_out=g_ffn_out, w_ple_gate=g_ple_gate, w_ple_proj=g_ple_proj)
    cx_send, cx_recv, late_stacked, late_theirs, cx_token = _core_exchange_start(
        [stack(n, late_grads[n]) for n in late])
    dzt, dyp, dyl, g_w_out, vec_g = _b2_gates(dh1, zt, yp, yl, b_gate_full + cx_token[0, 0], full["w_out"])
    late_stacked, late_theirs = _core_exchange_wait(cx_send, cx_recv, late_stacked, late_theirs, dzt)
    late_sums = [_sum_cores(g, t, core, "sum_cores_" + n) for g, t, n in zip(late_stacked, late_theirs, late)]
    ex_send, ex_recv, late_sums, late_slots, ex_token = _chip_exchange_start(late_sums, late_rows, "late")
    dzl, dzg, g_lru_proj, g_w_rg, g_w_ig, vec_l = _b2_lru(
        dyl, zl, zg, hs, xc_saved, r_saved, ig_saved, conv_w_full, w_rg_b, w_ig_b, lru_lambda + ex_token[0, 0],
        full["lru_proj"])
    dzp, grad_x, g_pool_proj, g_pool_w, vec_p = _b12_pool_in_proj(
        dyp, zp, dzl, dzg, dzt, xs, dh1, norm1_g, full["w_in"], pool_w_b, pool_scale, full["pool_proj"])
    small_full = dict(
        norm1_g=vec_p[1], b_gate=vec_g[0:2], pool_w=_unpair_blocks(g_pool_w), pool_scale=vec_p[0, :POOL_WIDTH],
        conv_w=vec_l[_V_CONVW:_V_CONVW + CONV_WIDTH], conv_b=vec_l[_V_CONVB], w_rg=_unpair_blocks(g_w_rg),
        b_rg=vec_l[_V_BRG], w_ig=_unpair_blocks(g_w_ig), b_ig=vec_l[_V_BIG], lru_lambda=vec_l[_V_LAM], norm2_g=vec3[0], ple_norm_g=vec4[1],
        final_g=vec4[0])
    full_shapes = [small_full[n].shape for n in small]
    n_full = sum(int(small_full[n].size) for n in small)
    rows_full = -(-n_full // (128 * ROW_TILE)) * ROW_TILE
    sm_send, sm_recv, sm_pack, sm_slots, sm_token = _small_exchange_start(
        _pack_rows([small_full[n] for n in small], rows_full))
    g_w_in = jnp.concatenate([
        _wgrad(u, dzp, POOL_WIDTH, "wgrad_in_pool", after=sm_token),
        _wgrad(u, dzl, D_MODEL, "wgrad_in_lru", after=sm_token),
        _wgrad(u, dzg, D_MODEL, "wgrad_in_gelu", after=sm_token),
        _wgrad(u, dzt, D_MODEL, "wgrad_in_gate", after=sm_token)], axis=1)

    loss = lax.psum(loss_sum[0, 0] * (0.5 / D_MODEL), ("x", "y", "c"))

    early_rows = [rows_of[n] for n in early]
    early_sums = chip_sums_of(early, dict(w_in=g_w_in, pool_proj=g_pool_proj, lru_proj=g_lru_proj, w_out=g_w_out),
                              "early")
    e_send, e_recv, early_sums, early_slots, e_token = _chip_exchange_start(early_sums, early_rows, "early")
    grads, deltas, new_m, new_v = {}, {}, {}, {}

    def finish(group, sums, slots, tag):
        reduced = _core_share([_sum_chips(s, q, rows_of[n], place, "sum_chips_" + n)
                               for s, q, n in zip(sums, slots, group)], tag)
        for n, r in zip(group, reduced):
            g = r.reshape(r.shape[0] * r.shape[1], r.shape[2])
            d, nm, nv = _adamw(weights[n][0], g, m_in[n][0], v_in[n][0], "adamw_" + n)
            grads[n], deltas[n], new_m[n], new_v[n] = g[None], d[None], nm[None], nv[None]

    late_sums, late_slots = _chip_exchange_wait(ex_send, ex_recv, late_sums, late_slots, late_rows, e_token, "late")
    finish(late, late_sums, late_slots, "late")

    sm_pack, sm_slots = _small_exchange_wait(sm_send, sm_recv, sm_pack, sm_slots, e_token)
    device = (4 * lax.axis_index("x") + 2 * lax.axis_index("y") + lax.axis_index("c")).astype(jnp.int32)
    summed = dict(zip(small, _unpack_rows(_sum_small(sm_pack, sm_slots, device), full_shapes)))
    summed["b_gate"] = lax.dynamic_slice_in_dim(summed["b_gate"], shard_j * 256, 256, axis=1)
    summed["conv_w"] = lax.dynamic_slice_in_dim(summed["conv_w"], shard_j * 256, 256, axis=1)
    local_shapes = [weights[n].shape for n in small]
    n_local = sum(int(weights[n].size) for n in small)
    rows_local = -(-n_local // (128 * ROW_TILE)) * ROW_TILE
    packs = [_pack_rows([src[n] for n in small], rows_local) for src in (weights, summed, m_in, v_in)]
    d_s, nm_s, nv_s = _adamw(*packs, "adamw_small")
    for dst, pack in ((grads, packs[1]), (deltas, d_s), (new_m, nm_s), (new_v, nv_s)):
        dst.update(zip(small, _unpack_rows(pack, local_shapes)))

    done = d_s[:SUBLANES]
    for n in late:
        done = done + deltas[n][0, :SUBLANES, :128]
    early_sums, early_slots = _chip_exchange_wait(e_send, e_recv, early_sums, early_slots, early_rows, done, "early")
    finish(early, early_sums, early_slots, "early")

    return (loss, grad_x[None], *[grads[n] for n in names], *[deltas[n] for n in names],
            *[new_m[n] for n in names], *[new_v[n] for n in names])
```

```python
import functools

import jax
import jax.numpy as jnp
from jax import lax
from jax.experimental import pallas as pl
from jax.experimental.pallas import tpu as pltpu

F32 = jnp.float32
BF16 = jnp.bfloat16

D_MODEL = 1024
POOL_WIDTH = 512
POOL_GROUP_DIM = 128
POOL_WINDOWS = (2, 4, 8, 16)
POOL_HALO = 16
LRU_HEADS = 8
LRU_HEAD_DIM = 128
CONV_WIDTH = 4
LRU_C = 8.0
D_FF = 2816
PLE_DIM = 256
RMS_EPS = 1e-6
N_SHARDS = 4
N_DEV = 8

ADAM_LR = 0.001
ADAM_B1 = 0.9
ADAM_B2 = 0.999
ADAM_EPS = 1e-08
ADAM_WD = 0.01
ADAM_STEP = 10

ROW_TILE = 256
WIDE_TILE = 512
WGRAD_TOKENS = 2048
SUBLANES = 8
VMEM_LIMIT = 56 * 1024 * 1024
MESH = pl.DeviceIdType.MESH
ANY = pl.BlockSpec(memory_space=pl.ANY)


def _params(semantics=None):
    return pltpu.CompilerParams(dimension_semantics=semantics, vmem_limit_bytes=VMEM_LIMIT)


def _resident(shape):
    n = len(shape)
    return pl.BlockSpec(shape, lambda *_: (0,) * n, pipeline_mode=pl.Buffered(1))


def _acc(shape):
    n = len(shape)
    return pl.BlockSpec(shape, lambda *_: (0,) * n)


def _rows(tile, cols):
    return pl.BlockSpec((tile, cols), lambda i: (i, 0))


def _rows_rev(tile, cols, n_tiles):
    return pl.BlockSpec((tile, cols), lambda i: (n_tiles - 1 - i, 0))


def _halo_before_rev(rows, cols, tile, n_tiles):
    per = tile // rows
    return pl.BlockSpec((rows, cols), lambda i: (jnp.maximum((n_tiles - 1 - i) * per - 1, 0), 0))


def _nn(a, b):
    return jnp.dot(a, b, preferred_element_type=F32)


def _nt(a, b):
    return lax.dot_general(a, b, (((1,), (1,)), ((), ())), preferred_element_type=F32)


def _tn(a, b):
    return lax.dot_general(a, b, (((0,), (0,)), ((), ())), preferred_element_type=F32)


def _rms(x):
    r = lax.rsqrt(jnp.mean(x * x, axis=-1, keepdims=True) + RMS_EPS)
    return x * r, r


def _rms_bwd(dn, n, r):
    return r * (dn - n * jnp.mean(dn * n, axis=-1, keepdims=True))


def _sigmoid(x):
    return 0.5 * jnp.tanh(0.5 * x) + 0.5


_GELU_C = 0.7978845608028654
_GELU_A = 0.044715


def _gelu(x):
    t = jnp.tanh(_GELU_C * (x + _GELU_A * x * x * x))
    return 0.5 * x * (1.0 + t)


def _gelu_and_grad(x):
    x2 = x * x
    t = jnp.tanh(_GELU_C * (x + _GELU_A * x2 * x))
    cdf = 0.5 * (1.0 + t)
    grad = cdf + 0.5 * x * (1.0 - t * t) * _GELU_C * (1.0 + 3.0 * _GELU_A * x2)
    return x * cdf, grad


def _softplus_neg(lam):
    e = jnp.exp(-jnp.abs(lam))
    sp = jnp.maximum(-lam, 0.0) + jnp.log1p(e)
    return sp, -_sigmoid(-lam)


def _colsum(v):
    return jnp.sum(v, axis=0, keepdims=True)


def _row_ids(shape):
    return lax.broadcasted_iota(jnp.int32, shape, 0)


def _shift_down(cat, k):
    return pltpu.roll(cat, k, 0) if k else cat


def _shift_up(cat, k):
    return pltpu.roll(cat, cat.shape[0] - k, 0) if k else cat


IN_SPLITS = (0, POOL_WIDTH, POOL_WIDTH + D_MODEL, POOL_WIDTH + 2 * D_MODEL, POOL_WIDTH + 4 * D_MODEL)
IN_WIDTHS = tuple(IN_SPLITS[k + 1] - IN_SPLITS[k] for k in range(4))
PROJ_CHUNK = 256
PAIR_DIM = 2 * LRU_HEAD_DIM


def _pair_blocks(w):
    zero = jnp.zeros_like(w[0::2])
    return jnp.concatenate([jnp.concatenate([w[0::2], zero], axis=2), jnp.concatenate([zero, w[1::2]], axis=2)], axis=1)


def _unpair_blocks(w):
    n, d2, _ = w.shape
    d = d2 // 2
    return jnp.stack([w[:, :d, :d], w[:, d:, d:]], axis=1).reshape(2 * n, d, d)


def _no_tick():
    pass


class _Interleaved:
    def __init__(self, pieces):
        self._pieces = iter(pieces)

    def tick(self, n=1):
        for _ in range(n):
            piece = next(self._pieces, None)
            if piece is not None:
                piece()

    def flush(self):
        for piece in self._pieces:
            piece()


def _pool_forward(zp_cat, pw_ref, first_row, tick=_no_tick):
    tt = zp_cat.shape[0] - POOL_HALO
    t_glob = first_row + _row_ids((tt, POOL_GROUP_DIM))
    pooled, mixed = [], []
    for g, w in enumerate(POOL_WINDOWS):
        cat = zp_cat[:, g * POOL_GROUP_DIM:(g + 1) * POOL_GROUP_DIM]
        s, k = cat, 1
        while k < w:
            s = s + _shift_down(s, k)
            k *= 2
        cnt = jnp.minimum(t_glob + 1, w).astype(F32)
        pooled.append(s[POOL_HALO:] / cnt - cat[POOL_HALO:])
        per = pw_ref.shape[-1] // POOL_GROUP_DIM
        if (g + 1) % per == 0:
            block = jnp.concatenate(pooled[-per:], axis=1).astype(BF16)
            mixed.append(_nn(block, pw_ref[g // per]))
        tick()
    return jnp.concatenate(pooled, axis=1), jnp.concatenate(mixed, axis=1)


def _lru_gates(zl_cat, conv_w, conv_b, wrg_ref, brg, wig_ref, big, sp, first_row, tick=_no_tick):
    xc = conv_w[CONV_WIDTH - 1:CONV_WIDTH] * zl_cat
    for k in range(1, CONV_WIDTH):
        xc = xc + conv_w[CONV_WIDTH - 1 - k:CONV_WIDTH - k] * _shift_down(zl_cat, k)
        tick()
    xc = xc[SUBLANES:] + conv_b
    xh = xc.astype(BF16)
    pr, pi = [], []
    width = wrg_ref.shape[-1]
    for p in range(D_MODEL // width):
        xs = xh[:, p * width:(p + 1) * width]
        pr.append(_nn(xs, wrg_ref[p]))
        pi.append(_nn(xs, wig_ref[p]))
    r = _sigmoid(jnp.concatenate(pr, axis=1) + brg)
    tick()
    ig = _sigmoid(jnp.concatenate(pi, axis=1) + big)
    tick()
    a, mult = _decay(r, sp, first_row, tick)
    tick()
    return xc, r, ig, a, mult


def _decay(r, sp, first_row, tick=_no_tick):
    a = jnp.exp(-LRU_C * r * sp)
    tick()
    mult = jnp.sqrt(jnp.maximum(1.0 - a * a, 0.0))
    t_glob = first_row + _row_ids(r.shape)
    return a, jnp.where(t_glob == 0, 1.0, mult)


def _f12_mixer(x, norm1_g, w_in, b_gate, pool_w, pool_scale, pool_proj, conv_w, conv_b, w_rg, b_rg, w_ig, b_ig,
               lru_lambda, lru_proj, w_out):
    T = x.shape[0]
    tt = ROW_TILE
    nt = T // tt
    n_groups = tt // SUBLANES
    proj_mid = IN_SPLITS[3] + D_MODEL // 2

    def body(xm_ref, x_ref, g1_ref, win_ref, bg_ref, pw_ref, ps_ref, pp_ref, cw_ref, cb_ref,
             wrg_ref, brg_ref, wig_ref, big_ref, lam_ref, lp_ref, wo_ref,
             zp_ref, zl_ref, zg_ref, zt_ref, u_ref, h1_ref, hs_ref, yp_ref, yl_ref, xc_ref, r_ref, ig_ref,
             zbuf, zp_halo, zl_halo, a_s, b_s, carry_s):
        s = pl.program_id(0)

        @pl.when(s == 0)
        def _():
            zbuf[1] = jnp.zeros((tt, IN_SPLITS[4]), F32)
            zp_halo[...] = jnp.zeros_like(zp_halo)
            zl_halo[...] = jnp.zeros_like(zl_halo)
            carry_s[...] = jnp.zeros_like(carry_s)

        z_new, z_old = zbuf.at[s % 2], zbuf.at[(s + 1) % 2]
        first = s <= 1
        first_row = jnp.maximum(s - 1, 0) * tt

        n1, _ = _rms(xm_ref[...])
        u = (n1 * g1_ref[...]).astype(BF16)
        u_ref[...] = u

        z_refs = (zp_ref, zl_ref, zg_ref, zt_ref)

        def project(lo):
            k = max(i for i in range(4) if IN_SPLITS[i] <= lo)
            part = _nn(u, win_ref[:, lo:lo + PROJ_CHUNK])
            z_new[:, lo:lo + PROJ_CHUNK] = part
            z_refs[k][:, lo - IN_SPLITS[k]:lo - IN_SPLITS[k] + PROJ_CHUNK] = part.astype(z_refs[k].dtype)

        before_scan = _Interleaved(functools.partial(project, lo) for lo in range(0, proj_mid, PROJ_CHUNK))
        after_scan = _Interleaved(functools.partial(project, lo) for lo in range(proj_mid, IN_SPLITS[4], PROJ_CHUNK))

        zp_cat = jnp.concatenate([jnp.where(first, 0.0, zp_halo[...]), z_old[:, IN_SPLITS[0]:IN_SPLITS[1]]], axis=0)
        _, mixed = _pool_forward(zp_cat, pw_ref, first_row, before_scan.tick)
        y_pool = _nn((mixed * ps_ref[...]).astype(BF16), pp_ref[...])

        sp, _ = _softplus_neg(lam_ref[...])
        zl_cat = jnp.concatenate([jnp.where(first, 0.0, zl_halo[...]), z_old[:, IN_SPLITS[1]:IN_SPLITS[2]]], axis=0)
        xc, r, ig, a, mult = _lru_gates(zl_cat, cw_ref[...], cb_ref[...], wrg_ref, brg_ref[...], wig_ref,
                                        big_ref[...], sp, first_row, before_scan.tick)
        a_s[...] = a
        b_s[...] = mult * ig * xc
        xc_ref[...] = xc.astype(BF16)
        r_ref[...] = r.astype(BF16)
        ig_ref[...] = ig.astype(BF16)
        before_scan.flush()

        rows8 = _row_ids((SUBLANES, D_MODEL))

        def group(g, carry):
            at = pl.ds(pl.multiple_of(g * SUBLANES, SUBLANES), SUBLANES)
            A, B = a_s[at, :], b_s[at, :]
            for s in (1, 2, 4):
                m = rows8 >= s
                B = jnp.where(m, A * pltpu.roll(B, s, 0) + B, B)
                A = jnp.where(m, A * pltpu.roll(A, s, 0), A)
            h = A * carry + B
            hs_ref[at, :] = h
            return jnp.broadcast_to(h[SUBLANES - 1:SUBLANES, :], (SUBLANES, D_MODEL))

        carry_s[...] = lax.fori_loop(0, n_groups, group, jnp.where(first, 0.0, carry_s[...]))
        gelu = _gelu(z_old[:, IN_SPLITS[2]:IN_SPLITS[3]])
        after_scan.tick(2)
        y_lru = _nn((hs_ref[...] * gelu).astype(BF16), lp_ref[...])

        gates = _sigmoid(z_old[:, IN_SPLITS[3]:IN_SPLITS[4]] + bg_ref[...])
        after_scan.tick(2)
        merged = gates[:, :D_MODEL] * y_pool + gates[:, D_MODEL:] * y_lru
        after_scan.flush()
        h1_ref[...] = x_ref[...] + _nn(merged.astype(BF16), wo_ref[...])
        yp_ref[...] = y_pool.astype(BF16)
        yl_ref[...] = y_lru.astype(BF16)
        zp_halo[...] = z_old[tt - POOL_HALO:, IN_SPLITS[0]:IN_SPLITS[1]]
        zl_halo[...] = z_old[tt - SUBLANES:, IN_SPLITS[1]:IN_SPLITS[2]]

    def ahead(cols):
        return pl.BlockSpec((tt, cols), lambda s: (jnp.minimum(s, nt - 1), 0))

    def behind(cols):
        return pl.BlockSpec((tt, cols), lambda s: (jnp.maximum(s - 1, 0), 0))

    res = [norm1_g, w_in, b_gate, pool_w, pool_scale, pool_proj, conv_w, conv_b, w_rg, b_rg, w_ig, b_ig, lru_lambda,
           lru_proj, w_out]
    return pl.pallas_call(
        body, name="f12_mixer", grid=(nt + 1,),
        in_specs=[ahead(D_MODEL), behind(D_MODEL)] + [_resident(w.shape) for w in res],
        out_specs=[ahead(w) for w in IN_WIDTHS] + [ahead(D_MODEL)] + [behind(D_MODEL)] * 7,
        out_shape=[jax.ShapeDtypeStruct((T, w), dt) for w, dt in zip(IN_WIDTHS, (F32, F32, F32, BF16))]
        + [jax.ShapeDtypeStruct((T, D_MODEL), BF16), jax.ShapeDtypeStruct((T, D_MODEL), F32),
           jax.ShapeDtypeStruct((T, D_MODEL), F32)] + [jax.ShapeDtypeStruct((T, D_MODEL), BF16)] * 5,
        scratch_shapes=[pltpu.VMEM((2, tt, IN_SPLITS[4]), F32), pltpu.VMEM((POOL_HALO, POOL_WIDTH), F32),
                        pltpu.VMEM((SUBLANES, D_MODEL), F32), pltpu.VMEM((tt, D_MODEL), F32),
                        pltpu.VMEM((tt, D_MODEL), F32), pltpu.VMEM((SUBLANES, D_MODEL), F32)],
        compiler_params=_params(("arbitrary",)),
    )(x, x, *res)


def _f3_ffn(h1, norm2_g, w_ffn_in, w_ffn_out):
    T = h1.shape[0]
    tm = ROW_TILE

    def body(h_ref, g_ref, wi_ref, wo_ref, h2_ref, v_ref, ff_ref, act_ref):
        h = h_ref[...]
        n, _ = _rms(h)
        v = (n * g_ref[...]).astype(BF16)
        v_ref[...] = v
        g_ff = _nn(v, wi_ref[:, :D_FF])
        u_ff = _nn(v, wi_ref[:, D_FF:])
        ff_ref[:, :D_FF] = g_ff.astype(BF16)
        ff_ref[:, D_FF:] = u_ff.astype(BF16)
        act = (g_ff * _sigmoid(g_ff) * u_ff).astype(BF16)
        act_ref[...] = act
        h2_ref[...] = h + _nn(act, wo_ref[...])

    return pl.pallas_call(
        body, name="f3_ffn", grid=(T // tm,),
        in_specs=[_rows(tm, D_MODEL), _resident((1, D_MODEL)), _resident(w_ffn_in.shape), _resident(w_ffn_out.shape)],
        out_specs=[_rows(tm, D_MODEL), _rows(tm, D_MODEL), _rows(tm, 2 * D_FF), _rows(tm, D_FF)],
        out_shape=[jax.ShapeDtypeStruct((T, D_MODEL), F32), jax.ShapeDtypeStruct((T, D_MODEL), BF16),
                   jax.ShapeDtypeStruct((T, 2 * D_FF), BF16), jax.ShapeDtypeStruct((T, D_FF), BF16)],
        compiler_params=_params(("arbitrary",)),
    )(h1, norm2_g, w_ffn_in, w_ffn_out)


def _b4_ple_loss(h2, p, target, ple_norm_g, w_ple_gate, w_ple_proj, final_g):
    T = h2.shape[0]
    tm = WIDE_TILE

    def body(h_ref, p_ref, t_ref, gp_ref, wg_ref, wp_ref, gf_ref, loss_ref, dh2_ref, dwg_ref, dwp_ref, vec_ref):
        @pl.when(pl.program_id(0) == 0)
        def _():
            loss_ref[...] = jnp.zeros_like(loss_ref)
            dwg_ref[...] = jnp.zeros_like(dwg_ref)
            dwp_ref[...] = jnp.zeros_like(dwp_ref)
            vec_ref[...] = jnp.zeros_like(vec_ref)

        h2v = h_ref[...]
        n3, r3 = _rms(h2v)
        n3g = (n3 * gp_ref[...]).astype(BF16)
        pb = p_ref[...].astype(BF16)
        q = _nn(n3g, wg_ref[...])
        e = _nn(pb, wp_ref[...])
        pg = _sigmoid(q)
        h3 = h2v + pg * e
        n4, r4 = _rms(h3)
        diff = n4 * gf_ref[...] - t_ref[...]
        loss_ref[...] += jnp.sum(diff * diff).reshape(1, 1)
        dy = diff * (1.0 / D_MODEL)
        vec_ref[0:1, :] += _colsum(dy * n4)
        dh3 = _rms_bwd(dy * gf_ref[...], n4, r4)
        de = (dh3 * pg).astype(BF16)
        dq = (dh3 * e * pg * (1.0 - pg)).astype(BF16)
        dn3g = _nt(dq, wg_ref[...])
        dwg_ref[...] += _tn(n3g, dq)
        dwp_ref[...] += _tn(pb, de)
        vec_ref[1:2, :] += _colsum(dn3g * n3)
        dh2_ref[...] = dh3 + _rms_bwd(dn3g * gp_ref[...], n3, r3)

    return pl.pallas_call(
        body, name="b4_ple_loss", grid=(T // tm,),
        in_specs=[_rows(tm, D_MODEL), _rows(tm, PLE_DIM), _rows(tm, D_MODEL), _resident((1, D_MODEL)),
                  _resident(w_ple_gate.shape), _resident(w_ple_proj.shape), _resident((1, D_MODEL))],
        out_specs=[_acc((1, 1)), _rows(tm, D_MODEL), _acc(w_ple_gate.shape), _acc(w_ple_proj.shape),
                   _acc((SUBLANES, D_MODEL))],
        out_shape=[jax.ShapeDtypeStruct((1, 1), F32), jax.ShapeDtypeStruct((T, D_MODEL), F32),
                   jax.ShapeDtypeStruct(w_ple_gate.shape, F32), jax.ShapeDtypeStruct(w_ple_proj.shape, F32),
                   jax.ShapeDtypeStruct((SUBLANES, D_MODEL), F32)],
        compiler_params=_params(("arbitrary",)),
    )(h2, p, target, ple_norm_g, w_ple_gate, w_ple_proj, final_g)


def _b3_ffn(dh2, h1, ff, norm2_g, w_ffn_in, w_ffn_out):
    T = h1.shape[0]
    tm = ROW_TILE

    def body(d_ref, h_ref, ff_ref, g_ref, wi_ref, wo_ref, dff_ref, dh1_ref, vec_ref):
        @pl.when(pl.program_id(0) == 0)
        def _():
            vec_ref[...] = jnp.zeros_like(vec_ref)

        dh2v = d_ref[...]
        dact = _nt(dh2v.astype(BF16), wo_ref[...])
        g_ff = ff_ref[:, :D_FF].astype(F32)
        u_ff = ff_ref[:, D_FF:].astype(F32)
        s = _sigmoid(g_ff)
        dg = (dact * u_ff * (s * (1.0 + g_ff * (1.0 - s)))).astype(BF16)
        du = (dact * (g_ff * s)).astype(BF16)
        dff_ref[:, :D_FF] = dg
        dff_ref[:, D_FF:] = du
        dv = _nt(dg, wi_ref[:, :D_FF]) + _nt(du, wi_ref[:, D_FF:])
        n2, r2 = _rms(h_ref[...])
        vec_ref[0:1, :] += _colsum(dv * n2)
        dh1_ref[...] = dh2v + _rms_bwd(dv * g_ref[...], n2, r2)

    return pl.pallas_call(
        body, name="b3_ffn", grid=(T // tm,),
        in_specs=[_rows(tm, D_MODEL), _rows(tm, D_MODEL), _rows(tm, 2 * D_FF), _resident((1, D_MODEL)),
                  _resident(w_ffn_in.shape), _resident(w_ffn_out.shape)],
        out_specs=[_rows(tm, 2 * D_FF), _rows(tm, D_MODEL), _acc((SUBLANES, D_MODEL))],
        out_shape=[jax.ShapeDtypeStruct((T, 2 * D_FF), BF16), jax.ShapeDtypeStruct((T, D_MODEL), F32),
                   jax.ShapeDtypeStruct((SUBLANES, D_MODEL), F32)],
        compiler_params=_params(("arbitrary",)),
    )(dh2, h1, ff, norm2_g, w_ffn_in, w_ffn_out)


def _wgrad(a, b, col_tile, name, tokens=WGRAD_TOKENS, after=None):
    T, K = a.shape
    N = b.shape[1]
    tk = min(T, tokens)

    def body(a_ref, b_ref, *rest):
        o_ref = rest[-1]

        @pl.when(pl.program_id(1) == 0)
        def _():
            o_ref[...] = jnp.zeros_like(o_ref)

        o_ref[...] += _tn(a_ref[...].astype(BF16), b_ref[...].astype(BF16))

    return pl.pallas_call(
        body, name=name, grid=(N // col_tile, T // tk),
        in_specs=[pl.BlockSpec((tk, K), lambda j, k: (k, 0)), pl.BlockSpec((tk, col_tile), lambda j, k: (k, j))]
        + ([] if after is None else [ANY]),
        out_specs=pl.BlockSpec((K, col_tile), lambda j, k: (0, j)),
        out_shape=jax.ShapeDtypeStruct((K, N), F32),
        compiler_params=_params(("arbitrary", "arbitrary")),
    )(a, b, *([] if after is None else [after]))


def _wgrad_parts(a, parts, col_tile, name, after):
    T, K = a.shape
    tk = min(T, WGRAD_TOKENS)
    blocks = [p.shape[1] // col_tile for p in parts]
    starts = [sum(blocks[:i]) for i in range(len(parts))]

    def body(a_ref, *rest):
        b_refs, o_ref = rest[:len(parts)], rest[-1]
        j = pl.program_id(0)

        @pl.when(pl.program_id(1) == 0)
        def _():
            o_ref[...] = jnp.zeros_like(o_ref)

        for b_ref, lo, nb in zip(b_refs, starts, blocks):
            @pl.when((j >= lo) & (j < lo + nb))
            def _(b_ref=b_ref):
                o_ref[...] += _tn(a_ref[...], b_ref[...])

    def part_spec(lo, nb):
        def index(j, k):
            mine = (j >= lo) & (j < lo + nb)
            return jnp.where(mine, k, 0), jnp.where(mine, j - lo, 0)
        return pl.BlockSpec((tk, col_tile), index)

    return pl.pallas_call(
        body, name=name, grid=(sum(blocks), T // tk),
        in_specs=[pl.BlockSpec((tk, K), lambda j, k: (k, 0))] + [part_spec(lo, nb) for lo, nb in zip(starts, blocks)]
        + [ANY],
        out_specs=pl.BlockSpec((K, col_tile), lambda j, k: (0, j)),
        out_shape=jax.ShapeDtypeStruct((K, sum(blocks) * col_tile), F32),
        compiler_params=_params(("arbitrary", "arbitrary")),
    )(a, *parts, after)


def _b2_gates(dh1, zt, yp, yl, b_gate, w_out):
    T = dh1.shape[0]
    tm = WIDE_TILE

    def body(d_ref, zt_ref, yp_ref, yl_ref, bg_ref, wo_ref, dzt_ref, dyp_ref, dyl_ref, dwo_ref, vec_ref):
        @pl.when(pl.program_id(0) == 0)
        def _():
            dwo_ref[...] = jnp.zeros_like(dwo_ref)
            vec_ref[...] = jnp.zeros_like(vec_ref)

        db = d_ref[...].astype(BF16)
        dm = _nt(db, wo_ref[...])
        gates = _sigmoid(zt_ref[...].astype(F32) + bg_ref[...])
        g0, g1 = gates[:, :D_MODEL], gates[:, D_MODEL:]
        y_pool, y_lru = yp_ref[...].astype(F32), yl_ref[...].astype(F32)
        dwo_ref[...] += _tn((g0 * y_pool + g1 * y_lru).astype(BF16), db)
        dz0 = dm * y_pool * g0 * (1.0 - g0)
        dz1 = dm * y_lru * g1 * (1.0 - g1)
        vec_ref[0:1, :] += _colsum(dz0)
        vec_ref[1:2, :] += _colsum(dz1)
        dzt_ref[:, :D_MODEL] = dz0.astype(BF16)
        dzt_ref[:, D_MODEL:] = dz1.astype(BF16)
        dyp_ref[...] = (dm * g0).astype(BF16)
        dyl_ref[...] = (dm * g1).astype(BF16)

    return pl.pallas_call(
        body, name="b2_gates", grid=(T // tm,),
        in_specs=[_rows(tm, D_MODEL), _rows(tm, 2 * D_MODEL), _rows(tm, D_MODEL), _rows(tm, D_MODEL),
                  _resident(b_gate.shape), _resident(w_out.shape)],
        out_specs=[_rows(tm, 2 * D_MODEL), _rows(tm, D_MODEL), _rows(tm, D_MODEL), _acc(w_out.shape),
                   _acc((SUBLANES, D_MODEL))],
        out_shape=[jax.ShapeDtypeStruct((T, 2 * D_MODEL), BF16), jax.ShapeDtypeStruct((T, D_MODEL), BF16),
                   jax.ShapeDtypeStruct((T, D_MODEL), BF16), jax.ShapeDtypeStruct(w_out.shape, F32),
                   jax.ShapeDtypeStruct((SUBLANES, D_MODEL), F32)],
        compiler_params=_params(("arbitrary",)),
    )(dh1, zt, yp, yl, b_gate, w_out)


def _b12_pool_in_proj(dyp, zp, dzl, dzg, dzt, x, dh1, norm1_g, w_in, pool_w, pool_scale, pool_proj):
    T = zp.shape[0]
    tt = ROW_TILE
    nt = T // tt

    def body(dy_ref, zp_ref, zph_ref, dzl_ref, dzg_ref, dzt_ref, x_ref, dh_ref, g1_ref, win_ref, pw_ref, ps_ref, pp_ref,
             dzp_ref, dx_ref, dpp_ref, dpw_ref, vec_ref, q_next):
        i = pl.program_id(0)
        ti = nt - 1 - i
        first_row = ti * tt

        @pl.when(i == 0)
        def _():
            dpp_ref[...] = jnp.zeros_like(dpp_ref)
            dpw_ref[...] = jnp.zeros_like(dpw_ref)
            vec_ref[...] = jnp.zeros_like(vec_ref)
            q_next[...] = jnp.zeros_like(q_next)

        du_parts = []

        def project(lo):
            k = max(i for i in range(4) if IN_SPLITS[i] <= lo)
            dz_ref = (None, dzl_ref, dzg_ref, dzt_ref)[k]
            at = lo - IN_SPLITS[k]
            part = _nt(dz_ref[:, at:at + PROJ_CHUNK], win_ref[:, lo:lo + PROJ_CHUNK])
            du_parts[:] = [part if not du_parts else du_parts[0] + part]

        mxu = _Interleaved(functools.partial(project, lo) for lo in range(IN_SPLITS[1], IN_SPLITS[4], PROJ_CHUNK))

        keep = (ti > 0).astype(F32)
        zp_cat = jnp.concatenate([zph_ref[...] * keep, zp_ref[...]], axis=0)
        pooled, mixed = _pool_forward(zp_cat, pw_ref, first_row, mxu.tick)
        dy = dy_ref[...]
        dpp_ref[...] += _tn((mixed * ps_ref[...]).astype(BF16), dy)
        mxu.tick(2)
        dms = _nt(dy, pp_ref[...])
        mxu.tick(2)
        vec_ref[0:1, :POOL_WIDTH] += _colsum(dms * mixed)
        dmixed = (dms * ps_ref[...]).astype(BF16)
        t_glob = first_row + _row_ids((tt, POOL_GROUP_DIM))
        dz, q_all, dpooled_pairs = [], [], []
        for p in range(len(POOL_WINDOWS) // 2):
            pair = slice(p * PAIR_DIM, (p + 1) * PAIR_DIM)
            dpw_ref[p] += _tn(pooled[:, pair].astype(BF16), dmixed[:, pair])
            dpooled_pairs.append(_nt(dmixed[:, pair], pw_ref[p]))
        dpooled_all = jnp.concatenate(dpooled_pairs, axis=1)
        for g, w in enumerate(POOL_WINDOWS):
            cols = slice(g * POOL_GROUP_DIM, (g + 1) * POOL_GROUP_DIM)
            dpooled = dpooled_all[:, cols]
            q = dpooled / jnp.minimum(t_glob + 1, w).astype(F32)
            q_all.append(q)
            s, k = jnp.concatenate([q, q_next[:, cols]], axis=0), 1
            while k < w:
                s = s + _shift_up(s, k)
                k *= 2
            dz.append(s[:tt] - dpooled)
            mxu.tick(2)
        dzp = jnp.concatenate(dz, axis=1).astype(BF16)
        dzp_ref[...] = dzp
        q_next[...] = jnp.concatenate([q[:POOL_HALO] for q in q_all], axis=1)
        mxu.flush()

        du = du_parts[0] + _nt(dzp, win_ref[:, IN_SPLITS[0]:IN_SPLITS[1]])
        n1, r1 = _rms(x_ref[...])
        vec_ref[1:2, :] += _colsum(du * n1)
        dx_ref[...] = dh_ref[...] + _rms_bwd(du * g1_ref[...], n1, r1)

    rev = functools.partial(_rows_rev, n_tiles=nt)
    res = [norm1_g, w_in, pool_w, pool_scale, pool_proj]
    return pl.pallas_call(
        body, name="b12_pool_in_proj", grid=(nt,),
        in_specs=[rev(tt, D_MODEL), rev(tt, POOL_WIDTH), _halo_before_rev(POOL_HALO, POOL_WIDTH, tt, nt),
                  rev(tt, D_MODEL), rev(tt, D_MODEL), rev(tt, 2 * D_MODEL), rev(tt, D_MODEL), rev(tt, D_MODEL)]
        + [_resident(w.shape) for w in res],
        out_specs=[rev(tt, POOL_WIDTH), rev(tt, D_MODEL), _acc(pool_proj.shape), _acc(pool_w.shape),
                   _acc((SUBLANES, D_MODEL))],
        out_shape=[jax.ShapeDtypeStruct((T, POOL_WIDTH), BF16), jax.ShapeDtypeStruct((T, D_MODEL), F32),
                   jax.ShapeDtypeStruct(pool_proj.shape, F32), jax.ShapeDtypeStruct(pool_w.shape, F32),
                   jax.ShapeDtypeStruct((SUBLANES, D_MODEL), F32)],
        scratch_shapes=[pltpu.VMEM((POOL_HALO, POOL_WIDTH), F32)],
        compiler_params=_params(("arbitrary",)),
    )(dyp, zp, zp, dzl, dzg, dzt, x, dh1, *res)


_V_CONVW, _V_CONVB, _V_BRG, _V_BIG, _V_LAM = 0, 4, 5, 6, 7


def _b2_lru(dyl, zl, zg, hs, xc_saved, r_saved, ig_saved, conv_w, w_rg, w_ig, lru_lambda, lru_proj):
    T = zl.shape[0]
    tt = ROW_TILE
    nt = T // tt
    n_groups = tt // SUBLANES

    def body(dy_ref, zl_ref, zlh_ref, zg_ref, hs_ref, hsh_ref, xc_ref, r_ref, ig_ref, cw_ref, wrg_ref, wig_ref,
             lam_ref, lp_ref, dzl_ref, dzg_ref, dlp_ref, dwrg_ref, dwig_ref, vec_ref,
             c_s, d_s, g_s, g_next, a_next, dxc_next):
        i = pl.program_id(0)
        ti = nt - 1 - i
        first_row = ti * tt

        @pl.when(i == 0)
        def _():
            dlp_ref[...] = jnp.zeros_like(dlp_ref)
            dwrg_ref[...] = jnp.zeros_like(dwrg_ref)
            dwig_ref[...] = jnp.zeros_like(dwig_ref)
            vec_ref[...] = jnp.zeros_like(vec_ref)
            g_next[...] = jnp.zeros_like(g_next)
            a_next[...] = jnp.zeros_like(a_next)
            dxc_next[...] = jnp.zeros_like(dxc_next)

        keep = (ti > 0).astype(F32)
        sp, dsp_dlam = _softplus_neg(lam_ref[...])
        hs = hs_ref[...]
        gelu, dgelu = _gelu_and_grad(zg_ref[...])
        dy = dy_ref[...]
        dlp_ref[...] += _tn((hs * gelu).astype(BF16), dy)
        dyl = _nt(dy, lp_ref[...])
        dzg_ref[...] = (dyl * hs * dgelu).astype(BF16)

        d_s[...] = dyl * gelu
        a_tile = jnp.exp(-LRU_C * r_ref[...].astype(F32) * sp)
        c_s[...] = _shift_up(jnp.concatenate([a_tile, a_next[...]], axis=0), 1)[:tt]
        a_next[...] = jnp.broadcast_to(a_tile[0:1, :], (SUBLANES, D_MODEL))
        rows8 = _row_ids((SUBLANES, D_MODEL))

        def group(k, carry):
            at = pl.ds(pl.multiple_of((n_groups - 1 - k) * SUBLANES, SUBLANES), SUBLANES)
            C, Dv = c_s[at, :], d_s[at, :]
            for s in (1, 2, 4):
                m = rows8 < SUBLANES - s
                Dv = jnp.where(m, C * pltpu.roll(Dv, SUBLANES - s, 0) + Dv, Dv)
                C = jnp.where(m, C * pltpu.roll(C, SUBLANES - s, 0), C)
            G = C * carry + Dv
            g_s[at, :] = G
            return jnp.broadcast_to(G[0:1, :], (SUBLANES, D_MODEL))

        g_next[...] = lax.fori_loop(0, n_groups, group, g_next[...])
        G = g_s[...]

        cw = cw_ref[...]
        zl_cat = jnp.concatenate([zlh_ref[...] * keep, zl_ref[...]], axis=0)
        xc, r, ig = xc_ref[...].astype(F32), r_ref[...].astype(F32), ig_ref[...].astype(F32)
        a, mult = _decay(r, sp, first_row)
        h_prev = _shift_down(jnp.concatenate([hsh_ref[...] * keep, hs_ref[...]], axis=0), 1)[SUBLANES:]
        t_glob = first_row + _row_ids((tt, D_MODEL))
        dmult = jnp.where(t_glob == 0, 0.0, G * ig * xc)
        dla = G * h_prev * a - dmult * (a * a) / mult
        vec_ref[_V_LAM:_V_LAM + 1, :] += _colsum(dla * r) * (-LRU_C) * dsp_dlam
        dpr = dla * (-LRU_C) * sp * r * (1.0 - r)
        dpi = G * mult * xc * ig * (1.0 - ig)
        vec_ref[_V_BRG:_V_BRG + 1, :] += _colsum(dpr)
        vec_ref[_V_BIG:_V_BIG + 1, :] += _colsum(dpi)
        dprb, dpib, xh = dpr.astype(BF16), dpi.astype(BF16), xc_ref[...]
        dxc_h = []
        for p in range(LRU_HEADS // 2):
            cols = slice(p * PAIR_DIM, (p + 1) * PAIR_DIM)
            dwrg_ref[p] += _tn(xh[:, cols], dprb[:, cols])
            dwig_ref[p] += _tn(xh[:, cols], dpib[:, cols])
            dxc_h.append(_nt(dprb[:, cols], wrg_ref[p]) + _nt(dpib[:, cols], wig_ref[p]))
        dxc = G * mult * ig + jnp.concatenate(dxc_h, axis=1)

        vec_ref[_V_CONVB:_V_CONVB + 1, :] += _colsum(dxc)
        dxc_cat = jnp.concatenate([dxc, dxc_next[...]], axis=0)
        dzl = cw[CONV_WIDTH - 1:CONV_WIDTH] * dxc
        for k in range(CONV_WIDTH):
            lag = CONV_WIDTH - 1 - k
            vec_ref[_V_CONVW + k:_V_CONVW + k + 1, :] += _colsum(dxc * _shift_down(zl_cat, lag)[SUBLANES:])
            if lag:
                dzl = dzl + cw[k:k + 1] * _shift_up(dxc_cat, lag)[:tt]
        dzl_ref[...] = dzl.astype(BF16)
        dxc_next[...] = dxc[:SUBLANES]

    res = [conv_w, w_rg, w_ig, lru_lambda, lru_proj]
    return pl.pallas_call(
        body, name="b2_lru", grid=(nt,),
        in_specs=[_rows_rev(tt, D_MODEL, nt), _rows_rev(tt, D_MODEL, nt), _halo_before_rev(SUBLANES, D_MODEL, tt, nt),
                  _rows_rev(tt, D_MODEL, nt), _rows_rev(tt, D_MODEL, nt), _halo_before_rev(SUBLANES, D_MODEL, tt, nt)]
        + [_rows_rev(tt, D_MODEL, nt)] * 3 + [_resident(w.shape) for w in res],
        out_specs=[_rows_rev(tt, D_MODEL, nt), _rows_rev(tt, D_MODEL, nt), _acc(lru_proj.shape), _acc(w_rg.shape),
                   _acc(w_ig.shape), _acc((SUBLANES, D_MODEL))],
        out_shape=[jax.ShapeDtypeStruct((T, D_MODEL), BF16), jax.ShapeDtypeStruct((T, D_MODEL), BF16),
                   jax.ShapeDtypeStruct(lru_proj.shape, F32), jax.ShapeDtypeStruct(w_rg.shape, F32),
                   jax.ShapeDtypeStruct(w_ig.shape, F32), jax.ShapeDtypeStruct((SUBLANES, D_MODEL), F32)],
        scratch_shapes=[pltpu.VMEM((tt, D_MODEL), F32)] * 3 + [pltpu.VMEM((SUBLANES, D_MODEL), F32)] * 3,
        compiler_params=_params(("arbitrary",)),
    )(dyl, zl, zl, zg, hs, hs, xc_saved, r_saved, ig_saved, *res)


WHOLE_BLOCK_BYTES = 3 * 512 * 1024
MAX_BLOCK_BYTES = 4 * 1024 * 1024


def _row_tile(rows, cols):
    for t in (rows, rows // 2):
        if t % 16 == 0 and rows % t == 0 and t * cols * 4 <= WHOLE_BLOCK_BYTES:
            return t
    for t in (512, 256, 128, 64, 32, 16):
        if rows % t == 0 and t * cols * 4 <= MAX_BLOCK_BYTES:
            return t
    return 8 if rows % 8 == 0 else rows


def _scalar_grid(grid, in_specs, out_specs):
    return pltpu.PrefetchScalarGridSpec(num_scalar_prefetch=1, grid=grid, in_specs=in_specs, out_specs=out_specs)


def _cast_into_block(w, by_rows, shard_j, name):
    R, C = w.shape
    tr = _row_tile(R, C)
    if by_rows:
        out_shape, out_map = (N_SHARDS * R, C), lambda i, j: (j[0] * (R // tr) + i, 0)
    else:
        out_shape, out_map = (R, N_SHARDS * C), lambda i, j: (i, j[0])

    def body(j_ref, w_ref, o_ref):
        o_ref[...] = w_ref[...].astype(BF16)

    return pl.pallas_call(
        body, name=name,
        grid_spec=_scalar_grid((R // tr,), [pl.BlockSpec((tr, C), lambda i, j: (i, 0))], pl.BlockSpec((tr, C), out_map)),
        out_shape=jax.ShapeDtypeStruct(out_shape, BF16),
        compiler_params=_params(("arbitrary",)),
    )(shard_j.reshape(1), w)


def _sum_cores(g, theirs, core, name):
    S, R, C = g.shape
    H = R // 2
    tr = _row_tile(H, C)
    nh = H // tr

    def body(c_ref, g_ref, t_ref, o_ref):
        o_ref[...] = (g_ref[...] + t_ref[...]).astype(BF16)

    half = pl.BlockSpec((None, tr, C), lambda s, i, c: (s, i, 0))
    return pl.pallas_call(
        body, name=name,
        grid_spec=_scalar_grid((S, nh), [pl.BlockSpec((None, tr, C), lambda s, i, c: (s, c[0] * nh + i, 0)), half], half),
        out_shape=jax.ShapeDtypeStruct((S, H, C), BF16),
        compiler_params=_params(("arbitrary", "arbitrary")),
    )(core.reshape(1), g, theirs)


def _sum_chips(sums, slots, by_rows, place, name):
    _, H, C = slots.shape
    tr = _row_tile(H, C)
    own_map = (lambda i, p: (p[0], i, 0)) if by_rows else (lambda i, p: (0, i, p[0]))

    def body(p_ref, s_ref, q_ref, o_ref):
        o_ref[...] = ((s_ref[...].astype(F32) + q_ref[0].astype(F32)) + q_ref[1].astype(F32)) + q_ref[2].astype(F32)

    return pl.pallas_call(
        body, name=name,
        grid_spec=_scalar_grid(
            (H // tr,),
            [pl.BlockSpec((None, tr, C), own_map), pl.BlockSpec((3, tr, C), lambda i, p: (0, i, 0))],
            pl.BlockSpec((None, tr, C), lambda i, p: (p[1], i, 0))),
        out_shape=jax.ShapeDtypeStruct((2, H, C), F32),
        compiler_params=_params(("arbitrary",)),
    )(place, sums, slots)


def _adamw(w, g, m, v, name):
    R, C = w.shape
    tr = _row_tile(R, C)
    c1 = 1.0 - ADAM_B1 ** ADAM_STEP
    c2 = 1.0 - ADAM_B2 ** ADAM_STEP

    def body(w_ref, g_ref, m_ref, v_ref, d_ref, nm_ref, nv_ref):
        gv = g_ref[...]
        nm = ADAM_B1 * m_ref[...] + (1.0 - ADAM_B1) * gv
        nv = ADAM_B2 * v_ref[...] + (1.0 - ADAM_B2) * (gv * gv)
        d_ref[...] = -ADAM_LR * ((nm / c1) / (jnp.sqrt(nv / c2) + ADAM_EPS) + ADAM_WD * w_ref[...])
        nm_ref[...] = nm
        nv_ref[...] = nv

    return pl.pallas_call(
        body, name=name, grid=(R // tr,),
        in_specs=[_rows(tr, C)] * 4, out_specs=[_rows(tr, C)] * 3,
        out_shape=[jax.ShapeDtypeStruct((R, C), F32)] * 3,
        compiler_params=_params(("arbitrary",)),
    )(w, g, m, v)


def _place():
    return lax.axis_index("x"), lax.axis_index("y"), lax.axis_index("c")


def _other_chips(x, y):
    return [(1 - x, y), (x, 1 - y), (1 - x, 1 - y)]


def _shard_block(ref, by_rows, R, C, j, half_rows=None):
    if half_rows is None:
        rows, r0 = R, 0
    else:
        rows = R // 2
        r0 = pl.multiple_of(half_rows * rows, 16)
    if by_rows:
        return ref.at[pl.ds(pl.multiple_of(j * R, 16) + r0, rows), :]
    return ref.at[pl.ds(r0, rows), pl.ds(pl.multiple_of(j * C, 128), C)]


def _all_gather_weights(gathered, shapes, by_rows, small):
    n = len(gathered)

    def body(*refs):
        small_in = refs[n]
        outs, small_out = refs[n + 1:2 * n + 1], refs[2 * n + 1]
        send_sems, recv_sems, local_sem = refs[2 * n + 2:]
        x, y, c = _place()
        me_j = 2 * x + y
        chips = _other_chips(x, y)
        sibling = (x, y, 1 - c)

        def block(i, j, half):
            R, C = shapes[i]
            return _shard_block(outs[i], by_rows[i], R, C, j, half)

        def ici(i, k, src_j):
            return pltpu.make_async_remote_copy(
                src_ref=block(i, src_j, c), dst_ref=block(i, src_j, c),
                send_sem=send_sems.at[6 * i + k], recv_sem=recv_sems.at[6 * i + k],
                device_id=(*chips[k], c), device_id_type=MESH)

        def relay(i, k, half):
            kj = 2 * chips[k][0] + chips[k][1]
            return pltpu.make_async_remote_copy(
                src_ref=block(i, kj, half), dst_ref=block(i, kj, half),
                send_sem=send_sems.at[6 * i + 3 + k], recv_sem=recv_sems.at[6 * i + 3 + k],
                device_id=sibling, device_id_type=MESH)

        def small_copy(k, src_j):
            cols = pl.ds(pl.multiple_of(src_j * 256, 128), 256)
            return pltpu.make_async_remote_copy(
                src_ref=small_in, dst_ref=small_out.at[:, cols],
                send_sem=send_sems.at[6 * n + k], recv_sem=recv_sems.at[6 * n + k],
                device_id=(*chips[k], c), device_id_type=MESH)

        sends = []
        for i in range(n):
            for k in range(3):
                cp = ici(i, k, me_j)
                cp.start()
                sends.append(cp)
        for k in range(3):
            cp = small_copy(k, me_j)
            cp.start()
            sends.append(cp)
        local = pltpu.make_async_copy(small_in, small_out.at[:, pl.ds(pl.multiple_of(me_j * 256, 128), 256)], local_sem)
        local.start()
        for i in range(n):
            for k in range(3):
                kj = 2 * chips[k][0] + chips[k][1]
                ici(i, k, kj).wait_recv()
                cp = relay(i, k, c)
                cp.start()
                sends.append(cp)
        for k in range(3):
            small_copy(k, 2 * chips[k][0] + chips[k][1]).wait_recv()
        for i in range(n):
            for k in range(3):
                relay(i, k, 1 - c).wait_recv()
        for cp in sends:
            cp.wait_send()
        local.wait()

    out_shape = [jax.ShapeDtypeStruct(g.shape, BF16) for g in gathered]
    out_shape.append(jax.ShapeDtypeStruct((8, N_SHARDS * 256), F32))
    n_sems = 6 * n + 3
    return pl.pallas_call(
        body, name="all_gather_weights",
        in_specs=[ANY] * (n + 1), out_specs=[ANY] * (n + 1), out_shape=out_shape,
        input_output_aliases={i: i for i in range(n)},
        scratch_shapes=[pltpu.SemaphoreType.DMA((n_sems,)), pltpu.SemaphoreType.DMA((n_sems,)),
                        pltpu.SemaphoreType.DMA],
    )(*gathered, small)


def _core_exchange(grads, name):
    n = len(grads)

    def body(*refs):
        copies = _core_exchange_copies(refs[:n], refs[n:2 * n], refs[2 * n], refs[2 * n + 1])
        for cp in copies:
            cp.start()
        for cp in copies:
            cp.wait()

    return pl.pallas_call(
        body, name=name,
        in_specs=[ANY] * n, out_specs=[ANY] * n,
        out_shape=[jax.ShapeDtypeStruct((g.shape[0], g.shape[1] // 2, g.shape[2]), F32) for g in grads],
        scratch_shapes=[pltpu.SemaphoreType.DMA((n,))] * 2,
    )(*grads)


HBM = pl.BlockSpec(memory_space=pltpu.HBM)
SEM = pl.BlockSpec(memory_space=pltpu.SEMAPHORE)
TOKEN = jax.ShapeDtypeStruct((SUBLANES, 128), F32)


def _in_hbm(a):
    return pltpu.with_memory_space_constraint(a, pltpu.HBM)


def _split_params():
    return pltpu.CompilerParams(has_side_effects=pltpu.SideEffectType.DATAFLOW_SIDE_EFFECTING)


def _gather_rest_copies(refs, shapes, by_rows, send_sems, recv_sems):
    x, y, c = _place()
    me_j = 2 * x + y
    chips = _other_chips(x, y)
    pairs = []
    for i, ref in enumerate(refs):
        R, C = shapes[i]
        for k in range(3):
            kj = 2 * chips[k][0] + chips[k][1]

            def copy(j, ref=ref, i=i, k=k, R=R, C=C):
                blk = _shard_block(ref, by_rows[i], R, C, j)
                return pltpu.make_async_remote_copy(
                    src_ref=blk, dst_ref=blk, send_sem=send_sems.at[3 * i + k], recv_sem=recv_sems.at[3 * i + k],
                    device_id=(*chips[k], c), device_id_type=MESH)

            pairs.append((copy(me_j), copy(kj)))
    return pairs


def _gather_rest_start(gathered, shapes, by_rows, after):
    n = len(gathered)

    def body(*refs):
        ins = refs[:n]
        send_sems, recv_sems = refs[n + 1], refs[n + 2]
        token = refs[-1]
        for mine, _ in _gather_rest_copies(ins, shapes, by_rows, send_sems, recv_sems):
            mine.start()
        token[...] = jnp.zeros_like(token)

    out = pl.pallas_call(
        body, name="gather_rest_start",
        out_shape=(pltpu.SemaphoreType.DMA((3 * n,)), pltpu.SemaphoreType.DMA((3 * n,)),
                   *[pltpu.HBM(g.shape, g.dtype) for g in gathered], TOKEN),
        in_specs=[HBM] * n + [ANY], out_specs=(SEM, SEM, *[HBM] * n, pl.BlockSpec(memory_space=pltpu.VMEM)),
        input_output_aliases={i: 2 + i for i in range(n)},
        compiler_params=_split_params(),
    )(*[_in_hbm(g) for g in gathered], after)
    return out[0], out[1], out[2:2 + n], out[-1]


def _gather_rest_wait(send_sems, recv_sems, gathered, shapes, by_rows, after):
    n = len(gathered)

    def body(*refs):
        ins = refs[:n]
        send, recv = refs[n], refs[n + 1]
        for mine, theirs in _gather_rest_copies(ins, shapes, by_rows, send, recv):
            mine.wait_send()
            theirs.wait_recv()

    return pl.pallas_call(
        body, name="gather_rest_wait",
        out_shape=tuple(pltpu.HBM(g.shape, g.dtype) for g in gathered),
        in_specs=[HBM] * n + [SEM, SEM, ANY], out_specs=tuple([HBM] * n),
        input_output_aliases={i: i for i in range(n)},
        compiler_params=_split_params(),
    )(*gathered, send_sems, recv_sems, after)


def _chip_exchange_copies(ins, slots, dims, by_rows, send_sems, recv_sems):
    x, y, c = _place()
    chips = _other_chips(x, y)
    pairs = []
    for i in range(len(ins)):
        for k in range(3):
            kj = 2 * chips[k][0] + chips[k][1]
            if by_rows[i]:
                src = ins[i].at[kj]
            else:
                src = ins[i].at[0, :, pl.ds(pl.multiple_of(kj * dims[i][1], 128), dims[i][1])]
            cp = pltpu.make_async_remote_copy(
                src_ref=src, dst_ref=slots[i].at[k], send_sem=send_sems.at[3 * i + k], recv_sem=recv_sems.at[3 * i + k],
                device_id=(*chips[k], c), device_id_type=MESH)
            pairs.append((cp, cp))
    return pairs


def _exchange_dims(sums, by_rows):
    return [(s.shape[1], s.shape[2]) if by_rows[i] else (s.shape[1], s.shape[2] // N_SHARDS) for i, s in enumerate(sums)]


def _chip_exchange_start(sums, by_rows, tag):
    n = len(sums)
    sums = list(sums)
    dims = _exchange_dims(sums, by_rows)
    slots = [lax.empty((3, h, cc), BF16) for h, cc in dims]

    def body(*refs):
        ins, land = refs[:n], refs[n:2 * n]
        send_sems, recv_sems = refs[2 * n], refs[2 * n + 1]
        token = refs[-1]
        for cp, _ in _chip_exchange_copies(ins, land, dims, by_rows, send_sems, recv_sems):
            cp.start()
        token[...] = jnp.zeros_like(token)

    out = pl.pallas_call(
        body, name="grad_chip_exchange_start_" + tag,
        out_shape=(pltpu.SemaphoreType.DMA((3 * n,)), pltpu.SemaphoreType.DMA((3 * n,)),
                   *[pltpu.HBM(a.shape, a.dtype) for a in sums + slots], TOKEN),
        in_specs=[HBM] * (2 * n), out_specs=(SEM, SEM, *[HBM] * (2 * n), pl.BlockSpec(memory_space=pltpu.VMEM)),
        input_output_aliases={i: 2 + i for i in range(2 * n)},
        compiler_params=_split_params(),
    )(*[_in_hbm(a) for a in sums + slots])
    return out[0], out[1], out[2:2 + n], out[2 + n:2 + 2 * n], out[-1]


def _chip_exchange_wait(send_sems, recv_sems, sums, slots, by_rows, after, tag):
    n = len(sums)
    sums, slots = list(sums), list(slots)
    dims = _exchange_dims(sums, by_rows)

    def body(*refs):
        ins, land = refs[:n], refs[n:2 * n]
        send, recv = refs[2 * n], refs[2 * n + 1]
        for cp, _ in _chip_exchange_copies(ins, land, dims, by_rows, send, recv):
            cp.wait_send()
            cp.wait_recv()

    out = pl.pallas_call(
        body, name="grad_chip_exchange_wait_" + tag,
        out_shape=tuple(pltpu.HBM(a.shape, a.dtype) for a in sums + slots),
        in_specs=[HBM] * (2 * n) + [SEM, SEM, ANY], out_specs=tuple([HBM] * (2 * n)),
        input_output_aliases={i: i for i in range(2 * n)},
        compiler_params=_split_params(),
    )(*sums, *slots, send_sems, recv_sems, after)
    return out[:n], out[n:]


def _core_exchange_copies(ins, theirs, send_sems, recv_sems):
    x, y, c = _place()
    copies = []
    for i in range(len(ins)):
        H = ins[i].shape[1] // 2
        copies.append(pltpu.make_async_remote_copy(
            src_ref=ins[i].at[:, pl.ds(pl.multiple_of((1 - c) * H, 8), H), :], dst_ref=theirs[i],
            send_sem=send_sems.at[i], recv_sem=recv_sems.at[i], device_id=(x, y, 1 - c), device_id_type=MESH))
    return copies


def _core_exchange_start(grads):
    n = len(grads)
    grads = list(grads)
    theirs = [lax.empty((g.shape[0], g.shape[1] // 2, g.shape[2]), F32) for g in grads]

    def body(*refs):
        for cp in _core_exchange_copies(refs[:n], refs[n:2 * n], refs[2 * n], refs[2 * n + 1]):
            cp.start()
        refs[-1][...] = jnp.zeros_like(refs[-1])

    out = pl.pallas_call(
        body, name="grad_core_exchange_start",
        out_shape=(pltpu.SemaphoreType.DMA((n,)), pltpu.SemaphoreType.DMA((n,)),
                   *[pltpu.HBM(a.shape, a.dtype) for a in grads + theirs], TOKEN),
        in_specs=[HBM] * (2 * n), out_specs=(SEM, SEM, *[HBM] * (2 * n), pl.BlockSpec(memory_space=pltpu.VMEM)),
        input_output_aliases={i: 2 + i for i in range(2 * n)},
        compiler_params=_split_params(),
    )(*[_in_hbm(a) for a in grads + theirs])
    return out[0], out[1], out[2:2 + n], out[2 + n:2 + 2 * n], out[-1]


def _core_exchange_wait(send_sems, recv_sems, grads, theirs, after):
    n = len(grads)
    grads, theirs = list(grads), list(theirs)

    def body(*refs):
        for cp in _core_exchange_copies(refs[:n], refs[n:2 * n], refs[2 * n], refs[2 * n + 1]):
            cp.wait_send()
            cp.wait_recv()

    out = pl.pallas_call(
        body, name="grad_core_exchange_wait",
        out_shape=tuple(pltpu.HBM(a.shape, a.dtype) for a in grads + theirs),
        in_specs=[HBM] * (2 * n) + [SEM, SEM, ANY], out_specs=tuple([HBM] * (2 * n)),
        input_output_aliases={i: i for i in range(2 * n)},
        compiler_params=_split_params(),
    )(*grads, *theirs, send_sems, recv_sems, after)
    return out[:n], out[n:]


def _core_share(reduced, tag):
    n = len(reduced)

    def body(*refs):
        outs = refs[n:2 * n]
        send_sems, recv_sems = refs[2 * n:]
        x, y, c = _place()
        copies = []
        for i in range(n):
            cp = pltpu.make_async_remote_copy(
                src_ref=outs[i].at[c], dst_ref=outs[i].at[c], send_sem=send_sems.at[i], recv_sem=recv_sems.at[i],
                device_id=(x, y, 1 - c), device_id_type=MESH)
            cp.start()
            copies.append(cp)
        for cp in copies:
            cp.wait()

    return pl.pallas_call(
        body, name="grad_core_share_" + tag,
        in_specs=[ANY] * n, out_specs=[ANY] * n,
        out_shape=[jax.ShapeDtypeStruct(r.shape, F32) for r in reduced],
        input_output_aliases={i: i for i in range(n)},
        scratch_shapes=[pltpu.SemaphoreType.DMA((n,))] * 2,
    )(*reduced)


def _small_exchange_copies(pack_ref, slots_ref, send_sems, recv_sems):
    x, y, c = _place()
    peers = [(px, py, pc) for px in (x, 1 - x) for py in (y, 1 - y) for pc in (c, 1 - c)][1:]
    pairs = []
    for k, peer in enumerate(peers):
        def copy(sender, k=k, peer=peer):
            return pltpu.make_async_remote_copy(
                src_ref=pack_ref, dst_ref=slots_ref.at[4 * sender[0] + 2 * sender[1] + sender[2]],
                send_sem=send_sems.at[k], recv_sem=recv_sems.at[k], device_id=peer, device_id_type=MESH)

        pairs.append((copy((x, y, c)), copy(peer)))
    return pairs


def _small_exchange_start(pack):
    slots = lax.empty((N_DEV,) + pack.shape, F32)

    def body(pack_ref, slots_ref, send_sems, recv_sems, pack_thru, slots_thru, token):
        for mine, _ in _small_exchange_copies(pack_ref, slots_ref, send_sems, recv_sems):
            mine.start()
        token[...] = jnp.zeros_like(token)

    return pl.pallas_call(
        body, name="grad_small_exchange_start",
        out_shape=(pltpu.SemaphoreType.DMA((N_DEV - 1,)), pltpu.SemaphoreType.DMA((N_DEV - 1,)),
                   pltpu.HBM(pack.shape, F32), pltpu.HBM(slots.shape, F32), TOKEN),
        in_specs=[HBM, HBM], out_specs=(SEM, SEM, HBM, HBM, pl.BlockSpec(memory_space=pltpu.VMEM)),
        input_output_aliases={0: 2, 1: 3},
        compiler_params=_split_params(),
    )(_in_hbm(pack), _in_hbm(slots))


def _small_exchange_wait(send_sems, recv_sems, pack, slots, after):
    def body(pack_ref, slots_ref, send, recv, after_ref, pack_thru, slots_thru):
        for mine, theirs in _small_exchange_copies(pack_ref, slots_ref, send, recv):
            mine.wait_send()
            theirs.wait_recv()

    return pl.pallas_call(
        body, name="grad_small_exchange_wait",
        out_shape=(pltpu.HBM(pack.shape, F32), pltpu.HBM(slots.shape, F32)),
        in_specs=[HBM, HBM, SEM, SEM, ANY], out_specs=(HBM, HBM),
        input_output_aliases={0: 0, 1: 1},
        compiler_params=_split_params(),
    )(pack, slots, send_sems, recv_sems, after)


def _sum_small(pack, slots, me):
    R, C = pack.shape
    tr = _row_tile(R, C)

    def body(me_ref, p_ref, q_ref, o_ref):
        acc = jnp.where(me_ref[0] == 0, p_ref[...], q_ref[0])
        for d in range(1, N_DEV):
            acc = acc + jnp.where(me_ref[0] == d, p_ref[...], q_ref[d])
        o_ref[...] = acc

    return pl.pallas_call(
        body, name="sum_small",
        grid_spec=_scalar_grid((R // tr,), [pl.BlockSpec((tr, C), lambda i, m: (i, 0)),
                                            pl.BlockSpec((N_DEV, tr, C), lambda i, m: (0, i, 0))],
                               pl.BlockSpec((tr, C), lambda i, m: (i, 0))),
        out_shape=jax.ShapeDtypeStruct((R, C), F32),
        compiler_params=_params(("arbitrary",)),
    )(me.reshape(1), pack, slots)


def _pack_rows(parts, rows):
    flat = jnp.concatenate([a.reshape(-1) for a in parts])
    return jnp.pad(flat, (0, rows * 128 - flat.shape[0])).reshape(rows, 128)


def _unpack_rows(pack, shapes):
    flat = pack.reshape(-1)
    out, at = [], 0
    for s in shapes:
        size = 1
        for d in s:
            size *= d
        out.append(flat[at:at + size].reshape(s))
        at += size
    return out


def kernel(x, p, norm1_g, w_in, b_gate, pool_w, pool_scale, pool_proj, conv_w, conv_b, w_rg, b_rg, w_ig, b_ig, lru_lambda, lru_proj, w_out, norm2_g, w_ffn_in, w_ffn_out, ple_norm_g, w_ple_gate, w_ple_proj, final_g, loss_target, m_norm1_g, m_w_in, m_b_gate, m_pool_w, m_pool_scale, m_pool_proj, m_conv_w, m_conv_b, m_w_rg, m_b_rg, m_w_ig, m_b_ig, m_lru_lambda, m_lru_proj, m_w_out, m_norm2_g, m_w_ffn_in, m_w_ffn_out, m_ple_norm_g, m_w_ple_gate, m_w_ple_proj, m_final_g, v_norm1_g, v_w_in, v_b_gate, v_pool_w, v_pool_scale, v_pool_proj, v_conv_w, v_conv_b, v_w_rg, v_b_rg, v_w_ig, v_b_ig, v_lru_lambda, v_lru_proj, v_w_out, v_norm2_g, v_w_ffn_in, v_w_ffn_out, v_ple_norm_g, v_w_ple_gate, v_w_ple_proj, v_final_g):
    weights = dict(norm1_g=norm1_g, w_in=w_in, b_gate=b_gate, pool_w=pool_w, pool_scale=pool_scale,
                   pool_proj=pool_proj, conv_w=conv_w, conv_b=conv_b, w_rg=w_rg, b_rg=b_rg, w_ig=w_ig, b_ig=b_ig,
                   lru_lambda=lru_lambda, lru_proj=lru_proj, w_out=w_out, norm2_g=norm2_g, w_ffn_in=w_ffn_in,
                   w_ffn_out=w_ffn_out, ple_norm_g=ple_norm_g, w_ple_gate=w_ple_gate, w_ple_proj=w_ple_proj,
                   final_g=final_g)
    m_in = dict(norm1_g=m_norm1_g, w_in=m_w_in, b_gate=m_b_gate, pool_w=m_pool_w, pool_scale=m_pool_scale,
                pool_proj=m_pool_proj, conv_w=m_conv_w, conv_b=m_conv_b, w_rg=m_w_rg, b_rg=m_b_rg, w_ig=m_w_ig,
                b_ig=m_b_ig, lru_lambda=m_lru_lambda, lru_proj=m_lru_proj, w_out=m_w_out, norm2_g=m_norm2_g,
                w_ffn_in=m_w_ffn_in, w_ffn_out=m_w_ffn_out, ple_norm_g=m_ple_norm_g, w_ple_gate=m_w_ple_gate,
                w_ple_proj=m_w_ple_proj, final_g=m_final_g)
    v_in = dict(norm1_g=v_norm1_g, w_in=v_w_in, b_gate=v_b_gate, pool_w=v_pool_w, pool_scale=v_pool_scale,
                pool_proj=v_pool_proj, conv_w=v_conv_w, conv_b=v_conv_b, w_rg=v_w_rg, b_rg=v_b_rg, w_ig=v_w_ig,
                b_ig=v_b_ig, lru_lambda=v_lru_lambda, lru_proj=v_lru_proj, w_out=v_w_out, norm2_g=v_norm2_g,
                w_ffn_in=v_w_ffn_in, w_ffn_out=v_w_ffn_out, ple_norm_g=v_ple_norm_g, w_ple_gate=v_w_ple_gate,
                w_ple_proj=v_w_ple_proj, final_g=v_final_g)
    names = list(weights)
    big = ["w_in", "pool_proj", "lru_proj", "w_out", "w_ffn_in", "w_ffn_out", "w_ple_gate", "w_ple_proj"]
    by_rows = [n in ("lru_proj", "w_out", "w_ffn_out", "w_ple_gate") for n in big]
    small = [n for n in names if n not in big]

    shard_j = 2 * lax.axis_index("x") + lax.axis_index("y")
    T = x.shape[1]
    xs, ps, tgt = x[0], p[0, 0], loss_target[0]

    small_local = jnp.concatenate([b_gate[0], conv_w[0], jnp.zeros((2, 256), F32)], axis=0)
    core = lax.axis_index("c").astype(jnp.int32)
    place = jnp.stack([shard_j, core]).astype(jnp.int32)
    rows_of = dict(zip(big, by_rows))
    shard_shape = {n: weights[n].shape[1:] for n in big}
    blocks = {n: _cast_into_block(weights[n][0], rows_of[n], place[0], "cast_" + n) for n in big}
    early, late = big[:4], big[4:]
    gathered = _all_gather_weights([blocks[n] for n in early], [shard_shape[n] for n in early],
                                   [rows_of[n] for n in early], small_local)
    full = dict(zip(early, gathered[:-1]))
    late_send, late_recv, late_bufs, late_token = _gather_rest_start(
        [blocks[n] for n in late], [shard_shape[n] for n in late], [rows_of[n] for n in late], gathered[-1])
    b_gate_full = gathered[-1][0:2].reshape(1, 2 * D_MODEL)
    conv_w_full = gathered[-1][2:6]
    pool_w_1, w_rg_1, w_ig_1 = [w[0].astype(BF16) for w in (pool_w, w_rg, w_ig)]
    pool_w_b, w_rg_b, w_ig_b = [_pair_blocks(w) for w in (pool_w_1, w_rg_1, w_ig_1)]
    b_rg_row, b_ig_row = b_rg.reshape(1, D_MODEL), b_ig.reshape(1, D_MODEL)
    final_row = final_g.reshape(1, D_MODEL)

    zp, zl, zg, zt, u, h1, hs, yp, yl, xc_saved, r_saved, ig_saved = _f12_mixer(
        xs, norm1_g + late_token[0, 0], full["w_in"], b_gate_full, pool_w_1, pool_scale, full["pool_proj"],
        conv_w_full, conv_b, w_rg_1, b_rg_row, w_ig_1, b_ig_row, lru_lambda, full["lru_proj"], full["w_out"])
    full.update(zip(late, _gather_rest_wait(late_send, late_recv, late_bufs, [shard_shape[n] for n in late],
                                            [rows_of[n] for n in late], h1)))
    h2, v, ff, act = _f3_ffn(h1, norm2_g, full["w_ffn_in"], full["w_ffn_out"])

    loss_sum, dh2, g_ple_gate, g_ple_proj, vec4 = _b4_ple_loss(
        h2, ps, tgt, ple_norm_g, full["w_ple_gate"], full["w_ple_proj"], final_row)
    dff, dh1, vec3 = _b3_ffn(dh2, h1, ff, norm2_g, full["w_ffn_in"], full["w_ffn_out"])
    g_ffn_in = _wgrad(v, dff, 2 * D_FF // N_SHARDS, "wgrad_ffn_in")
    g_ffn_out = _wgrad(act, dh2, D_MODEL, "wgrad_ffn_out", tokens=WGRAD_TOKENS // 2)

    def stack(n, g):
        return g.reshape(N_SHARDS, g.shape[0] // N_SHARDS, g.shape[1]) if rows_of[n] else g[None]

    def chip_sums_of(group, grads_of, tag):
        stacked = [stack(n, grads_of[n]) for n in group]
        theirs = _core_exchange(stacked, "grad_core_exchange_" + tag)
        return [_sum_cores(g, t, core, "sum_cores_" + n) for g, t, n in zip(stacked, theirs, group)]

    late_rows = [rows_of[n] for n in late]
    late_grads = dict(w_ffn_in=g_ffn_in, w_ffn_out=g_ffn_out, w_ple_gate=g_ple_gate, w_ple_proj=g_ple_proj)
    cx_send, cx_recv, late_stacked, late_theirs, cx_token = _core_exchange_start(
        [stack(n, late_grads[n]) for n in late])
    dzt, dyp, dyl, g_w_out, vec_g = _b2_gates(dh1, zt, yp, yl, b_gate_full + cx_token[0, 0], full["w_out"])
    late_stacked, late_theirs = _core_exchange_wait(cx_send, cx_recv, late_stacked, late_theirs, dzt)
    late_sums = [_sum_cores(g, t, core, "sum_cores_" + n) for g, t, n in zip(late_stacked, late_theirs, late)]
    ex_send, ex_recv, late_sums, late_slots, ex_token = _chip_exchange_start(late_sums, late_rows, "late")
    dzl, dzg, g_lru_proj, g_w_rg, g_w_ig, vec_l = _b2_lru(
        dyl, zl, zg, hs, xc_saved, r_saved, ig_saved, conv_w_full, w_rg_b, w_ig_b, lru_lambda + ex_token[0, 0],
        full["lru_proj"])
    dzp, grad_x, g_pool_proj, g_pool_w, vec_p = _b12_pool_in_proj(
        dyp, zp, dzl, dzg, dzt, xs, dh1, norm1_g, full["w_in"], pool_w_b, pool_scale, full["pool_proj"])
    small_full = dict(
        norm1_g=vec_p[1], b_gate=vec_g[0:2], pool_w=_unpair_blocks(g_pool_w), pool_scale=vec_p[0, :POOL_WIDTH],
        conv_w=vec_l[_V_CONVW:_V_CONVW + CONV_WIDTH], conv_b=vec_l[_V_CONVB], w_rg=_unpair_blocks(g_w_rg),
        b_rg=vec_l[_V_BRG], w_ig=_unpair_blocks(g_w_ig), b_ig=vec_l[_V_BIG], lru_lambda=vec_l[_V_LAM], norm2_g=vec3[0], ple_norm_g=vec4[1],
        final_g=vec4[0])
    full_shapes = [small_full[n].shape for n in small]
    n_full = sum(int(small_full[n].size) for n in small)
    rows_full = -(-n_full // (128 * ROW_TILE)) * ROW_TILE
    sm_send, sm_recv, sm_pack, sm_slots, sm_token = _small_exchange_start(
        _pack_rows([small_full[n] for n in small], rows_full))
    g_w_in = _wgrad_parts(u, [dzp, dzl, dzg, dzt], POOL_WIDTH, "wgrad_in", sm_token)

    loss = lax.psum(loss_sum[0, 0] * (0.5 / D_MODEL), ("x", "y", "c"))

    early_rows = [rows_of[n] for n in early]
    early_sums = chip_sums_of(early, dict(w_in=g_w_in, pool_proj=g_pool_proj, lru_proj=g_lru_proj, w_out=g_w_out),
                              "early")
    e_send, e_recv, early_sums, early_slots, e_token = _chip_exchange_start(early_sums, early_rows, "early")
    grads, deltas, new_m, new_v = {}, {}, {}, {}

    def finish(group, sums, slots, tag):
        reduced = _core_share([_sum_chips(s, q, rows_of[n], place, "sum_chips_" + n)
                               for s, q, n in zip(sums, slots, group)], tag)
        for n, r in zip(group, reduced):
            g = r.reshape(r.shape[0] * r.shape[1], r.shape[2])
            d, nm, nv = _adamw(weights[n][0], g, m_in[n][0], v_in[n][0], "adamw_" + n)
            grads[n], deltas[n], new_m[n], new_v[n] = g[None], d[None], nm[None], nv[None]

    late_sums, late_slots = _chip_exchange_wait(ex_send, ex_recv, late_sums, late_slots, late_rows, e_token, "late")
    finish(late, late_sums, late_slots, "late")

    sm_pack, sm_slots = _small_exchange_wait(sm_send, sm_recv, sm_pack, sm_slots, e_token)
    device = (4 * lax.axis_index("x") + 2 * lax.axis_index("y") + lax.axis_index("c")).astype(jnp.int32)
    summed = dict(zip(small, _unpack_rows(_sum_small(sm_pack, sm_slots, device), full_shapes)))
    summed["b_gate"] = lax.dynamic_slice_in_dim(summed["b_gate"], shard_j * 256, 256, axis=1)
    summed["conv_w"] = lax.dynamic_slice_in_dim(summed["conv_w"], shard_j * 256, 256, axis=1)
    local_shapes = [weights[n].shape for n in small]
    n_local = sum(int(weights[n].size) for n in small)
    rows_local = -(-n_local // (128 * ROW_TILE)) * ROW_TILE
    packs = [_pack_rows([src[n] for n in small], rows_local) for src in (weights, summed, m_in, v_in)]
    d_s, nm_s, nv_s = _adamw(*packs, "adamw_small")
    for dst, pack in ((grads, packs[1]), (deltas, d_s), (new_m, nm_s), (new_v, nv_s)):
        dst.update(zip(small, _unpack_rows(pack, local_shapes)))

    done = d_s[:SUBLANES]
    for n in late:
        done = done + deltas[n][0, :SUBLANES, :128]
    early_sums, early_slots = _chip_exchange_wait(e_send, e_recv, early_sums, early_slots, early_rows, done, "early")
    finish(early, early_sums, early_slots, "early")

    return (loss, grad_x[None], *[grads[n] for n in names], *[deltas[n] for n in names],
            *[new_m[n] for n in names], *[new_v[n] for n in names])
```
